```python
import jax
import jax.numpy as jnp
from jax import lax
import numpy as np

D_MODEL = 2048
BATCH = 4
SEQ = 2048
DEPTH = 4
DEC_BATCH = 8
DEC_SEQ = 1
PAST_LEN = 16384
PAGE_SIZE = 128

N_EVEN = (DEPTH + 1) // 2
N_ODD = DEPTH // 2
N_HEADS = 16
HEAD_DIM = 96
KV_HEADS = 4
HPG = N_HEADS // KV_HEADS
ATT_W = N_HEADS * HEAD_DIM
KVW = 2 * KV_HEADS * HEAD_DIM
CMP_BLOCK = 32
SEL_BLOCK = 64
SEL_RATIO = SEL_BLOCK // CMP_BLOCK
TOP_K = 16
WINDOW = 512
SEL_QBLK = 32
WIN_QBLK = 128
ROPE_THETA = 10000.0
SCALE = HEAD_DIM ** -0.5
FORCE = 1e9
NEG = -1e30
CONV_CH = D_MODEL // 4
CONV_W = 31
POOL_CH = D_MODEL // 4
POOL_WINDOWS = (2, 4, 8, 16)
POOL_GROUPS = len(POOL_WINDOWS)
POOL_GC = POOL_CH // POOL_GROUPS
POOL_STATE = max(POOL_WINDOWS) - 1
SGU_CH = D_MODEL - POOL_CH
SGU_GROUPS = 4
SGU_GC = SGU_CH // SGU_GROUPS
CHUNK = 128
D_FF = 4 * D_MODEL
ALPHA = (2 * DEPTH) ** 0.25
BETA = (8 * DEPTH) ** -0.25
LN_EPS = 1e-5
EVEN_IN = ATT_W + 3 * KVW + 3 * N_HEADS + 2 * CONV_CH
EVEN_SPLITS = [ATT_W, ATT_W + KVW, ATT_W + 2 * KVW, ATT_W + 3 * KVW, ATT_W + 3 * KVW + 3 * N_HEADS]
EVEN_CAT = ATT_W + CONV_CH
ODD_IN = POOL_CH + 2 * SGU_CH
ODD_CAT = POOL_CH + SGU_CH

kernel_name = 'nsa_conformer_pool_sgu_deepnorm_step'


def layer_norm(x, g, b):
    xf = x.astype(jnp.float32)
    mu = jnp.mean(xf, axis=-1, keepdims=True)
    var = jnp.mean(jnp.square(xf - mu), axis=-1, keepdims=True)
    y = (xf - mu) * lax.rsqrt(var + LN_EPS)
    return (y * g.astype(jnp.float32) + b.astype(jnp.float32)).astype(x.dtype)


def rope(x, pos):
    half = HEAD_DIM // 2
    inv = jnp.power(ROPE_THETA, -jnp.arange(half, dtype=jnp.float32) / half)
    ang = pos.astype(jnp.float32)[:, None] * inv[None, :]
    cos = jnp.cos(ang)[None, :, None, :]
    sin = jnp.sin(ang)[None, :, None, :]
    xf = x.astype(jnp.float32)
    x1, x2 = xf[..., :half], xf[..., half:]
    return jnp.concatenate([x1 * cos - x2 * sin, x2 * cos + x1 * sin], axis=-1).astype(x.dtype)


def masked_softmax(s, mask):
    s = jnp.where(mask, s.astype(jnp.float32), NEG)
    m = jnp.max(s, axis=-1, keepdims=True)
    p = jnp.where(mask, jnp.exp(s - m), 0.0)
    return p / jnp.maximum(jnp.sum(p, axis=-1, keepdims=True), 1e-30)


def compress(rows, pe, w):
    B, L = rows.shape[:2]
    nb = L // CMP_BLOCK
    blk = rows[:, :nb * CMP_BLOCK].reshape(B, nb, CMP_BLOCK, KV_HEADS, HEAD_DIM)
    summ = jnp.mean((blk * pe[None, None, :, None, :]).astype(jnp.float32), axis=2).astype(rows.dtype)
    return jnp.einsum('bngd,de->bnge', summ, w)


def gather_blocks(blocks, ix):
    return jax.vmap(jax.vmap(lambda b_, i_: b_[i_]))(blocks, ix)


def sel_attend(qr, qpos, idx, kb, vb):
    B, Sq = qr.shape[:2]
    k = idx.shape[-1]
    ix = idx.transpose(0, 2, 1, 3)
    kg = gather_blocks(kb, ix)
    vg = gather_blocks(vb, ix)
    qg = qr.reshape(B, Sq, KV_HEADS, HPG, HEAD_DIM)
    s = jnp.einsum('bqghd,bgqkld->bqghkl', qg, kg) * SCALE
    kpos = idx[..., None] * SEL_BLOCK + jnp.arange(SEL_BLOCK)
    mask = (kpos <= qpos[None, :, None, None, None]).reshape(B, Sq, KV_HEADS, 1, k * SEL_BLOCK)
    p = masked_softmax(s.reshape(B, Sq, KV_HEADS, HPG, k * SEL_BLOCK), mask)
    p = p.reshape(B, Sq, KV_HEADS, HPG, k, SEL_BLOCK).astype(vg.dtype)
    o = jnp.einsum('bqghkl,bgqkld->bqghd', p, vg)
    return o.reshape(B, Sq, N_HEADS, HEAD_DIM)


def nsa_cmp_sel(q, q_rot, qpos, kvc, kvs, pe_k, pe_v, w_ck, w_cv):
    B, Sq = q.shape[:2]
    L = kvs.shape[1]
    ck = compress(kvc[:, :, 0], pe_k, w_ck)
    cv = compress(kvc[:, :, 1], pe_v, w_cv)
    ncb = ck.shape[1]
    qg = q.reshape(B, Sq, KV_HEADS, HPG, HEAD_DIM)
    s = jnp.einsum('bqghd,bngd->bqghn', qg, ck) * SCALE
    blk_end = (jnp.arange(ncb) + 1) * CMP_BLOCK - 1
    cmask = blk_end[None, :] <= qpos[:, None]
    p = masked_softmax(s, cmask[None, :, None, None, :])
    o_cmp = jnp.einsum('bqghn,bngd->bqghd', p.astype(cv.dtype), cv).reshape(B, Sq, N_HEADS, HEAD_DIM)
    nsb = -(-L // SEL_BLOCK)
    imp = jnp.sum(p, axis=3)
    imp = jnp.pad(imp, ((0, 0), (0, 0), (0, 0), (0, nsb * SEL_RATIO - ncb)))
    imp = imp.reshape(B, Sq, KV_HEADS, nsb, SEL_RATIO).sum(-1)
    sb = jnp.arange(nsb)
    visible = (sb[None, :] * SEL_BLOCK <= qpos[:, None])[None, :, None, :]
    current = (sb[None, :] == (qpos // SEL_BLOCK)[:, None])[None, :, None, :]
    imp = jnp.where(current, FORCE, jnp.where(visible, imp, -FORCE))
    k = min(TOP_K, nsb)
    _, idx = lax.top_k(imp, k)
    pad = nsb * SEL_BLOCK - L

    def to_blocks(r):
        r = jnp.pad(r, ((0, 0), (0, pad), (0, 0), (0, 0)))
        return r.reshape(B, nsb, SEL_BLOCK, KV_HEADS, HEAD_DIM).transpose(0, 3, 1, 2, 4)

    kb = to_blocks(kvs[:, :, 0])
    vb = to_blocks(kvs[:, :, 1])
    if Sq > SEL_QBLK and Sq % SEL_QBLK == 0:
        nq = Sq // SEL_QBLK
        qs = q_rot.reshape(B, nq, SEL_QBLK, N_HEADS, HEAD_DIM).swapaxes(0, 1)
        ps = qpos.reshape(nq, SEL_QBLK)
        ix = idx.reshape(B, nq, SEL_QBLK, KV_HEADS, k).swapaxes(0, 1)
        o = lax.map(lambda a: sel_attend(a[0], a[1], a[2], kb, vb), (qs, ps, ix))
        o_sel = o.swapaxes(0, 1).reshape(B, Sq, N_HEADS, HEAD_DIM)
    else:
        o_sel = sel_attend(q_rot, qpos, idx, kb, vb)
    return o_cmp, o_sel


def win_attend_banded(qr, kr, vr):
    B, S = qr.shape[:2]
    nb = S // WIN_QBLK
    npad = WINDOW // WIN_QBLK

    def bands(r):
        rp = jnp.pad(r, ((0, 0), (npad * WIN_QBLK, 0), (0, 0), (0, 0)))
        rp = rp.reshape(B, nb + npad, WIN_QBLK, KV_HEADS, HEAD_DIM)
        return jnp.concatenate([rp[:, j:j + nb] for j in range(npad + 1)], axis=2)

    kband = bands(kr)
    vband = bands(vr)
    qb = qr.reshape(B, nb, WIN_QBLK, KV_HEADS, HPG, HEAD_DIM)
    s = jnp.einsum('bnqghd,bnkgd->bnqghk', qb, kband) * SCALE
    qpos = jnp.arange(S).reshape(nb, WIN_QBLK)
    kpos = jnp.arange(nb)[:, None] * WIN_QBLK - npad * WIN_QBLK + jnp.arange((npad + 1) * WIN_QBLK)[None, :]
    kp = kpos[:, None, :]
    qp = qpos[:, :, None]
    mask = (kp <= qp) & (kp >= qp - WINDOW) & (kp >= 0)
    p = masked_softmax(s, mask[None, :, :, None, None, :])
    o = jnp.einsum('bnqghk,bnkgd->bnqghd', p.astype(vr.dtype), vband)
    return o.reshape(B, S, N_HEADS, HEAD_DIM)


def win_attend_cached(qr, qpos, kr, vr, kpos):
    B, Sq = qr.shape[:2]
    qg = qr.reshape(B, Sq, KV_HEADS, HPG, HEAD_DIM)
    s = jnp.einsum('bqghd,bkgd->bqghk', qg, kr) * SCALE
    mask = (kpos[None, :] <= qpos[:, None]) & (kpos[None, :] >= qpos[:, None] - WINDOW)
    p = masked_softmax(s, mask[None, :, None, None, :])
    o = jnp.einsum('bqghk,bkgd->bqghd', p.astype(vr.dtype), vr)
    return o.reshape(B, Sq, N_HEADS, HEAD_DIM)


def even_project(x, w_in, pos):
    B, S = x.shape[:2]
    h = x @ w_in
    q, kvc, kvs, kvw, gates, glu = jnp.split(h, EVEN_SPLITS, axis=-1)
    q = q.reshape(B, S, N_HEADS, HEAD_DIM)
    q_rot = rope(q, pos)
    kvc = kvc.reshape(B, S, 2, KV_HEADS, HEAD_DIM)
    kvs = kvs.reshape(B, S, 2, KV_HEADS, HEAD_DIM)
    kvw = kvw.reshape(B, S, 2, KV_HEADS, HEAD_DIM)
    kvs = jnp.stack([rope(kvs[:, :, 0], pos), kvs[:, :, 1]], axis=2)
    kvw = jnp.stack([rope(kvw[:, :, 0], pos), kvw[:, :, 1]], axis=2)
    gates = jax.nn.sigmoid(gates.reshape(B, S, 3, N_HEADS))
    a, g = jnp.split(glu, 2, axis=-1)
    u = a * jax.nn.sigmoid(g)
    return q, q_rot, kvc, kvs, kvw, gates, u


def conv_module(u_ext, conv_w, conv_b, ln_g, ln_b):
    y = lax.conv_general_dilated(u_ext, conv_w[:, None, :], window_strides=(1,), padding='VALID',
                                 dimension_numbers=('NWC', 'WIO', 'NWC'),
                                 feature_group_count=CONV_CH) + conv_b
    return jax.nn.silu(layer_norm(y, ln_g, ln_b))


def even_combine(o_cmp, o_sel, o_win, gates, conv_out, w_out):
    B, S = o_cmp.shape[:2]
    o = gates[:, :, 0, :, None] * o_cmp + gates[:, :, 1, :, None] * o_sel + gates[:, :, 2, :, None] * o_win
    cat = jnp.concatenate([o.reshape(B, S, ATT_W), conv_out], axis=-1)
    return cat @ w_out


def even_mixer_prompt(x, pos, w_in, w_out, pe_k, pe_v, w_ck, w_cv, cw, cb, cg, cbn):
    B, S = x.shape[:2]
    q, q_rot, kvc, kvs, kvw, gates, u = even_project(x, w_in, pos)
    o_cmp, o_sel = nsa_cmp_sel(q, q_rot, pos, kvc, kvs, pe_k, pe_v, w_ck, w_cv)
    o_win = win_attend_banded(q_rot, kvw[:, :, 0], kvw[:, :, 1])
    u_ext = jnp.pad(u, ((0, 0), (CONV_W - 1, 0), (0, 0)))
    c = conv_module(u_ext, cw, cb, cg, cbn)
    out = even_combine(o_cmp, o_sel, o_win, gates, c, w_out)
    wb = min(WINDOW, S)
    return out, kvc, kvs, kvw[:, S - wb:], u_ext[:, u_ext.shape[1] - (CONV_W - 1):]


def even_mixer_sample(x, pos, cmp_pages, sel_pages, win_buf, conv_buf, page_table,
                      w_in, w_out, pe_k, pe_v, w_ck, w_cv, cw, cb, cg, cbn):
    Bd, Sd = x.shape[:2]
    past = page_table.shape[1] * PAGE_SIZE
    q, q_rot, kvc, kvs, kvw, gates, u = even_project(x, w_in, pos)

    def history(pages, new):
        h = pages[page_table].reshape(Bd, past, 2, KV_HEADS, HEAD_DIM)
        return jnp.concatenate([h, new], axis=1)

    o_cmp, o_sel = nsa_cmp_sel(q, q_rot, pos, history(cmp_pages, kvc), history(sel_pages, kvs),
                               pe_k, pe_v, w_ck, w_cv)
    wb = win_buf.shape[1]
    wkv = jnp.concatenate([win_buf, kvw], axis=1)
    kpos = past - wb + jnp.arange(wb + Sd)
    o_win = win_attend_cached(q_rot, pos, wkv[:, :, 0], wkv[:, :, 1], kpos)
    u_ext = jnp.concatenate([conv_buf, u], axis=1)
    c = conv_module(u_ext, cw, cb, cg, cbn)
    out = even_combine(o_cmp, o_sel, o_win, gates, c, w_out)
    wb_new = min(WINDOW, wb + Sd)
    return out, kvc, kvs, wkv[:, wkv.shape[1] - wb_new:], u_ext[:, u_ext.shape[1] - (CONV_W - 1):]


def pool_mix(p_ext, start_pos, pool_w, pool_scale):
    B, T, C = p_ext.shape
    S = T - POOL_STATE
    cs = jnp.concatenate([jnp.zeros((B, 1, C), jnp.float32), jnp.cumsum(p_ext.astype(jnp.float32), axis=1)], axis=1)
    pos = start_pos + jnp.arange(S)
    outs = []
    for g, w in enumerate(POOL_WINDOWS):
        sl = slice(g * POOL_GC, (g + 1) * POOL_GC)
        hi = cs[:, POOL_STATE + 1:POOL_STATE + 1 + S, sl]
        lo = cs[:, POOL_STATE + 1 - w:POOL_STATE + 1 - w + S, sl]
        cnt = jnp.minimum(w, pos + 1).astype(jnp.float32)[None, :, None]
        outs.append((hi - lo) / cnt)
    pooled = jnp.concatenate(outs, axis=-1).astype(p_ext.dtype)
    d = (pooled - p_ext[:, POOL_STATE:]).reshape(B, S, POOL_GROUPS, POOL_GC)
    y = jnp.einsum('bsgc,gce->bsge', d, pool_w).reshape(B, S, POOL_CH)
    return y * pool_scale


def sgu(u, v, sgu_w, sgu_b, ln_g, ln_b):
    B, S = u.shape[:2]
    vn = layer_norm(v.reshape(B, S, SGU_GROUPS, SGU_GC), ln_g.reshape(SGU_GROUPS, SGU_GC),
                    ln_b.reshape(SGU_GROUPS, SGU_GC))
    nc = -(-S // CHUNK)
    vp = jnp.pad(vn, ((0, 0), (0, nc * CHUNK - S), (0, 0), (0, 0))).reshape(B, nc, CHUNK, SGU_GROUPS, SGU_GC)
    tri = jnp.tril(jnp.ones((CHUNK, CHUNK), dtype=bool))
    ws = jnp.where(tri[None], sgu_w, jnp.zeros_like(sgu_w))
    mixed = jnp.einsum('gij,bnjgc->bnigc', ws, vp) + sgu_b.T[None, None, :, :, None]
    mixed = mixed.reshape(B, nc * CHUNK, SGU_CH)[:, :S]
    return u * mixed, vn.reshape(B, S, SGU_CH)


def odd_mixer(x, pool_prev, start_pos, w_in, w_out, pool_w, pool_scale, ln_g, ln_b, sgu_w, sgu_b):
    h = x @ w_in
    pin, uv = jnp.split(h, [POOL_CH], axis=-1)
    u, v = jnp.split(jax.nn.gelu(uv), 2, axis=-1)
    p_ext = jnp.concatenate([pool_prev, pin], axis=1)
    y_pool = pool_mix(p_ext, start_pos, pool_w, pool_scale)
    y_sgu, vn = sgu(u, v, sgu_w, sgu_b, ln_g, ln_b)
    out = jnp.concatenate([y_pool, y_sgu], axis=-1) @ w_out
    return out, p_ext[:, p_ext.shape[1] - POOL_STATE:], vn


def sq_relu_mlp(x, w1, w2):
    return jnp.square(jax.nn.relu(x @ w1)) @ w2


def setup_inputs(seed: int = 0) -> dict:
    key = jax.random.key(seed)
    ks = iter(jax.random.split(key, 48))
    f32 = jnp.float32

    def nrm(shape, scale):
        return jax.random.normal(next(ks), shape, f32) * scale

    n_pages = PAST_LEN // PAGE_SIZE
    n_used = DEC_BATCH * n_pages
    n_pool = n_used + n_used // 4
    wb = min(WINDOW, PAST_LEN)
    x_prompt = nrm((BATCH, SEQ, D_MODEL), 1.0)
    x_sample = nrm((DEC_BATCH, DEC_SEQ, D_MODEL), 1.0)
    cache_cmp_kv = nrm((N_EVEN, n_pool, PAGE_SIZE, 2, KV_HEADS, HEAD_DIM), 1.0)
    cache_sel_kv = nrm((N_EVEN, n_pool, PAGE_SIZE, 2, KV_HEADS, HEAD_DIM), 1.0)
    cache_win_kv = nrm((N_EVEN, DEC_BATCH, wb, 2, KV_HEADS, HEAD_DIM), 1.0)
    state_conv = nrm((N_EVEN, DEC_BATCH, CONV_W - 1, CONV_CH), 0.5)
    state_pool = nrm((N_ODD, DEC_BATCH, POOL_STATE, POOL_CH), 1.0)
    page_table = jax.random.permutation(next(ks), n_pool)[:n_used].reshape(DEC_BATCH, n_pages).astype(jnp.int32)
    return {
        'x_prompt': x_prompt,
        'x_sample': x_sample,
        'cache_cmp_kv': cache_cmp_kv,
        'cache_sel_kv': cache_sel_kv,
        'cache_win_kv': cache_win_kv,
        'state_conv': state_conv,
        'state_pool': state_pool,
        'page_table': page_table,
        'w_in_even': nrm((N_EVEN, D_MODEL, EVEN_IN), D_MODEL ** -0.5),
        'w_out_even': nrm((N_EVEN, EVEN_CAT, D_MODEL), BETA * EVEN_CAT ** -0.5),
        'cmp_pe_k': 1.0 + nrm((N_EVEN, CMP_BLOCK, HEAD_DIM), 0.1),
        'cmp_pe_v': 1.0 + nrm((N_EVEN, CMP_BLOCK, HEAD_DIM), 0.1),
        'cmp_w_k': nrm((N_EVEN, HEAD_DIM, HEAD_DIM), HEAD_DIM ** -0.5),
        'cmp_w_v': nrm((N_EVEN, HEAD_DIM, HEAD_DIM), HEAD_DIM ** -0.5),
        'conv_w': nrm((N_EVEN, CONV_W, CONV_CH), CONV_W ** -0.5),
        'conv_b': nrm((N_EVEN, CONV_CH), 0.02),
        'conv_ln_g': 1.0 + nrm((N_EVEN, CONV_CH), 0.1),
        'conv_ln_b': nrm((N_EVEN, CONV_CH), 0.02),
        'w_in_odd': nrm((N_ODD, D_MODEL, ODD_IN), D_MODEL ** -0.5),
        'w_out_odd': nrm((N_ODD, ODD_CAT, D_MODEL), BETA * ODD_CAT ** -0.5),
        'pool_w': nrm((N_ODD, POOL_GROUPS, POOL_GC, POOL_GC), POOL_GC ** -0.5),
        'pool_scale': 1.0 + nrm((N_ODD, POOL_CH), 0.1),
        'sgu_ln_g': 1.0 + nrm((N_ODD, SGU_CH), 0.1),
        'sgu_ln_b': nrm((N_ODD, SGU_CH), 0.02),
        'sgu_w': nrm((N_ODD, SGU_GROUPS, CHUNK, CHUNK), CHUNK ** -0.5),
        'sgu_b': 1.0 + nrm((N_ODD, SGU_GROUPS, CHUNK), 0.1),
        'mlp_w1': nrm((DEPTH, D_MODEL, D_FF), D_MODEL ** -0.5),
        'mlp_w2': nrm((DEPTH, D_FF, D_MODEL), BETA * D_FF ** -0.5),
        'ln_mix_g': 1.0 + nrm((DEPTH, D_MODEL), 0.1),
        'ln_mix_b': nrm((DEPTH, D_MODEL), 0.02),
        'ln_ffn_g': 1.0 + nrm((DEPTH, D_MODEL), 0.1),
        'ln_ffn_b': nrm((DEPTH, D_MODEL), 0.02),
    }


def reference(x_prompt, x_sample, cache_cmp_kv, cache_sel_kv, cache_win_kv, state_conv, state_pool, page_table,
              w_in_even, w_out_even, cmp_pe_k, cmp_pe_v, cmp_w_k, cmp_w_v, conv_w, conv_b, conv_ln_g, conv_ln_b,
              w_in_odd, w_out_odd, pool_w, pool_scale, sgu_ln_g, sgu_ln_b, sgu_w, sgu_b,
              mlp_w1, mlp_w2, ln_mix_g, ln_mix_b, ln_ffn_g, ln_ffn_b):
    B, S = x_prompt.shape[:2]
    Bd, Sd = x_sample.shape[:2]
    past = page_table.shape[1] * PAGE_SIZE
    pos_p = jnp.arange(S, dtype=jnp.int32)
    pos_s = past + jnp.arange(Sd, dtype=jnp.int32)
    xp, xs = x_prompt, x_sample
    cmp_p, cmp_s, sel_p, sel_s, win_p, win_s = [], [], [], [], [], []
    conv_p, conv_s, pool_p, pool_s, sgu_p, sgu_s = [], [], [], [], [], []
    for layer in range(DEPTH):
        if layer % 2 == 0:
            e = layer // 2
            ew = (w_in_even[e], w_out_even[e], cmp_pe_k[e], cmp_pe_v[e], cmp_w_k[e], cmp_w_v[e],
                  conv_w[e], conv_b[e], conv_ln_g[e], conv_ln_b[e])
            mp, c1, s1, w1_, k1 = even_mixer_prompt(xp, pos_p, *ew)
            ms, c2, s2, w2_, k2 = even_mixer_sample(xs, pos_s, cache_cmp_kv[e], cache_sel_kv[e],
                                                    cache_win_kv[e], state_conv[e], page_table, *ew)
            cmp_p.append(c1); cmp_s.append(c2)
            sel_p.append(s1); sel_s.append(s2)
            win_p.append(w1_); win_s.append(w2_)
            conv_p.append(k1); conv_s.append(k2)
        else:
            o = layer // 2
            ow = (w_in_odd[o], w_out_odd[o], pool_w[o], pool_scale[o], sgu_ln_g[o], sgu_ln_b[o], sgu_w[o], sgu_b[o])
            mp, p1, v1 = odd_mixer(xp, jnp.zeros((B, POOL_STATE, POOL_CH), xp.dtype), 0, *ow)
            ms, p2, v2 = odd_mixer(xs, state_pool[o], past, *ow)
            pool_p.append(p1); pool_s.append(p2)
            sgu_p.append(v1[:, ((S - 1) // CHUNK) * CHUNK:]); sgu_s.append(v2)
        xp = layer_norm(ALPHA * xp + mp, ln_mix_g[layer], ln_mix_b[layer])
        xs = layer_norm(ALPHA * xs + ms, ln_mix_g[layer], ln_mix_b[layer])
        xp = layer_norm(ALPHA * xp + sq_relu_mlp(xp, mlp_w1[layer], mlp_w2[layer]), ln_ffn_g[layer], ln_ffn_b[layer])
        xs = layer_norm(ALPHA * xs + sq_relu_mlp(xs, mlp_w1[layer], mlp_w2[layer]), ln_ffn_g[layer], ln_ffn_b[layer])
    return (xp, xs,
            jnp.stack(cmp_p), jnp.stack(cmp_s),
            jnp.stack(sel_p), jnp.stack(sel_s),
            jnp.stack(win_p), jnp.stack(win_s),
            jnp.stack(conv_p), jnp.stack(conv_s),
            jnp.stack(pool_p), jnp.stack(pool_s),
            jnp.stack(sgu_p), jnp.stack(sgu_s))
```

```python
import functools

import jax
import jax.numpy as jnp
from jax import lax
from jax.experimental import pallas as pl
from jax.experimental.pallas import tpu as pltpu

F32 = jnp.float32
BF16 = jnp.bfloat16

D_MODEL = 2048
DEPTH = 4
PAGE_SIZE = 128
N_HEADS = 16
HEAD_DIM = 96
KV_HEADS = 4
HPG = N_HEADS // KV_HEADS
ATT_W = N_HEADS * HEAD_DIM
KVW = 2 * KV_HEADS * HEAD_DIM
CMP_BLOCK = 32
SEL_BLOCK = 64
TOP_K = 16
WINDOW = 512
ROPE_THETA = 10000.0
SCALE = HEAD_DIM ** -0.5
FORCE = 1e9
NEG = -1e30
CONV_CH = D_MODEL // 4
CONV_W = 31
POOL_CH = D_MODEL // 4
POOL_WINDOWS = (2, 4, 8, 16)
POOL_GROUPS = len(POOL_WINDOWS)
POOL_GC = POOL_CH // POOL_GROUPS
POOL_STATE = max(POOL_WINDOWS) - 1
SGU_CH = D_MODEL - POOL_CH
SGU_GROUPS = 4
SGU_GC = SGU_CH // SGU_GROUPS
CHUNK = 128
D_FF = 4 * D_MODEL
ALPHA = (2 * DEPTH) ** 0.25
LN_EPS = 1e-5

LANES = 128
HALF = HEAD_DIM // 2
HSLOT = LANES
HALF_OFF = LANES // 2
QW = N_HEADS * HSLOT
KVP = 2 * KV_HEADS * HSLOT
GATE_W = KV_HEADS * LANES
SAMPLE_ROWS = 16
VMEM_LIMIT = 52 * 1024 * 1024

E_Q, E_KVC, E_KVS, E_KVW = 0, QW, QW + KVP, QW + 2 * KVP
E_GLU = QW + 3 * KVP
E_GATE = E_GLU + 2 * CONV_CH
E_TOT = E_GATE + GATE_W
O_V, O_PIN, O_U = 0, SGU_CH, SGU_CH + POOL_CH


def _cparams(*sem):
    return pltpu.CompilerParams(dimension_semantics=sem, vmem_limit_bytes=VMEM_LIMIT)


def _pad_head(x):
    z = jnp.zeros(x.shape[:-1] + (HALF_OFF - HALF,), x.dtype)
    return jnp.concatenate([x[..., :HALF], z, x[..., HALF:], z], axis=-1)


def _unpad_heads(x, nheads):
    xh = x.reshape(x.shape[:-1] + (nheads, HSLOT))
    y = jnp.concatenate([xh[..., :HALF], xh[..., HALF_OFF:HALF_OFF + HALF]], axis=-1)
    return y.reshape(x.shape[:-1] + (nheads * HEAD_DIM,))


def _rope_tables(pos):
    inv = jnp.power(ROPE_THETA, -jnp.arange(HALF, dtype=F32) / HALF)
    ang = pos.astype(F32)[:, None] * inv[None, :]
    cos, sin = jnp.cos(ang), jnp.sin(ang)
    z = jnp.zeros((pos.shape[0], HALF_OFF - HALF), F32)
    return (jnp.concatenate([cos, z, cos, z], axis=1),
            jnp.concatenate([-sin, z, sin, z], axis=1))


def _prep_even_weights(w_in, w_out, pe_k, pe_v, w_ck, w_cv):
    d = w_in.shape[0]
    q = _pad_head(w_in[:, :ATT_W].reshape(d, N_HEADS, HEAD_DIM)).reshape(d, QW)
    kvs = []
    for s in range(3):
        blk = w_in[:, ATT_W + s * KVW:ATT_W + (s + 1) * KVW].reshape(d, 2 * KV_HEADS, HEAD_DIM)
        kvs.append(_pad_head(blk).reshape(d, KVP))
    g0 = ATT_W + 3 * KVW
    gates = w_in[:, g0:g0 + 3 * N_HEADS].reshape(d, 3, KV_HEADS, HPG).transpose(0, 2, 1, 3)
    gates = gates.reshape(d, KV_HEADS, 3 * HPG)
    gates = jnp.pad(gates, ((0, 0), (0, 0), (0, LANES - 3 * HPG))).reshape(d, GATE_W)
    glu = w_in[:, g0 + 3 * N_HEADS:]
    a = glu[:, :CONV_CH].reshape(d, 2, CONV_CH // 2)
    g = glu[:, CONV_CH:].reshape(d, 2, CONV_CH // 2)
    glu = jnp.stack([a, g], axis=2).reshape(d, 2 * CONV_CH)
    w_in_p = jnp.concatenate([q] + kvs + [glu, gates], axis=1).astype(BF16)
    wo_att = _pad_head(w_out[:ATT_W].reshape(N_HEADS, HEAD_DIM, d).transpose(0, 2, 1))
    wo_att = wo_att.transpose(0, 2, 1).reshape(QW, d).astype(BF16)
    wo_conv = w_out[ATT_W:].astype(BF16)
    pe = jnp.concatenate([jnp.tile(_pad_head(pe_k), (1, KV_HEADS)), jnp.tile(_pad_head(pe_v), (1, KV_HEADS))], axis=1)
    pe_c = jnp.concatenate([jnp.tile(pe_k, (1, KV_HEADS)), jnp.tile(pe_v, (1, KV_HEADS))], axis=1)
    wk_cols = _pad_head(w_ck)
    wv_cols = _pad_head(w_cv)
    eye = jnp.eye(KV_HEADS, dtype=F32)
    big_c = jnp.concatenate([
        jnp.concatenate([jnp.kron(eye, wk_cols), jnp.zeros((KV_HEADS * HEAD_DIM, KV_HEADS * HSLOT), F32)], axis=1),
        jnp.concatenate([jnp.zeros((KV_HEADS * HEAD_DIM, KV_HEADS * HSLOT), F32), jnp.kron(eye, wv_cols)], axis=1)],
        axis=0)
    wk_full = _pad_head(wk_cols.T).T
    wv_full = _pad_head(wv_cols.T).T
    zp = jnp.zeros((KV_HEADS * HSLOT, KV_HEADS * HSLOT), F32)
    big_p = jnp.concatenate([jnp.concatenate([jnp.kron(eye, wk_full), zp], axis=1),
                             jnp.concatenate([zp, jnp.kron(eye, wv_full)], axis=1)], axis=0)
    return dict(w_in=w_in_p, wo_att=wo_att, wo_conv=wo_conv, pe=pe, pe_c=pe_c,
                big_c=big_c.astype(BF16), big_p=big_p.astype(BF16))


def _head_select_matrix():
    return _pad_head(jnp.eye(2 * KV_HEADS * HEAD_DIM, dtype=F32).reshape(KVW, 2 * KV_HEADS, HEAD_DIM)).reshape(KVW, KVP)


def _ln_rows(y, g, b):
    mu = jnp.mean(y, axis=-1, keepdims=True)
    yc = y - mu
    var = jnp.mean(yc * yc, axis=-1, keepdims=True)
    return yc * lax.rsqrt(var + LN_EPS) * g + b


def _rope_slot(x, cos, sin):
    return x * cos + pltpu.roll(x, HALF_OFF, axis=1) * sin


_NT = (((1,), (1,)), ((), ()))


def _mm(x, w, *, tm, tn, n_off, n_cols, epilogue, extras=(), outs, name):
    m, k = x.shape
    assert m % tm == 0 and n_cols % tn == 0 and n_off % tn == 0
    joff = n_off // tn
    in_specs = [pl.BlockSpec((tm, k), lambda i, j: (i, 0)),
                pl.BlockSpec((k, tn), lambda i, j: (0, joff + j))]
    in_specs += [pl.BlockSpec(bs, im) for _, bs, im in extras]
    out_shape = [jax.ShapeDtypeStruct((m, c), dt) for c, dt, _ in outs]
    out_specs = [pl.BlockSpec((tm, bc), lambda i, j: (i, j)) for _, _, bc in outs]
    ne = len(extras)

    def body(x_ref, w_ref, *refs):
        acc = jnp.dot(x_ref[...], w_ref[...], preferred_element_type=F32)
        epilogue(acc, refs[:ne], refs[ne:])

    return pl.pallas_call(
        body, grid=(m // tm, n_cols // tn), in_specs=in_specs, out_specs=out_specs, out_shape=out_shape,
        compiler_params=_cparams("parallel", "arbitrary"), name=name,
    )(x, w, *[a for a, _, _ in extras])


def _ep_q(acc, ex, outs):
    cos, sin = ex[0][...], ex[1][...]
    outs[0][...] = acc.astype(BF16)
    for j in range(acc.shape[1] // HSLOT):
        sl = slice(j * HSLOT, (j + 1) * HSLOT)
        outs[1][:, sl] = _rope_slot(acc[:, sl], cos, sin).astype(BF16)


def _ep_kv(acc, ex, outs, *, rope):
    if rope:
        cos, sin = ex[0][...], ex[1][...]
        for j in range(KV_HEADS):
            sl = slice(j * HSLOT, (j + 1) * HSLOT)
            r = _rope_slot(acc[:, sl], cos, sin)
            outs[0][:, sl] = r
            outs[1][:, sl] = r.astype(BF16)
        vs = slice(KV_HEADS * HSLOT, KVP)
        outs[0][:, vs] = acc[:, vs]
        outs[1][:, vs] = acc[:, vs].astype(BF16)
    else:
        outs[0][...] = acc


def _ep_glu(acc, ex, outs):
    h = acc.shape[1] // 2
    outs[0][...] = acc[:, :h] * jax.nn.sigmoid(acc[:, h:])


def _ep_sigmoid(acc, ex, outs):
    outs[0][...] = jax.nn.sigmoid(acc)


def _ep_plain(acc, ex, outs):
    outs[0][...] = acc.astype(outs[0].dtype)


def _ep_relu2(acc, ex, outs):
    r = jnp.maximum(acc, 0.0)
    outs[0][...] = (r * r).astype(outs[0].dtype)


def _ep_gelu(acc, ex, outs):
    outs[0][...] = jax.nn.gelu(acc)


def _ep_gelu_gln(acc, ex, outs):
    g, b = ex[0][...], ex[1][...]
    v = jax.nn.gelu(acc)
    for j in range(acc.shape[1] // SGU_GC):
        sl = slice(j * SGU_GC, (j + 1) * SGU_GC)
        outs[0][:, sl] = _ln_rows(v[:, sl], g[:, sl], b[:, sl])


def _proj_ln(a_list, w_list, resid, g, b, *, tm, name):
    m, n = resid.shape
    npair = len(a_list)
    in_specs = []
    for a in a_list:
        in_specs.append(pl.BlockSpec((tm, a.shape[1]), lambda i: (i, 0)))
    for w in w_list:
        in_specs.append(pl.BlockSpec(w.shape, lambda i: (0, 0), pipeline_mode=pl.Buffered(1)))
    in_specs += [pl.BlockSpec((tm, n), lambda i: (i, 0)),
                 pl.BlockSpec((1, n), lambda i: (0, 0)), pl.BlockSpec((1, n), lambda i: (0, 0))]

    def body(*refs):
        a_refs, w_refs = refs[:npair], refs[npair:2 * npair]
        r_ref, g_ref, b_ref, o_ref, ob_ref = refs[2 * npair:]
        acc = ALPHA * r_ref[...]
        for a_ref, w_ref in zip(a_refs, w_refs):
            acc = acc + jnp.dot(a_ref[...], w_ref[...], preferred_element_type=F32)
        y = _ln_rows(acc, g_ref[...], b_ref[...])
        o_ref[...] = y
        ob_ref[...] = y.astype(BF16)

    return pl.pallas_call(
        body, grid=(m // tm,), in_specs=in_specs,
        out_specs=[pl.BlockSpec((tm, n), lambda i: (i, 0)), pl.BlockSpec((tm, n), lambda i: (i, 0))],
        out_shape=[jax.ShapeDtypeStruct((m, n), F32), jax.ShapeDtypeStruct((m, n), BF16)],
        compiler_params=_cparams("parallel"), name=name,
    )(*a_list, *w_list, resid, g.reshape(1, n), b.reshape(1, n))


def _mlp2_ln(h, w2, resid, g, b, *, tm, tk, name):
    m, kf = h.shape
    n = w2.shape[1]
    nk = kf // tk

    def body(h_ref, w_ref, r_ref, g_ref, b_ref, o_ref, ob_ref, acc_ref):
        k = pl.program_id(1)

        @pl.when(k == 0)
        def _():
            acc_ref[...] = ALPHA * r_ref[...]

        acc_ref[...] += jnp.dot(h_ref[...], w_ref[...], preferred_element_type=F32)

        @pl.when(k == nk - 1)
        def _():
            y = _ln_rows(acc_ref[...], g_ref[...], b_ref[...])
            o_ref[...] = y
            ob_ref[...] = y.astype(BF16)

    return pl.pallas_call(
        body, grid=(m // tm, nk),
        in_specs=[pl.BlockSpec((tm, tk), lambda i, k: (i, k)), pl.BlockSpec((tk, n), lambda i, k: (k, 0)),
                  pl.BlockSpec((tm, n), lambda i, k: (i, 0)),
                  pl.BlockSpec((1, n), lambda i, k: (0, 0)), pl.BlockSpec((1, n), lambda i, k: (0, 0))],
        out_specs=[pl.BlockSpec((tm, n), lambda i, k: (i, 0)), pl.BlockSpec((tm, n), lambda i, k: (i, 0))],
        out_shape=[jax.ShapeDtypeStruct((m, n), F32), jax.ShapeDtypeStruct((m, n), BF16)],
        scratch_shapes=[pltpu.VMEM((tm, n), F32)],
        compiler_params=_cparams("parallel", "arbitrary"), name=name,
    )(h, w2, resid, g.reshape(1, n), b.reshape(1, n))


def _compress_rows(kvc, pe, *, rows, name):
    m, c = kvc.shape
    nb = rows // CMP_BLOCK

    def body(x_ref, pe_ref, o_ref):
        x = x_ref[...].reshape(nb, CMP_BLOCK, c) * pe_ref[...][None]
        o_ref[...] = jnp.sum(x, axis=1) * (1.0 / CMP_BLOCK)

    return pl.pallas_call(
        body, grid=(m // rows,),
        in_specs=[pl.BlockSpec((rows, c), lambda i: (i, 0)), pl.BlockSpec((CMP_BLOCK, c), lambda i: (0, 0))],
        out_specs=pl.BlockSpec((nb, c), lambda i: (i, 0)),
        out_shape=jax.ShapeDtypeStruct((m // CMP_BLOCK, c), F32),
        compiler_params=_cparams("parallel"), name=name,
    )(kvc, pe)


def _compress_pages(cache, page_table, pe_c, layer, *, pages_per_step, name):
    n_layers, n_pool = cache.shape[:2]
    bd, n_pages = page_table.shape
    pages = cache.reshape(n_layers * n_pool, PAGE_SIZE, KVW)
    pps = pages_per_step
    bpp = PAGE_SIZE // CMP_BLOCK
    base = layer * n_pool

    def body(pt_ref, *refs):
        page_refs, pe_ref, o_ref = refs[:pps], refs[pps], refs[pps + 1]
        for p in range(pps):
            x = page_refs[p][...].reshape(bpp, CMP_BLOCK, KVW) * pe_ref[...][None]
            o_ref[p * bpp:(p + 1) * bpp, :] = jnp.sum(x, axis=1) * (1.0 / CMP_BLOCK)

    def page_map(p):
        return lambda b, j, pt: (base + pt[b, j * pps + p], 0, 0)

    grid_spec = pltpu.PrefetchScalarGridSpec(
        num_scalar_prefetch=1, grid=(bd, n_pages // pps),
        in_specs=[pl.BlockSpec((None, PAGE_SIZE, KVW), page_map(p)) for p in range(pps)]
        + [pl.BlockSpec((CMP_BLOCK, KVW), lambda b, j, pt: (0, 0))],
        out_specs=pl.BlockSpec((pps * bpp, KVW), lambda b, j, pt: (b * (n_pages // pps) + j, 0)))
    return pl.pallas_call(
        body, grid_spec=grid_spec,
        out_shape=jax.ShapeDtypeStruct((bd * n_pages * bpp, KVW), F32),
        compiler_params=_cparams("parallel", "arbitrary"), name=name,
    )(page_table, *([pages] * pps), pe_c)


def _flash_step(q, k, v, mask, m_ref, l_ref, acc_ref):
    s = lax.dot_general(q, k, _NT, preferred_element_type=F32) * SCALE
    s = jnp.where(mask, s, NEG)
    m_prev = m_ref[...]
    m_new = jnp.maximum(m_prev, jnp.max(s, axis=-1, keepdims=True))
    alpha = jnp.exp(m_prev - m_new)
    p = jnp.where(mask, jnp.exp(s - m_new), 0.0)
    l_ref[...] = alpha * l_ref[...] + jnp.sum(p, axis=-1, keepdims=True)
    acc_ref[...] = alpha * acc_ref[...] + jnp.dot(p.astype(BF16), v, preferred_element_type=F32)
    m_ref[...] = m_new


def _attn_body(qraw_ref, qrot_ref, gate_ref, ck_ref, cv_ref, ks_ref, vs_ref, kw_ref, vw_ref, o_ref,
               m_ref, l_ref, acc_ref, selm_ref, *, tq, seq, k_top):
    i = pl.program_id(2)
    rows = HPG * tq
    ncb = seq // CMP_BLOCK
    nsb = seq // SEL_BLOCK
    q0 = i * tq
    row = lax.broadcasted_iota(jnp.int32, (rows, 1), 0)
    qpos = q0 + (row & (tq - 1))
    qp1 = q0 + lax.broadcasted_iota(jnp.int32, (tq, 1), 0)

    qr = jnp.concatenate([qraw_ref[:, j * HSLOT:(j + 1) * HSLOT] for j in range(HPG)], axis=0)
    n = lax.broadcasted_iota(jnp.int32, (1, nsb), 1)
    parts = []
    for par in range(2):
        ck = ck_ref[pl.ds(par, nsb, stride=2), :].astype(BF16)
        s = lax.dot_general(qr, ck, _NT, preferred_element_type=F32) * SCALE
        mk = ((2 * n + par + 1) * CMP_BLOCK - 1) <= qpos
        parts.append((jnp.where(mk, s, NEG), mk))
    mx = jnp.maximum(jnp.max(parts[0][0], axis=-1, keepdims=True), jnp.max(parts[1][0], axis=-1, keepdims=True))
    pe = [jnp.where(mk, jnp.exp(s - mx), 0.0) for s, mk in parts]
    den = jnp.maximum(jnp.sum(pe[0], axis=-1, keepdims=True) + jnp.sum(pe[1], axis=-1, keepdims=True), 1e-30)
    pn = [p / den for p in pe]
    o_cmp = jnp.zeros((rows, HSLOT), F32)
    for par in range(2):
        cv = cv_ref[pl.ds(par, nsb, stride=2), :].astype(BF16)
        o_cmp = o_cmp + jnp.dot(pn[par].astype(BF16), cv, preferred_element_type=F32)
    pp = pn[0] + pn[1]
    imp = pp[0:tq]
    for j in range(1, HPG):
        imp = imp + pp[j * tq:(j + 1) * tq]

    vis = (n * SEL_BLOCK) <= qp1
    cur = n == (qp1 >> (SEL_BLOCK.bit_length() - 1))
    imp = jnp.where(cur, FORCE, jnp.where(vis, imp, -FORCE))
    cnt = jnp.zeros((tq, nsb), F32)
    for j in range(nsb):
        col = imp[:, j:j + 1]
        beats = (col > imp) | ((col == imp) & (j < n))
        cnt = cnt + jnp.where(beats, 1.0, 0.0)
    sel = jnp.where(cnt < k_top, 1.0, 0.0)
    lane = lax.broadcasted_iota(jnp.int32, (1, LANES), 1)
    for t in range(seq // LANES):
        selm_ref[t] = jnp.where(lane < SEL_BLOCK, sel[:, 2 * t:2 * t + 1], sel[:, 2 * t + 1:2 * t + 2])

    qt = jnp.concatenate([qrot_ref[:, j * HSLOT:(j + 1) * HSLOT] for j in range(HPG)], axis=0)

    def reset():
        m_ref[...] = jnp.full((rows, 1), NEG, F32)
        l_ref[...] = jnp.zeros((rows, 1), F32)
        acc_ref[...] = jnp.zeros((rows, HSLOT), F32)

    def result():
        return acc_ref[...] / jnp.maximum(l_ref[...], 1e-30)

    reset()

    def sel_step(t, c):
        k0 = pl.multiple_of(t * LANES, LANES)
        kp = k0 + lane
        sm = selm_ref[t]
        sm = jnp.concatenate([sm] * HPG, axis=0) > 0.5
        mask = (kp <= qpos) & sm
        _flash_step(qt, ks_ref[pl.ds(k0, LANES), :], vs_ref[pl.ds(k0, LANES), :], mask, m_ref, l_ref, acc_ref)
        return c

    lax.fori_loop(0, i + 1, sel_step, 0)
    o_sel = result()

    reset()

    def win_step(t, c):
        k0 = pl.multiple_of(t * LANES, LANES)
        kp = k0 + lane
        mask = (kp <= qpos) & (kp >= qpos - WINDOW)
        _flash_step(qt, kw_ref[pl.ds(k0, LANES), :], vw_ref[pl.ds(k0, LANES), :], mask, m_ref, l_ref, acc_ref)
        return c

    lax.fori_loop(jnp.maximum(i - WINDOW // LANES, 0), i + 1, win_step, 0)
    o_win = result()

    gts = gate_ref[...]
    outs = []
    for br, o_br in enumerate((o_cmp, o_sel, o_win)):
        gcol = jnp.concatenate([gts[:, br * HPG + j:br * HPG + j + 1] for j in range(HPG)], axis=0)
        outs.append(gcol * o_br)
    o = outs[0] + outs[1] + outs[2]
    for j in range(HPG):
        o_ref[:, j * HSLOT:(j + 1) * HSLOT] = o[j * tq:(j + 1) * tq].astype(BF16)


def _prompt_attention(qraw, qrot, gates, ckv, kvs_b, kvw_b, *, batch, seq, tq):
    nq = seq // tq
    ncb = seq // CMP_BLOCK
    rows = HPG * tq
    k_top = min(TOP_K, seq // SEL_BLOCK)
    body = functools.partial(_attn_body, tq=tq, seq=seq, k_top=k_top)
    gw = HPG * HSLOT
    kmap = lambda b, g, i: (b, g)
    vmap_ = lambda b, g, i: (b, KV_HEADS + g)
    return pl.pallas_call(
        body, grid=(batch, KV_HEADS, nq),
        in_specs=[pl.BlockSpec((tq, gw), lambda b, g, i: (b * nq + i, g)),
                  pl.BlockSpec((tq, gw), lambda b, g, i: (b * nq + i, g)),
                  pl.BlockSpec((tq, LANES), lambda b, g, i: (b * nq + i, g)),
                  pl.BlockSpec((ncb, HSLOT), kmap), pl.BlockSpec((ncb, HSLOT), vmap_),
                  pl.BlockSpec((seq, HSLOT), kmap), pl.BlockSpec((seq, HSLOT), vmap_),
                  pl.BlockSpec((seq, HSLOT), kmap), pl.BlockSpec((seq, HSLOT), vmap_)],
        out_specs=pl.BlockSpec((tq, gw), lambda b, g, i: (b * nq + i, g)),
        out_shape=jax.ShapeDtypeStruct((batch * seq, QW), BF16),
        scratch_shapes=[pltpu.VMEM((rows, 1), F32), pltpu.VMEM((rows, 1), F32), pltpu.VMEM((rows, HSLOT), F32),
                        pltpu.VMEM((seq // LANES, tq, LANES), F32)],
        compiler_params=_cparams("parallel", "parallel", "arbitrary"), name="prompt_attention",
    )(qraw, qrot, gates, ckv, ckv, kvs_b, kvs_b, kvw_b, kvw_b)


CONV_HALO = 32


def _conv_body(cur_ref, prev_ref, w_ref, b_ref, g_ref, bn_ref, o_ref, ext_ref, *, ts):
    c = pl.program_id(1)
    ext_ref[0:CONV_HALO, :] = jnp.where(c > 0, prev_ref[...], 0.0)
    ext_ref[CONV_HALO:CONV_HALO + ts, :] = cur_ref[...]
    acc = jnp.zeros((ts, CONV_CH), F32) + b_ref[...]
    off = CONV_HALO - (CONV_W - 1)
    for k in range(CONV_W):
        acc = acc + ext_ref[pl.ds(off + k, ts), :] * w_ref[k:k + 1, :]
    y = _ln_rows(acc, g_ref[...], bn_ref[...])
    o_ref[...] = (y * jax.nn.sigmoid(y)).astype(BF16)


def _prompt_conv(u, cw, cb, cg, cbn, *, batch, seq, ts):
    nt = seq // ts
    r = ts // CONV_HALO
    cwp = jnp.pad(cw, ((0, CONV_HALO - CONV_W), (0, 0)))
    vec = lambda a: a.reshape(1, CONV_CH)
    cst = lambda b, c: (0, 0)
    return pl.pallas_call(
        functools.partial(_conv_body, ts=ts), grid=(batch, nt),
        in_specs=[pl.BlockSpec((ts, CONV_CH), lambda b, c: (b * nt + c, 0)),
                  pl.BlockSpec((CONV_HALO, CONV_CH), lambda b, c: (jnp.maximum((b * nt + c) * r - 1, 0), 0)),
                  pl.BlockSpec((CONV_HALO, CONV_CH), cst),
                  pl.BlockSpec((1, CONV_CH), cst), pl.BlockSpec((1, CONV_CH), cst), pl.BlockSpec((1, CONV_CH), cst)],
        out_specs=pl.BlockSpec((ts, CONV_CH), lambda b, c: (b * nt + c, 0)),
        out_shape=jax.ShapeDtypeStruct((batch * seq, CONV_CH), BF16),
        scratch_shapes=[pltpu.VMEM((CONV_HALO + ts, CONV_CH), F32)],
        compiler_params=_cparams("parallel", "arbitrary"), name="prompt_conv",
    )(u, u, cwp, vec(cb), vec(cg), vec(cbn))


POOL_HALO = 16


def _odd_mix_body(pin_ref, prev_ref, u_ref, vn_ref, pw_ref, ps_ref, sw_ref, sb_ref, o_ref, ext_ref):
    c = pl.program_id(1)
    ext_ref[0:POOL_HALO, :] = jnp.where(c > 0, prev_ref[...], 0.0)
    ext_ref[POOL_HALO:POOL_HALO + CHUNK, :] = pin_ref[...]
    t = c * CHUNK + lax.broadcasted_iota(jnp.int32, (CHUNK, 1), 0)
    for g, w in enumerate(POOL_WINDOWS):
        sl = slice(g * POOL_GC, (g + 1) * POOL_GC)
        tot = ext_ref[pl.ds(POOL_HALO, CHUNK), sl]
        for j in range(1, w):
            tot = tot + ext_ref[pl.ds(POOL_HALO - j, CHUNK), sl]
        cnt = jnp.minimum(w, t + 1).astype(F32)
        d = tot / cnt - pin_ref[:, sl]
        y = jnp.dot(d.astype(BF16), pw_ref[g], preferred_element_type=F32)
        o_ref[:, sl] = (y * ps_ref[:, sl]).astype(BF16)
    ri = lax.broadcasted_iota(jnp.int32, (CHUNK, CHUNK), 0)
    ci = lax.broadcasted_iota(jnp.int32, (CHUNK, CHUNK), 1)
    for g in range(SGU_GROUPS):
        sl = slice(g * SGU_GC, (g + 1) * SGU_GC)
        ws = jnp.where(ci <= ri, sw_ref[g], 0.0).astype(BF16)
        mixed = jnp.dot(ws, vn_ref[:, sl].astype(BF16), preferred_element_type=F32) + sb_ref[:, g:g + 1]
        o_ref[:, POOL_CH + g * SGU_GC:POOL_CH + (g + 1) * SGU_GC] = (u_ref[:, sl] * mixed).astype(BF16)


def _prompt_odd_mix(pin, u, vn, pool_w, pool_scale, sgu_w, sgu_b, *, batch, seq):
    nt = seq // CHUNK
    r = CHUNK // POOL_HALO
    cst2 = lambda b, c: (0, 0)
    cst3 = lambda b, c: (0, 0, 0)
    row = lambda b, c: (b * nt + c, 0)
    return pl.pallas_call(
        _odd_mix_body, grid=(batch, nt),
        in_specs=[pl.BlockSpec((CHUNK, POOL_CH), row),
                  pl.BlockSpec((POOL_HALO, POOL_CH), lambda b, c: (jnp.maximum((b * nt + c) * r - 1, 0), 0)),
                  pl.BlockSpec((CHUNK, SGU_CH), row), pl.BlockSpec((CHUNK, SGU_CH), row),
                  pl.BlockSpec((POOL_GROUPS, POOL_GC, POOL_GC), cst3), pl.BlockSpec((1, POOL_CH), cst2),
                  pl.BlockSpec((SGU_GROUPS, CHUNK, CHUNK), cst3), pl.BlockSpec((CHUNK, SGU_GROUPS), cst2)],
        out_specs=pl.BlockSpec((CHUNK, D_MODEL), row),
        out_shape=jax.ShapeDtypeStruct((batch * seq, D_MODEL), BF16),
        scratch_shapes=[pltpu.VMEM((POOL_HALO + CHUNK, POOL_CH), F32)],
        compiler_params=_cparams("parallel", "arbitrary"), name="prompt_pool_sgu",
    )(pin, pin, u, vn, pool_w.astype(BF16), pool_scale.reshape(1, POOL_CH), sgu_w, sgu_b.T)


def _group_rows(nrows):
    return lax.broadcasted_iota(jnp.int32, (nrows, 1), 0) >> (HPG.bit_length() - 1)


def _sample_cmp_body(q_ref, *refs, qpos, nsb_past, k_past):
    ckv_refs, o_ref, idx_ref = refs[:2 * KV_HEADS], refs[2 * KV_HEADS], refs[2 * KV_HEADS + 1]
    q = q_ref[...].astype(BF16)
    rg = _group_rows(N_HEADS)
    n = lax.broadcasted_iota(jnp.int32, (1, nsb_past), 1)
    s_par, mk_par = [], []
    for par in range(2):
        s_acc = jnp.zeros((N_HEADS, nsb_past), F32)
        for g in range(KV_HEADS):
            ck = ckv_refs[g][pl.ds(par, nsb_past, stride=2), :].astype(BF16)
            s = lax.dot_general(q, ck, _NT, preferred_element_type=F32) * SCALE
            s_acc = jnp.where(rg == g, s, s_acc)
        mk = jnp.broadcast_to(((2 * n + par + 1) * CMP_BLOCK - 1) <= qpos, (N_HEADS, nsb_past))
        s_par.append(jnp.where(mk, s_acc, NEG))
        mk_par.append(mk)
    mx = jnp.maximum(jnp.max(s_par[0], axis=-1, keepdims=True), jnp.max(s_par[1], axis=-1, keepdims=True))
    pe = [jnp.where(mk, jnp.exp(s - mx), 0.0) for s, mk in zip(s_par, mk_par)]
    den = jnp.maximum(jnp.sum(pe[0], axis=-1, keepdims=True) + jnp.sum(pe[1], axis=-1, keepdims=True), 1e-30)
    pn = [p / den for p in pe]
    o = jnp.zeros((N_HEADS, HSLOT), F32)
    for g in range(KV_HEADS):
        og = jnp.zeros((N_HEADS, HSLOT), F32)
        for par in range(2):
            cv = ckv_refs[KV_HEADS + g][pl.ds(par, nsb_past, stride=2), :]
            og = og + jnp.dot(pn[par].astype(BF16), cv.astype(BF16), preferred_element_type=F32)
        o = jnp.where(rg == g, og, o)
    o_ref[...] = o
    pp = pn[0] + pn[1]
    ri = lax.broadcasted_iota(jnp.int32, (nsb_past, nsb_past), 0)
    ci = lax.broadcasted_iota(jnp.int32, (nsb_past, nsb_past), 1)
    slot = lax.broadcasted_iota(jnp.int32, (TOP_K, 1), 0)
    for g in range(KV_HEADS):
        imp = jnp.sum(jnp.where(rg == g, pp, 0.0), axis=0, keepdims=True)
        vis = (n * SEL_BLOCK) <= qpos
        imp = jnp.where(vis, imp, -FORCE)
        a = jnp.broadcast_to(imp, (nsb_past, nsb_past))
        bt = a.T
        beats = (bt > a) | ((bt == a) & (ri < ci))
        rank = jnp.sum(jnp.where(beats, 1.0, 0.0), axis=0, keepdims=True)
        onehot = jnp.where(rank == slot.astype(F32), 1.0, 0.0)
        idx = jnp.sum(onehot * n.astype(F32), axis=-1, keepdims=True)
        idx = jnp.where(slot < k_past, idx, 0.0)
        idx_ref[g * TOP_K:(g + 1) * TOP_K, :] = jnp.broadcast_to(idx, (TOP_K, LANES)).astype(jnp.int32)


def _sample_cmp(q3, ckv, *, bd, past, qpos):
    ncb = past // CMP_BLOCK
    nsb_past = past // SEL_BLOCK
    k_past = min(TOP_K - 1, nsb_past)
    body = functools.partial(_sample_cmp_body, qpos=qpos, nsb_past=nsb_past, k_past=k_past)
    return pl.pallas_call(
        body, grid=(bd,),
        in_specs=[pl.BlockSpec((None, N_HEADS, HSLOT), lambda b: (b, 0, 0))]
        + [pl.BlockSpec((ncb, HSLOT), functools.partial(lambda b, h: (b, h), h=h)) for h in range(2 * KV_HEADS)],
        out_specs=[pl.BlockSpec((None, N_HEADS, HSLOT), lambda b: (b, 0, 0)),
                   pl.BlockSpec((None, KV_HEADS * TOP_K, LANES), lambda b: (b, 0, 0))],
        out_shape=[jax.ShapeDtypeStruct((bd, N_HEADS, HSLOT), F32),
                   jax.ShapeDtypeStruct((bd, KV_HEADS * TOP_K, LANES), jnp.int32)],
        compiler_params=_cparams("parallel"), name="sample_cmp_attention",
    )(q3, *([ckv] * (2 * KV_HEADS)))


Q_PAD_ROWS = 8


def _sample_sel_body(pt_ref, idx_ref, *refs, k_past):
    blk_refs = refs[:k_past]
    sel_ref, q_ref, knew_ref, vnew_ref, o_ref, k_scr, v_scr = refs[k_past:]
    b = pl.program_id(0)
    for s in range(k_past):
        kv = jnp.dot(blk_refs[s][...].astype(BF16), sel_ref[...], preferred_element_type=F32)
        k_scr[s * SEL_BLOCK:(s + 1) * SEL_BLOCK, :] = kv[:, :HSLOT].astype(BF16)
        v_scr[s * SEL_BLOCK:(s + 1) * SEL_BLOCK, :] = kv[:, HSLOT:].astype(BF16)
    q = q_ref[...].astype(BF16)
    s_old = lax.dot_general(q, k_scr[...], _NT, preferred_element_type=F32) * SCALE
    s_all = lax.dot_general(q, knew_ref[...].astype(BF16), _NT, preferred_element_type=F32) * SCALE
    lane = lax.broadcasted_iota(jnp.int32, s_all.shape, 1)
    s_new = jnp.sum(jnp.where(lane == b, s_all, 0.0), axis=-1, keepdims=True)
    mx = jnp.maximum(jnp.max(s_old, axis=-1, keepdims=True), s_new)
    p_old = jnp.exp(s_old - mx)
    p_new = jnp.exp(s_new - mx)
    den = jnp.maximum(jnp.sum(p_old, axis=-1, keepdims=True) + p_new, 1e-30)
    v_new = vnew_ref[pl.ds(b, 1), :].astype(BF16).astype(F32)
    o = jnp.dot(p_old.astype(BF16), v_scr[...], preferred_element_type=F32)
    o = o + p_new.astype(BF16).astype(F32) * v_new
    o_ref[...] = o / den


def _sample_sel(cache, page_table, idx, layer, sel_g, q4, kvs_new, *, bd, past):
    n_layers, n_pool = cache.shape[:2]
    nsb_past = past // SEL_BLOCK
    k_past = min(TOP_K - 1, nsb_past)
    bpp = PAGE_SIZE // SEL_BLOCK
    halves = cache.reshape(n_layers * n_pool * bpp, SEL_BLOCK, KVW)
    base = layer * n_pool

    def blk_map(s):
        def f(b, g, pt, ix):
            blk = ix[b, g, s]
            return ((base + pt[b, blk // bpp]) * bpp + blk % bpp, 0, 0)
        return f

    grid_spec = pltpu.PrefetchScalarGridSpec(
        num_scalar_prefetch=2, grid=(bd, KV_HEADS),
        in_specs=[pl.BlockSpec((None, SEL_BLOCK, KVW), blk_map(s)) for s in range(k_past)]
        + [pl.BlockSpec((None, KVW, 2 * HSLOT), lambda b, g, pt, ix: (g, 0, 0)),
           pl.BlockSpec((None, None, Q_PAD_ROWS, HSLOT), lambda b, g, pt, ix: (b, g, 0, 0)),
           pl.BlockSpec((SAMPLE_ROWS, HSLOT), lambda b, g, pt, ix: (0, g)),
           pl.BlockSpec((SAMPLE_ROWS, HSLOT), lambda b, g, pt, ix: (0, KV_HEADS + g))],
        out_specs=pl.BlockSpec((None, None, Q_PAD_ROWS, HSLOT), lambda b, g, pt, ix: (b, g, 0, 0)),
        scratch_shapes=[pltpu.VMEM((k_past * SEL_BLOCK, HSLOT), BF16), pltpu.VMEM((k_past * SEL_BLOCK, HSLOT), BF16)])
    return pl.pallas_call(
        functools.partial(_sample_sel_body, k_past=k_past), grid_spec=grid_spec,
        out_shape=jax.ShapeDtypeStruct((bd, KV_HEADS, Q_PAD_ROWS, HSLOT), F32),
        compiler_params=_cparams("parallel", "arbitrary"), name="sample_sel_attention",
    )(page_table, idx, *([halves] * k_past), sel_g, q4, kvs_new, kvs_new)


def _sample_win_body(q_ref, win_ref, selm_ref, new_ref, ocmp_ref, osel_ref, gate_ref, o_ref, *, qpos, past, wb):
    b = pl.program_id(0)
    q = q_ref[...].astype(BF16)
    rg = _group_rows(N_HEADS)
    kvp = jnp.dot(win_ref[...].astype(BF16), selm_ref[...], preferred_element_type=F32)
    new = new_ref[pl.ds(b, 1), :]
    kpos = (past - wb) + lax.broadcasted_iota(jnp.int32, (1, wb), 1)
    mk = jnp.broadcast_to((kpos <= qpos) & (kpos >= qpos - WINDOW), (N_HEADS, wb))
    s_old = jnp.zeros((N_HEADS, wb), F32)
    s_new = jnp.zeros((N_HEADS, 1), F32)
    qf = q.astype(F32)
    for g in range(KV_HEADS):
        sl = slice(g * HSLOT, (g + 1) * HSLOT)
        s = lax.dot_general(q, kvp[:, sl].astype(BF16), _NT, preferred_element_type=F32) * SCALE
        s_old = jnp.where(rg == g, s, s_old)
        sn = jnp.sum(qf * new[:, sl].astype(BF16).astype(F32), axis=-1, keepdims=True) * SCALE
        s_new = jnp.where(rg == g, sn, s_new)
    s_old = jnp.where(mk, s_old, NEG)
    mx = jnp.maximum(jnp.max(s_old, axis=-1, keepdims=True), s_new)
    p_old = jnp.where(mk, jnp.exp(s_old - mx), 0.0)
    p_new = jnp.exp(s_new - mx)
    den = jnp.maximum(jnp.sum(p_old, axis=-1, keepdims=True) + p_new, 1e-30)
    o_win = jnp.zeros((N_HEADS, HSLOT), F32)
    for g in range(KV_HEADS):
        sl = slice((KV_HEADS + g) * HSLOT, (KV_HEADS + g + 1) * HSLOT)
        og = jnp.dot(p_old.astype(BF16), kvp[:, sl].astype(BF16), preferred_element_type=F32)
        og = og + p_new.astype(BF16).astype(F32) * new[:, sl].astype(BF16).astype(F32)
        o_win = jnp.where(rg == g, og, o_win)
    o_win = o_win / den
    gts = gate_ref[...]
    o_ref[...] = gts[:, 0:1] * ocmp_ref[...] + gts[:, 1:2] * osel_ref[...] + gts[:, 2:3] * o_win


def _sample_win(q3, win_cache, layer, selm, kvw_new, o_cmp, o_sel, gates3, *, bd, past, qpos):
    n_layers, _, wb = win_cache.shape[:3]
    win = win_cache.reshape(n_layers * bd, wb, KVW)
    head3 = pl.BlockSpec((None, N_HEADS, HSLOT), lambda b: (b, 0, 0))
    return pl.pallas_call(
        functools.partial(_sample_win_body, qpos=qpos, past=past, wb=wb), grid=(bd,),
        in_specs=[head3, pl.BlockSpec((None, wb, KVW), lambda b: (layer * bd + b, 0, 0)),
                  pl.BlockSpec((KVW, KVP), lambda b: (0, 0)),
                  pl.BlockSpec((SAMPLE_ROWS, KVP), lambda b: (0, 0)), head3, head3, head3],
        out_specs=head3,
        out_shape=jax.ShapeDtypeStruct((bd, N_HEADS, HSLOT), F32),
        compiler_params=_cparams("parallel"), name="sample_win_attention",
    )(q3, win, selm, kvw_new, o_cmp, o_sel, gates3)


def _sample_conv_body(st_ref, u_ref, w_ref, b_ref, g_ref, bn_ref, o_ref, *, bd):
    w = w_ref[...]
    y = jnp.sum(st_ref[...] * w[None, :CONV_W - 1, :], axis=1) + u_ref[0:bd, :] * w[CONV_W - 1:CONV_W, :] + b_ref[...]
    y = _ln_rows(y, g_ref[...], bn_ref[...])
    o_ref[...] = y * jax.nn.sigmoid(y)


def _sample_conv(state, layer, u, cw, cb, cg, cbn, *, bd):
    vec = lambda a: a.reshape(1, CONV_CH)
    cst = lambda i: (0, 0)
    return pl.pallas_call(
        functools.partial(_sample_conv_body, bd=bd), grid=(1,),
        in_specs=[pl.BlockSpec((None, bd, CONV_W - 1, CONV_CH), lambda i: (layer, 0, 0, 0)),
                  pl.BlockSpec((SAMPLE_ROWS, CONV_CH), cst), pl.BlockSpec((CONV_W, CONV_CH), cst),
                  pl.BlockSpec((1, CONV_CH), cst), pl.BlockSpec((1, CONV_CH), cst), pl.BlockSpec((1, CONV_CH), cst)],
        out_specs=pl.BlockSpec((bd, CONV_CH), cst),
        out_shape=jax.ShapeDtypeStruct((bd, CONV_CH), F32),
        compiler_params=_cparams("arbitrary"), name="sample_conv",
    )(state, u, cw, vec(cb), vec(cg), vec(cbn))


def _sample_odd_body(st_ref, pin_ref, u_ref, vn_ref, pw_ref, ps_ref, w0_ref, b0_ref, o_ref, *, bd, start_pos):
    pin = pin_ref[0:bd, :]
    st = st_ref[...]
    for g, w in enumerate(POOL_WINDOWS):
        sl = slice(g * POOL_GC, (g + 1) * POOL_GC)
        tot = pin[:, sl] + jnp.sum(st[:, POOL_STATE - (w - 1):, sl], axis=1)
        d = tot / float(min(w, start_pos + 1)) - pin[:, sl]
        dp = jnp.concatenate([d, jnp.zeros((SAMPLE_ROWS - bd, POOL_GC), F32)], axis=0).astype(BF16)
        y = jnp.dot(dp, pw_ref[g], preferred_element_type=F32)[0:bd]
        o_ref[:, sl] = y * ps_ref[:, sl]
    mixed = w0_ref[...] * vn_ref[0:bd, :] + b0_ref[...]
    o_ref[:, POOL_CH:] = u_ref[0:bd, :] * mixed


def _sample_odd_mix(state, layer, pin, u, vn, pool_w, pool_scale, sgu_w, sgu_b, *, bd, start_pos):
    w0 = jnp.repeat(sgu_w[:, 0, 0], SGU_GC).reshape(1, SGU_CH)
    b0 = jnp.repeat(sgu_b[:, 0], SGU_GC).reshape(1, SGU_CH)
    cst = lambda i: (0, 0)
    return pl.pallas_call(
        functools.partial(_sample_odd_body, bd=bd, start_pos=start_pos), grid=(1,),
        in_specs=[pl.BlockSpec((None, bd, POOL_STATE, POOL_CH), lambda i: (layer, 0, 0, 0)),
                  pl.BlockSpec((SAMPLE_ROWS, POOL_CH), cst), pl.BlockSpec((SAMPLE_ROWS, SGU_CH), cst),
                  pl.BlockSpec((SAMPLE_ROWS, SGU_CH), cst),
                  pl.BlockSpec((POOL_GROUPS, POOL_GC, POOL_GC), lambda i: (0, 0, 0)),
                  pl.BlockSpec((1, POOL_CH), cst), pl.BlockSpec((1, SGU_CH), cst), pl.BlockSpec((1, SGU_CH), cst)],
        out_specs=pl.BlockSpec((bd, D_MODEL), cst),
        out_shape=jax.ShapeDtypeStruct((bd, D_MODEL), F32),
        compiler_params=_cparams("arbitrary"), name="sample_pool_sgu",
    )(state, pin, u, vn, pool_w.astype(BF16), pool_scale.reshape(1, POOL_CH), w0, b0)


def _pad_rows(x, rows):
    return jnp.pad(x, ((0, rows - x.shape[0]), (0, 0)))


def _even_in_proj(xb, w_in, cos, sin, *, tm):
    m = xb.shape[0]
    nrep = cos.shape[0] // tm
    tab = lambda i, j: (i % nrep, 0)
    rope_ex = ((cos, (tm, LANES), tab), (sin, (tm, LANES), tab))
    gw = HPG * HSLOT
    qraw, qrot = _mm(xb, w_in, tm=tm, tn=gw, n_off=E_Q, n_cols=QW, epilogue=_ep_q, extras=rope_ex,
                     outs=((QW, BF16, gw), (QW, BF16, gw)), name="even_in_q")
    (kvc,) = _mm(xb, w_in, tm=tm, tn=KVP, n_off=E_KVC, n_cols=KVP, epilogue=functools.partial(_ep_kv, rope=False),
                 outs=((KVP, F32, KVP),), name="even_in_kvc")
    kvs, kvs_b = _mm(xb, w_in, tm=tm, tn=KVP, n_off=E_KVS, n_cols=KVP, epilogue=functools.partial(_ep_kv, rope=True),
                     extras=rope_ex, outs=((KVP, F32, KVP), (KVP, BF16, KVP)), name="even_in_kvs")
    kvw, kvw_b = _mm(xb, w_in, tm=tm, tn=KVP, n_off=E_KVW, n_cols=KVP, epilogue=functools.partial(_ep_kv, rope=True),
                     extras=rope_ex, outs=((KVP, F32, KVP), (KVP, BF16, KVP)), name="even_in_kvw")
    (u,) = _mm(xb, w_in, tm=tm, tn=CONV_CH, n_off=E_GLU, n_cols=2 * CONV_CH, epilogue=_ep_glu,
               outs=((CONV_CH, F32, CONV_CH // 2),), name="even_in_glu")
    (gates,) = _mm(xb, w_in, tm=tm, tn=GATE_W, n_off=E_GATE, n_cols=GATE_W, epilogue=_ep_sigmoid,
                   outs=((GATE_W, F32, GATE_W),), name="even_in_gates")
    return qraw, qrot, kvc, kvs, kvs_b, kvw, kvw_b, u, gates


def _mlp(x, xb, w1, w2, g, b, *, tm1, tn1, tm2, tk2):
    (h,) = _mm(xb, w1, tm=tm1, tn=tn1, n_off=0, n_cols=w1.shape[1], epilogue=_ep_relu2,
               outs=((w1.shape[1], BF16, tn1),), name="mlp_up")
    return _mlp2_ln(h, w2, x, g, b, tm=tm2, tk=tk2, name="mlp_down_ln")


def kernel(x_prompt, x_sample, cache_cmp_kv, cache_sel_kv, cache_win_kv, state_conv, state_pool, page_table,
           w_in_even, w_out_even, cmp_pe_k, cmp_pe_v, cmp_w_k, cmp_w_v, conv_w, conv_b, conv_ln_g, conv_ln_b,
           w_in_odd, w_out_odd, pool_w, pool_scale, sgu_ln_g, sgu_ln_b, sgu_w, sgu_b,
           mlp_w1, mlp_w2, ln_mix_g, ln_mix_b, ln_ffn_g, ln_ffn_b):
    B, S, D = x_prompt.shape
    Bd, Sd, _ = x_sample.shape
    n_pages = page_table.shape[1]
    past = n_pages * PAGE_SIZE
    assert D == D_MODEL and Sd == 1 and Bd <= SAMPLE_ROWS
    assert S % 1024 == 0 and past % SEL_BLOCK == 0 and S >= WINDOW
    M = B * S
    Ms = SAMPLE_ROWS
    tm_p = 1024

    cos_p, sin_p = _rope_tables(jnp.arange(S, dtype=jnp.int32))
    cos_s, sin_s = _rope_tables(jnp.full((Ms,), past, jnp.int32))
    selm = _head_select_matrix().astype(BF16)
    sel_g = selm.reshape(KVW, 2, KV_HEADS, HSLOT).transpose(2, 0, 1, 3).reshape(KV_HEADS, KVW, 2 * HSLOT)
    pps = 8 if n_pages % 8 == 0 else 2

    xp = x_prompt.reshape(M, D)
    xs = _pad_rows(x_sample.reshape(Bd, D), Ms)
    xpb, xsb = xp.astype(BF16), xs.astype(BF16)

    outs = {k: [] for k in ("cmp_p", "cmp_s", "sel_p", "sel_s", "win_p", "win_s", "conv_p", "conv_s",
                            "pool_p", "pool_s", "sgu_p", "sgu_s")}
    kv6 = lambda a, lead: a.reshape(lead + (2, KV_HEADS, HEAD_DIM))

    for layer in range(DEPTH):
        if layer % 2 == 0:
            e = layer // 2
            wts = _prep_even_weights(w_in_even[e], w_out_even[e], cmp_pe_k[e], cmp_pe_v[e], cmp_w_k[e], cmp_w_v[e])
            qraw, qrot, kvc, kvs, kvs_b, kvw, kvw_b, u, gates = _even_in_proj(xpb, wts["w_in"], cos_p, sin_p, tm=tm_p)
            summ = _compress_rows(kvc, wts["pe"], rows=512, name="prompt_compress")
            (ckv,) = _mm(summ.astype(BF16), wts["big_p"], tm=summ.shape[0] if summ.shape[0] <= 512 else 512, tn=KVP,
                         n_off=0, n_cols=KVP, epilogue=_ep_plain, outs=((KVP, F32, KVP),), name="prompt_compress_map")
            o_att = _prompt_attention(qraw, qrot, gates, ckv, kvs_b, kvw_b, batch=B, seq=S, tq=128)
            c = _prompt_conv(u, conv_w[e], conv_b[e], conv_ln_g[e], conv_ln_b[e], batch=B, seq=S, ts=256)
            xp, xpb = _proj_ln([o_att, c], [wts["wo_att"], wts["wo_conv"]], xp, ln_mix_g[layer], ln_mix_b[layer],
                               tm=512, name="even_out_ln")
            outs["cmp_p"].append(kv6(_unpad_heads(kvc, 2 * KV_HEADS), (B, S)))
            outs["sel_p"].append(kv6(_unpad_heads(kvs, 2 * KV_HEADS), (B, S)))
            outs["win_p"].append(kv6(_unpad_heads(kvw.reshape(B, S, KVP)[:, S - WINDOW:], 2 * KV_HEADS), (B, WINDOW)))
            outs["conv_p"].append(u.reshape(B, S, CONV_CH)[:, S - (CONV_W - 1):])
            qraw_s, qrot_s, kvc_s, kvs_s, _, kvw_s, _, u_s, gates_s = _even_in_proj(xsb, wts["w_in"], cos_s, sin_s, tm=Ms)
            summ_s = _compress_pages(cache_cmp_kv, page_table, wts["pe_c"], e, pages_per_step=pps, name="sample_compress")
            (ckv_s,) = _mm(summ_s.astype(BF16), wts["big_c"], tm=past // CMP_BLOCK, tn=KVP, n_off=0, n_cols=KVP,
                           epilogue=_ep_plain, outs=((KVP, F32, KVP),), name="sample_compress_map")
            q3 = qraw_s.astype(F32)[:Bd].reshape(Bd, N_HEADS, HSLOT)
            o_cmp, idx = _sample_cmp(q3, ckv_s, bd=Bd, past=past, qpos=past)
            idx = idx[:, :, 0].reshape(Bd, KV_HEADS, TOP_K)
            qr3 = qrot_s.astype(F32)[:Bd].reshape(Bd, N_HEADS, HSLOT)
            q4 = jnp.pad(qr3.reshape(Bd, KV_HEADS, HPG, HSLOT), ((0, 0), (0, 0), (0, Q_PAD_ROWS - HPG), (0, 0)))
            o_sel = _sample_sel(cache_sel_kv, page_table, idx, e, sel_g, q4, kvs_s, bd=Bd, past=past)
            o_sel = o_sel[:, :, :HPG].reshape(Bd, N_HEADS, HSLOT)
            g3 = gates_s[:Bd].reshape(Bd, KV_HEADS, LANES)[:, :, :3 * HPG].reshape(Bd, KV_HEADS, 3, HPG)
            g3 = g3.transpose(0, 1, 3, 2).reshape(Bd, N_HEADS, 3)
            g3 = jnp.pad(g3, ((0, 0), (0, 0), (0, LANES - 3)))
            o_s = _sample_win(qr3, cache_win_kv, e, selm, kvw_s, o_cmp, o_sel, g3, bd=Bd, past=past, qpos=past)
            c_s = _sample_conv(state_conv, e, u_s, conv_w[e], conv_b[e], conv_ln_g[e], conv_ln_b[e], bd=Bd)
            o_sb = _pad_rows(o_s.reshape(Bd, QW), Ms).astype(BF16)
            c_sb = _pad_rows(c_s, Ms).astype(BF16)
            xs, xsb = _proj_ln([o_sb, c_sb], [wts["wo_att"], wts["wo_conv"]], xs, ln_mix_g[layer], ln_mix_b[layer],
                               tm=Ms, name="even_out_ln_s")
            kvc_c = _unpad_heads(kvc_s[:Bd], 2 * KV_HEADS)
            kvs_c = _unpad_heads(kvs_s[:Bd], 2 * KV_HEADS)
            kvw_c = _unpad_heads(kvw_s[:Bd], 2 * KV_HEADS)
            outs["cmp_s"].append(kv6(kvc_c, (Bd, 1)))
            outs["sel_s"].append(kv6(kvs_c, (Bd, 1)))
            wkv = jnp.concatenate([cache_win_kv[e], kv6(kvw_c, (Bd, 1))], axis=1)
            outs["win_s"].append(wkv[:, wkv.shape[1] - min(WINDOW, wkv.shape[1]):])
            outs["conv_s"].append(jnp.concatenate([state_conv[e], u_s[:Bd, None, :]], axis=1)[:, 1:])
        else:
            o = layer // 2
            w_in = w_in_odd[o]
            w_in_p = jnp.concatenate([w_in[:, POOL_CH + SGU_CH:], w_in[:, :POOL_CH], w_in[:, POOL_CH:POOL_CH + SGU_CH]],
                                     axis=1).astype(BF16)
            w_out_p = w_out_odd[o].astype(BF16)
            lg, lb = sgu_ln_g[o].reshape(1, SGU_CH), sgu_ln_b[o].reshape(1, SGU_CH)

            def odd_in(xb, tm):
                gl_ex = ((lg, (1, 2 * SGU_GC), lambda i, j: (0, j)), (lb, (1, 2 * SGU_GC), lambda i, j: (0, j)))
                (vn,) = _mm(xb, w_in_p, tm=tm, tn=2 * SGU_GC, n_off=O_V, n_cols=SGU_CH, epilogue=_ep_gelu_gln,
                            extras=gl_ex, outs=((SGU_CH, F32, 2 * SGU_GC),), name="odd_in_v")
                (pin,) = _mm(xb, w_in_p, tm=tm, tn=POOL_CH, n_off=O_PIN, n_cols=POOL_CH, epilogue=_ep_plain,
                             outs=((POOL_CH, F32, POOL_CH),), name="odd_in_pool")
                (uu,) = _mm(xb, w_in_p, tm=tm, tn=POOL_CH, n_off=O_U, n_cols=SGU_CH, epilogue=_ep_gelu,
                            outs=((SGU_CH, F32, POOL_CH),), name="odd_in_u")
                return vn, pin, uu

            vn, pin, uu = odd_in(xpb, tm_p)
            cat = _prompt_odd_mix(pin, uu, vn, pool_w[o], pool_scale[o], sgu_w[o], sgu_b[o], batch=B, seq=S)
            xp, xpb = _proj_ln([cat], [w_out_p], xp, ln_mix_g[layer], ln_mix_b[layer], tm=512, name="odd_out_ln")
            outs["pool_p"].append(pin.reshape(B, S, POOL_CH)[:, S - POOL_STATE:])
            outs["sgu_p"].append(vn.reshape(B, S, SGU_CH)[:, ((S - 1) // CHUNK) * CHUNK:])
            vn_s, pin_s, uu_s = odd_in(xsb, Ms)
            cat_s = _sample_odd_mix(state_pool, o, pin_s, uu_s, vn_s, pool_w[o], pool_scale[o], sgu_w[o], sgu_b[o],
                                    bd=Bd, start_pos=past)
            xs, xsb = _proj_ln([_pad_rows(cat_s, Ms).astype(BF16)], [w_out_p], xs, ln_mix_g[layer], ln_mix_b[layer],
                               tm=Ms, name="odd_out_ln_s")
            outs["pool_s"].append(jnp.concatenate([state_pool[o], pin_s[:Bd, None, :]], axis=1)[:, 1:])
            outs["sgu_s"].append(vn_s[:Bd, None, :])
        w1b, w2b = mlp_w1[layer].astype(BF16), mlp_w2[layer].astype(BF16)
        xp, xpb = _mlp(xp, xpb, w1b, w2b, ln_ffn_g[layer], ln_ffn_b[layer], tm1=tm_p, tn1=1024, tm2=512, tk2=1024)
        xs, xsb = _mlp(xs, xsb, w1b, w2b, ln_ffn_g[layer], ln_ffn_b[layer], tm1=Ms, tn1=2048, tm2=Ms, tk2=2048)

    st = lambda k: jnp.stack(outs[k])
    return (xp.reshape(B, S, D), xs[:Bd].reshape(Bd, Sd, D),
            st("cmp_p"), st("cmp_s"), st("sel_p"), st("sel_s"), st("win_p"), st("win_s"),
            st("conv_p"), st("conv_s"), st("pool_p"), st("pool_s"), st("sgu_p"), st("sgu_s"))
```

```python
import functools

import jax
import jax.numpy as jnp
from jax import lax
from jax.experimental import pallas as pl
from jax.experimental.pallas import tpu as pltpu

F32 = jnp.float32
BF16 = jnp.bfloat16

D_MODEL = 2048
DEPTH = 4
PAGE_SIZE = 128
N_HEADS = 16
HEAD_DIM = 96
KV_HEADS = 4
HPG = N_HEADS // KV_HEADS
ATT_W = N_HEADS * HEAD_DIM
KVW = 2 * KV_HEADS * HEAD_DIM
CMP_BLOCK = 32
SEL_BLOCK = 64
TOP_K = 16
WINDOW = 512
ROPE_THETA = 10000.0
SCALE = HEAD_DIM ** -0.5
FORCE = 1e9
NEG = -1e30
CONV_CH = D_MODEL // 4
CONV_W = 31
POOL_CH = D_MODEL // 4
POOL_WINDOWS = (2, 4, 8, 16)
POOL_GROUPS = len(POOL_WINDOWS)
POOL_GC = POOL_CH // POOL_GROUPS
POOL_STATE = max(POOL_WINDOWS) - 1
SGU_CH = D_MODEL - POOL_CH
SGU_GROUPS = 4
SGU_GC = SGU_CH // SGU_GROUPS
CHUNK = 128
D_FF = 4 * D_MODEL
ALPHA = (2 * DEPTH) ** 0.25
LN_EPS = 1e-5

LANES = 128
SUBLANES = 8
HALF = HEAD_DIM // 2
HSLOT = LANES
HALF_OFF = LANES // 2
QW = N_HEADS * HSLOT
KVP = 2 * KV_HEADS * HSLOT
GATE_W = KV_HEADS * LANES
SAMPLE_ROWS = 16
VMEM_LIMIT = 52 * 1024 * 1024

E_Q, E_KVC, E_KVS, E_KVW = 0, QW, QW + KVP, QW + 2 * KVP
E_GLU = QW + 3 * KVP
E_GATE = E_GLU + 2 * CONV_CH
E_TOT = E_GATE + GATE_W
O_V, O_PIN, O_U = 0, SGU_CH, SGU_CH + POOL_CH


def _cparams(*sem):
    return pltpu.CompilerParams(dimension_semantics=sem, vmem_limit_bytes=VMEM_LIMIT)


def _pad_head(x):
    z = jnp.zeros(x.shape[:-1] + (HALF_OFF - HALF,), x.dtype)
    return jnp.concatenate([x[..., :HALF], z, x[..., HALF:], z], axis=-1)


def _pad_nat(x):
    return jnp.concatenate([x, jnp.zeros(x.shape[:-1] + (HSLOT - HEAD_DIM,), x.dtype)], axis=-1)


def _unpad_heads(x, nheads):
    xh = x.reshape(x.shape[:-1] + (nheads, HSLOT))
    y = jnp.concatenate([xh[..., :HALF], xh[..., HALF_OFF:HALF_OFF + HALF]], axis=-1)
    return y.reshape(x.shape[:-1] + (nheads * HEAD_DIM,))


def _rope_tables(pos):
    inv = jnp.power(ROPE_THETA, -jnp.arange(HALF, dtype=F32) / HALF)
    ang = pos.astype(F32)[:, None] * inv[None, :]
    cos, sin = jnp.cos(ang), jnp.sin(ang)
    z = jnp.zeros((pos.shape[0], HALF_OFF - HALF), F32)
    return (jnp.concatenate([cos, z, cos, z], axis=1),
            jnp.concatenate([-sin, z, sin, z], axis=1))


def _block_diag2(a, b):
    za = jnp.zeros((a.shape[0], b.shape[1]), a.dtype)
    zb = jnp.zeros((b.shape[0], a.shape[1]), a.dtype)
    return jnp.concatenate([jnp.concatenate([a, za], axis=1), jnp.concatenate([zb, b], axis=1)], axis=0)


def _prep_even_weights(w_in, w_out, pe_k, pe_v, w_ck, w_cv):
    d = w_in.shape[0]
    q = _pad_head(w_in[:, :ATT_W].reshape(d, N_HEADS, HEAD_DIM)).reshape(d, QW)
    kvs = []
    for s in range(3):
        blk = w_in[:, ATT_W + s * KVW:ATT_W + (s + 1) * KVW].reshape(d, 2 * KV_HEADS, HEAD_DIM)
        kvs.append(_pad_head(blk).reshape(d, KVP))
    g0 = ATT_W + 3 * KVW
    gates = w_in[:, g0:g0 + 3 * N_HEADS].reshape(d, 3, KV_HEADS, HPG).transpose(0, 2, 1, 3)
    gates = gates.reshape(d, KV_HEADS, 3 * HPG)
    gates = jnp.pad(gates, ((0, 0), (0, 0), (0, LANES - 3 * HPG))).reshape(d, GATE_W)
    glu = w_in[:, g0 + 3 * N_HEADS:]
    a = glu[:, :CONV_CH].reshape(d, 2, CONV_CH // 2)
    g = glu[:, CONV_CH:].reshape(d, 2, CONV_CH // 2)
    glu = jnp.stack([a, g], axis=2).reshape(d, 2 * CONV_CH)
    w_in_p = jnp.concatenate([q] + kvs + [glu, gates], axis=1).astype(BF16)
    wo_att = _pad_head(w_out[:ATT_W].reshape(N_HEADS, HEAD_DIM, d).transpose(0, 2, 1))
    wo_att = wo_att.transpose(0, 2, 1).reshape(QW, d).astype(BF16)
    wo_conv = w_out[ATT_W:].astype(BF16)
    eye = jnp.eye(KV_HEADS, dtype=F32)
    pe = jnp.concatenate([jnp.tile(_pad_head(pe_k), (1, KV_HEADS)), jnp.tile(_pad_head(pe_v), (1, KV_HEADS))], axis=1)
    wk_full = _pad_head(_pad_head(w_ck).T).T
    wv_full = _pad_head(_pad_head(w_cv).T).T
    big_p = _block_diag2(jnp.kron(eye, wk_full), jnp.kron(eye, wv_full))
    pe_t = jnp.concatenate([jnp.tile(jnp.tile(pe_k.T, (1, PAGE_SIZE // CMP_BLOCK)), (KV_HEADS, 1)),
                            jnp.tile(jnp.tile(pe_v.T, (1, PAGE_SIZE // CMP_BLOCK)), (KV_HEADS, 1))], axis=0)
    big_t = _block_diag2(jnp.kron(eye, _pad_head(w_ck).T), jnp.kron(eye, _pad_nat(w_cv).T))
    return dict(w_in=w_in_p, wo_att=wo_att, wo_conv=wo_conv, pe=pe, pe_t=pe_t,
                big_p=big_p.astype(BF16), big_t=big_t.astype(BF16))


def _block_sum_matrices(pps):
    bpp = PAGE_SIZE // CMP_BLOCK
    nbl = bpp * pps
    p = jnp.arange(pps)[:, None, None]
    i = (jnp.arange(PAGE_SIZE) // CMP_BLOCK)[None, :, None]
    c = jnp.arange(nbl)[None, None, :]
    col = (i % 2) * (nbl // 2) + (bpp // 2) * p + i // 2
    return jnp.where(c == col, 1.0 / CMP_BLOCK, 0.0).astype(BF16)


def _ln_rows(y, g, b):
    mu = jnp.mean(y, axis=-1, keepdims=True)
    yc = y - mu
    var = jnp.mean(yc * yc, axis=-1, keepdims=True)
    return yc * lax.rsqrt(var + LN_EPS) * g + b


def _rope_slot(x, cos, sin):
    return x * cos + pltpu.roll(x, HALF_OFF, axis=1) * sin


_NT = (((1,), (1,)), ((), ()))


def _pad_rows_to(x, rows):
    return jnp.concatenate([x, jnp.zeros((rows - x.shape[0],) + x.shape[1:], x.dtype)], axis=0)


def _mm(x, w, *, tm, tn, n_off, n_cols, epilogue, extras=(), outs, name):
    m, k = x.shape
    assert m % tm == 0 and n_cols % tn == 0 and n_off % tn == 0
    joff = n_off // tn
    in_specs = [pl.BlockSpec((tm, k), lambda i, j: (i, 0)),
                pl.BlockSpec((k, tn), lambda i, j: (0, joff + j))]
    in_specs += [pl.BlockSpec(bs, im) for _, bs, im in extras]
    out_shape = [jax.ShapeDtypeStruct((m, c), dt) for c, dt, _ in outs]
    out_specs = [pl.BlockSpec((tm, bc), lambda i, j: (i, j)) for _, _, bc in outs]
    ne = len(extras)

    def body(x_ref, w_ref, *refs):
        acc = jnp.dot(x_ref[...], w_ref[...], preferred_element_type=F32)
        epilogue(acc, refs[:ne], refs[ne:])

    return pl.pallas_call(
        body, grid=(m // tm, n_cols // tn), in_specs=in_specs, out_specs=out_specs, out_shape=out_shape,
        compiler_params=_cparams("parallel", "arbitrary"), name=name,
    )(x, w, *[a for a, _, _ in extras])


def _ep_q(acc, ex, outs):
    cos, sin = ex[0][...], ex[1][...]
    outs[0][...] = acc.astype(BF16)
    for j in range(acc.shape[1] // HSLOT):
        sl = slice(j * HSLOT, (j + 1) * HSLOT)
        outs[1][:, sl] = _rope_slot(acc[:, sl], cos, sin).astype(BF16)


def _ep_kv(acc, ex, outs, *, rope):
    if rope:
        cos, sin = ex[0][...], ex[1][...]
        for j in range(KV_HEADS):
            sl = slice(j * HSLOT, (j + 1) * HSLOT)
            r = _rope_slot(acc[:, sl], cos, sin)
            outs[0][:, sl] = r
            outs[1][:, sl] = r.astype(BF16)
        vs = slice(KV_HEADS * HSLOT, KVP)
        outs[0][:, vs] = acc[:, vs]
        outs[1][:, vs] = acc[:, vs].astype(BF16)
    else:
        outs[0][...] = acc


def _ep_glu(acc, ex, outs):
    h = acc.shape[1] // 2
    outs[0][...] = acc[:, :h] * jax.nn.sigmoid(acc[:, h:])


def _ep_sigmoid(acc, ex, outs):
    outs[0][...] = jax.nn.sigmoid(acc)


def _ep_plain(acc, ex, outs):
    outs[0][...] = acc.astype(outs[0].dtype)


def _ep_relu2(acc, ex, outs):
    r = jnp.maximum(acc, 0.0)
    outs[0][...] = (r * r).astype(outs[0].dtype)


def _ep_gelu(acc, ex, outs):
    outs[0][...] = jax.nn.gelu(acc)


def _ep_gelu_gln(acc, ex, outs):
    g, b = ex[0][...], ex[1][...]
    v = jax.nn.gelu(acc)
    for j in range(acc.shape[1] // SGU_GC):
        sl = slice(j * SGU_GC, (j + 1) * SGU_GC)
        outs[0][:, sl] = _ln_rows(v[:, sl], g[:, sl], b[:, sl])


def _proj_ln(a_list, w_list, resid, g, b, *, tm, name):
    m, n = resid.shape
    npair = len(a_list)
    in_specs = []
    for a in a_list:
        in_specs.append(pl.BlockSpec((tm, a.shape[1]), lambda i: (i, 0)))
    for w in w_list:
        in_specs.append(pl.BlockSpec(w.shape, lambda i: (0, 0), pipeline_mode=pl.Buffered(1)))
    in_specs += [pl.BlockSpec((tm, n), lambda i: (i, 0)),
                 pl.BlockSpec((1, n), lambda i: (0, 0)), pl.BlockSpec((1, n), lambda i: (0, 0))]

    def body(*refs):
        a_refs, w_refs = refs[:npair], refs[npair:2 * npair]
        r_ref, g_ref, b_ref, o_ref, ob_ref = refs[2 * npair:]
        acc = ALPHA * r_ref[...]
        for a_ref, w_ref in zip(a_refs, w_refs):
            acc = acc + jnp.dot(a_ref[...], w_ref[...], preferred_element_type=F32)
        y = _ln_rows(acc, g_ref[...], b_ref[...])
        o_ref[...] = y
        ob_ref[...] = y.astype(BF16)

    return pl.pallas_call(
        body, grid=(m // tm,), in_specs=in_specs,
        out_specs=[pl.BlockSpec((tm, n), lambda i: (i, 0)), pl.BlockSpec((tm, n), lambda i: (i, 0))],
        out_shape=[jax.ShapeDtypeStruct((m, n), F32), jax.ShapeDtypeStruct((m, n), BF16)],
        compiler_params=_cparams("parallel"), name=name,
    )(*a_list, *w_list, resid, g.reshape(1, n), b.reshape(1, n))


def _mlp2_ln(h, w2, resid, g, b, *, tm, tk, name):
    m, kf = h.shape
    n = w2.shape[1]
    nk = kf // tk

    def body(h_ref, w_ref, r_ref, g_ref, b_ref, o_ref, ob_ref, acc_ref):
        k = pl.program_id(1)

        @pl.when(k == 0)
        def _():
            acc_ref[...] = ALPHA * r_ref[...]

        acc_ref[...] += jnp.dot(h_ref[...], w_ref[...], preferred_element_type=F32)

        @pl.when(k == nk - 1)
        def _():
            y = _ln_rows(acc_ref[...], g_ref[...], b_ref[...])
            o_ref[...] = y
            ob_ref[...] = y.astype(BF16)

    return pl.pallas_call(
        body, grid=(m // tm, nk),
        in_specs=[pl.BlockSpec((tm, tk), lambda i, k: (i, k)), pl.BlockSpec((tk, n), lambda i, k: (k, 0)),
                  pl.BlockSpec((tm, n), lambda i, k: (i, 0)),
                  pl.BlockSpec((1, n), lambda i, k: (0, 0)), pl.BlockSpec((1, n), lambda i, k: (0, 0))],
        out_specs=[pl.BlockSpec((tm, n), lambda i, k: (i, 0)), pl.BlockSpec((tm, n), lambda i, k: (i, 0))],
        out_shape=[jax.ShapeDtypeStruct((m, n), F32), jax.ShapeDtypeStruct((m, n), BF16)],
        scratch_shapes=[pltpu.VMEM((tm, n), F32)],
        compiler_params=_cparams("parallel", "arbitrary"), name=name,
    )(h, w2, resid, g.reshape(1, n), b.reshape(1, n))


def _compress_rows(kvc, pe, *, rows, name):
    m, c = kvc.shape
    nb = rows // CMP_BLOCK

    def body(x_ref, pe_ref, o_ref):
        x = x_ref[...].reshape(nb, CMP_BLOCK, c) * pe_ref[...][None]
        o_ref[...] = jnp.sum(x, axis=1) * (1.0 / CMP_BLOCK)

    return pl.pallas_call(
        body, grid=(m // rows,),
        in_specs=[pl.BlockSpec((rows, c), lambda i: (i, 0)), pl.BlockSpec((CMP_BLOCK, c), lambda i: (0, 0))],
        out_specs=pl.BlockSpec((nb, c), lambda i: (i, 0)),
        out_shape=jax.ShapeDtypeStruct((m // CMP_BLOCK, c), F32),
        compiler_params=_cparams("parallel"), name=name,
    )(kvc, pe)


def _compress_pages(pages_t, page_table, pe_t, big_t, bsum, layer_base, *, pps, name):
    bd, n_pages = page_table.shape
    bpp = PAGE_SIZE // CMP_BLOCK
    nbl = bpp * pps
    nsteps = n_pages // pps

    def body(pt_ref, *refs):
        page_refs = refs[:pps]
        pe_ref, big_ref, bsum_ref, o_ref = refs[pps:]
        acc = jnp.zeros((KVW, nbl), F32)
        for p in range(pps):
            x = page_refs[p][...] * pe_ref[...]
            hi = x.astype(BF16)
            lo = (x - hi.astype(F32)).astype(BF16)
            e = bsum_ref[p]
            acc = acc + jnp.dot(hi, e, preferred_element_type=F32) + jnp.dot(lo, e, preferred_element_type=F32)
        o_ref[...] = jnp.dot(big_ref[...], acc.astype(BF16), preferred_element_type=F32)

    def page_map(p):
        return lambda b, j, pt: (layer_base + pt[b, j * pps + p], 0, 0)

    cst2 = lambda b, j, pt: (0, 0)
    grid_spec = pltpu.PrefetchScalarGridSpec(
        num_scalar_prefetch=1, grid=(bd, nsteps),
        in_specs=[pl.BlockSpec((None, KVW, PAGE_SIZE), page_map(p)) for p in range(pps)]
        + [pl.BlockSpec((KVW, PAGE_SIZE), cst2), pl.BlockSpec((KVP, KVW), cst2),
           pl.BlockSpec((pps, PAGE_SIZE, nbl), lambda b, j, pt: (0, 0, 0))],
        out_specs=pl.BlockSpec((None, KVP, nbl), lambda b, j, pt: (b, 0, j)))
    return pl.pallas_call(
        body, grid_spec=grid_spec,
        out_shape=jax.ShapeDtypeStruct((bd, KVP, n_pages * bpp), F32),
        compiler_params=_cparams("parallel", "arbitrary"), name=name,
    )(page_table, *([pages_t] * pps), pe_t, big_t, bsum)


def _flash_step_t(q, k, vt, mask, m_ref, l_ref, acc_ref):
    s = jnp.dot(k, q, preferred_element_type=F32) * SCALE
    s = jnp.where(mask, s, NEG)
    m_prev = m_ref[...]
    m_new = jnp.maximum(m_prev, jnp.max(s, axis=0, keepdims=True))
    alpha = jnp.exp(m_prev - m_new)
    p = jnp.where(mask, jnp.exp(s - m_new), 0.0)
    l_ref[...] = alpha * l_ref[...] + jnp.sum(p, axis=0, keepdims=True)
    acc_ref[...] = alpha * acc_ref[...] + jnp.dot(vt, p.astype(BF16), preferred_element_type=F32)
    m_ref[...] = m_new


def _attn_body(qraw_ref, qrot_ref, gate_ref, ck_ref, cv_ref, ks_ref, vs_ref, kw_ref, vw_ref, o_ref,
               m_ref, l_ref, acc_ref, sel_ref, vts_ref, vtw_ref, *, tq, tk, seq, k_top):
    i = pl.program_id(2)
    cols = HPG * tq
    nsb = seq // SEL_BLOCK
    nt = seq // tk
    bpt = tk // SEL_BLOCK
    q0 = i * tq
    lane_q = lax.broadcasted_iota(jnp.int32, (1, cols), 1)
    qpos = q0 + (lane_q & (tq - 1))
    qp1 = q0 + lax.broadcasted_iota(jnp.int32, (1, tq), 1)

    @pl.when(i == 0)
    def _():
        def tr(t, c):
            for h in range(tk // LANES):
                k0 = pl.multiple_of(t * tk + h * LANES, LANES)
                hs = slice(h * LANES, (h + 1) * LANES)
                vts_ref[t, :, hs] = vs_ref[pl.ds(k0, LANES), :].astype(F32).T.astype(BF16)
                vtw_ref[t, :, hs] = vw_ref[pl.ds(k0, LANES), :].astype(F32).T.astype(BF16)
            return c
        lax.fori_loop(0, nt, tr, 0)

    def heads_t(ref):
        parts = [ref[:, j * HSLOT:(j + 1) * HSLOT].astype(F32).T for j in range(HPG)]
        return jnp.concatenate(parts, axis=1).astype(BF16)

    qr = heads_t(qraw_ref)
    ck = jnp.concatenate([ck_ref[pl.ds(0, nsb, stride=2), :], ck_ref[pl.ds(1, nsb, stride=2), :]], axis=0)
    s = jnp.dot(ck.astype(BF16), qr, preferred_element_type=F32) * SCALE
    r = lax.broadcasted_iota(jnp.int32, (2 * nsb, 1), 0)
    n_of = jnp.where(r < nsb, 2 * r, 2 * (r - nsb) + 1)
    mk = ((n_of + 1) * CMP_BLOCK - 1) <= qpos
    s = jnp.where(mk, s, NEG)
    mx = jnp.max(s, axis=0, keepdims=True)
    p = jnp.where(mk, jnp.exp(s - mx), 0.0)
    pn = p / jnp.maximum(jnp.sum(p, axis=0, keepdims=True), 1e-30)
    cv = jnp.concatenate([cv_ref[pl.ds(0, nsb, stride=2), :], cv_ref[pl.ds(1, nsb, stride=2), :]], axis=0)
    cvt = _pad_rows_to(cv, LANES).T.astype(BF16)
    o_cmp = jnp.dot(cvt, _pad_rows_to(pn, LANES).astype(BF16), preferred_element_type=F32)
    pp = pn[0:nsb] + pn[nsb:2 * nsb]
    imp = pp[:, 0:tq]
    for j in range(1, HPG):
        imp = imp + pp[:, j * tq:(j + 1) * tq]

    sb = lax.broadcasted_iota(jnp.int32, (nsb, 1), 0)
    vis = (sb * SEL_BLOCK) <= qp1
    cur = sb == (qp1 >> (SEL_BLOCK.bit_length() - 1))
    imp = jnp.where(cur, FORCE, jnp.where(vis, imp, -FORCE))
    cnt = jnp.zeros((nsb, tq), F32)
    for j in range(nsb):
        rowj = imp[j:j + 1, :]
        beats = (rowj > imp) | ((rowj == imp) & (j < sb))
        cnt = cnt + jnp.where(beats, 1.0, 0.0)
    sel = jnp.where(cnt < k_top, 1.0, 0.0)
    for t in range(nt):
        sel_ref[t, 0:bpt, :] = sel[bpt * t:bpt * (t + 1), :]

    qt = heads_t(qrot_ref)
    rowk = lax.broadcasted_iota(jnp.int32, (tk, 1), 0)

    def reset():
        m_ref[...] = jnp.full((1, cols), NEG, F32)
        l_ref[...] = jnp.zeros((1, cols), F32)
        acc_ref[...] = jnp.zeros((HSLOT, cols), F32)

    def result():
        return acc_ref[...] / jnp.maximum(l_ref[...], 1e-30)

    reset()

    def sel_step(t, c):
        k0 = pl.multiple_of(t * tk, tk)
        kp = k0 + rowk
        sm = sel_ref[t, bpt - 1:bpt, :]
        for j in range(bpt - 2, -1, -1):
            sm = jnp.where(rowk < (j + 1) * SEL_BLOCK, sel_ref[t, j:j + 1, :], sm)
        sm = jnp.concatenate([sm] * HPG, axis=1) > 0.5
        mask = (kp <= qpos) & sm
        _flash_step_t(qt, ks_ref[pl.ds(k0, tk), :], vts_ref[t], mask, m_ref, l_ref, acc_ref)
        return c

    t_end = lax.div(q0 + tq + tk - 1, tk)
    lax.fori_loop(0, t_end, sel_step, 0)
    o_sel = result()

    reset()

    def win_step(t, c):
        k0 = pl.multiple_of(t * tk, tk)
        kp = k0 + rowk
        mask = (kp <= qpos) & (kp >= qpos - WINDOW)
        _flash_step_t(qt, kw_ref[pl.ds(k0, tk), :], vtw_ref[t], mask, m_ref, l_ref, acc_ref)
        return c

    lax.fori_loop(lax.div(jnp.maximum(q0 - WINDOW, 0), tk), t_end, win_step, 0)
    o_win = result()

    gt = gate_ref[...].T
    o = jnp.zeros((HSLOT, cols), F32)
    for br, o_br in enumerate((o_cmp, o_sel, o_win)):
        grow = jnp.concatenate([gt[br * HPG + j:br * HPG + j + 1, :] for j in range(HPG)], axis=1)
        o = o + grow * o_br
    for j in range(HPG):
        o_ref[:, j * HSLOT:(j + 1) * HSLOT] = o[:, j * tq:(j + 1) * tq].T.astype(BF16)


def _prompt_attention(qraw, qrot, gates, ckv, kvs_b, kvw_b, *, batch, seq, tq, tk):
    nq = seq // tq
    ncb = seq // CMP_BLOCK
    nt = seq // tk
    cols = HPG * tq
    k_top = min(TOP_K, seq // SEL_BLOCK)
    assert tq & (tq - 1) == 0 and seq % tk == 0 and tk % LANES == 0 and tk // SEL_BLOCK <= SUBLANES
    body = functools.partial(_attn_body, tq=tq, tk=tk, seq=seq, k_top=k_top)
    gw = HPG * HSLOT
    kmap = lambda b, g, i: (b, g)
    vmap_ = lambda b, g, i: (b, KV_HEADS + g)
    return pl.pallas_call(
        body, grid=(batch, KV_HEADS, nq),
        in_specs=[pl.BlockSpec((tq, gw), lambda b, g, i: (b * nq + i, g)),
                  pl.BlockSpec((tq, gw), lambda b, g, i: (b * nq + i, g)),
                  pl.BlockSpec((tq, LANES), lambda b, g, i: (b * nq + i, g)),
                  pl.BlockSpec((ncb, HSLOT), kmap), pl.BlockSpec((ncb, HSLOT), vmap_),
                  pl.BlockSpec((seq, HSLOT), kmap), pl.BlockSpec((seq, HSLOT), vmap_),
                  pl.BlockSpec((seq, HSLOT), kmap), pl.BlockSpec((seq, HSLOT), vmap_)],
        out_specs=pl.BlockSpec((tq, gw), lambda b, g, i: (b * nq + i, g)),
        out_shape=jax.ShapeDtypeStruct((batch * seq, QW), BF16),
        scratch_shapes=[pltpu.VMEM((1, cols), F32), pltpu.VMEM((1, cols), F32), pltpu.VMEM((HSLOT, cols), F32),
                        pltpu.VMEM((nt, SUBLANES, tq), F32),
                        pltpu.VMEM((nt, HSLOT, tk), BF16), pltpu.VMEM((nt, HSLOT, tk), BF16)],
        compiler_params=_cparams("parallel", "parallel", "arbitrary"), name="prompt_attention",
    )(qraw, qrot, gates, ckv, ckv, kvs_b, kvs_b, kvw_b, kvw_b)


CONV_HALO = 32


def _conv_body(cur_ref, prev_ref, w_ref, b_ref, g_ref, bn_ref, o_ref, ext_ref, *, ts):
    c = pl.program_id(1)
    ext_ref[0:CONV_HALO, :] = jnp.where(c > 0, prev_ref[...], 0.0)
    ext_ref[CONV_HALO:CONV_HALO + ts, :] = cur_ref[...]
    acc = jnp.zeros((ts, CONV_CH), F32) + b_ref[...]
    off = CONV_HALO - (CONV_W - 1)
    for k in range(CONV_W):
        acc = acc + ext_ref[pl.ds(off + k, ts), :] * w_ref[k:k + 1, :]
    y = _ln_rows(acc, g_ref[...], bn_ref[...])
    o_ref[...] = (y * jax.nn.sigmoid(y)).astype(BF16)


def _prompt_conv(u, cw, cb, cg, cbn, *, batch, seq, ts):
    nt = seq // ts
    r = ts // CONV_HALO
    cwp = jnp.pad(cw, ((0, CONV_HALO - CONV_W), (0, 0)))
    vec = lambda a: a.reshape(1, CONV_CH)
    cst = lambda b, c: (0, 0)
    return pl.pallas_call(
        functools.partial(_conv_body, ts=ts), grid=(batch, nt),
        in_specs=[pl.BlockSpec((ts, CONV_CH), lambda b, c: (b * nt + c, 0)),
                  pl.BlockSpec((CONV_HALO, CONV_CH), lambda b, c: (jnp.maximum((b * nt + c) * r - 1, 0), 0)),
                  pl.BlockSpec((CONV_HALO, CONV_CH), cst),
                  pl.BlockSpec((1, CONV_CH), cst), pl.BlockSpec((1, CONV_CH), cst), pl.BlockSpec((1, CONV_CH), cst)],
        out_specs=pl.BlockSpec((ts, CONV_CH), lambda b, c: (b * nt + c, 0)),
        out_shape=jax.ShapeDtypeStruct((batch * seq, CONV_CH), BF16),
        scratch_shapes=[pltpu.VMEM((CONV_HALO + ts, CONV_CH), F32)],
        compiler_params=_cparams("parallel", "arbitrary"), name="prompt_conv",
    )(u, u, cwp, vec(cb), vec(cg), vec(cbn))


POOL_HALO = 16


def _odd_mix_body(pin_ref, prev_ref, u_ref, vn_ref, pw_ref, ps_ref, sw_ref, sb_ref, o_ref, ext_ref):
    c = pl.program_id(1)
    ext_ref[0:POOL_HALO, :] = jnp.where(c > 0, prev_ref[...], 0.0)
    ext_ref[POOL_HALO:POOL_HALO + CHUNK, :] = pin_ref[...]
    t = c * CHUNK + lax.broadcasted_iota(jnp.int32, (CHUNK, 1), 0)
    for g, w in enumerate(POOL_WINDOWS):
        sl = slice(g * POOL_GC, (g + 1) * POOL_GC)
        tot = ext_ref[pl.ds(POOL_HALO, CHUNK), sl]
        for j in range(1, w):
            tot = tot + ext_ref[pl.ds(POOL_HALO - j, CHUNK), sl]
        cnt = jnp.minimum(w, t + 1).astype(F32)
        d = tot / cnt - pin_ref[:, sl]
        y = jnp.dot(d.astype(BF16), pw_ref[g], preferred_element_type=F32)
        o_ref[:, sl] = (y * ps_ref[:, sl]).astype(BF16)
    ri = lax.broadcasted_iota(jnp.int32, (CHUNK, CHUNK), 0)
    ci = lax.broadcasted_iota(jnp.int32, (CHUNK, CHUNK), 1)
    for g in range(SGU_GROUPS):
        sl = slice(g * SGU_GC, (g + 1) * SGU_GC)
        ws = jnp.where(ci <= ri, sw_ref[g], 0.0).astype(BF16)
        mixed = jnp.dot(ws, vn_ref[:, sl].astype(BF16), preferred_element_type=F32) + sb_ref[:, g:g + 1]
        o_ref[:, POOL_CH + g * SGU_GC:POOL_CH + (g + 1) * SGU_GC] = (u_ref[:, sl] * mixed).astype(BF16)


def _prompt_odd_mix(pin, u, vn, pool_w, pool_scale, sgu_w, sgu_b, *, batch, seq):
    nt = seq // CHUNK
    r = CHUNK // POOL_HALO
    cst2 = lambda b, c: (0, 0)
    cst3 = lambda b, c: (0, 0, 0)
    row = lambda b, c: (b * nt + c, 0)
    return pl.pallas_call(
        _odd_mix_body, grid=(batch, nt),
        in_specs=[pl.BlockSpec((CHUNK, POOL_CH), row),
                  pl.BlockSpec((POOL_HALO, POOL_CH), lambda b, c: (jnp.maximum((b * nt + c) * r - 1, 0), 0)),
                  pl.BlockSpec((CHUNK, SGU_CH), row), pl.BlockSpec((CHUNK, SGU_CH), row),
                  pl.BlockSpec((POOL_GROUPS, POOL_GC, POOL_GC), cst3), pl.BlockSpec((1, POOL_CH), cst2),
                  pl.BlockSpec((SGU_GROUPS, CHUNK, CHUNK), cst3), pl.BlockSpec((CHUNK, SGU_GROUPS), cst2)],
        out_specs=pl.BlockSpec((CHUNK, D_MODEL), row),
        out_shape=jax.ShapeDtypeStruct((batch * seq, D_MODEL), BF16),
        scratch_shapes=[pltpu.VMEM((POOL_HALO + CHUNK, POOL_CH), F32)],
        compiler_params=_cparams("parallel", "arbitrary"), name="prompt_pool_sgu",
    )(pin, pin, u, vn, pool_w.astype(BF16), pool_scale.reshape(1, POOL_CH), sgu_w, sgu_b.T)


def _group_rows(nrows):
    return lax.broadcasted_iota(jnp.int32, (nrows, 1), 0) >> (HPG.bit_length() - 1)


def _sample_cmp_body(q_ref, ckv_ref, o_ref, idx_ref, *, qpos, ncb, nbl, k_past):
    q = q_ref[...].astype(BF16)
    rg = _group_rows(N_HEADS)
    half = nbl // 2
    assert nbl & (nbl - 1) == 0
    sh = nbl.bit_length() - 1
    lane = lax.broadcasted_iota(jnp.int32, (1, ncb), 1)
    grp, w = lane >> sh, lane & (nbl - 1)
    n_cmp = grp * nbl + 2 * (w & (half - 1)) + (w >> (sh - 1))
    mk = jnp.broadcast_to(((n_cmp + 1) * CMP_BLOCK - 1) <= qpos, (N_HEADS, ncb))
    s = jnp.zeros((N_HEADS, ncb), F32)
    for g in range(KV_HEADS):
        ck = ckv_ref[g * HSLOT:(g + 1) * HSLOT, :].astype(BF16)
        s = jnp.where(rg == g, jnp.dot(q, ck, preferred_element_type=F32) * SCALE, s)
    s = jnp.where(mk, s, NEG)
    mx = jnp.max(s, axis=-1, keepdims=True)
    p = jnp.where(mk, jnp.exp(s - mx), 0.0)
    pn = p / jnp.maximum(jnp.sum(p, axis=-1, keepdims=True), 1e-30)
    o = jnp.zeros((N_HEADS, HSLOT), F32)
    for g in range(KV_HEADS):
        cv = ckv_ref[(KV_HEADS + g) * HSLOT:(KV_HEADS + g + 1) * HSLOT, :].astype(BF16)
        o = jnp.where(rg == g, lax.dot_general(pn.astype(BF16), cv, _NT, preferred_element_type=F32), o)
    o_ref[...] = o
    pair = pn + pltpu.roll(pn, ncb - half, axis=1)
    valid = w < half
    sb = grp * half + w
    vis = (sb * SEL_BLOCK) <= qpos
    ri = lax.broadcasted_iota(jnp.int32, (ncb, ncb), 0)
    sb_r = (ri >> sh) * half + (ri & (nbl - 1))
    sb_c = jnp.broadcast_to(sb, (ncb, ncb))
    slot = lax.broadcasted_iota(jnp.int32, (TOP_K, 1), 0)
    for g in range(KV_HEADS):
        imp = jnp.sum(jnp.where(rg == g, pair, 0.0), axis=0, keepdims=True)
        imp = jnp.where(valid, jnp.where(vis, imp, -FORCE), -2.0 * FORCE)
        a = jnp.broadcast_to(imp, (ncb, ncb))
        bt = a.T
        beats = (bt > a) | ((bt == a) & (sb_r < sb_c))
        rank = jnp.sum(jnp.where(beats, 1.0, 0.0), axis=0, keepdims=True)
        onehot = jnp.where((rank == slot.astype(F32)) & valid, 1.0, 0.0)
        idx = jnp.sum(onehot * sb.astype(F32), axis=-1, keepdims=True)
        idx = jnp.where(slot < k_past, idx, 0.0)
        idx_ref[g * TOP_K:(g + 1) * TOP_K, :] = jnp.broadcast_to(idx, (TOP_K, LANES)).astype(jnp.int32)


def _sample_cmp(q3, ckv_t, *, bd, past, qpos, nbl):
    ncb = past // CMP_BLOCK
    k_past = min(TOP_K - 1, past // SEL_BLOCK)
    body = functools.partial(_sample_cmp_body, qpos=qpos, ncb=ncb, nbl=nbl, k_past=k_past)
    head3 = pl.BlockSpec((None, N_HEADS, HSLOT), lambda b: (b, 0, 0))
    return pl.pallas_call(
        body, grid=(bd,),
        in_specs=[head3, pl.BlockSpec((None, KVP, ncb), lambda b: (b, 0, 0))],
        out_specs=[head3, pl.BlockSpec((None, KV_HEADS * TOP_K, LANES), lambda b: (b, 0, 0))],
        out_shape=[jax.ShapeDtypeStruct((bd, N_HEADS, HSLOT), F32),
                   jax.ShapeDtypeStruct((bd, KV_HEADS * TOP_K, LANES), jnp.int32)],
        compiler_params=_cparams("parallel"), name="sample_cmp_attention",
    )(q3, ckv_t)


Q_PAD_ROWS = 8


def _pad_dt(x):
    return _pad_rows_to(x, HSLOT)


def _sample_sel_body(pt_ref, idx_ref, *refs, k_past):
    k_refs, v_refs = refs[:k_past], refs[k_past:2 * k_past]
    q_ref, knew_ref, vnew_ref, o_ref = refs[2 * k_past:]
    b, g = pl.program_id(0), pl.program_id(1)
    bpp = PAGE_SIZE // SEL_BLOCK
    q = q_ref[...].astype(BF16)
    half_of_lane = lax.broadcasted_iota(jnp.int32, (1, PAGE_SIZE), 1) // SEL_BLOCK
    s_parts, m_parts = [], []
    for s in range(k_past):
        kt = _pad_dt(k_refs[s][...]).astype(BF16)
        s_parts.append(jnp.dot(q, kt, preferred_element_type=F32) * SCALE)
        m_parts.append(jnp.broadcast_to(half_of_lane == idx_ref[b, g, s] % bpp, (Q_PAD_ROWS, PAGE_SIZE)))
    s_old = jnp.concatenate(s_parts, axis=1)
    mk = jnp.concatenate(m_parts, axis=1)
    s_old = jnp.where(mk, s_old, NEG)
    s_all = lax.dot_general(q, knew_ref[...].astype(BF16), _NT, preferred_element_type=F32) * SCALE
    lane = lax.broadcasted_iota(jnp.int32, s_all.shape, 1)
    s_new = jnp.sum(jnp.where(lane == b, s_all, 0.0), axis=-1, keepdims=True)
    mx = jnp.maximum(jnp.max(s_old, axis=-1, keepdims=True), s_new)
    p_old = jnp.where(mk, jnp.exp(s_old - mx), 0.0)
    p_new = jnp.exp(s_new - mx)
    den = jnp.maximum(jnp.sum(p_old, axis=-1, keepdims=True) + p_new, 1e-30)
    v_new = vnew_ref[pl.ds(b, 1), :].astype(BF16).astype(F32)
    o = p_new.astype(BF16).astype(F32) * v_new
    for s in range(k_past):
        vt = _pad_dt(v_refs[s][...]).astype(BF16)
        ps = p_old[:, s * PAGE_SIZE:(s + 1) * PAGE_SIZE].astype(BF16)
        o = o + lax.dot_general(ps, vt, _NT, preferred_element_type=F32)
    o_ref[...] = o / den


def _sample_sel(pages5, page_table, idx, layer_base, q4, kvs_new, *, bd, past):
    k_past = min(TOP_K - 1, past // SEL_BLOCK)
    bpp = PAGE_SIZE // SEL_BLOCK

    def blk_map(s, kv):
        def f(b, g, pt, ix):
            return (layer_base + pt[b, ix[b, g, s] // bpp], kv, g, 0, 0)
        return f

    tile = lambda s, kv: pl.BlockSpec((None, None, None, HEAD_DIM, PAGE_SIZE), blk_map(s, kv))
    grid_spec = pltpu.PrefetchScalarGridSpec(
        num_scalar_prefetch=2, grid=(bd, KV_HEADS),
        in_specs=[tile(s, 0) for s in range(k_past)] + [tile(s, 1) for s in range(k_past)]
        + [pl.BlockSpec((None, None, Q_PAD_ROWS, HSLOT), lambda b, g, pt, ix: (b, g, 0, 0)),
           pl.BlockSpec((SAMPLE_ROWS, HSLOT), lambda b, g, pt, ix: (0, g)),
           pl.BlockSpec((SAMPLE_ROWS, HSLOT), lambda b, g, pt, ix: (0, KV_HEADS + g))],
        out_specs=pl.BlockSpec((None, None, Q_PAD_ROWS, HSLOT), lambda b, g, pt, ix: (b, g, 0, 0)))
    return pl.pallas_call(
        functools.partial(_sample_sel_body, k_past=k_past), grid_spec=grid_spec,
        out_shape=jax.ShapeDtypeStruct((bd, KV_HEADS, Q_PAD_ROWS, HSLOT), F32),
        compiler_params=_cparams("parallel", "arbitrary"), name="sample_sel_attention",
    )(page_table, idx, *([pages5] * (2 * k_past)), q4, kvs_new, kvs_new)


def _sample_win_body(q_ref, win_ref, new_ref, ocmp_ref, osel_ref, gate_ref, o_ref, *, qpos, past, wb):
    b = pl.program_id(0)
    q = q_ref[...].astype(BF16)
    qf = q.astype(F32)
    rg = _group_rows(N_HEADS)
    new = new_ref[pl.ds(b, 1), :].astype(BF16).astype(F32)
    kpos = (past - wb) + lax.broadcasted_iota(jnp.int32, (1, wb), 1)
    mk = jnp.broadcast_to((kpos <= qpos) & (kpos >= qpos - WINDOW), (N_HEADS, wb))
    s_old = jnp.zeros((N_HEADS, wb), F32)
    s_new = jnp.zeros((N_HEADS, 1), F32)
    for g in range(KV_HEADS):
        kt = _pad_dt(win_ref[g]).astype(BF16)
        s_old = jnp.where(rg == g, jnp.dot(q, kt, preferred_element_type=F32) * SCALE, s_old)
        sn = jnp.sum(qf * new[:, g * HSLOT:(g + 1) * HSLOT], axis=-1, keepdims=True) * SCALE
        s_new = jnp.where(rg == g, sn, s_new)
    s_old = jnp.where(mk, s_old, NEG)
    mx = jnp.maximum(jnp.max(s_old, axis=-1, keepdims=True), s_new)
    p_old = jnp.where(mk, jnp.exp(s_old - mx), 0.0)
    p_new = jnp.exp(s_new - mx)
    den = jnp.maximum(jnp.sum(p_old, axis=-1, keepdims=True) + p_new, 1e-30)
    o_win = jnp.zeros((N_HEADS, HSLOT), F32)
    for g in range(KV_HEADS):
        vt = _pad_dt(win_ref[KV_HEADS + g]).astype(BF16)
        og = lax.dot_general(p_old.astype(BF16), vt, _NT, preferred_element_type=F32)
        og = og + p_new.astype(BF16).astype(F32) * new[:, (KV_HEADS + g) * HSLOT:(KV_HEADS + g + 1) * HSLOT]
        o_win = jnp.where(rg == g, og, o_win)
    o_win = o_win / den
    gts = gate_ref[...]
    o_ref[...] = gts[:, 0:1] * ocmp_ref[...] + gts[:, 1:2] * osel_ref[...] + gts[:, 2:3] * o_win


def _sample_win(q3, win4, layer, kvw_new, o_cmp, o_sel, gates3, *, bd, past, qpos):
    wb = win4.shape[-1]
    head3 = pl.BlockSpec((None, N_HEADS, HSLOT), lambda b: (b, 0, 0))
    return pl.pallas_call(
        functools.partial(_sample_win_body, qpos=qpos, past=past, wb=wb), grid=(bd,),
        in_specs=[head3, pl.BlockSpec((None, 2 * KV_HEADS, HEAD_DIM, wb), lambda b: (layer * bd + b, 0, 0, 0)),
                  pl.BlockSpec((SAMPLE_ROWS, KVP), lambda b: (0, 0)), head3, head3, head3],
        out_specs=head3,
        out_shape=jax.ShapeDtypeStruct((bd, N_HEADS, HSLOT), F32),
        compiler_params=_cparams("parallel"), name="sample_win_attention",
    )(q3, win4, kvw_new, o_cmp, o_sel, gates3)


def _sample_conv_body(st_ref, u_ref, w_ref, b_ref, g_ref, bn_ref, o_ref, *, bd):
    w = w_ref[...]
    y = jnp.sum(st_ref[...] * w[None, :CONV_W - 1, :], axis=1) + u_ref[0:bd, :] * w[CONV_W - 1:CONV_W, :] + b_ref[...]
    y = _ln_rows(y, g_ref[...], bn_ref[...])
    o_ref[...] = y * jax.nn.sigmoid(y)


def _sample_conv(state, layer, u, cw, cb, cg, cbn, *, bd):
    vec = lambda a: a.reshape(1, CONV_CH)
    cst = lambda i: (0, 0)
    return pl.pallas_call(
        functools.partial(_sample_conv_body, bd=bd), grid=(1,),
        in_specs=[pl.BlockSpec((None, bd, CONV_W - 1, CONV_CH), lambda i: (layer, 0, 0, 0)),
                  pl.BlockSpec((SAMPLE_ROWS, CONV_CH), cst), pl.BlockSpec((CONV_W, CONV_CH), cst),
                  pl.BlockSpec((1, CONV_CH), cst), pl.BlockSpec((1, CONV_CH), cst), pl.BlockSpec((1, CONV_CH), cst)],
        out_specs=pl.BlockSpec((bd, CONV_CH), cst),
        out_shape=jax.ShapeDtypeStruct((bd, CONV_CH), F32),
        compiler_params=_cparams("arbitrary"), name="sample_conv",
    )(state, u, cw, vec(cb), vec(cg), vec(cbn))


def _sample_odd_body(st_ref, pin_ref, u_ref, vn_ref, pw_ref, ps_ref, w0_ref, b0_ref, o_ref, *, bd, start_pos):
    pin = pin_ref[0:bd, :]
    st = st_ref[...]
    for g, w in enumerate(POOL_WINDOWS):
        sl = slice(g * POOL_GC, (g + 1) * POOL_GC)
        tot = pin[:, sl] + jnp.sum(st[:, POOL_STATE - (w - 1):, sl], axis=1)
        d = tot / float(min(w, start_pos + 1)) - pin[:, sl]
        dp = jnp.concatenate([d, jnp.zeros((SAMPLE_ROWS - bd, POOL_GC), F32)], axis=0).astype(BF16)
        y = jnp.dot(dp, pw_ref[g], preferred_element_type=F32)[0:bd]
        o_ref[:, sl] = y * ps_ref[:, sl]
    mixed = w0_ref[...] * vn_ref[0:bd, :] + b0_ref[...]
    o_ref[:, POOL_CH:] = u_ref[0:bd, :] * mixed


def _sample_odd_mix(state, layer, pin, u, vn, pool_w, pool_scale, sgu_w, sgu_b, *, bd, start_pos):
    w0 = jnp.repeat(sgu_w[:, 0, 0], SGU_GC).reshape(1, SGU_CH)
    b0 = jnp.repeat(sgu_b[:, 0], SGU_GC).reshape(1, SGU_CH)
    cst = lambda i: (0, 0)
    return pl.pallas_call(
        functools.partial(_sample_odd_body, bd=bd, start_pos=start_pos), grid=(1,),
        in_specs=[pl.BlockSpec((None, bd, POOL_STATE, POOL_CH), lambda i: (layer, 0, 0, 0)),
                  pl.BlockSpec((SAMPLE_ROWS, POOL_CH), cst), pl.BlockSpec((SAMPLE_ROWS, SGU_CH), cst),
                  pl.BlockSpec((SAMPLE_ROWS, SGU_CH), cst),
                  pl.BlockSpec((POOL_GROUPS, POOL_GC, POOL_GC), lambda i: (0, 0, 0)),
                  pl.BlockSpec((1, POOL_CH), cst), pl.BlockSpec((1, SGU_CH), cst), pl.BlockSpec((1, SGU_CH), cst)],
        out_specs=pl.BlockSpec((bd, D_MODEL), cst),
        out_shape=jax.ShapeDtypeStruct((bd, D_MODEL), F32),
        compiler_params=_cparams("arbitrary"), name="sample_pool_sgu",
    )(state, pin, u, vn, pool_w.astype(BF16), pool_scale.reshape(1, POOL_CH), w0, b0)


def _pad_rows(x, rows):
    return jnp.pad(x, ((0, rows - x.shape[0]), (0, 0)))


def _split_to_nat(x, nheads):
    y = _unpad_heads(x, nheads)
    return _pad_nat(y.reshape(y.shape[:-1] + (nheads, HEAD_DIM))).reshape(x.shape)


def _nat_to_split(x, nheads):
    xh = x.reshape(x.shape[:-1] + (nheads, HSLOT))[..., :HEAD_DIM]
    return _pad_head(xh).reshape(x.shape)


def _even_in_proj(xb, w_in, cos, sin, *, tm):
    nrep = cos.shape[0] // tm
    tab = lambda i, j: (i % nrep, 0)
    rope_ex = ((cos, (tm, LANES), tab), (sin, (tm, LANES), tab))
    gw = HPG * HSLOT
    qraw, qrot = _mm(xb, w_in, tm=tm, tn=gw, n_off=E_Q, n_cols=QW, epilogue=_ep_q, extras=rope_ex,
                     outs=((QW, BF16, gw), (QW, BF16, gw)), name="even_in_q")
    (kvc,) = _mm(xb, w_in, tm=tm, tn=KVP, n_off=E_KVC, n_cols=KVP, epilogue=functools.partial(_ep_kv, rope=False),
                 outs=((KVP, F32, KVP),), name="even_in_kvc")
    kvs, kvs_b = _mm(xb, w_in, tm=tm, tn=KVP, n_off=E_KVS, n_cols=KVP, epilogue=functools.partial(_ep_kv, rope=True),
                     extras=rope_ex, outs=((KVP, F32, KVP), (KVP, BF16, KVP)), name="even_in_kvs")
    kvw, kvw_b = _mm(xb, w_in, tm=tm, tn=KVP, n_off=E_KVW, n_cols=KVP, epilogue=functools.partial(_ep_kv, rope=True),
                     extras=rope_ex, outs=((KVP, F32, KVP), (KVP, BF16, KVP)), name="even_in_kvw")
    (u,) = _mm(xb, w_in, tm=tm, tn=CONV_CH, n_off=E_GLU, n_cols=2 * CONV_CH, epilogue=_ep_glu,
               outs=((CONV_CH, F32, CONV_CH // 2),), name="even_in_glu")
    (gates,) = _mm(xb, w_in, tm=tm, tn=GATE_W, n_off=E_GATE, n_cols=GATE_W, epilogue=_ep_sigmoid,
                   outs=((GATE_W, F32, GATE_W),), name="even_in_gates")
    return qraw, qrot, kvc, kvs, kvs_b, kvw, kvw_b, u, gates


def _mlp(x, xb, w1, w2, g, b, *, tm1, tn1, tm2, tk2):
    (h,) = _mm(xb, w1, tm=tm1, tn=tn1, n_off=0, n_cols=w1.shape[1], epilogue=_ep_relu2,
               outs=((w1.shape[1], BF16, tn1),), name="mlp_up")
    return _mlp2_ln(h, w2, x, g, b, tm=tm2, tk=tk2, name="mlp_down_ln")


def kernel(x_prompt, x_sample, cache_cmp_kv, cache_sel_kv, cache_win_kv, state_conv, state_pool, page_table,
           w_in_even, w_out_even, cmp_pe_k, cmp_pe_v, cmp_w_k, cmp_w_v, conv_w, conv_b, conv_ln_g, conv_ln_b,
           w_in_odd, w_out_odd, pool_w, pool_scale, sgu_ln_g, sgu_ln_b, sgu_w, sgu_b,
           mlp_w1, mlp_w2, ln_mix_g, ln_mix_b, ln_ffn_g, ln_ffn_b):
    B, S, D = x_prompt.shape
    Bd, Sd, _ = x_sample.shape
    n_pages = page_table.shape[1]
    past = n_pages * PAGE_SIZE
    n_even, n_pool = cache_cmp_kv.shape[:2]
    wb = cache_win_kv.shape[2]
    assert D == D_MODEL and Sd == 1 and Bd <= SAMPLE_ROWS
    assert S % 1024 == 0 and past % SEL_BLOCK == 0 and S >= WINDOW
    M = B * S
    Ms = SAMPLE_ROWS
    tm_p = 1024

    cos_p, sin_p = _rope_tables(jnp.arange(S, dtype=jnp.int32))
    cos_s, sin_s = _rope_tables(jnp.full((Ms,), past, jnp.int32))
    pps = min(32, n_pages)
    assert n_pages % pps == 0
    bsum = _block_sum_matrices(pps)
    cmp_t = cache_cmp_kv.transpose(0, 1, 3, 4, 5, 2).reshape(n_even * n_pool, KVW, PAGE_SIZE)
    sel_t = cache_sel_kv.transpose(0, 1, 3, 4, 5, 2).reshape(n_even * n_pool, 2, KV_HEADS, HEAD_DIM, PAGE_SIZE)
    win_t = cache_win_kv.transpose(0, 1, 3, 4, 5, 2).reshape(n_even * Bd, 2 * KV_HEADS, HEAD_DIM, wb)

    xp = x_prompt.reshape(M, D)
    xs = _pad_rows(x_sample.reshape(Bd, D), Ms)
    xpb, xsb = xp.astype(BF16), xs.astype(BF16)

    outs = {k: [] for k in ("cmp_p", "cmp_s", "sel_p", "sel_s", "win_p", "win_s", "conv_p", "conv_s",
                            "pool_p", "pool_s", "sgu_p", "sgu_s")}
    kv6 = lambda a, lead: a.reshape(lead + (2, KV_HEADS, HEAD_DIM))

    for layer in range(DEPTH):
        if layer % 2 == 0:
            e = layer // 2
            wts = _prep_even_weights(w_in_even[e], w_out_even[e], cmp_pe_k[e], cmp_pe_v[e], cmp_w_k[e], cmp_w_v[e])
            qraw, qrot, kvc, kvs, kvs_b, kvw, kvw_b, u, gates = _even_in_proj(xpb, wts["w_in"], cos_p, sin_p, tm=tm_p)
            summ = _compress_rows(kvc, wts["pe"], rows=512, name="prompt_compress")
            (ckv,) = _mm(summ.astype(BF16), wts["big_p"], tm=min(summ.shape[0], 512), tn=KVP,
                         n_off=0, n_cols=KVP, epilogue=_ep_plain, outs=((KVP, F32, KVP),), name="prompt_compress_map")
            o_att = _prompt_attention(qraw, qrot, gates, ckv, kvs_b, kvw_b, batch=B, seq=S, tq=256, tk=256)
            c = _prompt_conv(u, conv_w[e], conv_b[e], conv_ln_g[e], conv_ln_b[e], batch=B, seq=S, ts=256)
            xp, xpb = _proj_ln([o_att, c], [wts["wo_att"], wts["wo_conv"]], xp, ln_mix_g[layer], ln_mix_b[layer],
                               tm=512, name="even_out_ln")
            outs["cmp_p"].append(kv6(_unpad_heads(kvc, 2 * KV_HEADS), (B, S)))
            outs["sel_p"].append(kv6(_unpad_heads(kvs, 2 * KV_HEADS), (B, S)))
            outs["win_p"].append(kv6(_unpad_heads(kvw.reshape(B, S, KVP)[:, S - WINDOW:], 2 * KV_HEADS), (B, WINDOW)))
            outs["conv_p"].append(u.reshape(B, S, CONV_CH)[:, S - (CONV_W - 1):])
            qraw_s, qrot_s, kvc_s, kvs_s, _, kvw_s, _, u_s, gates_s = _even_in_proj(xsb, wts["w_in"], cos_s, sin_s, tm=Ms)
            ckv_t = _compress_pages(cmp_t, page_table, wts["pe_t"], wts["big_t"], bsum, e * n_pool, pps=pps,
                                    name="sample_compress")
            q3 = qraw_s.astype(F32)[:Bd].reshape(Bd, N_HEADS, HSLOT)
            o_cmp, idx = _sample_cmp(q3, ckv_t, bd=Bd, past=past, qpos=past, nbl=pps * (PAGE_SIZE // CMP_BLOCK))
            idx = idx[:, :, 0].reshape(Bd, KV_HEADS, TOP_K)
            qr3 = _split_to_nat(qrot_s.astype(F32)[:Bd], N_HEADS).reshape(Bd, N_HEADS, HSLOT)
            q4 = jnp.pad(qr3.reshape(Bd, KV_HEADS, HPG, HSLOT), ((0, 0), (0, 0), (0, Q_PAD_ROWS - HPG), (0, 0)))
            kvs_nat = _split_to_nat(kvs_s, 2 * KV_HEADS)
            kvw_nat = _split_to_nat(kvw_s, 2 * KV_HEADS)
            o_sel = _sample_sel(sel_t, page_table, idx, e * n_pool, q4, kvs_nat, bd=Bd, past=past)
            o_sel = o_sel[:, :, :HPG].reshape(Bd, N_HEADS, HSLOT)
            g3 = gates_s[:Bd].reshape(Bd, KV_HEADS, LANES)[:, :, :3 * HPG].reshape(Bd, KV_HEADS, 3, HPG)
            g3 = g3.transpose(0, 1, 3, 2).reshape(Bd, N_HEADS, 3)
            g3 = jnp.pad(g3, ((0, 0), (0, 0), (0, LANES - 3)))
            o_s = _sample_win(qr3, win_t, e, kvw_nat, o_cmp, o_sel, g3, bd=Bd, past=past, qpos=past)
            c_s = _sample_conv(state_conv, e, u_s, conv_w[e], conv_b[e], conv_ln_g[e], conv_ln_b[e], bd=Bd)
            o_sb = _pad_rows(_nat_to_split(o_s.reshape(Bd, QW), N_HEADS), Ms).astype(BF16)
            c_sb = _pad_rows(c_s, Ms).astype(BF16)
            xs, xsb = _proj_ln([o_sb, c_sb], [wts["wo_att"], wts["wo_conv"]], xs, ln_mix_g[layer], ln_mix_b[layer],
                               tm=Ms, name="even_out_ln_s")
            kvc_c = _unpad_heads(kvc_s[:Bd], 2 * KV_HEADS)
            kvs_c = _unpad_heads(kvs_s[:Bd], 2 * KV_HEADS)
            kvw_c = _unpad_heads(kvw_s[:Bd], 2 * KV_HEADS)
            outs["cmp_s"].append(kv6(kvc_c, (Bd, 1)))
            outs["sel_s"].append(kv6(kvs_c, (Bd, 1)))
            wkv = jnp.concatenate([cache_win_kv[e], kv6(kvw_c, (Bd, 1))], axis=1)
            outs["win_s"].append(wkv[:, wkv.shape[1] - min(WINDOW, wkv.shape[1]):])
            outs["conv_s"].append(jnp.concatenate([state_conv[e], u_s[:Bd, None, :]], axis=1)[:, 1:])
        else:
            o = layer // 2
            w_in = w_in_odd[o]
            w_in_p = jnp.concatenate([w_in[:, POOL_CH + SGU_CH:], w_in[:, :POOL_CH], w_in[:, POOL_CH:POOL_CH + SGU_CH]],
                                     axis=1).astype(BF16)
            w_out_p = w_out_odd[o].astype(BF16)
            lg, lb = sgu_ln_g[o].reshape(1, SGU_CH), sgu_ln_b[o].reshape(1, SGU_CH)

            def odd_in(xb, tm):
                gl_ex = ((lg, (1, 2 * SGU_GC), lambda i, j: (0, j)), (lb, (1, 2 * SGU_GC), lambda i, j: (0, j)))
                (vn,) = _mm(xb, w_in_p, tm=tm, tn=2 * SGU_GC, n_off=O_V, n_cols=SGU_CH, epilogue=_ep_gelu_gln,
                            extras=gl_ex, outs=((SGU_CH, F32, 2 * SGU_GC),), name="odd_in_v")
                (pin,) = _mm(xb, w_in_p, tm=tm, tn=POOL_CH, n_off=O_PIN, n_cols=POOL_CH, epilogue=_ep_plain,
                             outs=((POOL_CH, F32, POOL_CH),), name="odd_in_pool")
                (uu,) = _mm(xb, w_in_p, tm=tm, tn=POOL_CH, n_off=O_U, n_cols=SGU_CH, epilogue=_ep_gelu,
                            outs=((SGU_CH, F32, POOL_CH),), name="odd_in_u")
                return vn, pin, uu

            vn, pin, uu = odd_in(xpb, tm_p)
            cat = _prompt_odd_mix(pin, uu, vn, pool_w[o], pool_scale[o], sgu_w[o], sgu_b[o], batch=B, seq=S)
            xp, xpb = _proj_ln([cat], [w_out_p], xp, ln_mix_g[layer], ln_mix_b[layer], tm=512, name="odd_out_ln")
            outs["pool_p"].append(pin.reshape(B, S, POOL_CH)[:, S - POOL_STATE:])
            outs["sgu_p"].append(vn.reshape(B, S, SGU_CH)[:, ((S - 1) // CHUNK) * CHUNK:])
            vn_s, pin_s, uu_s = odd_in(xsb, Ms)
            cat_s = _sample_odd_mix(state_pool, o, pin_s, uu_s, vn_s, pool_w[o], pool_scale[o], sgu_w[o], sgu_b[o],
                                    bd=Bd, start_pos=past)
            xs, xsb = _proj_ln([_pad_rows(cat_s, Ms).astype(BF16)], [w_out_p], xs, ln_mix_g[layer], ln_mix_b[layer],
                               tm=Ms, name="odd_out_ln_s")
            outs["pool_s"].append(jnp.concatenate([state_pool[o], pin_s[:Bd, None, :]], axis=1)[:, 1:])
            outs["sgu_s"].append(vn_s[:Bd, None, :])
        w1b, w2b = mlp_w1[layer].astype(BF16), mlp_w2[layer].astype(BF16)
        xp, xpb = _mlp(xp, xpb, w1b, w2b, ln_ffn_g[layer], ln_ffn_b[layer], tm1=tm_p, tn1=1024, tm2=512, tk2=1024)
        xs, xsb = _mlp(xs, xsb, w1b, w2b, ln_ffn_g[layer], ln_ffn_b[layer], tm1=Ms, tn1=2048, tm2=Ms, tk2=2048)

    st = lambda k: jnp.stack(outs[k])
    return (xp.reshape(B, S, D), xs[:Bd].reshape(Bd, Sd, D),
            st("cmp_p"), st("cmp_s"), st("sel_p"), st("sel_s"), st("win_p"), st("win_s"),
            st("conv_p"), st("conv_s"), st("pool_p"), st("pool_s"), st("sgu_p"), st("sgu_s"))
```

```python
import functools

import jax
import jax.numpy as jnp
from jax import lax
from jax.experimental import pallas as pl
from jax.experimental.pallas import tpu as pltpu

F32 = jnp.float32
BF16 = jnp.bfloat16

D_MODEL = 2048
DEPTH = 4
PAGE_SIZE = 128
N_HEADS = 16
HEAD_DIM = 96
KV_HEADS = 4
HPG = N_HEADS // KV_HEADS
ATT_W = N_HEADS * HEAD_DIM
KVW = 2 * KV_HEADS * HEAD_DIM
CMP_BLOCK = 32
SEL_BLOCK = 64
TOP_K = 16
WINDOW = 512
ROPE_THETA = 10000.0
SCALE = HEAD_DIM ** -0.5
LOG2E = 1.4426950408889634
FORCE = 1e9
NEG = -1e30
CONV_CH = D_MODEL // 4
CONV_W = 31
POOL_CH = D_MODEL // 4
POOL_WINDOWS = (2, 4, 8, 16)
POOL_GROUPS = len(POOL_WINDOWS)
POOL_GC = POOL_CH // POOL_GROUPS
POOL_STATE = max(POOL_WINDOWS) - 1
SGU_CH = D_MODEL - POOL_CH
SGU_GROUPS = 4
SGU_GC = SGU_CH // SGU_GROUPS
CHUNK = 128
D_FF = 4 * D_MODEL
ALPHA = (2 * DEPTH) ** 0.25
LN_EPS = 1e-5

LANES = 128
SUBLANES = 8
HALF = HEAD_DIM // 2
HSLOT = LANES
HALF_OFF = LANES // 2
QW = N_HEADS * HSLOT
KVP = 2 * KV_HEADS * HSLOT
GATE_W = KV_HEADS * LANES
SAMPLE_ROWS = 16
VMEM_LIMIT = 52 * 1024 * 1024

E_Q, E_KVC, E_KVS, E_KVW = 0, QW, QW + KVP, QW + 2 * KVP
E_GLU = QW + 3 * KVP
E_GATE = E_GLU + 2 * CONV_CH
E_TOT = E_GATE + GATE_W
O_V, O_PIN, O_U = 0, SGU_CH, SGU_CH + POOL_CH


def _cparams(*sem):
    return pltpu.CompilerParams(dimension_semantics=sem, vmem_limit_bytes=VMEM_LIMIT)


def _pad_head(x):
    z = jnp.zeros(x.shape[:-1] + (HALF_OFF - HALF,), x.dtype)
    return jnp.concatenate([x[..., :HALF], z, x[..., HALF:], z], axis=-1)


def _pad_nat(x):
    return jnp.concatenate([x, jnp.zeros(x.shape[:-1] + (HSLOT - HEAD_DIM,), x.dtype)], axis=-1)


def _unpad_heads(x, nheads):
    xh = x.reshape(x.shape[:-1] + (nheads, HSLOT))
    y = jnp.concatenate([xh[..., :HALF], xh[..., HALF_OFF:HALF_OFF + HALF]], axis=-1)
    return y.reshape(x.shape[:-1] + (nheads * HEAD_DIM,))


def _rope_tables(pos):
    inv = jnp.power(ROPE_THETA, -jnp.arange(HALF, dtype=F32) / HALF)
    ang = pos.astype(F32)[:, None] * inv[None, :]
    cos, sin = jnp.cos(ang), jnp.sin(ang)
    z = jnp.zeros((pos.shape[0], HALF_OFF - HALF), F32)
    return (jnp.concatenate([cos, z, cos, z], axis=1),
            jnp.concatenate([-sin, z, sin, z], axis=1))


def _block_diag2(a, b):
    za = jnp.zeros((a.shape[0], b.shape[1]), a.dtype)
    zb = jnp.zeros((b.shape[0], a.shape[1]), a.dtype)
    return jnp.concatenate([jnp.concatenate([a, za], axis=1), jnp.concatenate([zb, b], axis=1)], axis=0)


def _prep_even_weights(w_in, w_out, pe_k, pe_v, w_ck, w_cv):
    d = w_in.shape[0]
    q = _pad_head(w_in[:, :ATT_W].reshape(d, N_HEADS, HEAD_DIM)).reshape(d, QW)
    kvs = []
    for s in range(3):
        blk = w_in[:, ATT_W + s * KVW:ATT_W + (s + 1) * KVW].reshape(d, 2 * KV_HEADS, HEAD_DIM)
        kvs.append(_pad_head(blk).reshape(d, KVP))
    g0 = ATT_W + 3 * KVW
    gates = w_in[:, g0:g0 + 3 * N_HEADS].reshape(d, 3, KV_HEADS, HPG).transpose(0, 2, 1, 3)
    gates = gates.reshape(d, KV_HEADS, 3 * HPG)
    gates = jnp.pad(gates, ((0, 0), (0, 0), (0, LANES - 3 * HPG))).reshape(d, GATE_W)
    glu = w_in[:, g0 + 3 * N_HEADS:]
    a = glu[:, :CONV_CH].reshape(d, 2, CONV_CH // 2)
    g = glu[:, CONV_CH:].reshape(d, 2, CONV_CH // 2)
    glu = jnp.stack([a, g], axis=2).reshape(d, 2 * CONV_CH)
    w_in_p = jnp.concatenate([q] + kvs + [glu, gates], axis=1).astype(BF16)
    wo_att = _pad_head(w_out[:ATT_W].reshape(N_HEADS, HEAD_DIM, d).transpose(0, 2, 1))
    wo_att = wo_att.transpose(0, 2, 1).reshape(QW, d).astype(BF16)
    wo_conv = w_out[ATT_W:].astype(BF16)
    eye = jnp.eye(KV_HEADS, dtype=F32)
    pe = jnp.concatenate([jnp.tile(_pad_head(pe_k), (1, KV_HEADS)), jnp.tile(_pad_head(pe_v), (1, KV_HEADS))], axis=1)
    wk_full = _pad_head(_pad_head(w_ck).T).T
    wv_full = _pad_head(_pad_head(w_cv).T).T
    big_p = _block_diag2(jnp.kron(eye, wk_full), jnp.kron(eye, wv_full))
    pe_t = jnp.concatenate([jnp.tile(jnp.tile(pe_k.T, (1, PAGE_SIZE // CMP_BLOCK)), (KV_HEADS, 1)),
                            jnp.tile(jnp.tile(pe_v.T, (1, PAGE_SIZE // CMP_BLOCK)), (KV_HEADS, 1))], axis=0)
    big_t = _block_diag2(jnp.kron(eye, _pad_head(w_ck).T), jnp.kron(eye, _pad_nat(w_cv).T))
    return dict(w_in=w_in_p, wo_att=wo_att, wo_conv=wo_conv, pe=pe, pe_t=pe_t,
                big_p=big_p.astype(BF16), big_t=big_t.astype(BF16))


def _block_sum_matrices(pps):
    bpp = PAGE_SIZE // CMP_BLOCK
    nbl = bpp * pps
    p = jnp.arange(pps)[:, None, None]
    i = (jnp.arange(PAGE_SIZE) // CMP_BLOCK)[None, :, None]
    c = jnp.arange(nbl)[None, None, :]
    col = (i % 2) * (nbl // 2) + (bpp // 2) * p + i // 2
    return jnp.where(c == col, 1.0 / CMP_BLOCK, 0.0).astype(BF16)


def _ln_rows(y, g, b):
    mu = jnp.mean(y, axis=-1, keepdims=True)
    yc = y - mu
    var = jnp.mean(yc * yc, axis=-1, keepdims=True)
    return yc * lax.rsqrt(var + LN_EPS) * g + b


def _rope_slot(x, cos, sin):
    return x * cos + pltpu.roll(x, HALF_OFF, axis=1) * sin


_NT = (((1,), (1,)), ((), ()))


def _pad_rows_to(x, rows):
    return jnp.concatenate([x, jnp.zeros((rows - x.shape[0],) + x.shape[1:], x.dtype)], axis=0)


def _mm(x, w, *, tm, tn, n_off, n_cols, epilogue, extras=(), outs, name):
    m, k = x.shape
    assert m % tm == 0 and n_cols % tn == 0 and n_off % tn == 0
    joff = n_off // tn
    if isinstance(w, tuple):
        w, layer = w
        w_spec = pl.BlockSpec((None, k, tn), lambda i, j: (layer, 0, joff + j))
    else:
        w_spec = pl.BlockSpec((k, tn), lambda i, j: (0, joff + j))
    in_specs = [pl.BlockSpec((tm, k), lambda i, j: (i, 0)), w_spec]
    in_specs += [pl.BlockSpec(bs, im) for _, bs, im in extras]
    out_shape, out_specs = [], []
    for o in outs:
        if len(o) == 3:
            out_shape.append(jax.ShapeDtypeStruct((m, o[0]), o[1]))
            out_specs.append(pl.BlockSpec((tm, o[2]), lambda i, j: (i, j)))
        else:
            out_shape.append(jax.ShapeDtypeStruct(o[0], o[1]))
            out_specs.append(pl.BlockSpec(o[2], o[3]))
    ne = len(extras)

    def body(x_ref, w_ref, *refs):
        acc = jnp.dot(x_ref[...], w_ref[...], preferred_element_type=F32)
        epilogue(acc, refs[:ne], refs[ne:])

    return pl.pallas_call(
        body, grid=(m // tm, n_cols // tn), in_specs=in_specs, out_specs=out_specs, out_shape=out_shape,
        compiler_params=_cparams("parallel", "arbitrary"), name=name,
    )(x, w, *[a for a, _, _ in extras])


def _ep_q(acc, ex, outs):
    cos, sin = ex[0][...], ex[1][...]
    outs[0][...] = acc.astype(BF16)
    for j in range(acc.shape[1] // HSLOT):
        sl = slice(j * HSLOT, (j + 1) * HSLOT)
        outs[1][:, sl] = _rope_slot(acc[:, sl], cos, sin).astype(BF16)


def _ep_kv(acc, ex, outs, *, rope, want):
    o = dict(zip(want, outs))
    if rope:
        cos, sin = ex[0][...], ex[1][...]
    for j in range(2 * KV_HEADS):
        sl = slice(j * HSLOT, (j + 1) * HSLOT)
        x = acc[:, sl]
        if rope and j < KV_HEADS:
            x = _rope_slot(x, cos, sin)
        if "f32" in o:
            o["f32"][:, sl] = x
        if "bf16" in o:
            o["bf16"][:, sl] = x.astype(BF16)
        if "t" in o:
            xt = x.T
            o["t"][j * HEAD_DIM:j * HEAD_DIM + HALF, :] = xt[0:HALF]
            o["t"][j * HEAD_DIM + HALF:(j + 1) * HEAD_DIM, :] = xt[HALF_OFF:HALF_OFF + HALF]


def _ep_glu(acc, ex, outs):
    h = acc.shape[1] // 2
    outs[0][...] = acc[:, :h] * jax.nn.sigmoid(acc[:, h:])


def _ep_sigmoid(acc, ex, outs):
    outs[0][...] = jax.nn.sigmoid(acc)


def _ep_plain(acc, ex, outs):
    outs[0][...] = acc.astype(outs[0].dtype)


def _ep_relu2(acc, ex, outs):
    r = jnp.maximum(acc, 0.0)
    outs[0][...] = (r * r).astype(outs[0].dtype)


def _ep_gelu(acc, ex, outs):
    outs[0][...] = jax.nn.gelu(acc)


def _ep_gelu_gln(acc, ex, outs):
    g, b = ex[0][...], ex[1][...]
    v = jax.nn.gelu(acc)
    for j in range(acc.shape[1] // SGU_GC):
        sl = slice(j * SGU_GC, (j + 1) * SGU_GC)
        outs[0][:, sl] = _ln_rows(v[:, sl], g[:, sl], b[:, sl])


def _proj_ln(a_list, w_list, resid, g, b, *, tm, name):
    m, n = resid.shape
    npair = len(a_list)
    in_specs = []
    for a in a_list:
        in_specs.append(pl.BlockSpec((tm, a.shape[1]), lambda i: (i, 0)))
    for w in w_list:
        in_specs.append(pl.BlockSpec(w.shape, lambda i: (0, 0), pipeline_mode=pl.Buffered(1)))
    in_specs += [pl.BlockSpec((tm, n), lambda i: (i, 0)),
                 pl.BlockSpec((1, n), lambda i: (0, 0)), pl.BlockSpec((1, n), lambda i: (0, 0))]

    def body(*refs):
        a_refs, w_refs = refs[:npair], refs[npair:2 * npair]
        r_ref, g_ref, b_ref, o_ref, ob_ref = refs[2 * npair:]
        acc = ALPHA * r_ref[...]
        for a_ref, w_ref in zip(a_refs, w_refs):
            acc = acc + jnp.dot(a_ref[...], w_ref[...], preferred_element_type=F32)
        y = _ln_rows(acc, g_ref[...], b_ref[...])
        o_ref[...] = y
        ob_ref[...] = y.astype(BF16)

    return pl.pallas_call(
        body, grid=(m // tm,), in_specs=in_specs,
        out_specs=[pl.BlockSpec((tm, n), lambda i: (i, 0)), pl.BlockSpec((tm, n), lambda i: (i, 0))],
        out_shape=[jax.ShapeDtypeStruct((m, n), F32), jax.ShapeDtypeStruct((m, n), BF16)],
        compiler_params=_cparams("parallel"), name=name,
    )(*a_list, *w_list, resid, g.reshape(1, n), b.reshape(1, n))


def _mlp2_ln(h, w2, layer, resid, g, b, *, tm, tk, name):
    m, kf = h.shape
    n = w2.shape[2]
    nk = kf // tk

    def body(h_ref, w_ref, r_ref, g_ref, b_ref, o_ref, ob_ref):
        k = pl.program_id(1)

        @pl.when(k == 0)
        def _():
            o_ref[...] = ALPHA * r_ref[...]

        o_ref[...] += jnp.dot(h_ref[...], w_ref[...], preferred_element_type=F32)

        @pl.when(k == nk - 1)
        def _():
            y = _ln_rows(o_ref[...], g_ref[...], b_ref[...])
            o_ref[...] = y
            ob_ref[...] = y.astype(BF16)

    return pl.pallas_call(
        body, grid=(m // tm, nk),
        in_specs=[pl.BlockSpec((tm, tk), lambda i, k: (i, k)), pl.BlockSpec((None, tk, n), lambda i, k: (layer, k, 0)),
                  pl.BlockSpec((tm, n), lambda i, k: (i, 0)),
                  pl.BlockSpec((1, n), lambda i, k: (0, 0)), pl.BlockSpec((1, n), lambda i, k: (0, 0))],
        out_specs=[pl.BlockSpec((tm, n), lambda i, k: (i, 0)), pl.BlockSpec((tm, n), lambda i, k: (i, 0))],
        out_shape=[jax.ShapeDtypeStruct((m, n), F32), jax.ShapeDtypeStruct((m, n), BF16)],
        compiler_params=_cparams("parallel", "arbitrary"), name=name,
    )(h, w2, resid, g.reshape(1, n), b.reshape(1, n))


def _compress_rows(kvc, pe, *, rows, name):
    m, c = kvc.shape
    nb = rows // CMP_BLOCK

    def body(x_ref, pe_ref, o_ref):
        x = x_ref[...].reshape(nb, CMP_BLOCK, c) * pe_ref[...][None]
        o_ref[...] = jnp.sum(x, axis=1) * (1.0 / CMP_BLOCK)

    return pl.pallas_call(
        body, grid=(m // rows,),
        in_specs=[pl.BlockSpec((rows, c), lambda i: (i, 0)), pl.BlockSpec((CMP_BLOCK, c), lambda i: (0, 0))],
        out_specs=pl.BlockSpec((nb, c), lambda i: (i, 0)),
        out_shape=jax.ShapeDtypeStruct((m // CMP_BLOCK, c), F32),
        compiler_params=_cparams("parallel"), name=name,
    )(kvc, pe)


def _compress_pages(pages_t, page_table, pe_t, big_t, bsum, layer_base, *, pps, name):
    bd, n_pages = page_table.shape
    bpp = PAGE_SIZE // CMP_BLOCK
    nbl = bpp * pps
    nsteps = n_pages // pps

    def body(pt_ref, *refs):
        page_refs = refs[:pps]
        pe_ref, big_ref, bsum_ref, o_ref = refs[pps:]
        acc = jnp.zeros((KVW, nbl), F32)
        for p in range(pps):
            x = page_refs[p][...] * pe_ref[...]
            hi = x.astype(BF16)
            lo = (x - hi.astype(F32)).astype(BF16)
            e = bsum_ref[p]
            acc = acc + jnp.dot(hi, e, preferred_element_type=F32) + jnp.dot(lo, e, preferred_element_type=F32)
        o_ref[...] = jnp.dot(big_ref[...], acc.astype(BF16), preferred_element_type=F32)

    def page_map(p):
        return lambda b, j, pt: (layer_base + pt[b, j * pps + p], 0, 0)

    cst2 = lambda b, j, pt: (0, 0)
    grid_spec = pltpu.PrefetchScalarGridSpec(
        num_scalar_prefetch=1, grid=(bd, nsteps),
        in_specs=[pl.BlockSpec((None, KVW, PAGE_SIZE), page_map(p)) for p in range(pps)]
        + [pl.BlockSpec((KVW, PAGE_SIZE), cst2), pl.BlockSpec((KVP, KVW), cst2),
           pl.BlockSpec((pps, PAGE_SIZE, nbl), lambda b, j, pt: (0, 0, 0))],
        out_specs=pl.BlockSpec((None, KVP, nbl), lambda b, j, pt: (b, 0, j)))
    return pl.pallas_call(
        body, grid_spec=grid_spec,
        out_shape=jax.ShapeDtypeStruct((bd, KVP, n_pages * bpp), F32),
        compiler_params=_cparams("parallel", "arbitrary"), name=name,
    )(page_table, *([pages_t] * pps), pe_t, big_t, bsum)


def _flash_step_t(q, k, vt, valid, m_ref, l_ref, acc_ref):
    bias = jnp.where(valid, 0.0, NEG)
    s = jnp.dot(k, q, preferred_element_type=F32) + jnp.concatenate([bias] * HPG, axis=1)
    m_prev = m_ref[...]
    m_new = jnp.maximum(m_prev, jnp.max(s, axis=0, keepdims=True))
    alpha = jnp.exp2(m_prev - m_new)
    p = jnp.exp2(s - m_new)
    l_ref[...] = alpha * l_ref[...] + jnp.sum(p, axis=0, keepdims=True)
    acc_ref[...] = alpha * acc_ref[...] + jnp.dot(vt, p.astype(BF16), preferred_element_type=F32)
    m_ref[...] = m_new


def _attn_body(qraw_ref, qrot_ref, gate_ref, ck_ref, cv_ref, ks_ref, vs_ref, kw_ref, vw_ref, o_ref,
               m_ref, l_ref, acc_ref, sel_ref, vts_ref, vtw_ref, *, tq, tk, seq, k_top):
    i = pl.program_id(2)
    cols = HPG * tq
    nsb = seq // SEL_BLOCK
    nt = seq // tk
    bpt = tk // SEL_BLOCK
    q0 = i * tq
    lane_q = lax.broadcasted_iota(jnp.int32, (1, cols), 1)
    qpos = q0 + (lane_q & (tq - 1))
    qp1 = q0 + lax.broadcasted_iota(jnp.int32, (1, tq), 1)

    @pl.when(i == 0)
    def _():
        def tr(t, c):
            for h in range(tk // LANES):
                k0 = pl.multiple_of(t * tk + h * LANES, LANES)
                hs = slice(h * LANES, (h + 1) * LANES)
                vts_ref[t, :, hs] = vs_ref[pl.ds(k0, LANES), :].astype(F32).T.astype(BF16)
                vtw_ref[t, :, hs] = vw_ref[pl.ds(k0, LANES), :].astype(F32).T.astype(BF16)
            return c
        lax.fori_loop(0, nt, tr, 0)

    def heads_t(ref, scale=1.0):
        parts = [ref[:, j * HSLOT:(j + 1) * HSLOT].astype(F32).T * scale for j in range(HPG)]
        return jnp.concatenate(parts, axis=1).astype(BF16)

    qr = heads_t(qraw_ref)
    ck = jnp.concatenate([ck_ref[pl.ds(0, nsb, stride=2), :], ck_ref[pl.ds(1, nsb, stride=2), :]], axis=0)
    s = jnp.dot(ck.astype(BF16), qr, preferred_element_type=F32) * SCALE
    r = lax.broadcasted_iota(jnp.int32, (2 * nsb, 1), 0)
    n_of = jnp.where(r < nsb, 2 * r, 2 * (r - nsb) + 1)
    mk = ((n_of + 1) * CMP_BLOCK - 1) <= qpos
    s = jnp.where(mk, s, NEG)
    mx = jnp.max(s, axis=0, keepdims=True)
    p = jnp.where(mk, jnp.exp(s - mx), 0.0)
    pn = p / jnp.maximum(jnp.sum(p, axis=0, keepdims=True), 1e-30)
    cv = jnp.concatenate([cv_ref[pl.ds(0, nsb, stride=2), :], cv_ref[pl.ds(1, nsb, stride=2), :]], axis=0)
    cvt = _pad_rows_to(cv, LANES).T.astype(BF16)
    o_cmp = jnp.dot(cvt, _pad_rows_to(pn, LANES).astype(BF16), preferred_element_type=F32)
    pp = pn[0:nsb] + pn[nsb:2 * nsb]
    imp = pp[:, 0:tq]
    for j in range(1, HPG):
        imp = imp + pp[:, j * tq:(j + 1) * tq]

    sb = lax.broadcasted_iota(jnp.int32, (nsb, 1), 0)
    vis = (sb * SEL_BLOCK) <= qp1
    cur = sb == (qp1 >> (SEL_BLOCK.bit_length() - 1))
    imp = jnp.where(cur, FORCE, jnp.where(vis, imp, -FORCE))
    cnt = jnp.zeros((nsb, tq), F32)
    for j in range(nsb):
        rowj = imp[j:j + 1, :]
        beats = (rowj > imp) | ((rowj == imp) & (j < sb))
        cnt = cnt + jnp.where(beats, 1.0, 0.0)
    sel = jnp.where(cnt < k_top, 1.0, 0.0)
    for t in range(nt):
        sel_ref[t, 0:bpt, :] = sel[bpt * t:bpt * (t + 1), :]

    qt = heads_t(qrot_ref, SCALE * LOG2E)
    rowk = lax.broadcasted_iota(jnp.int32, (tk, 1), 0)

    def reset():
        m_ref[...] = jnp.full((1, cols), NEG, F32)
        l_ref[...] = jnp.zeros((1, cols), F32)
        acc_ref[...] = jnp.zeros((HSLOT, cols), F32)

    def result():
        return acc_ref[...] / jnp.maximum(l_ref[...], 1e-30)

    reset()

    def sel_step(t, c):
        k0 = pl.multiple_of(t * tk, tk)
        kp = k0 + rowk
        sm = sel_ref[t, bpt - 1:bpt, :]
        for j in range(bpt - 2, -1, -1):
            sm = jnp.where(rowk < (j + 1) * SEL_BLOCK, sel_ref[t, j:j + 1, :], sm)
        valid = (kp <= qp1) & (sm > 0.5)
        _flash_step_t(qt, ks_ref[pl.ds(k0, tk), :], vts_ref[t], valid, m_ref, l_ref, acc_ref)
        return c

    t_end = lax.div(q0 + tq + tk - 1, tk)
    lax.fori_loop(0, t_end, sel_step, 0)
    o_sel = result()

    reset()

    def win_step(t, c):
        k0 = pl.multiple_of(t * tk, tk)
        kp = k0 + rowk
        valid = (kp <= qp1) & (kp >= qp1 - WINDOW)
        _flash_step_t(qt, kw_ref[pl.ds(k0, tk), :], vtw_ref[t], valid, m_ref, l_ref, acc_ref)
        return c

    lax.fori_loop(lax.div(jnp.maximum(q0 - WINDOW, 0), tk), t_end, win_step, 0)
    o_win = result()

    gt = gate_ref[...].T
    o = jnp.zeros((HSLOT, cols), F32)
    for br, o_br in enumerate((o_cmp, o_sel, o_win)):
        grow = jnp.concatenate([gt[br * HPG + j:br * HPG + j + 1, :] for j in range(HPG)], axis=1)
        o = o + grow * o_br
    for j in range(HPG):
        o_ref[:, j * HSLOT:(j + 1) * HSLOT] = o[:, j * tq:(j + 1) * tq].T.astype(BF16)


def _prompt_attention(qraw, qrot, gates, ckv, kvs_b, kvw_b, *, batch, seq, tq, tk):
    nq = seq // tq
    ncb = seq // CMP_BLOCK
    nt = seq // tk
    cols = HPG * tq
    k_top = min(TOP_K, seq // SEL_BLOCK)
    assert tq & (tq - 1) == 0 and seq % tk == 0 and tk % LANES == 0 and tk // SEL_BLOCK <= SUBLANES
    body = functools.partial(_attn_body, tq=tq, tk=tk, seq=seq, k_top=k_top)
    gw = HPG * HSLOT
    kmap = lambda b, g, i: (b, g)
    vmap_ = lambda b, g, i: (b, KV_HEADS + g)
    return pl.pallas_call(
        body, grid=(batch, KV_HEADS, nq),
        in_specs=[pl.BlockSpec((tq, gw), lambda b, g, i: (b * nq + i, g)),
                  pl.BlockSpec((tq, gw), lambda b, g, i: (b * nq + i, g)),
                  pl.BlockSpec((tq, LANES), lambda b, g, i: (b * nq + i, g)),
                  pl.BlockSpec((ncb, HSLOT), kmap), pl.BlockSpec((ncb, HSLOT), vmap_),
                  pl.BlockSpec((seq, HSLOT), kmap), pl.BlockSpec((seq, HSLOT), vmap_),
                  pl.BlockSpec((seq, HSLOT), kmap), pl.BlockSpec((seq, HSLOT), vmap_)],
        out_specs=pl.BlockSpec((tq, gw), lambda b, g, i: (b * nq + i, g)),
        out_shape=jax.ShapeDtypeStruct((batch * seq, QW), BF16),
        scratch_shapes=[pltpu.VMEM((1, cols), F32), pltpu.VMEM((1, cols), F32), pltpu.VMEM((HSLOT, cols), F32),
                        pltpu.VMEM((nt, SUBLANES, tq), F32),
                        pltpu.VMEM((nt, HSLOT, tk), BF16), pltpu.VMEM((nt, HSLOT, tk), BF16)],
        compiler_params=_cparams("parallel", "parallel", "arbitrary"), name="prompt_attention",
    )(qraw, qrot, gates, ckv, ckv, kvs_b, kvs_b, kvw_b, kvw_b)


CONV_HALO = 32


def _conv_body(cur_ref, prev_ref, w_ref, b_ref, g_ref, bn_ref, o_ref, ext_ref, *, ts):
    c = pl.program_id(1)
    ext_ref[0:CONV_HALO, :] = jnp.where(c > 0, prev_ref[...], 0.0)
    ext_ref[CONV_HALO:CONV_HALO + ts, :] = cur_ref[...]
    acc = jnp.zeros((ts, CONV_CH), F32) + b_ref[...]
    off = CONV_HALO - (CONV_W - 1)
    for k in range(CONV_W):
        acc = acc + ext_ref[pl.ds(off + k, ts), :] * w_ref[k:k + 1, :]
    y = _ln_rows(acc, g_ref[...], bn_ref[...])
    o_ref[...] = (y * jax.nn.sigmoid(y)).astype(BF16)


def _prompt_conv(u, cw, cb, cg, cbn, *, batch, seq, ts):
    nt = seq // ts
    r = ts // CONV_HALO
    cwp = jnp.pad(cw, ((0, CONV_HALO - CONV_W), (0, 0)))
    vec = lambda a: a.reshape(1, CONV_CH)
    cst = lambda b, c: (0, 0)
    return pl.pallas_call(
        functools.partial(_conv_body, ts=ts), grid=(batch, nt),
        in_specs=[pl.BlockSpec((ts, CONV_CH), lambda b, c: (b * nt + c, 0)),
                  pl.BlockSpec((CONV_HALO, CONV_CH), lambda b, c: (jnp.maximum((b * nt + c) * r - 1, 0), 0)),
                  pl.BlockSpec((CONV_HALO, CONV_CH), cst),
                  pl.BlockSpec((1, CONV_CH), cst), pl.BlockSpec((1, CONV_CH), cst), pl.BlockSpec((1, CONV_CH), cst)],
        out_specs=pl.BlockSpec((ts, CONV_CH), lambda b, c: (b * nt + c, 0)),
        out_shape=jax.ShapeDtypeStruct((batch * seq, CONV_CH), BF16),
        scratch_shapes=[pltpu.VMEM((CONV_HALO + ts, CONV_CH), F32)],
        compiler_params=_cparams("parallel", "arbitrary"), name="prompt_conv",
    )(u, u, cwp, vec(cb), vec(cg), vec(cbn))


POOL_HALO = 16


def _odd_mix_body(pin_ref, prev_ref, u_ref, vn_ref, pw_ref, ps_ref, sw_ref, sb_ref, o_ref, ext_ref):
    c = pl.program_id(1)
    ext_ref[0:POOL_HALO, :] = jnp.where(c > 0, prev_ref[...], 0.0)
    ext_ref[POOL_HALO:POOL_HALO + CHUNK, :] = pin_ref[...]
    t = c * CHUNK + lax.broadcasted_iota(jnp.int32, (CHUNK, 1), 0)
    for g, w in enumerate(POOL_WINDOWS):
        sl = slice(g * POOL_GC, (g + 1) * POOL_GC)
        tot = ext_ref[pl.ds(POOL_HALO, CHUNK), sl]
        for j in range(1, w):
            tot = tot + ext_ref[pl.ds(POOL_HALO - j, CHUNK), sl]
        cnt = jnp.minimum(w, t + 1).astype(F32)
        d = tot / cnt - pin_ref[:, sl]
        y = jnp.dot(d.astype(BF16), pw_ref[g], preferred_element_type=F32)
        o_ref[:, sl] = (y * ps_ref[:, sl]).astype(BF16)
    ri = lax.broadcasted_iota(jnp.int32, (CHUNK, CHUNK), 0)
    ci = lax.broadcasted_iota(jnp.int32, (CHUNK, CHUNK), 1)
    for g in range(SGU_GROUPS):
        sl = slice(g * SGU_GC, (g + 1) * SGU_GC)
        ws = jnp.where(ci <= ri, sw_ref[g], 0.0).astype(BF16)
        mixed = jnp.dot(ws, vn_ref[:, sl].astype(BF16), preferred_element_type=F32) + sb_ref[:, g:g + 1]
        o_ref[:, POOL_CH + g * SGU_GC:POOL_CH + (g + 1) * SGU_GC] = (u_ref[:, sl] * mixed).astype(BF16)


def _prompt_odd_mix(pin, u, vn, pool_w, pool_scale, sgu_w, sgu_b, *, batch, seq):
    nt = seq // CHUNK
    r = CHUNK // POOL_HALO
    cst2 = lambda b, c: (0, 0)
    cst3 = lambda b, c: (0, 0, 0)
    row = lambda b, c: (b * nt + c, 0)
    return pl.pallas_call(
        _odd_mix_body, grid=(batch, nt),
        in_specs=[pl.BlockSpec((CHUNK, POOL_CH), row),
                  pl.BlockSpec((POOL_HALO, POOL_CH), lambda b, c: (jnp.maximum((b * nt + c) * r - 1, 0), 0)),
                  pl.BlockSpec((CHUNK, SGU_CH), row), pl.BlockSpec((CHUNK, SGU_CH), row),
                  pl.BlockSpec((POOL_GROUPS, POOL_GC, POOL_GC), cst3), pl.BlockSpec((1, POOL_CH), cst2),
                  pl.BlockSpec((SGU_GROUPS, CHUNK, CHUNK), cst3), pl.BlockSpec((CHUNK, SGU_GROUPS), cst2)],
        out_specs=pl.BlockSpec((CHUNK, D_MODEL), row),
        out_shape=jax.ShapeDtypeStruct((batch * seq, D_MODEL), BF16),
        scratch_shapes=[pltpu.VMEM((POOL_HALO + CHUNK, POOL_CH), F32)],
        compiler_params=_cparams("parallel", "arbitrary"), name="prompt_pool_sgu",
    )(pin, pin, u, vn, pool_w.astype(BF16), pool_scale.reshape(1, POOL_CH), sgu_w, sgu_b.T)


def _group_rows(nrows):
    return lax.broadcasted_iota(jnp.int32, (nrows, 1), 0) >> (HPG.bit_length() - 1)


def _sample_cmp_body(q_ref, ckv_ref, o_ref, idx_ref, *, qpos, ncb, nbl, k_past):
    q = q_ref[...].astype(BF16)
    rg = _group_rows(N_HEADS)
    half = nbl // 2
    assert nbl & (nbl - 1) == 0
    sh = nbl.bit_length() - 1
    lane = lax.broadcasted_iota(jnp.int32, (1, ncb), 1)
    grp, w = lane >> sh, lane & (nbl - 1)
    n_cmp = grp * nbl + 2 * (w & (half - 1)) + (w >> (sh - 1))
    mk = jnp.broadcast_to(((n_cmp + 1) * CMP_BLOCK - 1) <= qpos, (N_HEADS, ncb))
    s = jnp.zeros((N_HEADS, ncb), F32)
    for g in range(KV_HEADS):
        ck = ckv_ref[g * HSLOT:(g + 1) * HSLOT, :].astype(BF16)
        s = jnp.where(rg == g, jnp.dot(q, ck, preferred_element_type=F32) * SCALE, s)
    s = jnp.where(mk, s, NEG)
    mx = jnp.max(s, axis=-1, keepdims=True)
    p = jnp.where(mk, jnp.exp(s - mx), 0.0)
    pn = p / jnp.maximum(jnp.sum(p, axis=-1, keepdims=True), 1e-30)
    o = jnp.zeros((N_HEADS, HSLOT), F32)
    for g in range(KV_HEADS):
        cv = ckv_ref[(KV_HEADS + g) * HSLOT:(KV_HEADS + g + 1) * HSLOT, :].astype(BF16)
        o = jnp.where(rg == g, lax.dot_general(pn.astype(BF16), cv, _NT, preferred_element_type=F32), o)
    o_ref[...] = o
    pair = pn + pltpu.roll(pn, ncb - half, axis=1)
    valid = w < half
    sb = grp * half + w
    vis = (sb * SEL_BLOCK) <= qpos
    ri = lax.broadcasted_iota(jnp.int32, (ncb, ncb), 0)
    sb_r = (ri >> sh) * half + (ri & (nbl - 1))
    sb_c = jnp.broadcast_to(sb, (ncb, ncb))
    slot = lax.broadcasted_iota(jnp.int32, (TOP_K, 1), 0)
    for g in range(KV_HEADS):
        imp = jnp.sum(jnp.where(rg == g, pair, 0.0), axis=0, keepdims=True)
        imp = jnp.where(valid, jnp.where(vis, imp, -FORCE), -2.0 * FORCE)
        a = jnp.broadcast_to(imp, (ncb, ncb))
        bt = a.T
        beats = (bt > a) | ((bt == a) & (sb_r < sb_c))
        rank = jnp.sum(jnp.where(beats, 1.0, 0.0), axis=0, keepdims=True)
        onehot = jnp.where((rank == slot.astype(F32)) & valid, 1.0, 0.0)
        idx = jnp.sum(onehot * sb.astype(F32), axis=-1, keepdims=True)
        idx = jnp.where(slot < k_past, idx, 0.0)
        idx_ref[g * TOP_K:(g + 1) * TOP_K, :] = jnp.broadcast_to(idx, (TOP_K, LANES)).astype(jnp.int32)


def _sample_cmp(q3, ckv_t, *, bd, past, qpos, nbl):
    ncb = past // CMP_BLOCK
    k_past = min(TOP_K - 1, past // SEL_BLOCK)
    body = functools.partial(_sample_cmp_body, qpos=qpos, ncb=ncb, nbl=nbl, k_past=k_past)
    head3 = pl.BlockSpec((None, N_HEADS, HSLOT), lambda b: (b, 0, 0))
    return pl.pallas_call(
        body, grid=(bd,),
        in_specs=[head3, pl.BlockSpec((None, KVP, ncb), lambda b: (b, 0, 0))],
        out_specs=[head3, pl.BlockSpec((None, KV_HEADS * TOP_K, LANES), lambda b: (b, 0, 0))],
        out_shape=[jax.ShapeDtypeStruct((bd, N_HEADS, HSLOT), F32),
                   jax.ShapeDtypeStruct((bd, KV_HEADS * TOP_K, LANES), jnp.int32)],
        compiler_params=_cparams("parallel"), name="sample_cmp_attention",
    )(q3, ckv_t)


Q_PAD_ROWS = 8


def _pad_dt(x):
    return _pad_rows_to(x, HSLOT)


def _sample_sel_body(pt_ref, idx_ref, *refs, k_past):
    k_refs, v_refs = refs[:k_past], refs[k_past:2 * k_past]
    q_ref, knew_ref, vnew_ref, o_ref = refs[2 * k_past:]
    b, g = pl.program_id(0), pl.program_id(1)
    bpp = PAGE_SIZE // SEL_BLOCK
    q = q_ref[...].astype(BF16)
    half_of_lane = lax.broadcasted_iota(jnp.int32, (1, PAGE_SIZE), 1) // SEL_BLOCK
    s_parts, m_parts = [], []
    for s in range(k_past):
        kt = _pad_dt(k_refs[s][...]).astype(BF16)
        s_parts.append(jnp.dot(q, kt, preferred_element_type=F32) * SCALE)
        m_parts.append(jnp.broadcast_to(half_of_lane == idx_ref[b, g, s] % bpp, (Q_PAD_ROWS, PAGE_SIZE)))
    s_old = jnp.concatenate(s_parts, axis=1)
    mk = jnp.concatenate(m_parts, axis=1)
    s_old = jnp.where(mk, s_old, NEG)
    s_all = lax.dot_general(q, knew_ref[...].astype(BF16), _NT, preferred_element_type=F32) * SCALE
    lane = lax.broadcasted_iota(jnp.int32, s_all.shape, 1)
    s_new = jnp.sum(jnp.where(lane == b, s_all, 0.0), axis=-1, keepdims=True)
    mx = jnp.maximum(jnp.max(s_old, axis=-1, keepdims=True), s_new)
    p_old = jnp.where(mk, jnp.exp(s_old - mx), 0.0)
    p_new = jnp.exp(s_new - mx)
    den = jnp.maximum(jnp.sum(p_old, axis=-1, keepdims=True) + p_new, 1e-30)
    v_new = vnew_ref[pl.ds(b, 1), :].astype(BF16).astype(F32)
    o = p_new.astype(BF16).astype(F32) * v_new
    for s in range(k_past):
        vt = _pad_dt(v_refs[s][...]).astype(BF16)
        ps = p_old[:, s * PAGE_SIZE:(s + 1) * PAGE_SIZE].astype(BF16)
        o = o + lax.dot_general(ps, vt, _NT, preferred_element_type=F32)
    o_ref[...] = o / den


def _sample_sel(pages5, page_table, idx, layer_base, q4, kvs_new, *, bd, past):
    k_past = min(TOP_K - 1, past // SEL_BLOCK)
    bpp = PAGE_SIZE // SEL_BLOCK

    def blk_map(s, kv):
        def f(b, g, pt, ix):
            return (layer_base + pt[b, ix[b, g, s] // bpp], kv, g, 0, 0)
        return f

    tile = lambda s, kv: pl.BlockSpec((None, None, None, HEAD_DIM, PAGE_SIZE), blk_map(s, kv))
    grid_spec = pltpu.PrefetchScalarGridSpec(
        num_scalar_prefetch=2, grid=(bd, KV_HEADS),
        in_specs=[tile(s, 0) for s in range(k_past)] + [tile(s, 1) for s in range(k_past)]
        + [pl.BlockSpec((None, None, Q_PAD_ROWS, HSLOT), lambda b, g, pt, ix: (b, g, 0, 0)),
           pl.BlockSpec((SAMPLE_ROWS, HSLOT), lambda b, g, pt, ix: (0, g)),
           pl.BlockSpec((SAMPLE_ROWS, HSLOT), lambda b, g, pt, ix: (0, KV_HEADS + g))],
        out_specs=pl.BlockSpec((None, None, Q_PAD_ROWS, HSLOT), lambda b, g, pt, ix: (b, g, 0, 0)))
    return pl.pallas_call(
        functools.partial(_sample_sel_body, k_past=k_past), grid_spec=grid_spec,
        out_shape=jax.ShapeDtypeStruct((bd, KV_HEADS, Q_PAD_ROWS, HSLOT), F32),
        compiler_params=_cparams("parallel", "arbitrary"), name="sample_sel_attention",
    )(page_table, idx, *([pages5] * (2 * k_past)), q4, kvs_new, kvs_new)


def _sample_win_body(q_ref, win_ref, new_ref, ocmp_ref, osel_ref, gate_ref, o_ref, *, qpos, past, wb):
    b = pl.program_id(0)
    q = q_ref[...].astype(BF16)
    qf = q.astype(F32)
    rg = _group_rows(N_HEADS)
    new = new_ref[pl.ds(b, 1), :].astype(BF16).astype(F32)
    kpos = (past - wb) + lax.broadcasted_iota(jnp.int32, (1, wb), 1)
    mk = jnp.broadcast_to((kpos <= qpos) & (kpos >= qpos - WINDOW), (N_HEADS, wb))
    s_old = jnp.zeros((N_HEADS, wb), F32)
    s_new = jnp.zeros((N_HEADS, 1), F32)
    for g in range(KV_HEADS):
        kt = _pad_dt(win_ref[g]).astype(BF16)
        s_old = jnp.where(rg == g, jnp.dot(q, kt, preferred_element_type=F32) * SCALE, s_old)
        sn = jnp.sum(qf * new[:, g * HSLOT:(g + 1) * HSLOT], axis=-1, keepdims=True) * SCALE
        s_new = jnp.where(rg == g, sn, s_new)
    s_old = jnp.where(mk, s_old, NEG)
    mx = jnp.maximum(jnp.max(s_old, axis=-1, keepdims=True), s_new)
    p_old = jnp.where(mk, jnp.exp(s_old - mx), 0.0)
    p_new = jnp.exp(s_new - mx)
    den = jnp.maximum(jnp.sum(p_old, axis=-1, keepdims=True) + p_new, 1e-30)
    o_win = jnp.zeros((N_HEADS, HSLOT), F32)
    for g in range(KV_HEADS):
        vt = _pad_dt(win_ref[KV_HEADS + g]).astype(BF16)
        og = lax.dot_general(p_old.astype(BF16), vt, _NT, preferred_element_type=F32)
        og = og + p_new.astype(BF16).astype(F32) * new[:, (KV_HEADS + g) * HSLOT:(KV_HEADS + g + 1) * HSLOT]
        o_win = jnp.where(rg == g, og, o_win)
    o_win = o_win / den
    gts = gate_ref[...]
    o_ref[...] = gts[:, 0:1] * ocmp_ref[...] + gts[:, 1:2] * osel_ref[...] + gts[:, 2:3] * o_win


def _sample_win(q3, win4, layer, kvw_new, o_cmp, o_sel, gates3, *, bd, past, qpos):
    wb = win4.shape[-1]
    head3 = pl.BlockSpec((None, N_HEADS, HSLOT), lambda b: (b, 0, 0))
    return pl.pallas_call(
        functools.partial(_sample_win_body, qpos=qpos, past=past, wb=wb), grid=(bd,),
        in_specs=[head3, pl.BlockSpec((None, 2 * KV_HEADS, HEAD_DIM, wb), lambda b: (layer * bd + b, 0, 0, 0)),
                  pl.BlockSpec((SAMPLE_ROWS, KVP), lambda b: (0, 0)), head3, head3, head3],
        out_specs=head3,
        out_shape=jax.ShapeDtypeStruct((bd, N_HEADS, HSLOT), F32),
        compiler_params=_cparams("parallel"), name="sample_win_attention",
    )(q3, win4, kvw_new, o_cmp, o_sel, gates3)


def _sample_conv_body(st_ref, u_ref, w_ref, b_ref, g_ref, bn_ref, o_ref, *, bd):
    w = w_ref[...]
    y = jnp.sum(st_ref[...] * w[None, :CONV_W - 1, :], axis=1) + u_ref[0:bd, :] * w[CONV_W - 1:CONV_W, :] + b_ref[...]
    y = _ln_rows(y, g_ref[...], bn_ref[...])
    o_ref[...] = y * jax.nn.sigmoid(y)


def _sample_conv(state, layer, u, cw, cb, cg, cbn, *, bd):
    vec = lambda a: a.reshape(1, CONV_CH)
    cst = lambda i: (0, 0)
    return pl.pallas_call(
        functools.partial(_sample_conv_body, bd=bd), grid=(1,),
        in_specs=[pl.BlockSpec((None, bd, CONV_W - 1, CONV_CH), lambda i: (layer, 0, 0, 0)),
                  pl.BlockSpec((SAMPLE_ROWS, CONV_CH), cst), pl.BlockSpec((CONV_W, CONV_CH), cst),
                  pl.BlockSpec((1, CONV_CH), cst), pl.BlockSpec((1, CONV_CH), cst), pl.BlockSpec((1, CONV_CH), cst)],
        out_specs=pl.BlockSpec((bd, CONV_CH), cst),
        out_shape=jax.ShapeDtypeStruct((bd, CONV_CH), F32),
        compiler_params=_cparams("arbitrary"), name="sample_conv",
    )(state, u, cw, vec(cb), vec(cg), vec(cbn))


def _sample_odd_body(st_ref, pin_ref, u_ref, vn_ref, pw_ref, ps_ref, w0_ref, b0_ref, o_ref, *, bd, start_pos):
    pin = pin_ref[0:bd, :]
    st = st_ref[...]
    for g, w in enumerate(POOL_WINDOWS):
        sl = slice(g * POOL_GC, (g + 1) * POOL_GC)
        tot = pin[:, sl] + jnp.sum(st[:, POOL_STATE - (w - 1):, sl], axis=1)
        d = tot / float(min(w, start_pos + 1)) - pin[:, sl]
        dp = jnp.concatenate([d, jnp.zeros((SAMPLE_ROWS - bd, POOL_GC), F32)], axis=0).astype(BF16)
        y = jnp.dot(dp, pw_ref[g], preferred_element_type=F32)[0:bd]
        o_ref[:, sl] = y * ps_ref[:, sl]
    mixed = w0_ref[...] * vn_ref[0:bd, :] + b0_ref[...]
    o_ref[:, POOL_CH:] = u_ref[0:bd, :] * mixed


def _sample_odd_mix(state, layer, pin, u, vn, pool_w, pool_scale, sgu_w, sgu_b, *, bd, start_pos):
    w0 = jnp.repeat(sgu_w[:, 0, 0], SGU_GC).reshape(1, SGU_CH)
    b0 = jnp.repeat(sgu_b[:, 0], SGU_GC).reshape(1, SGU_CH)
    cst = lambda i: (0, 0)
    return pl.pallas_call(
        functools.partial(_sample_odd_body, bd=bd, start_pos=start_pos), grid=(1,),
        in_specs=[pl.BlockSpec((None, bd, POOL_STATE, POOL_CH), lambda i: (layer, 0, 0, 0)),
                  pl.BlockSpec((SAMPLE_ROWS, POOL_CH), cst), pl.BlockSpec((SAMPLE_ROWS, SGU_CH), cst),
                  pl.BlockSpec((SAMPLE_ROWS, SGU_CH), cst),
                  pl.BlockSpec((POOL_GROUPS, POOL_GC, POOL_GC), lambda i: (0, 0, 0)),
                  pl.BlockSpec((1, POOL_CH), cst), pl.BlockSpec((1, SGU_CH), cst), pl.BlockSpec((1, SGU_CH), cst)],
        out_specs=pl.BlockSpec((bd, D_MODEL), cst),
        out_shape=jax.ShapeDtypeStruct((bd, D_MODEL), F32),
        compiler_params=_cparams("arbitrary"), name="sample_pool_sgu",
    )(state, pin, u, vn, pool_w.astype(BF16), pool_scale.reshape(1, POOL_CH), w0, b0)


def _pad_rows(x, rows):
    return jnp.pad(x, ((0, rows - x.shape[0]), (0, 0)))


def _split_to_nat(x, nheads):
    y = _unpad_heads(x, nheads)
    return _pad_nat(y.reshape(y.shape[:-1] + (nheads, HEAD_DIM))).reshape(x.shape)


def _nat_to_split(x, nheads):
    xh = x.reshape(x.shape[:-1] + (nheads, HSLOT))[..., :HEAD_DIM]
    return _pad_head(xh).reshape(x.shape)


def _even_in_proj(xb, w_in, cos, sin, *, tm, bs=None):
    nrep = cos.shape[0] // tm
    tab = lambda i, j: (i % nrep, 0)
    rope_ex = ((cos, (tm, LANES), tab), (sin, (tm, LANES), tab))
    gw = HPG * HSLOT
    qraw, qrot = _mm(xb, w_in, tm=tm, tn=gw, n_off=E_Q, n_cols=QW, epilogue=_ep_q, extras=rope_ex,
                     outs=((QW, BF16, gw), (QW, BF16, gw)), name="even_in_q")
    row = {"f32": (KVP, F32, KVP), "bf16": (KVP, BF16, KVP)}
    if bs is not None:
        nst = bs[1] // tm
        row["t"] = ((bs[0], KVW, bs[1]), F32, (None, KVW, tm), lambda i, j: (i // nst, 0, i % nst))
        wants = (("f32", "t"), ("bf16", "t"), ("bf16", "t"))
    else:
        wants = (("f32",), ("f32",), ("f32",))
    kv_out = []
    for sec, (off, want) in enumerate(zip((E_KVC, E_KVS, E_KVW), wants)):
        kv_out.append(_mm(xb, w_in, tm=tm, tn=KVP, n_off=off, n_cols=KVP,
                          epilogue=functools.partial(_ep_kv, rope=sec > 0, want=want),
                          extras=rope_ex if sec > 0 else (), outs=tuple(row[k] for k in want),
                          name=("even_in_kvc", "even_in_kvs", "even_in_kvw")[sec]))
    kvc, kvs, kvw = kv_out
    (u,) = _mm(xb, w_in, tm=tm, tn=CONV_CH, n_off=E_GLU, n_cols=2 * CONV_CH, epilogue=_ep_glu,
               outs=((CONV_CH, F32, CONV_CH // 2),), name="even_in_glu")
    (gates,) = _mm(xb, w_in, tm=tm, tn=GATE_W, n_off=E_GATE, n_cols=GATE_W, epilogue=_ep_sigmoid,
                   outs=((GATE_W, F32, GATE_W),), name="even_in_gates")
    return qraw, qrot, kvc, kvs, kvw, u, gates


def _mlp(x, xb, w1, w2, layer, g, b, *, tm1, tn1, tm2, tk2):
    dff = w1.shape[2]
    (h,) = _mm(xb, (w1, layer), tm=tm1, tn=tn1, n_off=0, n_cols=dff, epilogue=_ep_relu2,
               outs=((dff, BF16, tn1),), name="mlp_up")
    return _mlp2_ln(h, w2, layer, x, g, b, tm=tm2, tk=tk2, name="mlp_down_ln")


def kernel(x_prompt, x_sample, cache_cmp_kv, cache_sel_kv, cache_win_kv, state_conv, state_pool, page_table,
           w_in_even, w_out_even, cmp_pe_k, cmp_pe_v, cmp_w_k, cmp_w_v, conv_w, conv_b, conv_ln_g, conv_ln_b,
           w_in_odd, w_out_odd, pool_w, pool_scale, sgu_ln_g, sgu_ln_b, sgu_w, sgu_b,
           mlp_w1, mlp_w2, ln_mix_g, ln_mix_b, ln_ffn_g, ln_ffn_b):
    B, S, D = x_prompt.shape
    Bd, Sd, _ = x_sample.shape
    n_pages = page_table.shape[1]
    past = n_pages * PAGE_SIZE
    n_even, n_pool = cache_cmp_kv.shape[:2]
    wb = cache_win_kv.shape[2]
    assert D == D_MODEL and Sd == 1 and Bd <= SAMPLE_ROWS
    assert S % 1024 == 0 and past % SEL_BLOCK == 0 and S >= WINDOW
    M = B * S
    Ms = SAMPLE_ROWS
    tm_p = 1024

    cos_p, sin_p = _rope_tables(jnp.arange(S, dtype=jnp.int32))
    cos_s, sin_s = _rope_tables(jnp.full((Ms,), past, jnp.int32))
    pps = min(32, n_pages)
    assert n_pages % pps == 0
    bsum = _block_sum_matrices(pps)
    cmp_t = cache_cmp_kv.transpose(0, 1, 3, 4, 5, 2).reshape(n_even * n_pool, KVW, PAGE_SIZE)
    sel_t = cache_sel_kv.transpose(0, 1, 3, 4, 5, 2).reshape(n_even * n_pool, 2, KV_HEADS, HEAD_DIM, PAGE_SIZE)
    win_t = cache_win_kv.transpose(0, 1, 3, 4, 5, 2).reshape(n_even * Bd, 2 * KV_HEADS, HEAD_DIM, wb)

    xp = x_prompt.reshape(M, D)
    xs = _pad_rows(x_sample.reshape(Bd, D), Ms)
    xpb, xsb = xp.astype(BF16), xs.astype(BF16)
    w1b, w2b = mlp_w1.astype(BF16), mlp_w2.astype(BF16)

    outs = {k: [] for k in ("cmp_p", "cmp_s", "sel_p", "sel_s", "win_p", "win_s", "conv_p", "conv_s",
                            "pool_p", "pool_s", "sgu_p", "sgu_s")}
    kv6 = lambda a, lead: a.reshape(lead + (2, KV_HEADS, HEAD_DIM))

    for layer in range(DEPTH):
        if layer % 2 == 0:
            e = layer // 2
            wts = _prep_even_weights(w_in_even[e], w_out_even[e], cmp_pe_k[e], cmp_pe_v[e], cmp_w_k[e], cmp_w_v[e])
            qraw, qrot, (kvc, kvc_t), (kvs_b, kvs_t), (kvw_b, kvw_t), u, gates = _even_in_proj(
                xpb, wts["w_in"], cos_p, sin_p, tm=tm_p, bs=(B, S))
            summ = _compress_rows(kvc, wts["pe"], rows=512, name="prompt_compress")
            (ckv,) = _mm(summ.astype(BF16), wts["big_p"], tm=min(summ.shape[0], 512), tn=KVP,
                         n_off=0, n_cols=KVP, epilogue=_ep_plain, outs=((KVP, F32, KVP),), name="prompt_compress_map")
            o_att = _prompt_attention(qraw, qrot, gates, ckv, kvs_b, kvw_b, batch=B, seq=S, tq=256, tk=256)
            c = _prompt_conv(u, conv_w[e], conv_b[e], conv_ln_g[e], conv_ln_b[e], batch=B, seq=S, ts=256)
            xp, xpb = _proj_ln([o_att, c], [wts["wo_att"], wts["wo_conv"]], xp, ln_mix_g[layer], ln_mix_b[layer],
                               tm=512, name="even_out_ln")
            rows_last = lambda a: a.reshape(B, 2, KV_HEADS, HEAD_DIM, a.shape[-1]).transpose(0, 4, 1, 2, 3)
            outs["cmp_p"].append(rows_last(kvc_t))
            outs["sel_p"].append(rows_last(kvs_t))
            outs["win_p"].append(rows_last(kvw_t[:, :, S - WINDOW:]))
            outs["conv_p"].append(u.reshape(B, S, CONV_CH)[:, S - (CONV_W - 1):])
            qraw_s, qrot_s, (kvc_s,), (kvs_s,), (kvw_s,), u_s, gates_s = _even_in_proj(
                xsb, wts["w_in"], cos_s, sin_s, tm=Ms)
            ckv_t = _compress_pages(cmp_t, page_table, wts["pe_t"], wts["big_t"], bsum, e * n_pool, pps=pps,
                                    name="sample_compress")
            q3 = qraw_s.astype(F32)[:Bd].reshape(Bd, N_HEADS, HSLOT)
            o_cmp, idx = _sample_cmp(q3, ckv_t, bd=Bd, past=past, qpos=past, nbl=pps * (PAGE_SIZE // CMP_BLOCK))
            idx = idx[:, :, 0].reshape(Bd, KV_HEADS, TOP_K)
            qr3 = _split_to_nat(qrot_s.astype(F32)[:Bd], N_HEADS).reshape(Bd, N_HEADS, HSLOT)
            q4 = jnp.pad(qr3.reshape(Bd, KV_HEADS, HPG, HSLOT), ((0, 0), (0, 0), (0, Q_PAD_ROWS - HPG), (0, 0)))
            kvs_nat = _split_to_nat(kvs_s, 2 * KV_HEADS)
            kvw_nat = _split_to_nat(kvw_s, 2 * KV_HEADS)
            o_sel = _sample_sel(sel_t, page_table, idx, e * n_pool, q4, kvs_nat, bd=Bd, past=past)
            o_sel = o_sel[:, :, :HPG].reshape(Bd, N_HEADS, HSLOT)
            g3 = gates_s[:Bd].reshape(Bd, KV_HEADS, LANES)[:, :, :3 * HPG].reshape(Bd, KV_HEADS, 3, HPG)
            g3 = g3.transpose(0, 1, 3, 2).reshape(Bd, N_HEADS, 3)
            g3 = jnp.pad(g3, ((0, 0), (0, 0), (0, LANES - 3)))
            o_s = _sample_win(qr3, win_t, e, kvw_nat, o_cmp, o_sel, g3, bd=Bd, past=past, qpos=past)
            c_s = _sample_conv(state_conv, e, u_s, conv_w[e], conv_b[e], conv_ln_g[e], conv_ln_b[e], bd=Bd)
            o_sb = _pad_rows(_nat_to_split(o_s.reshape(Bd, QW), N_HEADS), Ms).astype(BF16)
            c_sb = _pad_rows(c_s, Ms).astype(BF16)
            xs, xsb = _proj_ln([o_sb, c_sb], [wts["wo_att"], wts["wo_conv"]], xs, ln_mix_g[layer], ln_mix_b[layer],
                               tm=Ms, name="even_out_ln_s")
            kvc_c = _unpad_heads(kvc_s[:Bd], 2 * KV_HEADS)
            kvs_c = _unpad_heads(kvs_s[:Bd], 2 * KV_HEADS)
            kvw_c = _unpad_heads(kvw_s[:Bd], 2 * KV_HEADS)
            outs["cmp_s"].append(kv6(kvc_c, (Bd, 1)))
            outs["sel_s"].append(kv6(kvs_c, (Bd, 1)))
            wkv = jnp.concatenate([cache_win_kv[e], kv6(kvw_c, (Bd, 1))], axis=1)
            outs["win_s"].append(wkv[:, wkv.shape[1] - min(WINDOW, wkv.shape[1]):])
            outs["conv_s"].append(jnp.concatenate([state_conv[e], u_s[:Bd, None, :]], axis=1)[:, 1:])
        else:
            o = layer // 2
            w_in = w_in_odd[o]
            w_in_p = jnp.concatenate([w_in[:, POOL_CH + SGU_CH:], w_in[:, :POOL_CH], w_in[:, POOL_CH:POOL_CH + SGU_CH]],
                                     axis=1).astype(BF16)
            w_out_p = w_out_odd[o].astype(BF16)
            lg, lb = sgu_ln_g[o].reshape(1, SGU_CH), sgu_ln_b[o].reshape(1, SGU_CH)

            def odd_in(xb, tm):
                gl_ex = ((lg, (1, 2 * SGU_GC), lambda i, j: (0, j)), (lb, (1, 2 * SGU_GC), lambda i, j: (0, j)))
                (vn,) = _mm(xb, w_in_p, tm=tm, tn=2 * SGU_GC, n_off=O_V, n_cols=SGU_CH, epilogue=_ep_gelu_gln,
                            extras=gl_ex, outs=((SGU_CH, F32, 2 * SGU_GC),), name="odd_in_v")
                (pin,) = _mm(xb, w_in_p, tm=tm, tn=POOL_CH, n_off=O_PIN, n_cols=POOL_CH, epilogue=_ep_plain,
                             outs=((POOL_CH, F32, POOL_CH),), name="odd_in_pool")
                (uu,) = _mm(xb, w_in_p, tm=tm, tn=POOL_CH, n_off=O_U, n_cols=SGU_CH, epilogue=_ep_gelu,
                            outs=((SGU_CH, F32, POOL_CH),), name="odd_in_u")
                return vn, pin, uu

            vn, pin, uu = odd_in(xpb, tm_p)
            cat = _prompt_odd_mix(pin, uu, vn, pool_w[o], pool_scale[o], sgu_w[o], sgu_b[o], batch=B, seq=S)
            xp, xpb = _proj_ln([cat], [w_out_p], xp, ln_mix_g[layer], ln_mix_b[layer], tm=512, name="odd_out_ln")
            outs["pool_p"].append(pin.reshape(B, S, POOL_CH)[:, S - POOL_STATE:])
            outs["sgu_p"].append(vn.reshape(B, S, SGU_CH)[:, ((S - 1) // CHUNK) * CHUNK:])
            vn_s, pin_s, uu_s = odd_in(xsb, Ms)
            cat_s = _sample_odd_mix(state_pool, o, pin_s, uu_s, vn_s, pool_w[o], pool_scale[o], sgu_w[o], sgu_b[o],
                                    bd=Bd, start_pos=past)
            xs, xsb = _proj_ln([_pad_rows(cat_s, Ms).astype(BF16)], [w_out_p], xs, ln_mix_g[layer], ln_mix_b[layer],
                               tm=Ms, name="odd_out_ln_s")
            outs["pool_s"].append(jnp.concatenate([state_pool[o], pin_s[:Bd, None, :]], axis=1)[:, 1:])
            outs["sgu_s"].append(vn_s[:Bd, None, :])
        xp, xpb = _mlp(xp, xpb, w1b, w2b, layer, ln_ffn_g[layer], ln_ffn_b[layer],
                       tm1=tm_p, tn1=1024, tm2=1024, tk2=512)
        xs, xsb = _mlp(xs, xsb, w1b, w2b, layer, ln_ffn_g[layer], ln_ffn_b[layer],
                       tm1=Ms, tn1=2048, tm2=Ms, tk2=2048)

    st = lambda k: jnp.stack(outs[k])
    return (xp.reshape(B, S, D), xs[:Bd].reshape(Bd, Sd, D),
            st("cmp_p"), st("cmp_s"), st("sel_p"), st("sel_s"), st("win_p"), st("win_s"),
            st("conv_p"), st("conv_s"), st("pool_p"), st("pool_s"), st("sgu_p"), st("sgu_s"))
```

```python
import functools

import jax
import jax.numpy as jnp
from jax import lax
from jax.experimental import pallas as pl
from jax.experimental.pallas import tpu as pltpu

F32 = jnp.float32
BF16 = jnp.bfloat16

D_MODEL = 2048
DEPTH = 4
PAGE_SIZE = 128
N_HEADS = 16
HEAD_DIM = 96
KV_HEADS = 4
HPG = N_HEADS // KV_HEADS
ATT_W = N_HEADS * HEAD_DIM
KVW = 2 * KV_HEADS * HEAD_DIM
CMP_BLOCK = 32
SEL_BLOCK = 64
TOP_K = 16
WINDOW = 512
ROPE_THETA = 10000.0
SCALE = HEAD_DIM ** -0.5
LOG2E = 1.4426950408889634
FORCE = 1e9
NEG = -1e30
CONV_CH = D_MODEL // 4
CONV_W = 31
POOL_CH = D_MODEL // 4
POOL_WINDOWS = (2, 4, 8, 16)
POOL_GROUPS = len(POOL_WINDOWS)
POOL_GC = POOL_CH // POOL_GROUPS
POOL_STATE = max(POOL_WINDOWS) - 1
SGU_CH = D_MODEL - POOL_CH
SGU_GROUPS = 4
SGU_GC = SGU_CH // SGU_GROUPS
CHUNK = 128
D_FF = 4 * D_MODEL
ALPHA = (2 * DEPTH) ** 0.25
LN_EPS = 1e-5

LANES = 128
SUBLANES = 8
HALF = HEAD_DIM // 2
HSLOT = LANES
HALF_OFF = LANES // 2
QW = N_HEADS * HSLOT
KVP = 2 * KV_HEADS * HSLOT
GATE_W = KV_HEADS * LANES
SAMPLE_ROWS = 16
ROW_SUB = 256
VMEM_LIMIT = 52 * 1024 * 1024

E_Q, E_KVC, E_KVS, E_KVW = 0, QW, QW + KVP, QW + 2 * KVP
E_GLU = QW + 3 * KVP
E_GATE = E_GLU + 2 * CONV_CH
E_TOT = E_GATE + GATE_W
O_V, O_PIN, O_U = 0, SGU_CH, SGU_CH + POOL_CH


def _cparams(*sem):
    return pltpu.CompilerParams(dimension_semantics=sem, vmem_limit_bytes=VMEM_LIMIT)


def _pad_head(x):
    z = jnp.zeros(x.shape[:-1] + (HALF_OFF - HALF,), x.dtype)
    return jnp.concatenate([x[..., :HALF], z, x[..., HALF:], z], axis=-1)


def _pad_nat(x):
    return jnp.concatenate([x, jnp.zeros(x.shape[:-1] + (HSLOT - HEAD_DIM,), x.dtype)], axis=-1)


def _unpad_heads(x, nheads):
    xh = x.reshape(x.shape[:-1] + (nheads, HSLOT))
    y = jnp.concatenate([xh[..., :HALF], xh[..., HALF_OFF:HALF_OFF + HALF]], axis=-1)
    return y.reshape(x.shape[:-1] + (nheads * HEAD_DIM,))


def _rope_tables(pos):
    inv = jnp.power(ROPE_THETA, -jnp.arange(HALF, dtype=F32) / HALF)
    ang = pos.astype(F32)[:, None] * inv[None, :]
    cos, sin = jnp.cos(ang), jnp.sin(ang)
    z = jnp.zeros((pos.shape[0], HALF_OFF - HALF), F32)
    return (jnp.concatenate([cos, z, cos, z], axis=1),
            jnp.concatenate([-sin, z, sin, z], axis=1))


def _block_diag2(a, b):
    za = jnp.zeros((a.shape[0], b.shape[1]), a.dtype)
    zb = jnp.zeros((b.shape[0], a.shape[1]), a.dtype)
    return jnp.concatenate([jnp.concatenate([a, za], axis=1), jnp.concatenate([zb, b], axis=1)], axis=0)


def _prep_even_weights(w_in, w_out, pe_k, pe_v, w_ck, w_cv):
    d = w_in.shape[0]
    q = _pad_head(w_in[:, :ATT_W].reshape(d, N_HEADS, HEAD_DIM)).reshape(d, QW)
    kvs = []
    for s in range(3):
        blk = w_in[:, ATT_W + s * KVW:ATT_W + (s + 1) * KVW].reshape(d, 2 * KV_HEADS, HEAD_DIM)
        kvs.append(_pad_head(blk).reshape(d, KVP))
    g0 = ATT_W + 3 * KVW
    gates = w_in[:, g0:g0 + 3 * N_HEADS].reshape(d, 3, KV_HEADS, HPG).transpose(0, 2, 1, 3)
    gates = gates.reshape(d, KV_HEADS, 3 * HPG)
    gates = jnp.pad(gates, ((0, 0), (0, 0), (0, LANES - 3 * HPG))).reshape(d, GATE_W)
    glu = w_in[:, g0 + 3 * N_HEADS:]
    a = glu[:, :CONV_CH].reshape(d, 2, CONV_CH // 2)
    g = glu[:, CONV_CH:].reshape(d, 2, CONV_CH // 2)
    glu = jnp.stack([a, g], axis=2).reshape(d, 2 * CONV_CH)
    w_in_p = jnp.concatenate([q] + kvs + [glu, gates], axis=1).astype(BF16)
    wo_att = _pad_head(w_out[:ATT_W].reshape(N_HEADS, HEAD_DIM, d).transpose(0, 2, 1))
    wo_att = wo_att.transpose(0, 2, 1).reshape(QW, d).astype(BF16)
    wo_conv = w_out[ATT_W:].astype(BF16)
    eye = jnp.eye(KV_HEADS, dtype=F32)
    pe = jnp.concatenate([jnp.tile(_pad_head(pe_k), (1, KV_HEADS)), jnp.tile(_pad_head(pe_v), (1, KV_HEADS))], axis=1)
    wk_full = _pad_head(_pad_head(w_ck).T).T
    wv_full = _pad_head(_pad_head(w_cv).T).T
    big_p = _block_diag2(jnp.kron(eye, wk_full), jnp.kron(eye, wv_full))
    pe_t = jnp.concatenate([jnp.tile(jnp.tile(pe_k.T, (1, PAGE_SIZE // CMP_BLOCK)), (KV_HEADS, 1)),
                            jnp.tile(jnp.tile(pe_v.T, (1, PAGE_SIZE // CMP_BLOCK)), (KV_HEADS, 1))], axis=0)
    big_t = _block_diag2(jnp.kron(eye, _pad_head(w_ck).T), jnp.kron(eye, _pad_nat(w_cv).T))
    return dict(w_in=w_in_p, wo_att=wo_att, wo_conv=wo_conv, pe=pe, pe_t=pe_t,
                big_p=big_p.astype(BF16), big_t=big_t.astype(BF16))


def _block_sum_matrices(pps):
    bpp = PAGE_SIZE // CMP_BLOCK
    nbl = bpp * pps
    p = jnp.arange(pps)[:, None, None]
    i = (jnp.arange(PAGE_SIZE) // CMP_BLOCK)[None, :, None]
    c = jnp.arange(nbl)[None, None, :]
    col = (i % 2) * (nbl // 2) + (bpp // 2) * p + i // 2
    e = jnp.where(c == col, 1.0 / CMP_BLOCK, 0.0).astype(BF16)
    return jnp.concatenate([e, e], axis=1)


def _ln_rows(y, g, b):
    mu = jnp.mean(y, axis=-1, keepdims=True)
    yc = y - mu
    var = jnp.mean(yc * yc, axis=-1, keepdims=True)
    return yc * lax.rsqrt(var + LN_EPS) * g + b


def _rope_slot(x, cos, sin):
    return x * cos + pltpu.roll(x, HALF_OFF, axis=1) * sin


_NT = (((1,), (1,)), ((), ()))


def _pad_rows_to(x, rows):
    return jnp.concatenate([x, jnp.zeros((rows - x.shape[0],) + x.shape[1:], x.dtype)], axis=0)


def _mm(x, w, *, tm, tn, n_off, n_cols, epilogue, extras=(), outs, name):
    m, k = x.shape
    assert m % tm == 0 and n_cols % tn == 0 and n_off % tn == 0
    joff = n_off // tn
    if isinstance(w, tuple):
        w, layer = w
        w_spec = pl.BlockSpec((None, k, tn), lambda i, j: (layer, 0, joff + j))
    else:
        w_spec = pl.BlockSpec((k, tn), lambda i, j: (0, joff + j))
    in_specs = [pl.BlockSpec((tm, k), lambda i, j: (i, 0)), w_spec]
    in_specs += [pl.BlockSpec(bs, im) for _, bs, im in extras]
    out_shape, out_specs = [], []
    for o in outs:
        if len(o) == 3:
            out_shape.append(jax.ShapeDtypeStruct((m, o[0]), o[1]))
            out_specs.append(pl.BlockSpec((tm, o[2]), lambda i, j: (i, j)))
        else:
            out_shape.append(jax.ShapeDtypeStruct(o[0], o[1]))
            out_specs.append(pl.BlockSpec(o[2], o[3]))
    ne = len(extras)

    rsub = min(tm, ROW_SUB)

    def body(x_ref, w_ref, *refs):
        for r in range(tm // rsub):
            rows = slice(r * rsub, (r + 1) * rsub)
            acc = jnp.dot(x_ref[rows, :], w_ref[...], preferred_element_type=F32)
            epilogue(acc, refs[:ne], refs[ne:], rows)

    return pl.pallas_call(
        body, grid=(m // tm, n_cols // tn), in_specs=in_specs, out_specs=out_specs, out_shape=out_shape,
        compiler_params=_cparams("parallel", "arbitrary"), name=name,
    )(x, w, *[a for a, _, _ in extras])


def _ep_q(acc, ex, outs, rows):
    cos, sin = ex[0][rows, :], ex[1][rows, :]
    outs[0][rows, :] = acc.astype(BF16)
    for j in range(acc.shape[1] // HSLOT):
        sl = slice(j * HSLOT, (j + 1) * HSLOT)
        outs[1][rows, sl] = _rope_slot(acc[:, sl], cos, sin).astype(BF16)


def _ep_kv(acc, ex, outs, rows, *, rope, want):
    o = dict(zip(want, outs))
    if rope:
        cos, sin = ex[0][rows, :], ex[1][rows, :]
    for j in range(2 * KV_HEADS):
        sl = slice(j * HSLOT, (j + 1) * HSLOT)
        x = acc[:, sl]
        if rope and j < KV_HEADS:
            x = _rope_slot(x, cos, sin)
        if "f32" in o:
            o["f32"][rows, sl] = x
        if "bf16" in o:
            o["bf16"][rows, sl] = x.astype(BF16)
        if "t" in o:
            xt = x.T
            o["t"][j * HEAD_DIM:j * HEAD_DIM + HALF, rows] = xt[0:HALF]
            o["t"][j * HEAD_DIM + HALF:(j + 1) * HEAD_DIM, rows] = xt[HALF_OFF:HALF_OFF + HALF]


def _ep_glu(acc, ex, outs, rows):
    h = acc.shape[1] // 2
    outs[0][rows, :] = acc[:, :h] * jax.nn.sigmoid(acc[:, h:])


def _ep_sigmoid(acc, ex, outs, rows):
    outs[0][rows, :] = jax.nn.sigmoid(acc)


def _ep_plain(acc, ex, outs, rows):
    outs[0][rows, :] = acc.astype(outs[0].dtype)


def _ep_relu2(acc, ex, outs, rows):
    r = jnp.maximum(acc, 0.0)
    outs[0][rows, :] = (r * r).astype(outs[0].dtype)


def _ep_gelu(acc, ex, outs, rows):
    outs[0][rows, :] = jax.nn.gelu(acc)


def _ep_gelu_gln(acc, ex, outs, rows):
    g, b = ex[0][...], ex[1][...]
    v = jax.nn.gelu(acc)
    for j in range(acc.shape[1] // SGU_GC):
        sl = slice(j * SGU_GC, (j + 1) * SGU_GC)
        outs[0][rows, sl] = _ln_rows(v[:, sl], g[:, sl], b[:, sl])


def _proj_ln(a_list, w_list, resid, g, b, *, tm, name):
    m, n = resid.shape
    npair = len(a_list)
    in_specs = []
    for a in a_list:
        in_specs.append(pl.BlockSpec((tm, a.shape[1]), lambda i: (i, 0)))
    for w in w_list:
        in_specs.append(pl.BlockSpec(w.shape, lambda i: (0, 0), pipeline_mode=pl.Buffered(1)))
    in_specs += [pl.BlockSpec((tm, n), lambda i: (i, 0)),
                 pl.BlockSpec((1, n), lambda i: (0, 0)), pl.BlockSpec((1, n), lambda i: (0, 0))]

    def body(*refs):
        a_refs, w_refs = refs[:npair], refs[npair:2 * npair]
        r_ref, g_ref, b_ref, o_ref, ob_ref = refs[2 * npair:]
        rsub = min(tm, ROW_SUB)
        for r in range(tm // rsub):
            rows = slice(r * rsub, (r + 1) * rsub)
            acc = ALPHA * r_ref[rows, :]
            for a_ref, w_ref in zip(a_refs, w_refs):
                acc = acc + jnp.dot(a_ref[rows, :], w_ref[...], preferred_element_type=F32)
            y = _ln_rows(acc, g_ref[...], b_ref[...])
            o_ref[rows, :] = y
            ob_ref[rows, :] = y.astype(BF16)

    return pl.pallas_call(
        body, grid=(m // tm,), in_specs=in_specs,
        out_specs=[pl.BlockSpec((tm, n), lambda i: (i, 0)), pl.BlockSpec((tm, n), lambda i: (i, 0))],
        out_shape=[jax.ShapeDtypeStruct((m, n), F32), jax.ShapeDtypeStruct((m, n), BF16)],
        compiler_params=_cparams("parallel"), name=name,
    )(*a_list, *w_list, resid, g.reshape(1, n), b.reshape(1, n))


def _mlp2_ln(h, w2, resid, g, b, *, tm, tk, name):
    m, kf = h.shape
    n = w2.shape[1]
    nk = kf // tk

    def body(h_ref, w_ref, r_ref, g_ref, b_ref, o_ref, ob_ref):
        k = pl.program_id(1)

        rsub = min(tm, ROW_SUB)

        @pl.when(k == 0)
        def _():
            o_ref[...] = ALPHA * r_ref[...]

        for r in range(tm // rsub):
            rows = slice(r * rsub, (r + 1) * rsub)
            o_ref[rows, :] += jnp.dot(h_ref[rows, :], w_ref[...], preferred_element_type=F32)

        @pl.when(k == nk - 1)
        def _():
            for r in range(tm // rsub):
                rows = slice(r * rsub, (r + 1) * rsub)
                y = _ln_rows(o_ref[rows, :], g_ref[...], b_ref[...])
                o_ref[rows, :] = y
                ob_ref[rows, :] = y.astype(BF16)

    return pl.pallas_call(
        body, grid=(m // tm, nk),
        in_specs=[pl.BlockSpec((tm, tk), lambda i, k: (i, k)), pl.BlockSpec((tk, n), lambda i, k: (k, 0)),
                  pl.BlockSpec((tm, n), lambda i, k: (i, 0)),
                  pl.BlockSpec((1, n), lambda i, k: (0, 0)), pl.BlockSpec((1, n), lambda i, k: (0, 0))],
        out_specs=[pl.BlockSpec((tm, n), lambda i, k: (i, 0)), pl.BlockSpec((tm, n), lambda i, k: (i, 0))],
        out_shape=[jax.ShapeDtypeStruct((m, n), F32), jax.ShapeDtypeStruct((m, n), BF16)],
        compiler_params=_cparams("parallel", "arbitrary"), name=name,
    )(h, w2, resid, g.reshape(1, n), b.reshape(1, n))


def _compress_rows(kvc, pe, *, rows, name):
    m, c = kvc.shape
    nb = rows // CMP_BLOCK

    def body(x_ref, pe_ref, o_ref):
        x = x_ref[...].reshape(nb, CMP_BLOCK, c) * pe_ref[...][None]
        o_ref[...] = jnp.sum(x, axis=1) * (1.0 / CMP_BLOCK)

    return pl.pallas_call(
        body, grid=(m // rows,),
        in_specs=[pl.BlockSpec((rows, c), lambda i: (i, 0)), pl.BlockSpec((CMP_BLOCK, c), lambda i: (0, 0))],
        out_specs=pl.BlockSpec((nb, c), lambda i: (i, 0)),
        out_shape=jax.ShapeDtypeStruct((m // CMP_BLOCK, c), F32),
        compiler_params=_cparams("parallel"), name=name,
    )(kvc, pe)


def _compress_pages(pages_t, page_table, pe_t, big_t, bsum, layer_base, *, pps, name):
    bd, n_pages = page_table.shape
    bpp = PAGE_SIZE // CMP_BLOCK
    nbl = bpp * pps
    nsteps = n_pages // pps

    def body(pt_ref, *refs):
        page_refs = refs[:pps]
        pe_ref, big_ref, bsum_ref, o_ref = refs[pps:]
        acc = jnp.zeros((KVW, nbl), F32)
        for p in range(pps):
            x = page_refs[p][...] * pe_ref[...]
            hi = x.astype(BF16)
            lo = (x - hi.astype(F32)).astype(BF16)
            acc = acc + jnp.dot(jnp.concatenate([hi, lo], axis=1), bsum_ref[p], preferred_element_type=F32)
        o_ref[...] = jnp.dot(big_ref[...], acc.astype(BF16), preferred_element_type=F32)

    def page_map(p):
        return lambda b, j, pt: (layer_base + pt[b, j * pps + p], 0, 0)

    cst2 = lambda b, j, pt: (0, 0)
    grid_spec = pltpu.PrefetchScalarGridSpec(
        num_scalar_prefetch=1, grid=(bd, nsteps),
        in_specs=[pl.BlockSpec((None, KVW, PAGE_SIZE), page_map(p)) for p in range(pps)]
        + [pl.BlockSpec((KVW, PAGE_SIZE), cst2), pl.BlockSpec((KVP, KVW), cst2),
           pl.BlockSpec((pps, 2 * PAGE_SIZE, nbl), lambda b, j, pt: (0, 0, 0))],
        out_specs=pl.BlockSpec((None, KVP, nbl), lambda b, j, pt: (b, 0, j)))
    return pl.pallas_call(
        body, grid_spec=grid_spec,
        out_shape=jax.ShapeDtypeStruct((bd, KVP, n_pages * bpp), F32),
        compiler_params=_cparams("parallel", "arbitrary"), name=name,
    )(page_table, *([pages_t] * pps), pe_t, big_t, bsum)


def _flash_step_t(q, k, vt, valid, m_ref, l_ref, acc_ref):
    bias = jnp.where(valid, 0.0, NEG)
    s = jnp.dot(k, q, preferred_element_type=F32) + jnp.concatenate([bias] * HPG, axis=1)
    m_prev = m_ref[...]
    m_new = jnp.maximum(m_prev, jnp.max(s, axis=0, keepdims=True))
    alpha = jnp.exp2(m_prev - m_new)
    p = jnp.exp2(s - m_new)
    l_ref[...] = alpha * l_ref[...] + jnp.sum(p, axis=0, keepdims=True)
    acc_ref[...] = alpha * acc_ref[...] + jnp.dot(vt, p.astype(BF16), preferred_element_type=F32)
    m_ref[...] = m_new


def _attn_body(*refs, tq, tk, seq, k_top, ngrp):
    n_in = 9
    ins = [refs[g * n_in:(g + 1) * n_in] for g in range(ngrp)]
    o_ref = refs[ngrp * n_in]
    m_ref, l_ref, acc_ref, sel_ref, vts_ref, vtw_ref = refs[ngrp * n_in + 1:]
    i = pl.program_id(2)
    cols = HPG * tq
    nsb = seq // SEL_BLOCK
    nt = seq // tk
    bpt = tk // SEL_BLOCK
    q0 = i * tq
    lane_q = lax.broadcasted_iota(jnp.int32, (1, cols), 1)
    qpos = q0 + (lane_q & (tq - 1))
    qp1 = q0 + lax.broadcasted_iota(jnp.int32, (1, tq), 1)

    @pl.when(i == 0)
    def _():
        def tr(t, c):
            for g in range(ngrp):
                vs_ref, vw_ref = ins[g][6], ins[g][8]
                for h in range(tk // LANES):
                    k0 = pl.multiple_of(t * tk + h * LANES, LANES)
                    hs = slice(h * LANES, (h + 1) * LANES)
                    vts_ref[g, t, :, hs] = vs_ref[pl.ds(k0, LANES), :].astype(F32).T.astype(BF16)
                    vtw_ref[g, t, :, hs] = vw_ref[pl.ds(k0, LANES), :].astype(F32).T.astype(BF16)
            return c
        lax.fori_loop(0, nt, tr, 0)

    def heads_t(ref, scale=1.0):
        parts = [ref[:, j * HSLOT:(j + 1) * HSLOT].astype(F32).T * scale for j in range(HPG)]
        return jnp.concatenate(parts, axis=1).astype(BF16)

    r = lax.broadcasted_iota(jnp.int32, (2 * nsb, 1), 0)
    n_of = jnp.where(r < nsb, 2 * r, 2 * (r - nsb) + 1)
    mk = ((n_of + 1) * CMP_BLOCK - 1) <= qpos
    sb = lax.broadcasted_iota(jnp.int32, (nsb, 1), 0)
    vis = (sb * SEL_BLOCK) <= qp1
    cur = sb == (qp1 >> (SEL_BLOCK.bit_length() - 1))

    def compressed_branch(g):
        qraw_ref, ck_ref, cv_ref = ins[g][0], ins[g][3], ins[g][4]
        qr = heads_t(qraw_ref)
        ck = jnp.concatenate([ck_ref[pl.ds(0, nsb, stride=2), :], ck_ref[pl.ds(1, nsb, stride=2), :]], axis=0)
        s = jnp.dot(ck.astype(BF16), qr, preferred_element_type=F32) * SCALE
        s = jnp.where(mk, s, NEG)
        mx = jnp.max(s, axis=0, keepdims=True)
        p = jnp.where(mk, jnp.exp(s - mx), 0.0)
        pn = p / jnp.maximum(jnp.sum(p, axis=0, keepdims=True), 1e-30)
        cv = jnp.concatenate([cv_ref[pl.ds(0, nsb, stride=2), :], cv_ref[pl.ds(1, nsb, stride=2), :]], axis=0)
        cvt = _pad_rows_to(cv, LANES).T.astype(BF16)
        o_cmp = jnp.dot(cvt, _pad_rows_to(pn, LANES).astype(BF16), preferred_element_type=F32)
        pp = pn[0:nsb] + pn[nsb:2 * nsb]
        imp = pp[:, 0:tq]
        for j in range(1, HPG):
            imp = imp + pp[:, j * tq:(j + 1) * tq]
        imp = jnp.where(cur, FORCE, jnp.where(vis, imp, -FORCE))
        cnt = jnp.zeros((nsb, tq), F32)
        for j in range(nsb):
            rowj = imp[j:j + 1, :]
            beats = (rowj > imp) | ((rowj == imp) & (j < sb))
            cnt = cnt + jnp.where(beats, 1.0, 0.0)
        sel = jnp.where(cnt < k_top, 1.0, 0.0)
        for t in range(nt):
            sel_ref[g, t, 0:bpt, :] = sel[bpt * t:bpt * (t + 1), :]
        return o_cmp

    o_cmp = [compressed_branch(g) for g in range(ngrp)]
    qt = [heads_t(ins[g][1], SCALE * LOG2E) for g in range(ngrp)]
    rowk = lax.broadcasted_iota(jnp.int32, (tk, 1), 0)

    def reset():
        m_ref[...] = jnp.full((ngrp, 1, cols), NEG, F32)
        l_ref[...] = jnp.zeros((ngrp, 1, cols), F32)
        acc_ref[...] = jnp.zeros((ngrp, HSLOT, cols), F32)

    def result(g):
        return acc_ref[g] / jnp.maximum(l_ref[g], 1e-30)

    reset()

    def sel_step(t, c):
        k0 = pl.multiple_of(t * tk, tk)
        kp = k0 + rowk
        for g in range(ngrp):
            sm = sel_ref[g, t, bpt - 1:bpt, :]
            for j in range(bpt - 2, -1, -1):
                sm = jnp.where(rowk < (j + 1) * SEL_BLOCK, sel_ref[g, t, j:j + 1, :], sm)
            valid = (kp <= qp1) & (sm > 0.5)
            _flash_step_t(qt[g], ins[g][5][pl.ds(k0, tk), :], vts_ref[g, t], valid,
                          m_ref.at[g], l_ref.at[g], acc_ref.at[g])
        return c

    t_end = lax.div(q0 + tq + tk - 1, tk)
    lax.fori_loop(0, t_end, sel_step, 0)
    o_sel = [result(g) for g in range(ngrp)]

    reset()

    def win_step(t, c):
        k0 = pl.multiple_of(t * tk, tk)
        kp = k0 + rowk
        valid = (kp <= qp1) & (kp >= qp1 - WINDOW)
        for g in range(ngrp):
            _flash_step_t(qt[g], ins[g][7][pl.ds(k0, tk), :], vtw_ref[g, t], valid,
                          m_ref.at[g], l_ref.at[g], acc_ref.at[g])
        return c

    lax.fori_loop(lax.div(jnp.maximum(q0 - WINDOW, 0), tk), t_end, win_step, 0)

    gw = HPG * HSLOT
    for g in range(ngrp):
        gt = ins[g][2][...].T
        o = jnp.zeros((HSLOT, cols), F32)
        for br, o_br in enumerate((o_cmp[g], o_sel[g], result(g))):
            grow = jnp.concatenate([gt[br * HPG + j:br * HPG + j + 1, :] for j in range(HPG)], axis=1)
            o = o + grow * o_br
        for j in range(HPG):
            o_ref[:, g * gw + j * HSLOT:g * gw + (j + 1) * HSLOT] = o[:, j * tq:(j + 1) * tq].T.astype(BF16)


def _prompt_attention(qraw, qrot, gates, ckv, kvs_b, kvw_b, *, batch, seq, tq, tk, ngrp):
    nq = seq // tq
    ncb = seq // CMP_BLOCK
    nt = seq // tk
    cols = HPG * tq
    k_top = min(TOP_K, seq // SEL_BLOCK)
    assert tq & (tq - 1) == 0 and seq % tk == 0 and tk % LANES == 0 and tk // SEL_BLOCK <= SUBLANES
    assert KV_HEADS % ngrp == 0
    body = functools.partial(_attn_body, tq=tq, tk=tk, seq=seq, k_top=k_top, ngrp=ngrp)
    gw = HPG * HSLOT
    in_specs, operands = [], []
    for g in range(ngrp):
        qmap = functools.partial(lambda b, p, i, g: (b * nq + i, p * ngrp + g), g=g)
        kmap = functools.partial(lambda b, p, i, g: (b, p * ngrp + g), g=g)
        vmap_ = functools.partial(lambda b, p, i, g: (b, KV_HEADS + p * ngrp + g), g=g)
        in_specs += [pl.BlockSpec((tq, gw), qmap), pl.BlockSpec((tq, gw), qmap), pl.BlockSpec((tq, LANES), qmap),
                     pl.BlockSpec((ncb, HSLOT), kmap), pl.BlockSpec((ncb, HSLOT), vmap_),
                     pl.BlockSpec((seq, HSLOT), kmap), pl.BlockSpec((seq, HSLOT), vmap_),
                     pl.BlockSpec((seq, HSLOT), kmap), pl.BlockSpec((seq, HSLOT), vmap_)]
        operands += [qraw, qrot, gates, ckv, ckv, kvs_b, kvs_b, kvw_b, kvw_b]
    return pl.pallas_call(
        body, grid=(batch, KV_HEADS // ngrp, nq), in_specs=in_specs,
        out_specs=pl.BlockSpec((tq, ngrp * gw), lambda b, p, i: (b * nq + i, p)),
        out_shape=jax.ShapeDtypeStruct((batch * seq, QW), BF16),
        scratch_shapes=[pltpu.VMEM((ngrp, 1, cols), F32), pltpu.VMEM((ngrp, 1, cols), F32),
                        pltpu.VMEM((ngrp, HSLOT, cols), F32), pltpu.VMEM((ngrp, nt, SUBLANES, tq), F32),
                        pltpu.VMEM((ngrp, nt, HSLOT, tk), BF16), pltpu.VMEM((ngrp, nt, HSLOT, tk), BF16)],
        compiler_params=_cparams("parallel", "parallel", "arbitrary"), name="prompt_attention",
    )(*operands)


CONV_HALO = 32


def _conv_body(cur_ref, prev_ref, w_ref, b_ref, g_ref, bn_ref, o_ref, ext_ref, *, ts):
    c = pl.program_id(1)
    ext_ref[0:CONV_HALO, :] = jnp.where(c > 0, prev_ref[...], 0.0)
    ext_ref[CONV_HALO:CONV_HALO + ts, :] = cur_ref[...]
    acc = jnp.zeros((ts, CONV_CH), F32) + b_ref[...]
    off = CONV_HALO - (CONV_W - 1)
    for k in range(CONV_W):
        acc = acc + ext_ref[pl.ds(off + k, ts), :] * w_ref[k:k + 1, :]
    y = _ln_rows(acc, g_ref[...], bn_ref[...])
    o_ref[...] = (y * jax.nn.sigmoid(y)).astype(BF16)


def _prompt_conv(u, cw, cb, cg, cbn, *, batch, seq, ts):
    nt = seq // ts
    r = ts // CONV_HALO
    cwp = jnp.pad(cw, ((0, CONV_HALO - CONV_W), (0, 0)))
    vec = lambda a: a.reshape(1, CONV_CH)
    cst = lambda b, c: (0, 0)
    return pl.pallas_call(
        functools.partial(_conv_body, ts=ts), grid=(batch, nt),
        in_specs=[pl.BlockSpec((ts, CONV_CH), lambda b, c: (b * nt + c, 0)),
                  pl.BlockSpec((CONV_HALO, CONV_CH), lambda b, c: (jnp.maximum((b * nt + c) * r - 1, 0), 0)),
                  pl.BlockSpec((CONV_HALO, CONV_CH), cst),
                  pl.BlockSpec((1, CONV_CH), cst), pl.BlockSpec((1, CONV_CH), cst), pl.BlockSpec((1, CONV_CH), cst)],
        out_specs=pl.BlockSpec((ts, CONV_CH), lambda b, c: (b * nt + c, 0)),
        out_shape=jax.ShapeDtypeStruct((batch * seq, CONV_CH), BF16),
        scratch_shapes=[pltpu.VMEM((CONV_HALO + ts, CONV_CH), F32)],
        compiler_params=_cparams("parallel", "arbitrary"), name="prompt_conv",
    )(u, u, cwp, vec(cb), vec(cg), vec(cbn))


POOL_HALO = 16


def _odd_mix_body(pin_ref, prev_ref, u_ref, vn_ref, pw_ref, ps_ref, sw_ref, sb_ref, o_ref, ext_ref):
    c = pl.program_id(1)
    ext_ref[0:POOL_HALO, :] = jnp.where(c > 0, prev_ref[...], 0.0)
    ext_ref[POOL_HALO:POOL_HALO + CHUNK, :] = pin_ref[...]
    t = c * CHUNK + lax.broadcasted_iota(jnp.int32, (CHUNK, 1), 0)
    for g, w in enumerate(POOL_WINDOWS):
        sl = slice(g * POOL_GC, (g + 1) * POOL_GC)
        tot = ext_ref[pl.ds(POOL_HALO, CHUNK), sl]
        for j in range(1, w):
            tot = tot + ext_ref[pl.ds(POOL_HALO - j, CHUNK), sl]
        cnt = jnp.minimum(w, t + 1).astype(F32)
        d = tot / cnt - pin_ref[:, sl]
        y = jnp.dot(d.astype(BF16), pw_ref[g], preferred_element_type=F32)
        o_ref[:, sl] = (y * ps_ref[:, sl]).astype(BF16)
    ri = lax.broadcasted_iota(jnp.int32, (CHUNK, CHUNK), 0)
    ci = lax.broadcasted_iota(jnp.int32, (CHUNK, CHUNK), 1)
    for g in range(SGU_GROUPS):
        sl = slice(g * SGU_GC, (g + 1) * SGU_GC)
        ws = jnp.where(ci <= ri, sw_ref[g], 0.0).astype(BF16)
        mixed = jnp.dot(ws, vn_ref[:, sl].astype(BF16), preferred_element_type=F32) + sb_ref[:, g:g + 1]
        o_ref[:, POOL_CH + g * SGU_GC:POOL_CH + (g + 1) * SGU_GC] = (u_ref[:, sl] * mixed).astype(BF16)


def _prompt_odd_mix(pin, u, vn, pool_w, pool_scale, sgu_w, sgu_b, *, batch, seq):
    nt = seq // CHUNK
    r = CHUNK // POOL_HALO
    cst2 = lambda b, c: (0, 0)
    cst3 = lambda b, c: (0, 0, 0)
    row = lambda b, c: (b * nt + c, 0)
    return pl.pallas_call(
        _odd_mix_body, grid=(batch, nt),
        in_specs=[pl.BlockSpec((CHUNK, POOL_CH), row),
                  pl.BlockSpec((POOL_HALO, POOL_CH), lambda b, c: (jnp.maximum((b * nt + c) * r - 1, 0), 0)),
                  pl.BlockSpec((CHUNK, SGU_CH), row), pl.BlockSpec((CHUNK, SGU_CH), row),
                  pl.BlockSpec((POOL_GROUPS, POOL_GC, POOL_GC), cst3), pl.BlockSpec((1, POOL_CH), cst2),
                  pl.BlockSpec((SGU_GROUPS, CHUNK, CHUNK), cst3), pl.BlockSpec((CHUNK, SGU_GROUPS), cst2)],
        out_specs=pl.BlockSpec((CHUNK, D_MODEL), row),
        out_shape=jax.ShapeDtypeStruct((batch * seq, D_MODEL), BF16),
        scratch_shapes=[pltpu.VMEM((POOL_HALO + CHUNK, POOL_CH), F32)],
        compiler_params=_cparams("parallel", "arbitrary"), name="prompt_pool_sgu",
    )(pin, pin, u, vn, pool_w.astype(BF16), pool_scale.reshape(1, POOL_CH), sgu_w, sgu_b.T)


def _group_rows(nrows):
    return lax.broadcasted_iota(jnp.int32, (nrows, 1), 0) >> (HPG.bit_length() - 1)


def _sample_cmp_body(q_ref, ckv_ref, o_ref, idx_ref, *, qpos, ncb, nbl, k_past):
    q = q_ref[...].astype(BF16)
    rg = _group_rows(N_HEADS)
    half = nbl // 2
    assert nbl & (nbl - 1) == 0
    sh = nbl.bit_length() - 1
    lane = lax.broadcasted_iota(jnp.int32, (1, ncb), 1)
    grp, w = lane >> sh, lane & (nbl - 1)
    n_cmp = grp * nbl + 2 * (w & (half - 1)) + (w >> (sh - 1))
    mk = jnp.broadcast_to(((n_cmp + 1) * CMP_BLOCK - 1) <= qpos, (N_HEADS, ncb))
    s = jnp.zeros((N_HEADS, ncb), F32)
    for g in range(KV_HEADS):
        ck = ckv_ref[g * HSLOT:(g + 1) * HSLOT, :].astype(BF16)
        s = jnp.where(rg == g, jnp.dot(q, ck, preferred_element_type=F32) * SCALE, s)
    s = jnp.where(mk, s, NEG)
    mx = jnp.max(s, axis=-1, keepdims=True)
    p = jnp.where(mk, jnp.exp(s - mx), 0.0)
    pn = p / jnp.maximum(jnp.sum(p, axis=-1, keepdims=True), 1e-30)
    o = jnp.zeros((N_HEADS, HSLOT), F32)
    for g in range(KV_HEADS):
        cv = ckv_ref[(KV_HEADS + g) * HSLOT:(KV_HEADS + g + 1) * HSLOT, :].astype(BF16)
        o = jnp.where(rg == g, lax.dot_general(pn.astype(BF16), cv, _NT, preferred_element_type=F32), o)
    o_ref[...] = o
    pair = pn + pltpu.roll(pn, ncb - half, axis=1)
    valid = w < half
    sb = grp * half + w
    vis = (sb * SEL_BLOCK) <= qpos
    ri = lax.broadcasted_iota(jnp.int32, (ncb, ncb), 0)
    sb_r = (ri >> sh) * half + (ri & (nbl - 1))
    sb_c = jnp.broadcast_to(sb, (ncb, ncb))
    slot = lax.broadcasted_iota(jnp.int32, (TOP_K, 1), 0)
    for g in range(KV_HEADS):
        imp = jnp.sum(jnp.where(rg == g, pair, 0.0), axis=0, keepdims=True)
        imp = jnp.where(valid, jnp.where(vis, imp, -FORCE), -2.0 * FORCE)
        a = jnp.broadcast_to(imp, (ncb, ncb))
        bt = a.T
        beats = (bt > a) | ((bt == a) & (sb_r < sb_c))
        rank = jnp.sum(jnp.where(beats, 1.0, 0.0), axis=0, keepdims=True)
        onehot = jnp.where((rank == slot.astype(F32)) & valid, 1.0, 0.0)
        idx = jnp.sum(onehot * sb.astype(F32), axis=-1, keepdims=True)
        idx = jnp.where(slot < k_past, idx, 0.0)
        idx_ref[g * TOP_K:(g + 1) * TOP_K, :] = jnp.broadcast_to(idx, (TOP_K, LANES)).astype(jnp.int32)


def _sample_cmp(q3, ckv_t, *, bd, past, qpos, nbl):
    ncb = past // CMP_BLOCK
    k_past = min(TOP_K - 1, past // SEL_BLOCK)
    body = functools.partial(_sample_cmp_body, qpos=qpos, ncb=ncb, nbl=nbl, k_past=k_past)
    head3 = pl.BlockSpec((None, N_HEADS, HSLOT), lambda b: (b, 0, 0))
    return pl.pallas_call(
        body, grid=(bd,),
        in_specs=[head3, pl.BlockSpec((None, KVP, ncb), lambda b: (b, 0, 0))],
        out_specs=[head3, pl.BlockSpec((None, KV_HEADS * TOP_K, LANES), lambda b: (b, 0, 0))],
        out_shape=[jax.ShapeDtypeStruct((bd, N_HEADS, HSLOT), F32),
                   jax.ShapeDtypeStruct((bd, KV_HEADS * TOP_K, LANES), jnp.int32)],
        compiler_params=_cparams("parallel"), name="sample_cmp_attention",
    )(q3, ckv_t)


Q_PAD_ROWS = 8


def _pad_dt(x):
    return _pad_rows_to(x, HSLOT)


def _sample_sel_body(pt_ref, idx_ref, *refs, k_past):
    k_refs, v_refs = refs[:k_past], refs[k_past:2 * k_past]
    q_ref, knew_ref, vnew_ref, o_ref = refs[2 * k_past:]
    b, g = pl.program_id(0), pl.program_id(1)
    bpp = PAGE_SIZE // SEL_BLOCK
    q = q_ref[...].astype(BF16)
    half_of_lane = lax.broadcasted_iota(jnp.int32, (1, PAGE_SIZE), 1) // SEL_BLOCK
    s_parts, m_parts = [], []
    for s in range(k_past):
        kt = _pad_dt(k_refs[s][...]).astype(BF16)
        s_parts.append(jnp.dot(q, kt, preferred_element_type=F32) * SCALE)
        m_parts.append(jnp.broadcast_to(half_of_lane == idx_ref[b, g, s] % bpp, (Q_PAD_ROWS, PAGE_SIZE)))
    s_old = jnp.concatenate(s_parts, axis=1)
    mk = jnp.concatenate(m_parts, axis=1)
    s_old = jnp.where(mk, s_old, NEG)
    s_all = lax.dot_general(q, knew_ref[...].astype(BF16), _NT, preferred_element_type=F32) * SCALE
    lane = lax.broadcasted_iota(jnp.int32, s_all.shape, 1)
    s_new = jnp.sum(jnp.where(lane == b, s_all, 0.0), axis=-1, keepdims=True)
    mx = jnp.maximum(jnp.max(s_old, axis=-1, keepdims=True), s_new)
    p_old = jnp.where(mk, jnp.exp(s_old - mx), 0.0)
    p_new = jnp.exp(s_new - mx)
    den = jnp.maximum(jnp.sum(p_old, axis=-1, keepdims=True) + p_new, 1e-30)
    v_new = vnew_ref[pl.ds(b, 1), :].astype(BF16).astype(F32)
    o = p_new.astype(BF16).astype(F32) * v_new
    for s in range(k_past):
        vt = _pad_dt(v_refs[s][...]).astype(BF16)
        ps = p_old[:, s * PAGE_SIZE:(s + 1) * PAGE_SIZE].astype(BF16)
        o = o + lax.dot_general(ps, vt, _NT, preferred_element_type=F32)
    o_ref[...] = o / den


def _sample_sel(pages5, page_table, idx, layer_base, q4, kvs_new, *, bd, past):
    k_past = min(TOP_K - 1, past // SEL_BLOCK)
    bpp = PAGE_SIZE // SEL_BLOCK

    def blk_map(s, kv):
        def f(b, g, pt, ix):
            return (layer_base + pt[b, ix[b, g, s] // bpp], kv, g, 0, 0)
        return f

    tile = lambda s, kv: pl.BlockSpec((None, None, None, HEAD_DIM, PAGE_SIZE), blk_map(s, kv))
    grid_spec = pltpu.PrefetchScalarGridSpec(
        num_scalar_prefetch=2, grid=(bd, KV_HEADS),
        in_specs=[tile(s, 0) for s in range(k_past)] + [tile(s, 1) for s in range(k_past)]
        + [pl.BlockSpec((None, None, Q_PAD_ROWS, HSLOT), lambda b, g, pt, ix: (b, g, 0, 0)),
           pl.BlockSpec((SAMPLE_ROWS, HSLOT), lambda b, g, pt, ix: (0, g)),
           pl.BlockSpec((SAMPLE_ROWS, HSLOT), lambda b, g, pt, ix: (0, KV_HEADS + g))],
        out_specs=pl.BlockSpec((None, None, Q_PAD_ROWS, HSLOT), lambda b, g, pt, ix: (b, g, 0, 0)))
    return pl.pallas_call(
        functools.partial(_sample_sel_body, k_past=k_past), grid_spec=grid_spec,
        out_shape=jax.ShapeDtypeStruct((bd, KV_HEADS, Q_PAD_ROWS, HSLOT), F32),
        compiler_params=_cparams("parallel", "arbitrary"), name="sample_sel_attention",
    )(page_table, idx, *([pages5] * (2 * k_past)), q4, kvs_new, kvs_new)


def _sample_win_body(q_ref, win_ref, new_ref, ocmp_ref, osel_ref, gate_ref, o_ref, *, qpos, past, wb):
    b = pl.program_id(0)
    q = q_ref[...].astype(BF16)
    qf = q.astype(F32)
    rg = _group_rows(N_HEADS)
    new = new_ref[pl.ds(b, 1), :].astype(BF16).astype(F32)
    kpos = (past - wb) + lax.broadcasted_iota(jnp.int32, (1, wb), 1)
    mk = jnp.broadcast_to((kpos <= qpos) & (kpos >= qpos - WINDOW), (N_HEADS, wb))
    s_old = jnp.zeros((N_HEADS, wb), F32)
    s_new = jnp.zeros((N_HEADS, 1), F32)
    for g in range(KV_HEADS):
        kt = _pad_dt(win_ref[g]).astype(BF16)
        s_old = jnp.where(rg == g, jnp.dot(q, kt, preferred_element_type=F32) * SCALE, s_old)
        sn = jnp.sum(qf * new[:, g * HSLOT:(g + 1) * HSLOT], axis=-1, keepdims=True) * SCALE
        s_new = jnp.where(rg == g, sn, s_new)
    s_old = jnp.where(mk, s_old, NEG)
    mx = jnp.maximum(jnp.max(s_old, axis=-1, keepdims=True), s_new)
    p_old = jnp.where(mk, jnp.exp(s_old - mx), 0.0)
    p_new = jnp.exp(s_new - mx)
    den = jnp.maximum(jnp.sum(p_old, axis=-1, keepdims=True) + p_new, 1e-30)
    o_win = jnp.zeros((N_HEADS, HSLOT), F32)
    for g in range(KV_HEADS):
        vt = _pad_dt(win_ref[KV_HEADS + g]).astype(BF16)
        og = lax.dot_general(p_old.astype(BF16), vt, _NT, preferred_element_type=F32)
        og = og + p_new.astype(BF16).astype(F32) * new[:, (KV_HEADS + g) * HSLOT:(KV_HEADS + g + 1) * HSLOT]
        o_win = jnp.where(rg == g, og, o_win)
    o_win = o_win / den
    gts = gate_ref[...]
    o_ref[...] = gts[:, 0:1] * ocmp_ref[...] + gts[:, 1:2] * osel_ref[...] + gts[:, 2:3] * o_win


def _sample_win(q3, win4, layer, kvw_new, o_cmp, o_sel, gates3, *, bd, past, qpos):
    wb = win4.shape[-1]
    head3 = pl.BlockSpec((None, N_HEADS, HSLOT), lambda b: (b, 0, 0))
    return pl.pallas_call(
        functools.partial(_sample_win_body, qpos=qpos, past=past, wb=wb), grid=(bd,),
        in_specs=[head3, pl.BlockSpec((None, 2 * KV_HEADS, HEAD_DIM, wb), lambda b: (layer * bd + b, 0, 0, 0)),
                  pl.BlockSpec((SAMPLE_ROWS, KVP), lambda b: (0, 0)), head3, head3, head3],
        out_specs=head3,
        out_shape=jax.ShapeDtypeStruct((bd, N_HEADS, HSLOT), F32),
        compiler_params=_cparams("parallel"), name="sample_win_attention",
    )(q3, win4, kvw_new, o_cmp, o_sel, gates3)


def _sample_conv_body(st_ref, u_ref, w_ref, b_ref, g_ref, bn_ref, o_ref, *, bd):
    w = w_ref[...]
    y = jnp.sum(st_ref[...] * w[None, :CONV_W - 1, :], axis=1) + u_ref[0:bd, :] * w[CONV_W - 1:CONV_W, :] + b_ref[...]
    y = _ln_rows(y, g_ref[...], bn_ref[...])
    o_ref[...] = y * jax.nn.sigmoid(y)


def _sample_conv(state, layer, u, cw, cb, cg, cbn, *, bd):
    vec = lambda a: a.reshape(1, CONV_CH)
    cst = lambda i: (0, 0)
    return pl.pallas_call(
        functools.partial(_sample_conv_body, bd=bd), grid=(1,),
        in_specs=[pl.BlockSpec((None, bd, CONV_W - 1, CONV_CH), lambda i: (layer, 0, 0, 0)),
                  pl.BlockSpec((SAMPLE_ROWS, CONV_CH), cst), pl.BlockSpec((CONV_W, CONV_CH), cst),
                  pl.BlockSpec((1, CONV_CH), cst), pl.BlockSpec((1, CONV_CH), cst), pl.BlockSpec((1, CONV_CH), cst)],
        out_specs=pl.BlockSpec((bd, CONV_CH), cst),
        out_shape=jax.ShapeDtypeStruct((bd, CONV_CH), F32),
        compiler_params=_cparams("arbitrary"), name="sample_conv",
    )(state, u, cw, vec(cb), vec(cg), vec(cbn))


def _sample_odd_body(st_ref, pin_ref, u_ref, vn_ref, pw_ref, ps_ref, w0_ref, b0_ref, o_ref, *, bd, start_pos):
    pin = pin_ref[0:bd, :]
    st = st_ref[...]
    for g, w in enumerate(POOL_WINDOWS):
        sl = slice(g * POOL_GC, (g + 1) * POOL_GC)
        tot = pin[:, sl] + jnp.sum(st[:, POOL_STATE - (w - 1):, sl], axis=1)
        d = tot / float(min(w, start_pos + 1)) - pin[:, sl]
        dp = jnp.concatenate([d, jnp.zeros((SAMPLE_ROWS - bd, POOL_GC), F32)], axis=0).astype(BF16)
        y = jnp.dot(dp, pw_ref[g], preferred_element_type=F32)[0:bd]
        o_ref[:, sl] = y * ps_ref[:, sl]
    mixed = w0_ref[...] * vn_ref[0:bd, :] + b0_ref[...]
    o_ref[:, POOL_CH:] = u_ref[0:bd, :] * mixed


def _sample_odd_mix(state, layer, pin, u, vn, pool_w, pool_scale, sgu_w, sgu_b, *, bd, start_pos):
    w0 = jnp.repeat(sgu_w[:, 0, 0], SGU_GC).reshape(1, SGU_CH)
    b0 = jnp.repeat(sgu_b[:, 0], SGU_GC).reshape(1, SGU_CH)
    cst = lambda i: (0, 0)
    return pl.pallas_call(
        functools.partial(_sample_odd_body, bd=bd, start_pos=start_pos), grid=(1,),
        in_specs=[pl.BlockSpec((None, bd, POOL_STATE, POOL_CH), lambda i: (layer, 0, 0, 0)),
                  pl.BlockSpec((SAMPLE_ROWS, POOL_CH), cst), pl.BlockSpec((SAMPLE_ROWS, SGU_CH), cst),
                  pl.BlockSpec((SAMPLE_ROWS, SGU_CH), cst),
                  pl.BlockSpec((POOL_GROUPS, POOL_GC, POOL_GC), lambda i: (0, 0, 0)),
                  pl.BlockSpec((1, POOL_CH), cst), pl.BlockSpec((1, SGU_CH), cst), pl.BlockSpec((1, SGU_CH), cst)],
        out_specs=pl.BlockSpec((bd, D_MODEL), cst),
        out_shape=jax.ShapeDtypeStruct((bd, D_MODEL), F32),
        compiler_params=_cparams("arbitrary"), name="sample_pool_sgu",
    )(state, pin, u, vn, pool_w.astype(BF16), pool_scale.reshape(1, POOL_CH), w0, b0)


def _pad_rows(x, rows):
    return jnp.pad(x, ((0, rows - x.shape[0]), (0, 0)))


def _split_to_nat(x, nheads):
    y = _unpad_heads(x, nheads)
    return _pad_nat(y.reshape(y.shape[:-1] + (nheads, HEAD_DIM))).reshape(x.shape)


def _nat_to_split(x, nheads):
    xh = x.reshape(x.shape[:-1] + (nheads, HSLOT))[..., :HEAD_DIM]
    return _pad_head(xh).reshape(x.shape)


def _even_in_proj(xb, w_in, cos, sin, *, tm, bs=None):
    nrep = cos.shape[0] // tm
    tab = lambda i, j: (i % nrep, 0)
    rope_ex = ((cos, (tm, LANES), tab), (sin, (tm, LANES), tab))
    gw = HPG * HSLOT
    qraw, qrot = _mm(xb, w_in, tm=tm, tn=gw, n_off=E_Q, n_cols=QW, epilogue=_ep_q, extras=rope_ex,
                     outs=((QW, BF16, gw), (QW, BF16, gw)), name="even_in_q")
    row = {"f32": (KVP, F32, KVP), "bf16": (KVP, BF16, KVP)}
    if bs is not None:
        nst = bs[1] // tm
        row["t"] = ((bs[0], KVW, bs[1]), F32, (None, KVW, tm), lambda i, j: (i // nst, 0, i % nst))
        wants = (("f32", "t"), ("bf16", "t"), ("bf16", "t"))
    else:
        wants = (("f32",), ("f32",), ("f32",))
    kv_out = []
    for sec, (off, want) in enumerate(zip((E_KVC, E_KVS, E_KVW), wants)):
        kv_out.append(_mm(xb, w_in, tm=tm, tn=KVP, n_off=off, n_cols=KVP,
                          epilogue=functools.partial(_ep_kv, rope=sec > 0, want=want),
                          extras=rope_ex if sec > 0 else (), outs=tuple(row[k] for k in want),
                          name=("even_in_kvc", "even_in_kvs", "even_in_kvw")[sec]))
    kvc, kvs, kvw = kv_out
    (u,) = _mm(xb, w_in, tm=tm, tn=CONV_CH, n_off=E_GLU, n_cols=2 * CONV_CH, epilogue=_ep_glu,
               outs=((CONV_CH, F32, CONV_CH // 2),), name="even_in_glu")
    (gates,) = _mm(xb, w_in, tm=tm, tn=GATE_W, n_off=E_GATE, n_cols=GATE_W, epilogue=_ep_sigmoid,
                   outs=((GATE_W, F32, GATE_W),), name="even_in_gates")
    return qraw, qrot, kvc, kvs, kvw, u, gates


def _mlp_up_cast(xb, w1, w2, layer, *, tm, tn):
    m, k = xb.shape
    dff = w1.shape[2]
    d_out = w2.shape[2]
    ni, nj = m // tm, dff // tn
    slab = dff // (ni * nj)
    assert m % tm == 0 and dff % tn == 0 and dff % (ni * nj) == 0 and slab % SAMPLE_ROWS == 0
    rsub = min(tm, ROW_SUB)

    def body(x_ref, w1_ref, w2_ref, h_ref, w1b_ref, w2b_ref):
        @pl.when(pl.program_id(1) == 0)
        def _():
            w1b_ref[...] = w1_ref[...].astype(BF16)

        w2b_ref[...] = w2_ref[...].astype(BF16)
        for r in range(tm // rsub):
            rows = slice(r * rsub, (r + 1) * rsub)
            acc = jnp.dot(x_ref[rows, :], w1b_ref[...], preferred_element_type=F32)
            a = jnp.maximum(acc, 0.0)
            h_ref[rows, :] = (a * a).astype(BF16)

    return pl.pallas_call(
        body, grid=(nj, ni),
        in_specs=[pl.BlockSpec((tm, k), lambda j, i: (i, 0)),
                  pl.BlockSpec((None, k, tn), lambda j, i: (layer, 0, j)),
                  pl.BlockSpec((None, slab, d_out), lambda j, i: (layer, j * ni + i, 0))],
        out_specs=[pl.BlockSpec((tm, tn), lambda j, i: (i, j)),
                   pl.BlockSpec((k, tn), lambda j, i: (0, j)),
                   pl.BlockSpec((slab, d_out), lambda j, i: (j * ni + i, 0))],
        out_shape=[jax.ShapeDtypeStruct((m, dff), BF16), jax.ShapeDtypeStruct((k, dff), BF16),
                   jax.ShapeDtypeStruct((dff, d_out), BF16)],
        compiler_params=_cparams("arbitrary", "arbitrary"), name="mlp_up_cast",
    )(xb, w1, w2)


def _mlp(x, xb, w1b, w2b, g, b, *, tm1, tn1, tm2, tk2):
    dff = w1b.shape[1]
    (h,) = _mm(xb, w1b, tm=tm1, tn=tn1, n_off=0, n_cols=dff, epilogue=_ep_relu2,
               outs=((dff, BF16, tn1),), name="mlp_up")
    return _mlp2_ln(h, w2b, x, g, b, tm=tm2, tk=tk2, name="mlp_down_ln")


def kernel(x_prompt, x_sample, cache_cmp_kv, cache_sel_kv, cache_win_kv, state_conv, state_pool, page_table,
           w_in_even, w_out_even, cmp_pe_k, cmp_pe_v, cmp_w_k, cmp_w_v, conv_w, conv_b, conv_ln_g, conv_ln_b,
           w_in_odd, w_out_odd, pool_w, pool_scale, sgu_ln_g, sgu_ln_b, sgu_w, sgu_b,
           mlp_w1, mlp_w2, ln_mix_g, ln_mix_b, ln_ffn_g, ln_ffn_b):
    B, S, D = x_prompt.shape
    Bd, Sd, _ = x_sample.shape
    n_pages = page_table.shape[1]
    past = n_pages * PAGE_SIZE
    n_even, n_pool = cache_cmp_kv.shape[:2]
    wb = cache_win_kv.shape[2]
    assert D == D_MODEL and Sd == 1 and Bd <= SAMPLE_ROWS
    assert S % 1024 == 0 and past % SEL_BLOCK == 0 and S >= WINDOW
    M = B * S
    Ms = SAMPLE_ROWS
    tm_p = 1024

    cos_p, sin_p = _rope_tables(jnp.arange(S, dtype=jnp.int32))
    cos_s, sin_s = _rope_tables(jnp.full((Ms,), past, jnp.int32))
    pps = min(32, n_pages)
    assert n_pages % pps == 0
    bsum = _block_sum_matrices(pps)
    cmp_t = cache_cmp_kv.transpose(0, 1, 3, 4, 5, 2).reshape(n_even * n_pool, KVW, PAGE_SIZE)
    sel_t = cache_sel_kv.transpose(0, 1, 3, 4, 5, 2).reshape(n_even * n_pool, 2, KV_HEADS, HEAD_DIM, PAGE_SIZE)
    win_t = cache_win_kv.transpose(0, 1, 3, 4, 5, 2).reshape(n_even * Bd, 2 * KV_HEADS, HEAD_DIM, wb)

    xp = x_prompt.reshape(M, D)
    xs = _pad_rows(x_sample.reshape(Bd, D), Ms)
    xpb, xsb = xp.astype(BF16), xs.astype(BF16)

    outs = {k: [] for k in ("cmp_p", "cmp_s", "sel_p", "sel_s", "win_p", "win_s", "conv_p", "conv_s",
                            "pool_p", "pool_s", "sgu_p", "sgu_s")}
    kv6 = lambda a, lead: a.reshape(lead + (2, KV_HEADS, HEAD_DIM))

    for layer in range(DEPTH):
        if layer % 2 == 0:
            e = layer // 2
            wts = _prep_even_weights(w_in_even[e], w_out_even[e], cmp_pe_k[e], cmp_pe_v[e], cmp_w_k[e], cmp_w_v[e])
            qraw, qrot, (kvc, kvc_t), (kvs_b, kvs_t), (kvw_b, kvw_t), u, gates = _even_in_proj(
                xpb, wts["w_in"], cos_p, sin_p, tm=tm_p, bs=(B, S))
            summ = _compress_rows(kvc, wts["pe"], rows=512, name="prompt_compress")
            (ckv,) = _mm(summ.astype(BF16), wts["big_p"], tm=min(summ.shape[0], 512), tn=KVP,
                         n_off=0, n_cols=KVP, epilogue=_ep_plain, outs=((KVP, F32, KVP),), name="prompt_compress_map")
            o_att = _prompt_attention(qraw, qrot, gates, ckv, kvs_b, kvw_b, batch=B, seq=S, tq=256, tk=256, ngrp=2)
            c = _prompt_conv(u, conv_w[e], conv_b[e], conv_ln_g[e], conv_ln_b[e], batch=B, seq=S, ts=256)
            xp, xpb = _proj_ln([o_att, c], [wts["wo_att"], wts["wo_conv"]], xp, ln_mix_g[layer], ln_mix_b[layer],
                               tm=512, name="even_out_ln")
            rows_last = lambda a: a.reshape(B, 2, KV_HEADS, HEAD_DIM, a.shape[-1]).transpose(0, 4, 1, 2, 3)
            outs["cmp_p"].append(rows_last(kvc_t))
            outs["sel_p"].append(rows_last(kvs_t))
            outs["win_p"].append(rows_last(kvw_t[:, :, S - WINDOW:]))
            outs["conv_p"].append(u.reshape(B, S, CONV_CH)[:, S - (CONV_W - 1):])
            qraw_s, qrot_s, (kvc_s,), (kvs_s,), (kvw_s,), u_s, gates_s = _even_in_proj(
                xsb, wts["w_in"], cos_s, sin_s, tm=Ms)
            ckv_t = _compress_pages(cmp_t, page_table, wts["pe_t"], wts["big_t"], bsum, e * n_pool, pps=pps,
                                    name="sample_compress")
            q3 = qraw_s.astype(F32)[:Bd].reshape(Bd, N_HEADS, HSLOT)
            o_cmp, idx = _sample_cmp(q3, ckv_t, bd=Bd, past=past, qpos=past, nbl=pps * (PAGE_SIZE // CMP_BLOCK))
            idx = idx[:, :, 0].reshape(Bd, KV_HEADS, TOP_K)
            qr3 = _split_to_nat(qrot_s.astype(F32)[:Bd], N_HEADS).reshape(Bd, N_HEADS, HSLOT)
            q4 = jnp.pad(qr3.reshape(Bd, KV_HEADS, HPG, HSLOT), ((0, 0), (0, 0), (0, Q_PAD_ROWS - HPG), (0, 0)))
            kvs_nat = _split_to_nat(kvs_s, 2 * KV_HEADS)
            kvw_nat = _split_to_nat(kvw_s, 2 * KV_HEADS)
            o_sel = _sample_sel(sel_t, page_table, idx, e * n_pool, q4, kvs_nat, bd=Bd, past=past)
            o_sel = o_sel[:, :, :HPG].reshape(Bd, N_HEADS, HSLOT)
            g3 = gates_s[:Bd].reshape(Bd, KV_HEADS, LANES)[:, :, :3 * HPG].reshape(Bd, KV_HEADS, 3, HPG)
            g3 = g3.transpose(0, 1, 3, 2).reshape(Bd, N_HEADS, 3)
            g3 = jnp.pad(g3, ((0, 0), (0, 0), (0, LANES - 3)))
            o_s = _sample_win(qr3, win_t, e, kvw_nat, o_cmp, o_sel, g3, bd=Bd, past=past, qpos=past)
            c_s = _sample_conv(state_conv, e, u_s, conv_w[e], conv_b[e], conv_ln_g[e], conv_ln_b[e], bd=Bd)
            o_sb = _pad_rows(_nat_to_split(o_s.reshape(Bd, QW), N_HEADS), Ms).astype(BF16)
            c_sb = _pad_rows(c_s, Ms).astype(BF16)
            xs, xsb = _proj_ln([o_sb, c_sb], [wts["wo_att"], wts["wo_conv"]], xs, ln_mix_g[layer], ln_mix_b[layer],
                               tm=Ms, name="even_out_ln_s")
            kvc_c = _unpad_heads(kvc_s[:Bd], 2 * KV_HEADS)
            kvs_c = _unpad_heads(kvs_s[:Bd], 2 * KV_HEADS)
            kvw_c = _unpad_heads(kvw_s[:Bd], 2 * KV_HEADS)
            outs["cmp_s"].append(kv6(kvc_c, (Bd, 1)))
            outs["sel_s"].append(kv6(kvs_c, (Bd, 1)))
            wkv = jnp.concatenate([cache_win_kv[e], kv6(kvw_c, (Bd, 1))], axis=1)
            outs["win_s"].append(wkv[:, wkv.shape[1] - min(WINDOW, wkv.shape[1]):])
            outs["conv_s"].append(jnp.concatenate([state_conv[e], u_s[:Bd, None, :]], axis=1)[:, 1:])
        else:
            o = layer // 2
            w_in = w_in_odd[o]
            w_in_p = jnp.concatenate([w_in[:, POOL_CH + SGU_CH:], w_in[:, :POOL_CH], w_in[:, POOL_CH:POOL_CH + SGU_CH]],
                                     axis=1).astype(BF16)
            w_out_p = w_out_odd[o].astype(BF16)
            lg, lb = sgu_ln_g[o].reshape(1, SGU_CH), sgu_ln_b[o].reshape(1, SGU_CH)

            def odd_in(xb, tm):
                gl_ex = ((lg, (1, 2 * SGU_GC), lambda i, j: (0, j)), (lb, (1, 2 * SGU_GC), lambda i, j: (0, j)))
                (vn,) = _mm(xb, w_in_p, tm=tm, tn=2 * SGU_GC, n_off=O_V, n_cols=SGU_CH, epilogue=_ep_gelu_gln,
                            extras=gl_ex, outs=((SGU_CH, F32, 2 * SGU_GC),), name="odd_in_v")
                (pin,) = _mm(xb, w_in_p, tm=tm, tn=POOL_CH, n_off=O_PIN, n_cols=POOL_CH, epilogue=_ep_plain,
                             outs=((POOL_CH, F32, POOL_CH),), name="odd_in_pool")
                (uu,) = _mm(xb, w_in_p, tm=tm, tn=POOL_CH, n_off=O_U, n_cols=SGU_CH, epilogue=_ep_gelu,
                            outs=((SGU_CH, F32, POOL_CH),), name="odd_in_u")
                return vn, pin, uu

            vn, pin, uu = odd_in(xpb, tm_p)
            cat = _prompt_odd_mix(pin, uu, vn, pool_w[o], pool_scale[o], sgu_w[o], sgu_b[o], batch=B, seq=S)
            xp, xpb = _proj_ln([cat], [w_out_p], xp, ln_mix_g[layer], ln_mix_b[layer], tm=512, name="odd_out_ln")
            outs["pool_p"].append(pin.reshape(B, S, POOL_CH)[:, S - POOL_STATE:])
            outs["sgu_p"].append(vn.reshape(B, S, SGU_CH)[:, ((S - 1) // CHUNK) * CHUNK:])
            vn_s, pin_s, uu_s = odd_in(xsb, Ms)
            cat_s = _sample_odd_mix(state_pool, o, pin_s, uu_s, vn_s, pool_w[o], pool_scale[o], sgu_w[o], sgu_b[o],
                                    bd=Bd, start_pos=past)
            xs, xsb = _proj_ln([_pad_rows(cat_s, Ms).astype(BF16)], [w_out_p], xs, ln_mix_g[layer], ln_mix_b[layer],
                               tm=Ms, name="odd_out_ln_s")
            outs["pool_s"].append(jnp.concatenate([state_pool[o], pin_s[:Bd, None, :]], axis=1)[:, 1:])
            outs["sgu_s"].append(vn_s[:Bd, None, :])
        h, w1b, w2b = _mlp_up_cast(xpb, mlp_w1, mlp_w2, layer, tm=tm_p, tn=1024)
        xp, xpb = _mlp2_ln(h, w2b, xp, ln_ffn_g[layer], ln_ffn_b[layer], tm=1024, tk=512, name="mlp_down_ln")
        xs, xsb = _mlp(xs, xsb, w1b, w2b, ln_ffn_g[layer], ln_ffn_b[layer], tm1=Ms, tn1=2048, tm2=Ms, tk2=2048)

    st = lambda k: jnp.stack(outs[k])
    return (xp.reshape(B, S, D), xs[:Bd].reshape(Bd, Sd, D),
            st("cmp_p"), st("cmp_s"), st("sel_p"), st("sel_s"), st("win_p"), st("win_s"),
            st("conv_p"), st("conv_s"), st("pool_p"), st("pool_s"), st("sgu_p"), st("sgu_s"))
```

```python
import functools

import jax
import jax.numpy as jnp
from jax import lax
from jax.experimental import pallas as pl
from jax.experimental.pallas import tpu as pltpu

F32 = jnp.float32
BF16 = jnp.bfloat16

D_MODEL = 2048
DEPTH = 4
PAGE_SIZE = 128
N_HEADS = 16
HEAD_DIM = 96
KV_HEADS = 4
HPG = N_HEADS // KV_HEADS
ATT_W = N_HEADS * HEAD_DIM
KVW = 2 * KV_HEADS * HEAD_DIM
CMP_BLOCK = 32
SEL_BLOCK = 64
TOP_K = 16
WINDOW = 512
ROPE_THETA = 10000.0
SCALE = HEAD_DIM ** -0.5
LOG2E = 1.4426950408889634
FORCE = 1e9
NEG = -1e30
CONV_CH = D_MODEL // 4
CONV_W = 31
POOL_CH = D_MODEL // 4
POOL_WINDOWS = (2, 4, 8, 16)
POOL_GROUPS = len(POOL_WINDOWS)
POOL_GC = POOL_CH // POOL_GROUPS
POOL_STATE = max(POOL_WINDOWS) - 1
SGU_CH = D_MODEL - POOL_CH
SGU_GROUPS = 4
SGU_GC = SGU_CH // SGU_GROUPS
CHUNK = 128
D_FF = 4 * D_MODEL
ALPHA = (2 * DEPTH) ** 0.25
LN_EPS = 1e-5

LANES = 128
SUBLANES = 8
HALF = HEAD_DIM // 2
HSLOT = LANES
HALF_OFF = LANES // 2
QW = N_HEADS * HSLOT
KVP = 2 * KV_HEADS * HSLOT
GATE_W = KV_HEADS * LANES
SAMPLE_ROWS = 16
ROW_SUB = 256
VMEM_LIMIT = 52 * 1024 * 1024

E_Q, E_KVC, E_KVS, E_KVW = 0, QW, QW + KVP, QW + 2 * KVP
E_GLU = QW + 3 * KVP
E_GATE = E_GLU + 2 * CONV_CH
E_TOT = E_GATE + GATE_W
O_V, O_PIN, O_U = 0, SGU_CH, SGU_CH + POOL_CH


def _cparams(*sem):
    return pltpu.CompilerParams(dimension_semantics=sem, vmem_limit_bytes=VMEM_LIMIT)


def _pad_head(x):
    z = jnp.zeros(x.shape[:-1] + (HALF_OFF - HALF,), x.dtype)
    return jnp.concatenate([x[..., :HALF], z, x[..., HALF:], z], axis=-1)


def _pad_nat(x):
    return jnp.concatenate([x, jnp.zeros(x.shape[:-1] + (HSLOT - HEAD_DIM,), x.dtype)], axis=-1)


def _unpad_heads(x, nheads):
    xh = x.reshape(x.shape[:-1] + (nheads, HSLOT))
    y = jnp.concatenate([xh[..., :HALF], xh[..., HALF_OFF:HALF_OFF + HALF]], axis=-1)
    return y.reshape(x.shape[:-1] + (nheads * HEAD_DIM,))


def _rope_tables(pos):
    inv = jnp.power(ROPE_THETA, -jnp.arange(HALF, dtype=F32) / HALF)
    ang = pos.astype(F32)[:, None] * inv[None, :]
    cos, sin = jnp.cos(ang), jnp.sin(ang)
    z = jnp.zeros((pos.shape[0], HALF_OFF - HALF), F32)
    return (jnp.concatenate([cos, z, cos, z], axis=1),
            jnp.concatenate([-sin, z, sin, z], axis=1))


def _block_diag2(a, b):
    za = jnp.zeros((a.shape[0], b.shape[1]), a.dtype)
    zb = jnp.zeros((b.shape[0], a.shape[1]), a.dtype)
    return jnp.concatenate([jnp.concatenate([a, za], axis=1), jnp.concatenate([zb, b], axis=1)], axis=0)


def _prep_even_weights(w_in, w_out, pe_k, pe_v, w_ck, w_cv):
    d = w_in.shape[0]
    q = _pad_head(w_in[:, :ATT_W].reshape(d, N_HEADS, HEAD_DIM)).reshape(d, QW)
    kvs = []
    for s in range(3):
        blk = w_in[:, ATT_W + s * KVW:ATT_W + (s + 1) * KVW].reshape(d, 2 * KV_HEADS, HEAD_DIM)
        kvs.append(_pad_head(blk).reshape(d, KVP))
    g0 = ATT_W + 3 * KVW
    gates = w_in[:, g0:g0 + 3 * N_HEADS].reshape(d, 3, KV_HEADS, HPG).transpose(0, 2, 1, 3)
    gates = gates.reshape(d, KV_HEADS, 3 * HPG)
    gates = jnp.pad(gates, ((0, 0), (0, 0), (0, LANES - 3 * HPG))).reshape(d, GATE_W)
    glu = w_in[:, g0 + 3 * N_HEADS:]
    a = glu[:, :CONV_CH].reshape(d, 2, CONV_CH // 2)
    g = glu[:, CONV_CH:].reshape(d, 2, CONV_CH // 2)
    glu = jnp.stack([a, g], axis=2).reshape(d, 2 * CONV_CH)
    w_in_p = jnp.concatenate([q] + kvs + [glu, gates], axis=1).astype(BF16)
    wo_att = _pad_head(w_out[:ATT_W].reshape(N_HEADS, HEAD_DIM, d).transpose(0, 2, 1))
    wo_att = wo_att.transpose(0, 2, 1).reshape(QW, d).astype(BF16)
    wo_conv = w_out[ATT_W:].astype(BF16)
    eye = jnp.eye(KV_HEADS, dtype=F32)
    pe = jnp.concatenate([jnp.tile(_pad_head(pe_k), (1, KV_HEADS)), jnp.tile(_pad_head(pe_v), (1, KV_HEADS))], axis=1)
    wk_full = _pad_head(_pad_head(w_ck).T).T
    wv_full = _pad_head(_pad_head(w_cv).T).T
    big_p = _block_diag2(jnp.kron(eye, wk_full), jnp.kron(eye, wv_full))
    pe_t = jnp.concatenate([jnp.tile(jnp.tile(pe_k.T, (1, PAGE_SIZE // CMP_BLOCK)), (KV_HEADS, 1)),
                            jnp.tile(jnp.tile(pe_v.T, (1, PAGE_SIZE // CMP_BLOCK)), (KV_HEADS, 1))], axis=0)
    big_t = _block_diag2(jnp.kron(eye, _pad_head(w_ck).T), jnp.kron(eye, _pad_nat(w_cv).T))
    return dict(w_in=w_in_p, wo_att=wo_att, wo_conv=wo_conv, pe=pe, pe_t=pe_t,
                big_p=big_p.astype(BF16), big_t=big_t.astype(BF16))


def _block_sum_matrices(pps):
    bpp = PAGE_SIZE // CMP_BLOCK
    nbl = bpp * pps
    p = jnp.arange(pps)[:, None, None]
    i = (jnp.arange(PAGE_SIZE) // CMP_BLOCK)[None, :, None]
    c = jnp.arange(nbl)[None, None, :]
    col = (i % 2) * (nbl // 2) + (bpp // 2) * p + i // 2
    return jnp.where(c == col, 1.0 / CMP_BLOCK, 0.0).astype(BF16)


def _ln_rows(y, g, b):
    mu = jnp.mean(y, axis=-1, keepdims=True)
    yc = y - mu
    var = jnp.mean(yc * yc, axis=-1, keepdims=True)
    return yc * lax.rsqrt(var + LN_EPS) * g + b


def _rope_slot(x, cos, sin):
    return x * cos + pltpu.roll(x, HALF_OFF, axis=1) * sin


_NT = (((1,), (1,)), ((), ()))


def _pad_rows_to(x, rows):
    return jnp.concatenate([x, jnp.zeros((rows - x.shape[0],) + x.shape[1:], x.dtype)], axis=0)


def _mm(x, w, *, tm, tn, n_off, n_cols, epilogue, extras=(), outs, name):
    m, k = x.shape
    assert m % tm == 0 and n_cols % tn == 0 and n_off % tn == 0
    joff = n_off // tn
    in_specs = [pl.BlockSpec((tm, k), lambda i, j: (i, 0)),
                pl.BlockSpec((k, tn), lambda i, j: (0, joff + j))]
    in_specs += [pl.BlockSpec(bs, im) for _, bs, im in extras]
    out_shape, out_specs = [], []
    for o in outs:
        if len(o) == 3:
            out_shape.append(jax.ShapeDtypeStruct((m, o[0]), o[1]))
            out_specs.append(pl.BlockSpec((tm, o[2]), lambda i, j: (i, j)))
        else:
            out_shape.append(jax.ShapeDtypeStruct(o[0], o[1]))
            out_specs.append(pl.BlockSpec(o[2], o[3]))
    ne = len(extras)

    rsub = min(tm, ROW_SUB)

    def body(x_ref, w_ref, *refs):
        for r in range(tm // rsub):
            rows = slice(r * rsub, (r + 1) * rsub)
            acc = jnp.dot(x_ref[rows, :], w_ref[...], preferred_element_type=F32)
            epilogue(acc, refs[:ne], refs[ne:], rows)

    return pl.pallas_call(
        body, grid=(m // tm, n_cols // tn), in_specs=in_specs, out_specs=out_specs, out_shape=out_shape,
        compiler_params=_cparams("parallel", "arbitrary"), name=name,
    )(x, w, *[a for a, _, _ in extras])


def _ep_q(acc, ex, outs, rows):
    cos, sin = ex[0][rows, :], ex[1][rows, :]
    outs[0][rows, :] = acc.astype(BF16)
    for j in range(acc.shape[1] // HSLOT):
        sl = slice(j * HSLOT, (j + 1) * HSLOT)
        outs[1][rows, sl] = _rope_slot(acc[:, sl], cos, sin).astype(BF16)


def _ep_kv(acc, ex, outs, rows, *, rope, want):
    o = dict(zip(want, outs))
    if rope:
        cos, sin = ex[0][rows, :], ex[1][rows, :]
    for j in range(2 * KV_HEADS):
        sl = slice(j * HSLOT, (j + 1) * HSLOT)
        x = acc[:, sl]
        if rope and j < KV_HEADS:
            x = _rope_slot(x, cos, sin)
        if "f32" in o:
            o["f32"][rows, sl] = x
        if "bf16" in o:
            o["bf16"][rows, sl] = x.astype(BF16)
        if "t" in o:
            xt = x.T
            o["t"][j * HEAD_DIM:j * HEAD_DIM + HALF, rows] = xt[0:HALF]
            o["t"][j * HEAD_DIM + HALF:(j + 1) * HEAD_DIM, rows] = xt[HALF_OFF:HALF_OFF + HALF]


def _ep_glu(acc, ex, outs, rows):
    h = acc.shape[1] // 2
    outs[0][rows, :] = acc[:, :h] * jax.nn.sigmoid(acc[:, h:])


def _ep_sigmoid(acc, ex, outs, rows):
    outs[0][rows, :] = jax.nn.sigmoid(acc)


def _ep_plain(acc, ex, outs, rows):
    outs[0][rows, :] = acc.astype(outs[0].dtype)


def _ep_relu2(acc, ex, outs, rows):
    r = jnp.maximum(acc, 0.0)
    outs[0][rows, :] = (r * r).astype(outs[0].dtype)


def _ep_gelu(acc, ex, outs, rows):
    outs[0][rows, :] = jax.nn.gelu(acc)


def _ep_gelu_gln(acc, ex, outs, rows):
    g, b = ex[0][...], ex[1][...]
    v = jax.nn.gelu(acc)
    for j in range(acc.shape[1] // SGU_GC):
        sl = slice(j * SGU_GC, (j + 1) * SGU_GC)
        outs[0][rows, sl] = _ln_rows(v[:, sl], g[:, sl], b[:, sl])


def _proj_ln(a_list, w_list, resid, g, b, *, tm, name):
    m, n = resid.shape
    npair = len(a_list)
    in_specs = []
    for a in a_list:
        in_specs.append(pl.BlockSpec((tm, a.shape[1]), lambda i: (i, 0)))
    for w in w_list:
        in_specs.append(pl.BlockSpec(w.shape, lambda i: (0, 0), pipeline_mode=pl.Buffered(1)))
    in_specs += [pl.BlockSpec((tm, n), lambda i: (i, 0)),
                 pl.BlockSpec((1, n), lambda i: (0, 0)), pl.BlockSpec((1, n), lambda i: (0, 0))]

    def body(*refs):
        a_refs, w_refs = refs[:npair], refs[npair:2 * npair]
        r_ref, g_ref, b_ref, o_ref, ob_ref = refs[2 * npair:]
        rsub = min(tm, ROW_SUB)
        for r in range(tm // rsub):
            rows = slice(r * rsub, (r + 1) * rsub)
            acc = ALPHA * r_ref[rows, :]
            for a_ref, w_ref in zip(a_refs, w_refs):
                acc = acc + jnp.dot(a_ref[rows, :], w_ref[...], preferred_element_type=F32)
            y = _ln_rows(acc, g_ref[...], b_ref[...])
            o_ref[rows, :] = y
            ob_ref[rows, :] = y.astype(BF16)

    return pl.pallas_call(
        body, grid=(m // tm,), in_specs=in_specs,
        out_specs=[pl.BlockSpec((tm, n), lambda i: (i, 0)), pl.BlockSpec((tm, n), lambda i: (i, 0))],
        out_shape=[jax.ShapeDtypeStruct((m, n), F32), jax.ShapeDtypeStruct((m, n), BF16)],
        compiler_params=_cparams("parallel"), name=name,
    )(*a_list, *w_list, resid, g.reshape(1, n), b.reshape(1, n))


def _mlp2_ln(h, w2, resid, g, b, *, tm, tn, name):
    m, kf = h.shape
    n = w2.shape[1]
    nj = n // tn
    rsub = min(tm, ROW_SUB)

    def body(h_ref, w_ref, r_ref, g_ref, b_ref, o_ref, ob_ref):
        j = pl.program_id(1)
        for r in range(tm // rsub):
            rows = slice(r * rsub, (r + 1) * rsub)
            y = ALPHA * r_ref[rows, :] + jnp.dot(h_ref[rows, :], w_ref[...], preferred_element_type=F32)
            for jj in range(nj):
                @pl.when(j == jj)
                def _(y=y, rows=rows, jj=jj):
                    o_ref[rows, jj * tn:(jj + 1) * tn] = y

        @pl.when(j == nj - 1)
        def _():
            for r in range(tm // rsub):
                rows = slice(r * rsub, (r + 1) * rsub)
                y = _ln_rows(o_ref[rows, :], g_ref[...], b_ref[...])
                o_ref[rows, :] = y
                ob_ref[rows, :] = y.astype(BF16)

    return pl.pallas_call(
        body, grid=(m // tm, nj),
        in_specs=[pl.BlockSpec((tm, kf), lambda i, j: (i, 0)), pl.BlockSpec((kf, tn), lambda i, j: (0, j)),
                  pl.BlockSpec((tm, tn), lambda i, j: (i, j)),
                  pl.BlockSpec((1, n), lambda i, j: (0, 0)), pl.BlockSpec((1, n), lambda i, j: (0, 0))],
        out_specs=[pl.BlockSpec((tm, n), lambda i, j: (i, 0)), pl.BlockSpec((tm, n), lambda i, j: (i, 0))],
        out_shape=[jax.ShapeDtypeStruct((m, n), F32), jax.ShapeDtypeStruct((m, n), BF16)],
        compiler_params=_cparams("parallel", "arbitrary"), name=name,
    )(h, w2, resid, g.reshape(1, n), b.reshape(1, n))


def _compress_rows(kvc, pe, *, rows, name):
    m, c = kvc.shape
    nb = rows // CMP_BLOCK

    def body(x_ref, pe_ref, o_ref):
        x = x_ref[...].reshape(nb, CMP_BLOCK, c) * pe_ref[...][None]
        o_ref[...] = jnp.sum(x, axis=1) * (1.0 / CMP_BLOCK)

    return pl.pallas_call(
        body, grid=(m // rows,),
        in_specs=[pl.BlockSpec((rows, c), lambda i: (i, 0)), pl.BlockSpec((CMP_BLOCK, c), lambda i: (0, 0))],
        out_specs=pl.BlockSpec((nb, c), lambda i: (i, 0)),
        out_shape=jax.ShapeDtypeStruct((m // CMP_BLOCK, c), F32),
        compiler_params=_cparams("parallel"), name=name,
    )(kvc, pe)


def _compress_pages(pages_t, page_table, pe_t, big_t, bsum, layer_base, *, pps, name):
    bd, n_pages = page_table.shape
    bpp = PAGE_SIZE // CMP_BLOCK
    nbl = bpp * pps
    nsteps = n_pages // pps

    def body(pt_ref, *refs):
        page_refs = refs[:pps]
        pe_ref, big_ref, bsum_ref, o_ref = refs[pps:]
        acc = jnp.zeros((KVW, nbl), F32)
        for p in range(pps):
            x = page_refs[p][...] * pe_ref[...]
            acc = acc + jnp.dot(x.astype(BF16), bsum_ref[p], preferred_element_type=F32)
        o_ref[...] = jnp.dot(big_ref[...], acc.astype(BF16), preferred_element_type=F32)

    def page_map(p):
        return lambda b, j, pt: (layer_base + pt[b, j * pps + p], 0, 0)

    cst2 = lambda b, j, pt: (0, 0)
    grid_spec = pltpu.PrefetchScalarGridSpec(
        num_scalar_prefetch=1, grid=(bd, nsteps),
        in_specs=[pl.BlockSpec((None, KVW, PAGE_SIZE), page_map(p)) for p in range(pps)]
        + [pl.BlockSpec((KVW, PAGE_SIZE), cst2), pl.BlockSpec((KVP, KVW), cst2),
           pl.BlockSpec((pps, PAGE_SIZE, nbl), lambda b, j, pt: (0, 0, 0))],
        out_specs=pl.BlockSpec((None, KVP, nbl), lambda b, j, pt: (b, 0, j)))
    return pl.pallas_call(
        body, grid_spec=grid_spec,
        out_shape=jax.ShapeDtypeStruct((bd, KVP, n_pages * bpp), F32),
        compiler_params=_cparams("parallel", "arbitrary"), name=name,
    )(page_table, *([pages_t] * pps), pe_t, big_t, bsum)


def _flash_step_t(q, k, vt, valid, m_ref, l_ref, acc_ref):
    bias = jnp.where(valid, 0.0, NEG)
    s = jnp.dot(k, q, preferred_element_type=F32) + jnp.concatenate([bias] * HPG, axis=1)
    m_prev = m_ref[...]
    m_new = jnp.maximum(m_prev, jnp.max(s, axis=0, keepdims=True))
    alpha = jnp.exp2(m_prev - m_new)
    p = jnp.exp2(s - m_new)
    l_ref[...] = alpha * l_ref[...] + jnp.sum(p, axis=0, keepdims=True)
    acc_ref[...] = alpha * acc_ref[...] + jnp.dot(vt, p.astype(BF16), preferred_element_type=F32)
    m_ref[...] = m_new


def _attn_body(*refs, tq, tk, seq, k_top, ngrp):
    n_in = 9
    ins = [refs[g * n_in:(g + 1) * n_in] for g in range(ngrp)]
    o_ref = refs[ngrp * n_in]
    m_ref, l_ref, acc_ref, sel_ref, vts_ref, vtw_ref = refs[ngrp * n_in + 1:]
    i = pl.program_id(2)
    cols = HPG * tq
    nsb = seq // SEL_BLOCK
    nt = seq // tk
    bpt = tk // SEL_BLOCK
    q0 = i * tq
    lane_q = lax.broadcasted_iota(jnp.int32, (1, cols), 1)
    qpos = q0 + (lane_q & (tq - 1))
    qp1 = q0 + lax.broadcasted_iota(jnp.int32, (1, tq), 1)

    @pl.when(i == 0)
    def _():
        def tr(t, c):
            for g in range(ngrp):
                vs_ref, vw_ref = ins[g][6], ins[g][8]
                for h in range(tk // LANES):
                    k0 = pl.multiple_of(t * tk + h * LANES, LANES)
                    hs = slice(h * LANES, (h + 1) * LANES)
                    vts_ref[g, t, :, hs] = vs_ref[pl.ds(k0, LANES), :].astype(F32).T.astype(BF16)
                    vtw_ref[g, t, :, hs] = vw_ref[pl.ds(k0, LANES), :].astype(F32).T.astype(BF16)
            return c
        lax.fori_loop(0, nt, tr, 0)

    def heads_t(ref, scale=1.0):
        parts = [ref[:, j * HSLOT:(j + 1) * HSLOT].astype(F32).T * scale for j in range(HPG)]
        return jnp.concatenate(parts, axis=1).astype(BF16)

    r = lax.broadcasted_iota(jnp.int32, (2 * nsb, 1), 0)
    n_of = jnp.where(r < nsb, 2 * r, 2 * (r - nsb) + 1)
    mk = ((n_of + 1) * CMP_BLOCK - 1) <= qpos
    sb = lax.broadcasted_iota(jnp.int32, (nsb, 1), 0)
    vis = (sb * SEL_BLOCK) <= qp1
    cur = sb == (qp1 >> (SEL_BLOCK.bit_length() - 1))

    def compressed_branch(g):
        qraw_ref, ck_ref, cv_ref = ins[g][0], ins[g][3], ins[g][4]
        qr = heads_t(qraw_ref)
        ck = jnp.concatenate([ck_ref[pl.ds(0, nsb, stride=2), :], ck_ref[pl.ds(1, nsb, stride=2), :]], axis=0)
        s = jnp.dot(ck.astype(BF16), qr, preferred_element_type=F32) * SCALE
        s = jnp.where(mk, s, NEG)
        mx = jnp.max(s, axis=0, keepdims=True)
        p = jnp.where(mk, jnp.exp(s - mx), 0.0)
        pn = p / jnp.maximum(jnp.sum(p, axis=0, keepdims=True), 1e-30)
        cv = jnp.concatenate([cv_ref[pl.ds(0, nsb, stride=2), :], cv_ref[pl.ds(1, nsb, stride=2), :]], axis=0)
        cvt = _pad_rows_to(cv, LANES).T.astype(BF16)
        o_cmp = jnp.dot(cvt, _pad_rows_to(pn, LANES).astype(BF16), preferred_element_type=F32)
        pp = pn[0:nsb] + pn[nsb:2 * nsb]
        imp = pp[:, 0:tq]
        for j in range(1, HPG):
            imp = imp + pp[:, j * tq:(j + 1) * tq]
        imp = jnp.where(cur, FORCE, jnp.where(vis, imp, -FORCE))
        cnt = jnp.zeros((nsb, tq), F32)
        for j in range(nsb):
            rowj = imp[j:j + 1, :]
            beats = (rowj > imp) | ((rowj == imp) & (j < sb))
            cnt = cnt + jnp.where(beats, 1.0, 0.0)
        sel = jnp.where(cnt < k_top, 1.0, 0.0)
        for t in range(nt):
            sel_ref[g, t, 0:bpt, :] = sel[bpt * t:bpt * (t + 1), :]
        return o_cmp

    o_cmp = [compressed_branch(g) for g in range(ngrp)]
    qt = [heads_t(ins[g][1], SCALE * LOG2E) for g in range(ngrp)]
    rowk = lax.broadcasted_iota(jnp.int32, (tk, 1), 0)

    def reset():
        m_ref[...] = jnp.full((ngrp, 1, cols), NEG, F32)
        l_ref[...] = jnp.zeros((ngrp, 1, cols), F32)
        acc_ref[...] = jnp.zeros((ngrp, HSLOT, cols), F32)

    def result(g):
        return acc_ref[g] / jnp.maximum(l_ref[g], 1e-30)

    reset()

    def sel_step(t, c):
        k0 = pl.multiple_of(t * tk, tk)
        kp = k0 + rowk
        for g in range(ngrp):
            sm = sel_ref[g, t, bpt - 1:bpt, :]
            for j in range(bpt - 2, -1, -1):
                sm = jnp.where(rowk < (j + 1) * SEL_BLOCK, sel_ref[g, t, j:j + 1, :], sm)
            valid = (kp <= qp1) & (sm > 0.5)
            _flash_step_t(qt[g], ins[g][5][pl.ds(k0, tk), :], vts_ref[g, t], valid,
                          m_ref.at[g], l_ref.at[g], acc_ref.at[g])
        return c

    t_end = lax.div(q0 + tq + tk - 1, tk)
    lax.fori_loop(0, t_end, sel_step, 0)
    o_sel = [result(g) for g in range(ngrp)]

    reset()

    def win_step(t, c):
        k0 = pl.multiple_of(t * tk, tk)
        kp = k0 + rowk
        valid = (kp <= qp1) & (kp >= qp1 - WINDOW)
        for g in range(ngrp):
            _flash_step_t(qt[g], ins[g][7][pl.ds(k0, tk), :], vtw_ref[g, t], valid,
                          m_ref.at[g], l_ref.at[g], acc_ref.at[g])
        return c

    lax.fori_loop(lax.div(jnp.maximum(q0 - WINDOW, 0), tk), t_end, win_step, 0)

    gw = HPG * HSLOT
    for g in range(ngrp):
        gt = ins[g][2][...].T
        o = jnp.zeros((HSLOT, cols), F32)
        for br, o_br in enumerate((o_cmp[g], o_sel[g], result(g))):
            grow = jnp.concatenate([gt[br * HPG + j:br * HPG + j + 1, :] for j in range(HPG)], axis=1)
            o = o + grow * o_br
        for j in range(HPG):
            o_ref[:, g * gw + j * HSLOT:g * gw + (j + 1) * HSLOT] = o[:, j * tq:(j + 1) * tq].T.astype(BF16)


def _prompt_attention(qraw, qrot, gates, ckv, kvs_b, kvw_b, *, batch, seq, tq, tk, ngrp):
    nq = seq // tq
    ncb = seq // CMP_BLOCK
    nt = seq // tk
    cols = HPG * tq
    k_top = min(TOP_K, seq // SEL_BLOCK)
    assert tq & (tq - 1) == 0 and seq % tk == 0 and tk % LANES == 0 and tk // SEL_BLOCK <= SUBLANES
    assert KV_HEADS % ngrp == 0
    body = functools.partial(_attn_body, tq=tq, tk=tk, seq=seq, k_top=k_top, ngrp=ngrp)
    gw = HPG * HSLOT
    in_specs, operands = [], []
    for g in range(ngrp):
        qmap = functools.partial(lambda b, p, i, g: (b * nq + i, p * ngrp + g), g=g)
        kmap = functools.partial(lambda b, p, i, g: (b, p * ngrp + g), g=g)
        vmap_ = functools.partial(lambda b, p, i, g: (b, KV_HEADS + p * ngrp + g), g=g)
        in_specs += [pl.BlockSpec((tq, gw), qmap), pl.BlockSpec((tq, gw), qmap), pl.BlockSpec((tq, LANES), qmap),
                     pl.BlockSpec((ncb, HSLOT), kmap), pl.BlockSpec((ncb, HSLOT), vmap_),
                     pl.BlockSpec((seq, HSLOT), kmap), pl.BlockSpec((seq, HSLOT), vmap_),
                     pl.BlockSpec((seq, HSLOT), kmap), pl.BlockSpec((seq, HSLOT), vmap_)]
        operands += [qraw, qrot, gates, ckv, ckv, kvs_b, kvs_b, kvw_b, kvw_b]
    return pl.pallas_call(
        body, grid=(batch, KV_HEADS // ngrp, nq), in_specs=in_specs,
        out_specs=pl.BlockSpec((tq, ngrp * gw), lambda b, p, i: (b * nq + i, p)),
        out_shape=jax.ShapeDtypeStruct((batch * seq, QW), BF16),
        scratch_shapes=[pltpu.VMEM((ngrp, 1, cols), F32), pltpu.VMEM((ngrp, 1, cols), F32),
                        pltpu.VMEM((ngrp, HSLOT, cols), F32), pltpu.VMEM((ngrp, nt, SUBLANES, tq), F32),
                        pltpu.VMEM((ngrp, nt, HSLOT, tk), BF16), pltpu.VMEM((ngrp, nt, HSLOT, tk), BF16)],
        compiler_params=_cparams("parallel", "parallel", "arbitrary"), name="prompt_attention",
    )(*operands)


CONV_HALO = 32


def _conv_body(cur_ref, prev_ref, w_ref, b_ref, g_ref, bn_ref, o_ref, ext_ref, *, ts):
    c = pl.program_id(1)
    ext_ref[0:CONV_HALO, :] = jnp.where(c > 0, prev_ref[...], 0.0)
    ext_ref[CONV_HALO:CONV_HALO + ts, :] = cur_ref[...]
    acc = jnp.zeros((ts, CONV_CH), F32) + b_ref[...]
    off = CONV_HALO - (CONV_W - 1)
    for k in range(CONV_W):
        acc = acc + ext_ref[pl.ds(off + k, ts), :] * w_ref[k:k + 1, :]
    y = _ln_rows(acc, g_ref[...], bn_ref[...])
    o_ref[...] = (y * jax.nn.sigmoid(y)).astype(BF16)


def _prompt_conv(u, cw, cb, cg, cbn, *, batch, seq, ts):
    nt = seq // ts
    r = ts // CONV_HALO
    cwp = jnp.pad(cw, ((0, CONV_HALO - CONV_W), (0, 0)))
    vec = lambda a: a.reshape(1, CONV_CH)
    cst = lambda b, c: (0, 0)
    return pl.pallas_call(
        functools.partial(_conv_body, ts=ts), grid=(batch, nt),
        in_specs=[pl.BlockSpec((ts, CONV_CH), lambda b, c: (b * nt + c, 0)),
                  pl.BlockSpec((CONV_HALO, CONV_CH), lambda b, c: (jnp.maximum((b * nt + c) * r - 1, 0), 0)),
                  pl.BlockSpec((CONV_HALO, CONV_CH), cst),
                  pl.BlockSpec((1, CONV_CH), cst), pl.BlockSpec((1, CONV_CH), cst), pl.BlockSpec((1, CONV_CH), cst)],
        out_specs=pl.BlockSpec((ts, CONV_CH), lambda b, c: (b * nt + c, 0)),
        out_shape=jax.ShapeDtypeStruct((batch * seq, CONV_CH), BF16),
        scratch_shapes=[pltpu.VMEM((CONV_HALO + ts, CONV_CH), F32)],
        compiler_params=_cparams("parallel", "arbitrary"), name="prompt_conv",
    )(u, u, cwp, vec(cb), vec(cg), vec(cbn))


POOL_HALO = 16


def _odd_mix_body(pin_ref, prev_ref, u_ref, vn_ref, pw_ref, ps_ref, sw_ref, sb_ref, o_ref, ext_ref):
    c = pl.program_id(1)
    ext_ref[0:POOL_HALO, :] = jnp.where(c > 0, prev_ref[...], 0.0)
    ext_ref[POOL_HALO:POOL_HALO + CHUNK, :] = pin_ref[...]
    t = c * CHUNK + lax.broadcasted_iota(jnp.int32, (CHUNK, 1), 0)
    for g, w in enumerate(POOL_WINDOWS):
        sl = slice(g * POOL_GC, (g + 1) * POOL_GC)
        tot = ext_ref[pl.ds(POOL_HALO, CHUNK), sl]
        for j in range(1, w):
            tot = tot + ext_ref[pl.ds(POOL_HALO - j, CHUNK), sl]
        cnt = jnp.minimum(w, t + 1).astype(F32)
        d = tot / cnt - pin_ref[:, sl]
        y = jnp.dot(d.astype(BF16), pw_ref[g], preferred_element_type=F32)
        o_ref[:, sl] = (y * ps_ref[:, sl]).astype(BF16)
    ri = lax.broadcasted_iota(jnp.int32, (CHUNK, CHUNK), 0)
    ci = lax.broadcasted_iota(jnp.int32, (CHUNK, CHUNK), 1)
    for g in range(SGU_GROUPS):
        sl = slice(g * SGU_GC, (g + 1) * SGU_GC)
        ws = jnp.where(ci <= ri, sw_ref[g], 0.0).astype(BF16)
        mixed = jnp.dot(ws, vn_ref[:, sl].astype(BF16), preferred_element_type=F32) + sb_ref[:, g:g + 1]
        o_ref[:, POOL_CH + g * SGU_GC:POOL_CH + (g + 1) * SGU_GC] = (u_ref[:, sl] * mixed).astype(BF16)


def _prompt_odd_mix(pin, u, vn, pool_w, pool_scale, sgu_w, sgu_b, *, batch, seq):
    nt = seq // CHUNK
    r = CHUNK // POOL_HALO
    cst2 = lambda b, c: (0, 0)
    cst3 = lambda b, c: (0, 0, 0)
    row = lambda b, c: (b * nt + c, 0)
    return pl.pallas_call(
        _odd_mix_body, grid=(batch, nt),
        in_specs=[pl.BlockSpec((CHUNK, POOL_CH), row),
                  pl.BlockSpec((POOL_HALO, POOL_CH), lambda b, c: (jnp.maximum((b * nt + c) * r - 1, 0), 0)),
                  pl.BlockSpec((CHUNK, SGU_CH), row), pl.BlockSpec((CHUNK, SGU_CH), row),
                  pl.BlockSpec((POOL_GROUPS, POOL_GC, POOL_GC), cst3), pl.BlockSpec((1, POOL_CH), cst2),
                  pl.BlockSpec((SGU_GROUPS, CHUNK, CHUNK), cst3), pl.BlockSpec((CHUNK, SGU_GROUPS), cst2)],
        out_specs=pl.BlockSpec((CHUNK, D_MODEL), row),
        out_shape=jax.ShapeDtypeStruct((batch * seq, D_MODEL), BF16),
        scratch_shapes=[pltpu.VMEM((POOL_HALO + CHUNK, POOL_CH), F32)],
        compiler_params=_cparams("parallel", "arbitrary"), name="prompt_pool_sgu",
    )(pin, pin, u, vn, pool_w.astype(BF16), pool_scale.reshape(1, POOL_CH), sgu_w, sgu_b.T)


def _group_rows(nrows):
    return lax.broadcasted_iota(jnp.int32, (nrows, 1), 0) >> (HPG.bit_length() - 1)


def _sample_cmp_body(q_ref, ckv_ref, o_ref, idx_ref, *, qpos, ncb, nbl, k_past):
    q = q_ref[...].astype(BF16)
    rg = _group_rows(N_HEADS)
    half = nbl // 2
    assert nbl & (nbl - 1) == 0
    sh = nbl.bit_length() - 1
    lane = lax.broadcasted_iota(jnp.int32, (1, ncb), 1)
    grp, w = lane >> sh, lane & (nbl - 1)
    n_cmp = grp * nbl + 2 * (w & (half - 1)) + (w >> (sh - 1))
    mk = jnp.broadcast_to(((n_cmp + 1) * CMP_BLOCK - 1) <= qpos, (N_HEADS, ncb))
    s = jnp.zeros((N_HEADS, ncb), F32)
    for g in range(KV_HEADS):
        ck = ckv_ref[g * HSLOT:(g + 1) * HSLOT, :].astype(BF16)
        s = jnp.where(rg == g, jnp.dot(q, ck, preferred_element_type=F32) * SCALE, s)
    s = jnp.where(mk, s, NEG)
    mx = jnp.max(s, axis=-1, keepdims=True)
    p = jnp.where(mk, jnp.exp(s - mx), 0.0)
    pn = p / jnp.maximum(jnp.sum(p, axis=-1, keepdims=True), 1e-30)
    o = jnp.zeros((N_HEADS, HSLOT), F32)
    for g in range(KV_HEADS):
        cv = ckv_ref[(KV_HEADS + g) * HSLOT:(KV_HEADS + g + 1) * HSLOT, :].astype(BF16)
        o = jnp.where(rg == g, lax.dot_general(pn.astype(BF16), cv, _NT, preferred_element_type=F32), o)
    o_ref[...] = o
    pair = pn + pltpu.roll(pn, ncb - half, axis=1)
    valid = w < half
    sb = grp * half + w
    vis = (sb * SEL_BLOCK) <= qpos
    ri = lax.broadcasted_iota(jnp.int32, (ncb, ncb), 0)
    sb_r = (ri >> sh) * half + (ri & (nbl - 1))
    sb_c = jnp.broadcast_to(sb, (ncb, ncb))
    slot = lax.broadcasted_iota(jnp.int32, (TOP_K, 1), 0)
    for g in range(KV_HEADS):
        imp = jnp.sum(jnp.where(rg == g, pair, 0.0), axis=0, keepdims=True)
        imp = jnp.where(valid, jnp.where(vis, imp, -FORCE), -2.0 * FORCE)
        a = jnp.broadcast_to(imp, (ncb, ncb))
        bt = a.T
        beats = (bt > a) | ((bt == a) & (sb_r < sb_c))
        rank = jnp.sum(jnp.where(beats, 1.0, 0.0), axis=0, keepdims=True)
        onehot = jnp.where((rank == slot.astype(F32)) & valid, 1.0, 0.0)
        idx = jnp.sum(onehot * sb.astype(F32), axis=-1, keepdims=True)
        idx = jnp.where(slot < k_past, idx, 0.0)
        idx_ref[g * TOP_K:(g + 1) * TOP_K, :] = jnp.broadcast_to(idx, (TOP_K, LANES)).astype(jnp.int32)


def _sample_cmp(q3, ckv_t, *, bd, past, qpos, nbl):
    ncb = past // CMP_BLOCK
    k_past = min(TOP_K - 1, past // SEL_BLOCK)
    body = functools.partial(_sample_cmp_body, qpos=qpos, ncb=ncb, nbl=nbl, k_past=k_past)
    head3 = pl.BlockSpec((None, N_HEADS, HSLOT), lambda b: (b, 0, 0))
    return pl.pallas_call(
        body, grid=(bd,),
        in_specs=[head3, pl.BlockSpec((None, KVP, ncb), lambda b: (b, 0, 0))],
        out_specs=[head3, pl.BlockSpec((None, KV_HEADS * TOP_K, LANES), lambda b: (b, 0, 0))],
        out_shape=[jax.ShapeDtypeStruct((bd, N_HEADS, HSLOT), F32),
                   jax.ShapeDtypeStruct((bd, KV_HEADS * TOP_K, LANES), jnp.int32)],
        compiler_params=_cparams("parallel"), name="sample_cmp_attention",
    )(q3, ckv_t)


Q_PAD_ROWS = 8


def _pad_dt(x):
    return _pad_rows_to(x, HSLOT)


def _sample_sel_body(pt_ref, idx_ref, *refs, k_past):
    k_refs, v_refs = refs[:k_past], refs[k_past:2 * k_past]
    q_ref, knew_ref, vnew_ref, o_ref = refs[2 * k_past:]
    b, g = pl.program_id(0), pl.program_id(1)
    bpp = PAGE_SIZE // SEL_BLOCK
    q = q_ref[...].astype(BF16)
    half_of_lane = lax.broadcasted_iota(jnp.int32, (1, PAGE_SIZE), 1) // SEL_BLOCK
    s_parts, m_parts = [], []
    for s in range(k_past):
        kt = _pad_dt(k_refs[s][...]).astype(BF16)
        s_parts.append(jnp.dot(q, kt, preferred_element_type=F32) * SCALE)
        m_parts.append(jnp.broadcast_to(half_of_lane == idx_ref[b, g, s] % bpp, (Q_PAD_ROWS, PAGE_SIZE)))
    s_old = jnp.concatenate(s_parts, axis=1)
    mk = jnp.concatenate(m_parts, axis=1)
    s_old = jnp.where(mk, s_old, NEG)
    s_all = lax.dot_general(q, knew_ref[...].astype(BF16), _NT, preferred_element_type=F32) * SCALE
    lane = lax.broadcasted_iota(jnp.int32, s_all.shape, 1)
    s_new = jnp.sum(jnp.where(lane == b, s_all, 0.0), axis=-1, keepdims=True)
    mx = jnp.maximum(jnp.max(s_old, axis=-1, keepdims=True), s_new)
    p_old = jnp.where(mk, jnp.exp(s_old - mx), 0.0)
    p_new = jnp.exp(s_new - mx)
    den = jnp.maximum(jnp.sum(p_old, axis=-1, keepdims=True) + p_new, 1e-30)
    v_new = vnew_ref[pl.ds(b, 1), :].astype(BF16).astype(F32)
    o = p_new.astype(BF16).astype(F32) * v_new
    for s in range(k_past):
        vt = _pad_dt(v_refs[s][...]).astype(BF16)
        ps = p_old[:, s * PAGE_SIZE:(s + 1) * PAGE_SIZE].astype(BF16)
        o = o + lax.dot_general(ps, vt, _NT, preferred_element_type=F32)
    o_ref[...] = o / den


def _sample_sel(pages5, page_table, idx, layer_base, q4, kvs_new, *, bd, past):
    k_past = min(TOP_K - 1, past // SEL_BLOCK)
    bpp = PAGE_SIZE // SEL_BLOCK

    def blk_map(s, kv):
        def f(b, g, pt, ix):
            return (layer_base + pt[b, ix[b, g, s] // bpp], kv, g, 0, 0)
        return f

    tile = lambda s, kv: pl.BlockSpec((None, None, None, HEAD_DIM, PAGE_SIZE), blk_map(s, kv))
    grid_spec = pltpu.PrefetchScalarGridSpec(
        num_scalar_prefetch=2, grid=(bd, KV_HEADS),
        in_specs=[tile(s, 0) for s in range(k_past)] + [tile(s, 1) for s in range(k_past)]
        + [pl.BlockSpec((None, None, Q_PAD_ROWS, HSLOT), lambda b, g, pt, ix: (b, g, 0, 0)),
           pl.BlockSpec((SAMPLE_ROWS, HSLOT), lambda b, g, pt, ix: (0, g)),
           pl.BlockSpec((SAMPLE_ROWS, HSLOT), lambda b, g, pt, ix: (0, KV_HEADS + g))],
        out_specs=pl.BlockSpec((None, None, Q_PAD_ROWS, HSLOT), lambda b, g, pt, ix: (b, g, 0, 0)))
    return pl.pallas_call(
        functools.partial(_sample_sel_body, k_past=k_past), grid_spec=grid_spec,
        out_shape=jax.ShapeDtypeStruct((bd, KV_HEADS, Q_PAD_ROWS, HSLOT), F32),
        compiler_params=_cparams("parallel", "arbitrary"), name="sample_sel_attention",
    )(page_table, idx, *([pages5] * (2 * k_past)), q4, kvs_new, kvs_new)


def _sample_win_body(q_ref, win_ref, new_ref, ocmp_ref, osel_ref, gate_ref, o_ref, *, qpos, past, wb):
    b = pl.program_id(0)
    q = q_ref[...].astype(BF16)
    qf = q.astype(F32)
    rg = _group_rows(N_HEADS)
    new = new_ref[pl.ds(b, 1), :].astype(BF16).astype(F32)
    kpos = (past - wb) + lax.broadcasted_iota(jnp.int32, (1, wb), 1)
    mk = jnp.broadcast_to((kpos <= qpos) & (kpos >= qpos - WINDOW), (N_HEADS, wb))
    s_old = jnp.zeros((N_HEADS, wb), F32)
    s_new = jnp.zeros((N_HEADS, 1), F32)
    for g in range(KV_HEADS):
        kt = _pad_dt(win_ref[g]).astype(BF16)
        s_old = jnp.where(rg == g, jnp.dot(q, kt, preferred_element_type=F32) * SCALE, s_old)
        sn = jnp.sum(qf * new[:, g * HSLOT:(g + 1) * HSLOT], axis=-1, keepdims=True) * SCALE
        s_new = jnp.where(rg == g, sn, s_new)
    s_old = jnp.where(mk, s_old, NEG)
    mx = jnp.maximum(jnp.max(s_old, axis=-1, keepdims=True), s_new)
    p_old = jnp.where(mk, jnp.exp(s_old - mx), 0.0)
    p_new = jnp.exp(s_new - mx)
    den = jnp.maximum(jnp.sum(p_old, axis=-1, keepdims=True) + p_new, 1e-30)
    o_win = jnp.zeros((N_HEADS, HSLOT), F32)
    for g in range(KV_HEADS):
        vt = _pad_dt(win_ref[KV_HEADS + g]).astype(BF16)
        og = lax.dot_general(p_old.astype(BF16), vt, _NT, preferred_element_type=F32)
        og = og + p_new.astype(BF16).astype(F32) * new[:, (KV_HEADS + g) * HSLOT:(KV_HEADS + g + 1) * HSLOT]
        o_win = jnp.where(rg == g, og, o_win)
    o_win = o_win / den
    gts = gate_ref[...]
    o_ref[...] = gts[:, 0:1] * ocmp_ref[...] + gts[:, 1:2] * osel_ref[...] + gts[:, 2:3] * o_win


def _sample_win(q3, win4, layer, kvw_new, o_cmp, o_sel, gates3, *, bd, past, qpos):
    wb = win4.shape[-1]
    head3 = pl.BlockSpec((None, N_HEADS, HSLOT), lambda b: (b, 0, 0))
    return pl.pallas_call(
        functools.partial(_sample_win_body, qpos=qpos, past=past, wb=wb), grid=(bd,),
        in_specs=[head3, pl.BlockSpec((None, 2 * KV_HEADS, HEAD_DIM, wb), lambda b: (layer * bd + b, 0, 0, 0)),
                  pl.BlockSpec((SAMPLE_ROWS, KVP), lambda b: (0, 0)), head3, head3, head3],
        out_specs=head3,
        out_shape=jax.ShapeDtypeStruct((bd, N_HEADS, HSLOT), F32),
        compiler_params=_cparams("parallel"), name="sample_win_attention",
    )(q3, win4, kvw_new, o_cmp, o_sel, gates3)


def _sample_conv_body(st_ref, u_ref, w_ref, b_ref, g_ref, bn_ref, o_ref, *, bd):
    w = w_ref[...]
    y = jnp.sum(st_ref[...] * w[None, :CONV_W - 1, :], axis=1) + u_ref[0:bd, :] * w[CONV_W - 1:CONV_W, :] + b_ref[...]
    y = _ln_rows(y, g_ref[...], bn_ref[...])
    o_ref[...] = y * jax.nn.sigmoid(y)


def _sample_conv(state, layer, u, cw, cb, cg, cbn, *, bd):
    vec = lambda a: a.reshape(1, CONV_CH)
    cst = lambda i: (0, 0)
    return pl.pallas_call(
        functools.partial(_sample_conv_body, bd=bd), grid=(1,),
        in_specs=[pl.BlockSpec((None, bd, CONV_W - 1, CONV_CH), lambda i: (layer, 0, 0, 0)),
                  pl.BlockSpec((SAMPLE_ROWS, CONV_CH), cst), pl.BlockSpec((CONV_W, CONV_CH), cst),
                  pl.BlockSpec((1, CONV_CH), cst), pl.BlockSpec((1, CONV_CH), cst), pl.BlockSpec((1, CONV_CH), cst)],
        out_specs=pl.BlockSpec((bd, CONV_CH), cst),
        out_shape=jax.ShapeDtypeStruct((bd, CONV_CH), F32),
        compiler_params=_cparams("arbitrary"), name="sample_conv",
    )(state, u, cw, vec(cb), vec(cg), vec(cbn))


def _sample_odd_body(st_ref, pin_ref, u_ref, vn_ref, pw_ref, ps_ref, w0_ref, b0_ref, o_ref, *, bd, start_pos):
    pin = pin_ref[0:bd, :]
    st = st_ref[...]
    for g, w in enumerate(POOL_WINDOWS):
        sl = slice(g * POOL_GC, (g + 1) * POOL_GC)
        tot = pin[:, sl] + jnp.sum(st[:, POOL_STATE - (w - 1):, sl], axis=1)
        d = tot / float(min(w, start_pos + 1)) - pin[:, sl]
        dp = jnp.concatenate([d, jnp.zeros((SAMPLE_ROWS - bd, POOL_GC), F32)], axis=0).astype(BF16)
        y = jnp.dot(dp, pw_ref[g], preferred_element_type=F32)[0:bd]
        o_ref[:, sl] = y * ps_ref[:, sl]
    mixed = w0_ref[...] * vn_ref[0:bd, :] + b0_ref[...]
    o_ref[:, POOL_CH:] = u_ref[0:bd, :] * mixed


def _sample_odd_mix(state, layer, pin, u, vn, pool_w, pool_scale, sgu_w, sgu_b, *, bd, start_pos):
    w0 = jnp.repeat(sgu_w[:, 0, 0], SGU_GC).reshape(1, SGU_CH)
    b0 = jnp.repeat(sgu_b[:, 0], SGU_GC).reshape(1, SGU_CH)
    cst = lambda i: (0, 0)
    return pl.pallas_call(
        functools.partial(_sample_odd_body, bd=bd, start_pos=start_pos), grid=(1,),
        in_specs=[pl.BlockSpec((None, bd, POOL_STATE, POOL_CH), lambda i: (layer, 0, 0, 0)),
                  pl.BlockSpec((SAMPLE_ROWS, POOL_CH), cst), pl.BlockSpec((SAMPLE_ROWS, SGU_CH), cst),
                  pl.BlockSpec((SAMPLE_ROWS, SGU_CH), cst),
                  pl.BlockSpec((POOL_GROUPS, POOL_GC, POOL_GC), lambda i: (0, 0, 0)),
                  pl.BlockSpec((1, POOL_CH), cst), pl.BlockSpec((1, SGU_CH), cst), pl.BlockSpec((1, SGU_CH), cst)],
        out_specs=pl.BlockSpec((bd, D_MODEL), cst),
        out_shape=jax.ShapeDtypeStruct((bd, D_MODEL), F32),
        compiler_params=_cparams("arbitrary"), name="sample_pool_sgu",
    )(state, pin, u, vn, pool_w.astype(BF16), pool_scale.reshape(1, POOL_CH), w0, b0)


def _pad_rows(x, rows):
    return jnp.pad(x, ((0, rows - x.shape[0]), (0, 0)))


def _split_to_nat(x, nheads):
    y = _unpad_heads(x, nheads)
    return _pad_nat(y.reshape(y.shape[:-1] + (nheads, HEAD_DIM))).reshape(x.shape)


def _nat_to_split(x, nheads):
    xh = x.reshape(x.shape[:-1] + (nheads, HSLOT))[..., :HEAD_DIM]
    return _pad_head(xh).reshape(x.shape)


def _even_in_proj(xb, w_in, cos, sin, *, tm, bs=None):
    nrep = cos.shape[0] // tm
    tab = lambda i, j: (i % nrep, 0)
    rope_ex = ((cos, (tm, LANES), tab), (sin, (tm, LANES), tab))
    gw = HPG * HSLOT
    qraw, qrot = _mm(xb, w_in, tm=tm, tn=gw, n_off=E_Q, n_cols=QW, epilogue=_ep_q, extras=rope_ex,
                     outs=((QW, BF16, gw), (QW, BF16, gw)), name="even_in_q")
    row = {"f32": (KVP, F32, KVP), "bf16": (KVP, BF16, KVP)}
    if bs is not None:
        nst = bs[1] // tm
        row["t"] = ((bs[0], KVW, bs[1]), F32, (None, KVW, tm), lambda i, j: (i // nst, 0, i % nst))
        wants = (("f32", "t"), ("bf16", "t"), ("bf16", "t"))
    else:
        wants = (("f32",), ("f32",), ("f32",))
    kv_out = []
    for sec, (off, want) in enumerate(zip((E_KVC, E_KVS, E_KVW), wants)):
        kv_out.append(_mm(xb, w_in, tm=tm, tn=KVP, n_off=off, n_cols=KVP,
                          epilogue=functools.partial(_ep_kv, rope=sec > 0, want=want),
                          extras=rope_ex if sec > 0 else (), outs=tuple(row[k] for k in want),
                          name=("even_in_kvc", "even_in_kvs", "even_in_kvw")[sec]))
    kvc, kvs, kvw = kv_out
    (u,) = _mm(xb, w_in, tm=tm, tn=CONV_CH, n_off=E_GLU, n_cols=2 * CONV_CH, epilogue=_ep_glu,
               outs=((CONV_CH, F32, CONV_CH // 2),), name="even_in_glu")
    (gates,) = _mm(xb, w_in, tm=tm, tn=GATE_W, n_off=E_GATE, n_cols=GATE_W, epilogue=_ep_sigmoid,
                   outs=((GATE_W, F32, GATE_W),), name="even_in_gates")
    return qraw, qrot, kvc, kvs, kvw, u, gates


def _mlp_up_cast(xb, w1, w2, layer, *, tm, tn):
    m, k = xb.shape
    dff = w1.shape[2]
    d_out = w2.shape[2]
    ni, nj = m // tm, dff // tn
    slab = dff // (ni * nj)
    assert m % tm == 0 and dff % tn == 0 and dff % (ni * nj) == 0 and slab % SAMPLE_ROWS == 0
    rsub = min(tm, ROW_SUB)

    def body(x_ref, w1_ref, w2_ref, h_ref, w1b_ref, w2b_ref):
        @pl.when(pl.program_id(1) == 0)
        def _():
            w1b_ref[...] = w1_ref[...].astype(BF16)

        w2b_ref[...] = w2_ref[...].astype(BF16)
        for r in range(tm // rsub):
            rows = slice(r * rsub, (r + 1) * rsub)
            acc = jnp.dot(x_ref[rows, :], w1b_ref[...], preferred_element_type=F32)
            a = jnp.maximum(acc, 0.0)
            h_ref[rows, :] = (a * a).astype(BF16)

    return pl.pallas_call(
        body, grid=(nj, ni),
        in_specs=[pl.BlockSpec((tm, k), lambda j, i: (i, 0)),
                  pl.BlockSpec((None, k, tn), lambda j, i: (layer, 0, j)),
                  pl.BlockSpec((None, slab, d_out), lambda j, i: (layer, j * ni + i, 0))],
        out_specs=[pl.BlockSpec((tm, tn), lambda j, i: (i, j)),
                   pl.BlockSpec((k, tn), lambda j, i: (0, j)),
                   pl.BlockSpec((slab, d_out), lambda j, i: (j * ni + i, 0))],
        out_shape=[jax.ShapeDtypeStruct((m, dff), BF16), jax.ShapeDtypeStruct((k, dff), BF16),
                   jax.ShapeDtypeStruct((dff, d_out), BF16)],
        compiler_params=_cparams("arbitrary", "arbitrary"), name="mlp_up_cast",
    )(xb, w1, w2)


def _mlp(x, xb, w1b, w2b, g, b, *, tm1, tn1, tm2, tn2):
    dff = w1b.shape[1]
    (h,) = _mm(xb, w1b, tm=tm1, tn=tn1, n_off=0, n_cols=dff, epilogue=_ep_relu2,
               outs=((dff, BF16, tn1),), name="mlp_up")
    return _mlp2_ln(h, w2b, x, g, b, tm=tm2, tn=tn2, name="mlp_down_ln")


def kernel(x_prompt, x_sample, cache_cmp_kv, cache_sel_kv, cache_win_kv, state_conv, state_pool, page_table,
           w_in_even, w_out_even, cmp_pe_k, cmp_pe_v, cmp_w_k, cmp_w_v, conv_w, conv_b, conv_ln_g, conv_ln_b,
           w_in_odd, w_out_odd, pool_w, pool_scale, sgu_ln_g, sgu_ln_b, sgu_w, sgu_b,
           mlp_w1, mlp_w2, ln_mix_g, ln_mix_b, ln_ffn_g, ln_ffn_b):
    B, S, D = x_prompt.shape
    Bd, Sd, _ = x_sample.shape
    n_pages = page_table.shape[1]
    past = n_pages * PAGE_SIZE
    n_even, n_pool = cache_cmp_kv.shape[:2]
    wb = cache_win_kv.shape[2]
    assert D == D_MODEL and Sd == 1 and Bd <= SAMPLE_ROWS
    assert S % 1024 == 0 and past % SEL_BLOCK == 0 and S >= WINDOW
    M = B * S
    Ms = SAMPLE_ROWS
    tm_p = 1024

    cos_p, sin_p = _rope_tables(jnp.arange(S, dtype=jnp.int32))
    cos_s, sin_s = _rope_tables(jnp.full((Ms,), past, jnp.int32))
    pps = min(32, n_pages)
    assert n_pages % pps == 0
    bsum = _block_sum_matrices(pps)
    cmp_t = cache_cmp_kv.transpose(0, 1, 3, 4, 5, 2).reshape(n_even * n_pool, KVW, PAGE_SIZE)
    sel_t = cache_sel_kv.transpose(0, 1, 3, 4, 5, 2).reshape(n_even * n_pool, 2, KV_HEADS, HEAD_DIM, PAGE_SIZE)
    win_t = cache_win_kv.transpose(0, 1, 3, 4, 5, 2).reshape(n_even * Bd, 2 * KV_HEADS, HEAD_DIM, wb)

    xp = x_prompt.reshape(M, D)
    xs = _pad_rows(x_sample.reshape(Bd, D), Ms)
    xpb, xsb = xp.astype(BF16), xs.astype(BF16)

    outs = {k: [] for k in ("cmp_p", "cmp_s", "sel_p", "sel_s", "win_p", "win_s", "conv_p", "conv_s",
                            "pool_p", "pool_s", "sgu_p", "sgu_s")}
    kv6 = lambda a, lead: a.reshape(lead + (2, KV_HEADS, HEAD_DIM))

    for layer in range(DEPTH):
        if layer % 2 == 0:
            e = layer // 2
            wts = _prep_even_weights(w_in_even[e], w_out_even[e], cmp_pe_k[e], cmp_pe_v[e], cmp_w_k[e], cmp_w_v[e])
            qraw, qrot, (kvc, kvc_t), (kvs_b, kvs_t), (kvw_b, kvw_t), u, gates = _even_in_proj(
                xpb, wts["w_in"], cos_p, sin_p, tm=tm_p, bs=(B, S))
            summ = _compress_rows(kvc, wts["pe"], rows=512, name="prompt_compress")
            (ckv,) = _mm(summ.astype(BF16), wts["big_p"], tm=min(summ.shape[0], 512), tn=KVP,
                         n_off=0, n_cols=KVP, epilogue=_ep_plain, outs=((KVP, F32, KVP),), name="prompt_compress_map")
            o_att = _prompt_attention(qraw, qrot, gates, ckv, kvs_b, kvw_b, batch=B, seq=S, tq=256, tk=256, ngrp=2)
            c = _prompt_conv(u, conv_w[e], conv_b[e], conv_ln_g[e], conv_ln_b[e], batch=B, seq=S, ts=256)
            xp, xpb = _proj_ln([o_att, c], [wts["wo_att"], wts["wo_conv"]], xp, ln_mix_g[layer], ln_mix_b[layer],
                               tm=512, name="even_out_ln")
            rows_last = lambda a: a.reshape(B, 2, KV_HEADS, HEAD_DIM, a.shape[-1]).transpose(0, 4, 1, 2, 3)
            outs["cmp_p"].append(rows_last(kvc_t))
            outs["sel_p"].append(rows_last(kvs_t))
            outs["win_p"].append(rows_last(kvw_t[:, :, S - WINDOW:]))
            outs["conv_p"].append(u.reshape(B, S, CONV_CH)[:, S - (CONV_W - 1):])
            qraw_s, qrot_s, (kvc_s,), (kvs_s,), (kvw_s,), u_s, gates_s = _even_in_proj(
                xsb, wts["w_in"], cos_s, sin_s, tm=Ms)
            ckv_t = _compress_pages(cmp_t, page_table, wts["pe_t"], wts["big_t"], bsum, e * n_pool, pps=pps,
                                    name="sample_compress")
            q3 = qraw_s.astype(F32)[:Bd].reshape(Bd, N_HEADS, HSLOT)
            o_cmp, idx = _sample_cmp(q3, ckv_t, bd=Bd, past=past, qpos=past, nbl=pps * (PAGE_SIZE // CMP_BLOCK))
            idx = idx[:, :, 0].reshape(Bd, KV_HEADS, TOP_K)
            qr3 = _split_to_nat(qrot_s.astype(F32)[:Bd], N_HEADS).reshape(Bd, N_HEADS, HSLOT)
            q4 = jnp.pad(qr3.reshape(Bd, KV_HEADS, HPG, HSLOT), ((0, 0), (0, 0), (0, Q_PAD_ROWS - HPG), (0, 0)))
            kvs_nat = _split_to_nat(kvs_s, 2 * KV_HEADS)
            kvw_nat = _split_to_nat(kvw_s, 2 * KV_HEADS)
            o_sel = _sample_sel(sel_t, page_table, idx, e * n_pool, q4, kvs_nat, bd=Bd, past=past)
            o_sel = o_sel[:, :, :HPG].reshape(Bd, N_HEADS, HSLOT)
            g3 = gates_s[:Bd].reshape(Bd, KV_HEADS, LANES)[:, :, :3 * HPG].reshape(Bd, KV_HEADS, 3, HPG)
            g3 = g3.transpose(0, 1, 3, 2).reshape(Bd, N_HEADS, 3)
            g3 = jnp.pad(g3, ((0, 0), (0, 0), (0, LANES - 3)))
            o_s = _sample_win(qr3, win_t, e, kvw_nat, o_cmp, o_sel, g3, bd=Bd, past=past, qpos=past)
            c_s = _sample_conv(state_conv, e, u_s, conv_w[e], conv_b[e], conv_ln_g[e], conv_ln_b[e], bd=Bd)
            o_sb = _pad_rows(_nat_to_split(o_s.reshape(Bd, QW), N_HEADS), Ms).astype(BF16)
            c_sb = _pad_rows(c_s, Ms).astype(BF16)
            xs, xsb = _proj_ln([o_sb, c_sb], [wts["wo_att"], wts["wo_conv"]], xs, ln_mix_g[layer], ln_mix_b[layer],
                               tm=Ms, name="even_out_ln_s")
            kvc_c = _unpad_heads(kvc_s[:Bd], 2 * KV_HEADS)
            kvs_c = _unpad_heads(kvs_s[:Bd], 2 * KV_HEADS)
            kvw_c = _unpad_heads(kvw_s[:Bd], 2 * KV_HEADS)
            outs["cmp_s"].append(kv6(kvc_c, (Bd, 1)))
            outs["sel_s"].append(kv6(kvs_c, (Bd, 1)))
            wkv = jnp.concatenate([cache_win_kv[e], kv6(kvw_c, (Bd, 1))], axis=1)
            outs["win_s"].append(wkv[:, wkv.shape[1] - min(WINDOW, wkv.shape[1]):])
            outs["conv_s"].append(jnp.concatenate([state_conv[e], u_s[:Bd, None, :]], axis=1)[:, 1:])
        else:
            o = layer // 2
            w_in = w_in_odd[o]
            w_in_p = jnp.concatenate([w_in[:, POOL_CH + SGU_CH:], w_in[:, :POOL_CH], w_in[:, POOL_CH:POOL_CH + SGU_CH]],
                                     axis=1).astype(BF16)
            w_out_p = w_out_odd[o].astype(BF16)
            lg, lb = sgu_ln_g[o].reshape(1, SGU_CH), sgu_ln_b[o].reshape(1, SGU_CH)

            def odd_in(xb, tm):
                gl_ex = ((lg, (1, 2 * SGU_GC), lambda i, j: (0, j)), (lb, (1, 2 * SGU_GC), lambda i, j: (0, j)))
                (vn,) = _mm(xb, w_in_p, tm=tm, tn=2 * SGU_GC, n_off=O_V, n_cols=SGU_CH, epilogue=_ep_gelu_gln,
                            extras=gl_ex, outs=((SGU_CH, F32, 2 * SGU_GC),), name="odd_in_v")
                (pin,) = _mm(xb, w_in_p, tm=tm, tn=POOL_CH, n_off=O_PIN, n_cols=POOL_CH, epilogue=_ep_plain,
                             outs=((POOL_CH, F32, POOL_CH),), name="odd_in_pool")
                (uu,) = _mm(xb, w_in_p, tm=tm, tn=POOL_CH, n_off=O_U, n_cols=SGU_CH, epilogue=_ep_gelu,
                            outs=((SGU_CH, F32, POOL_CH),), name="odd_in_u")
                return vn, pin, uu

            vn, pin, uu = odd_in(xpb, tm_p)
            cat = _prompt_odd_mix(pin, uu, vn, pool_w[o], pool_scale[o], sgu_w[o], sgu_b[o], batch=B, seq=S)
            xp, xpb = _proj_ln([cat], [w_out_p], xp, ln_mix_g[layer], ln_mix_b[layer], tm=512, name="odd_out_ln")
            outs["pool_p"].append(pin.reshape(B, S, POOL_CH)[:, S - POOL_STATE:])
            outs["sgu_p"].append(vn.reshape(B, S, SGU_CH)[:, ((S - 1) // CHUNK) * CHUNK:])
            vn_s, pin_s, uu_s = odd_in(xsb, Ms)
            cat_s = _sample_odd_mix(state_pool, o, pin_s, uu_s, vn_s, pool_w[o], pool_scale[o], sgu_w[o], sgu_b[o],
                                    bd=Bd, start_pos=past)
            xs, xsb = _proj_ln([_pad_rows(cat_s, Ms).astype(BF16)], [w_out_p], xs, ln_mix_g[layer], ln_mix_b[layer],
                               tm=Ms, name="odd_out_ln_s")
            outs["pool_s"].append(jnp.concatenate([state_pool[o], pin_s[:Bd, None, :]], axis=1)[:, 1:])
            outs["sgu_s"].append(vn_s[:Bd, None, :])
        h, w1b, w2b = _mlp_up_cast(xpb, mlp_w1, mlp_w2, layer, tm=tm_p, tn=1024)
        xp, xpb = _mlp2_ln(h, w2b, xp, ln_ffn_g[layer], ln_ffn_b[layer], tm=512, tn=512, name="mlp_down_ln")
        xs, xsb = _mlp(xs, xsb, w1b, w2b, ln_ffn_g[layer], ln_ffn_b[layer], tm1=Ms, tn1=2048, tm2=Ms, tn2=1024)

    st = lambda k: jnp.stack(outs[k])
    return (xp.reshape(B, S, D), xs[:Bd].reshape(Bd, Sd, D),
            st("cmp_p"), st("cmp_s"), st("sel_p"), st("sel_s"), st("win_p"), st("win_s"),
            st("conv_p"), st("conv_s"), st("pool_p"), st("pool_s"), st("sgu_p"), st("sgu_s"))
```

```python
import functools

import jax
import jax.numpy as jnp
from jax import lax
from jax.experimental import pallas as pl
from jax.experimental.pallas import tpu as pltpu

F32 = jnp.float32
BF16 = jnp.bfloat16

D_MODEL = 2048
DEPTH = 4
PAGE_SIZE = 128
N_HEADS = 16
HEAD_DIM = 96
KV_HEADS = 4
HPG = N_HEADS // KV_HEADS
ATT_W = N_HEADS * HEAD_DIM
KVW = 2 * KV_HEADS * HEAD_DIM
CMP_BLOCK = 32
SEL_BLOCK = 64
TOP_K = 16
WINDOW = 512
ROPE_THETA = 10000.0
SCALE = HEAD_DIM ** -0.5
LOG2E = 1.4426950408889634
FORCE = 1e9
NEG = -1e30
CONV_CH = D_MODEL // 4
CONV_W = 31
POOL_CH = D_MODEL // 4
POOL_WINDOWS = (2, 4, 8, 16)
POOL_GROUPS = len(POOL_WINDOWS)
POOL_GC = POOL_CH // POOL_GROUPS
POOL_STATE = max(POOL_WINDOWS) - 1
SGU_CH = D_MODEL - POOL_CH
SGU_GROUPS = 4
SGU_GC = SGU_CH // SGU_GROUPS
CHUNK = 128
D_FF = 4 * D_MODEL
ALPHA = (2 * DEPTH) ** 0.25
LN_EPS = 1e-5

LANES = 128
SUBLANES = 8
HALF = HEAD_DIM // 2
HSLOT = LANES
HALF_OFF = LANES // 2
QW = N_HEADS * HSLOT
KVP = 2 * KV_HEADS * HSLOT
GATE_W = KV_HEADS * LANES
SAMPLE_ROWS = 16
ROW_SUB = 256
LN_ROW_SUB = 128
VMEM_LIMIT = 52 * 1024 * 1024

E_Q, E_KVC, E_KVS, E_KVW = 0, QW, QW + KVP, QW + 2 * KVP
E_GLU = QW + 3 * KVP
E_GATE = E_GLU + 2 * CONV_CH
E_TOT = E_GATE + GATE_W
O_V, O_PIN, O_U = 0, SGU_CH, SGU_CH + POOL_CH


def _cparams(*sem):
    return pltpu.CompilerParams(dimension_semantics=sem, vmem_limit_bytes=VMEM_LIMIT)


def _pad_head(x):
    z = jnp.zeros(x.shape[:-1] + (HALF_OFF - HALF,), x.dtype)
    return jnp.concatenate([x[..., :HALF], z, x[..., HALF:], z], axis=-1)


def _pad_nat(x):
    return jnp.concatenate([x, jnp.zeros(x.shape[:-1] + (HSLOT - HEAD_DIM,), x.dtype)], axis=-1)


def _unpad_heads(x, nheads):
    xh = x.reshape(x.shape[:-1] + (nheads, HSLOT))
    y = jnp.concatenate([xh[..., :HALF], xh[..., HALF_OFF:HALF_OFF + HALF]], axis=-1)
    return y.reshape(x.shape[:-1] + (nheads * HEAD_DIM,))


def _rope_tables(pos):
    inv = jnp.power(ROPE_THETA, -jnp.arange(HALF, dtype=F32) / HALF)
    ang = pos.astype(F32)[:, None] * inv[None, :]
    cos, sin = jnp.cos(ang), jnp.sin(ang)
    z = jnp.zeros((pos.shape[0], HALF_OFF - HALF), F32)
    return (jnp.concatenate([cos, z, cos, z], axis=1),
            jnp.concatenate([-sin, z, sin, z], axis=1))


def _block_diag2(a, b):
    za = jnp.zeros((a.shape[0], b.shape[1]), a.dtype)
    zb = jnp.zeros((b.shape[0], a.shape[1]), a.dtype)
    return jnp.concatenate([jnp.concatenate([a, za], axis=1), jnp.concatenate([zb, b], axis=1)], axis=0)


def _prep_even_weights(w_in, w_out, pe_k, pe_v, w_ck, w_cv):
    d = w_in.shape[0]
    q = _pad_head(w_in[:, :ATT_W].reshape(d, N_HEADS, HEAD_DIM)).reshape(d, QW)
    kvs = []
    for s in range(3):
        blk = w_in[:, ATT_W + s * KVW:ATT_W + (s + 1) * KVW].reshape(d, 2 * KV_HEADS, HEAD_DIM)
        kvs.append(_pad_head(blk).reshape(d, KVP))
    g0 = ATT_W + 3 * KVW
    gates = w_in[:, g0:g0 + 3 * N_HEADS].reshape(d, 3, KV_HEADS, HPG).transpose(0, 2, 1, 3)
    gates = gates.reshape(d, KV_HEADS, 3 * HPG)
    gates = jnp.pad(gates, ((0, 0), (0, 0), (0, LANES - 3 * HPG))).reshape(d, GATE_W)
    glu = w_in[:, g0 + 3 * N_HEADS:]
    a = glu[:, :CONV_CH].reshape(d, 2, CONV_CH // 2)
    g = glu[:, CONV_CH:].reshape(d, 2, CONV_CH // 2)
    glu = jnp.stack([a, g], axis=2).reshape(d, 2 * CONV_CH)
    w_in_p = jnp.concatenate([q] + kvs + [glu, gates], axis=1).astype(BF16)
    wo_att = _pad_head(w_out[:ATT_W].reshape(N_HEADS, HEAD_DIM, d).transpose(0, 2, 1))
    wo_att = wo_att.transpose(0, 2, 1).reshape(QW, d).astype(BF16)
    wo_conv = w_out[ATT_W:].astype(BF16)
    eye = jnp.eye(KV_HEADS, dtype=F32)
    pe = jnp.concatenate([jnp.tile(_pad_head(pe_k), (1, KV_HEADS)), jnp.tile(_pad_head(pe_v), (1, KV_HEADS))], axis=1)
    wk_full = _pad_head(_pad_head(w_ck).T).T
    wv_full = _pad_head(_pad_head(w_cv).T).T
    big_p = _block_diag2(jnp.kron(eye, wk_full), jnp.kron(eye, wv_full))
    pe_t = jnp.concatenate([jnp.tile(jnp.tile(pe_k.T, (1, PAGE_SIZE // CMP_BLOCK)), (KV_HEADS, 1)),
                            jnp.tile(jnp.tile(pe_v.T, (1, PAGE_SIZE // CMP_BLOCK)), (KV_HEADS, 1))], axis=0)
    big_t = _block_diag2(jnp.kron(eye, _pad_head(w_ck).T), jnp.kron(eye, _pad_nat(w_cv).T))
    return dict(w_in=w_in_p, wo_att=wo_att, wo_conv=wo_conv, pe=pe, pe_t=pe_t,
                big_p=big_p.astype(BF16), big_t=big_t.astype(BF16))


def _block_sum_matrices(pps):
    bpp = PAGE_SIZE // CMP_BLOCK
    nbl = bpp * pps
    p = jnp.arange(pps)[:, None, None]
    i = (jnp.arange(PAGE_SIZE) // CMP_BLOCK)[None, :, None]
    c = jnp.arange(nbl)[None, None, :]
    col = (i % 2) * (nbl // 2) + (bpp // 2) * p + i // 2
    return jnp.where(c == col, 1.0 / CMP_BLOCK, 0.0).astype(BF16)


def _ln_rows(y, g, b):
    mu = jnp.mean(y, axis=-1, keepdims=True)
    yc = y - mu
    var = jnp.mean(yc * yc, axis=-1, keepdims=True)
    return yc * lax.rsqrt(var + LN_EPS) * g + b


def _rope_slot(x, cos, sin):
    return x * cos + pltpu.roll(x, HALF_OFF, axis=1) * sin


_NT = (((1,), (1,)), ((), ()))


def _pad_rows_to(x, rows):
    return jnp.concatenate([x, jnp.zeros((rows - x.shape[0],) + x.shape[1:], x.dtype)], axis=0)


def _mm(x, w, *, tm, tn, n_off, n_cols, epilogue, extras=(), outs, name, side=None):
    m, k = x.shape
    assert m % tm == 0 and n_cols % tn == 0 and n_off % tn == 0
    joff = n_off // tn
    ji = lambda im: (lambda j, i: im(i, j))
    in_specs = [pl.BlockSpec((tm, k), lambda j, i: (i, 0)),
                pl.BlockSpec((k, tn), lambda j, i: (0, joff + j))]
    in_specs += [pl.BlockSpec(bs, ji(im)) for _, bs, im in extras]
    operands = [x, w] + [a for a, _, _ in extras]
    out_shape, out_specs = [], []
    for o in outs:
        if len(o) == 3:
            out_shape.append(jax.ShapeDtypeStruct((m, o[0]), o[1]))
            out_specs.append(pl.BlockSpec((tm, o[2]), lambda j, i: (i, j)))
        else:
            out_shape.append(jax.ShapeDtypeStruct(o[0], o[1]))
            out_specs.append(pl.BlockSpec(o[2], ji(o[3])))
    ne, no = len(extras), len(outs)
    nse = 0
    if side is not None:
        xs, s_epilogue, s_extras, s_outs = side
        rs = xs.shape[0]
        nse = len(s_extras)
        in_specs += [pl.BlockSpec((rs, k), lambda j, i: (0, 0))]
        in_specs += [pl.BlockSpec(bs, ji(im)) for _, bs, im in s_extras]
        operands += [xs] + [a for a, _, _ in s_extras]
        for cols, dt, bc in s_outs:
            out_shape.append(jax.ShapeDtypeStruct((rs, cols), dt))
            out_specs.append(pl.BlockSpec((rs, bc), lambda j, i: (0, j)))
    rsub = min(tm, ROW_SUB)

    def body(*refs):
        x_ref, w_ref = refs[:2]
        ex = refs[2:2 + ne]
        n_in = 2 + ne + (1 + nse if side is not None else 0)
        o_refs = refs[n_in:n_in + no]
        for r in range(tm // rsub):
            rows = slice(r * rsub, (r + 1) * rsub)
            acc = jnp.dot(x_ref[rows, :], w_ref[...], preferred_element_type=F32)
            epilogue(acc, ex, o_refs, rows)
        if side is not None:
            @pl.when(pl.program_id(1) == 0)
            def _():
                acc = jnp.dot(refs[2 + ne][...], w_ref[...], preferred_element_type=F32)
                s_epilogue(acc, refs[3 + ne:n_in], refs[n_in + no:], slice(0, rs))

    return pl.pallas_call(
        body, grid=(n_cols // tn, m // tm), in_specs=in_specs, out_specs=out_specs, out_shape=out_shape,
        compiler_params=_cparams("arbitrary", "arbitrary"), name=name,
    )(*operands)


def _ep_q(acc, ex, outs, rows):
    cos, sin = ex[0][rows, :], ex[1][rows, :]
    outs[0][rows, :] = acc.astype(BF16)
    for j in range(acc.shape[1] // HSLOT):
        sl = slice(j * HSLOT, (j + 1) * HSLOT)
        outs[1][rows, sl] = _rope_slot(acc[:, sl], cos, sin).astype(BF16)


def _ep_kv(acc, ex, outs, rows, *, rope, want):
    o = dict(zip(want, outs))
    if rope:
        cos, sin = ex[0][rows, :], ex[1][rows, :]
    for j in range(2 * KV_HEADS):
        sl = slice(j * HSLOT, (j + 1) * HSLOT)
        x = acc[:, sl]
        if rope and j < KV_HEADS:
            x = _rope_slot(x, cos, sin)
        if "f32" in o:
            o["f32"][rows, sl] = x
        if "bf16" in o:
            o["bf16"][rows, sl] = x.astype(BF16)
        if "t" in o:
            xt = x.T
            o["t"][j * HEAD_DIM:j * HEAD_DIM + HALF, rows] = xt[0:HALF]
            o["t"][j * HEAD_DIM + HALF:(j + 1) * HEAD_DIM, rows] = xt[HALF_OFF:HALF_OFF + HALF]


def _ep_glu(acc, ex, outs, rows):
    h = acc.shape[1] // 2
    outs[0][rows, :] = acc[:, :h] * jax.nn.sigmoid(acc[:, h:])


def _ep_sigmoid(acc, ex, outs, rows):
    outs[0][rows, :] = jax.nn.sigmoid(acc)


def _ep_plain(acc, ex, outs, rows):
    outs[0][rows, :] = acc.astype(outs[0].dtype)


def _ep_relu2(acc, ex, outs, rows):
    r = jnp.maximum(acc, 0.0)
    outs[0][rows, :] = (r * r).astype(outs[0].dtype)


def _ep_gelu(acc, ex, outs, rows):
    outs[0][rows, :] = jax.nn.gelu(acc)


def _ep_gelu_gln(acc, ex, outs, rows):
    g, b = ex[0][...], ex[1][...]
    v = jax.nn.gelu(acc)
    for j in range(acc.shape[1] // SGU_GC):
        sl = slice(j * SGU_GC, (j + 1) * SGU_GC)
        outs[0][rows, sl] = _ln_rows(v[:, sl], g[:, sl], b[:, sl])


def _proj_ln(a_list, w_list, resid, g, b, *, tm, name, side):
    m, n = resid.shape
    as_list, resid_s = side
    rs = resid_s.shape[0]
    npair = len(a_list)
    in_specs = []
    for a in a_list:
        in_specs.append(pl.BlockSpec((tm, a.shape[1]), lambda i: (i, 0)))
    for w in w_list:
        in_specs.append(pl.BlockSpec(w.shape, lambda i: (0, 0), pipeline_mode=pl.Buffered(1)))
    in_specs += [pl.BlockSpec((tm, n), lambda i: (i, 0)),
                 pl.BlockSpec((1, n), lambda i: (0, 0)), pl.BlockSpec((1, n), lambda i: (0, 0))]
    for a in as_list:
        in_specs.append(pl.BlockSpec((rs, a.shape[1]), lambda i: (0, 0)))
    in_specs.append(pl.BlockSpec((rs, n), lambda i: (0, 0)))

    def body(*refs):
        a_refs, w_refs = refs[:npair], refs[npair:2 * npair]
        r_ref, g_ref, b_ref = refs[2 * npair:2 * npair + 3]
        as_refs = refs[2 * npair + 3:3 * npair + 3]
        rs_ref, o_ref, ob_ref, os_ref, osb_ref = refs[3 * npair + 3:]

        def rows_out(a_rs, res, rows):
            acc = ALPHA * res[rows, :]
            for a_ref, w_ref in zip(a_rs, w_refs):
                acc = acc + jnp.dot(a_ref[rows, :], w_ref[...], preferred_element_type=F32)
            return _ln_rows(acc, g_ref[...], b_ref[...])

        rsub = min(tm, ROW_SUB)
        for r in range(tm // rsub):
            rows = slice(r * rsub, (r + 1) * rsub)
            y = rows_out(a_refs, r_ref, rows)
            o_ref[rows, :] = y
            ob_ref[rows, :] = y.astype(BF16)
        @pl.when(pl.program_id(0) == 0)
        def _():
            ys = rows_out(as_refs, rs_ref, slice(0, rs))
            os_ref[...] = ys
            osb_ref[...] = ys.astype(BF16)

    row = pl.BlockSpec((tm, n), lambda i: (i, 0))
    srow = pl.BlockSpec((rs, n), lambda i: (0, 0))
    return pl.pallas_call(
        body, grid=(m // tm,), in_specs=in_specs, out_specs=[row, row, srow, srow],
        out_shape=[jax.ShapeDtypeStruct((m, n), F32), jax.ShapeDtypeStruct((m, n), BF16),
                   jax.ShapeDtypeStruct((rs, n), F32), jax.ShapeDtypeStruct((rs, n), BF16)],
        compiler_params=_cparams("arbitrary"), name=name,
    )(*a_list, *w_list, resid, g.reshape(1, n), b.reshape(1, n), *as_list, resid_s)


def _mlp2_ln(h, w2, resid, g, b, *, tm, tn, name, side):
    m, kf = h.shape
    n = w2.shape[1]
    nj = n // tn
    hs, resid_s = side
    rs = hs.shape[0]
    rsub, rsub_ln = min(tm, ROW_SUB), min(tm, LN_ROW_SUB)

    def body(h_ref, w_ref, r_ref, g_ref, b_ref, hs_ref, rs_ref, o_ref, ob_ref, os_ref, osb_ref):
        j = pl.program_id(1)
        first = pl.program_id(0) == 0
        srows = slice(0, rs)

        def tile(hr, rr, rows):
            return ALPHA * rr[rows, :] + jnp.dot(hr[rows, :], w_ref[...], preferred_element_type=F32)

        def finish(o, ob, y, rows):
            y = _ln_rows(y, g_ref[...], b_ref[...])
            o[rows, :] = y
            ob[rows, :] = y.astype(BF16)

        for jj in range(nj - 1):
            @pl.when(j == jj)
            def _(jj=jj):
                cols = slice(jj * tn, (jj + 1) * tn)
                for r in range(tm // rsub):
                    rows = slice(r * rsub, (r + 1) * rsub)
                    o_ref[rows, cols] = tile(h_ref, r_ref, rows)

                @pl.when(first)
                def _():
                    os_ref[:, cols] = tile(hs_ref, rs_ref, srows)

        @pl.when(j == nj - 1)
        def _():
            done = slice(0, (nj - 1) * tn)
            for r in range(tm // rsub_ln):
                rows = slice(r * rsub_ln, (r + 1) * rsub_ln)
                finish(o_ref, ob_ref, jnp.concatenate([o_ref[rows, done], tile(h_ref, r_ref, rows)], axis=1), rows)

            @pl.when(first)
            def _():
                finish(os_ref, osb_ref, jnp.concatenate([os_ref[:, done], tile(hs_ref, rs_ref, srows)], axis=1), srows)

    assert nj > 1
    row = pl.BlockSpec((tm, n), lambda i, j: (i, 0))
    srow = pl.BlockSpec((rs, n), lambda i, j: (0, 0))
    return pl.pallas_call(
        body, grid=(m // tm, nj),
        in_specs=[pl.BlockSpec((tm, kf), lambda i, j: (i, 0)), pl.BlockSpec((kf, tn), lambda i, j: (0, j)),
                  pl.BlockSpec((tm, tn), lambda i, j: (i, j)),
                  pl.BlockSpec((1, n), lambda i, j: (0, 0)), pl.BlockSpec((1, n), lambda i, j: (0, 0)),
                  pl.BlockSpec((rs, kf), lambda i, j: (0, 0)), pl.BlockSpec((rs, tn), lambda i, j: (0, j))],
        out_specs=[row, row, srow, srow],
        out_shape=[jax.ShapeDtypeStruct((m, n), F32), jax.ShapeDtypeStruct((m, n), BF16),
                   jax.ShapeDtypeStruct((rs, n), F32), jax.ShapeDtypeStruct((rs, n), BF16)],
        compiler_params=_cparams("arbitrary", "arbitrary"), name=name,
    )(h, w2, resid, g.reshape(1, n), b.reshape(1, n), hs, resid_s)


def _compress_rows(kvc, pe, *, rows, name):
    m, c = kvc.shape
    nb = rows // CMP_BLOCK

    def body(x_ref, pe_ref, o_ref):
        x = x_ref[...].reshape(nb, CMP_BLOCK, c) * pe_ref[...][None]
        o_ref[...] = jnp.sum(x, axis=1) * (1.0 / CMP_BLOCK)

    return pl.pallas_call(
        body, grid=(m // rows,),
        in_specs=[pl.BlockSpec((rows, c), lambda i: (i, 0)), pl.BlockSpec((CMP_BLOCK, c), lambda i: (0, 0))],
        out_specs=pl.BlockSpec((nb, c), lambda i: (i, 0)),
        out_shape=jax.ShapeDtypeStruct((m // CMP_BLOCK, c), F32),
        compiler_params=_cparams("parallel"), name=name,
    )(kvc, pe)


def _compress_pages(pages_t, page_table, pe_t, big_t, bsum, layer_base, *, pps, name):
    bd, n_pages = page_table.shape
    bpp = PAGE_SIZE // CMP_BLOCK
    nbl = bpp * pps
    nsteps = n_pages // pps

    def body(pt_ref, *refs):
        page_refs = refs[:pps]
        pe_ref, big_ref, bsum_ref, o_ref = refs[pps:]
        acc = jnp.zeros((KVW, nbl), F32)
        for p in range(pps):
            x = page_refs[p][...] * pe_ref[...]
            acc = acc + jnp.dot(x.astype(BF16), bsum_ref[p], preferred_element_type=F32)
        o_ref[...] = jnp.dot(big_ref[...], acc.astype(BF16), preferred_element_type=F32)

    def page_map(p):
        return lambda b, j, pt: (layer_base + pt[b, j * pps + p], 0, 0)

    cst2 = lambda b, j, pt: (0, 0)
    grid_spec = pltpu.PrefetchScalarGridSpec(
        num_scalar_prefetch=1, grid=(bd, nsteps),
        in_specs=[pl.BlockSpec((None, KVW, PAGE_SIZE), page_map(p)) for p in range(pps)]
        + [pl.BlockSpec((KVW, PAGE_SIZE), cst2), pl.BlockSpec((KVP, KVW), cst2),
           pl.BlockSpec((pps, PAGE_SIZE, nbl), lambda b, j, pt: (0, 0, 0))],
        out_specs=pl.BlockSpec((None, KVP, nbl), lambda b, j, pt: (b, 0, j)))
    return pl.pallas_call(
        body, grid_spec=grid_spec,
        out_shape=jax.ShapeDtypeStruct((bd, KVP, n_pages * bpp), F32),
        compiler_params=_cparams("parallel", "arbitrary"), name=name,
    )(page_table, *([pages_t] * pps), pe_t, big_t, bsum)


def _flash_step_t(q, k, vt, valid, m_ref, l_ref, acc_ref):
    bias = jnp.where(valid, 0.0, NEG)
    s = jnp.dot(k, q, preferred_element_type=F32) + jnp.concatenate([bias] * HPG, axis=1)
    m_prev = m_ref[...]
    m_new = jnp.maximum(m_prev, jnp.max(s, axis=0, keepdims=True))
    alpha = jnp.exp2(m_prev - m_new)
    p = jnp.exp2(s - m_new)
    l_ref[...] = alpha * l_ref[...] + jnp.sum(p, axis=0, keepdims=True)
    acc_ref[...] = alpha * acc_ref[...] + jnp.dot(vt, p.astype(BF16), preferred_element_type=F32)
    m_ref[...] = m_new


def _attn_body(*refs, tq, tk, seq, k_top, ngrp):
    n_in = 9
    ins = [refs[g * n_in:(g + 1) * n_in] for g in range(ngrp)]
    o_ref = refs[ngrp * n_in]
    m_ref, l_ref, acc_ref, sel_ref, vts_ref, vtw_ref = refs[ngrp * n_in + 1:]
    i = pl.program_id(2)
    cols = HPG * tq
    nsb = seq // SEL_BLOCK
    nt = seq // tk
    bpt = tk // SEL_BLOCK
    q0 = i * tq
    lane_q = lax.broadcasted_iota(jnp.int32, (1, cols), 1)
    qpos = q0 + (lane_q & (tq - 1))
    qp1 = q0 + lax.broadcasted_iota(jnp.int32, (1, tq), 1)

    @pl.when(i == 0)
    def _():
        def tr(t, c):
            for g in range(ngrp):
                vs_ref, vw_ref = ins[g][6], ins[g][8]
                for h in range(tk // LANES):
                    k0 = pl.multiple_of(t * tk + h * LANES, LANES)
                    hs = slice(h * LANES, (h + 1) * LANES)
                    vts_ref[g, t, :, hs] = vs_ref[pl.ds(k0, LANES), :].astype(F32).T.astype(BF16)
                    vtw_ref[g, t, :, hs] = vw_ref[pl.ds(k0, LANES), :].astype(F32).T.astype(BF16)
            return c
        lax.fori_loop(0, nt, tr, 0)

    def heads_t(ref, scale=1.0):
        parts = [ref[:, j * HSLOT:(j + 1) * HSLOT].astype(F32).T * scale for j in range(HPG)]
        return jnp.concatenate(parts, axis=1).astype(BF16)

    r = lax.broadcasted_iota(jnp.int32, (2 * nsb, 1), 0)
    n_of = jnp.where(r < nsb, 2 * r, 2 * (r - nsb) + 1)
    mk = ((n_of + 1) * CMP_BLOCK - 1) <= qpos
    sb = lax.broadcasted_iota(jnp.int32, (nsb, 1), 0)
    vis = (sb * SEL_BLOCK) <= qp1
    cur = sb == (qp1 >> (SEL_BLOCK.bit_length() - 1))

    def compressed_branch(g):
        qraw_ref, ck_ref, cv_ref = ins[g][0], ins[g][3], ins[g][4]
        qr = heads_t(qraw_ref)
        ck = jnp.concatenate([ck_ref[pl.ds(0, nsb, stride=2), :], ck_ref[pl.ds(1, nsb, stride=2), :]], axis=0)
        s = jnp.dot(ck.astype(BF16), qr, preferred_element_type=F32) * SCALE
        s = jnp.where(mk, s, NEG)
        mx = jnp.max(s, axis=0, keepdims=True)
        p = jnp.where(mk, jnp.exp(s - mx), 0.0)
        pn = p / jnp.maximum(jnp.sum(p, axis=0, keepdims=True), 1e-30)
        cv = jnp.concatenate([cv_ref[pl.ds(0, nsb, stride=2), :], cv_ref[pl.ds(1, nsb, stride=2), :]], axis=0)
        cvt = _pad_rows_to(cv, LANES).T.astype(BF16)
        o_cmp = jnp.dot(cvt, _pad_rows_to(pn, LANES).astype(BF16), preferred_element_type=F32)
        pp = pn[0:nsb] + pn[nsb:2 * nsb]
        imp = pp[:, 0:tq]
        for j in range(1, HPG):
            imp = imp + pp[:, j * tq:(j + 1) * tq]
        imp = jnp.where(cur, FORCE, jnp.where(vis, imp, -FORCE))
        cnt = jnp.zeros((nsb, tq), F32)
        for j in range(nsb):
            rowj = imp[j:j + 1, :]
            beats = (rowj > imp) | ((rowj == imp) & (j < sb))
            cnt = cnt + jnp.where(beats, 1.0, 0.0)
        sel = jnp.where(cnt < k_top, 1.0, 0.0)
        for t in range(nt):
            sel_ref[g, t, 0:bpt, :] = sel[bpt * t:bpt * (t + 1), :]
        return o_cmp

    o_cmp = [compressed_branch(g) for g in range(ngrp)]
    qt = [heads_t(ins[g][1], SCALE * LOG2E) for g in range(ngrp)]
    rowk = lax.broadcasted_iota(jnp.int32, (tk, 1), 0)

    def reset():
        m_ref[...] = jnp.full((ngrp, 1, cols), NEG, F32)
        l_ref[...] = jnp.zeros((ngrp, 1, cols), F32)
        acc_ref[...] = jnp.zeros((ngrp, HSLOT, cols), F32)

    def result(g):
        return acc_ref[g] / jnp.maximum(l_ref[g], 1e-30)

    reset()

    def sel_step(t, c):
        k0 = pl.multiple_of(t * tk, tk)
        kp = k0 + rowk
        for g in range(ngrp):
            sm = sel_ref[g, t, bpt - 1:bpt, :]
            for j in range(bpt - 2, -1, -1):
                sm = jnp.where(rowk < (j + 1) * SEL_BLOCK, sel_ref[g, t, j:j + 1, :], sm)
            valid = (kp <= qp1) & (sm > 0.5)
            _flash_step_t(qt[g], ins[g][5][pl.ds(k0, tk), :], vts_ref[g, t], valid,
                          m_ref.at[g], l_ref.at[g], acc_ref.at[g])
        return c

    t_end = lax.div(q0 + tq + tk - 1, tk)
    lax.fori_loop(0, t_end, sel_step, 0)
    o_sel = [result(g) for g in range(ngrp)]

    reset()

    def win_step(t, c):
        k0 = pl.multiple_of(t * tk, tk)
        kp = k0 + rowk
        valid = (kp <= qp1) & (kp >= qp1 - WINDOW)
        for g in range(ngrp):
            _flash_step_t(qt[g], ins[g][7][pl.ds(k0, tk), :], vtw_ref[g, t], valid,
                          m_ref.at[g], l_ref.at[g], acc_ref.at[g])
        return c

    lax.fori_loop(lax.div(jnp.maximum(q0 - WINDOW, 0), tk), t_end, win_step, 0)

    gw = HPG * HSLOT
    for g in range(ngrp):
        gt = ins[g][2][...].T
        o = jnp.zeros((HSLOT, cols), F32)
        for br, o_br in enumerate((o_cmp[g], o_sel[g], result(g))):
            grow = jnp.concatenate([gt[br * HPG + j:br * HPG + j + 1, :] for j in range(HPG)], axis=1)
            o = o + grow * o_br
        for j in range(HPG):
            o_ref[:, g * gw + j * HSLOT:g * gw + (j + 1) * HSLOT] = o[:, j * tq:(j + 1) * tq].T.astype(BF16)


def _prompt_attention(qraw, qrot, gates, ckv, kvs_b, kvw_b, *, batch, seq, tq, tk, ngrp):
    nq = seq // tq
    ncb = seq // CMP_BLOCK
    nt = seq // tk
    cols = HPG * tq
    k_top = min(TOP_K, seq // SEL_BLOCK)
    assert tq & (tq - 1) == 0 and seq % tk == 0 and tk % LANES == 0 and tk // SEL_BLOCK <= SUBLANES
    assert KV_HEADS % ngrp == 0
    body = functools.partial(_attn_body, tq=tq, tk=tk, seq=seq, k_top=k_top, ngrp=ngrp)
    gw = HPG * HSLOT
    in_specs, operands = [], []
    for g in range(ngrp):
        qmap = functools.partial(lambda b, p, i, g: (b * nq + i, p * ngrp + g), g=g)
        kmap = functools.partial(lambda b, p, i, g: (b, p * ngrp + g), g=g)
        vmap_ = functools.partial(lambda b, p, i, g: (b, KV_HEADS + p * ngrp + g), g=g)
        in_specs += [pl.BlockSpec((tq, gw), qmap), pl.BlockSpec((tq, gw), qmap), pl.BlockSpec((tq, LANES), qmap),
                     pl.BlockSpec((ncb, HSLOT), kmap), pl.BlockSpec((ncb, HSLOT), vmap_),
                     pl.BlockSpec((seq, HSLOT), kmap), pl.BlockSpec((seq, HSLOT), vmap_),
                     pl.BlockSpec((seq, HSLOT), kmap), pl.BlockSpec((seq, HSLOT), vmap_)]
        operands += [qraw, qrot, gates, ckv, ckv, kvs_b, kvs_b, kvw_b, kvw_b]
    return pl.pallas_call(
        body, grid=(batch, KV_HEADS // ngrp, nq), in_specs=in_specs,
        out_specs=pl.BlockSpec((tq, ngrp * gw), lambda b, p, i: (b * nq + i, p)),
        out_shape=jax.ShapeDtypeStruct((batch * seq, QW), BF16),
        scratch_shapes=[pltpu.VMEM((ngrp, 1, cols), F32), pltpu.VMEM((ngrp, 1, cols), F32),
                        pltpu.VMEM((ngrp, HSLOT, cols), F32), pltpu.VMEM((ngrp, nt, SUBLANES, tq), F32),
                        pltpu.VMEM((ngrp, nt, HSLOT, tk), BF16), pltpu.VMEM((ngrp, nt, HSLOT, tk), BF16)],
        compiler_params=_cparams("parallel", "parallel", "arbitrary"), name="prompt_attention",
    )(*operands)


CONV_HALO = 32


def _conv_body(cur_ref, prev_ref, w_ref, b_ref, g_ref, bn_ref, o_ref, ext_ref, *, ts):
    c = pl.program_id(1)
    ext_ref[0:CONV_HALO, :] = jnp.where(c > 0, prev_ref[...], 0.0)
    ext_ref[CONV_HALO:CONV_HALO + ts, :] = cur_ref[...]
    acc = jnp.zeros((ts, CONV_CH), F32) + b_ref[...]
    off = CONV_HALO - (CONV_W - 1)
    for k in range(CONV_W):
        acc = acc + ext_ref[pl.ds(off + k, ts), :] * w_ref[k:k + 1, :]
    y = _ln_rows(acc, g_ref[...], bn_ref[...])
    o_ref[...] = (y * jax.nn.sigmoid(y)).astype(BF16)


def _prompt_conv(u, cw, cb, cg, cbn, *, batch, seq, ts):
    nt = seq // ts
    r = ts // CONV_HALO
    cwp = jnp.pad(cw, ((0, CONV_HALO - CONV_W), (0, 0)))
    vec = lambda a: a.reshape(1, CONV_CH)
    cst = lambda b, c: (0, 0)
    return pl.pallas_call(
        functools.partial(_conv_body, ts=ts), grid=(batch, nt),
        in_specs=[pl.BlockSpec((ts, CONV_CH), lambda b, c: (b * nt + c, 0)),
                  pl.BlockSpec((CONV_HALO, CONV_CH), lambda b, c: (jnp.maximum((b * nt + c) * r - 1, 0), 0)),
                  pl.BlockSpec((CONV_HALO, CONV_CH), cst),
                  pl.BlockSpec((1, CONV_CH), cst), pl.BlockSpec((1, CONV_CH), cst), pl.BlockSpec((1, CONV_CH), cst)],
        out_specs=pl.BlockSpec((ts, CONV_CH), lambda b, c: (b * nt + c, 0)),
        out_shape=jax.ShapeDtypeStruct((batch * seq, CONV_CH), BF16),
        scratch_shapes=[pltpu.VMEM((CONV_HALO + ts, CONV_CH), F32)],
        compiler_params=_cparams("parallel", "arbitrary"), name="prompt_conv",
    )(u, u, cwp, vec(cb), vec(cg), vec(cbn))


POOL_HALO = 16


def _odd_mix_body(pin_ref, prev_ref, u_ref, vn_ref, pw_ref, ps_ref, sw_ref, sb_ref, o_ref, ext_ref):
    c = pl.program_id(1)
    ext_ref[0:POOL_HALO, :] = jnp.where(c > 0, prev_ref[...], 0.0)
    ext_ref[POOL_HALO:POOL_HALO + CHUNK, :] = pin_ref[...]
    t = c * CHUNK + lax.broadcasted_iota(jnp.int32, (CHUNK, 1), 0)
    for g, w in enumerate(POOL_WINDOWS):
        sl = slice(g * POOL_GC, (g + 1) * POOL_GC)
        tot = ext_ref[pl.ds(POOL_HALO, CHUNK), sl]
        for j in range(1, w):
            tot = tot + ext_ref[pl.ds(POOL_HALO - j, CHUNK), sl]
        cnt = jnp.minimum(w, t + 1).astype(F32)
        d = tot / cnt - pin_ref[:, sl]
        y = jnp.dot(d.astype(BF16), pw_ref[g], preferred_element_type=F32)
        o_ref[:, sl] = (y * ps_ref[:, sl]).astype(BF16)
    ri = lax.broadcasted_iota(jnp.int32, (CHUNK, CHUNK), 0)
    ci = lax.broadcasted_iota(jnp.int32, (CHUNK, CHUNK), 1)
    for g in range(SGU_GROUPS):
        sl = slice(g * SGU_GC, (g + 1) * SGU_GC)
        ws = jnp.where(ci <= ri, sw_ref[g], 0.0).astype(BF16)
        mixed = jnp.dot(ws, vn_ref[:, sl].astype(BF16), preferred_element_type=F32) + sb_ref[:, g:g + 1]
        o_ref[:, POOL_CH + g * SGU_GC:POOL_CH + (g + 1) * SGU_GC] = (u_ref[:, sl] * mixed).astype(BF16)


def _prompt_odd_mix(pin, u, vn, pool_w, pool_scale, sgu_w, sgu_b, *, batch, seq):
    nt = seq // CHUNK
    r = CHUNK // POOL_HALO
    cst2 = lambda b, c: (0, 0)
    cst3 = lambda b, c: (0, 0, 0)
    row = lambda b, c: (b * nt + c, 0)
    return pl.pallas_call(
        _odd_mix_body, grid=(batch, nt),
        in_specs=[pl.BlockSpec((CHUNK, POOL_CH), row),
                  pl.BlockSpec((POOL_HALO, POOL_CH), lambda b, c: (jnp.maximum((b * nt + c) * r - 1, 0), 0)),
                  pl.BlockSpec((CHUNK, SGU_CH), row), pl.BlockSpec((CHUNK, SGU_CH), row),
                  pl.BlockSpec((POOL_GROUPS, POOL_GC, POOL_GC), cst3), pl.BlockSpec((1, POOL_CH), cst2),
                  pl.BlockSpec((SGU_GROUPS, CHUNK, CHUNK), cst3), pl.BlockSpec((CHUNK, SGU_GROUPS), cst2)],
        out_specs=pl.BlockSpec((CHUNK, D_MODEL), row),
        out_shape=jax.ShapeDtypeStruct((batch * seq, D_MODEL), BF16),
        scratch_shapes=[pltpu.VMEM((POOL_HALO + CHUNK, POOL_CH), F32)],
        compiler_params=_cparams("parallel", "arbitrary"), name="prompt_pool_sgu",
    )(pin, pin, u, vn, pool_w.astype(BF16), pool_scale.reshape(1, POOL_CH), sgu_w, sgu_b.T)


def _group_rows(nrows):
    return lax.broadcasted_iota(jnp.int32, (nrows, 1), 0) >> (HPG.bit_length() - 1)


def _sample_cmp_body(q_ref, ckv_ref, o_ref, idx_ref, *, qpos, ncb, nbl, k_past):
    q = q_ref[...].astype(BF16)
    rg = _group_rows(N_HEADS)
    half = nbl // 2
    assert nbl & (nbl - 1) == 0
    sh = nbl.bit_length() - 1
    lane = lax.broadcasted_iota(jnp.int32, (1, ncb), 1)
    grp, w = lane >> sh, lane & (nbl - 1)
    n_cmp = grp * nbl + 2 * (w & (half - 1)) + (w >> (sh - 1))
    mk = jnp.broadcast_to(((n_cmp + 1) * CMP_BLOCK - 1) <= qpos, (N_HEADS, ncb))
    s = jnp.zeros((N_HEADS, ncb), F32)
    for g in range(KV_HEADS):
        ck = ckv_ref[g * HSLOT:(g + 1) * HSLOT, :].astype(BF16)
        s = jnp.where(rg == g, jnp.dot(q, ck, preferred_element_type=F32) * SCALE, s)
    s = jnp.where(mk, s, NEG)
    mx = jnp.max(s, axis=-1, keepdims=True)
    p = jnp.where(mk, jnp.exp(s - mx), 0.0)
    pn = p / jnp.maximum(jnp.sum(p, axis=-1, keepdims=True), 1e-30)
    o = jnp.zeros((N_HEADS, HSLOT), F32)
    for g in range(KV_HEADS):
        cv = ckv_ref[(KV_HEADS + g) * HSLOT:(KV_HEADS + g + 1) * HSLOT, :].astype(BF16)
        o = jnp.where(rg == g, lax.dot_general(pn.astype(BF16), cv, _NT, preferred_element_type=F32), o)
    o_ref[...] = o
    pair = pn + pltpu.roll(pn, ncb - half, axis=1)
    valid = w < half
    sb = grp * half + w
    vis = (sb * SEL_BLOCK) <= qpos
    ri = lax.broadcasted_iota(jnp.int32, (ncb, ncb), 0)
    sb_r = (ri >> sh) * half + (ri & (nbl - 1))
    sb_c = jnp.broadcast_to(sb, (ncb, ncb))
    slot = lax.broadcasted_iota(jnp.int32, (TOP_K, 1), 0)
    for g in range(KV_HEADS):
        imp = jnp.sum(jnp.where(rg == g, pair, 0.0), axis=0, keepdims=True)
        imp = jnp.where(valid, jnp.where(vis, imp, -FORCE), -2.0 * FORCE)
        a = jnp.broadcast_to(imp, (ncb, ncb))
        bt = a.T
        beats = (bt > a) | ((bt == a) & (sb_r < sb_c))
        rank = jnp.sum(jnp.where(beats, 1.0, 0.0), axis=0, keepdims=True)
        onehot = jnp.where((rank == slot.astype(F32)) & valid, 1.0, 0.0)
        idx = jnp.sum(onehot * sb.astype(F32), axis=-1, keepdims=True)
        idx = jnp.where(slot < k_past, idx, 0.0)
        idx_ref[g * TOP_K:(g + 1) * TOP_K, :] = jnp.broadcast_to(idx, (TOP_K, LANES)).astype(jnp.int32)


def _sample_cmp(q3, ckv_t, *, bd, past, qpos, nbl):
    ncb = past // CMP_BLOCK
    k_past = min(TOP_K - 1, past // SEL_BLOCK)
    body = functools.partial(_sample_cmp_body, qpos=qpos, ncb=ncb, nbl=nbl, k_past=k_past)
    head3 = pl.BlockSpec((None, N_HEADS, HSLOT), lambda b: (b, 0, 0))
    return pl.pallas_call(
        body, grid=(bd,),
        in_specs=[head3, pl.BlockSpec((None, KVP, ncb), lambda b: (b, 0, 0))],
        out_specs=[head3, pl.BlockSpec((None, KV_HEADS * TOP_K, LANES), lambda b: (b, 0, 0))],
        out_shape=[jax.ShapeDtypeStruct((bd, N_HEADS, HSLOT), F32),
                   jax.ShapeDtypeStruct((bd, KV_HEADS * TOP_K, LANES), jnp.int32)],
        compiler_params=_cparams("parallel"), name="sample_cmp_attention",
    )(q3, ckv_t)


Q_PAD_ROWS = 8


def _pad_dt(x):
    return _pad_rows_to(x, HSLOT)


def _sample_sel_body(pt_ref, idx_ref, *refs, k_past):
    k_refs, v_refs = refs[:k_past], refs[k_past:2 * k_past]
    q_ref, knew_ref, vnew_ref, o_ref = refs[2 * k_past:]
    b, g = pl.program_id(0), pl.program_id(1)
    bpp = PAGE_SIZE // SEL_BLOCK
    q = q_ref[...].astype(BF16)
    half_of_lane = lax.broadcasted_iota(jnp.int32, (1, PAGE_SIZE), 1) // SEL_BLOCK
    s_parts, m_parts = [], []
    for s in range(k_past):
        kt = _pad_dt(k_refs[s][...]).astype(BF16)
        s_parts.append(jnp.dot(q, kt, preferred_element_type=F32) * SCALE)
        m_parts.append(jnp.broadcast_to(half_of_lane == idx_ref[b, g, s] % bpp, (Q_PAD_ROWS, PAGE_SIZE)))
    s_old = jnp.concatenate(s_parts, axis=1)
    mk = jnp.concatenate(m_parts, axis=1)
    s_old = jnp.where(mk, s_old, NEG)
    s_all = lax.dot_general(q, knew_ref[...].astype(BF16), _NT, preferred_element_type=F32) * SCALE
    lane = lax.broadcasted_iota(jnp.int32, s_all.shape, 1)
    s_new = jnp.sum(jnp.where(lane == b, s_all, 0.0), axis=-1, keepdims=True)
    mx = jnp.maximum(jnp.max(s_old, axis=-1, keepdims=True), s_new)
    p_old = jnp.where(mk, jnp.exp(s_old - mx), 0.0)
    p_new = jnp.exp(s_new - mx)
    den = jnp.maximum(jnp.sum(p_old, axis=-1, keepdims=True) + p_new, 1e-30)
    v_new = vnew_ref[pl.ds(b, 1), :].astype(BF16).astype(F32)
    o = p_new.astype(BF16).astype(F32) * v_new
    for s in range(k_past):
        vt = _pad_dt(v_refs[s][...]).astype(BF16)
        ps = p_old[:, s * PAGE_SIZE:(s + 1) * PAGE_SIZE].astype(BF16)
        o = o + lax.dot_general(ps, vt, _NT, preferred_element_type=F32)
    o_ref[...] = o / den


def _sample_sel(pages5, page_table, idx, layer_base, q4, kvs_new, *, bd, past):
    k_past = min(TOP_K - 1, past // SEL_BLOCK)
    bpp = PAGE_SIZE // SEL_BLOCK

    def blk_map(s, kv):
        def f(b, g, pt, ix):
            return (layer_base + pt[b, ix[b, g, s] // bpp], kv, g, 0, 0)
        return f

    tile = lambda s, kv: pl.BlockSpec((None, None, None, HEAD_DIM, PAGE_SIZE), blk_map(s, kv))
    grid_spec = pltpu.PrefetchScalarGridSpec(
        num_scalar_prefetch=2, grid=(bd, KV_HEADS),
        in_specs=[tile(s, 0) for s in range(k_past)] + [tile(s, 1) for s in range(k_past)]
        + [pl.BlockSpec((None, None, Q_PAD_ROWS, HSLOT), lambda b, g, pt, ix: (b, g, 0, 0)),
           pl.BlockSpec((SAMPLE_ROWS, HSLOT), lambda b, g, pt, ix: (0, g)),
           pl.BlockSpec((SAMPLE_ROWS, HSLOT), lambda b, g, pt, ix: (0, KV_HEADS + g))],
        out_specs=pl.BlockSpec((None, None, Q_PAD_ROWS, HSLOT), lambda b, g, pt, ix: (b, g, 0, 0)))
    return pl.pallas_call(
        functools.partial(_sample_sel_body, k_past=k_past), grid_spec=grid_spec,
        out_shape=jax.ShapeDtypeStruct((bd, KV_HEADS, Q_PAD_ROWS, HSLOT), F32),
        compiler_params=_cparams("parallel", "arbitrary"), name="sample_sel_attention",
    )(page_table, idx, *([pages5] * (2 * k_past)), q4, kvs_new, kvs_new)


def _sample_win_body(q_ref, win_ref, new_ref, ocmp_ref, osel_ref, gate_ref, o_ref, *, qpos, past, wb):
    b = pl.program_id(0)
    q = q_ref[...].astype(BF16)
    qf = q.astype(F32)
    rg = _group_rows(N_HEADS)
    new = new_ref[pl.ds(b, 1), :].astype(BF16).astype(F32)
    kpos = (past - wb) + lax.broadcasted_iota(jnp.int32, (1, wb), 1)
    mk = jnp.broadcast_to((kpos <= qpos) & (kpos >= qpos - WINDOW), (N_HEADS, wb))
    s_old = jnp.zeros((N_HEADS, wb), F32)
    s_new = jnp.zeros((N_HEADS, 1), F32)
    for g in range(KV_HEADS):
        kt = _pad_dt(win_ref[g]).astype(BF16)
        s_old = jnp.where(rg == g, jnp.dot(q, kt, preferred_element_type=F32) * SCALE, s_old)
        sn = jnp.sum(qf * new[:, g * HSLOT:(g + 1) * HSLOT], axis=-1, keepdims=True) * SCALE
        s_new = jnp.where(rg == g, sn, s_new)
    s_old = jnp.where(mk, s_old, NEG)
    mx = jnp.maximum(jnp.max(s_old, axis=-1, keepdims=True), s_new)
    p_old = jnp.where(mk, jnp.exp(s_old - mx), 0.0)
    p_new = jnp.exp(s_new - mx)
    den = jnp.maximum(jnp.sum(p_old, axis=-1, keepdims=True) + p_new, 1e-30)
    o_win = jnp.zeros((N_HEADS, HSLOT), F32)
    for g in range(KV_HEADS):
        vt = _pad_dt(win_ref[KV_HEADS + g]).astype(BF16)
        og = lax.dot_general(p_old.astype(BF16), vt, _NT, preferred_element_type=F32)
        og = og + p_new.astype(BF16).astype(F32) * new[:, (KV_HEADS + g) * HSLOT:(KV_HEADS + g + 1) * HSLOT]
        o_win = jnp.where(rg == g, og, o_win)
    o_win = o_win / den
    gts = gate_ref[...]
    o_ref[...] = gts[:, 0:1] * ocmp_ref[...] + gts[:, 1:2] * osel_ref[...] + gts[:, 2:3] * o_win


def _sample_win(q3, win4, layer, kvw_new, o_cmp, o_sel, gates3, *, bd, past, qpos):
    wb = win4.shape[-1]
    head3 = pl.BlockSpec((None, N_HEADS, HSLOT), lambda b: (b, 0, 0))
    return pl.pallas_call(
        functools.partial(_sample_win_body, qpos=qpos, past=past, wb=wb), grid=(bd,),
        in_specs=[head3, pl.BlockSpec((None, 2 * KV_HEADS, HEAD_DIM, wb), lambda b: (layer * bd + b, 0, 0, 0)),
                  pl.BlockSpec((SAMPLE_ROWS, KVP), lambda b: (0, 0)), head3, head3, head3],
        out_specs=head3,
        out_shape=jax.ShapeDtypeStruct((bd, N_HEADS, HSLOT), F32),
        compiler_params=_cparams("parallel"), name="sample_win_attention",
    )(q3, win4, kvw_new, o_cmp, o_sel, gates3)


def _sample_conv_body(st_ref, u_ref, w_ref, b_ref, g_ref, bn_ref, o_ref, *, bd):
    w = w_ref[...]
    y = jnp.sum(st_ref[...] * w[None, :CONV_W - 1, :], axis=1) + u_ref[0:bd, :] * w[CONV_W - 1:CONV_W, :] + b_ref[...]
    y = _ln_rows(y, g_ref[...], bn_ref[...])
    o_ref[...] = y * jax.nn.sigmoid(y)


def _sample_conv(state, layer, u, cw, cb, cg, cbn, *, bd):
    vec = lambda a: a.reshape(1, CONV_CH)
    cst = lambda i: (0, 0)
    return pl.pallas_call(
        functools.partial(_sample_conv_body, bd=bd), grid=(1,),
        in_specs=[pl.BlockSpec((None, bd, CONV_W - 1, CONV_CH), lambda i: (layer, 0, 0, 0)),
                  pl.BlockSpec((SAMPLE_ROWS, CONV_CH), cst), pl.BlockSpec((CONV_W, CONV_CH), cst),
                  pl.BlockSpec((1, CONV_CH), cst), pl.BlockSpec((1, CONV_CH), cst), pl.BlockSpec((1, CONV_CH), cst)],
        out_specs=pl.BlockSpec((bd, CONV_CH), cst),
        out_shape=jax.ShapeDtypeStruct((bd, CONV_CH), F32),
        compiler_params=_cparams("arbitrary"), name="sample_conv",
    )(state, u, cw, vec(cb), vec(cg), vec(cbn))


def _sample_odd_body(st_ref, pin_ref, u_ref, vn_ref, pw_ref, ps_ref, w0_ref, b0_ref, o_ref, *, bd, start_pos):
    pin = pin_ref[0:bd, :]
    st = st_ref[...]
    for g, w in enumerate(POOL_WINDOWS):
        sl = slice(g * POOL_GC, (g + 1) * POOL_GC)
        tot = pin[:, sl] + jnp.sum(st[:, POOL_STATE - (w - 1):, sl], axis=1)
        d = tot / float(min(w, start_pos + 1)) - pin[:, sl]
        dp = jnp.concatenate([d, jnp.zeros((SAMPLE_ROWS - bd, POOL_GC), F32)], axis=0).astype(BF16)
        y = jnp.dot(dp, pw_ref[g], preferred_element_type=F32)[0:bd]
        o_ref[:, sl] = y * ps_ref[:, sl]
    mixed = w0_ref[...] * vn_ref[0:bd, :] + b0_ref[...]
    o_ref[:, POOL_CH:] = u_ref[0:bd, :] * mixed


def _sample_odd_mix(state, layer, pin, u, vn, pool_w, pool_scale, sgu_w, sgu_b, *, bd, start_pos):
    w0 = jnp.repeat(sgu_w[:, 0, 0], SGU_GC).reshape(1, SGU_CH)
    b0 = jnp.repeat(sgu_b[:, 0], SGU_GC).reshape(1, SGU_CH)
    cst = lambda i: (0, 0)
    return pl.pallas_call(
        functools.partial(_sample_odd_body, bd=bd, start_pos=start_pos), grid=(1,),
        in_specs=[pl.BlockSpec((None, bd, POOL_STATE, POOL_CH), lambda i: (layer, 0, 0, 0)),
                  pl.BlockSpec((SAMPLE_ROWS, POOL_CH), cst), pl.BlockSpec((SAMPLE_ROWS, SGU_CH), cst),
                  pl.BlockSpec((SAMPLE_ROWS, SGU_CH), cst),
                  pl.BlockSpec((POOL_GROUPS, POOL_GC, POOL_GC), lambda i: (0, 0, 0)),
                  pl.BlockSpec((1, POOL_CH), cst), pl.BlockSpec((1, SGU_CH), cst), pl.BlockSpec((1, SGU_CH), cst)],
        out_specs=pl.BlockSpec((bd, D_MODEL), cst),
        out_shape=jax.ShapeDtypeStruct((bd, D_MODEL), F32),
        compiler_params=_cparams("arbitrary"), name="sample_pool_sgu",
    )(state, pin, u, vn, pool_w.astype(BF16), pool_scale.reshape(1, POOL_CH), w0, b0)


def _pad_rows(x, rows):
    return jnp.pad(x, ((0, rows - x.shape[0]), (0, 0)))


def _split_to_nat(x, nheads):
    y = _unpad_heads(x, nheads)
    return _pad_nat(y.reshape(y.shape[:-1] + (nheads, HEAD_DIM))).reshape(x.shape)


def _nat_to_split(x, nheads):
    xh = x.reshape(x.shape[:-1] + (nheads, HSLOT))[..., :HEAD_DIM]
    return _pad_head(xh).reshape(x.shape)


def _even_in_proj(xb, xsb, w_in, rope_p, rope_s, *, tm, bs):
    nrep = rope_p[0].shape[0] // tm
    rs = xsb.shape[0]
    tab = lambda i, j: (i % nrep, 0)
    rope_ex = tuple((t, (tm, LANES), tab) for t in rope_p)
    rope_sx = tuple((t, (rs, LANES), lambda i, j: (0, 0)) for t in rope_s)
    gw = HPG * HSLOT
    qraw, qrot, qraw_s, qrot_s = _mm(
        xb, w_in, tm=tm, tn=gw, n_off=E_Q, n_cols=QW, epilogue=_ep_q, extras=rope_ex,
        outs=((QW, BF16, gw), (QW, BF16, gw)), name="even_in_q",
        side=(xsb, _ep_q, rope_sx, ((QW, BF16, gw), (QW, BF16, gw))))
    nst = bs[1] // tm
    row = {"f32": (KVP, F32, KVP), "bf16": (KVP, BF16, KVP),
           "t": ((bs[0], KVW, bs[1]), F32, (None, KVW, tm), lambda i, j: (i // nst, 0, i % nst))}
    kv_p, kv_s = [], []
    for sec, (off, want) in enumerate(zip((E_KVC, E_KVS, E_KVW), (("f32", "t"), ("bf16", "t"), ("bf16", "t")))):
        res = _mm(xb, w_in, tm=tm, tn=KVP, n_off=off, n_cols=KVP,
                  epilogue=functools.partial(_ep_kv, rope=sec > 0, want=want),
                  extras=rope_ex if sec > 0 else (), outs=tuple(row[k] for k in want),
                  name=("even_in_kvc", "even_in_kvs", "even_in_kvw")[sec],
                  side=(xsb, functools.partial(_ep_kv, rope=sec > 0, want=("f32",)),
                        rope_sx if sec > 0 else (), (row["f32"],)))
        kv_p.append(res[:2])
        kv_s.append(res[2])
    u, u_s = _mm(xb, w_in, tm=tm, tn=CONV_CH, n_off=E_GLU, n_cols=2 * CONV_CH, epilogue=_ep_glu,
                 outs=((CONV_CH, F32, CONV_CH // 2),), name="even_in_glu",
                 side=(xsb, _ep_glu, (), ((CONV_CH, F32, CONV_CH // 2),)))
    gates, gates_s = _mm(xb, w_in, tm=tm, tn=GATE_W, n_off=E_GATE, n_cols=GATE_W, epilogue=_ep_sigmoid,
                         outs=((GATE_W, F32, GATE_W),), name="even_in_gates",
                         side=(xsb, _ep_sigmoid, (), ((GATE_W, F32, GATE_W),)))
    return (qraw, qrot, kv_p, u, gates), (qraw_s, qrot_s, kv_s, u_s, gates_s)


def _mlp_up_cast(xb, xsb, w1, w2, layer, *, tm, tn):
    m, k = xb.shape
    rs = xsb.shape[0]
    dff = w1.shape[2]
    d_out = w2.shape[2]
    ni, nj = m // tm, dff // tn
    slab = dff // (ni * nj)
    assert m % tm == 0 and dff % tn == 0 and dff % (ni * nj) == 0 and slab % SAMPLE_ROWS == 0
    rsub = min(tm, ROW_SUB)

    def body(x_ref, xs_ref, w1_ref, w2_ref, h_ref, hs_ref, w2b_ref, w1b_ref):
        def act(rows_ref, rows):
            a = jnp.maximum(jnp.dot(rows_ref[rows, :], w1b_ref[...], preferred_element_type=F32), 0.0)
            return (a * a).astype(BF16)

        @pl.when(pl.program_id(1) == 0)
        def _():
            w1b_ref[...] = w1_ref[...].astype(BF16)
            hs_ref[...] = act(xs_ref, slice(0, rs))

        w2b_ref[...] = w2_ref[...].astype(BF16)
        for r in range(tm // rsub):
            rows = slice(r * rsub, (r + 1) * rsub)
            h_ref[rows, :] = act(x_ref, rows)

    return pl.pallas_call(
        body, grid=(nj, ni),
        in_specs=[pl.BlockSpec((tm, k), lambda j, i: (i, 0)), pl.BlockSpec((rs, k), lambda j, i: (0, 0)),
                  pl.BlockSpec((None, k, tn), lambda j, i: (layer, 0, j)),
                  pl.BlockSpec((None, slab, d_out), lambda j, i: (layer, j * ni + i, 0))],
        out_specs=[pl.BlockSpec((tm, tn), lambda j, i: (i, j)), pl.BlockSpec((rs, tn), lambda j, i: (0, j)),
                   pl.BlockSpec((slab, d_out), lambda j, i: (j * ni + i, 0))],
        out_shape=[jax.ShapeDtypeStruct((m, dff), BF16), jax.ShapeDtypeStruct((rs, dff), BF16),
                   jax.ShapeDtypeStruct((dff, d_out), BF16)],
        scratch_shapes=[pltpu.VMEM((k, tn), BF16)],
        compiler_params=_cparams("arbitrary", "arbitrary"), name="mlp_up_cast",
    )(xb, xsb, w1, w2)


def kernel(x_prompt, x_sample, cache_cmp_kv, cache_sel_kv, cache_win_kv, state_conv, state_pool, page_table,
           w_in_even, w_out_even, cmp_pe_k, cmp_pe_v, cmp_w_k, cmp_w_v, conv_w, conv_b, conv_ln_g, conv_ln_b,
           w_in_odd, w_out_odd, pool_w, pool_scale, sgu_ln_g, sgu_ln_b, sgu_w, sgu_b,
           mlp_w1, mlp_w2, ln_mix_g, ln_mix_b, ln_ffn_g, ln_ffn_b):
    B, S, D = x_prompt.shape
    Bd, Sd, _ = x_sample.shape
    n_pages = page_table.shape[1]
    past = n_pages * PAGE_SIZE
    n_even, n_pool = cache_cmp_kv.shape[:2]
    wb = cache_win_kv.shape[2]
    assert D == D_MODEL and Sd == 1 and Bd <= SAMPLE_ROWS
    assert S % 1024 == 0 and past % SEL_BLOCK == 0 and S >= WINDOW
    M = B * S
    Ms = SAMPLE_ROWS
    tm_p = 1024

    rope_p = _rope_tables(jnp.arange(S, dtype=jnp.int32))
    rope_s = _rope_tables(jnp.full((Ms,), past, jnp.int32))
    pps = min(32, n_pages)
    assert n_pages % pps == 0
    bsum = _block_sum_matrices(pps)
    cmp_t = cache_cmp_kv.transpose(0, 1, 3, 4, 5, 2).reshape(n_even * n_pool, KVW, PAGE_SIZE)
    sel_t = cache_sel_kv.transpose(0, 1, 3, 4, 5, 2).reshape(n_even * n_pool, 2, KV_HEADS, HEAD_DIM, PAGE_SIZE)
    win_t = cache_win_kv.transpose(0, 1, 3, 4, 5, 2).reshape(n_even * Bd, 2 * KV_HEADS, HEAD_DIM, wb)

    xp = x_prompt.reshape(M, D)
    xs = _pad_rows(x_sample.reshape(Bd, D), Ms)
    xpb, xsb = xp.astype(BF16), xs.astype(BF16)

    outs = {k: [] for k in ("cmp_p", "cmp_s", "sel_p", "sel_s", "win_p", "win_s", "conv_p", "conv_s",
                            "pool_p", "pool_s", "sgu_p", "sgu_s")}
    kv6 = lambda a, lead: a.reshape(lead + (2, KV_HEADS, HEAD_DIM))

    for layer in range(DEPTH):
        if layer % 2 == 0:
            e = layer // 2
            wts = _prep_even_weights(w_in_even[e], w_out_even[e], cmp_pe_k[e], cmp_pe_v[e], cmp_w_k[e], cmp_w_v[e])
            prj_p, prj_s = _even_in_proj(xpb, xsb, wts["w_in"], rope_p, rope_s, tm=tm_p, bs=(B, S))
            qraw, qrot, ((kvc, kvc_t), (kvs_b, kvs_t), (kvw_b, kvw_t)), u, gates = prj_p
            summ = _compress_rows(kvc, wts["pe"], rows=512, name="prompt_compress")
            (ckv,) = _mm(summ.astype(BF16), wts["big_p"], tm=min(summ.shape[0], 512), tn=KVP,
                         n_off=0, n_cols=KVP, epilogue=_ep_plain, outs=((KVP, F32, KVP),), name="prompt_compress_map")
            o_att = _prompt_attention(qraw, qrot, gates, ckv, kvs_b, kvw_b, batch=B, seq=S, tq=256, tk=256, ngrp=2)
            c = _prompt_conv(u, conv_w[e], conv_b[e], conv_ln_g[e], conv_ln_b[e], batch=B, seq=S, ts=256)
            rows_last = lambda a: a.reshape(B, 2, KV_HEADS, HEAD_DIM, a.shape[-1]).transpose(0, 4, 1, 2, 3)
            outs["cmp_p"].append(rows_last(kvc_t))
            outs["sel_p"].append(rows_last(kvs_t))
            outs["win_p"].append(rows_last(kvw_t[:, :, S - WINDOW:]))
            outs["conv_p"].append(u.reshape(B, S, CONV_CH)[:, S - (CONV_W - 1):])
            qraw_s, qrot_s, (kvc_s, kvs_s, kvw_s), u_s, gates_s = prj_s
            ckv_t = _compress_pages(cmp_t, page_table, wts["pe_t"], wts["big_t"], bsum, e * n_pool, pps=pps,
                                    name="sample_compress")
            q3 = qraw_s.astype(F32)[:Bd].reshape(Bd, N_HEADS, HSLOT)
            o_cmp, idx = _sample_cmp(q3, ckv_t, bd=Bd, past=past, qpos=past, nbl=pps * (PAGE_SIZE // CMP_BLOCK))
            idx = idx[:, :, 0].reshape(Bd, KV_HEADS, TOP_K)
            qr3 = _split_to_nat(qrot_s.astype(F32)[:Bd], N_HEADS).reshape(Bd, N_HEADS, HSLOT)
            q4 = jnp.pad(qr3.reshape(Bd, KV_HEADS, HPG, HSLOT), ((0, 0), (0, 0), (0, Q_PAD_ROWS - HPG), (0, 0)))
            kvs_nat = _split_to_nat(kvs_s, 2 * KV_HEADS)
            kvw_nat = _split_to_nat(kvw_s, 2 * KV_HEADS)
            o_sel = _sample_sel(sel_t, page_table, idx, e * n_pool, q4, kvs_nat, bd=Bd, past=past)
            o_sel = o_sel[:, :, :HPG].reshape(Bd, N_HEADS, HSLOT)
            g3 = gates_s[:Bd].reshape(Bd, KV_HEADS, LANES)[:, :, :3 * HPG].reshape(Bd, KV_HEADS, 3, HPG)
            g3 = g3.transpose(0, 1, 3, 2).reshape(Bd, N_HEADS, 3)
            g3 = jnp.pad(g3, ((0, 0), (0, 0), (0, LANES - 3)))
            o_s = _sample_win(qr3, win_t, e, kvw_nat, o_cmp, o_sel, g3, bd=Bd, past=past, qpos=past)
            c_s = _sample_conv(state_conv, e, u_s, conv_w[e], conv_b[e], conv_ln_g[e], conv_ln_b[e], bd=Bd)
            o_sb = _pad_rows(_nat_to_split(o_s.reshape(Bd, QW), N_HEADS), Ms).astype(BF16)
            c_sb = _pad_rows(c_s, Ms).astype(BF16)
            xp, xpb, xs, xsb = _proj_ln([o_att, c], [wts["wo_att"], wts["wo_conv"]], xp, ln_mix_g[layer],
                                        ln_mix_b[layer], tm=512, name="even_out_ln", side=([o_sb, c_sb], xs))
            kvc_c = _unpad_heads(kvc_s[:Bd], 2 * KV_HEADS)
            kvs_c = _unpad_heads(kvs_s[:Bd], 2 * KV_HEADS)
            kvw_c = _unpad_heads(kvw_s[:Bd], 2 * KV_HEADS)
            outs["cmp_s"].append(kv6(kvc_c, (Bd, 1)))
            outs["sel_s"].append(kv6(kvs_c, (Bd, 1)))
            wkv = jnp.concatenate([cache_win_kv[e], kv6(kvw_c, (Bd, 1))], axis=1)
            outs["win_s"].append(wkv[:, wkv.shape[1] - min(WINDOW, wkv.shape[1]):])
            outs["conv_s"].append(jnp.concatenate([state_conv[e], u_s[:Bd, None, :]], axis=1)[:, 1:])
        else:
            o = layer // 2
            w_in = w_in_odd[o]
            w_in_p = jnp.concatenate([w_in[:, POOL_CH + SGU_CH:], w_in[:, :POOL_CH], w_in[:, POOL_CH:POOL_CH + SGU_CH]],
                                     axis=1).astype(BF16)
            w_out_p = w_out_odd[o].astype(BF16)
            lg, lb = sgu_ln_g[o].reshape(1, SGU_CH), sgu_ln_b[o].reshape(1, SGU_CH)

            gl_ex = ((lg, (1, 2 * SGU_GC), lambda i, j: (0, j)), (lb, (1, 2 * SGU_GC), lambda i, j: (0, j)))
            o_v, o_p, o_u = (SGU_CH, F32, 2 * SGU_GC), (POOL_CH, F32, POOL_CH), (SGU_CH, F32, POOL_CH)
            vn, vn_s = _mm(xpb, w_in_p, tm=tm_p, tn=2 * SGU_GC, n_off=O_V, n_cols=SGU_CH, epilogue=_ep_gelu_gln,
                           extras=gl_ex, outs=(o_v,), name="odd_in_v", side=(xsb, _ep_gelu_gln, gl_ex, (o_v,)))
            pin, pin_s = _mm(xpb, w_in_p, tm=tm_p, tn=POOL_CH, n_off=O_PIN, n_cols=POOL_CH, epilogue=_ep_plain,
                             outs=(o_p,), name="odd_in_pool", side=(xsb, _ep_plain, (), (o_p,)))
            uu, uu_s = _mm(xpb, w_in_p, tm=tm_p, tn=POOL_CH, n_off=O_U, n_cols=SGU_CH, epilogue=_ep_gelu,
                           outs=(o_u,), name="odd_in_u", side=(xsb, _ep_gelu, (), (o_u,)))
            cat = _prompt_odd_mix(pin, uu, vn, pool_w[o], pool_scale[o], sgu_w[o], sgu_b[o], batch=B, seq=S)
            outs["pool_p"].append(pin.reshape(B, S, POOL_CH)[:, S - POOL_STATE:])
            outs["sgu_p"].append(vn.reshape(B, S, SGU_CH)[:, ((S - 1) // CHUNK) * CHUNK:])
            cat_s = _sample_odd_mix(state_pool, o, pin_s, uu_s, vn_s, pool_w[o], pool_scale[o], sgu_w[o], sgu_b[o],
                                    bd=Bd, start_pos=past)
            xp, xpb, xs, xsb = _proj_ln([cat], [w_out_p], xp, ln_mix_g[layer], ln_mix_b[layer], tm=512,
                                        name="odd_out_ln", side=([_pad_rows(cat_s, Ms).astype(BF16)], xs))
            outs["pool_s"].append(jnp.concatenate([state_pool[o], pin_s[:Bd, None, :]], axis=1)[:, 1:])
            outs["sgu_s"].append(vn_s[:Bd, None, :])
        h, h_s, w2b = _mlp_up_cast(xpb, xsb, mlp_w1, mlp_w2, layer, tm=tm_p, tn=1024)
        xp, xpb, xs, xsb = _mlp2_ln(h, w2b, xp, ln_ffn_g[layer], ln_ffn_b[layer], tm=512, tn=512,
                                    name="mlp_down_ln", side=(h_s, xs))

    st = lambda k: jnp.stack(outs[k])
    return (xp.reshape(B, S, D), xs[:Bd].reshape(Bd, Sd, D),
            st("cmp_p"), st("cmp_s"), st("sel_p"), st("sel_s"), st("win_p"), st("win_s"),
            st("conv_p"), st("conv_s"), st("pool_p"), st("pool_s"), st("sgu_p"), st("sgu_s"))
```

```python
import functools

import jax
import jax.numpy as jnp
from jax import lax
from jax.experimental import pallas as pl
from jax.experimental.pallas import tpu as pltpu

F32 = jnp.float32
BF16 = jnp.bfloat16

D_MODEL = 2048
DEPTH = 4
PAGE_SIZE = 128
N_HEADS = 16
HEAD_DIM = 96
KV_HEADS = 4
HPG = N_HEADS // KV_HEADS
ATT_W = N_HEADS * HEAD_DIM
KVW = 2 * KV_HEADS * HEAD_DIM
CMP_BLOCK = 32
SEL_BLOCK = 64
TOP_K = 16
WINDOW = 512
ROPE_THETA = 10000.0
SCALE = HEAD_DIM ** -0.5
LOG2E = 1.4426950408889634
FORCE = 1e9
NEG = -1e30
CONV_CH = D_MODEL // 4
CONV_W = 31
POOL_CH = D_MODEL // 4
POOL_WINDOWS = (2, 4, 8, 16)
POOL_GROUPS = len(POOL_WINDOWS)
POOL_GC = POOL_CH // POOL_GROUPS
POOL_STATE = max(POOL_WINDOWS) - 1
SGU_CH = D_MODEL - POOL_CH
SGU_GROUPS = 4
SGU_GC = SGU_CH // SGU_GROUPS
CHUNK = 128
D_FF = 4 * D_MODEL
ALPHA = (2 * DEPTH) ** 0.25
LN_EPS = 1e-5

LANES = 128
SUBLANES = 8
HALF = HEAD_DIM // 2
HSLOT = LANES
HALF_OFF = LANES // 2
QW = N_HEADS * HSLOT
KVP = 2 * KV_HEADS * HSLOT
GATE_W = KV_HEADS * LANES
SAMPLE_ROWS = 16
ROW_SUB = 256
LN_ROW_SUB = 128
VMEM_LIMIT = 52 * 1024 * 1024

E_Q, E_KVC, E_KVS, E_KVW = 0, QW, QW + KVP, QW + 2 * KVP
E_GLU = QW + 3 * KVP
E_GATE = E_GLU + 2 * CONV_CH
E_TOT = E_GATE + GATE_W
O_V, O_PIN, O_U = 0, SGU_CH, SGU_CH + POOL_CH


def _cparams(*sem):
    return pltpu.CompilerParams(dimension_semantics=sem, vmem_limit_bytes=VMEM_LIMIT)


def _pad_head(x):
    halves = x.reshape(x.shape[:-1] + (2, HALF))
    halves = jnp.pad(halves, [(0, 0)] * (halves.ndim - 1) + [(0, HALF_OFF - HALF)])
    return halves.reshape(x.shape[:-1] + (HSLOT,))


def _pad_nat(x):
    return jnp.concatenate([x, jnp.zeros(x.shape[:-1] + (HSLOT - HEAD_DIM,), x.dtype)], axis=-1)


def _unpad_heads(x, nheads):
    xh = x.reshape(x.shape[:-1] + (nheads, HSLOT))
    y = jnp.concatenate([xh[..., :HALF], xh[..., HALF_OFF:HALF_OFF + HALF]], axis=-1)
    return y.reshape(x.shape[:-1] + (nheads * HEAD_DIM,))


def _rope_tables(pos):
    inv = jnp.power(ROPE_THETA, -jnp.arange(HALF, dtype=F32) / HALF)
    ang = pos.astype(F32)[:, None] * inv[None, :]
    cos, sin = jnp.cos(ang), jnp.sin(ang)
    z = jnp.zeros((pos.shape[0], HALF_OFF - HALF), F32)
    return (jnp.concatenate([cos, z, cos, z], axis=1),
            jnp.concatenate([-sin, z, sin, z], axis=1))


def _block_diag2(a, b):
    za = jnp.zeros((a.shape[0], b.shape[1]), a.dtype)
    zb = jnp.zeros((b.shape[0], a.shape[1]), a.dtype)
    return jnp.concatenate([jnp.concatenate([a, za], axis=1), jnp.concatenate([zb, b], axis=1)], axis=0)


def _prep_even_weights(w_in, w_out, pe_k, pe_v, w_ck, w_cv):
    d = w_in.shape[0]
    wb = w_in.astype(BF16)
    g0 = ATT_W + 3 * KVW
    n_heads_all = g0 // HEAD_DIM
    qkv = _pad_head(wb[:, :g0].reshape(d, n_heads_all, HEAD_DIM)).reshape(d, n_heads_all * HSLOT)
    gates = wb[:, g0:g0 + 3 * N_HEADS].reshape(d, 3, KV_HEADS, HPG).transpose(0, 2, 1, 3)
    gates = gates.reshape(d, KV_HEADS, 3 * HPG)
    gates = jnp.pad(gates, ((0, 0), (0, 0), (0, LANES - 3 * HPG))).reshape(d, GATE_W)
    glu = wb[:, g0 + 3 * N_HEADS:].reshape(d, 2, 2, CONV_CH // 2).transpose(0, 2, 1, 3)
    glu = glu.reshape(d, 2 * CONV_CH)
    w_in_p = jnp.concatenate([qkv, glu, gates], axis=1)
    wo = w_out.astype(BF16)
    wo_att = jnp.pad(wo[:ATT_W].reshape(2 * N_HEADS, HALF, d), ((0, 0), (0, HALF_OFF - HALF), (0, 0)))
    wo_att = wo_att.reshape(QW, d)
    wo_conv = wo[ATT_W:]
    eye = jnp.eye(KV_HEADS, dtype=F32)
    pe = jnp.concatenate([jnp.tile(_pad_head(pe_k), (1, KV_HEADS)), jnp.tile(_pad_head(pe_v), (1, KV_HEADS))], axis=1)
    wk_full = _pad_head(_pad_head(w_ck).T).T
    wv_full = _pad_head(_pad_head(w_cv).T).T
    big_p = _block_diag2(jnp.kron(eye, wk_full), jnp.kron(eye, wv_full))
    pe_t = jnp.concatenate([jnp.tile(jnp.tile(pe_k.T, (1, PAGE_SIZE // CMP_BLOCK)), (KV_HEADS, 1)),
                            jnp.tile(jnp.tile(pe_v.T, (1, PAGE_SIZE // CMP_BLOCK)), (KV_HEADS, 1))], axis=0)
    big_t = _block_diag2(jnp.kron(eye, _pad_head(w_ck).T), jnp.kron(eye, _pad_nat(w_cv).T))
    return dict(w_in=w_in_p, wo_att=wo_att, wo_conv=wo_conv, pe=pe, pe_t=pe_t,
                big_p=big_p.astype(BF16), big_t=big_t.astype(BF16))


def _block_sum_matrices(pps):
    bpp = PAGE_SIZE // CMP_BLOCK
    nbl = bpp * pps
    p = jnp.arange(pps)[:, None, None]
    i = (jnp.arange(PAGE_SIZE) // CMP_BLOCK)[None, :, None]
    c = jnp.arange(nbl)[None, None, :]
    col = (i % 2) * (nbl // 2) + (bpp // 2) * p + i // 2
    return jnp.where(c == col, 1.0 / CMP_BLOCK, 0.0).astype(BF16)


def _ln_rows(y, g, b):
    mu = jnp.mean(y, axis=-1, keepdims=True)
    yc = y - mu
    var = jnp.mean(yc * yc, axis=-1, keepdims=True)
    return yc * lax.rsqrt(var + LN_EPS) * g + b


def _rope_slot(x, cos, sin):
    return x * cos + pltpu.roll(x, HALF_OFF, axis=1) * sin


_NT = (((1,), (1,)), ((), ()))


def _pad_rows_to(x, rows):
    return jnp.concatenate([x, jnp.zeros((rows - x.shape[0],) + x.shape[1:], x.dtype)], axis=0)


def _mm(x, w, *, tm, tn, n_off, n_cols, epilogue, extras=(), outs, name, side=None, alias=None):
    m, k = x.shape
    assert m % tm == 0 and n_cols % tn == 0 and n_off % tn == 0
    joff = n_off // tn
    ji = lambda im: (lambda j, i: im(i, j))
    in_specs = [pl.BlockSpec((tm, k), lambda j, i: (i, 0)),
                pl.BlockSpec((k, tn), lambda j, i: (0, joff + j))]
    in_specs += [pl.BlockSpec(bs, ji(im)) for _, bs, im in extras]
    operands = [x, w] + [a for a, _, _ in extras]
    out_shape, out_specs = [], []
    for o in outs:
        if len(o) == 3:
            out_shape.append(jax.ShapeDtypeStruct((m, o[0]), o[1]))
            out_specs.append(pl.BlockSpec((tm, o[2]), lambda j, i: (i, j)))
        else:
            out_shape.append(jax.ShapeDtypeStruct(o[0], o[1]))
            out_specs.append(pl.BlockSpec(o[2], ji(o[3])))
    ne, no = len(extras), len(outs)
    nse = 0
    if side is not None:
        xs, s_epilogue, s_extras, s_outs = side
        rs = xs.shape[0]
        nse = len(s_extras)
        in_specs += [pl.BlockSpec((rs, k), lambda j, i: (0, 0))]
        in_specs += [pl.BlockSpec(bs, ji(im)) for _, bs, im in s_extras]
        operands += [xs] + [a for a, _, _ in s_extras]
        for cols, dt, bc in s_outs:
            out_shape.append(jax.ShapeDtypeStruct((rs, cols), dt))
            out_specs.append(pl.BlockSpec((rs, bc), lambda j, i: (0, j)))
    rsub = min(tm, ROW_SUB)
    n_in = len(operands)
    io_alias = {}
    if alias is not None:
        io_alias = {n_in: alias[1]}
        in_specs.append(pl.BlockSpec(memory_space=pl.ANY))
        operands.append(alias[0])

    def body(*refs):
        x_ref, w_ref = refs[:2]
        ex = refs[2:2 + ne]
        o_refs = refs[len(operands):len(operands) + no]
        for r in range(tm // rsub):
            rows = slice(r * rsub, (r + 1) * rsub)
            acc = jnp.dot(x_ref[rows, :], w_ref[...], preferred_element_type=F32)
            epilogue(acc, ex, o_refs, rows)
        if side is not None:
            @pl.when(pl.program_id(1) == 0)
            def _():
                acc = jnp.dot(refs[2 + ne][...], w_ref[...], preferred_element_type=F32)
                s_epilogue(acc, refs[3 + ne:n_in], refs[len(operands) + no:], slice(0, rs))

    return pl.pallas_call(
        body, grid=(n_cols // tn, m // tm), in_specs=in_specs, out_specs=out_specs, out_shape=out_shape,
        input_output_aliases=io_alias, compiler_params=_cparams("arbitrary", "arbitrary"), name=name,
    )(*operands)


def _ep_q(acc, ex, outs, rows):
    cos, sin = ex[0][rows, :], ex[1][rows, :]
    outs[0][rows, :] = acc.astype(BF16)
    for j in range(acc.shape[1] // HSLOT):
        sl = slice(j * HSLOT, (j + 1) * HSLOT)
        outs[1][rows, sl] = _rope_slot(acc[:, sl], cos, sin).astype(BF16)


def _ep_kv(acc, ex, outs, rows, *, rope, want):
    o = dict(zip(want, outs))
    if rope:
        cos, sin = ex[0][rows, :], ex[1][rows, :]
    for j in range(2 * KV_HEADS):
        sl = slice(j * HSLOT, (j + 1) * HSLOT)
        x = acc[:, sl]
        if rope and j < KV_HEADS:
            x = _rope_slot(x, cos, sin)
        if "f32" in o:
            o["f32"][rows, sl] = x
        if "bf16" in o:
            o["bf16"][rows, sl] = x.astype(BF16)
        if "t" in o:
            xt = x.T
            o["t"][j * HEAD_DIM:j * HEAD_DIM + HALF, rows] = xt[0:HALF]
            o["t"][j * HEAD_DIM + HALF:(j + 1) * HEAD_DIM, rows] = xt[HALF_OFF:HALF_OFF + HALF]


def _ep_glu(acc, ex, outs, rows):
    h = acc.shape[1] // 2
    outs[0][rows, :] = acc[:, :h] * jax.nn.sigmoid(acc[:, h:])


def _ep_sigmoid(acc, ex, outs, rows):
    outs[0][rows, :] = jax.nn.sigmoid(acc)


def _ep_plain(acc, ex, outs, rows):
    outs[0][rows, :] = acc.astype(outs[0].dtype)


def _ep_relu2(acc, ex, outs, rows):
    r = jnp.maximum(acc, 0.0)
    outs[0][rows, :] = (r * r).astype(outs[0].dtype)


def _ep_gelu(acc, ex, outs, rows):
    outs[0][rows, :] = jax.nn.gelu(acc)


def _ep_gelu_gln(acc, ex, outs, rows):
    g, b = ex[0][...], ex[1][...]
    v = jax.nn.gelu(acc)
    for j in range(acc.shape[1] // SGU_GC):
        sl = slice(j * SGU_GC, (j + 1) * SGU_GC)
        outs[0][rows, sl] = _ln_rows(v[:, sl], g[:, sl], b[:, sl])


def _proj_ln(a_list, w_list, resid, g, b, *, tm, name, side):
    m, n = resid.shape
    as_list, resid_s = side
    rs = resid_s.shape[0]
    npair = len(a_list)
    in_specs = []
    for a in a_list:
        in_specs.append(pl.BlockSpec((tm, a.shape[1]), lambda i: (i, 0)))
    for w in w_list:
        in_specs.append(pl.BlockSpec(w.shape, lambda i: (0, 0), pipeline_mode=pl.Buffered(1)))
    in_specs += [pl.BlockSpec((tm, n), lambda i: (i, 0)),
                 pl.BlockSpec((1, n), lambda i: (0, 0)), pl.BlockSpec((1, n), lambda i: (0, 0))]
    for a in as_list:
        in_specs.append(pl.BlockSpec((rs, a.shape[1]), lambda i: (0, 0)))
    in_specs.append(pl.BlockSpec((rs, n), lambda i: (0, 0)))

    def body(*refs):
        a_refs, w_refs = refs[:npair], refs[npair:2 * npair]
        r_ref, g_ref, b_ref = refs[2 * npair:2 * npair + 3]
        as_refs = refs[2 * npair + 3:3 * npair + 3]
        rs_ref, o_ref, ob_ref, os_ref, osb_ref = refs[3 * npair + 3:]

        def rows_out(a_rs, res, rows):
            acc = ALPHA * res[rows, :]
            for a_ref, w_ref in zip(a_rs, w_refs):
                acc = acc + jnp.dot(a_ref[rows, :], w_ref[...], preferred_element_type=F32)
            return _ln_rows(acc, g_ref[...], b_ref[...])

        rsub = min(tm, ROW_SUB)
        for r in range(tm // rsub):
            rows = slice(r * rsub, (r + 1) * rsub)
            y = rows_out(a_refs, r_ref, rows)
            o_ref[rows, :] = y
            ob_ref[rows, :] = y.astype(BF16)
        @pl.when(pl.program_id(0) == 0)
        def _():
            ys = rows_out(as_refs, rs_ref, slice(0, rs))
            os_ref[...] = ys
            osb_ref[...] = ys.astype(BF16)

    row = pl.BlockSpec((tm, n), lambda i: (i, 0))
    srow = pl.BlockSpec((rs, n), lambda i: (0, 0))
    return pl.pallas_call(
        body, grid=(m // tm,), in_specs=in_specs, out_specs=[row, row, srow, srow],
        out_shape=[jax.ShapeDtypeStruct((m, n), F32), jax.ShapeDtypeStruct((m, n), BF16),
                   jax.ShapeDtypeStruct((rs, n), F32), jax.ShapeDtypeStruct((rs, n), BF16)],
        compiler_params=_cparams("arbitrary"), name=name,
    )(*a_list, *w_list, resid, g.reshape(1, n), b.reshape(1, n), *as_list, resid_s)


def _mlp2_ln(h, w2, resid, g, b, *, tm, tn, name, side):
    m, kf = h.shape
    n = w2.shape[1]
    nj = n // tn
    hs, resid_s = side
    rs = hs.shape[0]
    rsub, rsub_ln = min(tm, ROW_SUB), min(tm, LN_ROW_SUB)

    def body(h_ref, w_ref, r_ref, g_ref, b_ref, hs_ref, rs_ref, o_ref, ob_ref, os_ref, osb_ref):
        j = pl.program_id(1)
        first = pl.program_id(0) == 0
        srows = slice(0, rs)

        def tile(hr, rr, rows):
            return ALPHA * rr[rows, :] + jnp.dot(hr[rows, :], w_ref[...], preferred_element_type=F32)

        def finish(o, ob, y, rows):
            y = _ln_rows(y, g_ref[...], b_ref[...])
            o[rows, :] = y
            ob[rows, :] = y.astype(BF16)

        for jj in range(nj - 1):
            @pl.when(j == jj)
            def _(jj=jj):
                cols = slice(jj * tn, (jj + 1) * tn)
                for r in range(tm // rsub):
                    rows = slice(r * rsub, (r + 1) * rsub)
                    o_ref[rows, cols] = tile(h_ref, r_ref, rows)

                @pl.when(first)
                def _():
                    os_ref[:, cols] = tile(hs_ref, rs_ref, srows)

        @pl.when(j == nj - 1)
        def _():
            done = slice(0, (nj - 1) * tn)
            for r in range(tm // rsub_ln):
                rows = slice(r * rsub_ln, (r + 1) * rsub_ln)
                finish(o_ref, ob_ref, jnp.concatenate([o_ref[rows, done], tile(h_ref, r_ref, rows)], axis=1), rows)

            @pl.when(first)
            def _():
                finish(os_ref, osb_ref, jnp.concatenate([os_ref[:, done], tile(hs_ref, rs_ref, srows)], axis=1), srows)

    assert nj > 1
    row = pl.BlockSpec((tm, n), lambda i, j: (i, 0))
    srow = pl.BlockSpec((rs, n), lambda i, j: (0, 0))
    return pl.pallas_call(
        body, grid=(m // tm, nj),
        in_specs=[pl.BlockSpec((tm, kf), lambda i, j: (i, 0)), pl.BlockSpec((kf, tn), lambda i, j: (0, j)),
                  pl.BlockSpec((tm, tn), lambda i, j: (i, j)),
                  pl.BlockSpec((1, n), lambda i, j: (0, 0)), pl.BlockSpec((1, n), lambda i, j: (0, 0)),
                  pl.BlockSpec((rs, kf), lambda i, j: (0, 0)), pl.BlockSpec((rs, tn), lambda i, j: (0, j))],
        out_specs=[row, row, srow, srow],
        out_shape=[jax.ShapeDtypeStruct((m, n), F32), jax.ShapeDtypeStruct((m, n), BF16),
                   jax.ShapeDtypeStruct((rs, n), F32), jax.ShapeDtypeStruct((rs, n), BF16)],
        compiler_params=_cparams("arbitrary", "arbitrary"), name=name,
    )(h, w2, resid, g.reshape(1, n), b.reshape(1, n), hs, resid_s)


def _compress_rows(kvc, pe, *, rows, name):
    m, c = kvc.shape
    nb = rows // CMP_BLOCK

    def body(x_ref, pe_ref, o_ref):
        x = x_ref[...].reshape(nb, CMP_BLOCK, c) * pe_ref[...][None]
        o_ref[...] = jnp.sum(x, axis=1) * (1.0 / CMP_BLOCK)

    return pl.pallas_call(
        body, grid=(m // rows,),
        in_specs=[pl.BlockSpec((rows, c), lambda i: (i, 0)), pl.BlockSpec((CMP_BLOCK, c), lambda i: (0, 0))],
        out_specs=pl.BlockSpec((nb, c), lambda i: (i, 0)),
        out_shape=jax.ShapeDtypeStruct((m // CMP_BLOCK, c), F32),
        compiler_params=_cparams("parallel"), name=name,
    )(kvc, pe)


def _compress_pages(pages_t, page_table, pe_t, big_t, bsum, layer_base, *, pps, name):
    bd, n_pages = page_table.shape
    bpp = PAGE_SIZE // CMP_BLOCK
    nbl = bpp * pps
    nsteps = n_pages // pps

    def body(pt_ref, *refs):
        page_refs = refs[:pps]
        pe_ref, big_ref, bsum_ref, o_ref = refs[pps:]
        acc = jnp.zeros((KVW, nbl), F32)
        for p in range(pps):
            x = page_refs[p][...] * pe_ref[...]
            acc = acc + jnp.dot(x.astype(BF16), bsum_ref[p], preferred_element_type=F32)
        o_ref[...] = jnp.dot(big_ref[...], acc.astype(BF16), preferred_element_type=F32)

    def page_map(p):
        return lambda b, j, pt: (layer_base + pt[b, j * pps + p], 0, 0)

    cst2 = lambda b, j, pt: (0, 0)
    grid_spec = pltpu.PrefetchScalarGridSpec(
        num_scalar_prefetch=1, grid=(bd, nsteps),
        in_specs=[pl.BlockSpec((None, KVW, PAGE_SIZE), page_map(p)) for p in range(pps)]
        + [pl.BlockSpec((KVW, PAGE_SIZE), cst2), pl.BlockSpec((KVP, KVW), cst2),
           pl.BlockSpec((pps, PAGE_SIZE, nbl), lambda b, j, pt: (0, 0, 0))],
        out_specs=pl.BlockSpec((None, KVP, nbl), lambda b, j, pt: (b, 0, j)))
    return pl.pallas_call(
        body, grid_spec=grid_spec,
        out_shape=jax.ShapeDtypeStruct((bd, KVP, n_pages * bpp), F32),
        compiler_params=_cparams("parallel", "arbitrary"), name=name,
    )(page_table, *([pages_t] * pps), pe_t, big_t, bsum)


def _flash_step_t(q, k, vt, valid, m_ref, l_ref, acc_ref):
    bias = jnp.where(valid, 0.0, NEG)
    s = jnp.dot(k, q, preferred_element_type=F32) + jnp.concatenate([bias] * HPG, axis=1)
    m_prev = m_ref[...]
    m_new = jnp.maximum(m_prev, jnp.max(s, axis=0, keepdims=True))
    alpha = jnp.exp2(m_prev - m_new)
    p = jnp.exp2(s - m_new)
    l_ref[...] = alpha * l_ref[...] + jnp.sum(p, axis=0, keepdims=True)
    acc_ref[...] = alpha * acc_ref[...] + jnp.dot(vt, p.astype(BF16), preferred_element_type=F32)
    m_ref[...] = m_new


def _attn_body(*refs, tq, tk, seq, k_top, ngrp):
    n_in = 9
    ins = [refs[g * n_in:(g + 1) * n_in] for g in range(ngrp)]
    o_ref = refs[ngrp * n_in]
    m_ref, l_ref, acc_ref, sel_ref, vts_ref, vtw_ref = refs[ngrp * n_in + 1:]
    i = pl.program_id(2)
    cols = HPG * tq
    nsb = seq // SEL_BLOCK
    nt = seq // tk
    bpt = tk // SEL_BLOCK
    q0 = i * tq
    lane_q = lax.broadcasted_iota(jnp.int32, (1, cols), 1)
    qpos = q0 + (lane_q & (tq - 1))
    qp1 = q0 + lax.broadcasted_iota(jnp.int32, (1, tq), 1)

    @pl.when(i == 0)
    def _():
        def tr(t, c):
            for g in range(ngrp):
                vs_ref, vw_ref = ins[g][6], ins[g][8]
                for h in range(tk // LANES):
                    k0 = pl.multiple_of(t * tk + h * LANES, LANES)
                    hs = slice(h * LANES, (h + 1) * LANES)
                    vts_ref[g, t, :, hs] = vs_ref[pl.ds(k0, LANES), :].astype(F32).T.astype(BF16)
                    vtw_ref[g, t, :, hs] = vw_ref[pl.ds(k0, LANES), :].astype(F32).T.astype(BF16)
            return c
        lax.fori_loop(0, nt, tr, 0)

    def heads_t(ref, scale=1.0):
        parts = [ref[:, j * HSLOT:(j + 1) * HSLOT].astype(F32).T * scale for j in range(HPG)]
        return jnp.concatenate(parts, axis=1).astype(BF16)

    r = lax.broadcasted_iota(jnp.int32, (2 * nsb, 1), 0)
    n_of = jnp.where(r < nsb, 2 * r, 2 * (r - nsb) + 1)
    mk = ((n_of + 1) * CMP_BLOCK - 1) <= qpos
    sb = lax.broadcasted_iota(jnp.int32, (nsb, 1), 0)
    vis = (sb * SEL_BLOCK) <= qp1
    cur = sb == (qp1 >> (SEL_BLOCK.bit_length() - 1))

    def compressed_branch(g):
        qraw_ref, ck_ref, cv_ref = ins[g][0], ins[g][3], ins[g][4]
        qr = heads_t(qraw_ref)
        ck = jnp.concatenate([ck_ref[pl.ds(0, nsb, stride=2), :], ck_ref[pl.ds(1, nsb, stride=2), :]], axis=0)
        s = jnp.dot(ck.astype(BF16), qr, preferred_element_type=F32) * SCALE
        s = jnp.where(mk, s, NEG)
        mx = jnp.max(s, axis=0, keepdims=True)
        p = jnp.where(mk, jnp.exp(s - mx), 0.0)
        pn = p / jnp.maximum(jnp.sum(p, axis=0, keepdims=True), 1e-30)
        cv = jnp.concatenate([cv_ref[pl.ds(0, nsb, stride=2), :], cv_ref[pl.ds(1, nsb, stride=2), :]], axis=0)
        cvt = _pad_rows_to(cv, LANES).T.astype(BF16)
        o_cmp = jnp.dot(cvt, _pad_rows_to(pn, LANES).astype(BF16), preferred_element_type=F32)
        pp = pn[0:nsb] + pn[nsb:2 * nsb]
        imp = pp[:, 0:tq]
        for j in range(1, HPG):
            imp = imp + pp[:, j * tq:(j + 1) * tq]
        imp = jnp.where(cur, FORCE, jnp.where(vis, imp, -FORCE))
        cnt = jnp.zeros((nsb, tq), F32)
        for j in range(nsb):
            rowj = imp[j:j + 1, :]
            beats = (rowj > imp) | ((rowj == imp) & (j < sb))
            cnt = cnt + jnp.where(beats, 1.0, 0.0)
        sel = jnp.where(cnt < k_top, 1.0, 0.0)
        for t in range(nt):
            sel_ref[g, t, 0:bpt, :] = sel[bpt * t:bpt * (t + 1), :]
        return o_cmp

    o_cmp = [compressed_branch(g) for g in range(ngrp)]
    qt = [heads_t(ins[g][1], SCALE * LOG2E) for g in range(ngrp)]
    rowk = lax.broadcasted_iota(jnp.int32, (tk, 1), 0)

    def reset():
        m_ref[...] = jnp.full((ngrp, 1, cols), NEG, F32)
        l_ref[...] = jnp.zeros((ngrp, 1, cols), F32)
        acc_ref[...] = jnp.zeros((ngrp, HSLOT, cols), F32)

    def result(g):
        return acc_ref[g] / jnp.maximum(l_ref[g], 1e-30)

    reset()

    def sel_step(t, c):
        k0 = pl.multiple_of(t * tk, tk)
        kp = k0 + rowk
        for g in range(ngrp):
            sm = sel_ref[g, t, bpt - 1:bpt, :]
            for j in range(bpt - 2, -1, -1):
                sm = jnp.where(rowk < (j + 1) * SEL_BLOCK, sel_ref[g, t, j:j + 1, :], sm)
            valid = (kp <= qp1) & (sm > 0.5)
            _flash_step_t(qt[g], ins[g][5][pl.ds(k0, tk), :], vts_ref[g, t], valid,
                          m_ref.at[g], l_ref.at[g], acc_ref.at[g])
        return c

    t_end = lax.div(q0 + tq + tk - 1, tk)
    lax.fori_loop(0, t_end, sel_step, 0)
    o_sel = [result(g) for g in range(ngrp)]

    reset()

    def win_step(t, c):
        k0 = pl.multiple_of(t * tk, tk)
        kp = k0 + rowk
        valid = (kp <= qp1) & (kp >= qp1 - WINDOW)
        for g in range(ngrp):
            _flash_step_t(qt[g], ins[g][7][pl.ds(k0, tk), :], vtw_ref[g, t], valid,
                          m_ref.at[g], l_ref.at[g], acc_ref.at[g])
        return c

    lax.fori_loop(lax.div(jnp.maximum(q0 - WINDOW, 0), tk), t_end, win_step, 0)

    gw = HPG * HSLOT
    for g in range(ngrp):
        gt = ins[g][2][...].T
        o = jnp.zeros((HSLOT, cols), F32)
        for br, o_br in enumerate((o_cmp[g], o_sel[g], result(g))):
            grow = jnp.concatenate([gt[br * HPG + j:br * HPG + j + 1, :] for j in range(HPG)], axis=1)
            o = o + grow * o_br
        for j in range(HPG):
            o_ref[:, g * gw + j * HSLOT:g * gw + (j + 1) * HSLOT] = o[:, j * tq:(j + 1) * tq].T.astype(BF16)


def _prompt_attention(qraw, qrot, gates, ckv, kvs_b, kvw_b, *, batch, seq, tq, tk, ngrp):
    nq = seq // tq
    ncb = seq // CMP_BLOCK
    nt = seq // tk
    cols = HPG * tq
    k_top = min(TOP_K, seq // SEL_BLOCK)
    assert tq & (tq - 1) == 0 and seq % tk == 0 and tk % LANES == 0 and tk // SEL_BLOCK <= SUBLANES
    assert KV_HEADS % ngrp == 0
    body = functools.partial(_attn_body, tq=tq, tk=tk, seq=seq, k_top=k_top, ngrp=ngrp)
    gw = HPG * HSLOT
    in_specs, operands = [], []
    for g in range(ngrp):
        qmap = functools.partial(lambda b, p, i, g: (b * nq + i, p * ngrp + g), g=g)
        kmap = functools.partial(lambda b, p, i, g: (b, p * ngrp + g), g=g)
        vmap_ = functools.partial(lambda b, p, i, g: (b, KV_HEADS + p * ngrp + g), g=g)
        in_specs += [pl.BlockSpec((tq, gw), qmap), pl.BlockSpec((tq, gw), qmap), pl.BlockSpec((tq, LANES), qmap),
                     pl.BlockSpec((ncb, HSLOT), kmap), pl.BlockSpec((ncb, HSLOT), vmap_),
                     pl.BlockSpec((seq, HSLOT), kmap), pl.BlockSpec((seq, HSLOT), vmap_),
                     pl.BlockSpec((seq, HSLOT), kmap), pl.BlockSpec((seq, HSLOT), vmap_)]
        operands += [qraw, qrot, gates, ckv, ckv, kvs_b, kvs_b, kvw_b, kvw_b]
    return pl.pallas_call(
        body, grid=(batch, KV_HEADS // ngrp, nq), in_specs=in_specs,
        out_specs=pl.BlockSpec((tq, ngrp * gw), lambda b, p, i: (b * nq + i, p)),
        out_shape=jax.ShapeDtypeStruct((batch * seq, QW), BF16),
        scratch_shapes=[pltpu.VMEM((ngrp, 1, cols), F32), pltpu.VMEM((ngrp, 1, cols), F32),
                        pltpu.VMEM((ngrp, HSLOT, cols), F32), pltpu.VMEM((ngrp, nt, SUBLANES, tq), F32),
                        pltpu.VMEM((ngrp, nt, HSLOT, tk), BF16), pltpu.VMEM((ngrp, nt, HSLOT, tk), BF16)],
        compiler_params=_cparams("parallel", "parallel", "arbitrary"), name="prompt_attention",
    )(*operands)


CONV_HALO = 32


def _conv_body(cur_ref, prev_ref, w_ref, b_ref, g_ref, bn_ref, o_ref, ext_ref, *, ts):
    c = pl.program_id(1)
    ext_ref[0:CONV_HALO, :] = jnp.where(c > 0, prev_ref[...], 0.0)
    ext_ref[CONV_HALO:CONV_HALO + ts, :] = cur_ref[...]
    acc = jnp.zeros((ts, CONV_CH), F32) + b_ref[...]
    off = CONV_HALO - (CONV_W - 1)
    for k in range(CONV_W):
        acc = acc + ext_ref[pl.ds(off + k, ts), :] * w_ref[k:k + 1, :]
    y = _ln_rows(acc, g_ref[...], bn_ref[...])
    o_ref[...] = (y * jax.nn.sigmoid(y)).astype(BF16)


def _prompt_conv(u, cw, cb, cg, cbn, *, batch, seq, ts):
    nt = seq // ts
    r = ts // CONV_HALO
    cwp = jnp.pad(cw, ((0, CONV_HALO - CONV_W), (0, 0)))
    vec = lambda a: a.reshape(1, CONV_CH)
    cst = lambda b, c: (0, 0)
    return pl.pallas_call(
        functools.partial(_conv_body, ts=ts), grid=(batch, nt),
        in_specs=[pl.BlockSpec((ts, CONV_CH), lambda b, c: (b * nt + c, 0)),
                  pl.BlockSpec((CONV_HALO, CONV_CH), lambda b, c: (jnp.maximum((b * nt + c) * r - 1, 0), 0)),
                  pl.BlockSpec((CONV_HALO, CONV_CH), cst),
                  pl.BlockSpec((1, CONV_CH), cst), pl.BlockSpec((1, CONV_CH), cst), pl.BlockSpec((1, CONV_CH), cst)],
        out_specs=pl.BlockSpec((ts, CONV_CH), lambda b, c: (b * nt + c, 0)),
        out_shape=jax.ShapeDtypeStruct((batch * seq, CONV_CH), BF16),
        scratch_shapes=[pltpu.VMEM((CONV_HALO + ts, CONV_CH), F32)],
        compiler_params=_cparams("parallel", "arbitrary"), name="prompt_conv",
    )(u, u, cwp, vec(cb), vec(cg), vec(cbn))


POOL_HALO = 16


def _odd_mix_body(pin_ref, prev_ref, u_ref, vn_ref, pw_ref, ps_ref, sw_ref, sb_ref, o_ref, ext_ref):
    c = pl.program_id(1)
    ext_ref[0:POOL_HALO, :] = jnp.where(c > 0, prev_ref[...], 0.0)
    ext_ref[POOL_HALO:POOL_HALO + CHUNK, :] = pin_ref[...]
    t = c * CHUNK + lax.broadcasted_iota(jnp.int32, (CHUNK, 1), 0)
    for g, w in enumerate(POOL_WINDOWS):
        sl = slice(g * POOL_GC, (g + 1) * POOL_GC)
        tot = ext_ref[pl.ds(POOL_HALO, CHUNK), sl]
        for j in range(1, w):
            tot = tot + ext_ref[pl.ds(POOL_HALO - j, CHUNK), sl]
        cnt = jnp.minimum(w, t + 1).astype(F32)
        d = tot / cnt - pin_ref[:, sl]
        y = jnp.dot(d.astype(BF16), pw_ref[g], preferred_element_type=F32)
        o_ref[:, sl] = (y * ps_ref[:, sl]).astype(BF16)
    ri = lax.broadcasted_iota(jnp.int32, (CHUNK, CHUNK), 0)
    ci = lax.broadcasted_iota(jnp.int32, (CHUNK, CHUNK), 1)
    for g in range(SGU_GROUPS):
        sl = slice(g * SGU_GC, (g + 1) * SGU_GC)
        ws = jnp.where(ci <= ri, sw_ref[g], 0.0).astype(BF16)
        mixed = jnp.dot(ws, vn_ref[:, sl].astype(BF16), preferred_element_type=F32) + sb_ref[:, g:g + 1]
        o_ref[:, POOL_CH + g * SGU_GC:POOL_CH + (g + 1) * SGU_GC] = (u_ref[:, sl] * mixed).astype(BF16)


def _prompt_odd_mix(pin, u, vn, pool_w, pool_scale, sgu_w, sgu_b, *, batch, seq):
    nt = seq // CHUNK
    r = CHUNK // POOL_HALO
    cst2 = lambda b, c: (0, 0)
    cst3 = lambda b, c: (0, 0, 0)
    row = lambda b, c: (b * nt + c, 0)
    return pl.pallas_call(
        _odd_mix_body, grid=(batch, nt),
        in_specs=[pl.BlockSpec((CHUNK, POOL_CH), row),
                  pl.BlockSpec((POOL_HALO, POOL_CH), lambda b, c: (jnp.maximum((b * nt + c) * r - 1, 0), 0)),
                  pl.BlockSpec((CHUNK, SGU_CH), row), pl.BlockSpec((CHUNK, SGU_CH), row),
                  pl.BlockSpec((POOL_GROUPS, POOL_GC, POOL_GC), cst3), pl.BlockSpec((1, POOL_CH), cst2),
                  pl.BlockSpec((SGU_GROUPS, CHUNK, CHUNK), cst3), pl.BlockSpec((CHUNK, SGU_GROUPS), cst2)],
        out_specs=pl.BlockSpec((CHUNK, D_MODEL), row),
        out_shape=jax.ShapeDtypeStruct((batch * seq, D_MODEL), BF16),
        scratch_shapes=[pltpu.VMEM((POOL_HALO + CHUNK, POOL_CH), F32)],
        compiler_params=_cparams("parallel", "arbitrary"), name="prompt_pool_sgu",
    )(pin, pin, u, vn, pool_w.astype(BF16), pool_scale.reshape(1, POOL_CH), sgu_w, sgu_b.T)


def _group_rows(nrows):
    return lax.broadcasted_iota(jnp.int32, (nrows, 1), 0) >> (HPG.bit_length() - 1)


def _sample_cmp_body(q_ref, ckv_ref, o_ref, idx_ref, *, qpos, ncb, nbl, k_past):
    q = q_ref[...].astype(BF16)
    rg = _group_rows(N_HEADS)
    half = nbl // 2
    assert nbl & (nbl - 1) == 0
    sh = nbl.bit_length() - 1
    lane = lax.broadcasted_iota(jnp.int32, (1, ncb), 1)
    grp, w = lane >> sh, lane & (nbl - 1)
    n_cmp = grp * nbl + 2 * (w & (half - 1)) + (w >> (sh - 1))
    mk = jnp.broadcast_to(((n_cmp + 1) * CMP_BLOCK - 1) <= qpos, (N_HEADS, ncb))
    s = jnp.zeros((N_HEADS, ncb), F32)
    for g in range(KV_HEADS):
        ck = ckv_ref[g * HSLOT:(g + 1) * HSLOT, :].astype(BF16)
        s = jnp.where(rg == g, jnp.dot(q, ck, preferred_element_type=F32) * SCALE, s)
    s = jnp.where(mk, s, NEG)
    mx = jnp.max(s, axis=-1, keepdims=True)
    p = jnp.where(mk, jnp.exp(s - mx), 0.0)
    pn = p / jnp.maximum(jnp.sum(p, axis=-1, keepdims=True), 1e-30)
    o = jnp.zeros((N_HEADS, HSLOT), F32)
    for g in range(KV_HEADS):
        cv = ckv_ref[(KV_HEADS + g) * HSLOT:(KV_HEADS + g + 1) * HSLOT, :].astype(BF16)
        o = jnp.where(rg == g, lax.dot_general(pn.astype(BF16), cv, _NT, preferred_element_type=F32), o)
    o_ref[...] = o
    pair = pn + pltpu.roll(pn, ncb - half, axis=1)
    valid = w < half
    sb = grp * half + w
    vis = (sb * SEL_BLOCK) <= qpos
    ri = lax.broadcasted_iota(jnp.int32, (ncb, ncb), 0)
    sb_r = (ri >> sh) * half + (ri & (nbl - 1))
    sb_c = jnp.broadcast_to(sb, (ncb, ncb))
    slot = lax.broadcasted_iota(jnp.int32, (TOP_K, 1), 0)
    for g in range(KV_HEADS):
        imp = jnp.sum(jnp.where(rg == g, pair, 0.0), axis=0, keepdims=True)
        imp = jnp.where(valid, jnp.where(vis, imp, -FORCE), -2.0 * FORCE)
        a = jnp.broadcast_to(imp, (ncb, ncb))
        bt = a.T
        beats = (bt > a) | ((bt == a) & (sb_r < sb_c))
        rank = jnp.sum(jnp.where(beats, 1.0, 0.0), axis=0, keepdims=True)
        onehot = jnp.where((rank == slot.astype(F32)) & valid, 1.0, 0.0)
        idx = jnp.sum(onehot * sb.astype(F32), axis=-1, keepdims=True)
        idx = jnp.where(slot < k_past, idx, 0.0)
        idx_ref[g * TOP_K:(g + 1) * TOP_K, :] = jnp.broadcast_to(idx, (TOP_K, LANES)).astype(jnp.int32)


def _sample_cmp(q3, ckv_t, *, bd, past, qpos, nbl):
    ncb = past // CMP_BLOCK
    k_past = min(TOP_K - 1, past // SEL_BLOCK)
    body = functools.partial(_sample_cmp_body, qpos=qpos, ncb=ncb, nbl=nbl, k_past=k_past)
    head3 = pl.BlockSpec((None, N_HEADS, HSLOT), lambda b: (b, 0, 0))
    return pl.pallas_call(
        body, grid=(bd,),
        in_specs=[head3, pl.BlockSpec((None, KVP, ncb), lambda b: (b, 0, 0))],
        out_specs=[head3, pl.BlockSpec((None, KV_HEADS * TOP_K, LANES), lambda b: (b, 0, 0))],
        out_shape=[jax.ShapeDtypeStruct((bd, N_HEADS, HSLOT), F32),
                   jax.ShapeDtypeStruct((bd, KV_HEADS * TOP_K, LANES), jnp.int32)],
        compiler_params=_cparams("parallel"), name="sample_cmp_attention",
    )(q3, ckv_t)


Q_PAD_ROWS = 8


def _pad_dt(x):
    return _pad_rows_to(x, HSLOT)


def _sample_sel_body(pt_ref, idx_ref, *refs, k_past):
    k_refs, v_refs = refs[:k_past], refs[k_past:2 * k_past]
    q_ref, knew_ref, vnew_ref, o_ref = refs[2 * k_past:]
    b, g = pl.program_id(0), pl.program_id(1)
    bpp = PAGE_SIZE // SEL_BLOCK
    q = q_ref[...].astype(BF16)
    half_of_lane = lax.broadcasted_iota(jnp.int32, (1, PAGE_SIZE), 1) // SEL_BLOCK
    s_parts, m_parts = [], []
    for s in range(k_past):
        kt = _pad_dt(k_refs[s][...]).astype(BF16)
        s_parts.append(jnp.dot(q, kt, preferred_element_type=F32) * SCALE)
        m_parts.append(jnp.broadcast_to(half_of_lane == idx_ref[b, g, s] % bpp, (Q_PAD_ROWS, PAGE_SIZE)))
    s_old = jnp.concatenate(s_parts, axis=1)
    mk = jnp.concatenate(m_parts, axis=1)
    s_old = jnp.where(mk, s_old, NEG)
    s_all = lax.dot_general(q, knew_ref[...].astype(BF16), _NT, preferred_element_type=F32) * SCALE
    lane = lax.broadcasted_iota(jnp.int32, s_all.shape, 1)
    s_new = jnp.sum(jnp.where(lane == b, s_all, 0.0), axis=-1, keepdims=True)
    mx = jnp.maximum(jnp.max(s_old, axis=-1, keepdims=True), s_new)
    p_old = jnp.where(mk, jnp.exp(s_old - mx), 0.0)
    p_new = jnp.exp(s_new - mx)
    den = jnp.maximum(jnp.sum(p_old, axis=-1, keepdims=True) + p_new, 1e-30)
    v_new = vnew_ref[pl.ds(b, 1), :].astype(BF16).astype(F32)
    o = p_new.astype(BF16).astype(F32) * v_new
    for s in range(k_past):
        vt = _pad_dt(v_refs[s][...]).astype(BF16)
        ps = p_old[:, s * PAGE_SIZE:(s + 1) * PAGE_SIZE].astype(BF16)
        o = o + lax.dot_general(ps, vt, _NT, preferred_element_type=F32)
    o_ref[...] = o / den


def _sample_sel(pages5, page_table, idx, layer_base, q4, kvs_new, *, bd, past):
    k_past = min(TOP_K - 1, past // SEL_BLOCK)
    bpp = PAGE_SIZE // SEL_BLOCK

    def blk_map(s, kv):
        def f(b, g, pt, ix):
            return (layer_base + pt[b, ix[b, g, s] // bpp], kv, g, 0, 0)
        return f

    tile = lambda s, kv: pl.BlockSpec((None, None, None, HEAD_DIM, PAGE_SIZE), blk_map(s, kv))
    grid_spec = pltpu.PrefetchScalarGridSpec(
        num_scalar_prefetch=2, grid=(bd, KV_HEADS),
        in_specs=[tile(s, 0) for s in range(k_past)] + [tile(s, 1) for s in range(k_past)]
        + [pl.BlockSpec((None, None, Q_PAD_ROWS, HSLOT), lambda b, g, pt, ix: (b, g, 0, 0)),
           pl.BlockSpec((SAMPLE_ROWS, HSLOT), lambda b, g, pt, ix: (0, g)),
           pl.BlockSpec((SAMPLE_ROWS, HSLOT), lambda b, g, pt, ix: (0, KV_HEADS + g))],
        out_specs=pl.BlockSpec((None, None, Q_PAD_ROWS, HSLOT), lambda b, g, pt, ix: (b, g, 0, 0)))
    return pl.pallas_call(
        functools.partial(_sample_sel_body, k_past=k_past), grid_spec=grid_spec,
        out_shape=jax.ShapeDtypeStruct((bd, KV_HEADS, Q_PAD_ROWS, HSLOT), F32),
        compiler_params=_cparams("parallel", "arbitrary"), name="sample_sel_attention",
    )(page_table, idx, *([pages5] * (2 * k_past)), q4, kvs_new, kvs_new)


def _sample_win_body(q_ref, win_ref, new_ref, ocmp_ref, osel_ref, gate_ref, o_ref, *, qpos, past, wb):
    b = pl.program_id(0)
    q = q_ref[...].astype(BF16)
    qf = q.astype(F32)
    rg = _group_rows(N_HEADS)
    new = new_ref[pl.ds(b, 1), :].astype(BF16).astype(F32)
    kpos = (past - wb) + lax.broadcasted_iota(jnp.int32, (1, wb), 1)
    mk = jnp.broadcast_to((kpos <= qpos) & (kpos >= qpos - WINDOW), (N_HEADS, wb))
    s_old = jnp.zeros((N_HEADS, wb), F32)
    s_new = jnp.zeros((N_HEADS, 1), F32)
    for g in range(KV_HEADS):
        kt = _pad_dt(win_ref[g]).astype(BF16)
        s_old = jnp.where(rg == g, jnp.dot(q, kt, preferred_element_type=F32) * SCALE, s_old)
        sn = jnp.sum(qf * new[:, g * HSLOT:(g + 1) * HSLOT], axis=-1, keepdims=True) * SCALE
        s_new = jnp.where(rg == g, sn, s_new)
    s_old = jnp.where(mk, s_old, NEG)
    mx = jnp.maximum(jnp.max(s_old, axis=-1, keepdims=True), s_new)
    p_old = jnp.where(mk, jnp.exp(s_old - mx), 0.0)
    p_new = jnp.exp(s_new - mx)
    den = jnp.maximum(jnp.sum(p_old, axis=-1, keepdims=True) + p_new, 1e-30)
    o_win = jnp.zeros((N_HEADS, HSLOT), F32)
    for g in range(KV_HEADS):
        vt = _pad_dt(win_ref[KV_HEADS + g]).astype(BF16)
        og = lax.dot_general(p_old.astype(BF16), vt, _NT, preferred_element_type=F32)
        og = og + p_new.astype(BF16).astype(F32) * new[:, (KV_HEADS + g) * HSLOT:(KV_HEADS + g + 1) * HSLOT]
        o_win = jnp.where(rg == g, og, o_win)
    o_win = o_win / den
    gts = gate_ref[...]
    o_ref[...] = gts[:, 0:1] * ocmp_ref[...] + gts[:, 1:2] * osel_ref[...] + gts[:, 2:3] * o_win


def _sample_win(q3, win4, layer, kvw_new, o_cmp, o_sel, gates3, *, bd, past, qpos):
    wb = win4.shape[-1]
    head3 = pl.BlockSpec((None, N_HEADS, HSLOT), lambda b: (b, 0, 0))
    return pl.pallas_call(
        functools.partial(_sample_win_body, qpos=qpos, past=past, wb=wb), grid=(bd,),
        in_specs=[head3, pl.BlockSpec((None, 2 * KV_HEADS, HEAD_DIM, wb), lambda b: (layer * bd + b, 0, 0, 0)),
                  pl.BlockSpec((SAMPLE_ROWS, KVP), lambda b: (0, 0)), head3, head3, head3],
        out_specs=head3,
        out_shape=jax.ShapeDtypeStruct((bd, N_HEADS, HSLOT), F32),
        compiler_params=_cparams("parallel"), name="sample_win_attention",
    )(q3, win4, kvw_new, o_cmp, o_sel, gates3)


def _sample_conv_body(st_ref, u_ref, w_ref, b_ref, g_ref, bn_ref, o_ref, *, bd):
    w = w_ref[...]
    y = jnp.sum(st_ref[...] * w[None, :CONV_W - 1, :], axis=1) + u_ref[0:bd, :] * w[CONV_W - 1:CONV_W, :] + b_ref[...]
    y = _ln_rows(y, g_ref[...], bn_ref[...])
    o_ref[...] = y * jax.nn.sigmoid(y)


def _sample_conv(state, layer, u, cw, cb, cg, cbn, *, bd):
    vec = lambda a: a.reshape(1, CONV_CH)
    cst = lambda i: (0, 0)
    return pl.pallas_call(
        functools.partial(_sample_conv_body, bd=bd), grid=(1,),
        in_specs=[pl.BlockSpec((None, bd, CONV_W - 1, CONV_CH), lambda i: (layer, 0, 0, 0)),
                  pl.BlockSpec((SAMPLE_ROWS, CONV_CH), cst), pl.BlockSpec((CONV_W, CONV_CH), cst),
                  pl.BlockSpec((1, CONV_CH), cst), pl.BlockSpec((1, CONV_CH), cst), pl.BlockSpec((1, CONV_CH), cst)],
        out_specs=pl.BlockSpec((bd, CONV_CH), cst),
        out_shape=jax.ShapeDtypeStruct((bd, CONV_CH), F32),
        compiler_params=_cparams("arbitrary"), name="sample_conv",
    )(state, u, cw, vec(cb), vec(cg), vec(cbn))


def _sample_odd_body(st_ref, pin_ref, u_ref, vn_ref, pw_ref, ps_ref, w0_ref, b0_ref, o_ref, *, bd, start_pos):
    pin = pin_ref[0:bd, :]
    st = st_ref[...]
    for g, w in enumerate(POOL_WINDOWS):
        sl = slice(g * POOL_GC, (g + 1) * POOL_GC)
        tot = pin[:, sl] + jnp.sum(st[:, POOL_STATE - (w - 1):, sl], axis=1)
        d = tot / float(min(w, start_pos + 1)) - pin[:, sl]
        dp = jnp.concatenate([d, jnp.zeros((SAMPLE_ROWS - bd, POOL_GC), F32)], axis=0).astype(BF16)
        y = jnp.dot(dp, pw_ref[g], preferred_element_type=F32)[0:bd]
        o_ref[:, sl] = y * ps_ref[:, sl]
    mixed = w0_ref[...] * vn_ref[0:bd, :] + b0_ref[...]
    o_ref[:, POOL_CH:] = u_ref[0:bd, :] * mixed


def _sample_odd_mix(state, layer, pin, u, vn, pool_w, pool_scale, sgu_w, sgu_b, *, bd, start_pos):
    w0 = jnp.repeat(sgu_w[:, 0, 0], SGU_GC).reshape(1, SGU_CH)
    b0 = jnp.repeat(sgu_b[:, 0], SGU_GC).reshape(1, SGU_CH)
    cst = lambda i: (0, 0)
    return pl.pallas_call(
        functools.partial(_sample_odd_body, bd=bd, start_pos=start_pos), grid=(1,),
        in_specs=[pl.BlockSpec((None, bd, POOL_STATE, POOL_CH), lambda i: (layer, 0, 0, 0)),
                  pl.BlockSpec((SAMPLE_ROWS, POOL_CH), cst), pl.BlockSpec((SAMPLE_ROWS, SGU_CH), cst),
                  pl.BlockSpec((SAMPLE_ROWS, SGU_CH), cst),
                  pl.BlockSpec((POOL_GROUPS, POOL_GC, POOL_GC), lambda i: (0, 0, 0)),
                  pl.BlockSpec((1, POOL_CH), cst), pl.BlockSpec((1, SGU_CH), cst), pl.BlockSpec((1, SGU_CH), cst)],
        out_specs=pl.BlockSpec((bd, D_MODEL), cst),
        out_shape=jax.ShapeDtypeStruct((bd, D_MODEL), F32),
        compiler_params=_cparams("arbitrary"), name="sample_pool_sgu",
    )(state, pin, u, vn, pool_w.astype(BF16), pool_scale.reshape(1, POOL_CH), w0, b0)


def _pad_rows(x, rows):
    return jnp.pad(x, ((0, rows - x.shape[0]), (0, 0)))


def _split_to_nat(x, nheads):
    y = _unpad_heads(x, nheads)
    return _pad_nat(y.reshape(y.shape[:-1] + (nheads, HEAD_DIM))).reshape(x.shape)


def _nat_to_split(x, nheads):
    xh = x.reshape(x.shape[:-1] + (nheads, HSLOT))[..., :HEAD_DIM]
    return _pad_head(xh).reshape(x.shape)


def _even_in_proj(xb, xsb, w_in, rope_p, rope_s, *, tm, bs, stack):
    nrep = rope_p[0].shape[0] // tm
    rs = xsb.shape[0]
    tab = lambda i, j: (i % nrep, 0)
    rope_ex = tuple((t, (tm, LANES), tab) for t in rope_p)
    rope_sx = tuple((t, (rs, LANES), lambda i, j: (0, 0)) for t in rope_s)
    gw = HPG * HSLOT
    qraw, qrot, qraw_s, qrot_s = _mm(
        xb, w_in, tm=tm, tn=gw, n_off=E_Q, n_cols=QW, epilogue=_ep_q, extras=rope_ex,
        outs=((QW, BF16, gw), (QW, BF16, gw)), name="even_in_q",
        side=(xsb, _ep_q, rope_sx, ((QW, BF16, gw), (QW, BF16, gw))))
    nst = bs[1] // tm
    e, n_even, t_prev = stack
    row = {"f32": (KVP, F32, KVP), "bf16": (KVP, BF16, KVP),
           "t": ((n_even, bs[0], KVW, bs[1]), F32, (None, None, KVW, tm), lambda i, j: (e, i // nst, 0, i % nst))}
    kv_p, kv_s = [], []
    for sec, (off, want) in enumerate(zip((E_KVC, E_KVS, E_KVW), (("f32", "t"), ("bf16", "t"), ("bf16", "t")))):
        res = _mm(xb, w_in, tm=tm, tn=KVP, n_off=off, n_cols=KVP,
                  epilogue=functools.partial(_ep_kv, rope=sec > 0, want=want),
                  extras=rope_ex if sec > 0 else (), outs=tuple(row[k] for k in want),
                  name=("even_in_kvc", "even_in_kvs", "even_in_kvw")[sec],
                  side=(xsb, functools.partial(_ep_kv, rope=sec > 0, want=("f32",)),
                        rope_sx if sec > 0 else (), (row["f32"],)),
                  alias=None if t_prev is None else (t_prev[sec], 1))
        kv_p.append(res[:2])
        kv_s.append(res[2])
    u, u_s = _mm(xb, w_in, tm=tm, tn=CONV_CH, n_off=E_GLU, n_cols=2 * CONV_CH, epilogue=_ep_glu,
                 outs=((CONV_CH, F32, CONV_CH // 2),), name="even_in_glu",
                 side=(xsb, _ep_glu, (), ((CONV_CH, F32, CONV_CH // 2),)))
    gates, gates_s = _mm(xb, w_in, tm=tm, tn=GATE_W, n_off=E_GATE, n_cols=GATE_W, epilogue=_ep_sigmoid,
                         outs=((GATE_W, F32, GATE_W),), name="even_in_gates",
                         side=(xsb, _ep_sigmoid, (), ((GATE_W, F32, GATE_W),)))
    return (qraw, qrot, kv_p, u, gates), (qraw_s, qrot_s, kv_s, u_s, gates_s)


def _mlp_up_cast(xb, xsb, w1, w2, layer, *, tm, tn):
    m, k = xb.shape
    rs = xsb.shape[0]
    dff = w1.shape[2]
    d_out = w2.shape[2]
    ni, nj = m // tm, dff // tn
    slab = dff // (ni * nj)
    assert m % tm == 0 and dff % tn == 0 and dff % (ni * nj) == 0 and slab % SAMPLE_ROWS == 0
    rsub = min(tm, ROW_SUB)

    def body(x_ref, xs_ref, w1_ref, w2_ref, h_ref, hs_ref, w2b_ref, w1b_ref):
        def act(rows_ref, rows):
            a = jnp.maximum(jnp.dot(rows_ref[rows, :], w1b_ref[...], preferred_element_type=F32), 0.0)
            return (a * a).astype(BF16)

        @pl.when(pl.program_id(1) == 0)
        def _():
            w1b_ref[...] = w1_ref[...].astype(BF16)
            hs_ref[...] = act(xs_ref, slice(0, rs))

        w2b_ref[...] = w2_ref[...].astype(BF16)
        for r in range(tm // rsub):
            rows = slice(r * rsub, (r + 1) * rsub)
            h_ref[rows, :] = act(x_ref, rows)

    return pl.pallas_call(
        body, grid=(nj, ni),
        in_specs=[pl.BlockSpec((tm, k), lambda j, i: (i, 0)), pl.BlockSpec((rs, k), lambda j, i: (0, 0)),
                  pl.BlockSpec((None, k, tn), lambda j, i: (layer, 0, j)),
                  pl.BlockSpec((None, slab, d_out), lambda j, i: (layer, j * ni + i, 0))],
        out_specs=[pl.BlockSpec((tm, tn), lambda j, i: (i, j)), pl.BlockSpec((rs, tn), lambda j, i: (0, j)),
                   pl.BlockSpec((slab, d_out), lambda j, i: (j * ni + i, 0))],
        out_shape=[jax.ShapeDtypeStruct((m, dff), BF16), jax.ShapeDtypeStruct((rs, dff), BF16),
                   jax.ShapeDtypeStruct((dff, d_out), BF16)],
        scratch_shapes=[pltpu.VMEM((k, tn), BF16)],
        compiler_params=_cparams("arbitrary", "arbitrary"), name="mlp_up_cast",
    )(xb, xsb, w1, w2)


def kernel(x_prompt, x_sample, cache_cmp_kv, cache_sel_kv, cache_win_kv, state_conv, state_pool, page_table,
           w_in_even, w_out_even, cmp_pe_k, cmp_pe_v, cmp_w_k, cmp_w_v, conv_w, conv_b, conv_ln_g, conv_ln_b,
           w_in_odd, w_out_odd, pool_w, pool_scale, sgu_ln_g, sgu_ln_b, sgu_w, sgu_b,
           mlp_w1, mlp_w2, ln_mix_g, ln_mix_b, ln_ffn_g, ln_ffn_b):
    B, S, D = x_prompt.shape
    Bd, Sd, _ = x_sample.shape
    n_pages = page_table.shape[1]
    past = n_pages * PAGE_SIZE
    n_even, n_pool = cache_cmp_kv.shape[:2]
    wb = cache_win_kv.shape[2]
    assert D == D_MODEL and Sd == 1 and Bd <= SAMPLE_ROWS
    assert S % 1024 == 0 and past % SEL_BLOCK == 0 and S >= WINDOW
    M = B * S
    Ms = SAMPLE_ROWS
    tm_p = 1024

    rope_p = _rope_tables(jnp.arange(S, dtype=jnp.int32))
    rope_s = _rope_tables(jnp.full((Ms,), past, jnp.int32))
    pps = min(32, n_pages)
    assert n_pages % pps == 0
    bsum = _block_sum_matrices(pps)
    cmp_t = cache_cmp_kv.transpose(0, 1, 3, 4, 5, 2).reshape(n_even * n_pool, KVW, PAGE_SIZE)
    sel_t = cache_sel_kv.transpose(0, 1, 3, 4, 5, 2).reshape(n_even * n_pool, 2, KV_HEADS, HEAD_DIM, PAGE_SIZE)
    win_t = cache_win_kv.transpose(0, 1, 3, 4, 5, 2).reshape(n_even * Bd, 2 * KV_HEADS, HEAD_DIM, wb)

    xp = x_prompt.reshape(M, D)
    xs = _pad_rows(x_sample.reshape(Bd, D), Ms)
    xpb, xsb = xp.astype(BF16), xs.astype(BF16)

    outs = {k: [] for k in ("cmp_s", "sel_s", "win_s", "conv_p", "conv_s", "pool_p", "pool_s", "sgu_p", "sgu_s")}
    kv6 = lambda a, lead: a.reshape(lead + (2, KV_HEADS, HEAD_DIM))
    kv_t = None

    for layer in range(DEPTH):
        if layer % 2 == 0:
            e = layer // 2
            wts = _prep_even_weights(w_in_even[e], w_out_even[e], cmp_pe_k[e], cmp_pe_v[e], cmp_w_k[e], cmp_w_v[e])
            prj_p, prj_s = _even_in_proj(xpb, xsb, wts["w_in"], rope_p, rope_s, tm=tm_p, bs=(B, S),
                                         stack=(e, n_even, kv_t))
            qraw, qrot, ((kvc, kvc_t), (kvs_b, kvs_t), (kvw_b, kvw_t)), u, gates = prj_p
            kv_t = (kvc_t, kvs_t, kvw_t)
            summ = _compress_rows(kvc, wts["pe"], rows=512, name="prompt_compress")
            (ckv,) = _mm(summ.astype(BF16), wts["big_p"], tm=min(summ.shape[0], 512), tn=KVP,
                         n_off=0, n_cols=KVP, epilogue=_ep_plain, outs=((KVP, F32, KVP),), name="prompt_compress_map")
            o_att = _prompt_attention(qraw, qrot, gates, ckv, kvs_b, kvw_b, batch=B, seq=S, tq=256, tk=256, ngrp=2)
            c = _prompt_conv(u, conv_w[e], conv_b[e], conv_ln_g[e], conv_ln_b[e], batch=B, seq=S, ts=256)
            outs["conv_p"].append(u.reshape(B, S, CONV_CH)[:, S - (CONV_W - 1):])
            qraw_s, qrot_s, (kvc_s, kvs_s, kvw_s), u_s, gates_s = prj_s
            ckv_t = _compress_pages(cmp_t, page_table, wts["pe_t"], wts["big_t"], bsum, e * n_pool, pps=pps,
                                    name="sample_compress")
            q3 = qraw_s.astype(F32)[:Bd].reshape(Bd, N_HEADS, HSLOT)
            o_cmp, idx = _sample_cmp(q3, ckv_t, bd=Bd, past=past, qpos=past, nbl=pps * (PAGE_SIZE // CMP_BLOCK))
            idx = idx[:, :, 0].reshape(Bd, KV_HEADS, TOP_K)
            qr3 = _split_to_nat(qrot_s.astype(F32)[:Bd], N_HEADS).reshape(Bd, N_HEADS, HSLOT)
            q4 = jnp.pad(qr3.reshape(Bd, KV_HEADS, HPG, HSLOT), ((0, 0), (0, 0), (0, Q_PAD_ROWS - HPG), (0, 0)))
            kvs_nat = _split_to_nat(kvs_s, 2 * KV_HEADS)
            kvw_nat = _split_to_nat(kvw_s, 2 * KV_HEADS)
            o_sel = _sample_sel(sel_t, page_table, idx, e * n_pool, q4, kvs_nat, bd=Bd, past=past)
            o_sel = o_sel[:, :, :HPG].reshape(Bd, N_HEADS, HSLOT)
            g3 = gates_s[:Bd].reshape(Bd, KV_HEADS, LANES)[:, :, :3 * HPG].reshape(Bd, KV_HEADS, 3, HPG)
            g3 = g3.transpose(0, 1, 3, 2).reshape(Bd, N_HEADS, 3)
            g3 = jnp.pad(g3, ((0, 0), (0, 0), (0, LANES - 3)))
            o_s = _sample_win(qr3, win_t, e, kvw_nat, o_cmp, o_sel, g3, bd=Bd, past=past, qpos=past)
            c_s = _sample_conv(state_conv, e, u_s, conv_w[e], conv_b[e], conv_ln_g[e], conv_ln_b[e], bd=Bd)
            o_sb = _pad_rows(_nat_to_split(o_s.reshape(Bd, QW), N_HEADS), Ms).astype(BF16)
            c_sb = _pad_rows(c_s, Ms).astype(BF16)
            xp, xpb, xs, xsb = _proj_ln([o_att, c], [wts["wo_att"], wts["wo_conv"]], xp, ln_mix_g[layer],
                                        ln_mix_b[layer], tm=512, name="even_out_ln", side=([o_sb, c_sb], xs))
            kvc_c = _unpad_heads(kvc_s[:Bd], 2 * KV_HEADS)
            kvs_c = _unpad_heads(kvs_s[:Bd], 2 * KV_HEADS)
            kvw_c = _unpad_heads(kvw_s[:Bd], 2 * KV_HEADS)
            outs["cmp_s"].append(kv6(kvc_c, (Bd, 1)))
            outs["sel_s"].append(kv6(kvs_c, (Bd, 1)))
            wkv = jnp.concatenate([cache_win_kv[e], kv6(kvw_c, (Bd, 1))], axis=1)
            outs["win_s"].append(wkv[:, wkv.shape[1] - min(WINDOW, wkv.shape[1]):])
            outs["conv_s"].append(jnp.concatenate([state_conv[e], u_s[:Bd, None, :]], axis=1)[:, 1:])
        else:
            o = layer // 2
            w_in = w_in_odd[o]
            w_in_p = jnp.concatenate([w_in[:, POOL_CH + SGU_CH:], w_in[:, :POOL_CH], w_in[:, POOL_CH:POOL_CH + SGU_CH]],
                                     axis=1).astype(BF16)
            w_out_p = w_out_odd[o].astype(BF16)
            lg, lb = sgu_ln_g[o].reshape(1, SGU_CH), sgu_ln_b[o].reshape(1, SGU_CH)

            gl_ex = ((lg, (1, 2 * SGU_GC), lambda i, j: (0, j)), (lb, (1, 2 * SGU_GC), lambda i, j: (0, j)))
            o_v, o_p, o_u = (SGU_CH, F32, 2 * SGU_GC), (POOL_CH, F32, POOL_CH), (SGU_CH, F32, POOL_CH)
            vn, vn_s = _mm(xpb, w_in_p, tm=tm_p, tn=2 * SGU_GC, n_off=O_V, n_cols=SGU_CH, epilogue=_ep_gelu_gln,
                           extras=gl_ex, outs=(o_v,), name="odd_in_v", side=(xsb, _ep_gelu_gln, gl_ex, (o_v,)))
            pin, pin_s = _mm(xpb, w_in_p, tm=tm_p, tn=POOL_CH, n_off=O_PIN, n_cols=POOL_CH, epilogue=_ep_plain,
                             outs=(o_p,), name="odd_in_pool", side=(xsb, _ep_plain, (), (o_p,)))
            uu, uu_s = _mm(xpb, w_in_p, tm=tm_p, tn=POOL_CH, n_off=O_U, n_cols=SGU_CH, epilogue=_ep_gelu,
                           outs=(o_u,), name="odd_in_u", side=(xsb, _ep_gelu, (), (o_u,)))
            cat = _prompt_odd_mix(pin, uu, vn, pool_w[o], pool_scale[o], sgu_w[o], sgu_b[o], batch=B, seq=S)
            outs["pool_p"].append(pin.reshape(B, S, POOL_CH)[:, S - POOL_STATE:])
            outs["sgu_p"].append(vn.reshape(B, S, SGU_CH)[:, ((S - 1) // CHUNK) * CHUNK:])
            cat_s = _sample_odd_mix(state_pool, o, pin_s, uu_s, vn_s, pool_w[o], pool_scale[o], sgu_w[o], sgu_b[o],
                                    bd=Bd, start_pos=past)
            xp, xpb, xs, xsb = _proj_ln([cat], [w_out_p], xp, ln_mix_g[layer], ln_mix_b[layer], tm=512,
                                        name="odd_out_ln", side=([_pad_rows(cat_s, Ms).astype(BF16)], xs))
            outs["pool_s"].append(jnp.concatenate([state_pool[o], pin_s[:Bd, None, :]], axis=1)[:, 1:])
            outs["sgu_s"].append(vn_s[:Bd, None, :])
        h, h_s, w2b = _mlp_up_cast(xpb, xsb, mlp_w1, mlp_w2, layer, tm=tm_p, tn=1024)
        xp, xpb, xs, xsb = _mlp2_ln(h, w2b, xp, ln_ffn_g[layer], ln_ffn_b[layer], tm=512, tn=512,
                                    name="mlp_down_ln", side=(h_s, xs))

    st = lambda k: jnp.stack(outs[k])
    rows_last = lambda a: a.reshape(n_even, B, 2, KV_HEADS, HEAD_DIM, a.shape[-1]).transpose(0, 1, 5, 2, 3, 4)
    kvc_t, kvs_t, kvw_t = kv_t
    return (xp.reshape(B, S, D), xs[:Bd].reshape(Bd, Sd, D),
            rows_last(kvc_t), st("cmp_s"), rows_last(kvs_t), st("sel_s"),
            rows_last(kvw_t[:, :, :, S - WINDOW:]), st("win_s"),
            st("conv_p"), st("conv_s"), st("pool_p"), st("pool_s"), st("sgu_p"), st("sgu_s"))
```

```python
import functools

import jax
import jax.numpy as jnp
from jax import lax
from jax.experimental import pallas as pl
from jax.experimental.pallas import tpu as pltpu

F32 = jnp.float32
BF16 = jnp.bfloat16

D_MODEL = 2048
DEPTH = 4
PAGE_SIZE = 128
N_HEADS = 16
HEAD_DIM = 96
KV_HEADS = 4
HPG = N_HEADS // KV_HEADS
ATT_W = N_HEADS * HEAD_DIM
KVW = 2 * KV_HEADS * HEAD_DIM
CMP_BLOCK = 32
SEL_BLOCK = 64
TOP_K = 16
WINDOW = 512
ROPE_THETA = 10000.0
SCALE = HEAD_DIM ** -0.5
LOG2E = 1.4426950408889634
FORCE = 1e9
NEG = -1e30
CONV_CH = D_MODEL // 4
CONV_W = 31
POOL_CH = D_MODEL // 4
POOL_WINDOWS = (2, 4, 8, 16)
POOL_GROUPS = len(POOL_WINDOWS)
POOL_GC = POOL_CH // POOL_GROUPS
POOL_STATE = max(POOL_WINDOWS) - 1
SGU_CH = D_MODEL - POOL_CH
SGU_GROUPS = 4
SGU_GC = SGU_CH // SGU_GROUPS
CHUNK = 128
D_FF = 4 * D_MODEL
ALPHA = (2 * DEPTH) ** 0.25
LN_EPS = 1e-5

LANES = 128
SUBLANES = 8
HALF = HEAD_DIM // 2
HSLOT = LANES
HALF_OFF = LANES // 2
QW = N_HEADS * HSLOT
KVP = 2 * KV_HEADS * HSLOT
GATE_W = KV_HEADS * LANES
SAMPLE_ROWS = 16
ROW_SUB = 256
LN_ROW_SUB = 128
VMEM_LIMIT = 52 * 1024 * 1024

E_Q, E_KVC, E_KVS, E_KVW = 0, QW, QW + KVP, QW + 2 * KVP
E_GLU = QW + 3 * KVP
E_GATE = E_GLU + 2 * CONV_CH
E_TOT = E_GATE + GATE_W
O_V, O_PIN, O_U = 0, SGU_CH, SGU_CH + POOL_CH


def _cparams(*sem):
    return pltpu.CompilerParams(dimension_semantics=sem, vmem_limit_bytes=VMEM_LIMIT)


def _pad_head(x):
    halves = x.reshape(x.shape[:-1] + (2, HALF))
    halves = jnp.pad(halves, [(0, 0)] * (halves.ndim - 1) + [(0, HALF_OFF - HALF)])
    return halves.reshape(x.shape[:-1] + (HSLOT,))


def _pad_nat(x):
    return jnp.concatenate([x, jnp.zeros(x.shape[:-1] + (HSLOT - HEAD_DIM,), x.dtype)], axis=-1)


def _unpad_heads(x, nheads):
    xh = x.reshape(x.shape[:-1] + (nheads, HSLOT))
    y = jnp.concatenate([xh[..., :HALF], xh[..., HALF_OFF:HALF_OFF + HALF]], axis=-1)
    return y.reshape(x.shape[:-1] + (nheads * HEAD_DIM,))


def _rope_tables(pos):
    inv = jnp.power(ROPE_THETA, -jnp.arange(HALF, dtype=F32) / HALF)
    ang = pos.astype(F32)[:, None] * inv[None, :]
    cos, sin = jnp.cos(ang), jnp.sin(ang)
    z = jnp.zeros((pos.shape[0], HALF_OFF - HALF), F32)
    return (jnp.concatenate([cos, z, cos, z], axis=1),
            jnp.concatenate([-sin, z, sin, z], axis=1))


def _block_diag2(a, b):
    za = jnp.zeros((a.shape[0], b.shape[1]), a.dtype)
    zb = jnp.zeros((b.shape[0], a.shape[1]), a.dtype)
    return jnp.concatenate([jnp.concatenate([a, za], axis=1), jnp.concatenate([zb, b], axis=1)], axis=0)


def _prep_even_weights(w_in, w_out, pe_k, pe_v, w_ck, w_cv):
    d = w_in.shape[0]
    wb = w_in.astype(BF16)
    g0 = ATT_W + 3 * KVW
    n_heads_all = g0 // HEAD_DIM
    qkv = _pad_head(wb[:, :g0].reshape(d, n_heads_all, HEAD_DIM)).reshape(d, n_heads_all * HSLOT)
    gates = wb[:, g0:g0 + 3 * N_HEADS].reshape(d, 3, KV_HEADS, HPG).transpose(0, 2, 1, 3)
    gates = gates.reshape(d, KV_HEADS, 3 * HPG)
    gates = jnp.pad(gates, ((0, 0), (0, 0), (0, LANES - 3 * HPG))).reshape(d, GATE_W)
    glu = wb[:, g0 + 3 * N_HEADS:].reshape(d, 2, 2, CONV_CH // 2).transpose(0, 2, 1, 3)
    glu = glu.reshape(d, 2 * CONV_CH)
    w_in_p = jnp.concatenate([qkv, glu, gates], axis=1)
    wo = w_out.astype(BF16)
    wo_att = jnp.pad(wo[:ATT_W].reshape(2 * N_HEADS, HALF, d), ((0, 0), (0, HALF_OFF - HALF), (0, 0)))
    wo_att = wo_att.reshape(QW, d)
    wo_conv = wo[ATT_W:]
    eye = jnp.eye(KV_HEADS, dtype=F32)
    pe = jnp.concatenate([jnp.tile(_pad_head(pe_k), (1, KV_HEADS)), jnp.tile(_pad_head(pe_v), (1, KV_HEADS))], axis=1)
    wk_full = _pad_head(_pad_head(w_ck).T).T
    wv_full = _pad_head(_pad_head(w_cv).T).T
    big_p = _block_diag2(jnp.kron(eye, wk_full), jnp.kron(eye, wv_full))
    pe_t = jnp.concatenate([jnp.tile(jnp.tile(pe_k.T, (1, PAGE_SIZE // CMP_BLOCK)), (KV_HEADS, 1)),
                            jnp.tile(jnp.tile(pe_v.T, (1, PAGE_SIZE // CMP_BLOCK)), (KV_HEADS, 1))], axis=0)
    big_t = _block_diag2(jnp.kron(eye, _pad_head(w_ck).T), jnp.kron(eye, _pad_nat(w_cv).T))
    return dict(w_in=w_in_p, wo_att=wo_att, wo_conv=wo_conv, pe=pe, pe_t=pe_t,
                big_p=big_p.astype(BF16), big_t=big_t.astype(BF16))


def _block_sum_matrices(pps):
    bpp = PAGE_SIZE // CMP_BLOCK
    nbl = bpp * pps
    p = jnp.arange(pps)[:, None, None]
    i = (jnp.arange(PAGE_SIZE) // CMP_BLOCK)[None, :, None]
    c = jnp.arange(nbl)[None, None, :]
    col = (i % 2) * (nbl // 2) + (bpp // 2) * p + i // 2
    return jnp.where(c == col, 1.0 / CMP_BLOCK, 0.0).astype(BF16)


def _ln_rows(y, g, b):
    mu = jnp.mean(y, axis=-1, keepdims=True)
    yc = y - mu
    var = jnp.mean(yc * yc, axis=-1, keepdims=True)
    return yc * lax.rsqrt(var + LN_EPS) * g + b


def _rope_slot(x, cos, sin):
    return x * cos + pltpu.roll(x, HALF_OFF, axis=1) * sin


_NT = (((1,), (1,)), ((), ()))


def _pad_rows_to(x, rows):
    return jnp.concatenate([x, jnp.zeros((rows - x.shape[0],) + x.shape[1:], x.dtype)], axis=0)


def _mm(x, w, *, tm, tn, n_off, n_cols, epilogue, extras=(), outs, name, side=None, alias=None):
    m, k = x.shape
    assert m % tm == 0 and n_cols % tn == 0 and n_off % tn == 0
    joff = n_off // tn
    ji = lambda im: (lambda j, i: im(i, j))
    in_specs = [pl.BlockSpec((tm, k), lambda j, i: (i, 0)),
                pl.BlockSpec((k, tn), lambda j, i: (0, joff + j))]
    in_specs += [pl.BlockSpec(bs, ji(im)) for _, bs, im in extras]
    operands = [x, w] + [a for a, _, _ in extras]
    out_shape, out_specs = [], []
    for o in outs:
        if len(o) == 3:
            out_shape.append(jax.ShapeDtypeStruct((m, o[0]), o[1]))
            out_specs.append(pl.BlockSpec((tm, o[2]), lambda j, i: (i, j)))
        else:
            out_shape.append(jax.ShapeDtypeStruct(o[0], o[1]))
            out_specs.append(pl.BlockSpec(o[2], ji(o[3])))
    ne, no = len(extras), len(outs)
    nse = 0
    if side is not None:
        xs, s_epilogue, s_extras, s_outs = side
        rs = xs.shape[0]
        nse = len(s_extras)
        in_specs += [pl.BlockSpec((rs, k), lambda j, i: (0, 0))]
        in_specs += [pl.BlockSpec(bs, ji(im)) for _, bs, im in s_extras]
        operands += [xs] + [a for a, _, _ in s_extras]
        for cols, dt, bc in s_outs:
            out_shape.append(jax.ShapeDtypeStruct((rs, cols), dt))
            out_specs.append(pl.BlockSpec((rs, bc), lambda j, i: (0, j)))
    rsub = min(tm, ROW_SUB)
    n_in = len(operands)
    io_alias = {}
    if alias is not None:
        io_alias = {n_in: alias[1]}
        in_specs.append(pl.BlockSpec(memory_space=pl.ANY))
        operands.append(alias[0])

    def body(*refs):
        x_ref, w_ref = refs[:2]
        ex = refs[2:2 + ne]
        o_refs = refs[len(operands):len(operands) + no]
        for r in range(tm // rsub):
            rows = slice(r * rsub, (r + 1) * rsub)
            acc = jnp.dot(x_ref[rows, :], w_ref[...], preferred_element_type=F32)
            epilogue(acc, ex, o_refs, rows)
        if side is not None:
            @pl.when(pl.program_id(1) == 0)
            def _():
                acc = jnp.dot(refs[2 + ne][...], w_ref[...], preferred_element_type=F32)
                s_epilogue(acc, refs[3 + ne:n_in], refs[len(operands) + no:], slice(0, rs))

    return pl.pallas_call(
        body, grid=(n_cols // tn, m // tm), in_specs=in_specs, out_specs=out_specs, out_shape=out_shape,
        input_output_aliases=io_alias, compiler_params=_cparams("arbitrary", "arbitrary"), name=name,
    )(*operands)


def _ep_q(acc, ex, outs, rows):
    cos, sin = ex[0][rows, :], ex[1][rows, :]
    outs[0][rows, :] = acc.astype(BF16)
    for j in range(acc.shape[1] // HSLOT):
        sl = slice(j * HSLOT, (j + 1) * HSLOT)
        outs[1][rows, sl] = _rope_slot(acc[:, sl], cos, sin).astype(BF16)


def _ep_kv(acc, ex, outs, rows, *, rope, want):
    o = dict(zip(want, outs))
    if rope:
        cos, sin = ex[0][rows, :], ex[1][rows, :]
    for j in range(2 * KV_HEADS):
        sl = slice(j * HSLOT, (j + 1) * HSLOT)
        x = acc[:, sl]
        if rope and j < KV_HEADS:
            x = _rope_slot(x, cos, sin)
        if "f32" in o:
            o["f32"][rows, sl] = x
        if "bf16" in o:
            o["bf16"][rows, sl] = x.astype(BF16)
        if "t" in o:
            xt = x.T
            o["t"][j * HEAD_DIM:j * HEAD_DIM + HALF, rows] = xt[0:HALF]
            o["t"][j * HEAD_DIM + HALF:(j + 1) * HEAD_DIM, rows] = xt[HALF_OFF:HALF_OFF + HALF]


def _ep_glu(acc, ex, outs, rows):
    h = acc.shape[1] // 2
    outs[0][rows, :] = acc[:, :h] * jax.nn.sigmoid(acc[:, h:])


def _ep_sigmoid(acc, ex, outs, rows):
    outs[0][rows, :] = jax.nn.sigmoid(acc)


def _ep_plain(acc, ex, outs, rows):
    outs[0][rows, :] = acc.astype(outs[0].dtype)


def _ep_relu2(acc, ex, outs, rows):
    r = jnp.maximum(acc, 0.0)
    outs[0][rows, :] = (r * r).astype(outs[0].dtype)


def _ep_gelu(acc, ex, outs, rows):
    outs[0][rows, :] = jax.nn.gelu(acc)


def _ep_gelu_gln(acc, ex, outs, rows):
    g, b = ex[0][...], ex[1][...]
    v = jax.nn.gelu(acc)
    for j in range(acc.shape[1] // SGU_GC):
        sl = slice(j * SGU_GC, (j + 1) * SGU_GC)
        outs[0][rows, sl] = _ln_rows(v[:, sl], g[:, sl], b[:, sl])


def _proj_ln(a_list, w_list, resid, g, b, *, tm, name, side):
    m, n = resid.shape
    as_list, resid_s = side
    rs = resid_s.shape[0]
    npair = len(a_list)
    in_specs = []
    for a in a_list:
        in_specs.append(pl.BlockSpec((tm, a.shape[1]), lambda i: (i, 0)))
    for w in w_list:
        in_specs.append(pl.BlockSpec(w.shape, lambda i: (0, 0), pipeline_mode=pl.Buffered(1)))
    in_specs += [pl.BlockSpec((tm, n), lambda i: (i, 0)),
                 pl.BlockSpec((1, n), lambda i: (0, 0)), pl.BlockSpec((1, n), lambda i: (0, 0))]
    for a in as_list:
        in_specs.append(pl.BlockSpec((rs, a.shape[1]), lambda i: (0, 0)))
    in_specs.append(pl.BlockSpec((rs, n), lambda i: (0, 0)))

    def body(*refs):
        a_refs, w_refs = refs[:npair], refs[npair:2 * npair]
        r_ref, g_ref, b_ref = refs[2 * npair:2 * npair + 3]
        as_refs = refs[2 * npair + 3:3 * npair + 3]
        rs_ref, o_ref, ob_ref, os_ref, osb_ref = refs[3 * npair + 3:]

        def rows_out(a_rs, res, rows):
            acc = ALPHA * res[rows, :]
            for a_ref, w_ref in zip(a_rs, w_refs):
                acc = acc + jnp.dot(a_ref[rows, :], w_ref[...], preferred_element_type=F32)
            return _ln_rows(acc, g_ref[...], b_ref[...])

        rsub = min(tm, LN_ROW_SUB)
        for r in range(tm // rsub):
            rows = slice(r * rsub, (r + 1) * rsub)
            y = rows_out(a_refs, r_ref, rows)
            o_ref[rows, :] = y
            ob_ref[rows, :] = y.astype(BF16)
        @pl.when(pl.program_id(0) == 0)
        def _():
            ys = rows_out(as_refs, rs_ref, slice(0, rs))
            os_ref[...] = ys
            osb_ref[...] = ys.astype(BF16)

    row = pl.BlockSpec((tm, n), lambda i: (i, 0))
    srow = pl.BlockSpec((rs, n), lambda i: (0, 0))
    return pl.pallas_call(
        body, grid=(m // tm,), in_specs=in_specs, out_specs=[row, row, srow, srow],
        out_shape=[jax.ShapeDtypeStruct((m, n), F32), jax.ShapeDtypeStruct((m, n), BF16),
                   jax.ShapeDtypeStruct((rs, n), F32), jax.ShapeDtypeStruct((rs, n), BF16)],
        compiler_params=_cparams("arbitrary"), name=name,
    )(*a_list, *w_list, resid, g.reshape(1, n), b.reshape(1, n), *as_list, resid_s)


def _mlp2_ln(h, w2, resid, g, b, *, tm, tn, name, side):
    m, kf = h.shape
    n = w2.shape[1]
    nj = n // tn
    hs, resid_s = side
    rs = hs.shape[0]
    rsub, rsub_ln = min(tm, ROW_SUB), min(tm, LN_ROW_SUB)

    def body(h_ref, w_ref, r_ref, g_ref, b_ref, hs_ref, rs_ref, o_ref, ob_ref, os_ref, osb_ref):
        j = pl.program_id(1)
        first = pl.program_id(0) == 0
        srows = slice(0, rs)

        def tile(hr, rr, rows):
            return ALPHA * rr[rows, :] + jnp.dot(hr[rows, :], w_ref[...], preferred_element_type=F32)

        def finish(o, ob, y, rows):
            y = _ln_rows(y, g_ref[...], b_ref[...])
            o[rows, :] = y
            ob[rows, :] = y.astype(BF16)

        for jj in range(nj - 1):
            @pl.when(j == jj)
            def _(jj=jj):
                cols = slice(jj * tn, (jj + 1) * tn)
                for r in range(tm // rsub):
                    rows = slice(r * rsub, (r + 1) * rsub)
                    o_ref[rows, cols] = tile(h_ref, r_ref, rows)

                @pl.when(first)
                def _():
                    os_ref[:, cols] = tile(hs_ref, rs_ref, srows)

        @pl.when(j == nj - 1)
        def _():
            done = slice(0, (nj - 1) * tn)
            for r in range(tm // rsub_ln):
                rows = slice(r * rsub_ln, (r + 1) * rsub_ln)
                finish(o_ref, ob_ref, jnp.concatenate([o_ref[rows, done], tile(h_ref, r_ref, rows)], axis=1), rows)

            @pl.when(first)
            def _():
                finish(os_ref, osb_ref, jnp.concatenate([os_ref[:, done], tile(hs_ref, rs_ref, srows)], axis=1), srows)

    assert nj > 1
    row = pl.BlockSpec((tm, n), lambda i, j: (i, 0))
    srow = pl.BlockSpec((rs, n), lambda i, j: (0, 0))
    return pl.pallas_call(
        body, grid=(m // tm, nj),
        in_specs=[pl.BlockSpec((tm, kf), lambda i, j: (i, 0)), pl.BlockSpec((kf, tn), lambda i, j: (0, j)),
                  pl.BlockSpec((tm, tn), lambda i, j: (i, j)),
                  pl.BlockSpec((1, n), lambda i, j: (0, 0)), pl.BlockSpec((1, n), lambda i, j: (0, 0)),
                  pl.BlockSpec((rs, kf), lambda i, j: (0, 0)), pl.BlockSpec((rs, tn), lambda i, j: (0, j))],
        out_specs=[row, row, srow, srow],
        out_shape=[jax.ShapeDtypeStruct((m, n), F32), jax.ShapeDtypeStruct((m, n), BF16),
                   jax.ShapeDtypeStruct((rs, n), F32), jax.ShapeDtypeStruct((rs, n), BF16)],
        compiler_params=_cparams("arbitrary", "arbitrary"), name=name,
    )(h, w2, resid, g.reshape(1, n), b.reshape(1, n), hs, resid_s)


def _compress_rows(kvc, pe, *, rows, name):
    m, c = kvc.shape
    nb = rows // CMP_BLOCK

    def body(x_ref, pe_ref, o_ref):
        x = x_ref[...].reshape(nb, CMP_BLOCK, c) * pe_ref[...][None]
        o_ref[...] = jnp.sum(x, axis=1) * (1.0 / CMP_BLOCK)

    return pl.pallas_call(
        body, grid=(m // rows,),
        in_specs=[pl.BlockSpec((rows, c), lambda i: (i, 0)), pl.BlockSpec((CMP_BLOCK, c), lambda i: (0, 0))],
        out_specs=pl.BlockSpec((nb, c), lambda i: (i, 0)),
        out_shape=jax.ShapeDtypeStruct((m // CMP_BLOCK, c), F32),
        compiler_params=_cparams("parallel"), name=name,
    )(kvc, pe)


def _compress_pages(pages_t, page_table, pe_t, big_t, bsum, layer_base, *, pps, name):
    bd, n_pages = page_table.shape
    bpp = PAGE_SIZE // CMP_BLOCK
    nbl = bpp * pps
    nsteps = n_pages // pps

    def body(pt_ref, *refs):
        page_refs = refs[:pps]
        pe_ref, big_ref, bsum_ref, o_ref = refs[pps:]
        acc = jnp.zeros((KVW, nbl), F32)
        for p in range(pps):
            x = page_refs[p][...] * pe_ref[...]
            acc = acc + jnp.dot(x.astype(BF16), bsum_ref[p], preferred_element_type=F32)
        o_ref[...] = jnp.dot(big_ref[...], acc.astype(BF16), preferred_element_type=F32)

    def page_map(p):
        return lambda b, j, pt: (layer_base + pt[b, j * pps + p], 0, 0)

    cst2 = lambda b, j, pt: (0, 0)
    grid_spec = pltpu.PrefetchScalarGridSpec(
        num_scalar_prefetch=1, grid=(bd, nsteps),
        in_specs=[pl.BlockSpec((None, KVW, PAGE_SIZE), page_map(p)) for p in range(pps)]
        + [pl.BlockSpec((KVW, PAGE_SIZE), cst2), pl.BlockSpec((KVP, KVW), cst2),
           pl.BlockSpec((pps, PAGE_SIZE, nbl), lambda b, j, pt: (0, 0, 0))],
        out_specs=pl.BlockSpec((None, KVP, nbl), lambda b, j, pt: (b, 0, j)))
    return pl.pallas_call(
        body, grid_spec=grid_spec,
        out_shape=jax.ShapeDtypeStruct((bd, KVP, n_pages * bpp), F32),
        compiler_params=_cparams("parallel", "arbitrary"), name=name,
    )(page_table, *([pages_t] * pps), pe_t, big_t, bsum)


def _flash_step_t(q, k, vt, valid, m_ref, l_ref, acc_ref):
    bias = jnp.where(valid, 0.0, NEG)
    s = jnp.dot(k, q, preferred_element_type=F32) + jnp.concatenate([bias] * HPG, axis=1)
    m_prev = m_ref[...]
    m_new = jnp.maximum(m_prev, jnp.max(s, axis=0, keepdims=True))
    alpha = jnp.exp2(m_prev - m_new)
    p = jnp.exp2(s - m_new)
    l_ref[...] = alpha * l_ref[...] + jnp.sum(p, axis=0, keepdims=True)
    acc_ref[...] = alpha * acc_ref[...] + jnp.dot(vt, p.astype(BF16), preferred_element_type=F32)
    m_ref[...] = m_new


def _attn_body(*refs, tq, tk, seq, k_top, ngrp):
    n_in = 9
    ins = [refs[g * n_in:(g + 1) * n_in] for g in range(ngrp)]
    o_ref = refs[ngrp * n_in]
    m_ref, l_ref, acc_ref, sel_ref, vts_ref, vtw_ref = refs[ngrp * n_in + 1:]
    i = pl.program_id(2)
    cols = HPG * tq
    nsb = seq // SEL_BLOCK
    nt = seq // tk
    bpt = tk // SEL_BLOCK
    q0 = i * tq
    lane_q = lax.broadcasted_iota(jnp.int32, (1, cols), 1)
    qpos = q0 + (lane_q & (tq - 1))
    qp1 = q0 + lax.broadcasted_iota(jnp.int32, (1, tq), 1)

    @pl.when(i == 0)
    def _():
        def tr(t, c):
            for g in range(ngrp):
                vs_ref, vw_ref = ins[g][6], ins[g][8]
                for h in range(tk // LANES):
                    k0 = pl.multiple_of(t * tk + h * LANES, LANES)
                    hs = slice(h * LANES, (h + 1) * LANES)
                    vts_ref[g, t, :, hs] = vs_ref[pl.ds(k0, LANES), :].astype(F32).T.astype(BF16)
                    vtw_ref[g, t, :, hs] = vw_ref[pl.ds(k0, LANES), :].astype(F32).T.astype(BF16)
            return c
        lax.fori_loop(0, nt, tr, 0)

    def heads_t(ref, scale=1.0):
        parts = [ref[:, j * HSLOT:(j + 1) * HSLOT].astype(F32).T * scale for j in range(HPG)]
        return jnp.concatenate(parts, axis=1).astype(BF16)

    r = lax.broadcasted_iota(jnp.int32, (2 * nsb, 1), 0)
    n_of = jnp.where(r < nsb, 2 * r, 2 * (r - nsb) + 1)
    mk = ((n_of + 1) * CMP_BLOCK - 1) <= qpos
    sb = lax.broadcasted_iota(jnp.int32, (nsb, 1), 0)
    vis = (sb * SEL_BLOCK) <= qp1
    cur = sb == (qp1 >> (SEL_BLOCK.bit_length() - 1))

    def compressed_branch(g):
        qraw_ref, ck_ref, cv_ref = ins[g][0], ins[g][3], ins[g][4]
        qr = heads_t(qraw_ref)
        ck = jnp.concatenate([ck_ref[pl.ds(0, nsb, stride=2), :], ck_ref[pl.ds(1, nsb, stride=2), :]], axis=0)
        s = jnp.dot(ck.astype(BF16), qr, preferred_element_type=F32) * SCALE
        s = jnp.where(mk, s, NEG)
        mx = jnp.max(s, axis=0, keepdims=True)
        p = jnp.where(mk, jnp.exp(s - mx), 0.0)
        pn = p / jnp.maximum(jnp.sum(p, axis=0, keepdims=True), 1e-30)
        cv = jnp.concatenate([cv_ref[pl.ds(0, nsb, stride=2), :], cv_ref[pl.ds(1, nsb, stride=2), :]], axis=0)
        cvt = _pad_rows_to(cv, LANES).T.astype(BF16)
        o_cmp = jnp.dot(cvt, _pad_rows_to(pn, LANES).astype(BF16), preferred_element_type=F32)
        pp = pn[0:nsb] + pn[nsb:2 * nsb]
        imp = pp[:, 0:tq]
        for j in range(1, HPG):
            imp = imp + pp[:, j * tq:(j + 1) * tq]
        imp = jnp.where(cur, FORCE, jnp.where(vis, imp, -FORCE))
        cnt = jnp.zeros((nsb, tq), F32)
        for j in range(nsb):
            rowj = imp[j:j + 1, :]
            beats = (rowj > imp) | ((rowj == imp) & (j < sb))
            cnt = cnt + jnp.where(beats, 1.0, 0.0)
        sel = jnp.where(cnt < k_top, 1.0, 0.0)
        for t in range(nt):
            sel_ref[g, t, 0:bpt, :] = sel[bpt * t:bpt * (t + 1), :]
        return o_cmp

    o_cmp = [compressed_branch(g) for g in range(ngrp)]
    qt = [heads_t(ins[g][1], SCALE * LOG2E) for g in range(ngrp)]
    rowk = lax.broadcasted_iota(jnp.int32, (tk, 1), 0)

    def reset():
        m_ref[...] = jnp.full((ngrp, 1, cols), NEG, F32)
        l_ref[...] = jnp.zeros((ngrp, 1, cols), F32)
        acc_ref[...] = jnp.zeros((ngrp, HSLOT, cols), F32)

    def result(g):
        return acc_ref[g] / jnp.maximum(l_ref[g], 1e-30)

    reset()

    def sel_step(t, c):
        k0 = pl.multiple_of(t * tk, tk)
        kp = k0 + rowk
        for g in range(ngrp):
            sm = sel_ref[g, t, bpt - 1:bpt, :]
            for j in range(bpt - 2, -1, -1):
                sm = jnp.where(rowk < (j + 1) * SEL_BLOCK, sel_ref[g, t, j:j + 1, :], sm)
            valid = (kp <= qp1) & (sm > 0.5)
            _flash_step_t(qt[g], ins[g][5][pl.ds(k0, tk), :], vts_ref[g, t], valid,
                          m_ref.at[g], l_ref.at[g], acc_ref.at[g])
        return c

    t_end = lax.div(q0 + tq + tk - 1, tk)
    lax.fori_loop(0, t_end, sel_step, 0)
    o_sel = [result(g) for g in range(ngrp)]

    reset()

    def win_step(t, c):
        k0 = pl.multiple_of(t * tk, tk)
        kp = k0 + rowk
        valid = (kp <= qp1) & (kp >= qp1 - WINDOW)
        for g in range(ngrp):
            _flash_step_t(qt[g], ins[g][7][pl.ds(k0, tk), :], vtw_ref[g, t], valid,
                          m_ref.at[g], l_ref.at[g], acc_ref.at[g])
        return c

    lax.fori_loop(lax.div(jnp.maximum(q0 - WINDOW, 0), tk), t_end, win_step, 0)

    gw = HPG * HSLOT
    for g in range(ngrp):
        gt = ins[g][2][...].T
        o = jnp.zeros((HSLOT, cols), F32)
        for br, o_br in enumerate((o_cmp[g], o_sel[g], result(g))):
            grow = jnp.concatenate([gt[br * HPG + j:br * HPG + j + 1, :] for j in range(HPG)], axis=1)
            o = o + grow * o_br
        for j in range(HPG):
            o_ref[:, g * gw + j * HSLOT:g * gw + (j + 1) * HSLOT] = o[:, j * tq:(j + 1) * tq].T.astype(BF16)


def _prompt_attention(qraw, qrot, gates, ckv, kvs_b, kvw_b, *, batch, seq, tq, tk, ngrp):
    nq = seq // tq
    ncb = seq // CMP_BLOCK
    nt = seq // tk
    cols = HPG * tq
    k_top = min(TOP_K, seq // SEL_BLOCK)
    assert tq & (tq - 1) == 0 and seq % tk == 0 and tk % LANES == 0 and tk // SEL_BLOCK <= SUBLANES
    assert KV_HEADS % ngrp == 0
    body = functools.partial(_attn_body, tq=tq, tk=tk, seq=seq, k_top=k_top, ngrp=ngrp)
    gw = HPG * HSLOT
    in_specs, operands = [], []
    for g in range(ngrp):
        qmap = functools.partial(lambda b, p, i, g: (b * nq + i, p * ngrp + g), g=g)
        kmap = functools.partial(lambda b, p, i, g: (b, p * ngrp + g), g=g)
        vmap_ = functools.partial(lambda b, p, i, g: (b, KV_HEADS + p * ngrp + g), g=g)
        in_specs += [pl.BlockSpec((tq, gw), qmap), pl.BlockSpec((tq, gw), qmap), pl.BlockSpec((tq, LANES), qmap),
                     pl.BlockSpec((ncb, HSLOT), kmap), pl.BlockSpec((ncb, HSLOT), vmap_),
                     pl.BlockSpec((seq, HSLOT), kmap), pl.BlockSpec((seq, HSLOT), vmap_),
                     pl.BlockSpec((seq, HSLOT), kmap), pl.BlockSpec((seq, HSLOT), vmap_)]
        operands += [qraw, qrot, gates, ckv, ckv, kvs_b, kvs_b, kvw_b, kvw_b]
    return pl.pallas_call(
        body, grid=(batch, KV_HEADS // ngrp, nq), in_specs=in_specs,
        out_specs=pl.BlockSpec((tq, ngrp * gw), lambda b, p, i: (b * nq + i, p)),
        out_shape=jax.ShapeDtypeStruct((batch * seq, QW), BF16),
        scratch_shapes=[pltpu.VMEM((ngrp, 1, cols), F32), pltpu.VMEM((ngrp, 1, cols), F32),
                        pltpu.VMEM((ngrp, HSLOT, cols), F32), pltpu.VMEM((ngrp, nt, SUBLANES, tq), F32),
                        pltpu.VMEM((ngrp, nt, HSLOT, tk), BF16), pltpu.VMEM((ngrp, nt, HSLOT, tk), BF16)],
        compiler_params=_cparams("parallel", "parallel", "arbitrary"), name="prompt_attention",
    )(*operands)


CONV_HALO = 32


def _conv_body(cur_ref, prev_ref, w_ref, b_ref, g_ref, bn_ref, o_ref, ext_ref, *, ts):
    c = pl.program_id(1)
    span = CONV_HALO + ts - SUBLANES
    ext_ref[0, 0:CONV_HALO, :] = jnp.where(c > 0, prev_ref[...], 0.0)
    ext_ref[0, CONV_HALO:CONV_HALO + ts, :] = cur_ref[...]
    for s in range(1, SUBLANES):
        ext_ref[s, 0:span, :] = ext_ref[0, pl.ds(s, span), :]
    acc = jnp.zeros((ts, CONV_CH), F32) + b_ref[...]
    off = CONV_HALO - (CONV_W - 1)
    for k in range(CONV_W):
        s = (off + k) % SUBLANES
        acc = acc + ext_ref[s, pl.ds(off + k - s, ts), :] * w_ref[k:k + 1, :]
    y = _ln_rows(acc, g_ref[...], bn_ref[...])
    o_ref[...] = (y * jax.nn.sigmoid(y)).astype(BF16)


def _prompt_conv(u, cw, cb, cg, cbn, *, batch, seq, ts):
    nt = seq // ts
    r = ts // CONV_HALO
    cwp = jnp.pad(cw, ((0, CONV_HALO - CONV_W), (0, 0)))
    vec = lambda a: a.reshape(1, CONV_CH)
    cst = lambda b, c: (0, 0)
    return pl.pallas_call(
        functools.partial(_conv_body, ts=ts), grid=(batch, nt),
        in_specs=[pl.BlockSpec((ts, CONV_CH), lambda b, c: (b * nt + c, 0)),
                  pl.BlockSpec((CONV_HALO, CONV_CH), lambda b, c: (jnp.maximum((b * nt + c) * r - 1, 0), 0)),
                  pl.BlockSpec((CONV_HALO, CONV_CH), cst),
                  pl.BlockSpec((1, CONV_CH), cst), pl.BlockSpec((1, CONV_CH), cst), pl.BlockSpec((1, CONV_CH), cst)],
        out_specs=pl.BlockSpec((ts, CONV_CH), lambda b, c: (b * nt + c, 0)),
        out_shape=jax.ShapeDtypeStruct((batch * seq, CONV_CH), BF16),
        scratch_shapes=[pltpu.VMEM((SUBLANES, CONV_HALO + ts, CONV_CH), F32)],
        compiler_params=_cparams("parallel", "arbitrary"), name="prompt_conv",
    )(u, u, cwp, vec(cb), vec(cg), vec(cbn))


POOL_HALO = 16


def _odd_mix_body(pin_ref, prev_ref, u_ref, vn_ref, pw_ref, ps_ref, sw_ref, sb_ref, o_ref, ext_ref):
    c = pl.program_id(1)
    ext_ref[0:POOL_HALO, :] = jnp.where(c > 0, prev_ref[...], 0.0)
    ext_ref[POOL_HALO:POOL_HALO + CHUNK, :] = pin_ref[...]
    t = c * CHUNK + lax.broadcasted_iota(jnp.int32, (CHUNK, 1), 0)
    for g, w in enumerate(POOL_WINDOWS):
        sl = slice(g * POOL_GC, (g + 1) * POOL_GC)
        tot = ext_ref[pl.ds(POOL_HALO, CHUNK), sl]
        for j in range(1, w):
            tot = tot + ext_ref[pl.ds(POOL_HALO - j, CHUNK), sl]
        cnt = jnp.minimum(w, t + 1).astype(F32)
        d = tot / cnt - pin_ref[:, sl]
        y = jnp.dot(d.astype(BF16), pw_ref[g], preferred_element_type=F32)
        o_ref[:, sl] = (y * ps_ref[:, sl]).astype(BF16)
    ri = lax.broadcasted_iota(jnp.int32, (CHUNK, CHUNK), 0)
    ci = lax.broadcasted_iota(jnp.int32, (CHUNK, CHUNK), 1)
    for g in range(SGU_GROUPS):
        sl = slice(g * SGU_GC, (g + 1) * SGU_GC)
        ws = jnp.where(ci <= ri, sw_ref[g], 0.0).astype(BF16)
        mixed = jnp.dot(ws, vn_ref[:, sl].astype(BF16), preferred_element_type=F32) + sb_ref[:, g:g + 1]
        o_ref[:, POOL_CH + g * SGU_GC:POOL_CH + (g + 1) * SGU_GC] = (u_ref[:, sl] * mixed).astype(BF16)


def _prompt_odd_mix(pin, u, vn, pool_w, pool_scale, sgu_w, sgu_b, *, batch, seq):
    nt = seq // CHUNK
    r = CHUNK // POOL_HALO
    cst2 = lambda b, c: (0, 0)
    cst3 = lambda b, c: (0, 0, 0)
    row = lambda b, c: (b * nt + c, 0)
    return pl.pallas_call(
        _odd_mix_body, grid=(batch, nt),
        in_specs=[pl.BlockSpec((CHUNK, POOL_CH), row),
                  pl.BlockSpec((POOL_HALO, POOL_CH), lambda b, c: (jnp.maximum((b * nt + c) * r - 1, 0), 0)),
                  pl.BlockSpec((CHUNK, SGU_CH), row), pl.BlockSpec((CHUNK, SGU_CH), row),
                  pl.BlockSpec((POOL_GROUPS, POOL_GC, POOL_GC), cst3), pl.BlockSpec((1, POOL_CH), cst2),
                  pl.BlockSpec((SGU_GROUPS, CHUNK, CHUNK), cst3), pl.BlockSpec((CHUNK, SGU_GROUPS), cst2)],
        out_specs=pl.BlockSpec((CHUNK, D_MODEL), row),
        out_shape=jax.ShapeDtypeStruct((batch * seq, D_MODEL), BF16),
        scratch_shapes=[pltpu.VMEM((POOL_HALO + CHUNK, POOL_CH), F32)],
        compiler_params=_cparams("parallel", "arbitrary"), name="prompt_pool_sgu",
    )(pin, pin, u, vn, pool_w.astype(BF16), pool_scale.reshape(1, POOL_CH), sgu_w, sgu_b.T)


def _group_rows(nrows):
    return lax.broadcasted_iota(jnp.int32, (nrows, 1), 0) >> (HPG.bit_length() - 1)


def _sample_cmp_body(q_ref, ckv_ref, o_ref, idx_ref, *, qpos, ncb, nbl, k_past):
    q = q_ref[...].astype(BF16)
    rg = _group_rows(N_HEADS)
    half = nbl // 2
    assert nbl & (nbl - 1) == 0
    sh = nbl.bit_length() - 1
    lane = lax.broadcasted_iota(jnp.int32, (1, ncb), 1)
    grp, w = lane >> sh, lane & (nbl - 1)
    n_cmp = grp * nbl + 2 * (w & (half - 1)) + (w >> (sh - 1))
    mk = jnp.broadcast_to(((n_cmp + 1) * CMP_BLOCK - 1) <= qpos, (N_HEADS, ncb))
    s = jnp.zeros((N_HEADS, ncb), F32)
    for g in range(KV_HEADS):
        ck = ckv_ref[g * HSLOT:(g + 1) * HSLOT, :].astype(BF16)
        s = jnp.where(rg == g, jnp.dot(q, ck, preferred_element_type=F32) * SCALE, s)
    s = jnp.where(mk, s, NEG)
    mx = jnp.max(s, axis=-1, keepdims=True)
    p = jnp.where(mk, jnp.exp(s - mx), 0.0)
    pn = p / jnp.maximum(jnp.sum(p, axis=-1, keepdims=True), 1e-30)
    o = jnp.zeros((N_HEADS, HSLOT), F32)
    for g in range(KV_HEADS):
        cv = ckv_ref[(KV_HEADS + g) * HSLOT:(KV_HEADS + g + 1) * HSLOT, :].astype(BF16)
        o = jnp.where(rg == g, lax.dot_general(pn.astype(BF16), cv, _NT, preferred_element_type=F32), o)
    o_ref[...] = o
    pair = pn + pltpu.roll(pn, ncb - half, axis=1)
    valid = w < half
    sb = grp * half + w
    vis = (sb * SEL_BLOCK) <= qpos
    ri = lax.broadcasted_iota(jnp.int32, (ncb, ncb), 0)
    sb_r = (ri >> sh) * half + (ri & (nbl - 1))
    sb_c = jnp.broadcast_to(sb, (ncb, ncb))
    slot = lax.broadcasted_iota(jnp.int32, (TOP_K, 1), 0)
    for g in range(KV_HEADS):
        imp = jnp.sum(jnp.where(rg == g, pair, 0.0), axis=0, keepdims=True)
        imp = jnp.where(valid, jnp.where(vis, imp, -FORCE), -2.0 * FORCE)
        a = jnp.broadcast_to(imp, (ncb, ncb))
        bt = a.T
        beats = (bt > a) | ((bt == a) & (sb_r < sb_c))
        rank = jnp.sum(jnp.where(beats, 1.0, 0.0), axis=0, keepdims=True)
        onehot = jnp.where((rank == slot.astype(F32)) & valid, 1.0, 0.0)
        idx = jnp.sum(onehot * sb.astype(F32), axis=-1, keepdims=True)
        idx = jnp.where(slot < k_past, idx, 0.0)
        idx_ref[g * TOP_K:(g + 1) * TOP_K, :] = jnp.broadcast_to(idx, (TOP_K, LANES)).astype(jnp.int32)


def _sample_cmp(q3, ckv_t, *, bd, past, qpos, nbl):
    ncb = past // CMP_BLOCK
    k_past = min(TOP_K - 1, past // SEL_BLOCK)
    body = functools.partial(_sample_cmp_body, qpos=qpos, ncb=ncb, nbl=nbl, k_past=k_past)
    head3 = pl.BlockSpec((None, N_HEADS, HSLOT), lambda b: (b, 0, 0))
    return pl.pallas_call(
        body, grid=(bd,),
        in_specs=[head3, pl.BlockSpec((None, KVP, ncb), lambda b: (b, 0, 0))],
        out_specs=[head3, pl.BlockSpec((None, KV_HEADS * TOP_K, LANES), lambda b: (b, 0, 0))],
        out_shape=[jax.ShapeDtypeStruct((bd, N_HEADS, HSLOT), F32),
                   jax.ShapeDtypeStruct((bd, KV_HEADS * TOP_K, LANES), jnp.int32)],
        compiler_params=_cparams("parallel"), name="sample_cmp_attention",
    )(q3, ckv_t)


Q_PAD_ROWS = 8


def _pad_dt(x):
    return _pad_rows_to(x, HSLOT)


def _sample_sel_body(pt_ref, idx_ref, *refs, k_past):
    k_refs, v_refs = refs[:k_past], refs[k_past:2 * k_past]
    q_ref, knew_ref, vnew_ref, o_ref = refs[2 * k_past:]
    b, g = pl.program_id(0), pl.program_id(1)
    bpp = PAGE_SIZE // SEL_BLOCK
    q = q_ref[...].astype(BF16)
    half_of_lane = lax.broadcasted_iota(jnp.int32, (1, PAGE_SIZE), 1) // SEL_BLOCK
    s_parts, m_parts = [], []
    for s in range(k_past):
        kt = _pad_dt(k_refs[s][...]).astype(BF16)
        s_parts.append(jnp.dot(q, kt, preferred_element_type=F32) * SCALE)
        m_parts.append(jnp.broadcast_to(half_of_lane == (idx_ref[b, g, s] & (bpp - 1)), (Q_PAD_ROWS, PAGE_SIZE)))
    s_old = jnp.concatenate(s_parts, axis=1)
    mk = jnp.concatenate(m_parts, axis=1)
    s_old = jnp.where(mk, s_old, NEG)
    s_all = lax.dot_general(q, knew_ref[...].astype(BF16), _NT, preferred_element_type=F32) * SCALE
    lane = lax.broadcasted_iota(jnp.int32, s_all.shape, 1)
    s_new = jnp.sum(jnp.where(lane == b, s_all, 0.0), axis=-1, keepdims=True)
    mx = jnp.maximum(jnp.max(s_old, axis=-1, keepdims=True), s_new)
    p_old = jnp.where(mk, jnp.exp(s_old - mx), 0.0)
    p_new = jnp.exp(s_new - mx)
    den = jnp.maximum(jnp.sum(p_old, axis=-1, keepdims=True) + p_new, 1e-30)
    v_new = vnew_ref[pl.ds(b, 1), :].astype(BF16).astype(F32)
    o = p_new.astype(BF16).astype(F32) * v_new
    for s in range(k_past):
        vt = _pad_dt(v_refs[s][...]).astype(BF16)
        ps = p_old[:, s * PAGE_SIZE:(s + 1) * PAGE_SIZE].astype(BF16)
        o = o + lax.dot_general(ps, vt, _NT, preferred_element_type=F32)
    o_ref[...] = o / den


def _sample_sel(pages5, page_table, idx, layer_base, q4, kvs_new, *, bd, past):
    k_past = min(TOP_K - 1, past // SEL_BLOCK)
    bpp_shift = (PAGE_SIZE // SEL_BLOCK).bit_length() - 1

    def blk_map(s, kv):
        def f(b, g, pt, ix):
            return (layer_base + pt[b, lax.shift_right_logical(ix[b, g, s], bpp_shift)], kv, g, 0, 0)
        return f

    tile = lambda s, kv: pl.BlockSpec((None, None, None, HEAD_DIM, PAGE_SIZE), blk_map(s, kv))
    grid_spec = pltpu.PrefetchScalarGridSpec(
        num_scalar_prefetch=2, grid=(bd, KV_HEADS),
        in_specs=[tile(s, 0) for s in range(k_past)] + [tile(s, 1) for s in range(k_past)]
        + [pl.BlockSpec((None, None, Q_PAD_ROWS, HSLOT), lambda b, g, pt, ix: (b, g, 0, 0)),
           pl.BlockSpec((SAMPLE_ROWS, HSLOT), lambda b, g, pt, ix: (0, g)),
           pl.BlockSpec((SAMPLE_ROWS, HSLOT), lambda b, g, pt, ix: (0, KV_HEADS + g))],
        out_specs=pl.BlockSpec((None, None, Q_PAD_ROWS, HSLOT), lambda b, g, pt, ix: (b, g, 0, 0)))
    return pl.pallas_call(
        functools.partial(_sample_sel_body, k_past=k_past), grid_spec=grid_spec,
        out_shape=jax.ShapeDtypeStruct((bd, KV_HEADS, Q_PAD_ROWS, HSLOT), F32),
        compiler_params=_cparams("parallel", "arbitrary"), name="sample_sel_attention",
    )(page_table, idx, *([pages5] * (2 * k_past)), q4, kvs_new, kvs_new)


def _sample_win_body(q_ref, win_ref, new_ref, ocmp_ref, osel_ref, gate_ref, o_ref, *, qpos, past, wb):
    b = pl.program_id(0)
    q = q_ref[...].astype(BF16)
    qf = q.astype(F32)
    rg = _group_rows(N_HEADS)
    new = new_ref[pl.ds(b, 1), :].astype(BF16).astype(F32)
    kpos = (past - wb) + lax.broadcasted_iota(jnp.int32, (1, wb), 1)
    mk = jnp.broadcast_to((kpos <= qpos) & (kpos >= qpos - WINDOW), (N_HEADS, wb))
    s_old = jnp.zeros((N_HEADS, wb), F32)
    s_new = jnp.zeros((N_HEADS, 1), F32)
    for g in range(KV_HEADS):
        kt = _pad_dt(win_ref[g]).astype(BF16)
        s_old = jnp.where(rg == g, jnp.dot(q, kt, preferred_element_type=F32) * SCALE, s_old)
        sn = jnp.sum(qf * new[:, g * HSLOT:(g + 1) * HSLOT], axis=-1, keepdims=True) * SCALE
        s_new = jnp.where(rg == g, sn, s_new)
    s_old = jnp.where(mk, s_old, NEG)
    mx = jnp.maximum(jnp.max(s_old, axis=-1, keepdims=True), s_new)
    p_old = jnp.where(mk, jnp.exp(s_old - mx), 0.0)
    p_new = jnp.exp(s_new - mx)
    den = jnp.maximum(jnp.sum(p_old, axis=-1, keepdims=True) + p_new, 1e-30)
    o_win = jnp.zeros((N_HEADS, HSLOT), F32)
    for g in range(KV_HEADS):
        vt = _pad_dt(win_ref[KV_HEADS + g]).astype(BF16)
        og = lax.dot_general(p_old.astype(BF16), vt, _NT, preferred_element_type=F32)
        og = og + p_new.astype(BF16).astype(F32) * new[:, (KV_HEADS + g) * HSLOT:(KV_HEADS + g + 1) * HSLOT]
        o_win = jnp.where(rg == g, og, o_win)
    o_win = o_win / den
    gts = gate_ref[...]
    o_ref[...] = gts[:, 0:1] * ocmp_ref[...] + gts[:, 1:2] * osel_ref[...] + gts[:, 2:3] * o_win


def _sample_win(q3, win4, layer, kvw_new, o_cmp, o_sel, gates3, *, bd, past, qpos):
    wb = win4.shape[-1]
    head3 = pl.BlockSpec((None, N_HEADS, HSLOT), lambda b: (b, 0, 0))
    return pl.pallas_call(
        functools.partial(_sample_win_body, qpos=qpos, past=past, wb=wb), grid=(bd,),
        in_specs=[head3, pl.BlockSpec((None, 2 * KV_HEADS, HEAD_DIM, wb), lambda b: (layer * bd + b, 0, 0, 0)),
                  pl.BlockSpec((SAMPLE_ROWS, KVP), lambda b: (0, 0)), head3, head3, head3],
        out_specs=head3,
        out_shape=jax.ShapeDtypeStruct((bd, N_HEADS, HSLOT), F32),
        compiler_params=_cparams("parallel"), name="sample_win_attention",
    )(q3, win4, kvw_new, o_cmp, o_sel, gates3)


def _sample_conv_body(st_ref, u_ref, w_ref, b_ref, g_ref, bn_ref, o_ref, *, bd):
    w = w_ref[...]
    y = jnp.sum(st_ref[...] * w[None, :CONV_W - 1, :], axis=1) + u_ref[0:bd, :] * w[CONV_W - 1:CONV_W, :] + b_ref[...]
    y = _ln_rows(y, g_ref[...], bn_ref[...])
    o_ref[...] = y * jax.nn.sigmoid(y)


def _sample_conv(state, layer, u, cw, cb, cg, cbn, *, bd):
    vec = lambda a: a.reshape(1, CONV_CH)
    cst = lambda i: (0, 0)
    return pl.pallas_call(
        functools.partial(_sample_conv_body, bd=bd), grid=(1,),
        in_specs=[pl.BlockSpec((None, bd, CONV_W - 1, CONV_CH), lambda i: (layer, 0, 0, 0)),
                  pl.BlockSpec((SAMPLE_ROWS, CONV_CH), cst), pl.BlockSpec((CONV_W, CONV_CH), cst),
                  pl.BlockSpec((1, CONV_CH), cst), pl.BlockSpec((1, CONV_CH), cst), pl.BlockSpec((1, CONV_CH), cst)],
        out_specs=pl.BlockSpec((bd, CONV_CH), cst),
        out_shape=jax.ShapeDtypeStruct((bd, CONV_CH), F32),
        compiler_params=_cparams("arbitrary"), name="sample_conv",
    )(state, u, cw, vec(cb), vec(cg), vec(cbn))


def _sample_odd_body(st_ref, pin_ref, u_ref, vn_ref, pw_ref, ps_ref, w0_ref, b0_ref, o_ref, *, bd, start_pos):
    pin = pin_ref[0:bd, :]
    st = st_ref[...]
    for g, w in enumerate(POOL_WINDOWS):
        sl = slice(g * POOL_GC, (g + 1) * POOL_GC)
        tot = pin[:, sl] + jnp.sum(st[:, POOL_STATE - (w - 1):, sl], axis=1)
        d = tot / float(min(w, start_pos + 1)) - pin[:, sl]
        dp = jnp.concatenate([d, jnp.zeros((SAMPLE_ROWS - bd, POOL_GC), F32)], axis=0).astype(BF16)
        y = jnp.dot(dp, pw_ref[g], preferred_element_type=F32)[0:bd]
        o_ref[:, sl] = y * ps_ref[:, sl]
    mixed = w0_ref[...] * vn_ref[0:bd, :] + b0_ref[...]
    o_ref[:, POOL_CH:] = u_ref[0:bd, :] * mixed


def _sample_odd_mix(state, layer, pin, u, vn, pool_w, pool_scale, sgu_w, sgu_b, *, bd, start_pos):
    w0 = jnp.repeat(sgu_w[:, 0, 0], SGU_GC).reshape(1, SGU_CH)
    b0 = jnp.repeat(sgu_b[:, 0], SGU_GC).reshape(1, SGU_CH)
    cst = lambda i: (0, 0)
    return pl.pallas_call(
        functools.partial(_sample_odd_body, bd=bd, start_pos=start_pos), grid=(1,),
        in_specs=[pl.BlockSpec((None, bd, POOL_STATE, POOL_CH), lambda i: (layer, 0, 0, 0)),
                  pl.BlockSpec((SAMPLE_ROWS, POOL_CH), cst), pl.BlockSpec((SAMPLE_ROWS, SGU_CH), cst),
                  pl.BlockSpec((SAMPLE_ROWS, SGU_CH), cst),
                  pl.BlockSpec((POOL_GROUPS, POOL_GC, POOL_GC), lambda i: (0, 0, 0)),
                  pl.BlockSpec((1, POOL_CH), cst), pl.BlockSpec((1, SGU_CH), cst), pl.BlockSpec((1, SGU_CH), cst)],
        out_specs=pl.BlockSpec((bd, D_MODEL), cst),
        out_shape=jax.ShapeDtypeStruct((bd, D_MODEL), F32),
        compiler_params=_cparams("arbitrary"), name="sample_pool_sgu",
    )(state, pin, u, vn, pool_w.astype(BF16), pool_scale.reshape(1, POOL_CH), w0, b0)


def _pad_rows(x, rows):
    return jnp.pad(x, ((0, rows - x.shape[0]), (0, 0)))


def _split_to_nat(x, nheads):
    y = _unpad_heads(x, nheads)
    return _pad_nat(y.reshape(y.shape[:-1] + (nheads, HEAD_DIM))).reshape(x.shape)


def _nat_to_split(x, nheads):
    xh = x.reshape(x.shape[:-1] + (nheads, HSLOT))[..., :HEAD_DIM]
    return _pad_head(xh).reshape(x.shape)


def _even_in_proj(xb, xsb, w_in, rope_p, rope_s, *, tm, bs, stack):
    nrep = rope_p[0].shape[0] // tm
    rs = xsb.shape[0]
    tab = lambda i, j: (i % nrep, 0)
    rope_ex = tuple((t, (tm, LANES), tab) for t in rope_p)
    rope_sx = tuple((t, (rs, LANES), lambda i, j: (0, 0)) for t in rope_s)
    gw = HPG * HSLOT
    qraw, qrot, qraw_s, qrot_s = _mm(
        xb, w_in, tm=tm, tn=gw, n_off=E_Q, n_cols=QW, epilogue=_ep_q, extras=rope_ex,
        outs=((QW, BF16, gw), (QW, BF16, gw)), name="even_in_q",
        side=(xsb, _ep_q, rope_sx, ((QW, BF16, gw), (QW, BF16, gw))))
    nst = bs[1] // tm
    e, n_even, t_prev = stack
    row = {"f32": (KVP, F32, KVP), "bf16": (KVP, BF16, KVP),
           "t": ((n_even, bs[0], KVW, bs[1]), F32, (None, None, KVW, tm), lambda i, j: (e, i // nst, 0, i % nst))}
    kv_p, kv_s = [], []
    for sec, (off, want) in enumerate(zip((E_KVC, E_KVS, E_KVW), (("f32", "t"), ("bf16", "t"), ("bf16", "t")))):
        res = _mm(xb, w_in, tm=tm, tn=KVP, n_off=off, n_cols=KVP,
                  epilogue=functools.partial(_ep_kv, rope=sec > 0, want=want),
                  extras=rope_ex if sec > 0 else (), outs=tuple(row[k] for k in want),
                  name=("even_in_kvc", "even_in_kvs", "even_in_kvw")[sec],
                  side=(xsb, functools.partial(_ep_kv, rope=sec > 0, want=("f32",)),
                        rope_sx if sec > 0 else (), (row["f32"],)),
                  alias=None if t_prev is None else (t_prev[sec], 1))
        kv_p.append(res[:2])
        kv_s.append(res[2])
    u, u_s = _mm(xb, w_in, tm=tm, tn=CONV_CH, n_off=E_GLU, n_cols=2 * CONV_CH, epilogue=_ep_glu,
                 outs=((CONV_CH, F32, CONV_CH // 2),), name="even_in_glu",
                 side=(xsb, _ep_glu, (), ((CONV_CH, F32, CONV_CH // 2),)))
    gates, gates_s = _mm(xb, w_in, tm=tm, tn=GATE_W, n_off=E_GATE, n_cols=GATE_W, epilogue=_ep_sigmoid,
                         outs=((GATE_W, F32, GATE_W),), name="even_in_gates",
                         side=(xsb, _ep_sigmoid, (), ((GATE_W, F32, GATE_W),)))
    return (qraw, qrot, kv_p, u, gates), (qraw_s, qrot_s, kv_s, u_s, gates_s)


def _mlp_up_cast(xb, xsb, w1, w2, layer, *, tm, tn):
    m, k = xb.shape
    rs = xsb.shape[0]
    dff = w1.shape[2]
    d_out = w2.shape[2]
    ni, nj = m // tm, dff // tn
    slab = dff // (ni * nj)
    assert m % tm == 0 and dff % tn == 0 and dff % (ni * nj) == 0 and slab % SAMPLE_ROWS == 0
    rsub = min(tm, ROW_SUB)

    def body(x_ref, xs_ref, w1_ref, w2_ref, h_ref, hs_ref, w2b_ref, w1b_ref):
        def act(rows_ref, rows):
            a = jnp.maximum(jnp.dot(rows_ref[rows, :], w1b_ref[...], preferred_element_type=F32), 0.0)
            return (a * a).astype(BF16)

        @pl.when(pl.program_id(1) == 0)
        def _():
            w1b_ref[...] = w1_ref[...].astype(BF16)
            hs_ref[...] = act(xs_ref, slice(0, rs))

        w2b_ref[...] = w2_ref[...].astype(BF16)
        for r in range(tm // rsub):
            rows = slice(r * rsub, (r + 1) * rsub)
            h_ref[rows, :] = act(x_ref, rows)

    return pl.pallas_call(
        body, grid=(nj, ni),
        in_specs=[pl.BlockSpec((tm, k), lambda j, i: (i, 0)), pl.BlockSpec((rs, k), lambda j, i: (0, 0)),
                  pl.BlockSpec((None, k, tn), lambda j, i: (layer, 0, j)),
                  pl.BlockSpec((None, slab, d_out), lambda j, i: (layer, j * ni + i, 0))],
        out_specs=[pl.BlockSpec((tm, tn), lambda j, i: (i, j)), pl.BlockSpec((rs, tn), lambda j, i: (0, j)),
                   pl.BlockSpec((slab, d_out), lambda j, i: (j * ni + i, 0))],
        out_shape=[jax.ShapeDtypeStruct((m, dff), BF16), jax.ShapeDtypeStruct((rs, dff), BF16),
                   jax.ShapeDtypeStruct((dff, d_out), BF16)],
        scratch_shapes=[pltpu.VMEM((k, tn), BF16)],
        compiler_params=_cparams("arbitrary", "arbitrary"), name="mlp_up_cast",
    )(xb, xsb, w1, w2)


def kernel(x_prompt, x_sample, cache_cmp_kv, cache_sel_kv, cache_win_kv, state_conv, state_pool, page_table,
           w_in_even, w_out_even, cmp_pe_k, cmp_pe_v, cmp_w_k, cmp_w_v, conv_w, conv_b, conv_ln_g, conv_ln_b,
           w_in_odd, w_out_odd, pool_w, pool_scale, sgu_ln_g, sgu_ln_b, sgu_w, sgu_b,
           mlp_w1, mlp_w2, ln_mix_g, ln_mix_b, ln_ffn_g, ln_ffn_b):
    B, S, D = x_prompt.shape
    Bd, Sd, _ = x_sample.shape
    n_pages = page_table.shape[1]
    past = n_pages * PAGE_SIZE
    n_even, n_pool = cache_cmp_kv.shape[:2]
    wb = cache_win_kv.shape[2]
    assert D == D_MODEL and Sd == 1 and Bd <= SAMPLE_ROWS
    assert S % 1024 == 0 and past % SEL_BLOCK == 0 and S >= WINDOW
    M = B * S
    Ms = SAMPLE_ROWS
    tm_p = 1024

    rope_p = _rope_tables(jnp.arange(S, dtype=jnp.int32))
    rope_s = _rope_tables(jnp.full((Ms,), past, jnp.int32))
    pps = min(32, n_pages)
    assert n_pages % pps == 0
    bsum = _block_sum_matrices(pps)
    cmp_t = cache_cmp_kv.transpose(0, 1, 3, 4, 5, 2).reshape(n_even * n_pool, KVW, PAGE_SIZE)
    sel_t = cache_sel_kv.transpose(0, 1, 3, 4, 5, 2).reshape(n_even * n_pool, 2, KV_HEADS, HEAD_DIM, PAGE_SIZE)
    win_t = cache_win_kv.transpose(0, 1, 3, 4, 5, 2).reshape(n_even * Bd, 2 * KV_HEADS, HEAD_DIM, wb)

    xp = x_prompt.reshape(M, D)
    xs = _pad_rows(x_sample.reshape(Bd, D), Ms)
    xpb, xsb = xp.astype(BF16), xs.astype(BF16)

    outs = {k: [] for k in ("cmp_s", "sel_s", "win_s", "conv_p", "conv_s", "pool_p", "pool_s", "sgu_p", "sgu_s")}
    kv6 = lambda a, lead: a.reshape(lead + (2, KV_HEADS, HEAD_DIM))
    kv_t = None

    for layer in range(DEPTH):
        if layer % 2 == 0:
            e = layer // 2
            wts = _prep_even_weights(w_in_even[e], w_out_even[e], cmp_pe_k[e], cmp_pe_v[e], cmp_w_k[e], cmp_w_v[e])
            prj_p, prj_s = _even_in_proj(xpb, xsb, wts["w_in"], rope_p, rope_s, tm=tm_p, bs=(B, S),
                                         stack=(e, n_even, kv_t))
            qraw, qrot, ((kvc, kvc_t), (kvs_b, kvs_t), (kvw_b, kvw_t)), u, gates = prj_p
            kv_t = (kvc_t, kvs_t, kvw_t)
            summ = _compress_rows(kvc, wts["pe"], rows=512, name="prompt_compress")
            (ckv,) = _mm(summ.astype(BF16), wts["big_p"], tm=min(summ.shape[0], 512), tn=KVP,
                         n_off=0, n_cols=KVP, epilogue=_ep_plain, outs=((KVP, F32, KVP),), name="prompt_compress_map")
            o_att = _prompt_attention(qraw, qrot, gates, ckv, kvs_b, kvw_b, batch=B, seq=S, tq=256, tk=256, ngrp=4)
            c = _prompt_conv(u, conv_w[e], conv_b[e], conv_ln_g[e], conv_ln_b[e], batch=B, seq=S, ts=256)
            outs["conv_p"].append(u.reshape(B, S, CONV_CH)[:, S - (CONV_W - 1):])
            qraw_s, qrot_s, (kvc_s, kvs_s, kvw_s), u_s, gates_s = prj_s
            ckv_t = _compress_pages(cmp_t, page_table, wts["pe_t"], wts["big_t"], bsum, e * n_pool, pps=pps,
                                    name="sample_compress")
            q3 = qraw_s.astype(F32)[:Bd].reshape(Bd, N_HEADS, HSLOT)
            o_cmp, idx = _sample_cmp(q3, ckv_t, bd=Bd, past=past, qpos=past, nbl=pps * (PAGE_SIZE // CMP_BLOCK))
            idx = idx[:, :, 0].reshape(Bd, KV_HEADS, TOP_K)
            qr3 = _split_to_nat(qrot_s.astype(F32)[:Bd], N_HEADS).reshape(Bd, N_HEADS, HSLOT)
            q4 = jnp.pad(qr3.reshape(Bd, KV_HEADS, HPG, HSLOT), ((0, 0), (0, 0), (0, Q_PAD_ROWS - HPG), (0, 0)))
            kvs_nat = _split_to_nat(kvs_s, 2 * KV_HEADS)
            kvw_nat = _split_to_nat(kvw_s, 2 * KV_HEADS)
            o_sel = _sample_sel(sel_t, page_table, idx, e * n_pool, q4, kvs_nat, bd=Bd, past=past)
            o_sel = o_sel[:, :, :HPG].reshape(Bd, N_HEADS, HSLOT)
            g3 = gates_s[:Bd].reshape(Bd, KV_HEADS, LANES)[:, :, :3 * HPG].reshape(Bd, KV_HEADS, 3, HPG)
            g3 = g3.transpose(0, 1, 3, 2).reshape(Bd, N_HEADS, 3)
            g3 = jnp.pad(g3, ((0, 0), (0, 0), (0, LANES - 3)))
            o_s = _sample_win(qr3, win_t, e, kvw_nat, o_cmp, o_sel, g3, bd=Bd, past=past, qpos=past)
            c_s = _sample_conv(state_conv, e, u_s, conv_w[e], conv_b[e], conv_ln_g[e], conv_ln_b[e], bd=Bd)
            o_sb = _pad_rows(_nat_to_split(o_s.reshape(Bd, QW), N_HEADS), Ms).astype(BF16)
            c_sb = _pad_rows(c_s, Ms).astype(BF16)
            xp, xpb, xs, xsb = _proj_ln([o_att, c], [wts["wo_att"], wts["wo_conv"]], xp, ln_mix_g[layer],
                                        ln_mix_b[layer], tm=512, name="even_out_ln", side=([o_sb, c_sb], xs))
            kvc_c = _unpad_heads(kvc_s[:Bd], 2 * KV_HEADS)
            kvs_c = _unpad_heads(kvs_s[:Bd], 2 * KV_HEADS)
            kvw_c = _unpad_heads(kvw_s[:Bd], 2 * KV_HEADS)
            outs["cmp_s"].append(kv6(kvc_c, (Bd, 1)))
            outs["sel_s"].append(kv6(kvs_c, (Bd, 1)))
            wkv = jnp.concatenate([cache_win_kv[e], kv6(kvw_c, (Bd, 1))], axis=1)
            outs["win_s"].append(wkv[:, wkv.shape[1] - min(WINDOW, wkv.shape[1]):])
            outs["conv_s"].append(jnp.concatenate([state_conv[e], u_s[:Bd, None, :]], axis=1)[:, 1:])
        else:
            o = layer // 2
            w_in = w_in_odd[o]
            w_in_p = jnp.concatenate([w_in[:, POOL_CH + SGU_CH:], w_in[:, :POOL_CH], w_in[:, POOL_CH:POOL_CH + SGU_CH]],
                                     axis=1).astype(BF16)
            w_out_p = w_out_odd[o].astype(BF16)
            lg, lb = sgu_ln_g[o].reshape(1, SGU_CH), sgu_ln_b[o].reshape(1, SGU_CH)

            gl_ex = ((lg, (1, 2 * SGU_GC), lambda i, j: (0, j)), (lb, (1, 2 * SGU_GC), lambda i, j: (0, j)))
            o_v, o_p, o_u = (SGU_CH, F32, 2 * SGU_GC), (POOL_CH, F32, POOL_CH), (SGU_CH, F32, POOL_CH)
            vn, vn_s = _mm(xpb, w_in_p, tm=tm_p, tn=2 * SGU_GC, n_off=O_V, n_cols=SGU_CH, epilogue=_ep_gelu_gln,
                           extras=gl_ex, outs=(o_v,), name="odd_in_v", side=(xsb, _ep_gelu_gln, gl_ex, (o_v,)))
            pin, pin_s = _mm(xpb, w_in_p, tm=tm_p, tn=POOL_CH, n_off=O_PIN, n_cols=POOL_CH, epilogue=_ep_plain,
                             outs=(o_p,), name="odd_in_pool", side=(xsb, _ep_plain, (), (o_p,)))
            uu, uu_s = _mm(xpb, w_in_p, tm=tm_p, tn=POOL_CH, n_off=O_U, n_cols=SGU_CH, epilogue=_ep_gelu,
                           outs=(o_u,), name="odd_in_u", side=(xsb, _ep_gelu, (), (o_u,)))
            cat = _prompt_odd_mix(pin, uu, vn, pool_w[o], pool_scale[o], sgu_w[o], sgu_b[o], batch=B, seq=S)
            outs["pool_p"].append(pin.reshape(B, S, POOL_CH)[:, S - POOL_STATE:])
            outs["sgu_p"].append(vn.reshape(B, S, SGU_CH)[:, ((S - 1) // CHUNK) * CHUNK:])
            cat_s = _sample_odd_mix(state_pool, o, pin_s, uu_s, vn_s, pool_w[o], pool_scale[o], sgu_w[o], sgu_b[o],
                                    bd=Bd, start_pos=past)
            xp, xpb, xs, xsb = _proj_ln([cat], [w_out_p], xp, ln_mix_g[layer], ln_mix_b[layer], tm=512,
                                        name="odd_out_ln", side=([_pad_rows(cat_s, Ms).astype(BF16)], xs))
            outs["pool_s"].append(jnp.concatenate([state_pool[o], pin_s[:Bd, None, :]], axis=1)[:, 1:])
            outs["sgu_s"].append(vn_s[:Bd, None, :])
        h, h_s, w2b = _mlp_up_cast(xpb, xsb, mlp_w1, mlp_w2, layer, tm=tm_p, tn=1024)
        xp, xpb, xs, xsb = _mlp2_ln(h, w2b, xp, ln_ffn_g[layer], ln_ffn_b[layer], tm=512, tn=512,
                                    name="mlp_down_ln", side=(h_s, xs))

    st = lambda k: jnp.stack(outs[k])
    rows_last = lambda a: a.reshape(n_even, B, 2, KV_HEADS, HEAD_DIM, a.shape[-1]).transpose(0, 1, 5, 2, 3, 4)
    kvc_t, kvs_t, kvw_t = kv_t
    return (xp.reshape(B, S, D), xs[:Bd].reshape(Bd, Sd, D),
            rows_last(kvc_t), st("cmp_s"), rows_last(kvs_t), st("sel_s"),
            rows_last(kvw_t[:, :, :, S - WINDOW:]), st("win_s"),
            st("conv_p"), st("conv_s"), st("pool_p"), st("pool_s"), st("sgu_p"), st("sgu_s"))
```

```python
import functools

import jax
import jax.numpy as jnp
from jax import lax
from jax.experimental import pallas as pl
from jax.experimental.pallas import tpu as pltpu

F32 = jnp.float32
BF16 = jnp.bfloat16

D_MODEL = 2048
DEPTH = 4
PAGE_SIZE = 128
N_HEADS = 16
HEAD_DIM = 96
KV_HEADS = 4
HPG = N_HEADS // KV_HEADS
ATT_W = N_HEADS * HEAD_DIM
KVW = 2 * KV_HEADS * HEAD_DIM
CMP_BLOCK = 32
SEL_BLOCK = 64
TOP_K = 16
WINDOW = 512
ROPE_THETA = 10000.0
SCALE = HEAD_DIM ** -0.5
LOG2E = 1.4426950408889634
FORCE = 1e9
NEG = -1e30
CONV_CH = D_MODEL // 4
CONV_W = 31
POOL_CH = D_MODEL // 4
POOL_WINDOWS = (2, 4, 8, 16)
POOL_GROUPS = len(POOL_WINDOWS)
POOL_GC = POOL_CH // POOL_GROUPS
POOL_STATE = max(POOL_WINDOWS) - 1
SGU_CH = D_MODEL - POOL_CH
SGU_GROUPS = 4
SGU_GC = SGU_CH // SGU_GROUPS
CHUNK = 128
D_FF = 4 * D_MODEL
ALPHA = (2 * DEPTH) ** 0.25
LN_EPS = 1e-5

LANES = 128
SUBLANES = 8
HALF = HEAD_DIM // 2
HSLOT = LANES
HALF_OFF = LANES // 2
QW = N_HEADS * HSLOT
KVP = 2 * KV_HEADS * HSLOT
GATE_W = KV_HEADS * LANES
SAMPLE_ROWS = 16
ROW_SUB = 256
LN_ROW_SUB = 128
VMEM_LIMIT = 52 * 1024 * 1024

E_Q, E_KVC, E_KVS, E_KVW = 0, QW, QW + KVP, QW + 2 * KVP
E_GLU = QW + 3 * KVP
E_GATE = E_GLU + 2 * CONV_CH
E_TOT = E_GATE + GATE_W
O_V, O_U, O_PIN = 0, SGU_CH, 2 * SGU_CH


def _cparams(*sem):
    return pltpu.CompilerParams(dimension_semantics=sem, vmem_limit_bytes=VMEM_LIMIT)


def _pad_head(x):
    halves = x.reshape(x.shape[:-1] + (2, HALF))
    halves = jnp.pad(halves, [(0, 0)] * (halves.ndim - 1) + [(0, HALF_OFF - HALF)])
    return halves.reshape(x.shape[:-1] + (HSLOT,))


def _pad_nat(x):
    return jnp.concatenate([x, jnp.zeros(x.shape[:-1] + (HSLOT - HEAD_DIM,), x.dtype)], axis=-1)


def _unpad_heads(x, nheads):
    xh = x.reshape(x.shape[:-1] + (nheads, HSLOT))
    y = jnp.concatenate([xh[..., :HALF], xh[..., HALF_OFF:HALF_OFF + HALF]], axis=-1)
    return y.reshape(x.shape[:-1] + (nheads * HEAD_DIM,))


def _rope_tables(pos):
    inv = jnp.power(ROPE_THETA, -jnp.arange(HALF, dtype=F32) / HALF)
    ang = pos.astype(F32)[:, None] * inv[None, :]
    cos, sin = jnp.cos(ang), jnp.sin(ang)
    z = jnp.zeros((pos.shape[0], HALF_OFF - HALF), F32)
    return (jnp.concatenate([cos, z, cos, z], axis=1),
            jnp.concatenate([-sin, z, sin, z], axis=1))


def _block_diag2(a, b):
    za = jnp.zeros((a.shape[0], b.shape[1]), a.dtype)
    zb = jnp.zeros((b.shape[0], a.shape[1]), a.dtype)
    return jnp.concatenate([jnp.concatenate([a, za], axis=1), jnp.concatenate([zb, b], axis=1)], axis=0)


def _prep_even_weights(w_in, w_out, pe_k, pe_v, w_ck, w_cv):
    d = w_in.shape[0]
    wb = w_in.astype(BF16)
    g0 = ATT_W + 3 * KVW
    n_heads_all = g0 // HEAD_DIM
    qkv = _pad_head(wb[:, :g0].reshape(d, n_heads_all, HEAD_DIM)).reshape(d, n_heads_all * HSLOT)
    gates = wb[:, g0:g0 + 3 * N_HEADS].reshape(d, 3, KV_HEADS, HPG).transpose(0, 2, 1, 3)
    gates = gates.reshape(d, KV_HEADS, 3 * HPG)
    gates = jnp.pad(gates, ((0, 0), (0, 0), (0, LANES - 3 * HPG))).reshape(d, GATE_W)
    w_in_p = jnp.concatenate([qkv, wb[:, g0 + 3 * N_HEADS:], gates], axis=1)
    wo = w_out.astype(BF16)
    wo_att = jnp.pad(wo[:ATT_W].reshape(2 * N_HEADS, HALF, d), ((0, 0), (0, HALF_OFF - HALF), (0, 0)))
    wo_att = wo_att.reshape(QW, d)
    wo_conv = wo[ATT_W:]
    eye = jnp.eye(KV_HEADS, dtype=F32)
    pe = jnp.concatenate([jnp.tile(_pad_head(pe_k), (1, KV_HEADS)), jnp.tile(_pad_head(pe_v), (1, KV_HEADS))], axis=1)
    wk_full = _pad_head(_pad_head(w_ck).T).T
    wv_full = _pad_head(_pad_head(w_cv).T).T
    big_p = _block_diag2(jnp.kron(eye, wk_full), jnp.kron(eye, wv_full))
    pe_t = jnp.concatenate([jnp.tile(jnp.tile(pe_k.T, (1, PAGE_SIZE // CMP_BLOCK)), (KV_HEADS, 1)),
                            jnp.tile(jnp.tile(pe_v.T, (1, PAGE_SIZE // CMP_BLOCK)), (KV_HEADS, 1))], axis=0)
    big_t = _block_diag2(jnp.kron(eye, _pad_head(w_ck).T), jnp.kron(eye, _pad_nat(w_cv).T))
    return dict(w_in=w_in_p, wo_att=wo_att, wo_conv=wo_conv, pe=pe, pe_t=pe_t,
                big_p=big_p.astype(BF16), big_t=big_t.astype(BF16))


def _block_sum_matrices(pps):
    bpp = PAGE_SIZE // CMP_BLOCK
    nbl = bpp * pps
    p = jnp.arange(pps)[:, None, None]
    i = (jnp.arange(PAGE_SIZE) // CMP_BLOCK)[None, :, None]
    c = jnp.arange(nbl)[None, None, :]
    col = (i % 2) * (nbl // 2) + (bpp // 2) * p + i // 2
    return jnp.where(c == col, 1.0 / CMP_BLOCK, 0.0).astype(BF16)


def _ln_rows(y, g, b):
    mu = jnp.mean(y, axis=-1, keepdims=True)
    yc = y - mu
    var = jnp.mean(yc * yc, axis=-1, keepdims=True)
    return yc * lax.rsqrt(var + LN_EPS) * g + b


def _rope_slot(x, cos, sin):
    return x * cos + pltpu.roll(x, HALF_OFF, axis=1) * sin


_NT = (((1,), (1,)), ((), ()))


def _pad_rows_to(x, rows):
    return jnp.concatenate([x, jnp.zeros((rows - x.shape[0],) + x.shape[1:], x.dtype)], axis=0)


def _mm(x, w, *, tm, tn, n_off, n_cols, epilogue, extras=(), outs, name, side=None, alias=None):
    m, k = x.shape
    assert m % tm == 0 and n_cols % tn == 0 and n_off % tn == 0
    joff = n_off // tn
    ji = lambda im: (lambda j, i: im(i, j))
    in_specs = [pl.BlockSpec((tm, k), lambda j, i: (i, 0)),
                pl.BlockSpec((k, tn), lambda j, i: (0, joff + j))]
    in_specs += [pl.BlockSpec(bs, ji(im)) for _, bs, im in extras]
    operands = [x, w] + [a for a, _, _ in extras]
    out_shape, out_specs = [], []
    for o in outs:
        if len(o) == 3:
            out_shape.append(jax.ShapeDtypeStruct((m, o[0]), o[1]))
            out_specs.append(pl.BlockSpec((tm, o[2]), lambda j, i: (i, j)))
        else:
            out_shape.append(jax.ShapeDtypeStruct(o[0], o[1]))
            out_specs.append(pl.BlockSpec(o[2], ji(o[3])))
    ne, no = len(extras), len(outs)
    nse = 0
    if side is not None:
        xs, s_epilogue, s_extras, s_outs = side
        rs = xs.shape[0]
        nse = len(s_extras)
        in_specs += [pl.BlockSpec((rs, k), lambda j, i: (0, 0))]
        in_specs += [pl.BlockSpec(bs, ji(im)) for _, bs, im in s_extras]
        operands += [xs] + [a for a, _, _ in s_extras]
        for cols, dt, bc in s_outs:
            out_shape.append(jax.ShapeDtypeStruct((rs, cols), dt))
            out_specs.append(pl.BlockSpec((rs, bc), lambda j, i: (0, j)))
    rsub = min(tm, ROW_SUB)
    n_in = len(operands)
    io_alias = {}
    if alias is not None:
        io_alias = {n_in: alias[1]}
        in_specs.append(pl.BlockSpec(memory_space=pl.ANY))
        operands.append(alias[0])

    def body(*refs):
        x_ref, w_ref = refs[:2]
        ex = refs[2:2 + ne]
        o_refs = refs[len(operands):len(operands) + no]
        for r in range(tm // rsub):
            rows = slice(r * rsub, (r + 1) * rsub)
            acc = jnp.dot(x_ref[rows, :], w_ref[...], preferred_element_type=F32)
            epilogue(acc, ex, o_refs, rows)
        if side is not None:
            @pl.when(pl.program_id(1) == 0)
            def _():
                acc = jnp.dot(refs[2 + ne][...], w_ref[...], preferred_element_type=F32)
                s_epilogue(acc, refs[3 + ne:n_in], refs[len(operands) + no:], slice(0, rs))

    return pl.pallas_call(
        body, grid=(n_cols // tn, m // tm), in_specs=in_specs, out_specs=out_specs, out_shape=out_shape,
        input_output_aliases=io_alias, compiler_params=_cparams("arbitrary", "arbitrary"), name=name,
    )(*operands)


def _ep_q(acc, ex, outs, rows):
    cos, sin = ex[0][rows, :], ex[1][rows, :]
    outs[0][rows, :] = acc.astype(BF16)
    for j in range(acc.shape[1] // HSLOT):
        sl = slice(j * HSLOT, (j + 1) * HSLOT)
        outs[1][rows, sl] = _rope_slot(acc[:, sl], cos, sin).astype(BF16)


def _ep_kv(acc, ex, outs, rows, *, rope, want):
    o = dict(zip(want, outs))
    if rope:
        cos, sin = ex[0][rows, :], ex[1][rows, :]
    for j in range(2 * KV_HEADS):
        sl = slice(j * HSLOT, (j + 1) * HSLOT)
        x = acc[:, sl]
        if rope and j < KV_HEADS:
            x = _rope_slot(x, cos, sin)
        if "f32" in o:
            o["f32"][rows, sl] = x
        if "bf16" in o:
            o["bf16"][rows, sl] = x.astype(BF16)
        if "t" in o:
            xt = x.T
            o["t"][j * HEAD_DIM:j * HEAD_DIM + HALF, rows] = xt[0:HALF]
            o["t"][j * HEAD_DIM + HALF:(j + 1) * HEAD_DIM, rows] = xt[HALF_OFF:HALF_OFF + HALF]


def _ep_glu(acc, ex, outs, rows):
    h = acc.shape[1] // 2
    outs[0][rows, :] = acc[:, :h] * jax.nn.sigmoid(acc[:, h:])


def _ep_sigmoid(acc, ex, outs, rows):
    outs[0][rows, :] = jax.nn.sigmoid(acc)


def _ep_plain(acc, ex, outs, rows):
    outs[0][rows, :] = acc.astype(outs[0].dtype)


def _ep_relu2(acc, ex, outs, rows):
    r = jnp.maximum(acc, 0.0)
    outs[0][rows, :] = (r * r).astype(outs[0].dtype)


def _ep_gelu(acc, ex, outs, rows):
    outs[0][rows, :] = jax.nn.gelu(acc)


def _ep_gelu_gln(acc, ex, outs, rows):
    g, b = ex[0][...], ex[1][...]
    v = jax.nn.gelu(acc)
    for j in range(acc.shape[1] // SGU_GC):
        sl = slice(j * SGU_GC, (j + 1) * SGU_GC)
        outs[0][rows, sl] = _ln_rows(v[:, sl], g[:, sl], b[:, sl])


def _proj_ln(a_list, w_list, resid, g, b, *, tm, name, side):
    m, n = resid.shape
    as_list, resid_s = side
    rs = resid_s.shape[0]
    npair = len(a_list)
    in_specs = []
    for a in a_list:
        in_specs.append(pl.BlockSpec((tm, a.shape[1]), lambda i: (i, 0)))
    for w in w_list:
        in_specs.append(pl.BlockSpec(w.shape, lambda i: (0, 0), pipeline_mode=pl.Buffered(1)))
    in_specs += [pl.BlockSpec((tm, n), lambda i: (i, 0)),
                 pl.BlockSpec((1, n), lambda i: (0, 0)), pl.BlockSpec((1, n), lambda i: (0, 0))]
    for a in as_list:
        in_specs.append(pl.BlockSpec((rs, a.shape[1]), lambda i: (0, 0)))
    in_specs.append(pl.BlockSpec((rs, n), lambda i: (0, 0)))

    def body(*refs):
        a_refs, w_refs = refs[:npair], refs[npair:2 * npair]
        r_ref, g_ref, b_ref = refs[2 * npair:2 * npair + 3]
        as_refs = refs[2 * npair + 3:3 * npair + 3]
        rs_ref, o_ref, ob_ref, os_ref, osb_ref = refs[3 * npair + 3:]

        def rows_out(a_rs, res, rows):
            acc = ALPHA * res[rows, :]
            for a_ref, w_ref in zip(a_rs, w_refs):
                acc = acc + jnp.dot(a_ref[rows, :], w_ref[...], preferred_element_type=F32)
            return _ln_rows(acc, g_ref[...], b_ref[...])

        rsub = min(tm, LN_ROW_SUB)
        for r in range(tm // rsub):
            rows = slice(r * rsub, (r + 1) * rsub)
            y = rows_out(a_refs, r_ref, rows)
            o_ref[rows, :] = y
            ob_ref[rows, :] = y.astype(BF16)
        @pl.when(pl.program_id(0) == 0)
        def _():
            ys = rows_out(as_refs, rs_ref, slice(0, rs))
            os_ref[...] = ys
            osb_ref[...] = ys.astype(BF16)

    row = pl.BlockSpec((tm, n), lambda i: (i, 0))
    srow = pl.BlockSpec((rs, n), lambda i: (0, 0))
    return pl.pallas_call(
        body, grid=(m // tm,), in_specs=in_specs, out_specs=[row, row, srow, srow],
        out_shape=[jax.ShapeDtypeStruct((m, n), F32), jax.ShapeDtypeStruct((m, n), BF16),
                   jax.ShapeDtypeStruct((rs, n), F32), jax.ShapeDtypeStruct((rs, n), BF16)],
        compiler_params=_cparams("arbitrary"), name=name,
    )(*a_list, *w_list, resid, g.reshape(1, n), b.reshape(1, n), *as_list, resid_s)


def _mlp2_ln(h, w2, resid, g, b, *, tm, tn, name, side):
    m, kf = h.shape
    n = w2.shape[1]
    nj = n // tn
    hs, resid_s = side
    rs = hs.shape[0]
    rsub, rsub_ln = min(tm, ROW_SUB), min(tm, LN_ROW_SUB)

    def body(h_ref, w_ref, r_ref, g_ref, b_ref, hs_ref, rs_ref, o_ref, ob_ref, os_ref, osb_ref):
        j = pl.program_id(1)
        first = pl.program_id(0) == 0
        srows = slice(0, rs)

        def tile(hr, rr, rows):
            return ALPHA * rr[rows, :] + jnp.dot(hr[rows, :], w_ref[...], preferred_element_type=F32)

        def finish(o, ob, y, rows):
            y = _ln_rows(y, g_ref[...], b_ref[...])
            o[rows, :] = y
            ob[rows, :] = y.astype(BF16)

        for jj in range(nj - 1):
            @pl.when(j == jj)
            def _(jj=jj):
                cols = slice(jj * tn, (jj + 1) * tn)
                for r in range(tm // rsub):
                    rows = slice(r * rsub, (r + 1) * rsub)
                    o_ref[rows, cols] = tile(h_ref, r_ref, rows)

                @pl.when(first)
                def _():
                    os_ref[:, cols] = tile(hs_ref, rs_ref, srows)

        @pl.when(j == nj - 1)
        def _():
            done = slice(0, (nj - 1) * tn)
            for r in range(tm // rsub_ln):
                rows = slice(r * rsub_ln, (r + 1) * rsub_ln)
                finish(o_ref, ob_ref, jnp.concatenate([o_ref[rows, done], tile(h_ref, r_ref, rows)], axis=1), rows)

            @pl.when(first)
            def _():
                finish(os_ref, osb_ref, jnp.concatenate([os_ref[:, done], tile(hs_ref, rs_ref, srows)], axis=1), srows)

    assert nj > 1
    row = pl.BlockSpec((tm, n), lambda i, j: (i, 0))
    srow = pl.BlockSpec((rs, n), lambda i, j: (0, 0))
    return pl.pallas_call(
        body, grid=(m // tm, nj),
        in_specs=[pl.BlockSpec((tm, kf), lambda i, j: (i, 0)), pl.BlockSpec((kf, tn), lambda i, j: (0, j)),
                  pl.BlockSpec((tm, tn), lambda i, j: (i, j)),
                  pl.BlockSpec((1, n), lambda i, j: (0, 0)), pl.BlockSpec((1, n), lambda i, j: (0, 0)),
                  pl.BlockSpec((rs, kf), lambda i, j: (0, 0)), pl.BlockSpec((rs, tn), lambda i, j: (0, j))],
        out_specs=[row, row, srow, srow],
        out_shape=[jax.ShapeDtypeStruct((m, n), F32), jax.ShapeDtypeStruct((m, n), BF16),
                   jax.ShapeDtypeStruct((rs, n), F32), jax.ShapeDtypeStruct((rs, n), BF16)],
        compiler_params=_cparams("arbitrary", "arbitrary"), name=name,
    )(h, w2, resid, g.reshape(1, n), b.reshape(1, n), hs, resid_s)


def _compress_rows(kvc, pe, *, rows, name):
    m, c = kvc.shape
    nb = rows // CMP_BLOCK

    def body(x_ref, pe_ref, o_ref):
        x = x_ref[...].reshape(nb, CMP_BLOCK, c) * pe_ref[...][None]
        o_ref[...] = jnp.sum(x, axis=1) * (1.0 / CMP_BLOCK)

    return pl.pallas_call(
        body, grid=(m // rows,),
        in_specs=[pl.BlockSpec((rows, c), lambda i: (i, 0)), pl.BlockSpec((CMP_BLOCK, c), lambda i: (0, 0))],
        out_specs=pl.BlockSpec((nb, c), lambda i: (i, 0)),
        out_shape=jax.ShapeDtypeStruct((m // CMP_BLOCK, c), F32),
        compiler_params=_cparams("parallel"), name=name,
    )(kvc, pe)


def _compress_pages(pages_t, page_table, pe_t, big_t, bsum, layer_base, *, pps, name):
    bd, n_pages = page_table.shape
    bpp = PAGE_SIZE // CMP_BLOCK
    nbl = bpp * pps
    nsteps = n_pages // pps

    def body(pt_ref, *refs):
        page_refs = refs[:pps]
        pe_ref, big_ref, bsum_ref, o_ref = refs[pps:]
        acc = jnp.zeros((KVW, nbl), F32)
        for p in range(pps):
            x = page_refs[p][...] * pe_ref[...]
            acc = acc + jnp.dot(x.astype(BF16), bsum_ref[p], preferred_element_type=F32)
        o_ref[...] = jnp.dot(big_ref[...], acc.astype(BF16), preferred_element_type=F32)

    def page_map(p):
        return lambda b, j, pt: (layer_base + pt[b, j * pps + p], 0, 0)

    cst2 = lambda b, j, pt: (0, 0)
    grid_spec = pltpu.PrefetchScalarGridSpec(
        num_scalar_prefetch=1, grid=(bd, nsteps),
        in_specs=[pl.BlockSpec((None, KVW, PAGE_SIZE), page_map(p)) for p in range(pps)]
        + [pl.BlockSpec((KVW, PAGE_SIZE), cst2), pl.BlockSpec((KVP, KVW), cst2),
           pl.BlockSpec((pps, PAGE_SIZE, nbl), lambda b, j, pt: (0, 0, 0))],
        out_specs=pl.BlockSpec((None, KVP, nbl), lambda b, j, pt: (b, 0, j)))
    return pl.pallas_call(
        body, grid_spec=grid_spec,
        out_shape=jax.ShapeDtypeStruct((bd, KVP, n_pages * bpp), F32),
        compiler_params=_cparams("parallel", "arbitrary"), name=name,
    )(page_table, *([pages_t] * pps), pe_t, big_t, bsum)


def _flash_step_t(q, k, vt, valid, m_ref, l_ref, acc_ref):
    bias = jnp.where(valid, 0.0, NEG)
    s = jnp.dot(k, q, preferred_element_type=F32) + jnp.concatenate([bias] * HPG, axis=1)
    m_prev = m_ref[...]
    m_new = jnp.maximum(m_prev, jnp.max(s, axis=0, keepdims=True))
    alpha = jnp.exp2(m_prev - m_new)
    p = jnp.exp2(s - m_new)
    l_ref[...] = alpha * l_ref[...] + jnp.sum(p, axis=0, keepdims=True)
    acc_ref[...] = alpha * acc_ref[...] + jnp.dot(vt, p.astype(BF16), preferred_element_type=F32)
    m_ref[...] = m_new


def _attn_body(*refs, tq, tk, seq, k_top, ngrp):
    n_in = 9
    ins = [refs[g * n_in:(g + 1) * n_in] for g in range(ngrp)]
    o_ref = refs[ngrp * n_in]
    m_ref, l_ref, acc_ref, sel_ref, vts_ref, vtw_ref = refs[ngrp * n_in + 1:]
    i = pl.program_id(2)
    cols = HPG * tq
    nsb = seq // SEL_BLOCK
    nt = seq // tk
    bpt = tk // SEL_BLOCK
    q0 = i * tq
    lane_q = lax.broadcasted_iota(jnp.int32, (1, cols), 1)
    qpos = q0 + (lane_q & (tq - 1))
    qp1 = q0 + lax.broadcasted_iota(jnp.int32, (1, tq), 1)

    @pl.when(i == 0)
    def _():
        def tr(t, c):
            for g in range(ngrp):
                vs_ref, vw_ref = ins[g][6], ins[g][8]
                for h in range(tk // LANES):
                    k0 = pl.multiple_of(t * tk + h * LANES, LANES)
                    hs = slice(h * LANES, (h + 1) * LANES)
                    vts_ref[g, t, :, hs] = vs_ref[pl.ds(k0, LANES), :].astype(F32).T.astype(BF16)
                    vtw_ref[g, t, :, hs] = vw_ref[pl.ds(k0, LANES), :].astype(F32).T.astype(BF16)
            return c
        lax.fori_loop(0, nt, tr, 0)

    def heads_t(ref, scale=1.0):
        parts = [ref[:, j * HSLOT:(j + 1) * HSLOT].astype(F32).T * scale for j in range(HPG)]
        return jnp.concatenate(parts, axis=1).astype(BF16)

    r = lax.broadcasted_iota(jnp.int32, (2 * nsb, 1), 0)
    n_of = jnp.where(r < nsb, 2 * r, 2 * (r - nsb) + 1)
    mk = ((n_of + 1) * CMP_BLOCK - 1) <= qpos
    sb = lax.broadcasted_iota(jnp.int32, (nsb, 1), 0)
    vis = (sb * SEL_BLOCK) <= qp1
    cur = sb == (qp1 >> (SEL_BLOCK.bit_length() - 1))

    def compressed_branch(g):
        qraw_ref, ck_ref, cv_ref = ins[g][0], ins[g][3], ins[g][4]
        qr = heads_t(qraw_ref)
        ck = jnp.concatenate([ck_ref[pl.ds(0, nsb, stride=2), :], ck_ref[pl.ds(1, nsb, stride=2), :]], axis=0)
        s = jnp.dot(ck.astype(BF16), qr, preferred_element_type=F32) * SCALE
        s = jnp.where(mk, s, NEG)
        mx = jnp.max(s, axis=0, keepdims=True)
        p = jnp.where(mk, jnp.exp(s - mx), 0.0)
        pn = p * (1.0 / jnp.maximum(jnp.sum(p, axis=0, keepdims=True), 1e-30))
        cv = jnp.concatenate([cv_ref[pl.ds(0, nsb, stride=2), :], cv_ref[pl.ds(1, nsb, stride=2), :]], axis=0)
        cvt = _pad_rows_to(cv, LANES).T.astype(BF16)
        o_cmp = jnp.dot(cvt, _pad_rows_to(pn, LANES).astype(BF16), preferred_element_type=F32)
        pp = pn[0:nsb] + pn[nsb:2 * nsb]
        imp = pp[:, 0:tq]
        for j in range(1, HPG):
            imp = imp + pp[:, j * tq:(j + 1) * tq]
        imp = jnp.where(cur, FORCE, jnp.where(vis, imp, -FORCE))
        cnt = jnp.zeros((nsb, tq), F32)
        for j in range(nsb):
            rowj = imp[j:j + 1, :]
            beats = (rowj > imp) | ((rowj == imp) & (j < sb))
            cnt = cnt + jnp.where(beats, 1.0, 0.0)
        sel = jnp.where(cnt < k_top, 1.0, 0.0)
        for t in range(nt):
            sel_ref[g, t, 0:bpt, :] = sel[bpt * t:bpt * (t + 1), :]
        return o_cmp

    o_cmp = [compressed_branch(g) for g in range(ngrp)]
    qt = [heads_t(ins[g][1], SCALE * LOG2E) for g in range(ngrp)]
    rowk = lax.broadcasted_iota(jnp.int32, (tk, 1), 0)

    def reset():
        m_ref[...] = jnp.full((ngrp, 1, cols), NEG, F32)
        l_ref[...] = jnp.zeros((ngrp, 1, cols), F32)
        acc_ref[...] = jnp.zeros((ngrp, HSLOT, cols), F32)

    def result(g):
        return acc_ref[g] * (1.0 / jnp.maximum(l_ref[g], 1e-30))

    reset()

    def sel_step(t, c):
        k0 = pl.multiple_of(t * tk, tk)
        kp = k0 + rowk
        for g in range(ngrp):
            sm = sel_ref[g, t, bpt - 1:bpt, :]
            for j in range(bpt - 2, -1, -1):
                sm = jnp.where(rowk < (j + 1) * SEL_BLOCK, sel_ref[g, t, j:j + 1, :], sm)
            valid = (kp <= qp1) & (sm > 0.5)
            _flash_step_t(qt[g], ins[g][5][pl.ds(k0, tk), :], vts_ref[g, t], valid,
                          m_ref.at[g], l_ref.at[g], acc_ref.at[g])
        return c

    t_end = lax.div(q0 + tq + tk - 1, tk)
    lax.fori_loop(0, t_end, sel_step, 0)
    o_sel = [result(g) for g in range(ngrp)]

    reset()

    def win_step(t, c):
        k0 = pl.multiple_of(t * tk, tk)
        kp = k0 + rowk
        valid = (kp <= qp1) & (kp >= qp1 - WINDOW)
        for g in range(ngrp):
            _flash_step_t(qt[g], ins[g][7][pl.ds(k0, tk), :], vtw_ref[g, t], valid,
                          m_ref.at[g], l_ref.at[g], acc_ref.at[g])
        return c

    lax.fori_loop(lax.div(jnp.maximum(q0 - WINDOW, 0), tk), t_end, win_step, 0)

    gw = HPG * HSLOT
    for g in range(ngrp):
        gt = ins[g][2][...].T
        o = jnp.zeros((HSLOT, cols), F32)
        for br, o_br in enumerate((o_cmp[g], o_sel[g], result(g))):
            grow = jnp.concatenate([gt[br * HPG + j:br * HPG + j + 1, :] for j in range(HPG)], axis=1)
            o = o + grow * o_br
        for j in range(HPG):
            o_ref[:, g * gw + j * HSLOT:g * gw + (j + 1) * HSLOT] = o[:, j * tq:(j + 1) * tq].T.astype(BF16)


def _prompt_attention(qraw, qrot, gates, ckv, kvs_b, kvw_b, *, batch, seq, tq, tk, ngrp):
    nq = seq // tq
    ncb = seq // CMP_BLOCK
    nt = seq // tk
    cols = HPG * tq
    k_top = min(TOP_K, seq // SEL_BLOCK)
    assert tq & (tq - 1) == 0 and seq % tk == 0 and tk % LANES == 0 and tk // SEL_BLOCK <= SUBLANES
    assert KV_HEADS % ngrp == 0
    body = functools.partial(_attn_body, tq=tq, tk=tk, seq=seq, k_top=k_top, ngrp=ngrp)
    gw = HPG * HSLOT
    in_specs, operands = [], []
    for g in range(ngrp):
        qmap = functools.partial(lambda b, p, i, g: (b * nq + i, p * ngrp + g), g=g)
        kmap = functools.partial(lambda b, p, i, g: (b, p * ngrp + g), g=g)
        vmap_ = functools.partial(lambda b, p, i, g: (b, KV_HEADS + p * ngrp + g), g=g)
        in_specs += [pl.BlockSpec((tq, gw), qmap), pl.BlockSpec((tq, gw), qmap), pl.BlockSpec((tq, LANES), qmap),
                     pl.BlockSpec((ncb, HSLOT), kmap), pl.BlockSpec((ncb, HSLOT), vmap_),
                     pl.BlockSpec((seq, HSLOT), kmap), pl.BlockSpec((seq, HSLOT), vmap_),
                     pl.BlockSpec((seq, HSLOT), kmap), pl.BlockSpec((seq, HSLOT), vmap_)]
        operands += [qraw, qrot, gates, ckv, ckv, kvs_b, kvs_b, kvw_b, kvw_b]
    return pl.pallas_call(
        body, grid=(batch, KV_HEADS // ngrp, nq), in_specs=in_specs,
        out_specs=pl.BlockSpec((tq, ngrp * gw), lambda b, p, i: (b * nq + i, p)),
        out_shape=jax.ShapeDtypeStruct((batch * seq, QW), BF16),
        scratch_shapes=[pltpu.VMEM((ngrp, 1, cols), F32), pltpu.VMEM((ngrp, 1, cols), F32),
                        pltpu.VMEM((ngrp, HSLOT, cols), F32), pltpu.VMEM((ngrp, nt, SUBLANES, tq), F32),
                        pltpu.VMEM((ngrp, nt, HSLOT, tk), BF16), pltpu.VMEM((ngrp, nt, HSLOT, tk), BF16)],
        compiler_params=_cparams("parallel", "parallel", "arbitrary"), name="prompt_attention",
    )(*operands)


CONV_HALO = 32


def _conv_body(cur_ref, prev_ref, w_ref, b_ref, g_ref, bn_ref, o_ref, ext_ref, *, ts):
    c = pl.program_id(1)
    span = CONV_HALO + ts - SUBLANES
    ext_ref[0, 0:CONV_HALO, :] = jnp.where(c > 0, prev_ref[...], 0.0)
    ext_ref[0, CONV_HALO:CONV_HALO + ts, :] = cur_ref[...]
    for s in range(1, SUBLANES):
        ext_ref[s, 0:span, :] = ext_ref[0, pl.ds(s, span), :]
    acc = jnp.zeros((ts, CONV_CH), F32) + b_ref[...]
    off = CONV_HALO - (CONV_W - 1)
    for k in range(CONV_W):
        s = (off + k) % SUBLANES
        acc = acc + ext_ref[s, pl.ds(off + k - s, ts), :] * w_ref[k:k + 1, :]
    y = _ln_rows(acc, g_ref[...], bn_ref[...])
    o_ref[...] = (y * jax.nn.sigmoid(y)).astype(BF16)


def _prompt_conv(u, cw, cb, cg, cbn, *, batch, seq, ts):
    nt = seq // ts
    r = ts // CONV_HALO
    cwp = jnp.pad(cw, ((0, CONV_HALO - CONV_W), (0, 0)))
    vec = lambda a: a.reshape(1, CONV_CH)
    cst = lambda b, c: (0, 0)
    return pl.pallas_call(
        functools.partial(_conv_body, ts=ts), grid=(batch, nt),
        in_specs=[pl.BlockSpec((ts, CONV_CH), lambda b, c: (b * nt + c, 0)),
                  pl.BlockSpec((CONV_HALO, CONV_CH), lambda b, c: (jnp.maximum((b * nt + c) * r - 1, 0), 0)),
                  pl.BlockSpec((CONV_HALO, CONV_CH), cst),
                  pl.BlockSpec((1, CONV_CH), cst), pl.BlockSpec((1, CONV_CH), cst), pl.BlockSpec((1, CONV_CH), cst)],
        out_specs=pl.BlockSpec((ts, CONV_CH), lambda b, c: (b * nt + c, 0)),
        out_shape=jax.ShapeDtypeStruct((batch * seq, CONV_CH), BF16),
        scratch_shapes=[pltpu.VMEM((SUBLANES, CONV_HALO + ts, CONV_CH), F32)],
        compiler_params=_cparams("parallel", "arbitrary"), name="prompt_conv",
    )(u, u, cwp, vec(cb), vec(cg), vec(cbn))


POOL_HALO = 16


def _odd_mix_body(pin_ref, prev_ref, u_ref, vn_ref, pw_ref, ps_ref, sw_ref, sb_ref, o_ref, ext_ref):
    c = pl.program_id(1)
    ext_ref[0:POOL_HALO, :] = jnp.where(c > 0, prev_ref[...], 0.0)
    ext_ref[POOL_HALO:POOL_HALO + CHUNK, :] = pin_ref[...]
    t = c * CHUNK + lax.broadcasted_iota(jnp.int32, (CHUNK, 1), 0)
    for g, w in enumerate(POOL_WINDOWS):
        sl = slice(g * POOL_GC, (g + 1) * POOL_GC)
        tot = ext_ref[pl.ds(POOL_HALO, CHUNK), sl]
        for j in range(1, w):
            tot = tot + ext_ref[pl.ds(POOL_HALO - j, CHUNK), sl]
        cnt = jnp.minimum(w, t + 1).astype(F32)
        d = tot / cnt - pin_ref[:, sl]
        y = jnp.dot(d.astype(BF16), pw_ref[g], preferred_element_type=F32)
        o_ref[:, sl] = (y * ps_ref[:, sl]).astype(BF16)
    ri = lax.broadcasted_iota(jnp.int32, (CHUNK, CHUNK), 0)
    ci = lax.broadcasted_iota(jnp.int32, (CHUNK, CHUNK), 1)
    for g in range(SGU_GROUPS):
        sl = slice(g * SGU_GC, (g + 1) * SGU_GC)
        ws = jnp.where(ci <= ri, sw_ref[g], 0.0).astype(BF16)
        mixed = jnp.dot(ws, vn_ref[:, sl].astype(BF16), preferred_element_type=F32) + sb_ref[:, g:g + 1]
        o_ref[:, POOL_CH + g * SGU_GC:POOL_CH + (g + 1) * SGU_GC] = (u_ref[:, sl] * mixed).astype(BF16)


def _prompt_odd_mix(pin, u, vn, pool_w, pool_scale, sgu_w, sgu_b, *, batch, seq):
    nt = seq // CHUNK
    r = CHUNK // POOL_HALO
    cst2 = lambda b, c: (0, 0)
    cst3 = lambda b, c: (0, 0, 0)
    row = lambda b, c: (b * nt + c, 0)
    return pl.pallas_call(
        _odd_mix_body, grid=(batch, nt),
        in_specs=[pl.BlockSpec((CHUNK, POOL_CH), row),
                  pl.BlockSpec((POOL_HALO, POOL_CH), lambda b, c: (jnp.maximum((b * nt + c) * r - 1, 0), 0)),
                  pl.BlockSpec((CHUNK, SGU_CH), row), pl.BlockSpec((CHUNK, SGU_CH), row),
                  pl.BlockSpec((POOL_GROUPS, POOL_GC, POOL_GC), cst3), pl.BlockSpec((1, POOL_CH), cst2),
                  pl.BlockSpec((SGU_GROUPS, CHUNK, CHUNK), cst3), pl.BlockSpec((CHUNK, SGU_GROUPS), cst2)],
        out_specs=pl.BlockSpec((CHUNK, D_MODEL), row),
        out_shape=jax.ShapeDtypeStruct((batch * seq, D_MODEL), BF16),
        scratch_shapes=[pltpu.VMEM((POOL_HALO + CHUNK, POOL_CH), F32)],
        compiler_params=_cparams("parallel", "arbitrary"), name="prompt_pool_sgu",
    )(pin, pin, u, vn, pool_w.astype(BF16), pool_scale.reshape(1, POOL_CH), sgu_w, sgu_b.T)


def _group_rows(nrows):
    return lax.broadcasted_iota(jnp.int32, (nrows, 1), 0) >> (HPG.bit_length() - 1)


def _sample_cmp_body(q_ref, ckv_ref, o_ref, idx_ref, *, qpos, ncb, nbl, k_past):
    q = q_ref[...].astype(BF16)
    rg = _group_rows(N_HEADS)
    half = nbl // 2
    assert nbl & (nbl - 1) == 0
    sh = nbl.bit_length() - 1
    lane = lax.broadcasted_iota(jnp.int32, (1, ncb), 1)
    grp, w = lane >> sh, lane & (nbl - 1)
    n_cmp = grp * nbl + 2 * (w & (half - 1)) + (w >> (sh - 1))
    mk = jnp.broadcast_to(((n_cmp + 1) * CMP_BLOCK - 1) <= qpos, (N_HEADS, ncb))
    s = jnp.zeros((N_HEADS, ncb), F32)
    for g in range(KV_HEADS):
        ck = ckv_ref[g * HSLOT:(g + 1) * HSLOT, :].astype(BF16)
        s = jnp.where(rg == g, jnp.dot(q, ck, preferred_element_type=F32) * SCALE, s)
    s = jnp.where(mk, s, NEG)
    mx = jnp.max(s, axis=-1, keepdims=True)
    p = jnp.where(mk, jnp.exp(s - mx), 0.0)
    pn = p / jnp.maximum(jnp.sum(p, axis=-1, keepdims=True), 1e-30)
    o = jnp.zeros((N_HEADS, HSLOT), F32)
    for g in range(KV_HEADS):
        cv = ckv_ref[(KV_HEADS + g) * HSLOT:(KV_HEADS + g + 1) * HSLOT, :].astype(BF16)
        o = jnp.where(rg == g, lax.dot_general(pn.astype(BF16), cv, _NT, preferred_element_type=F32), o)
    o_ref[...] = o
    pair = pn + pltpu.roll(pn, ncb - half, axis=1)
    valid = w < half
    sb = grp * half + w
    vis = (sb * SEL_BLOCK) <= qpos
    ri = lax.broadcasted_iota(jnp.int32, (ncb, ncb), 0)
    sb_r = (ri >> sh) * half + (ri & (nbl - 1))
    sb_c = jnp.broadcast_to(sb, (ncb, ncb))
    slot = lax.broadcasted_iota(jnp.int32, (TOP_K, 1), 0)
    for g in range(KV_HEADS):
        imp = jnp.sum(jnp.where(rg == g, pair, 0.0), axis=0, keepdims=True)
        imp = jnp.where(valid, jnp.where(vis, imp, -FORCE), -2.0 * FORCE)
        a = jnp.broadcast_to(imp, (ncb, ncb))
        bt = a.T
        beats = (bt > a) | ((bt == a) & (sb_r < sb_c))
        rank = jnp.sum(jnp.where(beats, 1.0, 0.0), axis=0, keepdims=True)
        onehot = jnp.where((rank == slot.astype(F32)) & valid, 1.0, 0.0)
        idx = jnp.sum(onehot * sb.astype(F32), axis=-1, keepdims=True)
        idx = jnp.where(slot < k_past, idx, 0.0)
        idx_ref[g * TOP_K:(g + 1) * TOP_K, :] = jnp.broadcast_to(idx, (TOP_K, LANES)).astype(jnp.int32)


def _sample_cmp(q3, ckv_t, *, bd, past, qpos, nbl):
    ncb = past // CMP_BLOCK
    k_past = min(TOP_K - 1, past // SEL_BLOCK)
    body = functools.partial(_sample_cmp_body, qpos=qpos, ncb=ncb, nbl=nbl, k_past=k_past)
    head3 = pl.BlockSpec((None, N_HEADS, HSLOT), lambda b: (b, 0, 0))
    return pl.pallas_call(
        body, grid=(bd,),
        in_specs=[head3, pl.BlockSpec((None, KVP, ncb), lambda b: (b, 0, 0))],
        out_specs=[head3, pl.BlockSpec((None, KV_HEADS * TOP_K, LANES), lambda b: (b, 0, 0))],
        out_shape=[jax.ShapeDtypeStruct((bd, N_HEADS, HSLOT), F32),
                   jax.ShapeDtypeStruct((bd, KV_HEADS * TOP_K, LANES), jnp.int32)],
        compiler_params=_cparams("parallel"), name="sample_cmp_attention",
    )(q3, ckv_t)


Q_PAD_ROWS = 8


def _pad_dt(x):
    return _pad_rows_to(x, HSLOT)


def _sample_sel_body(pt_ref, idx_ref, *refs, k_past):
    k_refs, v_refs = refs[:k_past], refs[k_past:2 * k_past]
    q_ref, knew_ref, vnew_ref, o_ref = refs[2 * k_past:]
    b, g = pl.program_id(0), pl.program_id(1)
    bpp = PAGE_SIZE // SEL_BLOCK
    q = q_ref[...].astype(BF16)
    half_of_lane = lax.broadcasted_iota(jnp.int32, (1, PAGE_SIZE), 1) // SEL_BLOCK
    s_parts, m_parts = [], []
    for s in range(k_past):
        kt = _pad_dt(k_refs[s][...]).astype(BF16)
        s_parts.append(jnp.dot(q, kt, preferred_element_type=F32) * SCALE)
        m_parts.append(jnp.broadcast_to(half_of_lane == (idx_ref[b, g, s] & (bpp - 1)), (Q_PAD_ROWS, PAGE_SIZE)))
    s_old = jnp.concatenate(s_parts, axis=1)
    mk = jnp.concatenate(m_parts, axis=1)
    s_old = jnp.where(mk, s_old, NEG)
    s_all = lax.dot_general(q, knew_ref[...].astype(BF16), _NT, preferred_element_type=F32) * SCALE
    lane = lax.broadcasted_iota(jnp.int32, s_all.shape, 1)
    s_new = jnp.sum(jnp.where(lane == b, s_all, 0.0), axis=-1, keepdims=True)
    mx = jnp.maximum(jnp.max(s_old, axis=-1, keepdims=True), s_new)
    p_old = jnp.where(mk, jnp.exp(s_old - mx), 0.0)
    p_new = jnp.exp(s_new - mx)
    den = jnp.maximum(jnp.sum(p_old, axis=-1, keepdims=True) + p_new, 1e-30)
    v_new = vnew_ref[pl.ds(b, 1), :].astype(BF16).astype(F32)
    o = p_new.astype(BF16).astype(F32) * v_new
    for s in range(k_past):
        vt = _pad_dt(v_refs[s][...]).astype(BF16)
        ps = p_old[:, s * PAGE_SIZE:(s + 1) * PAGE_SIZE].astype(BF16)
        o = o + lax.dot_general(ps, vt, _NT, preferred_element_type=F32)
    o_ref[...] = o / den


def _sample_sel(pages5, page_table, idx, layer_base, q4, kvs_new, *, bd, past):
    k_past = min(TOP_K - 1, past // SEL_BLOCK)
    bpp_shift = (PAGE_SIZE // SEL_BLOCK).bit_length() - 1

    def blk_map(s, kv):
        def f(b, g, pt, ix):
            return (layer_base + pt[b, lax.shift_right_logical(ix[b, g, s], bpp_shift)], kv, g, 0, 0)
        return f

    tile = lambda s, kv: pl.BlockSpec((None, None, None, HEAD_DIM, PAGE_SIZE), blk_map(s, kv))
    grid_spec = pltpu.PrefetchScalarGridSpec(
        num_scalar_prefetch=2, grid=(bd, KV_HEADS),
        in_specs=[tile(s, 0) for s in range(k_past)] + [tile(s, 1) for s in range(k_past)]
        + [pl.BlockSpec((None, None, Q_PAD_ROWS, HSLOT), lambda b, g, pt, ix: (b, g, 0, 0)),
           pl.BlockSpec((SAMPLE_ROWS, HSLOT), lambda b, g, pt, ix: (0, g)),
           pl.BlockSpec((SAMPLE_ROWS, HSLOT), lambda b, g, pt, ix: (0, KV_HEADS + g))],
        out_specs=pl.BlockSpec((None, None, Q_PAD_ROWS, HSLOT), lambda b, g, pt, ix: (b, g, 0, 0)))
    return pl.pallas_call(
        functools.partial(_sample_sel_body, k_past=k_past), grid_spec=grid_spec,
        out_shape=jax.ShapeDtypeStruct((bd, KV_HEADS, Q_PAD_ROWS, HSLOT), F32),
        compiler_params=_cparams("parallel", "arbitrary"), name="sample_sel_attention",
    )(page_table, idx, *([pages5] * (2 * k_past)), q4, kvs_new, kvs_new)


def _sample_win_body(q_ref, win_ref, new_ref, ocmp_ref, osel_ref, gate_ref, o_ref, *, qpos, past, wb):
    b = pl.program_id(0)
    q = q_ref[...].astype(BF16)
    qf = q.astype(F32)
    rg = _group_rows(N_HEADS)
    new = new_ref[pl.ds(b, 1), :].astype(BF16).astype(F32)
    kpos = (past - wb) + lax.broadcasted_iota(jnp.int32, (1, wb), 1)
    mk = jnp.broadcast_to((kpos <= qpos) & (kpos >= qpos - WINDOW), (N_HEADS, wb))
    s_old = jnp.zeros((N_HEADS, wb), F32)
    s_new = jnp.zeros((N_HEADS, 1), F32)
    for g in range(KV_HEADS):
        kt = _pad_dt(win_ref[g]).astype(BF16)
        s_old = jnp.where(rg == g, jnp.dot(q, kt, preferred_element_type=F32) * SCALE, s_old)
        sn = jnp.sum(qf * new[:, g * HSLOT:(g + 1) * HSLOT], axis=-1, keepdims=True) * SCALE
        s_new = jnp.where(rg == g, sn, s_new)
    s_old = jnp.where(mk, s_old, NEG)
    mx = jnp.maximum(jnp.max(s_old, axis=-1, keepdims=True), s_new)
    p_old = jnp.where(mk, jnp.exp(s_old - mx), 0.0)
    p_new = jnp.exp(s_new - mx)
    den = jnp.maximum(jnp.sum(p_old, axis=-1, keepdims=True) + p_new, 1e-30)
    o_win = jnp.zeros((N_HEADS, HSLOT), F32)
    for g in range(KV_HEADS):
        vt = _pad_dt(win_ref[KV_HEADS + g]).astype(BF16)
        og = lax.dot_general(p_old.astype(BF16), vt, _NT, preferred_element_type=F32)
        og = og + p_new.astype(BF16).astype(F32) * new[:, (KV_HEADS + g) * HSLOT:(KV_HEADS + g + 1) * HSLOT]
        o_win = jnp.where(rg == g, og, o_win)
    o_win = o_win / den
    gts = gate_ref[...]
    o_ref[...] = gts[:, 0:1] * ocmp_ref[...] + gts[:, 1:2] * osel_ref[...] + gts[:, 2:3] * o_win


def _sample_win(q3, win4, layer, kvw_new, o_cmp, o_sel, gates3, *, bd, past, qpos):
    wb = win4.shape[-1]
    head3 = pl.BlockSpec((None, N_HEADS, HSLOT), lambda b: (b, 0, 0))
    return pl.pallas_call(
        functools.partial(_sample_win_body, qpos=qpos, past=past, wb=wb), grid=(bd,),
        in_specs=[head3, pl.BlockSpec((None, 2 * KV_HEADS, HEAD_DIM, wb), lambda b: (layer * bd + b, 0, 0, 0)),
                  pl.BlockSpec((SAMPLE_ROWS, KVP), lambda b: (0, 0)), head3, head3, head3],
        out_specs=head3,
        out_shape=jax.ShapeDtypeStruct((bd, N_HEADS, HSLOT), F32),
        compiler_params=_cparams("parallel"), name="sample_win_attention",
    )(q3, win4, kvw_new, o_cmp, o_sel, gates3)


def _sample_conv_body(st_ref, u_ref, w_ref, b_ref, g_ref, bn_ref, o_ref, *, bd):
    w = w_ref[...]
    y = jnp.sum(st_ref[...] * w[None, :CONV_W - 1, :], axis=1) + u_ref[0:bd, :] * w[CONV_W - 1:CONV_W, :] + b_ref[...]
    y = _ln_rows(y, g_ref[...], bn_ref[...])
    o_ref[...] = y * jax.nn.sigmoid(y)


def _sample_conv(state, layer, u, cw, cb, cg, cbn, *, bd):
    vec = lambda a: a.reshape(1, CONV_CH)
    cst = lambda i: (0, 0)
    return pl.pallas_call(
        functools.partial(_sample_conv_body, bd=bd), grid=(1,),
        in_specs=[pl.BlockSpec((None, bd, CONV_W - 1, CONV_CH), lambda i: (layer, 0, 0, 0)),
                  pl.BlockSpec((SAMPLE_ROWS, CONV_CH), cst), pl.BlockSpec((CONV_W, CONV_CH), cst),
                  pl.BlockSpec((1, CONV_CH), cst), pl.BlockSpec((1, CONV_CH), cst), pl.BlockSpec((1, CONV_CH), cst)],
        out_specs=pl.BlockSpec((bd, CONV_CH), cst),
        out_shape=jax.ShapeDtypeStruct((bd, CONV_CH), F32),
        compiler_params=_cparams("arbitrary"), name="sample_conv",
    )(state, u, cw, vec(cb), vec(cg), vec(cbn))


def _sample_odd_body(st_ref, pin_ref, u_ref, vn_ref, pw_ref, ps_ref, w0_ref, b0_ref, o_ref, *, bd, start_pos):
    pin = pin_ref[0:bd, :]
    st = st_ref[...]
    for g, w in enumerate(POOL_WINDOWS):
        sl = slice(g * POOL_GC, (g + 1) * POOL_GC)
        tot = pin[:, sl] + jnp.sum(st[:, POOL_STATE - (w - 1):, sl], axis=1)
        d = tot / float(min(w, start_pos + 1)) - pin[:, sl]
        dp = jnp.concatenate([d, jnp.zeros((SAMPLE_ROWS - bd, POOL_GC), F32)], axis=0).astype(BF16)
        y = jnp.dot(dp, pw_ref[g], preferred_element_type=F32)[0:bd]
        o_ref[:, sl] = y * ps_ref[:, sl]
    mixed = w0_ref[...] * vn_ref[0:bd, :] + b0_ref[...]
    o_ref[:, POOL_CH:] = u_ref[0:bd, :] * mixed


def _sample_odd_mix(state, layer, pin, u, vn, pool_w, pool_scale, sgu_w, sgu_b, *, bd, start_pos):
    w0 = jnp.repeat(sgu_w[:, 0, 0], SGU_GC).reshape(1, SGU_CH)
    b0 = jnp.repeat(sgu_b[:, 0], SGU_GC).reshape(1, SGU_CH)
    cst = lambda i: (0, 0)
    return pl.pallas_call(
        functools.partial(_sample_odd_body, bd=bd, start_pos=start_pos), grid=(1,),
        in_specs=[pl.BlockSpec((None, bd, POOL_STATE, POOL_CH), lambda i: (layer, 0, 0, 0)),
                  pl.BlockSpec((SAMPLE_ROWS, POOL_CH), cst), pl.BlockSpec((SAMPLE_ROWS, SGU_CH), cst),
                  pl.BlockSpec((SAMPLE_ROWS, SGU_CH), cst),
                  pl.BlockSpec((POOL_GROUPS, POOL_GC, POOL_GC), lambda i: (0, 0, 0)),
                  pl.BlockSpec((1, POOL_CH), cst), pl.BlockSpec((1, SGU_CH), cst), pl.BlockSpec((1, SGU_CH), cst)],
        out_specs=pl.BlockSpec((bd, D_MODEL), cst),
        out_shape=jax.ShapeDtypeStruct((bd, D_MODEL), F32),
        compiler_params=_cparams("arbitrary"), name="sample_pool_sgu",
    )(state, pin, u, vn, pool_w.astype(BF16), pool_scale.reshape(1, POOL_CH), w0, b0)


def _pad_rows(x, rows):
    return jnp.pad(x, ((0, rows - x.shape[0]), (0, 0)))


def _split_to_nat(x, nheads):
    y = _unpad_heads(x, nheads)
    return _pad_nat(y.reshape(y.shape[:-1] + (nheads, HEAD_DIM))).reshape(x.shape)


def _nat_to_split(x, nheads):
    xh = x.reshape(x.shape[:-1] + (nheads, HSLOT))[..., :HEAD_DIM]
    return _pad_head(xh).reshape(x.shape)


def _even_in_proj(xb, xsb, w_in, rope_p, rope_s, *, tm, bs, stack):
    nrep = rope_p[0].shape[0] // tm
    rs = xsb.shape[0]
    tab = lambda i, j: (i % nrep, 0)
    rope_ex = tuple((t, (tm, LANES), tab) for t in rope_p)
    rope_sx = tuple((t, (rs, LANES), lambda i, j: (0, 0)) for t in rope_s)
    gw = HPG * HSLOT
    qraw, qrot, qraw_s, qrot_s = _mm(
        xb, w_in, tm=tm, tn=2 * gw, n_off=E_Q, n_cols=QW, epilogue=_ep_q, extras=rope_ex,
        outs=((QW, BF16, 2 * gw), (QW, BF16, 2 * gw)), name="even_in_q",
        side=(xsb, _ep_q, rope_sx, ((QW, BF16, 2 * gw), (QW, BF16, 2 * gw))))
    nst = bs[1] // tm
    e, n_even, t_prev = stack
    row = {"f32": (KVP, F32, KVP), "bf16": (KVP, BF16, KVP),
           "t": ((n_even, bs[0], KVW, bs[1]), F32, (None, None, KVW, tm), lambda i, j: (e, i // nst, 0, i % nst))}
    kv_p, kv_s = [], []
    for sec, (off, want) in enumerate(zip((E_KVC, E_KVS, E_KVW), (("f32", "t"), ("bf16", "t"), ("bf16", "t")))):
        res = _mm(xb, w_in, tm=tm, tn=KVP, n_off=off, n_cols=KVP,
                  epilogue=functools.partial(_ep_kv, rope=sec > 0, want=want),
                  extras=rope_ex if sec > 0 else (), outs=tuple(row[k] for k in want),
                  name=("even_in_kvc", "even_in_kvs", "even_in_kvw")[sec],
                  side=(xsb, functools.partial(_ep_kv, rope=sec > 0, want=("f32",)),
                        rope_sx if sec > 0 else (), (row["f32"],)),
                  alias=None if t_prev is None else (t_prev[sec], 1))
        kv_p.append(res[:2])
        kv_s.append(res[2])
    u, u_s = _mm(xb, w_in, tm=tm, tn=2 * CONV_CH, n_off=E_GLU, n_cols=2 * CONV_CH, epilogue=_ep_glu,
                 outs=((CONV_CH, F32, CONV_CH),), name="even_in_glu",
                 side=(xsb, _ep_glu, (), ((CONV_CH, F32, CONV_CH),)))
    gates, gates_s = _mm(xb, w_in, tm=tm, tn=GATE_W, n_off=E_GATE, n_cols=GATE_W, epilogue=_ep_sigmoid,
                         outs=((GATE_W, F32, GATE_W),), name="even_in_gates",
                         side=(xsb, _ep_sigmoid, (), ((GATE_W, F32, GATE_W),)))
    return (qraw, qrot, kv_p, u, gates), (qraw_s, qrot_s, kv_s, u_s, gates_s)


def _mlp_up_cast(xb, xsb, w1, w2, layer, *, tm, tn):
    m, k = xb.shape
    rs = xsb.shape[0]
    dff = w1.shape[2]
    d_out = w2.shape[2]
    ni, nj = m // tm, dff // tn
    slab = dff // (ni * nj)
    assert m % tm == 0 and dff % tn == 0 and dff % (ni * nj) == 0 and slab % SAMPLE_ROWS == 0
    rsub = min(tm, ROW_SUB)

    def body(x_ref, xs_ref, w1_ref, w2_ref, h_ref, hs_ref, w2b_ref, w1b_ref):
        def act(rows_ref, rows):
            a = jnp.maximum(jnp.dot(rows_ref[rows, :], w1b_ref[...], preferred_element_type=F32), 0.0)
            return (a * a).astype(BF16)

        @pl.when(pl.program_id(1) == 0)
        def _():
            w1b_ref[...] = w1_ref[...].astype(BF16)
            hs_ref[...] = act(xs_ref, slice(0, rs))

        w2b_ref[...] = w2_ref[...].astype(BF16)
        for r in range(tm // rsub):
            rows = slice(r * rsub, (r + 1) * rsub)
            h_ref[rows, :] = act(x_ref, rows)

    return pl.pallas_call(
        body, grid=(nj, ni),
        in_specs=[pl.BlockSpec((tm, k), lambda j, i: (i, 0)), pl.BlockSpec((rs, k), lambda j, i: (0, 0)),
                  pl.BlockSpec((None, k, tn), lambda j, i: (layer, 0, j)),
                  pl.BlockSpec((None, slab, d_out), lambda j, i: (layer, j * ni + i, 0))],
        out_specs=[pl.BlockSpec((tm, tn), lambda j, i: (i, j)), pl.BlockSpec((rs, tn), lambda j, i: (0, j)),
                   pl.BlockSpec((slab, d_out), lambda j, i: (j * ni + i, 0))],
        out_shape=[jax.ShapeDtypeStruct((m, dff), BF16), jax.ShapeDtypeStruct((rs, dff), BF16),
                   jax.ShapeDtypeStruct((dff, d_out), BF16)],
        scratch_shapes=[pltpu.VMEM((k, tn), BF16)],
        compiler_params=_cparams("arbitrary", "arbitrary"), name="mlp_up_cast",
    )(xb, xsb, w1, w2)


def kernel(x_prompt, x_sample, cache_cmp_kv, cache_sel_kv, cache_win_kv, state_conv, state_pool, page_table,
           w_in_even, w_out_even, cmp_pe_k, cmp_pe_v, cmp_w_k, cmp_w_v, conv_w, conv_b, conv_ln_g, conv_ln_b,
           w_in_odd, w_out_odd, pool_w, pool_scale, sgu_ln_g, sgu_ln_b, sgu_w, sgu_b,
           mlp_w1, mlp_w2, ln_mix_g, ln_mix_b, ln_ffn_g, ln_ffn_b):
    B, S, D = x_prompt.shape
    Bd, Sd, _ = x_sample.shape
    n_pages = page_table.shape[1]
    past = n_pages * PAGE_SIZE
    n_even, n_pool = cache_cmp_kv.shape[:2]
    wb = cache_win_kv.shape[2]
    assert D == D_MODEL and Sd == 1 and Bd <= SAMPLE_ROWS
    assert S % 1024 == 0 and past % SEL_BLOCK == 0 and S >= WINDOW
    M = B * S
    Ms = SAMPLE_ROWS
    tm_p = 1024

    rope_p = _rope_tables(jnp.arange(S, dtype=jnp.int32))
    rope_s = _rope_tables(jnp.full((Ms,), past, jnp.int32))
    pps = min(32, n_pages)
    assert n_pages % pps == 0
    bsum = _block_sum_matrices(pps)
    cmp_t = cache_cmp_kv.transpose(0, 1, 3, 4, 5, 2).reshape(n_even * n_pool, KVW, PAGE_SIZE)
    sel_t = cache_sel_kv.transpose(0, 1, 3, 4, 5, 2).reshape(n_even * n_pool, 2, KV_HEADS, HEAD_DIM, PAGE_SIZE)
    win_t = cache_win_kv.transpose(0, 1, 3, 4, 5, 2).reshape(n_even * Bd, 2 * KV_HEADS, HEAD_DIM, wb)

    xp = x_prompt.reshape(M, D)
    xs = _pad_rows(x_sample.reshape(Bd, D), Ms)
    xpb, xsb = xp.astype(BF16), xs.astype(BF16)

    outs = {k: [] for k in ("cmp_s", "sel_s", "win_s", "conv_p", "conv_s", "pool_p", "pool_s", "sgu_p", "sgu_s")}
    kv6 = lambda a, lead: a.reshape(lead + (2, KV_HEADS, HEAD_DIM))
    kv_t = None

    for layer in range(DEPTH):
        if layer % 2 == 0:
            e = layer // 2
            wts = _prep_even_weights(w_in_even[e], w_out_even[e], cmp_pe_k[e], cmp_pe_v[e], cmp_w_k[e], cmp_w_v[e])
            prj_p, prj_s = _even_in_proj(xpb, xsb, wts["w_in"], rope_p, rope_s, tm=tm_p, bs=(B, S),
                                         stack=(e, n_even, kv_t))
            qraw, qrot, ((kvc, kvc_t), (kvs_b, kvs_t), (kvw_b, kvw_t)), u, gates = prj_p
            kv_t = (kvc_t, kvs_t, kvw_t)
            summ = _compress_rows(kvc, wts["pe"], rows=512, name="prompt_compress")
            (ckv,) = _mm(summ.astype(BF16), wts["big_p"], tm=min(summ.shape[0], 512), tn=KVP,
                         n_off=0, n_cols=KVP, epilogue=_ep_plain, outs=((KVP, F32, KVP),), name="prompt_compress_map")
            o_att = _prompt_attention(qraw, qrot, gates, ckv, kvs_b, kvw_b, batch=B, seq=S, tq=256, tk=256, ngrp=4)
            c = _prompt_conv(u, conv_w[e], conv_b[e], conv_ln_g[e], conv_ln_b[e], batch=B, seq=S, ts=256)
            outs["conv_p"].append(u.reshape(B, S, CONV_CH)[:, S - (CONV_W - 1):])
            qraw_s, qrot_s, (kvc_s, kvs_s, kvw_s), u_s, gates_s = prj_s
            ckv_t = _compress_pages(cmp_t, page_table, wts["pe_t"], wts["big_t"], bsum, e * n_pool, pps=pps,
                                    name="sample_compress")
            q3 = qraw_s.astype(F32)[:Bd].reshape(Bd, N_HEADS, HSLOT)
            o_cmp, idx = _sample_cmp(q3, ckv_t, bd=Bd, past=past, qpos=past, nbl=pps * (PAGE_SIZE // CMP_BLOCK))
            idx = idx[:, :, 0].reshape(Bd, KV_HEADS, TOP_K)
            qr3 = _split_to_nat(qrot_s.astype(F32)[:Bd], N_HEADS).reshape(Bd, N_HEADS, HSLOT)
            q4 = jnp.pad(qr3.reshape(Bd, KV_HEADS, HPG, HSLOT), ((0, 0), (0, 0), (0, Q_PAD_ROWS - HPG), (0, 0)))
            kvs_nat = _split_to_nat(kvs_s, 2 * KV_HEADS)
            kvw_nat = _split_to_nat(kvw_s, 2 * KV_HEADS)
            o_sel = _sample_sel(sel_t, page_table, idx, e * n_pool, q4, kvs_nat, bd=Bd, past=past)
            o_sel = o_sel[:, :, :HPG].reshape(Bd, N_HEADS, HSLOT)
            g3 = gates_s[:Bd].reshape(Bd, KV_HEADS, LANES)[:, :, :3 * HPG].reshape(Bd, KV_HEADS, 3, HPG)
            g3 = g3.transpose(0, 1, 3, 2).reshape(Bd, N_HEADS, 3)
            g3 = jnp.pad(g3, ((0, 0), (0, 0), (0, LANES - 3)))
            o_s = _sample_win(qr3, win_t, e, kvw_nat, o_cmp, o_sel, g3, bd=Bd, past=past, qpos=past)
            c_s = _sample_conv(state_conv, e, u_s, conv_w[e], conv_b[e], conv_ln_g[e], conv_ln_b[e], bd=Bd)
            o_sb = _pad_rows(_nat_to_split(o_s.reshape(Bd, QW), N_HEADS), Ms).astype(BF16)
            c_sb = _pad_rows(c_s, Ms).astype(BF16)
            xp, xpb, xs, xsb = _proj_ln([o_att, c], [wts["wo_att"], wts["wo_conv"]], xp, ln_mix_g[layer],
                                        ln_mix_b[layer], tm=512, name="even_out_ln", side=([o_sb, c_sb], xs))
            kvc_c = _unpad_heads(kvc_s[:Bd], 2 * KV_HEADS)
            kvs_c = _unpad_heads(kvs_s[:Bd], 2 * KV_HEADS)
            kvw_c = _unpad_heads(kvw_s[:Bd], 2 * KV_HEADS)
            outs["cmp_s"].append(kv6(kvc_c, (Bd, 1)))
            outs["sel_s"].append(kv6(kvs_c, (Bd, 1)))
            wkv = jnp.concatenate([cache_win_kv[e], kv6(kvw_c, (Bd, 1))], axis=1)
            outs["win_s"].append(wkv[:, wkv.shape[1] - min(WINDOW, wkv.shape[1]):])
            outs["conv_s"].append(jnp.concatenate([state_conv[e], u_s[:Bd, None, :]], axis=1)[:, 1:])
        else:
            o = layer // 2
            w_in = w_in_odd[o]
            w_in_p = jnp.concatenate([w_in[:, POOL_CH + SGU_CH:], w_in[:, POOL_CH:POOL_CH + SGU_CH], w_in[:, :POOL_CH]],
                                     axis=1).astype(BF16)
            w_out_p = w_out_odd[o].astype(BF16)
            lg, lb = sgu_ln_g[o].reshape(1, SGU_CH), sgu_ln_b[o].reshape(1, SGU_CH)

            gl_ex = ((lg, (1, SGU_CH), lambda i, j: (0, j)), (lb, (1, SGU_CH), lambda i, j: (0, j)))
            o_v, o_p, o_u = (SGU_CH, F32, SGU_CH), (POOL_CH, F32, POOL_CH), (SGU_CH, F32, SGU_CH)
            vn, vn_s = _mm(xpb, w_in_p, tm=tm_p, tn=SGU_CH, n_off=O_V, n_cols=SGU_CH, epilogue=_ep_gelu_gln,
                           extras=gl_ex, outs=(o_v,), name="odd_in_v", side=(xsb, _ep_gelu_gln, gl_ex, (o_v,)))
            uu, uu_s = _mm(xpb, w_in_p, tm=tm_p, tn=SGU_CH, n_off=O_U, n_cols=SGU_CH, epilogue=_ep_gelu,
                           outs=(o_u,), name="odd_in_u", side=(xsb, _ep_gelu, (), (o_u,)))
            pin, pin_s = _mm(xpb, w_in_p, tm=tm_p, tn=POOL_CH, n_off=O_PIN, n_cols=POOL_CH, epilogue=_ep_plain,
                             outs=(o_p,), name="odd_in_pool", side=(xsb, _ep_plain, (), (o_p,)))
            cat = _prompt_odd_mix(pin, uu, vn, pool_w[o], pool_scale[o], sgu_w[o], sgu_b[o], batch=B, seq=S)
            outs["pool_p"].append(pin.reshape(B, S, POOL_CH)[:, S - POOL_STATE:])
            outs["sgu_p"].append(vn.reshape(B, S, SGU_CH)[:, ((S - 1) // CHUNK) * CHUNK:])
            cat_s = _sample_odd_mix(state_pool, o, pin_s, uu_s, vn_s, pool_w[o], pool_scale[o], sgu_w[o], sgu_b[o],
                                    bd=Bd, start_pos=past)
            xp, xpb, xs, xsb = _proj_ln([cat], [w_out_p], xp, ln_mix_g[layer], ln_mix_b[layer], tm=512,
                                        name="odd_out_ln", side=([_pad_rows(cat_s, Ms).astype(BF16)], xs))
            outs["pool_s"].append(jnp.concatenate([state_pool[o], pin_s[:Bd, None, :]], axis=1)[:, 1:])
            outs["sgu_s"].append(vn_s[:Bd, None, :])
        h, h_s, w2b = _mlp_up_cast(xpb, xsb, mlp_w1, mlp_w2, layer, tm=2 * tm_p, tn=1024)
        xp, xpb, xs, xsb = _mlp2_ln(h, w2b, xp, ln_ffn_g[layer], ln_ffn_b[layer], tm=512, tn=512,
                                    name="mlp_down_ln", side=(h_s, xs))

    st = lambda k: jnp.stack(outs[k])
    rows_last = lambda a: a.reshape(n_even, B, 2, KV_HEADS, HEAD_DIM, a.shape[-1]).transpose(0, 1, 5, 2, 3, 4)
    kvc_t, kvs_t, kvw_t = kv_t
    return (xp.reshape(B, S, D), xs[:Bd].reshape(Bd, Sd, D),
            rows_last(kvc_t), st("cmp_s"), rows_last(kvs_t), st("sel_s"),
            rows_last(kvw_t[:, :, :, S - WINDOW:]), st("win_s"),
            st("conv_p"), st("conv_s"), st("pool_p"), st("pool_s"), st("sgu_p"), st("sgu_s"))
```

```python
import functools

import jax
import jax.numpy as jnp
from jax import lax
from jax.experimental import pallas as pl
from jax.experimental.pallas import tpu as pltpu

F32 = jnp.float32
BF16 = jnp.bfloat16

D_MODEL = 2048
DEPTH = 4
PAGE_SIZE = 128
N_HEADS = 16
HEAD_DIM = 96
KV_HEADS = 4
HPG = N_HEADS // KV_HEADS
ATT_W = N_HEADS * HEAD_DIM
KVW = 2 * KV_HEADS * HEAD_DIM
CMP_BLOCK = 32
SEL_BLOCK = 64
TOP_K = 16
WINDOW = 512
ROPE_THETA = 10000.0
SCALE = HEAD_DIM ** -0.5
LOG2E = 1.4426950408889634
FORCE = 1e9
NEG = -1e30
CONV_CH = D_MODEL // 4
CONV_W = 31
POOL_CH = D_MODEL // 4
POOL_WINDOWS = (2, 4, 8, 16)
POOL_GROUPS = len(POOL_WINDOWS)
POOL_GC = POOL_CH // POOL_GROUPS
POOL_STATE = max(POOL_WINDOWS) - 1
SGU_CH = D_MODEL - POOL_CH
SGU_GROUPS = 4
SGU_GC = SGU_CH // SGU_GROUPS
CHUNK = 128
D_FF = 4 * D_MODEL
ALPHA = (2 * DEPTH) ** 0.25
LN_EPS = 1e-5

LANES = 128
SUBLANES = 8
HALF = HEAD_DIM // 2
HSLOT = LANES
HALF_OFF = LANES // 2
QW = N_HEADS * HSLOT
KVP = 2 * KV_HEADS * HSLOT
GATE_W = LANES
SAMPLE_ROWS = 16
ROW_SUB = 256
LN_ROW_SUB = 128
VMEM_LIMIT = 52 * 1024 * 1024

E_Q, E_KVC, E_KVS, E_KVW = 0, QW, QW + KVP, QW + 2 * KVP
GG_W = 2 * CONV_CH + GATE_W


def _cparams(*sem):
    return pltpu.CompilerParams(dimension_semantics=sem, vmem_limit_bytes=VMEM_LIMIT)


def _pad_head(x):
    halves = x.reshape(x.shape[:-1] + (2, HALF))
    halves = jnp.pad(halves, [(0, 0)] * (halves.ndim - 1) + [(0, HALF_OFF - HALF)])
    return halves.reshape(x.shape[:-1] + (HSLOT,))


def _pad_nat(x):
    return jnp.concatenate([x, jnp.zeros(x.shape[:-1] + (HSLOT - HEAD_DIM,), x.dtype)], axis=-1)


def _unpad_heads(x, nheads):
    xh = x.reshape(x.shape[:-1] + (nheads, HSLOT))
    y = jnp.concatenate([xh[..., :HALF], xh[..., HALF_OFF:HALF_OFF + HALF]], axis=-1)
    return y.reshape(x.shape[:-1] + (nheads * HEAD_DIM,))


def _rope_tables(pos):
    inv = jnp.power(ROPE_THETA, -jnp.arange(HALF, dtype=F32) / HALF)
    ang = pos.astype(F32)[:, None] * inv[None, :]
    cos, sin = jnp.cos(ang), jnp.sin(ang)
    z = jnp.zeros((pos.shape[0], HALF_OFF - HALF), F32)
    return (jnp.concatenate([cos, z, cos, z], axis=1),
            jnp.concatenate([-sin, z, sin, z], axis=1))


def _block_diag2(a, b):
    za = jnp.zeros((a.shape[0], b.shape[1]), a.dtype)
    zb = jnp.zeros((b.shape[0], a.shape[1]), a.dtype)
    return jnp.concatenate([jnp.concatenate([a, za], axis=1), jnp.concatenate([zb, b], axis=1)], axis=0)


def _prep_even_weights(w_in, w_out, pe_k, pe_v, w_ck, w_cv):
    d = w_in.shape[0]
    wb = w_in.astype(BF16)
    g0 = ATT_W + 3 * KVW
    n_heads_all = g0 // HEAD_DIM
    qkv = _pad_head(wb[:, :g0].reshape(d, n_heads_all, HEAD_DIM)).reshape(d, n_heads_all * HSLOT)
    gates = wb[:, g0:g0 + 3 * N_HEADS].reshape(d, 3, KV_HEADS, HPG).transpose(0, 2, 1, 3)
    gates = jnp.pad(gates.reshape(d, 3 * N_HEADS), ((0, 0), (0, GATE_W - 3 * N_HEADS)))
    w_gg = jnp.concatenate([wb[:, g0 + 3 * N_HEADS:], gates], axis=1)
    wo = w_out.astype(BF16)
    wo_att = jnp.pad(wo[:ATT_W].reshape(2 * N_HEADS, HALF, d), ((0, 0), (0, HALF_OFF - HALF), (0, 0)))
    wo_att = wo_att.reshape(QW, d)
    wo_conv = wo[ATT_W:]
    eye = jnp.eye(KV_HEADS, dtype=F32)
    pe = jnp.concatenate([jnp.tile(_pad_head(pe_k), (1, KV_HEADS)), jnp.tile(_pad_head(pe_v), (1, KV_HEADS))], axis=1)
    wk_full = _pad_head(_pad_head(w_ck).T).T
    wv_full = _pad_head(_pad_head(w_cv).T).T
    big_p = _block_diag2(jnp.kron(eye, wk_full), jnp.kron(eye, wv_full))
    pe_t = jnp.concatenate([jnp.tile(jnp.tile(pe_k.T, (1, PAGE_SIZE // CMP_BLOCK)), (KV_HEADS, 1)),
                            jnp.tile(jnp.tile(pe_v.T, (1, PAGE_SIZE // CMP_BLOCK)), (KV_HEADS, 1))], axis=0)
    big_t = _block_diag2(jnp.kron(eye, _pad_head(w_ck).T), jnp.kron(eye, _pad_nat(w_cv).T))
    return dict(w_qkv=qkv, w_gg=w_gg, wo_att=wo_att, wo_conv=wo_conv, pe=pe, pe_t=pe_t,
                big_p=big_p.astype(BF16), big_t=big_t.astype(BF16))


def _block_sum_matrices(pps):
    bpp = PAGE_SIZE // CMP_BLOCK
    nbl = bpp * pps
    p = jnp.arange(pps)[:, None, None]
    i = (jnp.arange(PAGE_SIZE) // CMP_BLOCK)[None, :, None]
    c = jnp.arange(nbl)[None, None, :]
    col = (i % 2) * (nbl // 2) + (bpp // 2) * p + i // 2
    return jnp.where(c == col, 1.0 / CMP_BLOCK, 0.0).astype(BF16)


def _ln_rows(y, g, b):
    mu = jnp.mean(y, axis=-1, keepdims=True)
    yc = y - mu
    var = jnp.mean(yc * yc, axis=-1, keepdims=True)
    return yc * lax.rsqrt(var + LN_EPS) * g + b


def _rope_slot(x, cos, sin):
    return x * cos + pltpu.roll(x, HALF_OFF, axis=1) * sin


_NT = (((1,), (1,)), ((), ()))


def _pad_rows_to(x, rows):
    return jnp.concatenate([x, jnp.zeros((rows - x.shape[0],) + x.shape[1:], x.dtype)], axis=0)


def _mm(x, w, *, tm, tn, n_off, n_cols, epilogue, extras=(), outs, name, side=None, alias=None):
    m, k = x.shape
    assert m % tm == 0 and n_cols % tn == 0 and n_off % tn == 0
    joff = n_off // tn
    ji = lambda im: (lambda j, i: im(i, j))
    in_specs = [pl.BlockSpec((tm, k), lambda j, i: (i, 0)),
                pl.BlockSpec((k, tn), lambda j, i: (0, joff + j))]
    in_specs += [pl.BlockSpec(bs, ji(im)) for _, bs, im in extras]
    operands = [x, w] + [a for a, _, _ in extras]
    out_shape, out_specs = [], []
    for o in outs:
        if len(o) == 3:
            out_shape.append(jax.ShapeDtypeStruct((m, o[0]), o[1]))
            out_specs.append(pl.BlockSpec((tm, o[2]), lambda j, i: (i, j)))
        else:
            out_shape.append(jax.ShapeDtypeStruct(o[0], o[1]))
            out_specs.append(pl.BlockSpec(o[2], ji(o[3])))
    ne, no = len(extras), len(outs)
    nse = 0
    if side is not None:
        xs, s_epilogue, s_extras, s_outs = side
        rs = xs.shape[0]
        nse = len(s_extras)
        in_specs += [pl.BlockSpec((rs, k), lambda j, i: (0, 0))]
        in_specs += [pl.BlockSpec(bs, ji(im)) for _, bs, im in s_extras]
        operands += [xs] + [a for a, _, _ in s_extras]
        for cols, dt, bc in s_outs:
            out_shape.append(jax.ShapeDtypeStruct((rs, cols), dt))
            out_specs.append(pl.BlockSpec((rs, bc), lambda j, i: (0, j)))
    rsub = min(tm, ROW_SUB)
    n_in = len(operands)
    io_alias = {}
    if alias is not None:
        io_alias = {n_in: alias[1]}
        in_specs.append(pl.BlockSpec(memory_space=pl.ANY))
        operands.append(alias[0])

    def body(*refs):
        x_ref, w_ref = refs[:2]
        ex = refs[2:2 + ne]
        o_refs = refs[len(operands):len(operands) + no]
        for r in range(tm // rsub):
            rows = slice(r * rsub, (r + 1) * rsub)
            acc = jnp.dot(x_ref[rows, :], w_ref[...], preferred_element_type=F32)
            epilogue(acc, ex, o_refs, rows)
        if side is not None:
            @pl.when(pl.program_id(1) == 0)
            def _():
                acc = jnp.dot(refs[2 + ne][...], w_ref[...], preferred_element_type=F32)
                s_epilogue(acc, refs[3 + ne:n_in], refs[len(operands) + no:], slice(0, rs))

    return pl.pallas_call(
        body, grid=(n_cols // tn, m // tm), in_specs=in_specs, out_specs=out_specs, out_shape=out_shape,
        input_output_aliases=io_alias, compiler_params=_cparams("arbitrary", "arbitrary"), name=name,
    )(*operands)


def _ep_q(acc, ex, outs, rows):
    cos, sin = ex[0][rows, :], ex[1][rows, :]
    outs[0][rows, :] = acc.astype(BF16)
    for j in range(acc.shape[1] // HSLOT):
        sl = slice(j * HSLOT, (j + 1) * HSLOT)
        outs[1][rows, sl] = _rope_slot(acc[:, sl], cos, sin).astype(BF16)


def _ep_kv(acc, ex, outs, rows, *, rope, want):
    o = dict(zip(want, outs))
    if rope:
        cos, sin = ex[0][rows, :], ex[1][rows, :]
    for j in range(2 * KV_HEADS):
        sl = slice(j * HSLOT, (j + 1) * HSLOT)
        x = acc[:, sl]
        if rope and j < KV_HEADS:
            x = _rope_slot(x, cos, sin)
        if "f32" in o:
            o["f32"][rows, sl] = x
        if "bf16" in o:
            o["bf16"][rows, sl] = x.astype(BF16)
        if "t" in o:
            xt = x.T
            o["t"][j * HEAD_DIM:j * HEAD_DIM + HALF, rows] = xt[0:HALF]
            o["t"][j * HEAD_DIM + HALF:(j + 1) * HEAD_DIM, rows] = xt[HALF_OFF:HALF_OFF + HALF]


def _ep_glu_gates(acc, ex, outs, rows):
    outs[0][rows, :] = acc[:, :CONV_CH] * jax.nn.sigmoid(acc[:, CONV_CH:2 * CONV_CH])
    outs[1][rows, :] = jax.nn.sigmoid(acc[:, 2 * CONV_CH:])


def _ep_plain(acc, ex, outs, rows):
    outs[0][rows, :] = acc.astype(outs[0].dtype)


def _ep_relu2(acc, ex, outs, rows):
    r = jnp.maximum(acc, 0.0)
    outs[0][rows, :] = (r * r).astype(outs[0].dtype)


def _ep_pool_gelu(acc, ex, outs, rows):
    outs[0][rows, :] = acc[:, :POOL_CH]
    outs[1][rows, :] = jax.nn.gelu(acc[:, POOL_CH:])


def _ep_gelu_gln(acc, ex, outs, rows):
    g, b = ex[0][...], ex[1][...]
    v = jax.nn.gelu(acc)
    for j in range(acc.shape[1] // SGU_GC):
        sl = slice(j * SGU_GC, (j + 1) * SGU_GC)
        outs[0][rows, sl] = _ln_rows(v[:, sl], g[:, sl], b[:, sl])


def _proj_ln(a_list, w_list, resid, g, b, *, tm, name, side):
    m, n = resid.shape
    as_list, resid_s = side
    rs = resid_s.shape[0]
    npair = len(a_list)
    in_specs = []
    for a in a_list:
        in_specs.append(pl.BlockSpec((tm, a.shape[1]), lambda i: (i, 0)))
    for w in w_list:
        in_specs.append(pl.BlockSpec(w.shape, lambda i: (0, 0), pipeline_mode=pl.Buffered(1)))
    in_specs += [pl.BlockSpec((tm, n), lambda i: (i, 0)),
                 pl.BlockSpec((1, n), lambda i: (0, 0)), pl.BlockSpec((1, n), lambda i: (0, 0))]
    for a in as_list:
        in_specs.append(pl.BlockSpec((rs, a.shape[1]), lambda i: (0, 0)))
    in_specs.append(pl.BlockSpec((rs, n), lambda i: (0, 0)))

    def body(*refs):
        a_refs, w_refs = refs[:npair], refs[npair:2 * npair]
        r_ref, g_ref, b_ref = refs[2 * npair:2 * npair + 3]
        as_refs = refs[2 * npair + 3:3 * npair + 3]
        rs_ref, o_ref, ob_ref, os_ref, osb_ref = refs[3 * npair + 3:]

        def rows_out(a_rs, res, rows):
            acc = ALPHA * res[rows, :]
            for a_ref, w_ref in zip(a_rs, w_refs):
                acc = acc + jnp.dot(a_ref[rows, :], w_ref[...], preferred_element_type=F32)
            return _ln_rows(acc, g_ref[...], b_ref[...])

        rsub = min(tm, LN_ROW_SUB)
        for r in range(tm // rsub):
            rows = slice(r * rsub, (r + 1) * rsub)
            y = rows_out(a_refs, r_ref, rows)
            o_ref[rows, :] = y
            ob_ref[rows, :] = y.astype(BF16)
        @pl.when(pl.program_id(0) == 0)
        def _():
            ys = rows_out(as_refs, rs_ref, slice(0, rs))
            os_ref[...] = ys
            osb_ref[...] = ys.astype(BF16)

    row = pl.BlockSpec((tm, n), lambda i: (i, 0))
    srow = pl.BlockSpec((rs, n), lambda i: (0, 0))
    return pl.pallas_call(
        body, grid=(m // tm,), in_specs=in_specs, out_specs=[row, row, srow, srow],
        out_shape=[jax.ShapeDtypeStruct((m, n), F32), jax.ShapeDtypeStruct((m, n), BF16),
                   jax.ShapeDtypeStruct((rs, n), F32), jax.ShapeDtypeStruct((rs, n), BF16)],
        compiler_params=_cparams("arbitrary"), name=name,
    )(*a_list, *w_list, resid, g.reshape(1, n), b.reshape(1, n), *as_list, resid_s)


def _mlp2_ln(h, w2, resid, g, b, *, tm, tn, name, side):
    m, kf = h.shape
    n = w2.shape[1]
    nj = n // tn
    hs, resid_s = side
    rs = hs.shape[0]
    rsub, rsub_ln = min(tm, ROW_SUB), min(tm, LN_ROW_SUB)

    def body(h_ref, w_ref, r_ref, g_ref, b_ref, hs_ref, rs_ref, o_ref, ob_ref, os_ref, osb_ref):
        j = pl.program_id(1)
        first = pl.program_id(0) == 0
        srows = slice(0, rs)

        def tile(hr, rr, rows):
            return ALPHA * rr[rows, :] + jnp.dot(hr[rows, :], w_ref[...], preferred_element_type=F32)

        def finish(o, ob, y, rows):
            y = _ln_rows(y, g_ref[...], b_ref[...])
            o[rows, :] = y
            ob[rows, :] = y.astype(BF16)

        for jj in range(nj - 1):
            @pl.when(j == jj)
            def _(jj=jj):
                cols = slice(jj * tn, (jj + 1) * tn)
                for r in range(tm // rsub):
                    rows = slice(r * rsub, (r + 1) * rsub)
                    o_ref[rows, cols] = tile(h_ref, r_ref, rows)

                @pl.when(first)
                def _():
                    os_ref[:, cols] = tile(hs_ref, rs_ref, srows)

        @pl.when(j == nj - 1)
        def _():
            done = slice(0, (nj - 1) * tn)
            for r in range(tm // rsub_ln):
                rows = slice(r * rsub_ln, (r + 1) * rsub_ln)
                finish(o_ref, ob_ref, jnp.concatenate([o_ref[rows, done], tile(h_ref, r_ref, rows)], axis=1), rows)

            @pl.when(first)
            def _():
                finish(os_ref, osb_ref, jnp.concatenate([os_ref[:, done], tile(hs_ref, rs_ref, srows)], axis=1), srows)

    assert nj > 1
    row = pl.BlockSpec((tm, n), lambda i, j: (i, 0))
    srow = pl.BlockSpec((rs, n), lambda i, j: (0, 0))
    return pl.pallas_call(
        body, grid=(m // tm, nj),
        in_specs=[pl.BlockSpec((tm, kf), lambda i, j: (i, 0)), pl.BlockSpec((kf, tn), lambda i, j: (0, j)),
                  pl.BlockSpec((tm, tn), lambda i, j: (i, j)),
                  pl.BlockSpec((1, n), lambda i, j: (0, 0)), pl.BlockSpec((1, n), lambda i, j: (0, 0)),
                  pl.BlockSpec((rs, kf), lambda i, j: (0, 0)), pl.BlockSpec((rs, tn), lambda i, j: (0, j))],
        out_specs=[row, row, srow, srow],
        out_shape=[jax.ShapeDtypeStruct((m, n), F32), jax.ShapeDtypeStruct((m, n), BF16),
                   jax.ShapeDtypeStruct((rs, n), F32), jax.ShapeDtypeStruct((rs, n), BF16)],
        compiler_params=_cparams("arbitrary", "arbitrary"), name=name,
    )(h, w2, resid, g.reshape(1, n), b.reshape(1, n), hs, resid_s)


def _compress_rows(kvc, pe, *, rows, name):
    m, c = kvc.shape
    nb = rows // CMP_BLOCK

    def body(x_ref, pe_ref, o_ref):
        x = x_ref[...].reshape(nb, CMP_BLOCK, c) * pe_ref[...][None]
        o_ref[...] = jnp.sum(x, axis=1) * (1.0 / CMP_BLOCK)

    return pl.pallas_call(
        body, grid=(m // rows,),
        in_specs=[pl.BlockSpec((rows, c), lambda i: (i, 0)), pl.BlockSpec((CMP_BLOCK, c), lambda i: (0, 0))],
        out_specs=pl.BlockSpec((nb, c), lambda i: (i, 0)),
        out_shape=jax.ShapeDtypeStruct((m // CMP_BLOCK, c), F32),
        compiler_params=_cparams("parallel"), name=name,
    )(kvc, pe)


def _compress_pages(pages_t, page_table, pe_t, big_t, bsum, layer_base, *, pps, name):
    bd, n_pages = page_table.shape
    bpp = PAGE_SIZE // CMP_BLOCK
    nbl = bpp * pps
    nsteps = n_pages // pps

    def body(pt_ref, *refs):
        page_refs = refs[:pps]
        pe_ref, big_ref, bsum_ref, o_ref = refs[pps:]
        acc = jnp.zeros((KVW, nbl), F32)
        for p in range(pps):
            x = page_refs[p][...] * pe_ref[...]
            acc = acc + jnp.dot(x.astype(BF16), bsum_ref[p], preferred_element_type=F32)
        o_ref[...] = jnp.dot(big_ref[...], acc.astype(BF16), preferred_element_type=F32)

    def page_map(p):
        return lambda b, j, pt: (layer_base + pt[b, j * pps + p], 0, 0)

    cst2 = lambda b, j, pt: (0, 0)
    grid_spec = pltpu.PrefetchScalarGridSpec(
        num_scalar_prefetch=1, grid=(bd, nsteps),
        in_specs=[pl.BlockSpec((None, KVW, PAGE_SIZE), page_map(p)) for p in range(pps)]
        + [pl.BlockSpec((KVW, PAGE_SIZE), cst2), pl.BlockSpec((KVP, KVW), cst2),
           pl.BlockSpec((pps, PAGE_SIZE, nbl), lambda b, j, pt: (0, 0, 0))],
        out_specs=pl.BlockSpec((None, KVP, nbl), lambda b, j, pt: (b, 0, j)))
    return pl.pallas_call(
        body, grid_spec=grid_spec,
        out_shape=jax.ShapeDtypeStruct((bd, KVP, n_pages * bpp), F32),
        compiler_params=_cparams("parallel", "arbitrary"), name=name,
    )(page_table, *([pages_t] * pps), pe_t, big_t, bsum)


def _flash_step_t(q, k, vt, valid, m_ref, l_ref, acc_ref):
    bias = jnp.where(valid, 0.0, NEG)
    s = jnp.dot(k, q, preferred_element_type=F32) + jnp.concatenate([bias] * HPG, axis=1)
    m_prev = m_ref[...]
    m_new = jnp.maximum(m_prev, jnp.max(s, axis=0, keepdims=True))
    alpha = jnp.exp2(m_prev - m_new)
    p = jnp.exp2(s - m_new)
    l_ref[...] = alpha * l_ref[...] + jnp.sum(p, axis=0, keepdims=True)
    acc_ref[...] = alpha * acc_ref[...] + jnp.dot(vt, p.astype(BF16), preferred_element_type=F32)
    m_ref[...] = m_new


def _attn_body(*refs, tq, tk, seq, k_top, ngrp):
    n_in = 9
    ins = [refs[g * n_in:(g + 1) * n_in] for g in range(ngrp)]
    o_ref = refs[ngrp * n_in]
    m_ref, l_ref, acc_ref, sel_ref, vts_ref, vtw_ref = refs[ngrp * n_in + 1:]
    i = pl.program_id(2)
    cols = HPG * tq
    nsb = seq // SEL_BLOCK
    nt = seq // tk
    bpt = tk // SEL_BLOCK
    q0 = i * tq
    lane_q = lax.broadcasted_iota(jnp.int32, (1, cols), 1)
    qpos = q0 + (lane_q & (tq - 1))
    qp1 = q0 + lax.broadcasted_iota(jnp.int32, (1, tq), 1)

    @pl.when(i == 0)
    def _():
        def tr(t, c):
            for g in range(ngrp):
                vs_ref, vw_ref = ins[g][6], ins[g][8]
                for h in range(tk // LANES):
                    k0 = pl.multiple_of(t * tk + h * LANES, LANES)
                    hs = slice(h * LANES, (h + 1) * LANES)
                    vts_ref[g, t, :, hs] = vs_ref[pl.ds(k0, LANES), :].astype(F32).T.astype(BF16)
                    vtw_ref[g, t, :, hs] = vw_ref[pl.ds(k0, LANES), :].astype(F32).T.astype(BF16)
            return c
        lax.fori_loop(0, nt, tr, 0)

    def heads_t(ref, scale=1.0):
        parts = [ref[:, j * HSLOT:(j + 1) * HSLOT].astype(F32).T * scale for j in range(HPG)]
        return jnp.concatenate(parts, axis=1).astype(BF16)

    r = lax.broadcasted_iota(jnp.int32, (2 * nsb, 1), 0)
    n_of = jnp.where(r < nsb, 2 * r, 2 * (r - nsb) + 1)
    mk = ((n_of + 1) * CMP_BLOCK - 1) <= qpos
    sb = lax.broadcasted_iota(jnp.int32, (nsb, 1), 0)
    vis = (sb * SEL_BLOCK) <= qp1
    cur = sb == (qp1 >> (SEL_BLOCK.bit_length() - 1))

    def compressed_branch(g):
        qraw_ref, ck_ref, cv_ref = ins[g][0], ins[g][3], ins[g][4]
        qr = heads_t(qraw_ref)
        ck = jnp.concatenate([ck_ref[pl.ds(0, nsb, stride=2), :], ck_ref[pl.ds(1, nsb, stride=2), :]], axis=0)
        s = jnp.dot(ck.astype(BF16), qr, preferred_element_type=F32) * SCALE
        s = jnp.where(mk, s, NEG)
        mx = jnp.max(s, axis=0, keepdims=True)
        p = jnp.where(mk, jnp.exp(s - mx), 0.0)
        pn = p * (1.0 / jnp.maximum(jnp.sum(p, axis=0, keepdims=True), 1e-30))
        cv = jnp.concatenate([cv_ref[pl.ds(0, nsb, stride=2), :], cv_ref[pl.ds(1, nsb, stride=2), :]], axis=0)
        cvt = _pad_rows_to(cv, LANES).T.astype(BF16)
        o_cmp = jnp.dot(cvt, _pad_rows_to(pn, LANES).astype(BF16), preferred_element_type=F32)
        pp = pn[0:nsb] + pn[nsb:2 * nsb]
        imp = pp[:, 0:tq]
        for j in range(1, HPG):
            imp = imp + pp[:, j * tq:(j + 1) * tq]
        imp = jnp.where(cur, FORCE, jnp.where(vis, imp, -FORCE))
        cnt = jnp.zeros((nsb, tq), F32)
        for j in range(nsb):
            rowj = imp[j:j + 1, :]
            beats = (rowj > imp) | ((rowj == imp) & (j < sb))
            cnt = cnt + jnp.where(beats, 1.0, 0.0)
        sel = jnp.where(cnt < k_top, 1.0, 0.0)
        for t in range(nt):
            sel_ref[g, t, 0:bpt, :] = sel[bpt * t:bpt * (t + 1), :]
        return o_cmp

    o_cmp = [compressed_branch(g) for g in range(ngrp)]
    qt = [heads_t(ins[g][1], SCALE * LOG2E) for g in range(ngrp)]
    rowk = lax.broadcasted_iota(jnp.int32, (tk, 1), 0)

    def reset():
        m_ref[...] = jnp.full((ngrp, 1, cols), NEG, F32)
        l_ref[...] = jnp.zeros((ngrp, 1, cols), F32)
        acc_ref[...] = jnp.zeros((ngrp, HSLOT, cols), F32)

    def result(g):
        return acc_ref[g] * (1.0 / jnp.maximum(l_ref[g], 1e-30))

    reset()

    def sel_step(t, c):
        k0 = pl.multiple_of(t * tk, tk)
        kp = k0 + rowk
        for g in range(ngrp):
            sm = sel_ref[g, t, bpt - 1:bpt, :]
            for j in range(bpt - 2, -1, -1):
                sm = jnp.where(rowk < (j + 1) * SEL_BLOCK, sel_ref[g, t, j:j + 1, :], sm)
            valid = (kp <= qp1) & (sm > 0.5)
            _flash_step_t(qt[g], ins[g][5][pl.ds(k0, tk), :], vts_ref[g, t], valid,
                          m_ref.at[g], l_ref.at[g], acc_ref.at[g])
        return c

    t_end = lax.div(q0 + tq + tk - 1, tk)
    lax.fori_loop(0, t_end, sel_step, 0)
    o_sel = [result(g) for g in range(ngrp)]

    reset()

    def win_step(t, c):
        k0 = pl.multiple_of(t * tk, tk)
        kp = k0 + rowk
        valid = (kp <= qp1) & (kp >= qp1 - WINDOW)
        for g in range(ngrp):
            _flash_step_t(qt[g], ins[g][7][pl.ds(k0, tk), :], vtw_ref[g, t], valid,
                          m_ref.at[g], l_ref.at[g], acc_ref.at[g])
        return c

    lax.fori_loop(lax.div(jnp.maximum(q0 - WINDOW, 0), tk), t_end, win_step, 0)

    gw = HPG * HSLOT
    for g in range(ngrp):
        gt = ins[g][2][...].T
        o = jnp.zeros((HSLOT, cols), F32)
        for br, o_br in enumerate((o_cmp[g], o_sel[g], result(g))):
            r0 = (g * 3 + br) * HPG
            grow = jnp.concatenate([gt[r0 + j:r0 + j + 1, :] for j in range(HPG)], axis=1)
            o = o + grow * o_br
        for j in range(HPG):
            o_ref[:, g * gw + j * HSLOT:g * gw + (j + 1) * HSLOT] = o[:, j * tq:(j + 1) * tq].T.astype(BF16)


def _prompt_attention(qraw, qrot, gates, ckv, kvs_b, kvw_b, *, batch, seq, tq, tk, ngrp):
    nq = seq // tq
    ncb = seq // CMP_BLOCK
    nt = seq // tk
    cols = HPG * tq
    k_top = min(TOP_K, seq // SEL_BLOCK)
    assert tq & (tq - 1) == 0 and seq % tk == 0 and tk % LANES == 0 and tk // SEL_BLOCK <= SUBLANES
    assert ngrp == KV_HEADS
    body = functools.partial(_attn_body, tq=tq, tk=tk, seq=seq, k_top=k_top, ngrp=ngrp)
    gw = HPG * HSLOT
    in_specs, operands = [], []
    for g in range(ngrp):
        qmap = functools.partial(lambda b, p, i, g: (b * nq + i, p * ngrp + g), g=g)
        kmap = functools.partial(lambda b, p, i, g: (b, p * ngrp + g), g=g)
        vmap_ = functools.partial(lambda b, p, i, g: (b, KV_HEADS + p * ngrp + g), g=g)
        in_specs += [pl.BlockSpec((tq, gw), qmap), pl.BlockSpec((tq, gw), qmap),
                     pl.BlockSpec((tq, GATE_W), lambda b, p, i: (b * nq + i, 0)),
                     pl.BlockSpec((ncb, HSLOT), kmap), pl.BlockSpec((ncb, HSLOT), vmap_),
                     pl.BlockSpec((seq, HSLOT), kmap), pl.BlockSpec((seq, HSLOT), vmap_),
                     pl.BlockSpec((seq, HSLOT), kmap), pl.BlockSpec((seq, HSLOT), vmap_)]
        operands += [qraw, qrot, gates, ckv, ckv, kvs_b, kvs_b, kvw_b, kvw_b]
    return pl.pallas_call(
        body, grid=(batch, KV_HEADS // ngrp, nq), in_specs=in_specs,
        out_specs=pl.BlockSpec((tq, ngrp * gw), lambda b, p, i: (b * nq + i, p)),
        out_shape=jax.ShapeDtypeStruct((batch * seq, QW), BF16),
        scratch_shapes=[pltpu.VMEM((ngrp, 1, cols), F32), pltpu.VMEM((ngrp, 1, cols), F32),
                        pltpu.VMEM((ngrp, HSLOT, cols), F32), pltpu.VMEM((ngrp, nt, SUBLANES, tq), F32),
                        pltpu.VMEM((ngrp, nt, HSLOT, tk), BF16), pltpu.VMEM((ngrp, nt, HSLOT, tk), BF16)],
        compiler_params=_cparams("parallel", "parallel", "arbitrary"), name="prompt_attention",
    )(*operands)


CONV_HALO = 32


def _conv_body(cur_ref, prev_ref, w_ref, b_ref, g_ref, bn_ref, o_ref, ext_ref, *, ts):
    c = pl.program_id(1)
    span = CONV_HALO + ts - SUBLANES
    ext_ref[0, 0:CONV_HALO, :] = jnp.where(c > 0, prev_ref[...], 0.0)
    ext_ref[0, CONV_HALO:CONV_HALO + ts, :] = cur_ref[...]
    for s in range(1, SUBLANES):
        ext_ref[s, 0:span, :] = ext_ref[0, pl.ds(s, span), :]
    acc = jnp.zeros((ts, CONV_CH), F32) + b_ref[...]
    off = CONV_HALO - (CONV_W - 1)
    for k in range(CONV_W):
        s = (off + k) % SUBLANES
        acc = acc + ext_ref[s, pl.ds(off + k - s, ts), :] * w_ref[k:k + 1, :]
    y = _ln_rows(acc, g_ref[...], bn_ref[...])
    o_ref[...] = (y * jax.nn.sigmoid(y)).astype(BF16)


def _prompt_conv(u, cw, cb, cg, cbn, *, batch, seq, ts):
    nt = seq // ts
    r = ts // CONV_HALO
    cwp = jnp.pad(cw, ((0, CONV_HALO - CONV_W), (0, 0)))
    vec = lambda a: a.reshape(1, CONV_CH)
    cst = lambda b, c: (0, 0)
    return pl.pallas_call(
        functools.partial(_conv_body, ts=ts), grid=(batch, nt),
        in_specs=[pl.BlockSpec((ts, CONV_CH), lambda b, c: (b * nt + c, 0)),
                  pl.BlockSpec((CONV_HALO, CONV_CH), lambda b, c: (jnp.maximum((b * nt + c) * r - 1, 0), 0)),
                  pl.BlockSpec((CONV_HALO, CONV_CH), cst),
                  pl.BlockSpec((1, CONV_CH), cst), pl.BlockSpec((1, CONV_CH), cst), pl.BlockSpec((1, CONV_CH), cst)],
        out_specs=pl.BlockSpec((ts, CONV_CH), lambda b, c: (b * nt + c, 0)),
        out_shape=jax.ShapeDtypeStruct((batch * seq, CONV_CH), BF16),
        scratch_shapes=[pltpu.VMEM((SUBLANES, CONV_HALO + ts, CONV_CH), F32)],
        compiler_params=_cparams("parallel", "arbitrary"), name="prompt_conv",
    )(u, u, cwp, vec(cb), vec(cg), vec(cbn))


POOL_HALO = 16


def _odd_mix_body(pin_ref, prev_ref, u_ref, vn_ref, pw_ref, ps_ref, sw_ref, sb_ref, o_ref, ext_ref, *, ts):
    c = pl.program_id(1)
    ext_ref[0:POOL_HALO, :] = jnp.where(c > 0, prev_ref[...], 0.0)
    ext_ref[POOL_HALO:POOL_HALO + ts, :] = pin_ref[...]
    t = c * ts + lax.broadcasted_iota(jnp.int32, (ts, 1), 0)
    for g, w in enumerate(POOL_WINDOWS):
        sl = slice(g * POOL_GC, (g + 1) * POOL_GC)
        tot = ext_ref[pl.ds(POOL_HALO, ts), sl]
        for j in range(1, w):
            tot = tot + ext_ref[pl.ds(POOL_HALO - j, ts), sl]
        cnt = jnp.minimum(w, t + 1).astype(F32)
        d = tot / cnt - pin_ref[:, sl]
        y = jnp.dot(d.astype(BF16), pw_ref[g], preferred_element_type=F32)
        o_ref[:, sl] = (y * ps_ref[:, sl]).astype(BF16)
    ri = lax.broadcasted_iota(jnp.int32, (CHUNK, CHUNK), 0)
    ci = lax.broadcasted_iota(jnp.int32, (CHUNK, CHUNK), 1)
    for g in range(SGU_GROUPS):
        sl = slice(g * SGU_GC, (g + 1) * SGU_GC)
        ws = jnp.where(ci <= ri, sw_ref[g], 0.0).astype(BF16)
        for q in range(ts // CHUNK):
            rows = slice(q * CHUNK, (q + 1) * CHUNK)
            mixed = jnp.dot(ws, vn_ref[rows, sl].astype(BF16), preferred_element_type=F32) + sb_ref[:, g:g + 1]
            o_ref[rows, POOL_CH + g * SGU_GC:POOL_CH + (g + 1) * SGU_GC] = (u_ref[rows, sl] * mixed).astype(BF16)


def _prompt_odd_mix(pin, u, vn, pool_w, pool_scale, sgu_w, sgu_b, *, batch, seq, ts):
    assert seq % ts == 0 and ts % CHUNK == 0
    nt = seq // ts
    r = ts // POOL_HALO
    cst2 = lambda b, c: (0, 0)
    cst3 = lambda b, c: (0, 0, 0)
    row = lambda b, c: (b * nt + c, 0)
    return pl.pallas_call(
        functools.partial(_odd_mix_body, ts=ts), grid=(batch, nt),
        in_specs=[pl.BlockSpec((ts, POOL_CH), row),
                  pl.BlockSpec((POOL_HALO, POOL_CH), lambda b, c: (jnp.maximum((b * nt + c) * r - 1, 0), 0)),
                  pl.BlockSpec((ts, SGU_CH), row), pl.BlockSpec((ts, SGU_CH), row),
                  pl.BlockSpec((POOL_GROUPS, POOL_GC, POOL_GC), cst3), pl.BlockSpec((1, POOL_CH), cst2),
                  pl.BlockSpec((SGU_GROUPS, CHUNK, CHUNK), cst3), pl.BlockSpec((CHUNK, SGU_GROUPS), cst2)],
        out_specs=pl.BlockSpec((ts, D_MODEL), row),
        out_shape=jax.ShapeDtypeStruct((batch * seq, D_MODEL), BF16),
        scratch_shapes=[pltpu.VMEM((POOL_HALO + ts, POOL_CH), F32)],
        compiler_params=_cparams("parallel", "arbitrary"), name="prompt_pool_sgu",
    )(pin, pin, u, vn, pool_w.astype(BF16), pool_scale.reshape(1, POOL_CH), sgu_w, sgu_b.T)


def _group_rows(nrows):
    return lax.broadcasted_iota(jnp.int32, (nrows, 1), 0) >> (HPG.bit_length() - 1)


def _sample_cmp_body(q_ref, ckv_ref, o_ref, idx_ref, *, qpos, ncb, nbl, k_past):
    q = q_ref[...].astype(BF16)
    rg = _group_rows(N_HEADS)
    half = nbl // 2
    assert nbl & (nbl - 1) == 0
    sh = nbl.bit_length() - 1
    lane = lax.broadcasted_iota(jnp.int32, (1, ncb), 1)
    grp, w = lane >> sh, lane & (nbl - 1)
    n_cmp = grp * nbl + 2 * (w & (half - 1)) + (w >> (sh - 1))
    mk = jnp.broadcast_to(((n_cmp + 1) * CMP_BLOCK - 1) <= qpos, (N_HEADS, ncb))
    s = jnp.zeros((N_HEADS, ncb), F32)
    for g in range(KV_HEADS):
        ck = ckv_ref[g * HSLOT:(g + 1) * HSLOT, :].astype(BF16)
        s = jnp.where(rg == g, jnp.dot(q, ck, preferred_element_type=F32) * SCALE, s)
    s = jnp.where(mk, s, NEG)
    mx = jnp.max(s, axis=-1, keepdims=True)
    p = jnp.where(mk, jnp.exp(s - mx), 0.0)
    pn = p / jnp.maximum(jnp.sum(p, axis=-1, keepdims=True), 1e-30)
    o = jnp.zeros((N_HEADS, HSLOT), F32)
    for g in range(KV_HEADS):
        cv = ckv_ref[(KV_HEADS + g) * HSLOT:(KV_HEADS + g + 1) * HSLOT, :].astype(BF16)
        o = jnp.where(rg == g, lax.dot_general(pn.astype(BF16), cv, _NT, preferred_element_type=F32), o)
    o_ref[...] = o
    pair = pn + pltpu.roll(pn, ncb - half, axis=1)
    valid = w < half
    sb = grp * half + w
    vis = (sb * SEL_BLOCK) <= qpos
    ri = lax.broadcasted_iota(jnp.int32, (ncb, ncb), 0)
    sb_r = (ri >> sh) * half + (ri & (nbl - 1))
    sb_c = jnp.broadcast_to(sb, (ncb, ncb))
    slot = lax.broadcasted_iota(jnp.int32, (TOP_K, 1), 0)
    for g in range(KV_HEADS):
        imp = jnp.sum(jnp.where(rg == g, pair, 0.0), axis=0, keepdims=True)
        imp = jnp.where(valid, jnp.where(vis, imp, -FORCE), -2.0 * FORCE)
        a = jnp.broadcast_to(imp, (ncb, ncb))
        bt = a.T
        beats = (bt > a) | ((bt == a) & (sb_r < sb_c))
        rank = jnp.sum(jnp.where(beats, 1.0, 0.0), axis=0, keepdims=True)
        onehot = jnp.where((rank == slot.astype(F32)) & valid, 1.0, 0.0)
        idx = jnp.sum(onehot * sb.astype(F32), axis=-1, keepdims=True)
        idx = jnp.where(slot < k_past, idx, 0.0)
        idx_ref[g * TOP_K:(g + 1) * TOP_K, :] = jnp.broadcast_to(idx, (TOP_K, LANES)).astype(jnp.int32)


def _sample_cmp(q3, ckv_t, *, bd, past, qpos, nbl):
    ncb = past // CMP_BLOCK
    k_past = min(TOP_K - 1, past // SEL_BLOCK)
    body = functools.partial(_sample_cmp_body, qpos=qpos, ncb=ncb, nbl=nbl, k_past=k_past)
    head3 = pl.BlockSpec((None, N_HEADS, HSLOT), lambda b: (b, 0, 0))
    return pl.pallas_call(
        body, grid=(bd,),
        in_specs=[head3, pl.BlockSpec((None, KVP, ncb), lambda b: (b, 0, 0))],
        out_specs=[head3, pl.BlockSpec((None, KV_HEADS * TOP_K, LANES), lambda b: (b, 0, 0))],
        out_shape=[jax.ShapeDtypeStruct((bd, N_HEADS, HSLOT), F32),
                   jax.ShapeDtypeStruct((bd, KV_HEADS * TOP_K, LANES), jnp.int32)],
        compiler_params=_cparams("parallel"), name="sample_cmp_attention",
    )(q3, ckv_t)


Q_PAD_ROWS = 8


def _pad_dt(x):
    return _pad_rows_to(x, HSLOT)


def _sample_sel_body(pt_ref, idx_ref, *refs, k_past):
    k_refs, v_refs = refs[:k_past], refs[k_past:2 * k_past]
    q_ref, knew_ref, vnew_ref, o_ref = refs[2 * k_past:]
    b, g = pl.program_id(0), pl.program_id(1)
    bpp = PAGE_SIZE // SEL_BLOCK
    q = q_ref[...].astype(BF16)
    half_of_lane = lax.broadcasted_iota(jnp.int32, (1, PAGE_SIZE), 1) // SEL_BLOCK
    s_parts, m_parts = [], []
    for s in range(k_past):
        kt = _pad_dt(k_refs[s][...]).astype(BF16)
        s_parts.append(jnp.dot(q, kt, preferred_element_type=F32) * SCALE)
        m_parts.append(jnp.broadcast_to(half_of_lane == (idx_ref[b, g, s] & (bpp - 1)), (Q_PAD_ROWS, PAGE_SIZE)))
    s_old = jnp.concatenate(s_parts, axis=1)
    mk = jnp.concatenate(m_parts, axis=1)
    s_old = jnp.where(mk, s_old, NEG)
    s_all = lax.dot_general(q, knew_ref[...].astype(BF16), _NT, preferred_element_type=F32) * SCALE
    lane = lax.broadcasted_iota(jnp.int32, s_all.shape, 1)
    s_new = jnp.sum(jnp.where(lane == b, s_all, 0.0), axis=-1, keepdims=True)
    mx = jnp.maximum(jnp.max(s_old, axis=-1, keepdims=True), s_new)
    p_old = jnp.where(mk, jnp.exp(s_old - mx), 0.0)
    p_new = jnp.exp(s_new - mx)
    den = jnp.maximum(jnp.sum(p_old, axis=-1, keepdims=True) + p_new, 1e-30)
    v_new = vnew_ref[pl.ds(b, 1), :].astype(BF16).astype(F32)
    o = p_new.astype(BF16).astype(F32) * v_new
    for s in range(k_past):
        vt = _pad_dt(v_refs[s][...]).astype(BF16)
        ps = p_old[:, s * PAGE_SIZE:(s + 1) * PAGE_SIZE].astype(BF16)
        o = o + lax.dot_general(ps, vt, _NT, preferred_element_type=F32)
    o_ref[...] = o / den


def _sample_sel(pages5, page_table, idx, layer_base, q4, kvs_new, *, bd, past):
    k_past = min(TOP_K - 1, past // SEL_BLOCK)
    bpp_shift = (PAGE_SIZE // SEL_BLOCK).bit_length() - 1

    def blk_map(s, kv):
        def f(b, g, pt, ix):
            return (layer_base + pt[b, lax.shift_right_logical(ix[b, g, s], bpp_shift)], kv, g, 0, 0)
        return f

    tile = lambda s, kv: pl.BlockSpec((None, None, None, HEAD_DIM, PAGE_SIZE), blk_map(s, kv))
    grid_spec = pltpu.PrefetchScalarGridSpec(
        num_scalar_prefetch=2, grid=(bd, KV_HEADS),
        in_specs=[tile(s, 0) for s in range(k_past)] + [tile(s, 1) for s in range(k_past)]
        + [pl.BlockSpec((None, None, Q_PAD_ROWS, HSLOT), lambda b, g, pt, ix: (b, g, 0, 0)),
           pl.BlockSpec((SAMPLE_ROWS, HSLOT), lambda b, g, pt, ix: (0, g)),
           pl.BlockSpec((SAMPLE_ROWS, HSLOT), lambda b, g, pt, ix: (0, KV_HEADS + g))],
        out_specs=pl.BlockSpec((None, None, Q_PAD_ROWS, HSLOT), lambda b, g, pt, ix: (b, g, 0, 0)))
    return pl.pallas_call(
        functools.partial(_sample_sel_body, k_past=k_past), grid_spec=grid_spec,
        out_shape=jax.ShapeDtypeStruct((bd, KV_HEADS, Q_PAD_ROWS, HSLOT), F32),
        compiler_params=_cparams("parallel", "arbitrary"), name="sample_sel_attention",
    )(page_table, idx, *([pages5] * (2 * k_past)), q4, kvs_new, kvs_new)


def _sample_win_body(q_ref, win_ref, new_ref, ocmp_ref, osel_ref, gate_ref, o_ref, *, qpos, past, wb):
    b = pl.program_id(0)
    q = q_ref[...].astype(BF16)
    qf = q.astype(F32)
    rg = _group_rows(N_HEADS)
    new = new_ref[pl.ds(b, 1), :].astype(BF16).astype(F32)
    kpos = (past - wb) + lax.broadcasted_iota(jnp.int32, (1, wb), 1)
    mk = jnp.broadcast_to((kpos <= qpos) & (kpos >= qpos - WINDOW), (N_HEADS, wb))
    s_old = jnp.zeros((N_HEADS, wb), F32)
    s_new = jnp.zeros((N_HEADS, 1), F32)
    for g in range(KV_HEADS):
        kt = _pad_dt(win_ref[g]).astype(BF16)
        s_old = jnp.where(rg == g, jnp.dot(q, kt, preferred_element_type=F32) * SCALE, s_old)
        sn = jnp.sum(qf * new[:, g * HSLOT:(g + 1) * HSLOT], axis=-1, keepdims=True) * SCALE
        s_new = jnp.where(rg == g, sn, s_new)
    s_old = jnp.where(mk, s_old, NEG)
    mx = jnp.maximum(jnp.max(s_old, axis=-1, keepdims=True), s_new)
    p_old = jnp.where(mk, jnp.exp(s_old - mx), 0.0)
    p_new = jnp.exp(s_new - mx)
    den = jnp.maximum(jnp.sum(p_old, axis=-1, keepdims=True) + p_new, 1e-30)
    o_win = jnp.zeros((N_HEADS, HSLOT), F32)
    for g in range(KV_HEADS):
        vt = _pad_dt(win_ref[KV_HEADS + g]).astype(BF16)
        og = lax.dot_general(p_old.astype(BF16), vt, _NT, preferred_element_type=F32)
        og = og + p_new.astype(BF16).astype(F32) * new[:, (KV_HEADS + g) * HSLOT:(KV_HEADS + g + 1) * HSLOT]
        o_win = jnp.where(rg == g, og, o_win)
    o_win = o_win / den
    gts = gate_ref[...]
    o_ref[...] = gts[:, 0:1] * ocmp_ref[...] + gts[:, 1:2] * osel_ref[...] + gts[:, 2:3] * o_win


def _sample_win(q3, win4, layer, kvw_new, o_cmp, o_sel, gates3, *, bd, past, qpos):
    wb = win4.shape[-1]
    head3 = pl.BlockSpec((None, N_HEADS, HSLOT), lambda b: (b, 0, 0))
    return pl.pallas_call(
        functools.partial(_sample_win_body, qpos=qpos, past=past, wb=wb), grid=(bd,),
        in_specs=[head3, pl.BlockSpec((None, 2 * KV_HEADS, HEAD_DIM, wb), lambda b: (layer * bd + b, 0, 0, 0)),
                  pl.BlockSpec((SAMPLE_ROWS, KVP), lambda b: (0, 0)), head3, head3, head3],
        out_specs=head3,
        out_shape=jax.ShapeDtypeStruct((bd, N_HEADS, HSLOT), F32),
        compiler_params=_cparams("parallel"), name="sample_win_attention",
    )(q3, win4, kvw_new, o_cmp, o_sel, gates3)


def _sample_conv_body(st_ref, u_ref, w_ref, b_ref, g_ref, bn_ref, o_ref, *, bd):
    w = w_ref[...]
    y = jnp.sum(st_ref[...] * w[None, :CONV_W - 1, :], axis=1) + u_ref[0:bd, :] * w[CONV_W - 1:CONV_W, :] + b_ref[...]
    y = _ln_rows(y, g_ref[...], bn_ref[...])
    o_ref[...] = y * jax.nn.sigmoid(y)


def _sample_conv(state, layer, u, cw, cb, cg, cbn, *, bd):
    vec = lambda a: a.reshape(1, CONV_CH)
    cst = lambda i: (0, 0)
    return pl.pallas_call(
        functools.partial(_sample_conv_body, bd=bd), grid=(1,),
        in_specs=[pl.BlockSpec((None, bd, CONV_W - 1, CONV_CH), lambda i: (layer, 0, 0, 0)),
                  pl.BlockSpec((SAMPLE_ROWS, CONV_CH), cst), pl.BlockSpec((CONV_W, CONV_CH), cst),
                  pl.BlockSpec((1, CONV_CH), cst), pl.BlockSpec((1, CONV_CH), cst), pl.BlockSpec((1, CONV_CH), cst)],
        out_specs=pl.BlockSpec((bd, CONV_CH), cst),
        out_shape=jax.ShapeDtypeStruct((bd, CONV_CH), F32),
        compiler_params=_cparams("arbitrary"), name="sample_conv",
    )(state, u, cw, vec(cb), vec(cg), vec(cbn))


def _sample_odd_body(st_ref, pin_ref, u_ref, vn_ref, pw_ref, ps_ref, w0_ref, b0_ref, o_ref, *, bd, start_pos):
    pin = pin_ref[0:bd, :]
    st = st_ref[...]
    for g, w in enumerate(POOL_WINDOWS):
        sl = slice(g * POOL_GC, (g + 1) * POOL_GC)
        tot = pin[:, sl] + jnp.sum(st[:, POOL_STATE - (w - 1):, sl], axis=1)
        d = tot / float(min(w, start_pos + 1)) - pin[:, sl]
        dp = jnp.concatenate([d, jnp.zeros((SAMPLE_ROWS - bd, POOL_GC), F32)], axis=0).astype(BF16)
        y = jnp.dot(dp, pw_ref[g], preferred_element_type=F32)[0:bd]
        o_ref[:, sl] = y * ps_ref[:, sl]
    mixed = w0_ref[...] * vn_ref[0:bd, :] + b0_ref[...]
    o_ref[:, POOL_CH:] = u_ref[0:bd, :] * mixed


def _sample_odd_mix(state, layer, pin, u, vn, pool_w, pool_scale, sgu_w, sgu_b, *, bd, start_pos):
    w0 = jnp.repeat(sgu_w[:, 0, 0], SGU_GC).reshape(1, SGU_CH)
    b0 = jnp.repeat(sgu_b[:, 0], SGU_GC).reshape(1, SGU_CH)
    cst = lambda i: (0, 0)
    return pl.pallas_call(
        functools.partial(_sample_odd_body, bd=bd, start_pos=start_pos), grid=(1,),
        in_specs=[pl.BlockSpec((None, bd, POOL_STATE, POOL_CH), lambda i: (layer, 0, 0, 0)),
                  pl.BlockSpec((SAMPLE_ROWS, POOL_CH), cst), pl.BlockSpec((SAMPLE_ROWS, SGU_CH), cst),
                  pl.BlockSpec((SAMPLE_ROWS, SGU_CH), cst),
                  pl.BlockSpec((POOL_GROUPS, POOL_GC, POOL_GC), lambda i: (0, 0, 0)),
                  pl.BlockSpec((1, POOL_CH), cst), pl.BlockSpec((1, SGU_CH), cst), pl.BlockSpec((1, SGU_CH), cst)],
        out_specs=pl.BlockSpec((bd, D_MODEL), cst),
        out_shape=jax.ShapeDtypeStruct((bd, D_MODEL), F32),
        compiler_params=_cparams("arbitrary"), name="sample_pool_sgu",
    )(state, pin, u, vn, pool_w.astype(BF16), pool_scale.reshape(1, POOL_CH), w0, b0)


def _pad_rows(x, rows):
    return jnp.pad(x, ((0, rows - x.shape[0]), (0, 0)))


def _split_to_nat(x, nheads):
    y = _unpad_heads(x, nheads)
    return _pad_nat(y.reshape(y.shape[:-1] + (nheads, HEAD_DIM))).reshape(x.shape)


def _nat_to_split(x, nheads):
    xh = x.reshape(x.shape[:-1] + (nheads, HSLOT))[..., :HEAD_DIM]
    return _pad_head(xh).reshape(x.shape)


def _even_in_proj(xb, xsb, w_in, w_gg, rope_p, rope_s, *, tm, bs, stack):
    nrep = rope_p[0].shape[0] // tm
    rs = xsb.shape[0]
    tab = lambda i, j: (i % nrep, 0)
    rope_ex = tuple((t, (tm, LANES), tab) for t in rope_p)
    rope_sx = tuple((t, (rs, LANES), lambda i, j: (0, 0)) for t in rope_s)
    gw = HPG * HSLOT
    qraw, qrot, qraw_s, qrot_s = _mm(
        xb, w_in, tm=tm, tn=2 * gw, n_off=E_Q, n_cols=QW, epilogue=_ep_q, extras=rope_ex,
        outs=((QW, BF16, 2 * gw), (QW, BF16, 2 * gw)), name="even_in_q",
        side=(xsb, _ep_q, rope_sx, ((QW, BF16, 2 * gw), (QW, BF16, 2 * gw))))
    nst = bs[1] // tm
    e, n_even, t_prev = stack
    row = {"f32": (KVP, F32, KVP), "bf16": (KVP, BF16, KVP),
           "t": ((n_even, bs[0], KVW, bs[1]), F32, (None, None, KVW, tm), lambda i, j: (e, i // nst, 0, i % nst))}
    kv_p, kv_s = [], []
    for sec, (off, want) in enumerate(zip((E_KVC, E_KVS, E_KVW), (("f32", "t"), ("bf16", "t"), ("bf16", "t")))):
        res = _mm(xb, w_in, tm=tm, tn=KVP, n_off=off, n_cols=KVP,
                  epilogue=functools.partial(_ep_kv, rope=sec > 0, want=want),
                  extras=rope_ex if sec > 0 else (), outs=tuple(row[k] for k in want),
                  name=("even_in_kvc", "even_in_kvs", "even_in_kvw")[sec],
                  side=(xsb, functools.partial(_ep_kv, rope=sec > 0, want=("f32",)),
                        rope_sx if sec > 0 else (), (row["f32"],)),
                  alias=None if t_prev is None else (t_prev[sec], 1))
        kv_p.append(res[:2])
        kv_s.append(res[2])
    gg_outs = ((CONV_CH, F32, CONV_CH), (GATE_W, F32, GATE_W))
    u, gates, u_s, gates_s = _mm(xb, w_gg, tm=tm, tn=GG_W, n_off=0, n_cols=GG_W, epilogue=_ep_glu_gates,
                                 outs=gg_outs, name="even_in_glu_gates", side=(xsb, _ep_glu_gates, (), gg_outs))
    return (qraw, qrot, kv_p, u, gates), (qraw_s, qrot_s, kv_s, u_s, gates_s)


def _mlp_up_cast(xb, xsb, w1, w2, layer, *, tm, tn):
    m, k = xb.shape
    rs = xsb.shape[0]
    dff = w1.shape[2]
    d_out = w2.shape[2]
    ni, nj = m // tm, dff // tn
    slab = dff // (ni * nj)
    assert m % tm == 0 and dff % tn == 0 and dff % (ni * nj) == 0 and slab % SAMPLE_ROWS == 0
    rsub = min(tm, ROW_SUB)

    def body(x_ref, xs_ref, w1_ref, w2_ref, h_ref, hs_ref, w2b_ref, w1b_ref):
        def act(rows_ref, rows):
            a = jnp.maximum(jnp.dot(rows_ref[rows, :], w1b_ref[...], preferred_element_type=F32), 0.0)
            return (a * a).astype(BF16)

        @pl.when(pl.program_id(1) == 0)
        def _():
            w1b_ref[...] = w1_ref[...].astype(BF16)
            hs_ref[...] = act(xs_ref, slice(0, rs))

        w2b_ref[...] = w2_ref[...].astype(BF16)
        for r in range(tm // rsub):
            rows = slice(r * rsub, (r + 1) * rsub)
            h_ref[rows, :] = act(x_ref, rows)

    return pl.pallas_call(
        body, grid=(nj, ni),
        in_specs=[pl.BlockSpec((tm, k), lambda j, i: (i, 0)), pl.BlockSpec((rs, k), lambda j, i: (0, 0)),
                  pl.BlockSpec((None, k, tn), lambda j, i: (layer, 0, j)),
                  pl.BlockSpec((None, slab, d_out), lambda j, i: (layer, j * ni + i, 0))],
        out_specs=[pl.BlockSpec((tm, tn), lambda j, i: (i, j)), pl.BlockSpec((rs, tn), lambda j, i: (0, j)),
                   pl.BlockSpec((slab, d_out), lambda j, i: (j * ni + i, 0))],
        out_shape=[jax.ShapeDtypeStruct((m, dff), BF16), jax.ShapeDtypeStruct((rs, dff), BF16),
                   jax.ShapeDtypeStruct((dff, d_out), BF16)],
        scratch_shapes=[pltpu.VMEM((k, tn), BF16)],
        compiler_params=_cparams("arbitrary", "arbitrary"), name="mlp_up_cast",
    )(xb, xsb, w1, w2)


def kernel(x_prompt, x_sample, cache_cmp_kv, cache_sel_kv, cache_win_kv, state_conv, state_pool, page_table,
           w_in_even, w_out_even, cmp_pe_k, cmp_pe_v, cmp_w_k, cmp_w_v, conv_w, conv_b, conv_ln_g, conv_ln_b,
           w_in_odd, w_out_odd, pool_w, pool_scale, sgu_ln_g, sgu_ln_b, sgu_w, sgu_b,
           mlp_w1, mlp_w2, ln_mix_g, ln_mix_b, ln_ffn_g, ln_ffn_b):
    B, S, D = x_prompt.shape
    Bd, Sd, _ = x_sample.shape
    n_pages = page_table.shape[1]
    past = n_pages * PAGE_SIZE
    n_even, n_pool = cache_cmp_kv.shape[:2]
    wb = cache_win_kv.shape[2]
    assert D == D_MODEL and Sd == 1 and Bd <= SAMPLE_ROWS
    assert S % 1024 == 0 and past % SEL_BLOCK == 0 and S >= WINDOW
    M = B * S
    Ms = SAMPLE_ROWS
    tm_p = 1024

    rope_p = _rope_tables(jnp.arange(S, dtype=jnp.int32))
    rope_s = _rope_tables(jnp.full((Ms,), past, jnp.int32))
    pps = min(32, n_pages)
    assert n_pages % pps == 0
    bsum = _block_sum_matrices(pps)
    cmp_t = cache_cmp_kv.transpose(0, 1, 3, 4, 5, 2).reshape(n_even * n_pool, KVW, PAGE_SIZE)
    sel_t = cache_sel_kv.transpose(0, 1, 3, 4, 5, 2).reshape(n_even * n_pool, 2, KV_HEADS, HEAD_DIM, PAGE_SIZE)
    win_t = cache_win_kv.transpose(0, 1, 3, 4, 5, 2).reshape(n_even * Bd, 2 * KV_HEADS, HEAD_DIM, wb)

    xp = x_prompt.reshape(M, D)
    xs = _pad_rows(x_sample.reshape(Bd, D), Ms)
    xpb, xsb = xp.astype(BF16), xs.astype(BF16)

    outs = {k: [] for k in ("cmp_s", "sel_s", "win_s", "conv_p", "conv_s", "pool_p", "pool_s", "sgu_p", "sgu_s")}
    kv6 = lambda a, lead: a.reshape(lead + (2, KV_HEADS, HEAD_DIM))
    kv_t = None

    for layer in range(DEPTH):
        if layer % 2 == 0:
            e = layer // 2
            wts = _prep_even_weights(w_in_even[e], w_out_even[e], cmp_pe_k[e], cmp_pe_v[e], cmp_w_k[e], cmp_w_v[e])
            prj_p, prj_s = _even_in_proj(xpb, xsb, wts["w_qkv"], wts["w_gg"], rope_p, rope_s, tm=tm_p, bs=(B, S),
                                         stack=(e, n_even, kv_t))
            qraw, qrot, ((kvc, kvc_t), (kvs_b, kvs_t), (kvw_b, kvw_t)), u, gates = prj_p
            kv_t = (kvc_t, kvs_t, kvw_t)
            summ = _compress_rows(kvc, wts["pe"], rows=512, name="prompt_compress")
            (ckv,) = _mm(summ.astype(BF16), wts["big_p"], tm=min(summ.shape[0], 512), tn=KVP,
                         n_off=0, n_cols=KVP, epilogue=_ep_plain, outs=((KVP, F32, KVP),), name="prompt_compress_map")
            o_att = _prompt_attention(qraw, qrot, gates, ckv, kvs_b, kvw_b, batch=B, seq=S, tq=256, tk=256, ngrp=4)
            c = _prompt_conv(u, conv_w[e], conv_b[e], conv_ln_g[e], conv_ln_b[e], batch=B, seq=S, ts=256)
            outs["conv_p"].append(u.reshape(B, S, CONV_CH)[:, S - (CONV_W - 1):])
            qraw_s, qrot_s, (kvc_s, kvs_s, kvw_s), u_s, gates_s = prj_s
            ckv_t = _compress_pages(cmp_t, page_table, wts["pe_t"], wts["big_t"], bsum, e * n_pool, pps=pps,
                                    name="sample_compress")
            q3 = qraw_s.astype(F32)[:Bd].reshape(Bd, N_HEADS, HSLOT)
            o_cmp, idx = _sample_cmp(q3, ckv_t, bd=Bd, past=past, qpos=past, nbl=pps * (PAGE_SIZE // CMP_BLOCK))
            idx = idx[:, :, 0].reshape(Bd, KV_HEADS, TOP_K)
            qr3 = _split_to_nat(qrot_s.astype(F32)[:Bd], N_HEADS).reshape(Bd, N_HEADS, HSLOT)
            q4 = jnp.pad(qr3.reshape(Bd, KV_HEADS, HPG, HSLOT), ((0, 0), (0, 0), (0, Q_PAD_ROWS - HPG), (0, 0)))
            kvs_nat = _split_to_nat(kvs_s, 2 * KV_HEADS)
            kvw_nat = _split_to_nat(kvw_s, 2 * KV_HEADS)
            o_sel = _sample_sel(sel_t, page_table, idx, e * n_pool, q4, kvs_nat, bd=Bd, past=past)
            o_sel = o_sel[:, :, :HPG].reshape(Bd, N_HEADS, HSLOT)
            g3 = gates_s[:Bd, :3 * N_HEADS].reshape(Bd, KV_HEADS, 3, HPG)
            g3 = g3.transpose(0, 1, 3, 2).reshape(Bd, N_HEADS, 3)
            g3 = jnp.pad(g3, ((0, 0), (0, 0), (0, LANES - 3)))
            o_s = _sample_win(qr3, win_t, e, kvw_nat, o_cmp, o_sel, g3, bd=Bd, past=past, qpos=past)
            c_s = _sample_conv(state_conv, e, u_s, conv_w[e], conv_b[e], conv_ln_g[e], conv_ln_b[e], bd=Bd)
            o_sb = _pad_rows(_nat_to_split(o_s.reshape(Bd, QW), N_HEADS), Ms).astype(BF16)
            c_sb = _pad_rows(c_s, Ms).astype(BF16)
            xp, xpb, xs, xsb = _proj_ln([o_att, c], [wts["wo_att"], wts["wo_conv"]], xp, ln_mix_g[layer],
                                        ln_mix_b[layer], tm=512, name="even_out_ln", side=([o_sb, c_sb], xs))
            kvc_c = _unpad_heads(kvc_s[:Bd], 2 * KV_HEADS)
            kvs_c = _unpad_heads(kvs_s[:Bd], 2 * KV_HEADS)
            kvw_c = _unpad_heads(kvw_s[:Bd], 2 * KV_HEADS)
            outs["cmp_s"].append(kv6(kvc_c, (Bd, 1)))
            outs["sel_s"].append(kv6(kvs_c, (Bd, 1)))
            wkv = jnp.concatenate([cache_win_kv[e], kv6(kvw_c, (Bd, 1))], axis=1)
            outs["win_s"].append(wkv[:, wkv.shape[1] - min(WINDOW, wkv.shape[1]):])
            outs["conv_s"].append(jnp.concatenate([state_conv[e], u_s[:Bd, None, :]], axis=1)[:, 1:])
        else:
            o = layer // 2
            w_in = w_in_odd[o]
            wb_in = w_in.astype(BF16)
            w_pu, w_v = wb_in[:, :POOL_CH + SGU_CH], wb_in[:, POOL_CH + SGU_CH:]
            w_out_p = w_out_odd[o].astype(BF16)
            lg, lb = sgu_ln_g[o].reshape(1, SGU_CH), sgu_ln_b[o].reshape(1, SGU_CH)

            gl_ex = ((lg, (1, SGU_CH), lambda i, j: (0, j)), (lb, (1, SGU_CH), lambda i, j: (0, j)))
            o_v, o_pu = ((SGU_CH, F32, SGU_CH),), ((POOL_CH, F32, POOL_CH), (SGU_CH, F32, SGU_CH))
            vn, vn_s = _mm(xpb, w_v, tm=tm_p, tn=SGU_CH, n_off=0, n_cols=SGU_CH, epilogue=_ep_gelu_gln,
                           extras=gl_ex, outs=o_v, name="odd_in_v", side=(xsb, _ep_gelu_gln, gl_ex, o_v))
            pin, uu, pin_s, uu_s = _mm(xpb, w_pu, tm=tm_p, tn=POOL_CH + SGU_CH, n_off=0, n_cols=POOL_CH + SGU_CH,
                                       epilogue=_ep_pool_gelu, outs=o_pu, name="odd_in_pool_u",
                                       side=(xsb, _ep_pool_gelu, (), o_pu))
            cat = _prompt_odd_mix(pin, uu, vn, pool_w[o], pool_scale[o], sgu_w[o], sgu_b[o], batch=B, seq=S, ts=512)
            outs["pool_p"].append(pin.reshape(B, S, POOL_CH)[:, S - POOL_STATE:])
            outs["sgu_p"].append(vn.reshape(B, S, SGU_CH)[:, ((S - 1) // CHUNK) * CHUNK:])
            cat_s = _sample_odd_mix(state_pool, o, pin_s, uu_s, vn_s, pool_w[o], pool_scale[o], sgu_w[o], sgu_b[o],
                                    bd=Bd, start_pos=past)
            xp, xpb, xs, xsb = _proj_ln([cat], [w_out_p], xp, ln_mix_g[layer], ln_mix_b[layer], tm=512,
                                        name="odd_out_ln", side=([_pad_rows(cat_s, Ms).astype(BF16)], xs))
            outs["pool_s"].append(jnp.concatenate([state_pool[o], pin_s[:Bd, None, :]], axis=1)[:, 1:])
            outs["sgu_s"].append(vn_s[:Bd, None, :])
        h, h_s, w2b = _mlp_up_cast(xpb, xsb, mlp_w1, mlp_w2, layer, tm=2 * tm_p, tn=1024)
        xp, xpb, xs, xsb = _mlp2_ln(h, w2b, xp, ln_ffn_g[layer], ln_ffn_b[layer], tm=512, tn=512,
                                    name="mlp_down_ln", side=(h_s, xs))

    st = lambda k: jnp.stack(outs[k])
    rows_last = lambda a: a.reshape(n_even, B, 2, KV_HEADS, HEAD_DIM, a.shape[-1]).transpose(0, 1, 5, 2, 3, 4)
    kvc_t, kvs_t, kvw_t = kv_t
    return (xp.reshape(B, S, D), xs[:Bd].reshape(Bd, Sd, D),
            rows_last(kvc_t), st("cmp_s"), rows_last(kvs_t), st("sel_s"),
            rows_last(kvw_t[:, :, :, S - WINDOW:]), st("win_s"),
            st("conv_p"), st("conv_s"), st("pool_p"), st("pool_s"), st("sgu_p"), st("sgu_s"))
```

```python
import functools

import jax
import jax.numpy as jnp
from jax import lax
from jax.experimental import pallas as pl
from jax.experimental.pallas import tpu as pltpu

F32 = jnp.float32
BF16 = jnp.bfloat16

D_MODEL = 2048
DEPTH = 4
PAGE_SIZE = 128
N_HEADS = 16
HEAD_DIM = 96
KV_HEADS = 4
HPG = N_HEADS // KV_HEADS
ATT_W = N_HEADS * HEAD_DIM
KVW = 2 * KV_HEADS * HEAD_DIM
CMP_BLOCK = 32
SEL_BLOCK = 64
TOP_K = 16
WINDOW = 512
ROPE_THETA = 10000.0
SCALE = HEAD_DIM ** -0.5
LOG2E = 1.4426950408889634
FORCE = 1e9
NEG = -1e30
CONV_CH = D_MODEL // 4
CONV_W = 31
POOL_CH = D_MODEL // 4
POOL_WINDOWS = (2, 4, 8, 16)
POOL_GROUPS = len(POOL_WINDOWS)
POOL_GC = POOL_CH // POOL_GROUPS
POOL_STATE = max(POOL_WINDOWS) - 1
SGU_CH = D_MODEL - POOL_CH
SGU_GROUPS = 4
SGU_GC = SGU_CH // SGU_GROUPS
CHUNK = 128
D_FF = 4 * D_MODEL
ALPHA = (2 * DEPTH) ** 0.25
LN_EPS = 1e-5

LANES = 128
SUBLANES = 8
HALF = HEAD_DIM // 2
HSLOT = LANES
HALF_OFF = LANES // 2
QW = N_HEADS * HSLOT
KVP = 2 * KV_HEADS * HSLOT
GATE_W = LANES
SAMPLE_ROWS = 16
ROW_SUB = 256
LN_ROW_SUB = 128
VMEM_LIMIT = 52 * 1024 * 1024

E_Q, E_KVC, E_KVS, E_KVW = 0, QW, QW + KVP, QW + 2 * KVP
GG_W = 2 * CONV_CH + GATE_W


def _cparams(*sem):
    return pltpu.CompilerParams(dimension_semantics=sem, vmem_limit_bytes=VMEM_LIMIT)


def _pad_head(x):
    halves = x.reshape(x.shape[:-1] + (2, HALF))
    halves = jnp.pad(halves, [(0, 0)] * (halves.ndim - 1) + [(0, HALF_OFF - HALF)])
    return halves.reshape(x.shape[:-1] + (HSLOT,))


def _pad_nat(x):
    return jnp.concatenate([x, jnp.zeros(x.shape[:-1] + (HSLOT - HEAD_DIM,), x.dtype)], axis=-1)


def _unpad_heads(x, nheads):
    xh = x.reshape(x.shape[:-1] + (nheads, HSLOT))
    y = jnp.concatenate([xh[..., :HALF], xh[..., HALF_OFF:HALF_OFF + HALF]], axis=-1)
    return y.reshape(x.shape[:-1] + (nheads * HEAD_DIM,))


def _rope_tables(pos):
    inv = jnp.power(ROPE_THETA, -jnp.arange(HALF, dtype=F32) / HALF)
    ang = pos.astype(F32)[:, None] * inv[None, :]
    cos, sin = jnp.cos(ang), jnp.sin(ang)
    z = jnp.zeros((pos.shape[0], HALF_OFF - HALF), F32)
    return (jnp.concatenate([cos, z, cos, z], axis=1),
            jnp.concatenate([-sin, z, sin, z], axis=1))


def _block_diag2(a, b):
    za = jnp.zeros((a.shape[0], b.shape[1]), a.dtype)
    zb = jnp.zeros((b.shape[0], a.shape[1]), a.dtype)
    return jnp.concatenate([jnp.concatenate([a, za], axis=1), jnp.concatenate([zb, b], axis=1)], axis=0)


def _prep_even_weights(w_in, w_out, pe_k, pe_v, w_ck, w_cv):
    d = w_in.shape[0]
    wb = w_in.astype(BF16)
    g0 = ATT_W + 3 * KVW
    n_heads_all = g0 // HEAD_DIM
    qkv = _pad_head(wb[:, :g0].reshape(d, n_heads_all, HEAD_DIM)).reshape(d, n_heads_all * HSLOT)
    gates = wb[:, g0:g0 + 3 * N_HEADS].reshape(d, 3, KV_HEADS, HPG).transpose(0, 2, 1, 3)
    gates = jnp.pad(gates.reshape(d, 3 * N_HEADS), ((0, 0), (0, GATE_W - 3 * N_HEADS)))
    w_gg = jnp.concatenate([wb[:, g0 + 3 * N_HEADS:], gates], axis=1)
    wo = w_out.astype(BF16)
    wo_att = jnp.pad(wo[:ATT_W].reshape(2 * N_HEADS, HALF, d), ((0, 0), (0, HALF_OFF - HALF), (0, 0)))
    wo_att = wo_att.reshape(QW, d)
    wo_conv = wo[ATT_W:]
    eye = jnp.eye(KV_HEADS, dtype=F32)
    pe = jnp.concatenate([jnp.tile(_pad_head(pe_k), (1, KV_HEADS)), jnp.tile(_pad_head(pe_v), (1, KV_HEADS))], axis=1)
    wk_full = _pad_head(_pad_head(w_ck).T).T
    wv_full = _pad_head(_pad_head(w_cv).T).T
    big_p = _block_diag2(jnp.kron(eye, wk_full), jnp.kron(eye, wv_full))
    pe_t = jnp.concatenate([jnp.tile(jnp.tile(pe_k.T, (1, PAGE_SIZE // CMP_BLOCK)), (KV_HEADS, 1)),
                            jnp.tile(jnp.tile(pe_v.T, (1, PAGE_SIZE // CMP_BLOCK)), (KV_HEADS, 1))], axis=0)
    big_t = _block_diag2(jnp.kron(eye, _pad_head(w_ck).T), jnp.kron(eye, _pad_nat(w_cv).T))
    return dict(w_qkv=qkv, w_gg=w_gg, wo_att=wo_att, wo_conv=wo_conv, pe=pe, pe_t=pe_t,
                big_p=big_p.astype(BF16), big_t=big_t.astype(BF16))


def _block_sum_matrices(pps):
    bpp = PAGE_SIZE // CMP_BLOCK
    nbl = bpp * pps
    p = jnp.arange(pps)[:, None, None]
    i = (jnp.arange(PAGE_SIZE) // CMP_BLOCK)[None, :, None]
    c = jnp.arange(nbl)[None, None, :]
    col = (i % 2) * (nbl // 2) + (bpp // 2) * p + i // 2
    return jnp.where(c == col, 1.0 / CMP_BLOCK, 0.0).astype(BF16)


def _ln_rows(y, g, b):
    mu = jnp.mean(y, axis=-1, keepdims=True)
    yc = y - mu
    var = jnp.mean(yc * yc, axis=-1, keepdims=True)
    return yc * lax.rsqrt(var + LN_EPS) * g + b


def _rope_slot(x, cos, sin):
    return x * cos + pltpu.roll(x, HALF_OFF, axis=1) * sin


_NT = (((1,), (1,)), ((), ()))


def _pad_rows_to(x, rows):
    return jnp.concatenate([x, jnp.zeros((rows - x.shape[0],) + x.shape[1:], x.dtype)], axis=0)


def _mm(x, w, *, tm, tn, n_off, n_cols, epilogue, extras=(), outs, name, side=None, alias=None):
    m, k = x.shape
    assert m % tm == 0 and n_cols % tn == 0 and n_off % tn == 0
    joff = n_off // tn
    ji = lambda im: (lambda j, i: im(i, j))
    in_specs = [pl.BlockSpec((tm, k), lambda j, i: (i, 0)),
                pl.BlockSpec((k, tn), lambda j, i: (0, joff + j))]
    in_specs += [pl.BlockSpec(bs, ji(im)) for _, bs, im in extras]
    operands = [x, w] + [a for a, _, _ in extras]
    out_shape, out_specs = [], []
    for o in outs:
        if len(o) == 3:
            out_shape.append(jax.ShapeDtypeStruct((m, o[0]), o[1]))
            out_specs.append(pl.BlockSpec((tm, o[2]), lambda j, i: (i, j)))
        else:
            out_shape.append(jax.ShapeDtypeStruct(o[0], o[1]))
            out_specs.append(pl.BlockSpec(o[2], ji(o[3])))
    ne, no = len(extras), len(outs)
    nse = 0
    if side is not None:
        xs, s_epilogue, s_extras, s_outs = side
        rs = xs.shape[0]
        nse = len(s_extras)
        in_specs += [pl.BlockSpec((rs, k), lambda j, i: (0, 0))]
        in_specs += [pl.BlockSpec(bs, ji(im)) for _, bs, im in s_extras]
        operands += [xs] + [a for a, _, _ in s_extras]
        for cols, dt, bc in s_outs:
            out_shape.append(jax.ShapeDtypeStruct((rs, cols), dt))
            out_specs.append(pl.BlockSpec((rs, bc), lambda j, i: (0, j)))
    rsub = min(tm, ROW_SUB)
    n_in = len(operands)
    io_alias = {}
    if alias is not None:
        io_alias = {n_in: alias[1]}
        in_specs.append(pl.BlockSpec(memory_space=pl.ANY))
        operands.append(alias[0])

    def body(*refs):
        x_ref, w_ref = refs[:2]
        ex = refs[2:2 + ne]
        o_refs = refs[len(operands):len(operands) + no]
        for r in range(tm // rsub):
            rows = slice(r * rsub, (r + 1) * rsub)
            acc = jnp.dot(x_ref[rows, :], w_ref[...], preferred_element_type=F32)
            epilogue(acc, ex, o_refs, rows)
        if side is not None:
            @pl.when(pl.program_id(1) == 0)
            def _():
                acc = jnp.dot(refs[2 + ne][...], w_ref[...], preferred_element_type=F32)
                s_epilogue(acc, refs[3 + ne:n_in], refs[len(operands) + no:], slice(0, rs))

    return pl.pallas_call(
        body, grid=(n_cols // tn, m // tm), in_specs=in_specs, out_specs=out_specs, out_shape=out_shape,
        input_output_aliases=io_alias, compiler_params=_cparams("arbitrary", "arbitrary"), name=name,
    )(*operands)


def _ep_q(acc, ex, outs, rows):
    cos, sin = ex[0][rows, :], ex[1][rows, :]
    outs[0][rows, :] = acc.astype(BF16)
    for j in range(acc.shape[1] // HSLOT):
        sl = slice(j * HSLOT, (j + 1) * HSLOT)
        outs[1][rows, sl] = _rope_slot(acc[:, sl], cos, sin).astype(BF16)


def _ep_kv(acc, ex, outs, rows, *, rope, want):
    o = dict(zip(want, outs))
    if rope:
        cos, sin = ex[0][rows, :], ex[1][rows, :]
    for j in range(2 * KV_HEADS):
        sl = slice(j * HSLOT, (j + 1) * HSLOT)
        x = acc[:, sl]
        if rope and j < KV_HEADS:
            x = _rope_slot(x, cos, sin)
        if "f32" in o:
            o["f32"][rows, sl] = x
        if "bf16" in o:
            o["bf16"][rows, sl] = x.astype(BF16)
        if "t" in o:
            xt = x.T
            o["t"][j * HEAD_DIM:j * HEAD_DIM + HALF, rows] = xt[0:HALF]
            o["t"][j * HEAD_DIM + HALF:(j + 1) * HEAD_DIM, rows] = xt[HALF_OFF:HALF_OFF + HALF]


def _ep_glu_gates(acc, ex, outs, rows):
    outs[0][rows, :] = acc[:, :CONV_CH] * jax.nn.sigmoid(acc[:, CONV_CH:2 * CONV_CH])
    outs[1][rows, :] = jax.nn.sigmoid(acc[:, 2 * CONV_CH:])


def _ep_plain(acc, ex, outs, rows):
    outs[0][rows, :] = acc.astype(outs[0].dtype)


def _ep_relu2(acc, ex, outs, rows):
    r = jnp.maximum(acc, 0.0)
    outs[0][rows, :] = (r * r).astype(outs[0].dtype)


def _ep_pool_gelu(acc, ex, outs, rows):
    outs[0][rows, :] = acc[:, :POOL_CH]
    outs[1][rows, :] = jax.nn.gelu(acc[:, POOL_CH:])


def _ep_gelu_gln(acc, ex, outs, rows):
    g, b = ex[0][...], ex[1][...]
    v = jax.nn.gelu(acc)
    for j in range(acc.shape[1] // SGU_GC):
        sl = slice(j * SGU_GC, (j + 1) * SGU_GC)
        outs[0][rows, sl] = _ln_rows(v[:, sl], g[:, sl], b[:, sl])


def _matmul_res_ln(a_list, w_list, resid, g, b, *, tm, tn, name, side):
    m, n = resid.shape
    nj = n // tn
    as_list, resid_s = side
    rs = resid_s.shape[0]
    npair = len(a_list)
    rsub, rsub_ln = min(tm, ROW_SUB), min(tm, LN_ROW_SUB)

    def body(*refs):
        h_ref, w_refs = refs[:npair], refs[npair:2 * npair]
        r_ref, g_ref, b_ref = refs[2 * npair:2 * npair + 3]
        hs_ref = refs[2 * npair + 3:3 * npair + 3]
        rs_ref, o_ref, ob_ref, os_ref, osb_ref = refs[3 * npair + 3:]
        j = pl.program_id(1)
        first = pl.program_id(0) == 0
        srows = slice(0, rs)

        def tile(hr, rr, rows):
            acc = ALPHA * rr[rows, :]
            for a_ref, w_ref in zip(hr, w_refs):
                acc = acc + jnp.dot(a_ref[rows, :], w_ref[...], preferred_element_type=F32)
            return acc

        def finish(o, ob, y, rows):
            y = _ln_rows(y, g_ref[...], b_ref[...])
            o[rows, :] = y
            ob[rows, :] = y.astype(BF16)

        for jj in range(nj - 1):
            @pl.when(j == jj)
            def _(jj=jj):
                cols = slice(jj * tn, (jj + 1) * tn)
                for r in range(tm // rsub):
                    rows = slice(r * rsub, (r + 1) * rsub)
                    o_ref[rows, cols] = tile(h_ref, r_ref, rows)

                @pl.when(first)
                def _():
                    os_ref[:, cols] = tile(hs_ref, rs_ref, srows)

        @pl.when(j == nj - 1)
        def _():
            done = slice(0, (nj - 1) * tn)
            for r in range(tm // rsub_ln):
                rows = slice(r * rsub_ln, (r + 1) * rsub_ln)
                finish(o_ref, ob_ref, jnp.concatenate([o_ref[rows, done], tile(h_ref, r_ref, rows)], axis=1), rows)

            @pl.when(first)
            def _():
                finish(os_ref, osb_ref, jnp.concatenate([os_ref[:, done], tile(hs_ref, rs_ref, srows)], axis=1), srows)

    assert nj > 1
    row = pl.BlockSpec((tm, n), lambda i, j: (i, 0))
    srow = pl.BlockSpec((rs, n), lambda i, j: (0, 0))
    cst = lambda i, j: (0, 0)
    in_specs = [pl.BlockSpec((tm, a.shape[1]), lambda i, j: (i, 0)) for a in a_list]
    in_specs += [pl.BlockSpec((w.shape[0], tn), lambda i, j: (0, j)) for w in w_list]
    in_specs += [pl.BlockSpec((tm, tn), lambda i, j: (i, j)), pl.BlockSpec((1, n), cst), pl.BlockSpec((1, n), cst)]
    in_specs += [pl.BlockSpec((rs, a.shape[1]), cst) for a in as_list]
    in_specs += [pl.BlockSpec((rs, tn), lambda i, j: (0, j))]
    return pl.pallas_call(
        body, grid=(m // tm, nj), in_specs=in_specs, out_specs=[row, row, srow, srow],
        out_shape=[jax.ShapeDtypeStruct((m, n), F32), jax.ShapeDtypeStruct((m, n), BF16),
                   jax.ShapeDtypeStruct((rs, n), F32), jax.ShapeDtypeStruct((rs, n), BF16)],
        compiler_params=_cparams("arbitrary", "arbitrary"), name=name,
    )(*a_list, *w_list, resid, g.reshape(1, n), b.reshape(1, n), *as_list, resid_s)


def _compress_rows(kvc, pe, *, rows, name):
    m, c = kvc.shape
    nb = rows // CMP_BLOCK

    def body(x_ref, pe_ref, o_ref):
        x = x_ref[...].reshape(nb, CMP_BLOCK, c) * pe_ref[...][None]
        o_ref[...] = jnp.sum(x, axis=1) * (1.0 / CMP_BLOCK)

    return pl.pallas_call(
        body, grid=(m // rows,),
        in_specs=[pl.BlockSpec((rows, c), lambda i: (i, 0)), pl.BlockSpec((CMP_BLOCK, c), lambda i: (0, 0))],
        out_specs=pl.BlockSpec((nb, c), lambda i: (i, 0)),
        out_shape=jax.ShapeDtypeStruct((m // CMP_BLOCK, c), F32),
        compiler_params=_cparams("parallel"), name=name,
    )(kvc, pe)


def _compress_pages(pages_t, page_table, pe_t, big_t, bsum, layer_base, *, pps, name):
    bd, n_pages = page_table.shape
    bpp = PAGE_SIZE // CMP_BLOCK
    nbl = bpp * pps
    nsteps = n_pages // pps

    def body(pt_ref, *refs):
        page_refs = refs[:pps]
        pe_ref, big_ref, bsum_ref, o_ref = refs[pps:]
        acc = jnp.zeros((KVW, nbl), F32)
        for p in range(pps):
            x = page_refs[p][...] * pe_ref[...]
            acc = acc + jnp.dot(x.astype(BF16), bsum_ref[p], preferred_element_type=F32)
        o_ref[...] = jnp.dot(big_ref[...], acc.astype(BF16), preferred_element_type=F32)

    def page_map(p):
        return lambda b, j, pt: (layer_base + pt[b, j * pps + p], 0, 0)

    cst2 = lambda b, j, pt: (0, 0)
    grid_spec = pltpu.PrefetchScalarGridSpec(
        num_scalar_prefetch=1, grid=(bd, nsteps),
        in_specs=[pl.BlockSpec((None, KVW, PAGE_SIZE), page_map(p)) for p in range(pps)]
        + [pl.BlockSpec((KVW, PAGE_SIZE), cst2), pl.BlockSpec((KVP, KVW), cst2),
           pl.BlockSpec((pps, PAGE_SIZE, nbl), lambda b, j, pt: (0, 0, 0))],
        out_specs=pl.BlockSpec((None, KVP, nbl), lambda b, j, pt: (b, 0, j)))
    return pl.pallas_call(
        body, grid_spec=grid_spec,
        out_shape=jax.ShapeDtypeStruct((bd, KVP, n_pages * bpp), F32),
        compiler_params=_cparams("parallel", "arbitrary"), name=name,
    )(page_table, *([pages_t] * pps), pe_t, big_t, bsum)


def _flash_step_t(q, k, vt, valid, m_ref, l_ref, acc_ref):
    bias = jnp.where(valid, 0.0, NEG)
    s = jnp.dot(k, q, preferred_element_type=F32) + jnp.concatenate([bias] * HPG, axis=1)
    m_prev = m_ref[...]
    m_new = jnp.maximum(m_prev, jnp.max(s, axis=0, keepdims=True))
    alpha = jnp.exp2(m_prev - m_new)
    p = jnp.exp2(s - m_new)
    l_ref[...] = alpha * l_ref[...] + jnp.sum(p, axis=0, keepdims=True)
    acc_ref[...] = alpha * acc_ref[...] + jnp.dot(vt, p.astype(BF16), preferred_element_type=F32)
    m_ref[...] = m_new


def _attn_body(*refs, tq, tk, seq, k_top, ngrp):
    n_in = 9
    ins = [refs[g * n_in:(g + 1) * n_in] for g in range(ngrp)]
    o_ref = refs[ngrp * n_in]
    m_ref, l_ref, acc_ref, sel_ref, vts_ref, vtw_ref = refs[ngrp * n_in + 1:]
    i = pl.program_id(2)
    cols = HPG * tq
    nsb = seq // SEL_BLOCK
    nt = seq // tk
    bpt = tk // SEL_BLOCK
    q0 = i * tq
    lane_q = lax.broadcasted_iota(jnp.int32, (1, cols), 1)
    qpos = q0 + (lane_q & (tq - 1))
    qp1 = q0 + lax.broadcasted_iota(jnp.int32, (1, tq), 1)

    @pl.when(i == 0)
    def _():
        def tr(t, c):
            for g in range(ngrp):
                vs_ref, vw_ref = ins[g][6], ins[g][8]
                for h in range(tk // LANES):
                    k0 = pl.multiple_of(t * tk + h * LANES, LANES)
                    hs = slice(h * LANES, (h + 1) * LANES)
                    vts_ref[g, t, :, hs] = vs_ref[pl.ds(k0, LANES), :].astype(F32).T.astype(BF16)
                    vtw_ref[g, t, :, hs] = vw_ref[pl.ds(k0, LANES), :].astype(F32).T.astype(BF16)
            return c
        lax.fori_loop(0, nt, tr, 0)

    def heads_t(ref, scale=1.0):
        parts = [ref[:, j * HSLOT:(j + 1) * HSLOT].astype(F32).T * scale for j in range(HPG)]
        return jnp.concatenate(parts, axis=1).astype(BF16)

    r = lax.broadcasted_iota(jnp.int32, (2 * nsb, 1), 0)
    n_of = jnp.where(r < nsb, 2 * r, 2 * (r - nsb) + 1)
    mk = ((n_of + 1) * CMP_BLOCK - 1) <= qpos
    sb = lax.broadcasted_iota(jnp.int32, (nsb, 1), 0)
    vis = (sb * SEL_BLOCK) <= qp1
    cur = sb == (qp1 >> (SEL_BLOCK.bit_length() - 1))

    def compressed_branch(g):
        qraw_ref, ck_ref, cv_ref = ins[g][0], ins[g][3], ins[g][4]
        qr = heads_t(qraw_ref)
        ck = jnp.concatenate([ck_ref[pl.ds(0, nsb, stride=2), :], ck_ref[pl.ds(1, nsb, stride=2), :]], axis=0)
        s = jnp.dot(ck.astype(BF16), qr, preferred_element_type=F32) * SCALE
        s = jnp.where(mk, s, NEG)
        mx = jnp.max(s, axis=0, keepdims=True)
        p = jnp.where(mk, jnp.exp(s - mx), 0.0)
        pn = p * (1.0 / jnp.maximum(jnp.sum(p, axis=0, keepdims=True), 1e-30))
        cv = jnp.concatenate([cv_ref[pl.ds(0, nsb, stride=2), :], cv_ref[pl.ds(1, nsb, stride=2), :]], axis=0)
        cvt = _pad_rows_to(cv, LANES).T.astype(BF16)
        o_cmp = jnp.dot(cvt, _pad_rows_to(pn, LANES).astype(BF16), preferred_element_type=F32)
        pp = pn[0:nsb] + pn[nsb:2 * nsb]
        imp = pp[:, 0:tq]
        for j in range(1, HPG):
            imp = imp + pp[:, j * tq:(j + 1) * tq]
        imp = jnp.where(cur, FORCE, jnp.where(vis, imp, -FORCE))
        cnt = jnp.zeros((nsb, tq), F32)
        for j in range(nsb):
            rowj = imp[j:j + 1, :]
            beats = (rowj > imp) | ((rowj == imp) & (j < sb))
            cnt = cnt + jnp.where(beats, 1.0, 0.0)
        sel = jnp.where(cnt < k_top, 1.0, 0.0)
        for t in range(nt):
            sel_ref[g, t, 0:bpt, :] = sel[bpt * t:bpt * (t + 1), :]
        return o_cmp

    o_cmp = [compressed_branch(g) for g in range(ngrp)]
    qt = [heads_t(ins[g][1], SCALE * LOG2E) for g in range(ngrp)]
    rowk = lax.broadcasted_iota(jnp.int32, (tk, 1), 0)

    def reset():
        m_ref[...] = jnp.full((ngrp, 1, cols), NEG, F32)
        l_ref[...] = jnp.zeros((ngrp, 1, cols), F32)
        acc_ref[...] = jnp.zeros((ngrp, HSLOT, cols), F32)

    def result(g):
        return acc_ref[g] * (1.0 / jnp.maximum(l_ref[g], 1e-30))

    reset()

    def sel_step(t, c):
        k0 = pl.multiple_of(t * tk, tk)
        kp = k0 + rowk
        for g in range(ngrp):
            sm = sel_ref[g, t, bpt - 1:bpt, :]
            for j in range(bpt - 2, -1, -1):
                sm = jnp.where(rowk < (j + 1) * SEL_BLOCK, sel_ref[g, t, j:j + 1, :], sm)
            valid = (kp <= qp1) & (sm > 0.5)
            _flash_step_t(qt[g], ins[g][5][pl.ds(k0, tk), :], vts_ref[g, t], valid,
                          m_ref.at[g], l_ref.at[g], acc_ref.at[g])
        return c

    t_end = lax.div(q0 + tq + tk - 1, tk)
    lax.fori_loop(0, t_end, sel_step, 0)
    o_sel = [result(g) for g in range(ngrp)]

    reset()

    def win_step(t, c):
        k0 = pl.multiple_of(t * tk, tk)
        kp = k0 + rowk
        valid = (kp <= qp1) & (kp >= qp1 - WINDOW)
        for g in range(ngrp):
            _flash_step_t(qt[g], ins[g][7][pl.ds(k0, tk), :], vtw_ref[g, t], valid,
                          m_ref.at[g], l_ref.at[g], acc_ref.at[g])
        return c

    lax.fori_loop(lax.div(jnp.maximum(q0 - WINDOW, 0), tk), t_end, win_step, 0)

    gw = HPG * HSLOT
    for g in range(ngrp):
        gt = ins[g][2][...].T
        o = jnp.zeros((HSLOT, cols), F32)
        for br, o_br in enumerate((o_cmp[g], o_sel[g], result(g))):
            r0 = (g * 3 + br) * HPG
            grow = jnp.concatenate([gt[r0 + j:r0 + j + 1, :] for j in range(HPG)], axis=1)
            o = o + grow * o_br
        for j in range(HPG):
            o_ref[:, g * gw + j * HSLOT:g * gw + (j + 1) * HSLOT] = o[:, j * tq:(j + 1) * tq].T.astype(BF16)


def _prompt_attention(qraw, qrot, gates, ckv, kvs_b, kvw_b, *, batch, seq, tq, tk, ngrp):
    nq = seq // tq
    ncb = seq // CMP_BLOCK
    nt = seq // tk
    cols = HPG * tq
    k_top = min(TOP_K, seq // SEL_BLOCK)
    assert tq & (tq - 1) == 0 and seq % tk == 0 and tk % LANES == 0 and tk // SEL_BLOCK <= SUBLANES
    assert ngrp == KV_HEADS
    body = functools.partial(_attn_body, tq=tq, tk=tk, seq=seq, k_top=k_top, ngrp=ngrp)
    gw = HPG * HSLOT
    in_specs, operands = [], []
    for g in range(ngrp):
        qmap = functools.partial(lambda b, p, i, g: (b * nq + i, p * ngrp + g), g=g)
        kmap = functools.partial(lambda b, p, i, g: (b, p * ngrp + g), g=g)
        vmap_ = functools.partial(lambda b, p, i, g: (b, KV_HEADS + p * ngrp + g), g=g)
        in_specs += [pl.BlockSpec((tq, gw), qmap), pl.BlockSpec((tq, gw), qmap),
                     pl.BlockSpec((tq, GATE_W), lambda b, p, i: (b * nq + i, 0)),
                     pl.BlockSpec((ncb, HSLOT), kmap), pl.BlockSpec((ncb, HSLOT), vmap_),
                     pl.BlockSpec((seq, HSLOT), kmap), pl.BlockSpec((seq, HSLOT), vmap_),
                     pl.BlockSpec((seq, HSLOT), kmap), pl.BlockSpec((seq, HSLOT), vmap_)]
        operands += [qraw, qrot, gates, ckv, ckv, kvs_b, kvs_b, kvw_b, kvw_b]
    return pl.pallas_call(
        body, grid=(batch, KV_HEADS // ngrp, nq), in_specs=in_specs,
        out_specs=pl.BlockSpec((tq, ngrp * gw), lambda b, p, i: (b * nq + i, p)),
        out_shape=jax.ShapeDtypeStruct((batch * seq, QW), BF16),
        scratch_shapes=[pltpu.VMEM((ngrp, 1, cols), F32), pltpu.VMEM((ngrp, 1, cols), F32),
                        pltpu.VMEM((ngrp, HSLOT, cols), F32), pltpu.VMEM((ngrp, nt, SUBLANES, tq), F32),
                        pltpu.VMEM((ngrp, nt, HSLOT, tk), BF16), pltpu.VMEM((ngrp, nt, HSLOT, tk), BF16)],
        compiler_params=_cparams("parallel", "parallel", "arbitrary"), name="prompt_attention",
    )(*operands)


CONV_HALO = 32


def _conv_body(cur_ref, prev_ref, w_ref, b_ref, g_ref, bn_ref, o_ref, ext_ref, *, ts):
    c = pl.program_id(1)
    span = CONV_HALO + ts - SUBLANES
    ext_ref[0, 0:CONV_HALO, :] = jnp.where(c > 0, prev_ref[...], 0.0)
    ext_ref[0, CONV_HALO:CONV_HALO + ts, :] = cur_ref[...]
    for s in range(1, SUBLANES):
        ext_ref[s, 0:span, :] = ext_ref[0, pl.ds(s, span), :]
    acc = jnp.zeros((ts, CONV_CH), F32) + b_ref[...]
    off = CONV_HALO - (CONV_W - 1)
    for k in range(CONV_W):
        s = (off + k) % SUBLANES
        acc = acc + ext_ref[s, pl.ds(off + k - s, ts), :] * w_ref[k:k + 1, :]
    y = _ln_rows(acc, g_ref[...], bn_ref[...])
    o_ref[...] = (y * jax.nn.sigmoid(y)).astype(BF16)


def _prompt_conv(u, cw, cb, cg, cbn, *, batch, seq, ts):
    nt = seq // ts
    r = ts // CONV_HALO
    cwp = jnp.pad(cw, ((0, CONV_HALO - CONV_W), (0, 0)))
    vec = lambda a: a.reshape(1, CONV_CH)
    cst = lambda b, c: (0, 0)
    return pl.pallas_call(
        functools.partial(_conv_body, ts=ts), grid=(batch, nt),
        in_specs=[pl.BlockSpec((ts, CONV_CH), lambda b, c: (b * nt + c, 0)),
                  pl.BlockSpec((CONV_HALO, CONV_CH), lambda b, c: (jnp.maximum((b * nt + c) * r - 1, 0), 0)),
                  pl.BlockSpec((CONV_HALO, CONV_CH), cst),
                  pl.BlockSpec((1, CONV_CH), cst), pl.BlockSpec((1, CONV_CH), cst), pl.BlockSpec((1, CONV_CH), cst)],
        out_specs=pl.BlockSpec((ts, CONV_CH), lambda b, c: (b * nt + c, 0)),
        out_shape=jax.ShapeDtypeStruct((batch * seq, CONV_CH), BF16),
        scratch_shapes=[pltpu.VMEM((SUBLANES, CONV_HALO + ts, CONV_CH), F32)],
        compiler_params=_cparams("parallel", "arbitrary"), name="prompt_conv",
    )(u, u, cwp, vec(cb), vec(cg), vec(cbn))


POOL_HALO = 16


def _odd_mix_body(pin_ref, prev_ref, u_ref, vn_ref, pw_ref, ps_ref, sw_ref, sb_ref, o_ref, ext_ref, *, ts):
    c = pl.program_id(1)
    ext_ref[0:POOL_HALO, :] = jnp.where(c > 0, prev_ref[...], 0.0)
    ext_ref[POOL_HALO:POOL_HALO + ts, :] = pin_ref[...]
    t = c * ts + lax.broadcasted_iota(jnp.int32, (ts, 1), 0)
    for g, w in enumerate(POOL_WINDOWS):
        sl = slice(g * POOL_GC, (g + 1) * POOL_GC)
        tot = ext_ref[pl.ds(POOL_HALO, ts), sl]
        for j in range(1, w):
            tot = tot + ext_ref[pl.ds(POOL_HALO - j, ts), sl]
        cnt = jnp.minimum(w, t + 1).astype(F32)
        d = tot / cnt - pin_ref[:, sl]
        y = jnp.dot(d.astype(BF16), pw_ref[g], preferred_element_type=F32)
        o_ref[:, sl] = (y * ps_ref[:, sl]).astype(BF16)
    ri = lax.broadcasted_iota(jnp.int32, (CHUNK, CHUNK), 0)
    ci = lax.broadcasted_iota(jnp.int32, (CHUNK, CHUNK), 1)
    for g in range(SGU_GROUPS):
        sl = slice(g * SGU_GC, (g + 1) * SGU_GC)
        ws = jnp.where(ci <= ri, sw_ref[g], 0.0).astype(BF16)
        for q in range(ts // CHUNK):
            rows = slice(q * CHUNK, (q + 1) * CHUNK)
            mixed = jnp.dot(ws, vn_ref[rows, sl].astype(BF16), preferred_element_type=F32) + sb_ref[:, g:g + 1]
            o_ref[rows, POOL_CH + g * SGU_GC:POOL_CH + (g + 1) * SGU_GC] = (u_ref[rows, sl] * mixed).astype(BF16)


def _prompt_odd_mix(pin, u, vn, pool_w, pool_scale, sgu_w, sgu_b, *, batch, seq, ts):
    assert seq % ts == 0 and ts % CHUNK == 0
    nt = seq // ts
    r = ts // POOL_HALO
    cst2 = lambda b, c: (0, 0)
    cst3 = lambda b, c: (0, 0, 0)
    row = lambda b, c: (b * nt + c, 0)
    return pl.pallas_call(
        functools.partial(_odd_mix_body, ts=ts), grid=(batch, nt),
        in_specs=[pl.BlockSpec((ts, POOL_CH), row),
                  pl.BlockSpec((POOL_HALO, POOL_CH), lambda b, c: (jnp.maximum((b * nt + c) * r - 1, 0), 0)),
                  pl.BlockSpec((ts, SGU_CH), row), pl.BlockSpec((ts, SGU_CH), row),
                  pl.BlockSpec((POOL_GROUPS, POOL_GC, POOL_GC), cst3), pl.BlockSpec((1, POOL_CH), cst2),
                  pl.BlockSpec((SGU_GROUPS, CHUNK, CHUNK), cst3), pl.BlockSpec((CHUNK, SGU_GROUPS), cst2)],
        out_specs=pl.BlockSpec((ts, D_MODEL), row),
        out_shape=jax.ShapeDtypeStruct((batch * seq, D_MODEL), BF16),
        scratch_shapes=[pltpu.VMEM((POOL_HALO + ts, POOL_CH), F32)],
        compiler_params=_cparams("parallel", "arbitrary"), name="prompt_pool_sgu",
    )(pin, pin, u, vn, pool_w.astype(BF16), pool_scale.reshape(1, POOL_CH), sgu_w, sgu_b.T)


def _group_rows(nrows):
    return lax.broadcasted_iota(jnp.int32, (nrows, 1), 0) >> (HPG.bit_length() - 1)


def _sample_cmp_body(q_ref, ckv_ref, o_ref, idx_ref, *, qpos, ncb, nbl, k_past):
    q = q_ref[...].astype(BF16)
    rg = _group_rows(N_HEADS)
    half = nbl // 2
    assert nbl & (nbl - 1) == 0
    sh = nbl.bit_length() - 1
    lane = lax.broadcasted_iota(jnp.int32, (1, ncb), 1)
    grp, w = lane >> sh, lane & (nbl - 1)
    n_cmp = grp * nbl + 2 * (w & (half - 1)) + (w >> (sh - 1))
    mk = jnp.broadcast_to(((n_cmp + 1) * CMP_BLOCK - 1) <= qpos, (N_HEADS, ncb))
    s = jnp.zeros((N_HEADS, ncb), F32)
    for g in range(KV_HEADS):
        ck = ckv_ref[g * HSLOT:(g + 1) * HSLOT, :].astype(BF16)
        s = jnp.where(rg == g, jnp.dot(q, ck, preferred_element_type=F32) * SCALE, s)
    s = jnp.where(mk, s, NEG)
    mx = jnp.max(s, axis=-1, keepdims=True)
    p = jnp.where(mk, jnp.exp(s - mx), 0.0)
    pn = p / jnp.maximum(jnp.sum(p, axis=-1, keepdims=True), 1e-30)
    o = jnp.zeros((N_HEADS, HSLOT), F32)
    for g in range(KV_HEADS):
        cv = ckv_ref[(KV_HEADS + g) * HSLOT:(KV_HEADS + g + 1) * HSLOT, :].astype(BF16)
        o = jnp.where(rg == g, lax.dot_general(pn.astype(BF16), cv, _NT, preferred_element_type=F32), o)
    o_ref[...] = o
    pair = pn + pltpu.roll(pn, ncb - half, axis=1)
    valid = w < half
    sb = grp * half + w
    vis = (sb * SEL_BLOCK) <= qpos
    ri = lax.broadcasted_iota(jnp.int32, (ncb, ncb), 0)
    sb_r = (ri >> sh) * half + (ri & (nbl - 1))
    sb_c = jnp.broadcast_to(sb, (ncb, ncb))
    slot = lax.broadcasted_iota(jnp.int32, (TOP_K, 1), 0)
    for g in range(KV_HEADS):
        imp = jnp.sum(jnp.where(rg == g, pair, 0.0), axis=0, keepdims=True)
        imp = jnp.where(valid, jnp.where(vis, imp, -FORCE), -2.0 * FORCE)
        a = jnp.broadcast_to(imp, (ncb, ncb))
        bt = a.T
        beats = (bt > a) | ((bt == a) & (sb_r < sb_c))
        rank = jnp.sum(jnp.where(beats, 1.0, 0.0), axis=0, keepdims=True)
        onehot = jnp.where((rank == slot.astype(F32)) & valid, 1.0, 0.0)
        idx = jnp.sum(onehot * sb.astype(F32), axis=-1, keepdims=True)
        idx = jnp.where(slot < k_past, idx, 0.0)
        idx_ref[g * TOP_K:(g + 1) * TOP_K, :] = jnp.broadcast_to(idx, (TOP_K, LANES)).astype(jnp.int32)


def _sample_cmp(q3, ckv_t, *, bd, past, qpos, nbl):
    ncb = past // CMP_BLOCK
    k_past = min(TOP_K - 1, past // SEL_BLOCK)
    body = functools.partial(_sample_cmp_body, qpos=qpos, ncb=ncb, nbl=nbl, k_past=k_past)
    head3 = pl.BlockSpec((None, N_HEADS, HSLOT), lambda b: (b, 0, 0))
    return pl.pallas_call(
        body, grid=(bd,),
        in_specs=[head3, pl.BlockSpec((None, KVP, ncb), lambda b: (b, 0, 0))],
        out_specs=[head3, pl.BlockSpec((None, KV_HEADS * TOP_K, LANES), lambda b: (b, 0, 0))],
        out_shape=[jax.ShapeDtypeStruct((bd, N_HEADS, HSLOT), F32),
                   jax.ShapeDtypeStruct((bd, KV_HEADS * TOP_K, LANES), jnp.int32)],
        compiler_params=_cparams("parallel"), name="sample_cmp_attention",
    )(q3, ckv_t)


Q_PAD_ROWS = 8


def _pad_dt(x):
    return _pad_rows_to(x, HSLOT)


def _sample_sel_body(pt_ref, idx_ref, *refs, k_past):
    k_refs, v_refs = refs[:k_past], refs[k_past:2 * k_past]
    q_ref, knew_ref, vnew_ref, o_ref = refs[2 * k_past:]
    b, g = pl.program_id(0), pl.program_id(1)
    bpp = PAGE_SIZE // SEL_BLOCK
    q = q_ref[...].astype(BF16)
    half_of_lane = lax.broadcasted_iota(jnp.int32, (1, PAGE_SIZE), 1) // SEL_BLOCK
    s_parts, m_parts = [], []
    for s in range(k_past):
        kt = _pad_dt(k_refs[s][...]).astype(BF16)
        s_parts.append(jnp.dot(q, kt, preferred_element_type=F32) * SCALE)
        m_parts.append(jnp.broadcast_to(half_of_lane == (idx_ref[b, g, s] & (bpp - 1)), (Q_PAD_ROWS, PAGE_SIZE)))
    s_old = jnp.concatenate(s_parts, axis=1)
    mk = jnp.concatenate(m_parts, axis=1)
    s_old = jnp.where(mk, s_old, NEG)
    s_all = lax.dot_general(q, knew_ref[...].astype(BF16), _NT, preferred_element_type=F32) * SCALE
    lane = lax.broadcasted_iota(jnp.int32, s_all.shape, 1)
    s_new = jnp.sum(jnp.where(lane == b, s_all, 0.0), axis=-1, keepdims=True)
    mx = jnp.maximum(jnp.max(s_old, axis=-1, keepdims=True), s_new)
    p_old = jnp.where(mk, jnp.exp(s_old - mx), 0.0)
    p_new = jnp.exp(s_new - mx)
    den = jnp.maximum(jnp.sum(p_old, axis=-1, keepdims=True) + p_new, 1e-30)
    v_new = vnew_ref[pl.ds(b, 1), :].astype(BF16).astype(F32)
    o = p_new.astype(BF16).astype(F32) * v_new
    for s in range(k_past):
        vt = _pad_dt(v_refs[s][...]).astype(BF16)
        ps = p_old[:, s * PAGE_SIZE:(s + 1) * PAGE_SIZE].astype(BF16)
        o = o + lax.dot_general(ps, vt, _NT, preferred_element_type=F32)
    o_ref[...] = o / den


def _sample_sel(pages5, page_table, idx, layer_base, q4, kvs_new, *, bd, past):
    k_past = min(TOP_K - 1, past // SEL_BLOCK)
    bpp_shift = (PAGE_SIZE // SEL_BLOCK).bit_length() - 1

    def blk_map(s, kv):
        def f(b, g, pt, ix):
            return (layer_base + pt[b, lax.shift_right_logical(ix[b, g, s], bpp_shift)], kv, g, 0, 0)
        return f

    tile = lambda s, kv: pl.BlockSpec((None, None, None, HEAD_DIM, PAGE_SIZE), blk_map(s, kv))
    grid_spec = pltpu.PrefetchScalarGridSpec(
        num_scalar_prefetch=2, grid=(bd, KV_HEADS),
        in_specs=[tile(s, 0) for s in range(k_past)] + [tile(s, 1) for s in range(k_past)]
        + [pl.BlockSpec((None, None, Q_PAD_ROWS, HSLOT), lambda b, g, pt, ix: (b, g, 0, 0)),
           pl.BlockSpec((SAMPLE_ROWS, HSLOT), lambda b, g, pt, ix: (0, g)),
           pl.BlockSpec((SAMPLE_ROWS, HSLOT), lambda b, g, pt, ix: (0, KV_HEADS + g))],
        out_specs=pl.BlockSpec((None, None, Q_PAD_ROWS, HSLOT), lambda b, g, pt, ix: (b, g, 0, 0)))
    return pl.pallas_call(
        functools.partial(_sample_sel_body, k_past=k_past), grid_spec=grid_spec,
        out_shape=jax.ShapeDtypeStruct((bd, KV_HEADS, Q_PAD_ROWS, HSLOT), F32),
        compiler_params=_cparams("parallel", "arbitrary"), name="sample_sel_attention",
    )(page_table, idx, *([pages5] * (2 * k_past)), q4, kvs_new, kvs_new)


def _sample_win_body(q_ref, win_ref, new_ref, ocmp_ref, osel_ref, gate_ref, o_ref, *, qpos, past, wb):
    b = pl.program_id(0)
    q = q_ref[...].astype(BF16)
    qf = q.astype(F32)
    rg = _group_rows(N_HEADS)
    new = new_ref[pl.ds(b, 1), :].astype(BF16).astype(F32)
    kpos = (past - wb) + lax.broadcasted_iota(jnp.int32, (1, wb), 1)
    mk = jnp.broadcast_to((kpos <= qpos) & (kpos >= qpos - WINDOW), (N_HEADS, wb))
    s_old = jnp.zeros((N_HEADS, wb), F32)
    s_new = jnp.zeros((N_HEADS, 1), F32)
    for g in range(KV_HEADS):
        kt = _pad_dt(win_ref[g]).astype(BF16)
        s_old = jnp.where(rg == g, jnp.dot(q, kt, preferred_element_type=F32) * SCALE, s_old)
        sn = jnp.sum(qf * new[:, g * HSLOT:(g + 1) * HSLOT], axis=-1, keepdims=True) * SCALE
        s_new = jnp.where(rg == g, sn, s_new)
    s_old = jnp.where(mk, s_old, NEG)
    mx = jnp.maximum(jnp.max(s_old, axis=-1, keepdims=True), s_new)
    p_old = jnp.where(mk, jnp.exp(s_old - mx), 0.0)
    p_new = jnp.exp(s_new - mx)
    den = jnp.maximum(jnp.sum(p_old, axis=-1, keepdims=True) + p_new, 1e-30)
    o_win = jnp.zeros((N_HEADS, HSLOT), F32)
    for g in range(KV_HEADS):
        vt = _pad_dt(win_ref[KV_HEADS + g]).astype(BF16)
        og = lax.dot_general(p_old.astype(BF16), vt, _NT, preferred_element_type=F32)
        og = og + p_new.astype(BF16).astype(F32) * new[:, (KV_HEADS + g) * HSLOT:(KV_HEADS + g + 1) * HSLOT]
        o_win = jnp.where(rg == g, og, o_win)
    o_win = o_win / den
    gts = gate_ref[...]
    o_ref[...] = gts[:, 0:1] * ocmp_ref[...] + gts[:, 1:2] * osel_ref[...] + gts[:, 2:3] * o_win


def _sample_win(q3, win4, layer, kvw_new, o_cmp, o_sel, gates3, *, bd, past, qpos):
    wb = win4.shape[-1]
    head3 = pl.BlockSpec((None, N_HEADS, HSLOT), lambda b: (b, 0, 0))
    return pl.pallas_call(
        functools.partial(_sample_win_body, qpos=qpos, past=past, wb=wb), grid=(bd,),
        in_specs=[head3, pl.BlockSpec((None, 2 * KV_HEADS, HEAD_DIM, wb), lambda b: (layer * bd + b, 0, 0, 0)),
                  pl.BlockSpec((SAMPLE_ROWS, KVP), lambda b: (0, 0)), head3, head3, head3],
        out_specs=head3,
        out_shape=jax.ShapeDtypeStruct((bd, N_HEADS, HSLOT), F32),
        compiler_params=_cparams("parallel"), name="sample_win_attention",
    )(q3, win4, kvw_new, o_cmp, o_sel, gates3)


def _window_shift_body(win_ref, new_ref, *refs, wb):
    o_ref = refs[-1]
    b = pl.program_id(0)
    new = new_ref[...]
    pick = lax.broadcasted_iota(jnp.int32, new.shape, 1) == b
    col = jnp.sum(jnp.where(pick, new, 0.0), axis=1, keepdims=True)
    lane = lax.broadcasted_iota(jnp.int32, (1, wb), 1)
    o_ref[...] = jnp.where(lane == wb - 1, col, pltpu.roll(win_ref[...], wb - 1, axis=1))


def _window_shift(win3, new_t, layer, n_layers, prev, *, bd):
    wb = win3.shape[-1]
    in_specs = [pl.BlockSpec((None, KVW, wb), lambda b: (layer * bd + b, 0, 0)),
                pl.BlockSpec((KVW, bd), lambda b: (0, 0))]
    operands = [win3, new_t]
    if prev is not None:
        in_specs.append(pl.BlockSpec(memory_space=pl.ANY))
        operands.append(prev)
    return pl.pallas_call(
        functools.partial(_window_shift_body, wb=wb), grid=(bd,), in_specs=in_specs,
        out_specs=pl.BlockSpec((None, None, KVW, wb), lambda b: (layer, b, 0, 0)),
        out_shape=jax.ShapeDtypeStruct((n_layers, bd, KVW, wb), F32),
        input_output_aliases={} if prev is None else {2: 0},
        compiler_params=_cparams("arbitrary"), name="sample_window_shift",
    )(*operands)


def _sample_conv_body(st_ref, u_ref, w_ref, b_ref, g_ref, bn_ref, o_ref, *, bd):
    w = w_ref[...]
    y = jnp.sum(st_ref[...] * w[None, :CONV_W - 1, :], axis=1) + u_ref[0:bd, :] * w[CONV_W - 1:CONV_W, :] + b_ref[...]
    y = _ln_rows(y, g_ref[...], bn_ref[...])
    o_ref[...] = y * jax.nn.sigmoid(y)


def _sample_conv(state, layer, u, cw, cb, cg, cbn, *, bd):
    vec = lambda a: a.reshape(1, CONV_CH)
    cst = lambda i: (0, 0)
    return pl.pallas_call(
        functools.partial(_sample_conv_body, bd=bd), grid=(1,),
        in_specs=[pl.BlockSpec((None, bd, CONV_W - 1, CONV_CH), lambda i: (layer, 0, 0, 0)),
                  pl.BlockSpec((SAMPLE_ROWS, CONV_CH), cst), pl.BlockSpec((CONV_W, CONV_CH), cst),
                  pl.BlockSpec((1, CONV_CH), cst), pl.BlockSpec((1, CONV_CH), cst), pl.BlockSpec((1, CONV_CH), cst)],
        out_specs=pl.BlockSpec((bd, CONV_CH), cst),
        out_shape=jax.ShapeDtypeStruct((bd, CONV_CH), F32),
        compiler_params=_cparams("arbitrary"), name="sample_conv",
    )(state, u, cw, vec(cb), vec(cg), vec(cbn))


def _sample_odd_body(st_ref, pin_ref, u_ref, vn_ref, pw_ref, ps_ref, w0_ref, b0_ref, o_ref, *, bd, start_pos):
    pin = pin_ref[0:bd, :]
    st = st_ref[...]
    for g, w in enumerate(POOL_WINDOWS):
        sl = slice(g * POOL_GC, (g + 1) * POOL_GC)
        tot = pin[:, sl] + jnp.sum(st[:, POOL_STATE - (w - 1):, sl], axis=1)
        d = tot / float(min(w, start_pos + 1)) - pin[:, sl]
        dp = jnp.concatenate([d, jnp.zeros((SAMPLE_ROWS - bd, POOL_GC), F32)], axis=0).astype(BF16)
        y = jnp.dot(dp, pw_ref[g], preferred_element_type=F32)[0:bd]
        o_ref[:, sl] = y * ps_ref[:, sl]
    mixed = w0_ref[...] * vn_ref[0:bd, :] + b0_ref[...]
    o_ref[:, POOL_CH:] = u_ref[0:bd, :] * mixed


def _sample_odd_mix(state, layer, pin, u, vn, pool_w, pool_scale, sgu_w, sgu_b, *, bd, start_pos):
    w0 = jnp.repeat(sgu_w[:, 0, 0], SGU_GC).reshape(1, SGU_CH)
    b0 = jnp.repeat(sgu_b[:, 0], SGU_GC).reshape(1, SGU_CH)
    cst = lambda i: (0, 0)
    return pl.pallas_call(
        functools.partial(_sample_odd_body, bd=bd, start_pos=start_pos), grid=(1,),
        in_specs=[pl.BlockSpec((None, bd, POOL_STATE, POOL_CH), lambda i: (layer, 0, 0, 0)),
                  pl.BlockSpec((SAMPLE_ROWS, POOL_CH), cst), pl.BlockSpec((SAMPLE_ROWS, SGU_CH), cst),
                  pl.BlockSpec((SAMPLE_ROWS, SGU_CH), cst),
                  pl.BlockSpec((POOL_GROUPS, POOL_GC, POOL_GC), lambda i: (0, 0, 0)),
                  pl.BlockSpec((1, POOL_CH), cst), pl.BlockSpec((1, SGU_CH), cst), pl.BlockSpec((1, SGU_CH), cst)],
        out_specs=pl.BlockSpec((bd, D_MODEL), cst),
        out_shape=jax.ShapeDtypeStruct((bd, D_MODEL), F32),
        compiler_params=_cparams("arbitrary"), name="sample_pool_sgu",
    )(state, pin, u, vn, pool_w.astype(BF16), pool_scale.reshape(1, POOL_CH), w0, b0)


def _pad_rows(x, rows):
    return jnp.pad(x, ((0, rows - x.shape[0]), (0, 0)))


def _split_to_nat(x, nheads):
    y = _unpad_heads(x, nheads)
    return _pad_nat(y.reshape(y.shape[:-1] + (nheads, HEAD_DIM))).reshape(x.shape)


def _nat_to_split(x, nheads):
    xh = x.reshape(x.shape[:-1] + (nheads, HSLOT))[..., :HEAD_DIM]
    return _pad_head(xh).reshape(x.shape)


def _even_in_proj(xb, xsb, w_in, w_gg, rope_p, rope_s, *, tm, bs, stack):
    nrep = rope_p[0].shape[0] // tm
    rs = xsb.shape[0]
    tab = lambda i, j: (i % nrep, 0)
    rope_ex = tuple((t, (tm, LANES), tab) for t in rope_p)
    rope_sx = tuple((t, (rs, LANES), lambda i, j: (0, 0)) for t in rope_s)
    gw = HPG * HSLOT
    qraw, qrot, qraw_s, qrot_s = _mm(
        xb, w_in, tm=tm, tn=2 * gw, n_off=E_Q, n_cols=QW, epilogue=_ep_q, extras=rope_ex,
        outs=((QW, BF16, 2 * gw), (QW, BF16, 2 * gw)), name="even_in_q",
        side=(xsb, _ep_q, rope_sx, ((QW, BF16, 2 * gw), (QW, BF16, 2 * gw))))
    nst = bs[1] // tm
    e, n_even, t_prev = stack
    row = {"f32": (KVP, F32, KVP), "bf16": (KVP, BF16, KVP),
           "t": ((n_even, bs[0], KVW, bs[1]), F32, (None, None, KVW, tm), lambda i, j: (e, i // nst, 0, i % nst))}
    kv_p, kv_s = [], []
    for sec, (off, want) in enumerate(zip((E_KVC, E_KVS, E_KVW), (("f32", "t"), ("bf16", "t"), ("bf16", "t")))):
        res = _mm(xb, w_in, tm=tm, tn=KVP, n_off=off, n_cols=KVP,
                  epilogue=functools.partial(_ep_kv, rope=sec > 0, want=want),
                  extras=rope_ex if sec > 0 else (), outs=tuple(row[k] for k in want),
                  name=("even_in_kvc", "even_in_kvs", "even_in_kvw")[sec],
                  side=(xsb, functools.partial(_ep_kv, rope=sec > 0, want=("f32",)),
                        rope_sx if sec > 0 else (), (row["f32"],)),
                  alias=None if t_prev is None else (t_prev[sec], 1))
        kv_p.append(res[:2])
        kv_s.append(res[2])
    gg_outs = ((CONV_CH, F32, CONV_CH), (GATE_W, F32, GATE_W))
    u, gates, u_s, gates_s = _mm(xb, w_gg, tm=tm, tn=GG_W, n_off=0, n_cols=GG_W, epilogue=_ep_glu_gates,
                                 outs=gg_outs, name="even_in_glu_gates", side=(xsb, _ep_glu_gates, (), gg_outs))
    return (qraw, qrot, kv_p, u, gates), (qraw_s, qrot_s, kv_s, u_s, gates_s)


def _mlp_up_cast(xb, xsb, w1, w2, layer, *, tm, tn):
    m, k = xb.shape
    rs = xsb.shape[0]
    dff = w1.shape[2]
    d_out = w2.shape[2]
    ni, nj = m // tm, dff // tn
    slab = dff // (ni * nj)
    assert m % tm == 0 and dff % tn == 0 and dff % (ni * nj) == 0 and slab % SAMPLE_ROWS == 0
    rsub = min(tm, ROW_SUB)

    def body(x_ref, xs_ref, w1_ref, w2_ref, h_ref, hs_ref, w2b_ref, w1b_ref):
        def act(rows_ref, rows):
            a = jnp.maximum(jnp.dot(rows_ref[rows, :], w1b_ref[...], preferred_element_type=F32), 0.0)
            return (a * a).astype(BF16)

        @pl.when(pl.program_id(1) == 0)
        def _():
            w1b_ref[...] = w1_ref[...].astype(BF16)
            hs_ref[...] = act(xs_ref, slice(0, rs))

        w2b_ref[...] = w2_ref[...].astype(BF16)
        for r in range(tm // rsub):
            rows = slice(r * rsub, (r + 1) * rsub)
            h_ref[rows, :] = act(x_ref, rows)

    return pl.pallas_call(
        body, grid=(nj, ni),
        in_specs=[pl.BlockSpec((tm, k), lambda j, i: (i, 0)), pl.BlockSpec((rs, k), lambda j, i: (0, 0)),
                  pl.BlockSpec((None, k, tn), lambda j, i: (layer, 0, j)),
                  pl.BlockSpec((None, slab, d_out), lambda j, i: (layer, j * ni + i, 0))],
        out_specs=[pl.BlockSpec((tm, tn), lambda j, i: (i, j)), pl.BlockSpec((rs, tn), lambda j, i: (0, j)),
                   pl.BlockSpec((slab, d_out), lambda j, i: (j * ni + i, 0))],
        out_shape=[jax.ShapeDtypeStruct((m, dff), BF16), jax.ShapeDtypeStruct((rs, dff), BF16),
                   jax.ShapeDtypeStruct((dff, d_out), BF16)],
        scratch_shapes=[pltpu.VMEM((k, tn), BF16)],
        compiler_params=_cparams("arbitrary", "arbitrary"), name="mlp_up_cast",
    )(xb, xsb, w1, w2)


def kernel(x_prompt, x_sample, cache_cmp_kv, cache_sel_kv, cache_win_kv, state_conv, state_pool, page_table,
           w_in_even, w_out_even, cmp_pe_k, cmp_pe_v, cmp_w_k, cmp_w_v, conv_w, conv_b, conv_ln_g, conv_ln_b,
           w_in_odd, w_out_odd, pool_w, pool_scale, sgu_ln_g, sgu_ln_b, sgu_w, sgu_b,
           mlp_w1, mlp_w2, ln_mix_g, ln_mix_b, ln_ffn_g, ln_ffn_b):
    B, S, D = x_prompt.shape
    Bd, Sd, _ = x_sample.shape
    n_pages = page_table.shape[1]
    past = n_pages * PAGE_SIZE
    n_even, n_pool = cache_cmp_kv.shape[:2]
    wb = cache_win_kv.shape[2]
    assert D == D_MODEL and Sd == 1 and Bd <= SAMPLE_ROWS
    assert S % 1024 == 0 and past % SEL_BLOCK == 0 and S >= WINDOW
    M = B * S
    Ms = SAMPLE_ROWS
    tm_p = 1024

    rope_p = _rope_tables(jnp.arange(S, dtype=jnp.int32))
    rope_s = _rope_tables(jnp.full((Ms,), past, jnp.int32))
    pps = min(32, n_pages)
    assert n_pages % pps == 0
    bsum = _block_sum_matrices(pps)
    cmp_t = cache_cmp_kv.transpose(0, 1, 3, 4, 5, 2).reshape(n_even * n_pool, KVW, PAGE_SIZE)
    sel_t = cache_sel_kv.transpose(0, 1, 3, 4, 5, 2).reshape(n_even * n_pool, 2, KV_HEADS, HEAD_DIM, PAGE_SIZE)
    win_t = cache_win_kv.transpose(0, 1, 3, 4, 5, 2).reshape(n_even * Bd, 2 * KV_HEADS, HEAD_DIM, wb)

    xp = x_prompt.reshape(M, D)
    xs = _pad_rows(x_sample.reshape(Bd, D), Ms)
    xpb, xsb = xp.astype(BF16), xs.astype(BF16)

    outs = {k: [] for k in ("cmp_s", "sel_s", "win_s", "conv_p", "conv_s", "pool_p", "pool_s", "sgu_p", "sgu_s")}
    kv6 = lambda a, lead: a.reshape(lead + (2, KV_HEADS, HEAD_DIM))
    kv_t = None
    win_s_t = None

    for layer in range(DEPTH):
        if layer % 2 == 0:
            e = layer // 2
            wts = _prep_even_weights(w_in_even[e], w_out_even[e], cmp_pe_k[e], cmp_pe_v[e], cmp_w_k[e], cmp_w_v[e])
            prj_p, prj_s = _even_in_proj(xpb, xsb, wts["w_qkv"], wts["w_gg"], rope_p, rope_s, tm=tm_p, bs=(B, S),
                                         stack=(e, n_even, kv_t))
            qraw, qrot, ((kvc, kvc_t), (kvs_b, kvs_t), (kvw_b, kvw_t)), u, gates = prj_p
            kv_t = (kvc_t, kvs_t, kvw_t)
            summ = _compress_rows(kvc, wts["pe"], rows=512, name="prompt_compress")
            (ckv,) = _mm(summ.astype(BF16), wts["big_p"], tm=min(summ.shape[0], 512), tn=KVP,
                         n_off=0, n_cols=KVP, epilogue=_ep_plain, outs=((KVP, F32, KVP),), name="prompt_compress_map")
            o_att = _prompt_attention(qraw, qrot, gates, ckv, kvs_b, kvw_b, batch=B, seq=S, tq=256, tk=256, ngrp=4)
            c = _prompt_conv(u, conv_w[e], conv_b[e], conv_ln_g[e], conv_ln_b[e], batch=B, seq=S, ts=256)
            outs["conv_p"].append(u.reshape(B, S, CONV_CH)[:, S - (CONV_W - 1):])
            qraw_s, qrot_s, (kvc_s, kvs_s, kvw_s), u_s, gates_s = prj_s
            ckv_t = _compress_pages(cmp_t, page_table, wts["pe_t"], wts["big_t"], bsum, e * n_pool, pps=pps,
                                    name="sample_compress")
            q3 = qraw_s.astype(F32)[:Bd].reshape(Bd, N_HEADS, HSLOT)
            o_cmp, idx = _sample_cmp(q3, ckv_t, bd=Bd, past=past, qpos=past, nbl=pps * (PAGE_SIZE // CMP_BLOCK))
            idx = idx[:, :, 0].reshape(Bd, KV_HEADS, TOP_K)
            qr3 = _split_to_nat(qrot_s.astype(F32)[:Bd], N_HEADS).reshape(Bd, N_HEADS, HSLOT)
            q4 = jnp.pad(qr3.reshape(Bd, KV_HEADS, HPG, HSLOT), ((0, 0), (0, 0), (0, Q_PAD_ROWS - HPG), (0, 0)))
            kvs_nat = _split_to_nat(kvs_s, 2 * KV_HEADS)
            kvw_nat = _split_to_nat(kvw_s, 2 * KV_HEADS)
            o_sel = _sample_sel(sel_t, page_table, idx, e * n_pool, q4, kvs_nat, bd=Bd, past=past)
            o_sel = o_sel[:, :, :HPG].reshape(Bd, N_HEADS, HSLOT)
            g3 = gates_s[:Bd, :3 * N_HEADS].reshape(Bd, KV_HEADS, 3, HPG)
            g3 = g3.transpose(0, 1, 3, 2).reshape(Bd, N_HEADS, 3)
            g3 = jnp.pad(g3, ((0, 0), (0, 0), (0, LANES - 3)))
            o_s = _sample_win(qr3, win_t, e, kvw_nat, o_cmp, o_sel, g3, bd=Bd, past=past, qpos=past)
            c_s = _sample_conv(state_conv, e, u_s, conv_w[e], conv_b[e], conv_ln_g[e], conv_ln_b[e], bd=Bd)
            o_sb = _pad_rows(_nat_to_split(o_s.reshape(Bd, QW), N_HEADS), Ms).astype(BF16)
            c_sb = _pad_rows(c_s, Ms).astype(BF16)
            xp, xpb, xs, xsb = _matmul_res_ln([o_att, c], [wts["wo_att"], wts["wo_conv"]], xp, ln_mix_g[layer],
                                              ln_mix_b[layer], tm=1024, tn=512, name="even_out_ln",
                                              side=([o_sb, c_sb], xs))
            kvc_c = _unpad_heads(kvc_s[:Bd], 2 * KV_HEADS)
            kvs_c = _unpad_heads(kvs_s[:Bd], 2 * KV_HEADS)
            kvw_c = _unpad_heads(kvw_s[:Bd], 2 * KV_HEADS)
            outs["cmp_s"].append(kv6(kvc_c, (Bd, 1)))
            outs["sel_s"].append(kv6(kvs_c, (Bd, 1)))
            if wb == WINDOW:
                win_s_t = _window_shift(win_t.reshape(n_even * Bd, KVW, wb), kvw_c.T, e, n_even, win_s_t, bd=Bd)
            else:
                wkv = jnp.concatenate([cache_win_kv[e], kv6(kvw_c, (Bd, 1))], axis=1)
                outs["win_s"].append(wkv[:, wkv.shape[1] - min(WINDOW, wkv.shape[1]):])
            outs["conv_s"].append(jnp.concatenate([state_conv[e], u_s[:Bd, None, :]], axis=1)[:, 1:])
        else:
            o = layer // 2
            w_in = w_in_odd[o]
            wb_in = w_in.astype(BF16)
            w_pu, w_v = wb_in[:, :POOL_CH + SGU_CH], wb_in[:, POOL_CH + SGU_CH:]
            w_out_p = w_out_odd[o].astype(BF16)
            lg, lb = sgu_ln_g[o].reshape(1, SGU_CH), sgu_ln_b[o].reshape(1, SGU_CH)

            gl_ex = ((lg, (1, SGU_CH), lambda i, j: (0, j)), (lb, (1, SGU_CH), lambda i, j: (0, j)))
            o_v, o_pu = ((SGU_CH, F32, SGU_CH),), ((POOL_CH, F32, POOL_CH), (SGU_CH, F32, SGU_CH))
            vn, vn_s = _mm(xpb, w_v, tm=tm_p, tn=SGU_CH, n_off=0, n_cols=SGU_CH, epilogue=_ep_gelu_gln,
                           extras=gl_ex, outs=o_v, name="odd_in_v", side=(xsb, _ep_gelu_gln, gl_ex, o_v))
            pin, uu, pin_s, uu_s = _mm(xpb, w_pu, tm=tm_p, tn=POOL_CH + SGU_CH, n_off=0, n_cols=POOL_CH + SGU_CH,
                                       epilogue=_ep_pool_gelu, outs=o_pu, name="odd_in_pool_u",
                                       side=(xsb, _ep_pool_gelu, (), o_pu))
            cat = _prompt_odd_mix(pin, uu, vn, pool_w[o], pool_scale[o], sgu_w[o], sgu_b[o], batch=B, seq=S, ts=512)
            outs["pool_p"].append(pin.reshape(B, S, POOL_CH)[:, S - POOL_STATE:])
            outs["sgu_p"].append(vn.reshape(B, S, SGU_CH)[:, ((S - 1) // CHUNK) * CHUNK:])
            cat_s = _sample_odd_mix(state_pool, o, pin_s, uu_s, vn_s, pool_w[o], pool_scale[o], sgu_w[o], sgu_b[o],
                                    bd=Bd, start_pos=past)
            xp, xpb, xs, xsb = _matmul_res_ln([cat], [w_out_p], xp, ln_mix_g[layer], ln_mix_b[layer], tm=1024,
                                              tn=512, name="odd_out_ln",
                                              side=([_pad_rows(cat_s, Ms).astype(BF16)], xs))
            outs["pool_s"].append(jnp.concatenate([state_pool[o], pin_s[:Bd, None, :]], axis=1)[:, 1:])
            outs["sgu_s"].append(vn_s[:Bd, None, :])
        h, h_s, w2b = _mlp_up_cast(xpb, xsb, mlp_w1, mlp_w2, layer, tm=2 * tm_p, tn=1024)
        xp, xpb, xs, xsb = _matmul_res_ln([h], [w2b], xp, ln_ffn_g[layer], ln_ffn_b[layer], tm=512, tn=512,
                                          name="mlp_down_ln", side=([h_s], xs))

    st = lambda k: jnp.stack(outs[k])
    rows_last = lambda a, nb=B: a.reshape(n_even, nb, 2, KV_HEADS, HEAD_DIM, a.shape[-1]).transpose(0, 1, 5, 2, 3, 4)
    kvc_t, kvs_t, kvw_t = kv_t
    return (xp.reshape(B, S, D), xs[:Bd].reshape(Bd, Sd, D),
            rows_last(kvc_t), st("cmp_s"), rows_last(kvs_t), st("sel_s"),
            rows_last(kvw_t[:, :, :, S - WINDOW:]),
            st("win_s") if win_s_t is None else rows_last(win_s_t, Bd),
            st("conv_p"), st("conv_s"), st("pool_p"), st("pool_s"), st("sgu_p"), st("sgu_s"))
```

```python
import functools

import jax
import jax.numpy as jnp
from jax import lax
from jax.experimental import pallas as pl
from jax.experimental.pallas import tpu as pltpu

F32 = jnp.float32
BF16 = jnp.bfloat16

D_MODEL = 2048
DEPTH = 4
PAGE_SIZE = 128
N_HEADS = 16
HEAD_DIM = 96
KV_HEADS = 4
HPG = N_HEADS // KV_HEADS
ATT_W = N_HEADS * HEAD_DIM
KVW = 2 * KV_HEADS * HEAD_DIM
CMP_BLOCK = 32
SEL_BLOCK = 64
TOP_K = 16
WINDOW = 512
ROPE_THETA = 10000.0
SCALE = HEAD_DIM ** -0.5
LOG2E = 1.4426950408889634
FORCE = 1e9
NEG = -1e30
CONV_CH = D_MODEL // 4
CONV_W = 31
POOL_CH = D_MODEL // 4
POOL_WINDOWS = (2, 4, 8, 16)
POOL_GROUPS = len(POOL_WINDOWS)
POOL_GC = POOL_CH // POOL_GROUPS
POOL_STATE = max(POOL_WINDOWS) - 1
SGU_CH = D_MODEL - POOL_CH
SGU_GROUPS = 4
SGU_GC = SGU_CH // SGU_GROUPS
CHUNK = 128
D_FF = 4 * D_MODEL
ALPHA = (2 * DEPTH) ** 0.25
LN_EPS = 1e-5

LANES = 128
SUBLANES = 8
HALF = HEAD_DIM // 2
HSLOT = LANES
HALF_OFF = LANES // 2
QW = N_HEADS * HSLOT
KVP = 2 * KV_HEADS * HSLOT
GATE_W = LANES
SAMPLE_ROWS = 16
ROW_SUB = 256
LN_ROW_SUB = 128
VMEM_LIMIT = 52 * 1024 * 1024

E_Q, E_KVC, E_KVS, E_KVW = 0, QW, QW + KVP, QW + 2 * KVP
GG_W = 2 * CONV_CH + GATE_W


def _cparams(*sem):
    return pltpu.CompilerParams(dimension_semantics=sem, vmem_limit_bytes=VMEM_LIMIT)


def _pad_head(x):
    halves = x.reshape(x.shape[:-1] + (2, HALF))
    halves = jnp.pad(halves, [(0, 0)] * (halves.ndim - 1) + [(0, HALF_OFF - HALF)])
    return halves.reshape(x.shape[:-1] + (HSLOT,))


def _pad_nat(x):
    return jnp.concatenate([x, jnp.zeros(x.shape[:-1] + (HSLOT - HEAD_DIM,), x.dtype)], axis=-1)


def _unpad_heads(x, nheads):
    xh = x.reshape(x.shape[:-1] + (nheads, HSLOT))
    y = jnp.concatenate([xh[..., :HALF], xh[..., HALF_OFF:HALF_OFF + HALF]], axis=-1)
    return y.reshape(x.shape[:-1] + (nheads * HEAD_DIM,))


def _rope_tables(pos):
    inv = jnp.power(ROPE_THETA, -jnp.arange(HALF, dtype=F32) / HALF)
    ang = pos.astype(F32)[:, None] * inv[None, :]
    cos, sin = jnp.cos(ang), jnp.sin(ang)
    z = jnp.zeros((pos.shape[0], HALF_OFF - HALF), F32)
    return (jnp.concatenate([cos, z, cos, z], axis=1),
            jnp.concatenate([-sin, z, sin, z], axis=1))


def _block_diag2(a, b):
    za = jnp.zeros((a.shape[0], b.shape[1]), a.dtype)
    zb = jnp.zeros((b.shape[0], a.shape[1]), a.dtype)
    return jnp.concatenate([jnp.concatenate([a, za], axis=1), jnp.concatenate([zb, b], axis=1)], axis=0)


def _prep_even_weights(w_in, w_out, pe_k, pe_v, w_ck, w_cv):
    d = w_in.shape[0]
    wb = w_in.astype(BF16)
    g0 = ATT_W + 3 * KVW
    n_heads_all = g0 // HEAD_DIM
    qkv = _pad_head(wb[:, :g0].reshape(d, n_heads_all, HEAD_DIM)).reshape(d, n_heads_all * HSLOT)
    gates = wb[:, g0:g0 + 3 * N_HEADS].reshape(d, 3, KV_HEADS, HPG).transpose(0, 2, 1, 3)
    gates = jnp.pad(gates.reshape(d, 3 * N_HEADS), ((0, 0), (0, GATE_W - 3 * N_HEADS)))
    w_gg = jnp.concatenate([wb[:, g0 + 3 * N_HEADS:], gates], axis=1)
    wo = w_out.astype(BF16)
    wo_att = jnp.pad(wo[:ATT_W].reshape(2 * N_HEADS, HALF, d), ((0, 0), (0, HALF_OFF - HALF), (0, 0)))
    wo_att = wo_att.reshape(QW, d)
    wo_conv = wo[ATT_W:]
    eye = jnp.eye(KV_HEADS, dtype=F32)
    pe = jnp.concatenate([jnp.tile(_pad_head(pe_k), (1, KV_HEADS)), jnp.tile(_pad_head(pe_v), (1, KV_HEADS))], axis=1)
    wk_full = _pad_head(_pad_head(w_ck).T).T
    wv_full = _pad_head(_pad_head(w_cv).T).T
    big_p = _block_diag2(jnp.kron(eye, wk_full), jnp.kron(eye, wv_full))
    pe_t = jnp.concatenate([jnp.tile(jnp.tile(pe_k.T, (1, PAGE_SIZE // CMP_BLOCK)), (KV_HEADS, 1)),
                            jnp.tile(jnp.tile(pe_v.T, (1, PAGE_SIZE // CMP_BLOCK)), (KV_HEADS, 1))], axis=0)
    big_t = _block_diag2(jnp.kron(eye, _pad_head(w_ck).T), jnp.kron(eye, _pad_nat(w_cv).T))
    return dict(w_qkv=qkv, w_gg=w_gg, wo_att=wo_att, wo_conv=wo_conv, pe=pe, pe_t=pe_t,
                big_p=big_p.astype(BF16), big_t=big_t.astype(BF16))


def _block_sum_matrices(pps):
    bpp = PAGE_SIZE // CMP_BLOCK
    nbl = bpp * pps
    p = jnp.arange(pps)[:, None, None]
    i = (jnp.arange(PAGE_SIZE) // CMP_BLOCK)[None, :, None]
    c = jnp.arange(nbl)[None, None, :]
    col = (i % 2) * (nbl // 2) + (bpp // 2) * p + i // 2
    return jnp.where(c == col, 1.0 / CMP_BLOCK, 0.0).astype(BF16)


def _ln_rows(y, g, b):
    mu = jnp.mean(y, axis=-1, keepdims=True)
    yc = y - mu
    var = jnp.mean(yc * yc, axis=-1, keepdims=True)
    return yc * lax.rsqrt(var + LN_EPS) * g + b


def _rope_slot(x, cos, sin):
    return x * cos + pltpu.roll(x, HALF_OFF, axis=1) * sin


_NT = (((1,), (1,)), ((), ()))


def _pad_rows_to(x, rows):
    return jnp.concatenate([x, jnp.zeros((rows - x.shape[0],) + x.shape[1:], x.dtype)], axis=0)


def _mm(x, w, *, tm, tn, n_off, n_cols, epilogue, extras=(), outs, name, side=None, alias=None):
    m, k = x.shape
    assert m % tm == 0 and n_cols % tn == 0 and n_off % tn == 0
    joff = n_off // tn
    ji = lambda im: (lambda j, i: im(i, j))
    in_specs = [pl.BlockSpec((tm, k), lambda j, i: (i, 0)),
                pl.BlockSpec((k, tn), lambda j, i: (0, joff + j))]
    in_specs += [pl.BlockSpec(bs, ji(im)) for _, bs, im in extras]
    operands = [x, w] + [a for a, _, _ in extras]
    out_shape, out_specs = [], []
    for o in outs:
        if len(o) == 3:
            out_shape.append(jax.ShapeDtypeStruct((m, o[0]), o[1]))
            out_specs.append(pl.BlockSpec((tm, o[2]), lambda j, i: (i, j)))
        else:
            out_shape.append(jax.ShapeDtypeStruct(o[0], o[1]))
            out_specs.append(pl.BlockSpec(o[2], ji(o[3])))
    ne, no = len(extras), len(outs)
    nse = 0
    if side is not None:
        xs, s_epilogue, s_extras, s_outs = side
        rs = xs.shape[0]
        nse = len(s_extras)
        in_specs += [pl.BlockSpec((rs, k), lambda j, i: (0, 0))]
        in_specs += [pl.BlockSpec(bs, ji(im)) for _, bs, im in s_extras]
        operands += [xs] + [a for a, _, _ in s_extras]
        for cols, dt, bc in s_outs:
            out_shape.append(jax.ShapeDtypeStruct((rs, cols), dt))
            out_specs.append(pl.BlockSpec((rs, bc), lambda j, i: (0, j)))
    rsub = min(tm, ROW_SUB)
    n_in = len(operands)
    io_alias = {}
    if alias is not None:
        io_alias = {n_in: alias[1]}
        in_specs.append(pl.BlockSpec(memory_space=pl.ANY))
        operands.append(alias[0])

    def body(*refs):
        x_ref, w_ref = refs[:2]
        ex = refs[2:2 + ne]
        o_refs = refs[len(operands):len(operands) + no]
        for r in range(tm // rsub):
            rows = slice(r * rsub, (r + 1) * rsub)
            acc = jnp.dot(x_ref[rows, :], w_ref[...], preferred_element_type=F32)
            epilogue(acc, ex, o_refs, rows)
        if side is not None:
            @pl.when(pl.program_id(1) == 0)
            def _():
                acc = jnp.dot(refs[2 + ne][...], w_ref[...], preferred_element_type=F32)
                s_epilogue(acc, refs[3 + ne:n_in], refs[len(operands) + no:], slice(0, rs))

    return pl.pallas_call(
        body, grid=(n_cols // tn, m // tm), in_specs=in_specs, out_specs=out_specs, out_shape=out_shape,
        input_output_aliases=io_alias, compiler_params=_cparams("arbitrary", "arbitrary"), name=name,
    )(*operands)


def _ep_q(acc, ex, outs, rows):
    cos, sin = ex[0][rows, :], ex[1][rows, :]
    outs[0][rows, :] = acc.astype(BF16)
    for j in range(acc.shape[1] // HSLOT):
        sl = slice(j * HSLOT, (j + 1) * HSLOT)
        outs[1][rows, sl] = _rope_slot(acc[:, sl], cos, sin).astype(BF16)


def _ep_kv(acc, ex, outs, rows, *, rope, want):
    o = dict(zip(want, outs))
    if rope:
        cos, sin = ex[0][rows, :], ex[1][rows, :]
    for j in range(2 * KV_HEADS):
        sl = slice(j * HSLOT, (j + 1) * HSLOT)
        x = acc[:, sl]
        if rope and j < KV_HEADS:
            x = _rope_slot(x, cos, sin)
        if "f32" in o:
            o["f32"][rows, sl] = x
        if "bf16" in o:
            o["bf16"][rows, sl] = x.astype(BF16)
        if "t" in o:
            xt = x.T
            o["t"][j * HEAD_DIM:j * HEAD_DIM + HALF, rows] = xt[0:HALF]
            o["t"][j * HEAD_DIM + HALF:(j + 1) * HEAD_DIM, rows] = xt[HALF_OFF:HALF_OFF + HALF]


def _ep_glu_gates(acc, ex, outs, rows):
    outs[0][rows, :] = acc[:, :CONV_CH] * jax.nn.sigmoid(acc[:, CONV_CH:2 * CONV_CH])
    outs[1][rows, :] = jax.nn.sigmoid(acc[:, 2 * CONV_CH:])


def _ep_plain(acc, ex, outs, rows):
    outs[0][rows, :] = acc.astype(outs[0].dtype)


def _ep_relu2(acc, ex, outs, rows):
    r = jnp.maximum(acc, 0.0)
    outs[0][rows, :] = (r * r).astype(outs[0].dtype)


def _ep_pool_gelu(acc, ex, outs, rows):
    outs[0][rows, :] = acc[:, :POOL_CH]
    outs[1][rows, :] = jax.nn.gelu(acc[:, POOL_CH:])


def _ep_gelu_gln(acc, ex, outs, rows):
    g, b = ex[0][...], ex[1][...]
    v = jax.nn.gelu(acc)
    for j in range(acc.shape[1] // SGU_GC):
        sl = slice(j * SGU_GC, (j + 1) * SGU_GC)
        outs[0][rows, sl] = _ln_rows(v[:, sl], g[:, sl], b[:, sl])


def _proj_ln(a_list, w_list, resid, g, b, *, tm, name, side):
    m, n = resid.shape
    as_list, resid_s = side
    rs = resid_s.shape[0]
    npair = len(a_list)
    in_specs = []
    for a in a_list:
        in_specs.append(pl.BlockSpec((tm, a.shape[1]), lambda i: (i, 0)))
    for w in w_list:
        in_specs.append(pl.BlockSpec(w.shape, lambda i: (0, 0), pipeline_mode=pl.Buffered(1)))
    in_specs += [pl.BlockSpec((tm, n), lambda i: (i, 0)),
                 pl.BlockSpec((1, n), lambda i: (0, 0)), pl.BlockSpec((1, n), lambda i: (0, 0))]
    for a in as_list:
        in_specs.append(pl.BlockSpec((rs, a.shape[1]), lambda i: (0, 0)))
    in_specs.append(pl.BlockSpec((rs, n), lambda i: (0, 0)))

    def body(*refs):
        a_refs, w_refs = refs[:npair], refs[npair:2 * npair]
        r_ref, g_ref, b_ref = refs[2 * npair:2 * npair + 3]
        as_refs = refs[2 * npair + 3:3 * npair + 3]
        rs_ref, o_ref, ob_ref, os_ref, osb_ref = refs[3 * npair + 3:]

        def rows_out(a_rs, res, rows):
            acc = ALPHA * res[rows, :]
            for a_ref, w_ref in zip(a_rs, w_refs):
                acc = acc + jnp.dot(a_ref[rows, :], w_ref[...], preferred_element_type=F32)
            return _ln_rows(acc, g_ref[...], b_ref[...])

        rsub = min(tm, LN_ROW_SUB)
        for r in range(tm // rsub):
            rows = slice(r * rsub, (r + 1) * rsub)
            y = rows_out(a_refs, r_ref, rows)
            o_ref[rows, :] = y
            ob_ref[rows, :] = y.astype(BF16)

        @pl.when(pl.program_id(0) == 0)
        def _():
            ys = rows_out(as_refs, rs_ref, slice(0, rs))
            os_ref[...] = ys
            osb_ref[...] = ys.astype(BF16)

    row = pl.BlockSpec((tm, n), lambda i: (i, 0))
    srow = pl.BlockSpec((rs, n), lambda i: (0, 0))
    return pl.pallas_call(
        body, grid=(m // tm,), in_specs=in_specs, out_specs=[row, row, srow, srow],
        out_shape=[jax.ShapeDtypeStruct((m, n), F32), jax.ShapeDtypeStruct((m, n), BF16),
                   jax.ShapeDtypeStruct((rs, n), F32), jax.ShapeDtypeStruct((rs, n), BF16)],
        compiler_params=_cparams("arbitrary"), name=name,
    )(*a_list, *w_list, resid, g.reshape(1, n), b.reshape(1, n), *as_list, resid_s)


def _matmul_res_ln(a_list, w_list, resid, g, b, *, tm, tn, name, side):
    m, n = resid.shape
    nj = n // tn
    as_list, resid_s = side
    rs = resid_s.shape[0]
    npair = len(a_list)
    rsub, rsub_ln = min(tm, ROW_SUB), min(tm, LN_ROW_SUB)

    def body(*refs):
        h_ref, w_refs = refs[:npair], refs[npair:2 * npair]
        r_ref, g_ref, b_ref = refs[2 * npair:2 * npair + 3]
        hs_ref = refs[2 * npair + 3:3 * npair + 3]
        rs_ref, o_ref, ob_ref, os_ref, osb_ref = refs[3 * npair + 3:]
        j = pl.program_id(1)
        first = pl.program_id(0) == 0
        srows = slice(0, rs)

        def tile(hr, rr, rows):
            acc = ALPHA * rr[rows, :]
            for a_ref, w_ref in zip(hr, w_refs):
                acc = acc + jnp.dot(a_ref[rows, :], w_ref[...], preferred_element_type=F32)
            return acc

        def finish(o, ob, y, rows):
            y = _ln_rows(y, g_ref[...], b_ref[...])
            o[rows, :] = y
            ob[rows, :] = y.astype(BF16)

        for jj in range(nj - 1):
            @pl.when(j == jj)
            def _(jj=jj):
                cols = slice(jj * tn, (jj + 1) * tn)
                for r in range(tm // rsub):
                    rows = slice(r * rsub, (r + 1) * rsub)
                    o_ref[rows, cols] = tile(h_ref, r_ref, rows)

                @pl.when(first)
                def _():
                    os_ref[:, cols] = tile(hs_ref, rs_ref, srows)

        @pl.when(j == nj - 1)
        def _():
            done = slice(0, (nj - 1) * tn)
            for r in range(tm // rsub_ln):
                rows = slice(r * rsub_ln, (r + 1) * rsub_ln)
                finish(o_ref, ob_ref, jnp.concatenate([o_ref[rows, done], tile(h_ref, r_ref, rows)], axis=1), rows)

            @pl.when(first)
            def _():
                finish(os_ref, osb_ref, jnp.concatenate([os_ref[:, done], tile(hs_ref, rs_ref, srows)], axis=1), srows)

    assert nj > 1
    row = pl.BlockSpec((tm, n), lambda i, j: (i, 0))
    srow = pl.BlockSpec((rs, n), lambda i, j: (0, 0))
    cst = lambda i, j: (0, 0)
    in_specs = [pl.BlockSpec((tm, a.shape[1]), lambda i, j: (i, 0)) for a in a_list]
    in_specs += [pl.BlockSpec((w.shape[0], tn), lambda i, j: (0, j)) for w in w_list]
    in_specs += [pl.BlockSpec((tm, tn), lambda i, j: (i, j)), pl.BlockSpec((1, n), cst), pl.BlockSpec((1, n), cst)]
    in_specs += [pl.BlockSpec((rs, a.shape[1]), cst) for a in as_list]
    in_specs += [pl.BlockSpec((rs, tn), lambda i, j: (0, j))]
    return pl.pallas_call(
        body, grid=(m // tm, nj), in_specs=in_specs, out_specs=[row, row, srow, srow],
        out_shape=[jax.ShapeDtypeStruct((m, n), F32), jax.ShapeDtypeStruct((m, n), BF16),
                   jax.ShapeDtypeStruct((rs, n), F32), jax.ShapeDtypeStruct((rs, n), BF16)],
        compiler_params=_cparams("arbitrary", "arbitrary"), name=name,
    )(*a_list, *w_list, resid, g.reshape(1, n), b.reshape(1, n), *as_list, resid_s)


def _compress_rows(kvc, pe, *, rows, name):
    m, c = kvc.shape
    nb = rows // CMP_BLOCK

    def body(x_ref, pe_ref, o_ref):
        x = x_ref[...].reshape(nb, CMP_BLOCK, c) * pe_ref[...][None]
        o_ref[...] = jnp.sum(x, axis=1) * (1.0 / CMP_BLOCK)

    return pl.pallas_call(
        body, grid=(m // rows,),
        in_specs=[pl.BlockSpec((rows, c), lambda i: (i, 0)), pl.BlockSpec((CMP_BLOCK, c), lambda i: (0, 0))],
        out_specs=pl.BlockSpec((nb, c), lambda i: (i, 0)),
        out_shape=jax.ShapeDtypeStruct((m // CMP_BLOCK, c), F32),
        compiler_params=_cparams("parallel"), name=name,
    )(kvc, pe)


def _compress_pages(pages_t, page_table, pe_t, big_t, bsum, layer_base, *, pps, name):
    bd, n_pages = page_table.shape
    bpp = PAGE_SIZE // CMP_BLOCK
    nbl = bpp * pps
    nsteps = n_pages // pps

    def body(pt_ref, *refs):
        page_refs = refs[:pps]
        pe_ref, big_ref, bsum_ref, o_ref = refs[pps:]
        acc = jnp.zeros((KVW, nbl), F32)
        for p in range(pps):
            x = page_refs[p][...] * pe_ref[...]
            acc = acc + jnp.dot(x.astype(BF16), bsum_ref[p], preferred_element_type=F32)
        o_ref[...] = jnp.dot(big_ref[...], acc.astype(BF16), preferred_element_type=F32)

    def page_map(p):
        return lambda b, j, pt: (layer_base + pt[b, j * pps + p], 0, 0)

    cst2 = lambda b, j, pt: (0, 0)
    grid_spec = pltpu.PrefetchScalarGridSpec(
        num_scalar_prefetch=1, grid=(bd, nsteps),
        in_specs=[pl.BlockSpec((None, KVW, PAGE_SIZE), page_map(p)) for p in range(pps)]
        + [pl.BlockSpec((KVW, PAGE_SIZE), cst2), pl.BlockSpec((KVP, KVW), cst2),
           pl.BlockSpec((pps, PAGE_SIZE, nbl), lambda b, j, pt: (0, 0, 0))],
        out_specs=pl.BlockSpec((None, KVP, nbl), lambda b, j, pt: (b, 0, j)))
    return pl.pallas_call(
        body, grid_spec=grid_spec,
        out_shape=jax.ShapeDtypeStruct((bd, KVP, n_pages * bpp), F32),
        compiler_params=_cparams("parallel", "arbitrary"), name=name,
    )(page_table, *([pages_t] * pps), pe_t, big_t, bsum)


def _flash_step_t(q, k, vt, valid, m_ref, l_ref, acc_ref):
    bias = jnp.where(valid, 0.0, NEG)
    s = jnp.dot(k, q, preferred_element_type=F32) + jnp.concatenate([bias] * HPG, axis=1)
    m_prev = m_ref[...]
    m_new = jnp.maximum(m_prev, jnp.max(s, axis=0, keepdims=True))
    alpha = jnp.exp2(m_prev - m_new)
    p = jnp.exp2(s - m_new)
    l_ref[...] = alpha * l_ref[...] + jnp.sum(p, axis=0, keepdims=True)
    acc_ref[...] = alpha * acc_ref[...] + jnp.dot(vt, p.astype(BF16), preferred_element_type=F32)
    m_ref[...] = m_new


def _attn_body(*refs, tq, tk, seq, k_top, ngrp):
    n_in = 9
    ins = [refs[g * n_in:(g + 1) * n_in] for g in range(ngrp)]
    o_ref = refs[ngrp * n_in]
    m_ref, l_ref, acc_ref, sel_ref, vts_ref, vtw_ref = refs[ngrp * n_in + 1:]
    i = pl.program_id(2)
    cols = HPG * tq
    nsb = seq // SEL_BLOCK
    nt = seq // tk
    bpt = tk // SEL_BLOCK
    q0 = i * tq
    lane_q = lax.broadcasted_iota(jnp.int32, (1, cols), 1)
    qpos = q0 + (lane_q & (tq - 1))
    qp1 = q0 + lax.broadcasted_iota(jnp.int32, (1, tq), 1)

    @pl.when(i == 0)
    def _():
        def tr(t, c):
            for g in range(ngrp):
                vs_ref, vw_ref = ins[g][6], ins[g][8]
                for h in range(tk // LANES):
                    k0 = pl.multiple_of(t * tk + h * LANES, LANES)
                    hs = slice(h * LANES, (h + 1) * LANES)
                    vts_ref[g, t, :, hs] = vs_ref[pl.ds(k0, LANES), :].astype(F32).T.astype(BF16)
                    vtw_ref[g, t, :, hs] = vw_ref[pl.ds(k0, LANES), :].astype(F32).T.astype(BF16)
            return c
        lax.fori_loop(0, nt, tr, 0)

    def heads_t(ref, scale=1.0):
        parts = [ref[:, j * HSLOT:(j + 1) * HSLOT].astype(F32).T * scale for j in range(HPG)]
        return jnp.concatenate(parts, axis=1).astype(BF16)

    r = lax.broadcasted_iota(jnp.int32, (2 * nsb, 1), 0)
    n_of = jnp.where(r < nsb, 2 * r, 2 * (r - nsb) + 1)
    mk = ((n_of + 1) * CMP_BLOCK - 1) <= qpos
    sb = lax.broadcasted_iota(jnp.int32, (nsb, 1), 0)
    vis = (sb * SEL_BLOCK) <= qp1
    cur = sb == (qp1 >> (SEL_BLOCK.bit_length() - 1))

    def compressed_branch(g):
        qraw_ref, ck_ref, cv_ref = ins[g][0], ins[g][3], ins[g][4]
        qr = heads_t(qraw_ref)
        ck = jnp.concatenate([ck_ref[pl.ds(0, nsb, stride=2), :], ck_ref[pl.ds(1, nsb, stride=2), :]], axis=0)
        s = jnp.dot(ck.astype(BF16), qr, preferred_element_type=F32) * SCALE
        s = jnp.where(mk, s, NEG)
        mx = jnp.max(s, axis=0, keepdims=True)
        p = jnp.where(mk, jnp.exp(s - mx), 0.0)
        pn = p * (1.0 / jnp.maximum(jnp.sum(p, axis=0, keepdims=True), 1e-30))
        cv = jnp.concatenate([cv_ref[pl.ds(0, nsb, stride=2), :], cv_ref[pl.ds(1, nsb, stride=2), :]], axis=0)
        cvt = _pad_rows_to(cv, LANES).T.astype(BF16)
        o_cmp = jnp.dot(cvt, _pad_rows_to(pn, LANES).astype(BF16), preferred_element_type=F32)
        pp = pn[0:nsb] + pn[nsb:2 * nsb]
        imp = pp[:, 0:tq]
        for j in range(1, HPG):
            imp = imp + pp[:, j * tq:(j + 1) * tq]
        imp = jnp.where(cur, FORCE, jnp.where(vis, imp, -FORCE))
        cnt = jnp.zeros((nsb, tq), F32)
        for j in range(nsb):
            rowj = imp[j:j + 1, :]
            beats = (rowj > imp) | ((rowj == imp) & (j < sb))
            cnt = cnt + jnp.where(beats, 1.0, 0.0)
        sel = jnp.where(cnt < k_top, 1.0, 0.0)
        for t in range(nt):
            sel_ref[g, t, 0:bpt, :] = sel[bpt * t:bpt * (t + 1), :]
        return o_cmp

    o_cmp = [compressed_branch(g) for g in range(ngrp)]
    qt = [heads_t(ins[g][1], SCALE * LOG2E) for g in range(ngrp)]
    rowk = lax.broadcasted_iota(jnp.int32, (tk, 1), 0)

    def reset():
        m_ref[...] = jnp.full((ngrp, 1, cols), NEG, F32)
        l_ref[...] = jnp.zeros((ngrp, 1, cols), F32)
        acc_ref[...] = jnp.zeros((ngrp, HSLOT, cols), F32)

    def result(g):
        return acc_ref[g] * (1.0 / jnp.maximum(l_ref[g], 1e-30))

    reset()

    def sel_step(t, c):
        k0 = pl.multiple_of(t * tk, tk)
        kp = k0 + rowk
        for g in range(ngrp):
            sm = sel_ref[g, t, bpt - 1:bpt, :]
            for j in range(bpt - 2, -1, -1):
                sm = jnp.where(rowk < (j + 1) * SEL_BLOCK, sel_ref[g, t, j:j + 1, :], sm)
            valid = (kp <= qp1) & (sm > 0.5)
            _flash_step_t(qt[g], ins[g][5][pl.ds(k0, tk), :], vts_ref[g, t], valid,
                          m_ref.at[g], l_ref.at[g], acc_ref.at[g])
        return c

    t_end = lax.div(q0 + tq + tk - 1, tk)
    lax.fori_loop(0, t_end, sel_step, 0)
    o_sel = [result(g) for g in range(ngrp)]

    reset()

    def win_step(t, c):
        k0 = pl.multiple_of(t * tk, tk)
        kp = k0 + rowk
        valid = (kp <= qp1) & (kp >= qp1 - WINDOW)
        for g in range(ngrp):
            _flash_step_t(qt[g], ins[g][7][pl.ds(k0, tk), :], vtw_ref[g, t], valid,
                          m_ref.at[g], l_ref.at[g], acc_ref.at[g])
        return c

    lax.fori_loop(lax.div(jnp.maximum(q0 - WINDOW, 0), tk), t_end, win_step, 0)

    gw = HPG * HSLOT
    for g in range(ngrp):
        gt = ins[g][2][...].T
        o = jnp.zeros((HSLOT, cols), F32)
        for br, o_br in enumerate((o_cmp[g], o_sel[g], result(g))):
            r0 = (g * 3 + br) * HPG
            grow = jnp.concatenate([gt[r0 + j:r0 + j + 1, :] for j in range(HPG)], axis=1)
            o = o + grow * o_br
        for j in range(HPG):
            o_ref[:, g * gw + j * HSLOT:g * gw + (j + 1) * HSLOT] = o[:, j * tq:(j + 1) * tq].T.astype(BF16)


def _prompt_attention(qraw, qrot, gates, ckv, kvs_b, kvw_b, *, batch, seq, tq, tk, ngrp):
    nq = seq // tq
    ncb = seq // CMP_BLOCK
    nt = seq // tk
    cols = HPG * tq
    k_top = min(TOP_K, seq // SEL_BLOCK)
    assert tq & (tq - 1) == 0 and seq % tk == 0 and tk % LANES == 0 and tk // SEL_BLOCK <= SUBLANES
    assert ngrp == KV_HEADS
    body = functools.partial(_attn_body, tq=tq, tk=tk, seq=seq, k_top=k_top, ngrp=ngrp)
    gw = HPG * HSLOT
    in_specs, operands = [], []
    for g in range(ngrp):
        qmap = functools.partial(lambda b, p, i, g: (b * nq + i, p * ngrp + g), g=g)
        kmap = functools.partial(lambda b, p, i, g: (b, p * ngrp + g), g=g)
        vmap_ = functools.partial(lambda b, p, i, g: (b, KV_HEADS + p * ngrp + g), g=g)
        in_specs += [pl.BlockSpec((tq, gw), qmap), pl.BlockSpec((tq, gw), qmap),
                     pl.BlockSpec((tq, GATE_W), lambda b, p, i: (b * nq + i, 0)),
                     pl.BlockSpec((ncb, HSLOT), kmap), pl.BlockSpec((ncb, HSLOT), vmap_),
                     pl.BlockSpec((seq, HSLOT), kmap), pl.BlockSpec((seq, HSLOT), vmap_),
                     pl.BlockSpec((seq, HSLOT), kmap), pl.BlockSpec((seq, HSLOT), vmap_)]
        operands += [qraw, qrot, gates, ckv, ckv, kvs_b, kvs_b, kvw_b, kvw_b]
    return pl.pallas_call(
        body, grid=(batch, KV_HEADS // ngrp, nq), in_specs=in_specs,
        out_specs=pl.BlockSpec((tq, ngrp * gw), lambda b, p, i: (b * nq + i, p)),
        out_shape=jax.ShapeDtypeStruct((batch * seq, QW), BF16),
        scratch_shapes=[pltpu.VMEM((ngrp, 1, cols), F32), pltpu.VMEM((ngrp, 1, cols), F32),
                        pltpu.VMEM((ngrp, HSLOT, cols), F32), pltpu.VMEM((ngrp, nt, SUBLANES, tq), F32),
                        pltpu.VMEM((ngrp, nt, HSLOT, tk), BF16), pltpu.VMEM((ngrp, nt, HSLOT, tk), BF16)],
        compiler_params=_cparams("parallel", "parallel", "arbitrary"), name="prompt_attention",
    )(*operands)


CONV_HALO = 32


def _conv_body(cur_ref, prev_ref, w_ref, b_ref, g_ref, bn_ref, o_ref, ext_ref, *, ts):
    c = pl.program_id(1)
    span = CONV_HALO + ts - SUBLANES
    ext_ref[0, 0:CONV_HALO, :] = jnp.where(c > 0, prev_ref[...], 0.0)
    ext_ref[0, CONV_HALO:CONV_HALO + ts, :] = cur_ref[...]
    for s in range(1, SUBLANES):
        ext_ref[s, 0:span, :] = ext_ref[0, pl.ds(s, span), :]
    acc = jnp.zeros((ts, CONV_CH), F32) + b_ref[...]
    off = CONV_HALO - (CONV_W - 1)
    for k in range(CONV_W):
        s = (off + k) % SUBLANES
        acc = acc + ext_ref[s, pl.ds(off + k - s, ts), :] * w_ref[k:k + 1, :]
    y = _ln_rows(acc, g_ref[...], bn_ref[...])
    o_ref[...] = (y * jax.nn.sigmoid(y)).astype(BF16)


def _prompt_conv(u, cw, cb, cg, cbn, *, batch, seq, ts):
    nt = seq // ts
    r = ts // CONV_HALO
    cwp = jnp.pad(cw, ((0, CONV_HALO - CONV_W), (0, 0)))
    vec = lambda a: a.reshape(1, CONV_CH)
    cst = lambda b, c: (0, 0)
    return pl.pallas_call(
        functools.partial(_conv_body, ts=ts), grid=(batch, nt),
        in_specs=[pl.BlockSpec((ts, CONV_CH), lambda b, c: (b * nt + c, 0)),
                  pl.BlockSpec((CONV_HALO, CONV_CH), lambda b, c: (jnp.maximum((b * nt + c) * r - 1, 0), 0)),
                  pl.BlockSpec((CONV_HALO, CONV_CH), cst),
                  pl.BlockSpec((1, CONV_CH), cst), pl.BlockSpec((1, CONV_CH), cst), pl.BlockSpec((1, CONV_CH), cst)],
        out_specs=pl.BlockSpec((ts, CONV_CH), lambda b, c: (b * nt + c, 0)),
        out_shape=jax.ShapeDtypeStruct((batch * seq, CONV_CH), BF16),
        scratch_shapes=[pltpu.VMEM((SUBLANES, CONV_HALO + ts, CONV_CH), F32)],
        compiler_params=_cparams("parallel", "arbitrary"), name="prompt_conv",
    )(u, u, cwp, vec(cb), vec(cg), vec(cbn))


POOL_HALO = 16


def _odd_mix_body(pin_ref, prev_ref, u_ref, vn_ref, pw_ref, ps_ref, sw_ref, sb_ref, o_ref, ext_ref, *, ts):
    c = pl.program_id(1)
    ext_ref[0:POOL_HALO, :] = jnp.where(c > 0, prev_ref[...], 0.0)
    ext_ref[POOL_HALO:POOL_HALO + ts, :] = pin_ref[...]
    t = c * ts + lax.broadcasted_iota(jnp.int32, (ts, 1), 0)
    for g, w in enumerate(POOL_WINDOWS):
        sl = slice(g * POOL_GC, (g + 1) * POOL_GC)
        tot = ext_ref[pl.ds(POOL_HALO, ts), sl]
        for j in range(1, w):
            tot = tot + ext_ref[pl.ds(POOL_HALO - j, ts), sl]
        cnt = jnp.minimum(w, t + 1).astype(F32)
        d = tot / cnt - pin_ref[:, sl]
        y = jnp.dot(d.astype(BF16), pw_ref[g], preferred_element_type=F32)
        o_ref[:, sl] = (y * ps_ref[:, sl]).astype(BF16)
    ri = lax.broadcasted_iota(jnp.int32, (CHUNK, CHUNK), 0)
    ci = lax.broadcasted_iota(jnp.int32, (CHUNK, CHUNK), 1)
    for g in range(SGU_GROUPS):
        sl = slice(g * SGU_GC, (g + 1) * SGU_GC)
        ws = jnp.where(ci <= ri, sw_ref[g], 0.0).astype(BF16)
        for q in range(ts // CHUNK):
            rows = slice(q * CHUNK, (q + 1) * CHUNK)
            mixed = jnp.dot(ws, vn_ref[rows, sl].astype(BF16), preferred_element_type=F32) + sb_ref[:, g:g + 1]
            o_ref[rows, POOL_CH + g * SGU_GC:POOL_CH + (g + 1) * SGU_GC] = (u_ref[rows, sl] * mixed).astype(BF16)


def _prompt_odd_mix(pin, u, vn, pool_w, pool_scale, sgu_w, sgu_b, *, batch, seq, ts):
    assert seq % ts == 0 and ts % CHUNK == 0
    nt = seq // ts
    r = ts // POOL_HALO
    cst2 = lambda b, c: (0, 0)
    cst3 = lambda b, c: (0, 0, 0)
    row = lambda b, c: (b * nt + c, 0)
    return pl.pallas_call(
        functools.partial(_odd_mix_body, ts=ts), grid=(batch, nt),
        in_specs=[pl.BlockSpec((ts, POOL_CH), row),
                  pl.BlockSpec((POOL_HALO, POOL_CH), lambda b, c: (jnp.maximum((b * nt + c) * r - 1, 0), 0)),
                  pl.BlockSpec((ts, SGU_CH), row), pl.BlockSpec((ts, SGU_CH), row),
                  pl.BlockSpec((POOL_GROUPS, POOL_GC, POOL_GC), cst3), pl.BlockSpec((1, POOL_CH), cst2),
                  pl.BlockSpec((SGU_GROUPS, CHUNK, CHUNK), cst3), pl.BlockSpec((CHUNK, SGU_GROUPS), cst2)],
        out_specs=pl.BlockSpec((ts, D_MODEL), row),
        out_shape=jax.ShapeDtypeStruct((batch * seq, D_MODEL), BF16),
        scratch_shapes=[pltpu.VMEM((POOL_HALO + ts, POOL_CH), F32)],
        compiler_params=_cparams("parallel", "arbitrary"), name="prompt_pool_sgu",
    )(pin, pin, u, vn, pool_w.astype(BF16), pool_scale.reshape(1, POOL_CH), sgu_w, sgu_b.T)


def _group_rows(nrows):
    return lax.broadcasted_iota(jnp.int32, (nrows, 1), 0) >> (HPG.bit_length() - 1)


def _sample_cmp_body(q_ref, ckv_ref, o_ref, idx_ref, *, qpos, ncb, nbl, k_past):
    q = q_ref[...].astype(BF16)
    rg = _group_rows(N_HEADS)
    half = nbl // 2
    assert nbl & (nbl - 1) == 0
    sh = nbl.bit_length() - 1
    lane = lax.broadcasted_iota(jnp.int32, (1, ncb), 1)
    grp, w = lane >> sh, lane & (nbl - 1)
    n_cmp = grp * nbl + 2 * (w & (half - 1)) + (w >> (sh - 1))
    mk = jnp.broadcast_to(((n_cmp + 1) * CMP_BLOCK - 1) <= qpos, (N_HEADS, ncb))
    s = jnp.zeros((N_HEADS, ncb), F32)
    for g in range(KV_HEADS):
        ck = ckv_ref[g * HSLOT:(g + 1) * HSLOT, :].astype(BF16)
        s = jnp.where(rg == g, jnp.dot(q, ck, preferred_element_type=F32) * SCALE, s)
    s = jnp.where(mk, s, NEG)
    mx = jnp.max(s, axis=-1, keepdims=True)
    p = jnp.where(mk, jnp.exp(s - mx), 0.0)
    pn = p / jnp.maximum(jnp.sum(p, axis=-1, keepdims=True), 1e-30)
    o = jnp.zeros((N_HEADS, HSLOT), F32)
    for g in range(KV_HEADS):
        cv = ckv_ref[(KV_HEADS + g) * HSLOT:(KV_HEADS + g + 1) * HSLOT, :].astype(BF16)
        o = jnp.where(rg == g, lax.dot_general(pn.astype(BF16), cv, _NT, preferred_element_type=F32), o)
    o_ref[...] = o
    pair = pn + pltpu.roll(pn, ncb - half, axis=1)
    valid = w < half
    sb = grp * half + w
    vis = (sb * SEL_BLOCK) <= qpos
    ri = lax.broadcasted_iota(jnp.int32, (ncb, ncb), 0)
    sb_r = (ri >> sh) * half + (ri & (nbl - 1))
    sb_c = jnp.broadcast_to(sb, (ncb, ncb))
    slot = lax.broadcasted_iota(jnp.int32, (TOP_K, 1), 0)
    for g in range(KV_HEADS):
        imp = jnp.sum(jnp.where(rg == g, pair, 0.0), axis=0, keepdims=True)
        imp = jnp.where(valid, jnp.where(vis, imp, -FORCE), -2.0 * FORCE)
        a = jnp.broadcast_to(imp, (ncb, ncb))
        bt = a.T
        beats = (bt > a) | ((bt == a) & (sb_r < sb_c))
        rank = jnp.sum(jnp.where(beats, 1.0, 0.0), axis=0, keepdims=True)
        onehot = jnp.where((rank == slot.astype(F32)) & valid, 1.0, 0.0)
        idx = jnp.sum(onehot * sb.astype(F32), axis=-1, keepdims=True)
        idx = jnp.where(slot < k_past, idx, 0.0)
        idx_ref[g * TOP_K:(g + 1) * TOP_K, :] = jnp.broadcast_to(idx, (TOP_K, LANES)).astype(jnp.int32)


def _sample_cmp(q3, ckv_t, *, bd, past, qpos, nbl):
    ncb = past // CMP_BLOCK
    k_past = min(TOP_K - 1, past // SEL_BLOCK)
    body = functools.partial(_sample_cmp_body, qpos=qpos, ncb=ncb, nbl=nbl, k_past=k_past)
    head3 = pl.BlockSpec((None, N_HEADS, HSLOT), lambda b: (b, 0, 0))
    return pl.pallas_call(
        body, grid=(bd,),
        in_specs=[head3, pl.BlockSpec((None, KVP, ncb), lambda b: (b, 0, 0))],
        out_specs=[head3, pl.BlockSpec((None, KV_HEADS * TOP_K, LANES), lambda b: (b, 0, 0))],
        out_shape=[jax.ShapeDtypeStruct((bd, N_HEADS, HSLOT), F32),
                   jax.ShapeDtypeStruct((bd, KV_HEADS * TOP_K, LANES), jnp.int32)],
        compiler_params=_cparams("parallel"), name="sample_cmp_attention",
    )(q3, ckv_t)


Q_PAD_ROWS = 8


def _pad_dt(x):
    return _pad_rows_to(x, HSLOT)


def _sample_sel_body(pt_ref, idx_ref, *refs, k_past):
    k_refs, v_refs = refs[:k_past], refs[k_past:2 * k_past]
    q_ref, knew_ref, vnew_ref, o_ref = refs[2 * k_past:]
    b, g = pl.program_id(0), pl.program_id(1)
    bpp = PAGE_SIZE // SEL_BLOCK
    q = q_ref[...].astype(BF16)
    half_of_lane = lax.broadcasted_iota(jnp.int32, (1, PAGE_SIZE), 1) // SEL_BLOCK
    s_parts, m_parts = [], []
    for s in range(k_past):
        kt = _pad_dt(k_refs[s][...]).astype(BF16)
        s_parts.append(jnp.dot(q, kt, preferred_element_type=F32) * SCALE)
        m_parts.append(jnp.broadcast_to(half_of_lane == (idx_ref[b, g, s] & (bpp - 1)), (Q_PAD_ROWS, PAGE_SIZE)))
    s_old = jnp.concatenate(s_parts, axis=1)
    mk = jnp.concatenate(m_parts, axis=1)
    s_old = jnp.where(mk, s_old, NEG)
    s_all = lax.dot_general(q, knew_ref[...].astype(BF16), _NT, preferred_element_type=F32) * SCALE
    lane = lax.broadcasted_iota(jnp.int32, s_all.shape, 1)
    s_new = jnp.sum(jnp.where(lane == b, s_all, 0.0), axis=-1, keepdims=True)
    mx = jnp.maximum(jnp.max(s_old, axis=-1, keepdims=True), s_new)
    p_old = jnp.where(mk, jnp.exp(s_old - mx), 0.0)
    p_new = jnp.exp(s_new - mx)
    den = jnp.maximum(jnp.sum(p_old, axis=-1, keepdims=True) + p_new, 1e-30)
    v_new = vnew_ref[pl.ds(b, 1), :].astype(BF16).astype(F32)
    o = p_new.astype(BF16).astype(F32) * v_new
    for s in range(k_past):
        vt = _pad_dt(v_refs[s][...]).astype(BF16)
        ps = p_old[:, s * PAGE_SIZE:(s + 1) * PAGE_SIZE].astype(BF16)
        o = o + lax.dot_general(ps, vt, _NT, preferred_element_type=F32)
    o_ref[...] = o / den


def _sample_sel(pages5, page_table, idx, layer_base, q4, kvs_new, *, bd, past):
    k_past = min(TOP_K - 1, past // SEL_BLOCK)
    bpp_shift = (PAGE_SIZE // SEL_BLOCK).bit_length() - 1

    def blk_map(s, kv):
        def f(b, g, pt, ix):
            return (layer_base + pt[b, lax.shift_right_logical(ix[b, g, s], bpp_shift)], kv, g, 0, 0)
        return f

    tile = lambda s, kv: pl.BlockSpec((None, None, None, HEAD_DIM, PAGE_SIZE), blk_map(s, kv))
    grid_spec = pltpu.PrefetchScalarGridSpec(
        num_scalar_prefetch=2, grid=(bd, KV_HEADS),
        in_specs=[tile(s, 0) for s in range(k_past)] + [tile(s, 1) for s in range(k_past)]
        + [pl.BlockSpec((None, None, Q_PAD_ROWS, HSLOT), lambda b, g, pt, ix: (b, g, 0, 0)),
           pl.BlockSpec((SAMPLE_ROWS, HSLOT), lambda b, g, pt, ix: (0, g)),
           pl.BlockSpec((SAMPLE_ROWS, HSLOT), lambda b, g, pt, ix: (0, KV_HEADS + g))],
        out_specs=pl.BlockSpec((None, None, Q_PAD_ROWS, HSLOT), lambda b, g, pt, ix: (b, g, 0, 0)))
    return pl.pallas_call(
        functools.partial(_sample_sel_body, k_past=k_past), grid_spec=grid_spec,
        out_shape=jax.ShapeDtypeStruct((bd, KV_HEADS, Q_PAD_ROWS, HSLOT), F32),
        compiler_params=_cparams("parallel", "arbitrary"), name="sample_sel_attention",
    )(page_table, idx, *([pages5] * (2 * k_past)), q4, kvs_new, kvs_new)


def _sample_win_body(q_ref, win_ref, new_ref, ocmp_ref, osel_ref, gate_ref, o_ref, *, qpos, past, wb):
    b = pl.program_id(0)
    q = q_ref[...].astype(BF16)
    qf = q.astype(F32)
    rg = _group_rows(N_HEADS)
    new = new_ref[pl.ds(b, 1), :].astype(BF16).astype(F32)
    kpos = (past - wb) + lax.broadcasted_iota(jnp.int32, (1, wb), 1)
    mk = jnp.broadcast_to((kpos <= qpos) & (kpos >= qpos - WINDOW), (N_HEADS, wb))
    s_old = jnp.zeros((N_HEADS, wb), F32)
    s_new = jnp.zeros((N_HEADS, 1), F32)
    for g in range(KV_HEADS):
        kt = _pad_dt(win_ref[g]).astype(BF16)
        s_old = jnp.where(rg == g, jnp.dot(q, kt, preferred_element_type=F32) * SCALE, s_old)
        sn = jnp.sum(qf * new[:, g * HSLOT:(g + 1) * HSLOT], axis=-1, keepdims=True) * SCALE
        s_new = jnp.where(rg == g, sn, s_new)
    s_old = jnp.where(mk, s_old, NEG)
    mx = jnp.maximum(jnp.max(s_old, axis=-1, keepdims=True), s_new)
    p_old = jnp.where(mk, jnp.exp(s_old - mx), 0.0)
    p_new = jnp.exp(s_new - mx)
    den = jnp.maximum(jnp.sum(p_old, axis=-1, keepdims=True) + p_new, 1e-30)
    o_win = jnp.zeros((N_HEADS, HSLOT), F32)
    for g in range(KV_HEADS):
        vt = _pad_dt(win_ref[KV_HEADS + g]).astype(BF16)
        og = lax.dot_general(p_old.astype(BF16), vt, _NT, preferred_element_type=F32)
        og = og + p_new.astype(BF16).astype(F32) * new[:, (KV_HEADS + g) * HSLOT:(KV_HEADS + g + 1) * HSLOT]
        o_win = jnp.where(rg == g, og, o_win)
    o_win = o_win / den
    gts = gate_ref[...]
    o_ref[...] = gts[:, 0:1] * ocmp_ref[...] + gts[:, 1:2] * osel_ref[...] + gts[:, 2:3] * o_win


def _sample_win(q3, win4, layer, kvw_new, o_cmp, o_sel, gates3, *, bd, past, qpos):
    wb = win4.shape[-1]
    head3 = pl.BlockSpec((None, N_HEADS, HSLOT), lambda b: (b, 0, 0))
    return pl.pallas_call(
        functools.partial(_sample_win_body, qpos=qpos, past=past, wb=wb), grid=(bd,),
        in_specs=[head3, pl.BlockSpec((None, 2 * KV_HEADS, HEAD_DIM, wb), lambda b: (layer * bd + b, 0, 0, 0)),
                  pl.BlockSpec((SAMPLE_ROWS, KVP), lambda b: (0, 0)), head3, head3, head3],
        out_specs=head3,
        out_shape=jax.ShapeDtypeStruct((bd, N_HEADS, HSLOT), F32),
        compiler_params=_cparams("parallel"), name="sample_win_attention",
    )(q3, win4, kvw_new, o_cmp, o_sel, gates3)


def _window_shift_body(win_ref, new_ref, *refs, wb):
    o_ref = refs[-1]
    b = pl.program_id(0)
    new = new_ref[...]
    pick = lax.broadcasted_iota(jnp.int32, new.shape, 1) == b
    col = jnp.sum(jnp.where(pick, new, 0.0), axis=1, keepdims=True)
    lane = lax.broadcasted_iota(jnp.int32, (1, wb), 1)
    o_ref[...] = jnp.where(lane == wb - 1, col, pltpu.roll(win_ref[...], wb - 1, axis=1))


def _window_shift(win3, new_t, layer, n_layers, prev, *, bd):
    wb = win3.shape[-1]
    in_specs = [pl.BlockSpec((None, KVW, wb), lambda b: (layer * bd + b, 0, 0)),
                pl.BlockSpec((KVW, bd), lambda b: (0, 0))]
    operands = [win3, new_t]
    if prev is not None:
        in_specs.append(pl.BlockSpec(memory_space=pl.ANY))
        operands.append(prev)
    return pl.pallas_call(
        functools.partial(_window_shift_body, wb=wb), grid=(bd,), in_specs=in_specs,
        out_specs=pl.BlockSpec((None, None, KVW, wb), lambda b: (layer, b, 0, 0)),
        out_shape=jax.ShapeDtypeStruct((n_layers, bd, KVW, wb), F32),
        input_output_aliases={} if prev is None else {2: 0},
        compiler_params=_cparams("arbitrary"), name="sample_window_shift",
    )(*operands)


def _sample_conv_body(st_ref, u_ref, w_ref, b_ref, g_ref, bn_ref, o_ref, *, bd):
    w = w_ref[...]
    y = jnp.sum(st_ref[...] * w[None, :CONV_W - 1, :], axis=1) + u_ref[0:bd, :] * w[CONV_W - 1:CONV_W, :] + b_ref[...]
    y = _ln_rows(y, g_ref[...], bn_ref[...])
    o_ref[...] = y * jax.nn.sigmoid(y)


def _sample_conv(state, layer, u, cw, cb, cg, cbn, *, bd):
    vec = lambda a: a.reshape(1, CONV_CH)
    cst = lambda i: (0, 0)
    return pl.pallas_call(
        functools.partial(_sample_conv_body, bd=bd), grid=(1,),
        in_specs=[pl.BlockSpec((None, bd, CONV_W - 1, CONV_CH), lambda i: (layer, 0, 0, 0)),
                  pl.BlockSpec((SAMPLE_ROWS, CONV_CH), cst), pl.BlockSpec((CONV_W, CONV_CH), cst),
                  pl.BlockSpec((1, CONV_CH), cst), pl.BlockSpec((1, CONV_CH), cst), pl.BlockSpec((1, CONV_CH), cst)],
        out_specs=pl.BlockSpec((bd, CONV_CH), cst),
        out_shape=jax.ShapeDtypeStruct((bd, CONV_CH), F32),
        compiler_params=_cparams("arbitrary"), name="sample_conv",
    )(state, u, cw, vec(cb), vec(cg), vec(cbn))


def _sample_odd_body(st_ref, pin_ref, u_ref, vn_ref, pw_ref, ps_ref, w0_ref, b0_ref, o_ref, *, bd, start_pos):
    pin = pin_ref[0:bd, :]
    st = st_ref[...]
    for g, w in enumerate(POOL_WINDOWS):
        sl = slice(g * POOL_GC, (g + 1) * POOL_GC)
        tot = pin[:, sl] + jnp.sum(st[:, POOL_STATE - (w - 1):, sl], axis=1)
        d = tot / float(min(w, start_pos + 1)) - pin[:, sl]
        dp = jnp.concatenate([d, jnp.zeros((SAMPLE_ROWS - bd, POOL_GC), F32)], axis=0).astype(BF16)
        y = jnp.dot(dp, pw_ref[g], preferred_element_type=F32)[0:bd]
        o_ref[:, sl] = y * ps_ref[:, sl]
    mixed = w0_ref[...] * vn_ref[0:bd, :] + b0_ref[...]
    o_ref[:, POOL_CH:] = u_ref[0:bd, :] * mixed


def _sample_odd_mix(state, layer, pin, u, vn, pool_w, pool_scale, sgu_w, sgu_b, *, bd, start_pos):
    w0 = jnp.repeat(sgu_w[:, 0, 0], SGU_GC).reshape(1, SGU_CH)
    b0 = jnp.repeat(sgu_b[:, 0], SGU_GC).reshape(1, SGU_CH)
    cst = lambda i: (0, 0)
    return pl.pallas_call(
        functools.partial(_sample_odd_body, bd=bd, start_pos=start_pos), grid=(1,),
        in_specs=[pl.BlockSpec((None, bd, POOL_STATE, POOL_CH), lambda i: (layer, 0, 0, 0)),
                  pl.BlockSpec((SAMPLE_ROWS, POOL_CH), cst), pl.BlockSpec((SAMPLE_ROWS, SGU_CH), cst),
                  pl.BlockSpec((SAMPLE_ROWS, SGU_CH), cst),
                  pl.BlockSpec((POOL_GROUPS, POOL_GC, POOL_GC), lambda i: (0, 0, 0)),
                  pl.BlockSpec((1, POOL_CH), cst), pl.BlockSpec((1, SGU_CH), cst), pl.BlockSpec((1, SGU_CH), cst)],
        out_specs=pl.BlockSpec((bd, D_MODEL), cst),
        out_shape=jax.ShapeDtypeStruct((bd, D_MODEL), F32),
        compiler_params=_cparams("arbitrary"), name="sample_pool_sgu",
    )(state, pin, u, vn, pool_w.astype(BF16), pool_scale.reshape(1, POOL_CH), w0, b0)


def _pad_rows(x, rows):
    return jnp.pad(x, ((0, rows - x.shape[0]), (0, 0)))


def _split_to_nat(x, nheads):
    y = _unpad_heads(x, nheads)
    return _pad_nat(y.reshape(y.shape[:-1] + (nheads, HEAD_DIM))).reshape(x.shape)


def _nat_to_split(x, nheads):
    xh = x.reshape(x.shape[:-1] + (nheads, HSLOT))[..., :HEAD_DIM]
    return _pad_head(xh).reshape(x.shape)


def _even_in_proj(xb, xsb, w_in, w_gg, rope_p, rope_s, *, tm, bs, stack):
    nrep = rope_p[0].shape[0] // tm
    rs = xsb.shape[0]
    tab = lambda i, j: (i % nrep, 0)
    rope_ex = tuple((t, (tm, LANES), tab) for t in rope_p)
    rope_sx = tuple((t, (rs, LANES), lambda i, j: (0, 0)) for t in rope_s)
    gw = HPG * HSLOT
    qraw, qrot, qraw_s, qrot_s = _mm(
        xb, w_in, tm=tm, tn=2 * gw, n_off=E_Q, n_cols=QW, epilogue=_ep_q, extras=rope_ex,
        outs=((QW, BF16, 2 * gw), (QW, BF16, 2 * gw)), name="even_in_q",
        side=(xsb, _ep_q, rope_sx, ((QW, BF16, 2 * gw), (QW, BF16, 2 * gw))))
    nst = bs[1] // tm
    e, n_even, t_prev = stack
    row = {"f32": (KVP, F32, KVP), "bf16": (KVP, BF16, KVP),
           "t": ((n_even, bs[0], KVW, bs[1]), F32, (None, None, KVW, tm), lambda i, j: (e, i // nst, 0, i % nst))}
    kv_p, kv_s = [], []
    for sec, (off, want) in enumerate(zip((E_KVC, E_KVS, E_KVW), (("f32", "t"), ("bf16", "t"), ("bf16", "t")))):
        res = _mm(xb, w_in, tm=tm, tn=KVP, n_off=off, n_cols=KVP,
                  epilogue=functools.partial(_ep_kv, rope=sec > 0, want=want),
                  extras=rope_ex if sec > 0 else (), outs=tuple(row[k] for k in want),
                  name=("even_in_kvc", "even_in_kvs", "even_in_kvw")[sec],
                  side=(xsb, functools.partial(_ep_kv, rope=sec > 0, want=("f32",)),
                        rope_sx if sec > 0 else (), (row["f32"],)),
                  alias=None if t_prev is None else (t_prev[sec], 1))
        kv_p.append(res[:2])
        kv_s.append(res[2])
    gg_outs = ((CONV_CH, F32, CONV_CH), (GATE_W, F32, GATE_W))
    u, gates, u_s, gates_s = _mm(xb, w_gg, tm=tm, tn=GG_W, n_off=0, n_cols=GG_W, epilogue=_ep_glu_gates,
                                 outs=gg_outs, name="even_in_glu_gates", side=(xsb, _ep_glu_gates, (), gg_outs))
    return (qraw, qrot, kv_p, u, gates), (qraw_s, qrot_s, kv_s, u_s, gates_s)


def _mlp_up_cast(xb, xsb, w1, w2, layer, *, tm, tn):
    m, k = xb.shape
    rs = xsb.shape[0]
    dff = w1.shape[2]
    d_out = w2.shape[2]
    ni, nj = m // tm, dff // tn
    slab = dff // (ni * nj)
    assert m % tm == 0 and dff % tn == 0 and dff % (ni * nj) == 0 and slab % SAMPLE_ROWS == 0
    rsub = min(tm, ROW_SUB)

    def body(x_ref, xs_ref, w1_ref, w2_ref, h_ref, hs_ref, w2b_ref, w1b_ref):
        def act(rows_ref, rows):
            a = jnp.maximum(jnp.dot(rows_ref[rows, :], w1b_ref[...], preferred_element_type=F32), 0.0)
            return (a * a).astype(BF16)

        @pl.when(pl.program_id(1) == 0)
        def _():
            w1b_ref[...] = w1_ref[...].astype(BF16)
            hs_ref[...] = act(xs_ref, slice(0, rs))

        w2b_ref[...] = w2_ref[...].astype(BF16)
        for r in range(tm // rsub):
            rows = slice(r * rsub, (r + 1) * rsub)
            h_ref[rows, :] = act(x_ref, rows)

    return pl.pallas_call(
        body, grid=(nj, ni),
        in_specs=[pl.BlockSpec((tm, k), lambda j, i: (i, 0)), pl.BlockSpec((rs, k), lambda j, i: (0, 0)),
                  pl.BlockSpec((None, k, tn), lambda j, i: (layer, 0, j)),
                  pl.BlockSpec((None, slab, d_out), lambda j, i: (layer, j * ni + i, 0))],
        out_specs=[pl.BlockSpec((tm, tn), lambda j, i: (i, j)), pl.BlockSpec((rs, tn), lambda j, i: (0, j)),
                   pl.BlockSpec((slab, d_out), lambda j, i: (j * ni + i, 0))],
        out_shape=[jax.ShapeDtypeStruct((m, dff), BF16), jax.ShapeDtypeStruct((rs, dff), BF16),
                   jax.ShapeDtypeStruct((dff, d_out), BF16)],
        scratch_shapes=[pltpu.VMEM((k, tn), BF16)],
        compiler_params=_cparams("arbitrary", "arbitrary"), name="mlp_up_cast",
    )(xb, xsb, w1, w2)


def kernel(x_prompt, x_sample, cache_cmp_kv, cache_sel_kv, cache_win_kv, state_conv, state_pool, page_table,
           w_in_even, w_out_even, cmp_pe_k, cmp_pe_v, cmp_w_k, cmp_w_v, conv_w, conv_b, conv_ln_g, conv_ln_b,
           w_in_odd, w_out_odd, pool_w, pool_scale, sgu_ln_g, sgu_ln_b, sgu_w, sgu_b,
           mlp_w1, mlp_w2, ln_mix_g, ln_mix_b, ln_ffn_g, ln_ffn_b):
    B, S, D = x_prompt.shape
    Bd, Sd, _ = x_sample.shape
    n_pages = page_table.shape[1]
    past = n_pages * PAGE_SIZE
    n_even, n_pool = cache_cmp_kv.shape[:2]
    wb = cache_win_kv.shape[2]
    assert D == D_MODEL and Sd == 1 and Bd <= SAMPLE_ROWS
    assert S % 1024 == 0 and past % SEL_BLOCK == 0 and S >= WINDOW
    M = B * S
    Ms = SAMPLE_ROWS
    tm_p = 1024

    rope_p = _rope_tables(jnp.arange(S, dtype=jnp.int32))
    rope_s = _rope_tables(jnp.full((Ms,), past, jnp.int32))
    pps = min(32, n_pages)
    assert n_pages % pps == 0
    bsum = _block_sum_matrices(pps)
    cmp_t = cache_cmp_kv.transpose(0, 1, 3, 4, 5, 2).reshape(n_even * n_pool, KVW, PAGE_SIZE)
    sel_t = cache_sel_kv.transpose(0, 1, 3, 4, 5, 2).reshape(n_even * n_pool, 2, KV_HEADS, HEAD_DIM, PAGE_SIZE)
    win_t = cache_win_kv.transpose(0, 1, 3, 4, 5, 2).reshape(n_even * Bd, 2 * KV_HEADS, HEAD_DIM, wb)

    xp = x_prompt.reshape(M, D)
    xs = _pad_rows(x_sample.reshape(Bd, D), Ms)
    xpb, xsb = xp.astype(BF16), xs.astype(BF16)

    outs = {k: [] for k in ("cmp_s", "sel_s", "win_s", "conv_p", "conv_s", "pool_p", "pool_s", "sgu_p", "sgu_s")}
    kv6 = lambda a, lead: a.reshape(lead + (2, KV_HEADS, HEAD_DIM))
    kv_t = None
    win_s_t = None

    for layer in range(DEPTH):
        if layer % 2 == 0:
            e = layer // 2
            wts = _prep_even_weights(w_in_even[e], w_out_even[e], cmp_pe_k[e], cmp_pe_v[e], cmp_w_k[e], cmp_w_v[e])
            prj_p, prj_s = _even_in_proj(xpb, xsb, wts["w_qkv"], wts["w_gg"], rope_p, rope_s, tm=tm_p, bs=(B, S),
                                         stack=(e, n_even, kv_t))
            qraw, qrot, ((kvc, kvc_t), (kvs_b, kvs_t), (kvw_b, kvw_t)), u, gates = prj_p
            kv_t = (kvc_t, kvs_t, kvw_t)
            summ = _compress_rows(kvc, wts["pe"], rows=512, name="prompt_compress")
            (ckv,) = _mm(summ.astype(BF16), wts["big_p"], tm=min(summ.shape[0], 512), tn=KVP,
                         n_off=0, n_cols=KVP, epilogue=_ep_plain, outs=((KVP, F32, KVP),), name="prompt_compress_map")
            o_att = _prompt_attention(qraw, qrot, gates, ckv, kvs_b, kvw_b, batch=B, seq=S, tq=256, tk=256, ngrp=4)
            c = _prompt_conv(u, conv_w[e], conv_b[e], conv_ln_g[e], conv_ln_b[e], batch=B, seq=S, ts=256)
            outs["conv_p"].append(u.reshape(B, S, CONV_CH)[:, S - (CONV_W - 1):])
            qraw_s, qrot_s, (kvc_s, kvs_s, kvw_s), u_s, gates_s = prj_s
            ckv_t = _compress_pages(cmp_t, page_table, wts["pe_t"], wts["big_t"], bsum, e * n_pool, pps=pps,
                                    name="sample_compress")
            q3 = qraw_s.astype(F32)[:Bd].reshape(Bd, N_HEADS, HSLOT)
            o_cmp, idx = _sample_cmp(q3, ckv_t, bd=Bd, past=past, qpos=past, nbl=pps * (PAGE_SIZE // CMP_BLOCK))
            idx = idx[:, :, 0].reshape(Bd, KV_HEADS, TOP_K)
            qr3 = _split_to_nat(qrot_s.astype(F32)[:Bd], N_HEADS).reshape(Bd, N_HEADS, HSLOT)
            q4 = jnp.pad(qr3.reshape(Bd, KV_HEADS, HPG, HSLOT), ((0, 0), (0, 0), (0, Q_PAD_ROWS - HPG), (0, 0)))
            kvs_nat = _split_to_nat(kvs_s, 2 * KV_HEADS)
            kvw_nat = _split_to_nat(kvw_s, 2 * KV_HEADS)
            o_sel = _sample_sel(sel_t, page_table, idx, e * n_pool, q4, kvs_nat, bd=Bd, past=past)
            o_sel = o_sel[:, :, :HPG].reshape(Bd, N_HEADS, HSLOT)
            g3 = gates_s[:Bd, :3 * N_HEADS].reshape(Bd, KV_HEADS, 3, HPG)
            g3 = g3.transpose(0, 1, 3, 2).reshape(Bd, N_HEADS, 3)
            g3 = jnp.pad(g3, ((0, 0), (0, 0), (0, LANES - 3)))
            o_s = _sample_win(qr3, win_t, e, kvw_nat, o_cmp, o_sel, g3, bd=Bd, past=past, qpos=past)
            c_s = _sample_conv(state_conv, e, u_s, conv_w[e], conv_b[e], conv_ln_g[e], conv_ln_b[e], bd=Bd)
            o_sb = _pad_rows(_nat_to_split(o_s.reshape(Bd, QW), N_HEADS), Ms).astype(BF16)
            c_sb = _pad_rows(c_s, Ms).astype(BF16)
            xp, xpb, xs, xsb = _proj_ln([o_att, c], [wts["wo_att"], wts["wo_conv"]], xp, ln_mix_g[layer],
                                        ln_mix_b[layer], tm=512, name="even_out_ln", side=([o_sb, c_sb], xs))
            kvc_c = _unpad_heads(kvc_s[:Bd], 2 * KV_HEADS)
            kvs_c = _unpad_heads(kvs_s[:Bd], 2 * KV_HEADS)
            kvw_c = _unpad_heads(kvw_s[:Bd], 2 * KV_HEADS)
            outs["cmp_s"].append(kv6(kvc_c, (Bd, 1)))
            outs["sel_s"].append(kv6(kvs_c, (Bd, 1)))
            if wb == WINDOW:
                win_s_t = _window_shift(win_t.reshape(n_even * Bd, KVW, wb), kvw_c.T, e, n_even, win_s_t, bd=Bd)
            else:
                wkv = jnp.concatenate([cache_win_kv[e], kv6(kvw_c, (Bd, 1))], axis=1)
                outs["win_s"].append(wkv[:, wkv.shape[1] - min(WINDOW, wkv.shape[1]):])
            outs["conv_s"].append(jnp.concatenate([state_conv[e], u_s[:Bd, None, :]], axis=1)[:, 1:])
        else:
            o = layer // 2
            w_in = w_in_odd[o]
            wb_in = w_in.astype(BF16)
            w_pu, w_v = wb_in[:, :POOL_CH + SGU_CH], wb_in[:, POOL_CH + SGU_CH:]
            w_out_p = w_out_odd[o].astype(BF16)
            lg, lb = sgu_ln_g[o].reshape(1, SGU_CH), sgu_ln_b[o].reshape(1, SGU_CH)

            gl_ex = ((lg, (1, SGU_CH), lambda i, j: (0, j)), (lb, (1, SGU_CH), lambda i, j: (0, j)))
            o_v, o_pu = ((SGU_CH, F32, SGU_CH),), ((POOL_CH, F32, POOL_CH), (SGU_CH, F32, SGU_CH))
            vn, vn_s = _mm(xpb, w_v, tm=tm_p, tn=SGU_CH, n_off=0, n_cols=SGU_CH, epilogue=_ep_gelu_gln,
                           extras=gl_ex, outs=o_v, name="odd_in_v", side=(xsb, _ep_gelu_gln, gl_ex, o_v))
            pin, uu, pin_s, uu_s = _mm(xpb, w_pu, tm=tm_p, tn=POOL_CH + SGU_CH, n_off=0, n_cols=POOL_CH + SGU_CH,
                                       epilogue=_ep_pool_gelu, outs=o_pu, name="odd_in_pool_u",
                                       side=(xsb, _ep_pool_gelu, (), o_pu))
            cat = _prompt_odd_mix(pin, uu, vn, pool_w[o], pool_scale[o], sgu_w[o], sgu_b[o], batch=B, seq=S, ts=512)
            outs["pool_p"].append(pin.reshape(B, S, POOL_CH)[:, S - POOL_STATE:])
            outs["sgu_p"].append(vn.reshape(B, S, SGU_CH)[:, ((S - 1) // CHUNK) * CHUNK:])
            cat_s = _sample_odd_mix(state_pool, o, pin_s, uu_s, vn_s, pool_w[o], pool_scale[o], sgu_w[o], sgu_b[o],
                                    bd=Bd, start_pos=past)
            xp, xpb, xs, xsb = _proj_ln([cat], [w_out_p], xp, ln_mix_g[layer], ln_mix_b[layer], tm=512,
                                        name="odd_out_ln", side=([_pad_rows(cat_s, Ms).astype(BF16)], xs))
            outs["pool_s"].append(jnp.concatenate([state_pool[o], pin_s[:Bd, None, :]], axis=1)[:, 1:])
            outs["sgu_s"].append(vn_s[:Bd, None, :])
        h, h_s, w2b = _mlp_up_cast(xpb, xsb, mlp_w1, mlp_w2, layer, tm=2 * tm_p, tn=1024)
        xp, xpb, xs, xsb = _matmul_res_ln([h], [w2b], xp, ln_ffn_g[layer], ln_ffn_b[layer], tm=512, tn=512,
                                          name="mlp_down_ln", side=([h_s], xs))

    st = lambda k: jnp.stack(outs[k])
    rows_last = lambda a, nb=B: a.reshape(n_even, nb, 2, KV_HEADS, HEAD_DIM, a.shape[-1]).transpose(0, 1, 5, 2, 3, 4)
    kvc_t, kvs_t, kvw_t = kv_t
    return (xp.reshape(B, S, D), xs[:Bd].reshape(Bd, Sd, D),
            rows_last(kvc_t), st("cmp_s"), rows_last(kvs_t), st("sel_s"),
            rows_last(kvw_t[:, :, :, S - WINDOW:]),
            st("win_s") if win_s_t is None else rows_last(win_s_t, Bd),
            st("conv_p"), st("conv_s"), st("pool_p"), st("pool_s"), st("sgu_p"), st("sgu_s"))
```

```python
import functools

import jax
import jax.numpy as jnp
from jax import lax
from jax.experimental import pallas as pl
from jax.experimental.pallas import tpu as pltpu

F32 = jnp.float32
BF16 = jnp.bfloat16

D_MODEL = 2048
DEPTH = 4
PAGE_SIZE = 128
N_HEADS = 16
HEAD_DIM = 96
KV_HEADS = 4
HPG = N_HEADS // KV_HEADS
ATT_W = N_HEADS * HEAD_DIM
KVW = 2 * KV_HEADS * HEAD_DIM
CMP_BLOCK = 32
SEL_BLOCK = 64
TOP_K = 16
WINDOW = 512
ROPE_THETA = 10000.0
SCALE = HEAD_DIM ** -0.5
LOG2E = 1.4426950408889634
FORCE = 1e9
NEG = -1e30
CONV_CH = D_MODEL // 4
CONV_W = 31
POOL_CH = D_MODEL // 4
POOL_WINDOWS = (2, 4, 8, 16)
POOL_GROUPS = len(POOL_WINDOWS)
POOL_GC = POOL_CH // POOL_GROUPS
POOL_STATE = max(POOL_WINDOWS) - 1
SGU_CH = D_MODEL - POOL_CH
SGU_GROUPS = 4
SGU_GC = SGU_CH // SGU_GROUPS
CHUNK = 128
D_FF = 4 * D_MODEL
ALPHA = (2 * DEPTH) ** 0.25
LN_EPS = 1e-5

LANES = 128
SUBLANES = 8
HALF = HEAD_DIM // 2
HSLOT = LANES
HALF_OFF = LANES // 2
QW = N_HEADS * HSLOT
KVP = 2 * KV_HEADS * HSLOT
GATE_W = LANES
SAMPLE_ROWS = 16
ROW_SUB = 256
LN_ROW_SUB = 128
VMEM_LIMIT = 52 * 1024 * 1024

E_Q, E_KVC, E_KVS, E_KVW = 0, QW, QW + KVP, QW + 2 * KVP
GG_W = 2 * CONV_CH + GATE_W


def _cparams(*sem):
    return pltpu.CompilerParams(dimension_semantics=sem, vmem_limit_bytes=VMEM_LIMIT)


def _pad_head(x):
    halves = x.reshape(x.shape[:-1] + (2, HALF))
    halves = jnp.pad(halves, [(0, 0)] * (halves.ndim - 1) + [(0, HALF_OFF - HALF)])
    return halves.reshape(x.shape[:-1] + (HSLOT,))


def _pad_nat(x):
    return jnp.concatenate([x, jnp.zeros(x.shape[:-1] + (HSLOT - HEAD_DIM,), x.dtype)], axis=-1)


def _unpad_heads(x, nheads):
    xh = x.reshape(x.shape[:-1] + (nheads, HSLOT))
    y = jnp.concatenate([xh[..., :HALF], xh[..., HALF_OFF:HALF_OFF + HALF]], axis=-1)
    return y.reshape(x.shape[:-1] + (nheads * HEAD_DIM,))


def _rope_tables(pos):
    inv = jnp.power(ROPE_THETA, -jnp.arange(HALF, dtype=F32) / HALF)
    ang = pos.astype(F32)[:, None] * inv[None, :]
    cos, sin = jnp.cos(ang), jnp.sin(ang)
    z = jnp.zeros((pos.shape[0], HALF_OFF - HALF), F32)
    return (jnp.concatenate([cos, z, cos, z], axis=1),
            jnp.concatenate([-sin, z, sin, z], axis=1))


def _block_diag2(a, b):
    za = jnp.zeros((a.shape[0], b.shape[1]), a.dtype)
    zb = jnp.zeros((b.shape[0], a.shape[1]), a.dtype)
    return jnp.concatenate([jnp.concatenate([a, za], axis=1), jnp.concatenate([zb, b], axis=1)], axis=0)


def _prep_even_weights(w_in, w_out, pe_k, pe_v, w_ck, w_cv):
    d = w_in.shape[0]
    wb = w_in.astype(BF16)
    g0 = ATT_W + 3 * KVW
    n_heads_all = g0 // HEAD_DIM
    qkv = _pad_head(wb[:, :g0].reshape(d, n_heads_all, HEAD_DIM)).reshape(d, n_heads_all * HSLOT)
    gates = wb[:, g0:g0 + 3 * N_HEADS].reshape(d, 3, KV_HEADS, HPG).transpose(0, 2, 1, 3)
    gates = jnp.pad(gates.reshape(d, 3 * N_HEADS), ((0, 0), (0, GATE_W - 3 * N_HEADS)))
    w_gg = jnp.concatenate([wb[:, g0 + 3 * N_HEADS:], gates], axis=1)
    wo = w_out.astype(BF16)
    wo_att = jnp.pad(wo[:ATT_W].reshape(2 * N_HEADS, HALF, d), ((0, 0), (0, HALF_OFF - HALF), (0, 0)))
    wo_att = wo_att.reshape(QW, d)
    wo_conv = wo[ATT_W:]
    eye = jnp.eye(KV_HEADS, dtype=F32)
    pe = jnp.concatenate([jnp.tile(_pad_head(pe_k), (1, KV_HEADS)), jnp.tile(_pad_head(pe_v), (1, KV_HEADS))], axis=1)
    wk_full = _pad_head(_pad_head(w_ck).T).T
    wv_full = _pad_head(_pad_head(w_cv).T).T
    big_p = _block_diag2(jnp.kron(eye, wk_full), jnp.kron(eye, wv_full))
    pe_t = jnp.concatenate([jnp.tile(jnp.tile(pe_k.T, (1, PAGE_SIZE // CMP_BLOCK)), (KV_HEADS, 1)),
                            jnp.tile(jnp.tile(pe_v.T, (1, PAGE_SIZE // CMP_BLOCK)), (KV_HEADS, 1))], axis=0)
    big_t = _block_diag2(jnp.kron(eye, _pad_head(w_ck).T), jnp.kron(eye, _pad_nat(w_cv).T))
    return dict(w_qkv=qkv, w_gg=w_gg, wo_att=wo_att, wo_conv=wo_conv, pe=pe, pe_t=pe_t,
                big_p=big_p.astype(BF16), big_t=big_t.astype(BF16))


def _block_sum_matrices(pps):
    bpp = PAGE_SIZE // CMP_BLOCK
    nbl = bpp * pps
    p = jnp.arange(pps)[:, None, None]
    i = (jnp.arange(PAGE_SIZE) // CMP_BLOCK)[None, :, None]
    c = jnp.arange(nbl)[None, None, :]
    col = (i % 2) * (nbl // 2) + (bpp // 2) * p + i // 2
    return jnp.where(c == col, 1.0 / CMP_BLOCK, 0.0).astype(BF16)


def _ln_rows(y, g, b):
    mu = jnp.mean(y, axis=-1, keepdims=True)
    yc = y - mu
    var = jnp.mean(yc * yc, axis=-1, keepdims=True)
    return yc * lax.rsqrt(var + LN_EPS) * g + b


def _rope_slot(x, cos, sin):
    return x * cos + pltpu.roll(x, HALF_OFF, axis=1) * sin


_NT = (((1,), (1,)), ((), ()))


def _pad_rows_to(x, rows):
    return jnp.concatenate([x, jnp.zeros((rows - x.shape[0],) + x.shape[1:], x.dtype)], axis=0)


def _mm(x, w, *, tm, tn, n_off, n_cols, epilogue, extras=(), outs, name, side=None, alias=None):
    m, k = x.shape
    assert m % tm == 0 and n_cols % tn == 0 and n_off % tn == 0
    joff = n_off // tn
    ji = lambda im: (lambda j, i: im(i, j))
    in_specs = [pl.BlockSpec((tm, k), lambda j, i: (i, 0)),
                pl.BlockSpec((k, tn), lambda j, i: (0, joff + j))]
    in_specs += [pl.BlockSpec(bs, ji(im)) for _, bs, im in extras]
    operands = [x, w] + [a for a, _, _ in extras]
    out_shape, out_specs = [], []
    for o in outs:
        if len(o) == 3:
            out_shape.append(jax.ShapeDtypeStruct((m, o[0]), o[1]))
            out_specs.append(pl.BlockSpec((tm, o[2]), lambda j, i: (i, j)))
        else:
            out_shape.append(jax.ShapeDtypeStruct(o[0], o[1]))
            out_specs.append(pl.BlockSpec(o[2], ji(o[3])))
    ne, no = len(extras), len(outs)
    nse = 0
    if side is not None:
        xs, s_epilogue, s_extras, s_outs = side
        rs = xs.shape[0]
        nse = len(s_extras)
        in_specs += [pl.BlockSpec((rs, k), lambda j, i: (0, 0))]
        in_specs += [pl.BlockSpec(bs, ji(im)) for _, bs, im in s_extras]
        operands += [xs] + [a for a, _, _ in s_extras]
        for cols, dt, bc in s_outs:
            out_shape.append(jax.ShapeDtypeStruct((rs, cols), dt))
            out_specs.append(pl.BlockSpec((rs, bc), lambda j, i: (0, j)))
    rsub = min(tm, ROW_SUB)
    n_in = len(operands)
    io_alias = {}
    if alias is not None:
        io_alias = {n_in: alias[1]}
        in_specs.append(pl.BlockSpec(memory_space=pl.ANY))
        operands.append(alias[0])

    def body(*refs):
        x_ref, w_ref = refs[:2]
        ex = refs[2:2 + ne]
        o_refs = refs[len(operands):len(operands) + no]
        for r in range(tm // rsub):
            rows = slice(r * rsub, (r + 1) * rsub)
            acc = jnp.dot(x_ref[rows, :], w_ref[...], preferred_element_type=F32)
            epilogue(acc, ex, o_refs, rows)
        if side is not None:
            @pl.when(pl.program_id(1) == 0)
            def _():
                acc = jnp.dot(refs[2 + ne][...], w_ref[...], preferred_element_type=F32)
                s_epilogue(acc, refs[3 + ne:n_in], refs[len(operands) + no:], slice(0, rs))

    return pl.pallas_call(
        body, grid=(n_cols // tn, m // tm), in_specs=in_specs, out_specs=out_specs, out_shape=out_shape,
        input_output_aliases=io_alias, compiler_params=_cparams("arbitrary", "arbitrary"), name=name,
    )(*operands)


def _ep_q(acc, ex, outs, rows):
    cos, sin = ex[0][rows, :], ex[1][rows, :]
    outs[0][rows, :] = acc.astype(BF16)
    for j in range(acc.shape[1] // HSLOT):
        sl = slice(j * HSLOT, (j + 1) * HSLOT)
        outs[1][rows, sl] = _rope_slot(acc[:, sl], cos, sin).astype(BF16)


def _ep_kv(acc, ex, outs, rows, *, rope, want):
    o = dict(zip(want, outs))
    if rope:
        cos, sin = ex[0][rows, :], ex[1][rows, :]
    for j in range(2 * KV_HEADS):
        sl = slice(j * HSLOT, (j + 1) * HSLOT)
        x = acc[:, sl]
        if rope and j < KV_HEADS:
            x = _rope_slot(x, cos, sin)
        if "f32" in o:
            o["f32"][rows, sl] = x
        if "bf16" in o:
            o["bf16"][rows, sl] = x.astype(BF16)
        if "t" in o:
            xt = x.T
            o["t"][j * HEAD_DIM:j * HEAD_DIM + HALF, rows] = xt[0:HALF]
            o["t"][j * HEAD_DIM + HALF:(j + 1) * HEAD_DIM, rows] = xt[HALF_OFF:HALF_OFF + HALF]


def _ep_glu_gates(acc, ex, outs, rows):
    outs[0][rows, :] = acc[:, :CONV_CH] * jax.nn.sigmoid(acc[:, CONV_CH:2 * CONV_CH])
    outs[1][rows, :] = jax.nn.sigmoid(acc[:, 2 * CONV_CH:])


def _ep_plain(acc, ex, outs, rows):
    outs[0][rows, :] = acc.astype(outs[0].dtype)


def _ep_relu2(acc, ex, outs, rows):
    r = jnp.maximum(acc, 0.0)
    outs[0][rows, :] = (r * r).astype(outs[0].dtype)


def _ep_pool_gelu(acc, ex, outs, rows):
    outs[0][rows, :] = acc[:, :POOL_CH]
    outs[1][rows, :] = jax.nn.gelu(acc[:, POOL_CH:])


def _ep_gelu_gln(acc, ex, outs, rows):
    g, b = ex[0][...], ex[1][...]
    v = jax.nn.gelu(acc)
    for j in range(acc.shape[1] // SGU_GC):
        sl = slice(j * SGU_GC, (j + 1) * SGU_GC)
        outs[0][rows, sl] = _ln_rows(v[:, sl], g[:, sl], b[:, sl])


def _proj_ln(a_list, w_list, resid, g, b, *, tm, name, side):
    m, n = resid.shape
    as_list, resid_s = side
    rs = resid_s.shape[0]
    npair = len(a_list)
    in_specs = []
    for a in a_list:
        in_specs.append(pl.BlockSpec((tm, a.shape[1]), lambda i: (i, 0)))
    for w in w_list:
        in_specs.append(pl.BlockSpec(w.shape, lambda i: (0, 0), pipeline_mode=pl.Buffered(1)))
    in_specs += [pl.BlockSpec((tm, n), lambda i: (i, 0)),
                 pl.BlockSpec((1, n), lambda i: (0, 0)), pl.BlockSpec((1, n), lambda i: (0, 0))]
    for a in as_list:
        in_specs.append(pl.BlockSpec((rs, a.shape[1]), lambda i: (0, 0)))
    in_specs.append(pl.BlockSpec((rs, n), lambda i: (0, 0)))

    def body(*refs):
        a_refs, w_refs = refs[:npair], refs[npair:2 * npair]
        r_ref, g_ref, b_ref = refs[2 * npair:2 * npair + 3]
        as_refs = refs[2 * npair + 3:3 * npair + 3]
        rs_ref, o_ref, ob_ref, os_ref, osb_ref = refs[3 * npair + 3:]

        def rows_out(a_rs, res, rows):
            acc = ALPHA * res[rows, :]
            for a_ref, w_ref in zip(a_rs, w_refs):
                acc = acc + jnp.dot(a_ref[rows, :], w_ref[...], preferred_element_type=F32)
            return _ln_rows(acc, g_ref[...], b_ref[...])

        rsub = min(tm, LN_ROW_SUB)
        for r in range(tm // rsub):
            rows = slice(r * rsub, (r + 1) * rsub)
            y = rows_out(a_refs, r_ref, rows)
            o_ref[rows, :] = y
            ob_ref[rows, :] = y.astype(BF16)

        @pl.when(pl.program_id(0) == 0)
        def _():
            ys = rows_out(as_refs, rs_ref, slice(0, rs))
            os_ref[...] = ys
            osb_ref[...] = ys.astype(BF16)

    row = pl.BlockSpec((tm, n), lambda i: (i, 0))
    srow = pl.BlockSpec((rs, n), lambda i: (0, 0))
    return pl.pallas_call(
        body, grid=(m // tm,), in_specs=in_specs, out_specs=[row, row, srow, srow],
        out_shape=[jax.ShapeDtypeStruct((m, n), F32), jax.ShapeDtypeStruct((m, n), BF16),
                   jax.ShapeDtypeStruct((rs, n), F32), jax.ShapeDtypeStruct((rs, n), BF16)],
        compiler_params=_cparams("arbitrary"), name=name,
    )(*a_list, *w_list, resid, g.reshape(1, n), b.reshape(1, n), *as_list, resid_s)


def _matmul_res_ln(a_list, w_list, resid, g, b, *, tm, tn, name, side):
    m, n = resid.shape
    nj = n // tn
    as_list, resid_s = side
    rs = resid_s.shape[0]
    npair = len(a_list)
    rsub, rsub_ln = min(tm, ROW_SUB), min(tm, LN_ROW_SUB)

    def body(*refs):
        h_ref, w_refs = refs[:npair], refs[npair:2 * npair]
        r_ref, g_ref, b_ref = refs[2 * npair:2 * npair + 3]
        hs_ref = refs[2 * npair + 3:3 * npair + 3]
        rs_ref, o_ref, ob_ref, os_ref, osb_ref = refs[3 * npair + 3:]
        j = pl.program_id(1)
        first = pl.program_id(0) == 0
        srows = slice(0, rs)

        def tile(hr, rr, rows):
            acc = ALPHA * rr[rows, :]
            for a_ref, w_ref in zip(hr, w_refs):
                acc = acc + jnp.dot(a_ref[rows, :], w_ref[...], preferred_element_type=F32)
            return acc

        def finish(o, ob, y, rows):
            y = _ln_rows(y, g_ref[...], b_ref[...])
            o[rows, :] = y
            ob[rows, :] = y.astype(BF16)

        for jj in range(nj - 1):
            @pl.when(j == jj)
            def _(jj=jj):
                cols = slice(jj * tn, (jj + 1) * tn)
                for r in range(tm // rsub):
                    rows = slice(r * rsub, (r + 1) * rsub)
                    o_ref[rows, cols] = tile(h_ref, r_ref, rows)

                @pl.when(first)
                def _():
                    os_ref[:, cols] = tile(hs_ref, rs_ref, srows)

        @pl.when(j == nj - 1)
        def _():
            done = slice(0, (nj - 1) * tn)
            for r in range(tm // rsub_ln):
                rows = slice(r * rsub_ln, (r + 1) * rsub_ln)
                finish(o_ref, ob_ref, jnp.concatenate([o_ref[rows, done], tile(h_ref, r_ref, rows)], axis=1), rows)

            @pl.when(first)
            def _():
                finish(os_ref, osb_ref, jnp.concatenate([os_ref[:, done], tile(hs_ref, rs_ref, srows)], axis=1), srows)

    assert nj > 1
    row = pl.BlockSpec((tm, n), lambda i, j: (i, 0))
    srow = pl.BlockSpec((rs, n), lambda i, j: (0, 0))
    cst = lambda i, j: (0, 0)
    in_specs = [pl.BlockSpec((tm, a.shape[1]), lambda i, j: (i, 0)) for a in a_list]
    in_specs += [pl.BlockSpec((w.shape[0], tn), lambda i, j: (0, j)) for w in w_list]
    in_specs += [pl.BlockSpec((tm, tn), lambda i, j: (i, j)), pl.BlockSpec((1, n), cst), pl.BlockSpec((1, n), cst)]
    in_specs += [pl.BlockSpec((rs, a.shape[1]), cst) for a in as_list]
    in_specs += [pl.BlockSpec((rs, tn), lambda i, j: (0, j))]
    return pl.pallas_call(
        body, grid=(m // tm, nj), in_specs=in_specs, out_specs=[row, row, srow, srow],
        out_shape=[jax.ShapeDtypeStruct((m, n), F32), jax.ShapeDtypeStruct((m, n), BF16),
                   jax.ShapeDtypeStruct((rs, n), F32), jax.ShapeDtypeStruct((rs, n), BF16)],
        compiler_params=_cparams("arbitrary", "arbitrary"), name=name,
    )(*a_list, *w_list, resid, g.reshape(1, n), b.reshape(1, n), *as_list, resid_s)


def _compress_rows(kvc, pe, *, rows, name):
    m, c = kvc.shape
    nb = rows // CMP_BLOCK

    def body(x_ref, pe_ref, o_ref):
        x = x_ref[...].reshape(nb, CMP_BLOCK, c) * pe_ref[...][None]
        o_ref[...] = jnp.sum(x, axis=1) * (1.0 / CMP_BLOCK)

    return pl.pallas_call(
        body, grid=(m // rows,),
        in_specs=[pl.BlockSpec((rows, c), lambda i: (i, 0)), pl.BlockSpec((CMP_BLOCK, c), lambda i: (0, 0))],
        out_specs=pl.BlockSpec((nb, c), lambda i: (i, 0)),
        out_shape=jax.ShapeDtypeStruct((m // CMP_BLOCK, c), F32),
        compiler_params=_cparams("parallel"), name=name,
    )(kvc, pe)


def _compress_pages(pages_t, page_table, pe_t, big_t, bsum, layer_base, *, pps, name):
    bd, n_pages = page_table.shape
    bpp = PAGE_SIZE // CMP_BLOCK
    nbl = bpp * pps
    nsteps = n_pages // pps

    def body(pt_ref, *refs):
        page_refs = refs[:pps]
        pe_ref, big_ref, bsum_ref, o_ref = refs[pps:]
        acc = jnp.zeros((KVW, nbl), F32)
        for p in range(pps):
            x = page_refs[p][...] * pe_ref[...]
            acc = acc + jnp.dot(x.astype(BF16), bsum_ref[p], preferred_element_type=F32)
        o_ref[...] = jnp.dot(big_ref[...], acc.astype(BF16), preferred_element_type=F32)

    def page_map(p):
        return lambda b, j, pt: (layer_base + pt[b, j * pps + p], 0, 0)

    cst2 = lambda b, j, pt: (0, 0)
    grid_spec = pltpu.PrefetchScalarGridSpec(
        num_scalar_prefetch=1, grid=(bd, nsteps),
        in_specs=[pl.BlockSpec((None, KVW, PAGE_SIZE), page_map(p)) for p in range(pps)]
        + [pl.BlockSpec((KVW, PAGE_SIZE), cst2), pl.BlockSpec((KVP, KVW), cst2),
           pl.BlockSpec((pps, PAGE_SIZE, nbl), lambda b, j, pt: (0, 0, 0))],
        out_specs=pl.BlockSpec((None, KVP, nbl), lambda b, j, pt: (b, 0, j)))
    return pl.pallas_call(
        body, grid_spec=grid_spec,
        out_shape=jax.ShapeDtypeStruct((bd, KVP, n_pages * bpp), F32),
        compiler_params=_cparams("parallel", "arbitrary"), name=name,
    )(page_table, *([pages_t] * pps), pe_t, big_t, bsum)


def _flash_step_t(q, k, vt, valid, m_ref, l_ref, acc_ref):
    bias = jnp.where(valid, 0.0, NEG)
    s = jnp.dot(k, q, preferred_element_type=F32) + jnp.concatenate([bias] * HPG, axis=1)
    m_prev = m_ref[...]
    m_new = jnp.maximum(m_prev, jnp.max(s, axis=0, keepdims=True))
    alpha = jnp.exp2(m_prev - m_new)
    p = jnp.exp2(s - m_new)
    l_ref[...] = alpha * l_ref[...] + jnp.sum(p, axis=0, keepdims=True)
    acc_ref[...] = alpha * acc_ref[...] + jnp.dot(vt, p.astype(BF16), preferred_element_type=F32)
    m_ref[...] = m_new


def _attn_body(*refs, tq, tk, seq, k_top, ngrp):
    n_in = 9
    ins = [refs[g * n_in:(g + 1) * n_in] for g in range(ngrp)]
    o_ref = refs[ngrp * n_in]
    m_ref, l_ref, acc_ref, sel_ref, vts_ref, vtw_ref = refs[ngrp * n_in + 1:]
    i = pl.program_id(2)
    cols = HPG * tq
    nsb = seq // SEL_BLOCK
    nt = seq // tk
    bpt = tk // SEL_BLOCK
    q0 = i * tq
    lane_q = lax.broadcasted_iota(jnp.int32, (1, cols), 1)
    qpos = q0 + (lane_q & (tq - 1))
    qp1 = q0 + lax.broadcasted_iota(jnp.int32, (1, tq), 1)

    @pl.when(i == 0)
    def _():
        def tr(t, c):
            for g in range(ngrp):
                vs_ref, vw_ref = ins[g][6], ins[g][8]
                for h in range(tk // LANES):
                    k0 = pl.multiple_of(t * tk + h * LANES, LANES)
                    hs = slice(h * LANES, (h + 1) * LANES)
                    vts_ref[g, t, :, hs] = vs_ref[pl.ds(k0, LANES), :].astype(F32).T.astype(BF16)
                    vtw_ref[g, t, :, hs] = vw_ref[pl.ds(k0, LANES), :].astype(F32).T.astype(BF16)
            return c
        lax.fori_loop(0, nt, tr, 0)

    def heads_t(ref, scale=1.0):
        parts = [ref[:, j * HSLOT:(j + 1) * HSLOT].astype(F32).T * scale for j in range(HPG)]
        return jnp.concatenate(parts, axis=1).astype(BF16)

    r = lax.broadcasted_iota(jnp.int32, (2 * nsb, 1), 0)
    n_of = jnp.where(r < nsb, 2 * r, 2 * (r - nsb) + 1)
    mk = ((n_of + 1) * CMP_BLOCK - 1) <= qpos
    sb = lax.broadcasted_iota(jnp.int32, (nsb, 1), 0)
    vis = (sb * SEL_BLOCK) <= qp1
    cur = sb == (qp1 >> (SEL_BLOCK.bit_length() - 1))

    def compressed_branch(g):
        qraw_ref, ck_ref, cv_ref = ins[g][0], ins[g][3], ins[g][4]
        qr = heads_t(qraw_ref)
        ck = jnp.concatenate([ck_ref[pl.ds(0, nsb, stride=2), :], ck_ref[pl.ds(1, nsb, stride=2), :]], axis=0)
        s = jnp.dot(ck.astype(BF16), qr, preferred_element_type=F32) * SCALE
        s = jnp.where(mk, s, NEG)
        mx = jnp.max(s, axis=0, keepdims=True)
        p = jnp.where(mk, jnp.exp(s - mx), 0.0)
        pn = p * (1.0 / jnp.maximum(jnp.sum(p, axis=0, keepdims=True), 1e-30))
        cv = jnp.concatenate([cv_ref[pl.ds(0, nsb, stride=2), :], cv_ref[pl.ds(1, nsb, stride=2), :]], axis=0)
        cvt = _pad_rows_to(cv, LANES).T.astype(BF16)
        o_cmp = jnp.dot(cvt, _pad_rows_to(pn, LANES).astype(BF16), preferred_element_type=F32)
        pp = pn[0:nsb] + pn[nsb:2 * nsb]
        imp = pp[:, 0:tq]
        for j in range(1, HPG):
            imp = imp + pp[:, j * tq:(j + 1) * tq]
        imp = jnp.where(cur, FORCE, jnp.where(vis, imp, -FORCE))
        cnt = jnp.zeros((nsb, tq), F32)
        for j in range(nsb):
            rowj = imp[j:j + 1, :]
            beats = (rowj > imp) | ((rowj == imp) & (j < sb))
            cnt = cnt + jnp.where(beats, 1.0, 0.0)
        sel = jnp.where(cnt < k_top, 1.0, 0.0)
        for t in range(nt):
            sel_ref[g, t, 0:bpt, :] = sel[bpt * t:bpt * (t + 1), :]
        return o_cmp

    o_cmp = [compressed_branch(g) for g in range(ngrp)]
    qt = [heads_t(ins[g][1], SCALE * LOG2E) for g in range(ngrp)]
    rowk = lax.broadcasted_iota(jnp.int32, (tk, 1), 0)

    def reset():
        m_ref[...] = jnp.full((ngrp, 1, cols), NEG, F32)
        l_ref[...] = jnp.zeros((ngrp, 1, cols), F32)
        acc_ref[...] = jnp.zeros((ngrp, HSLOT, cols), F32)

    def result(g):
        return acc_ref[g] * (1.0 / jnp.maximum(l_ref[g], 1e-30))

    reset()

    def sel_step(t, c):
        k0 = pl.multiple_of(t * tk, tk)
        kp = k0 + rowk
        for g in range(ngrp):
            sm = sel_ref[g, t, bpt - 1:bpt, :]
            for j in range(bpt - 2, -1, -1):
                sm = jnp.where(rowk < (j + 1) * SEL_BLOCK, sel_ref[g, t, j:j + 1, :], sm)
            valid = (kp <= qp1) & (sm > 0.5)
            _flash_step_t(qt[g], ins[g][5][pl.ds(k0, tk), :], vts_ref[g, t], valid,
                          m_ref.at[g], l_ref.at[g], acc_ref.at[g])
        return c

    t_end = lax.div(q0 + tq + tk - 1, tk)
    lax.fori_loop(0, t_end, sel_step, 0)
    o_sel = [result(g) for g in range(ngrp)]

    reset()

    def win_step(t, c):
        k0 = pl.multiple_of(t * tk, tk)
        kp = k0 + rowk
        valid = (kp <= qp1) & (kp >= qp1 - WINDOW)
        for g in range(ngrp):
            _flash_step_t(qt[g], ins[g][7][pl.ds(k0, tk), :], vtw_ref[g, t], valid,
                          m_ref.at[g], l_ref.at[g], acc_ref.at[g])
        return c

    lax.fori_loop(lax.div(jnp.maximum(q0 - WINDOW, 0), tk), t_end, win_step, 0)

    gw = HPG * HSLOT
    for g in range(ngrp):
        gt = ins[g][2][...].T
        o = jnp.zeros((HSLOT, cols), F32)
        for br, o_br in enumerate((o_cmp[g], o_sel[g], result(g))):
            r0 = (g * 3 + br) * HPG
            grow = jnp.concatenate([gt[r0 + j:r0 + j + 1, :] for j in range(HPG)], axis=1)
            o = o + grow * o_br
        for j in range(HPG):
            o_ref[:, g * gw + j * HSLOT:g * gw + (j + 1) * HSLOT] = o[:, j * tq:(j + 1) * tq].T.astype(BF16)


def _prompt_attention(qraw, qrot, gates, ckv, kvs_b, kvw_b, *, batch, seq, tq, tk, ngrp):
    nq = seq // tq
    ncb = seq // CMP_BLOCK
    nt = seq // tk
    cols = HPG * tq
    k_top = min(TOP_K, seq // SEL_BLOCK)
    assert tq & (tq - 1) == 0 and seq % tk == 0 and tk % LANES == 0 and tk // SEL_BLOCK <= SUBLANES
    assert ngrp == KV_HEADS
    body = functools.partial(_attn_body, tq=tq, tk=tk, seq=seq, k_top=k_top, ngrp=ngrp)
    gw = HPG * HSLOT
    in_specs, operands = [], []
    for g in range(ngrp):
        qmap = functools.partial(lambda b, p, i, g: (b * nq + i, p * ngrp + g), g=g)
        kmap = functools.partial(lambda b, p, i, g: (b, p * ngrp + g), g=g)
        vmap_ = functools.partial(lambda b, p, i, g: (b, KV_HEADS + p * ngrp + g), g=g)
        in_specs += [pl.BlockSpec((tq, gw), qmap), pl.BlockSpec((tq, gw), qmap),
                     pl.BlockSpec((tq, GATE_W), lambda b, p, i: (b * nq + i, 0)),
                     pl.BlockSpec((ncb, HSLOT), kmap), pl.BlockSpec((ncb, HSLOT), vmap_),
                     pl.BlockSpec((seq, HSLOT), kmap), pl.BlockSpec((seq, HSLOT), vmap_),
                     pl.BlockSpec((seq, HSLOT), kmap), pl.BlockSpec((seq, HSLOT), vmap_)]
        operands += [qraw, qrot, gates, ckv, ckv, kvs_b, kvs_b, kvw_b, kvw_b]
    return pl.pallas_call(
        body, grid=(batch, KV_HEADS // ngrp, nq), in_specs=in_specs,
        out_specs=pl.BlockSpec((tq, ngrp * gw), lambda b, p, i: (b * nq + i, p)),
        out_shape=jax.ShapeDtypeStruct((batch * seq, QW), BF16),
        scratch_shapes=[pltpu.VMEM((ngrp, 1, cols), F32), pltpu.VMEM((ngrp, 1, cols), F32),
                        pltpu.VMEM((ngrp, HSLOT, cols), F32), pltpu.VMEM((ngrp, nt, SUBLANES, tq), F32),
                        pltpu.VMEM((ngrp, nt, HSLOT, tk), BF16), pltpu.VMEM((ngrp, nt, HSLOT, tk), BF16)],
        compiler_params=_cparams("parallel", "parallel", "arbitrary"), name="prompt_attention",
    )(*operands)


CONV_HALO = 32


def _conv_body(cur_ref, prev_ref, w_ref, b_ref, g_ref, bn_ref, o_ref, ext_ref, *, ts):
    c = pl.program_id(1)
    span = CONV_HALO + ts - SUBLANES
    ext_ref[0, 0:CONV_HALO, :] = jnp.where(c > 0, prev_ref[...], 0.0)
    ext_ref[0, CONV_HALO:CONV_HALO + ts, :] = cur_ref[...]
    for s in range(1, SUBLANES):
        ext_ref[s, 0:span, :] = ext_ref[0, pl.ds(s, span), :]
    acc = jnp.zeros((ts, CONV_CH), F32) + b_ref[...]
    off = CONV_HALO - (CONV_W - 1)
    for k in range(CONV_W):
        s = (off + k) % SUBLANES
        acc = acc + ext_ref[s, pl.ds(off + k - s, ts), :] * w_ref[k:k + 1, :]
    y = _ln_rows(acc, g_ref[...], bn_ref[...])
    o_ref[...] = (y * jax.nn.sigmoid(y)).astype(BF16)


def _prompt_conv(u, cw, cb, cg, cbn, *, batch, seq, ts):
    nt = seq // ts
    r = ts // CONV_HALO
    cwp = jnp.pad(cw, ((0, CONV_HALO - CONV_W), (0, 0)))
    vec = lambda a: a.reshape(1, CONV_CH)
    cst = lambda b, c: (0, 0)
    return pl.pallas_call(
        functools.partial(_conv_body, ts=ts), grid=(batch, nt),
        in_specs=[pl.BlockSpec((ts, CONV_CH), lambda b, c: (b * nt + c, 0)),
                  pl.BlockSpec((CONV_HALO, CONV_CH), lambda b, c: (jnp.maximum((b * nt + c) * r - 1, 0), 0)),
                  pl.BlockSpec((CONV_HALO, CONV_CH), cst),
                  pl.BlockSpec((1, CONV_CH), cst), pl.BlockSpec((1, CONV_CH), cst), pl.BlockSpec((1, CONV_CH), cst)],
        out_specs=pl.BlockSpec((ts, CONV_CH), lambda b, c: (b * nt + c, 0)),
        out_shape=jax.ShapeDtypeStruct((batch * seq, CONV_CH), BF16),
        scratch_shapes=[pltpu.VMEM((SUBLANES, CONV_HALO + ts, CONV_CH), F32)],
        compiler_params=_cparams("parallel", "arbitrary"), name="prompt_conv",
    )(u, u, cwp, vec(cb), vec(cg), vec(cbn))


POOL_HALO = 16


def _odd_mix_body(pin_ref, prev_ref, u_ref, vn_ref, pw_ref, ps_ref, sw_ref, sb_ref, o_ref, ext_ref, *, ts):
    c = pl.program_id(1)
    ext_ref[0:POOL_HALO, :] = jnp.where(c > 0, prev_ref[...], 0.0)
    ext_ref[POOL_HALO:POOL_HALO + ts, :] = pin_ref[...]
    t = c * ts + lax.broadcasted_iota(jnp.int32, (ts, 1), 0)
    for g, w in enumerate(POOL_WINDOWS):
        sl = slice(g * POOL_GC, (g + 1) * POOL_GC)
        tot = ext_ref[pl.ds(POOL_HALO, ts), sl]
        for j in range(1, w):
            tot = tot + ext_ref[pl.ds(POOL_HALO - j, ts), sl]
        cnt = jnp.minimum(w, t + 1).astype(F32)
        d = tot / cnt - pin_ref[:, sl]
        y = jnp.dot(d.astype(BF16), pw_ref[g], preferred_element_type=F32)
        o_ref[:, sl] = (y * ps_ref[:, sl]).astype(BF16)
    ri = lax.broadcasted_iota(jnp.int32, (CHUNK, CHUNK), 0)
    ci = lax.broadcasted_iota(jnp.int32, (CHUNK, CHUNK), 1)
    for g in range(SGU_GROUPS):
        sl = slice(g * SGU_GC, (g + 1) * SGU_GC)
        ws = jnp.where(ci <= ri, sw_ref[g], 0.0).astype(BF16)
        for q in range(ts // CHUNK):
            rows = slice(q * CHUNK, (q + 1) * CHUNK)
            mixed = jnp.dot(ws, vn_ref[rows, sl].astype(BF16), preferred_element_type=F32) + sb_ref[:, g:g + 1]
            o_ref[rows, POOL_CH + g * SGU_GC:POOL_CH + (g + 1) * SGU_GC] = (u_ref[rows, sl] * mixed).astype(BF16)


def _prompt_odd_mix(pin, u, vn, pool_w, pool_scale, sgu_w, sgu_b, *, batch, seq, ts):
    assert seq % ts == 0 and ts % CHUNK == 0
    nt = seq // ts
    r = ts // POOL_HALO
    cst2 = lambda b, c: (0, 0)
    cst3 = lambda b, c: (0, 0, 0)
    row = lambda b, c: (b * nt + c, 0)
    return pl.pallas_call(
        functools.partial(_odd_mix_body, ts=ts), grid=(batch, nt),
        in_specs=[pl.BlockSpec((ts, POOL_CH), row),
                  pl.BlockSpec((POOL_HALO, POOL_CH), lambda b, c: (jnp.maximum((b * nt + c) * r - 1, 0), 0)),
                  pl.BlockSpec((ts, SGU_CH), row), pl.BlockSpec((ts, SGU_CH), row),
                  pl.BlockSpec((POOL_GROUPS, POOL_GC, POOL_GC), cst3), pl.BlockSpec((1, POOL_CH), cst2),
                  pl.BlockSpec((SGU_GROUPS, CHUNK, CHUNK), cst3), pl.BlockSpec((CHUNK, SGU_GROUPS), cst2)],
        out_specs=pl.BlockSpec((ts, D_MODEL), row),
        out_shape=jax.ShapeDtypeStruct((batch * seq, D_MODEL), BF16),
        scratch_shapes=[pltpu.VMEM((POOL_HALO + ts, POOL_CH), F32)],
        compiler_params=_cparams("parallel", "arbitrary"), name="prompt_pool_sgu",
    )(pin, pin, u, vn, pool_w.astype(BF16), pool_scale.reshape(1, POOL_CH), sgu_w, sgu_b.T)


def _group_rows(nrows):
    return lax.broadcasted_iota(jnp.int32, (nrows, 1), 0) >> (HPG.bit_length() - 1)


def _sample_cmp_body(q_ref, ckv_ref, o_ref, idx_ref, *, qpos, ncb, nbl, k_past):
    q = q_ref[...].astype(BF16)
    rg = _group_rows(N_HEADS)
    half = nbl // 2
    assert nbl & (nbl - 1) == 0
    sh = nbl.bit_length() - 1
    lane = lax.broadcasted_iota(jnp.int32, (1, ncb), 1)
    grp, w = lane >> sh, lane & (nbl - 1)
    n_cmp = grp * nbl + 2 * (w & (half - 1)) + (w >> (sh - 1))
    mk = jnp.broadcast_to(((n_cmp + 1) * CMP_BLOCK - 1) <= qpos, (N_HEADS, ncb))
    s = jnp.zeros((N_HEADS, ncb), F32)
    for g in range(KV_HEADS):
        ck = ckv_ref[g * HSLOT:(g + 1) * HSLOT, :].astype(BF16)
        s = jnp.where(rg == g, jnp.dot(q, ck, preferred_element_type=F32) * SCALE, s)
    s = jnp.where(mk, s, NEG)
    mx = jnp.max(s, axis=-1, keepdims=True)
    p = jnp.where(mk, jnp.exp(s - mx), 0.0)
    pn = p / jnp.maximum(jnp.sum(p, axis=-1, keepdims=True), 1e-30)
    o = jnp.zeros((N_HEADS, HSLOT), F32)
    for g in range(KV_HEADS):
        cv = ckv_ref[(KV_HEADS + g) * HSLOT:(KV_HEADS + g + 1) * HSLOT, :].astype(BF16)
        o = jnp.where(rg == g, lax.dot_general(pn.astype(BF16), cv, _NT, preferred_element_type=F32), o)
    o_ref[...] = o
    pair = pn + pltpu.roll(pn, ncb - half, axis=1)
    valid = w < half
    sb = grp * half + w
    vis = (sb * SEL_BLOCK) <= qpos
    ri = lax.broadcasted_iota(jnp.int32, (ncb, ncb), 0)
    sb_r = (ri >> sh) * half + (ri & (nbl - 1))
    sb_c = jnp.broadcast_to(sb, (ncb, ncb))
    slot = lax.broadcasted_iota(jnp.int32, (TOP_K, 1), 0)
    for g in range(KV_HEADS):
        imp = jnp.sum(jnp.where(rg == g, pair, 0.0), axis=0, keepdims=True)
        imp = jnp.where(valid, jnp.where(vis, imp, -FORCE), -2.0 * FORCE)
        a = jnp.broadcast_to(imp, (ncb, ncb))
        bt = a.T
        beats = (bt > a) | ((bt == a) & (sb_r < sb_c))
        rank = jnp.sum(jnp.where(beats, 1.0, 0.0), axis=0, keepdims=True)
        onehot = jnp.where((rank == slot.astype(F32)) & valid, 1.0, 0.0)
        idx = jnp.sum(onehot * sb.astype(F32), axis=-1, keepdims=True)
        idx = jnp.where(slot < k_past, idx, 0.0)
        idx_ref[g * TOP_K:(g + 1) * TOP_K, :] = jnp.broadcast_to(idx, (TOP_K, LANES)).astype(jnp.int32)


def _sample_cmp(q3, ckv_t, *, bd, past, qpos, nbl):
    ncb = past // CMP_BLOCK
    k_past = min(TOP_K - 1, past // SEL_BLOCK)
    body = functools.partial(_sample_cmp_body, qpos=qpos, ncb=ncb, nbl=nbl, k_past=k_past)
    head3 = pl.BlockSpec((None, N_HEADS, HSLOT), lambda b: (b, 0, 0))
    return pl.pallas_call(
        body, grid=(bd,),
        in_specs=[head3, pl.BlockSpec((None, KVP, ncb), lambda b: (b, 0, 0))],
        out_specs=[head3, pl.BlockSpec((None, KV_HEADS * TOP_K, LANES), lambda b: (b, 0, 0))],
        out_shape=[jax.ShapeDtypeStruct((bd, N_HEADS, HSLOT), F32),
                   jax.ShapeDtypeStruct((bd, KV_HEADS * TOP_K, LANES), jnp.int32)],
        compiler_params=_cparams("parallel"), name="sample_cmp_attention",
    )(q3, ckv_t)


Q_PAD_ROWS = 8


def _pad_dt(x):
    return _pad_rows_to(x, HSLOT)


def _sample_sel_body(pt_ref, idx_ref, *refs, k_past):
    kv_refs = refs[:k_past]
    k_refs = [r.at[0] for r in kv_refs]
    v_refs = [r.at[1] for r in kv_refs]
    q_ref, knew_ref, vnew_ref, o_ref = refs[k_past:]
    b, g = pl.program_id(0), pl.program_id(1)
    bpp = PAGE_SIZE // SEL_BLOCK
    q = q_ref[...].astype(BF16)
    half_of_lane = lax.broadcasted_iota(jnp.int32, (1, PAGE_SIZE), 1) // SEL_BLOCK
    s_parts, m_parts = [], []
    for s in range(k_past):
        kt = _pad_dt(k_refs[s][...]).astype(BF16)
        s_parts.append(jnp.dot(q, kt, preferred_element_type=F32) * SCALE)
        m_parts.append(jnp.broadcast_to(half_of_lane == (idx_ref[b, g, s] & (bpp - 1)), (Q_PAD_ROWS, PAGE_SIZE)))
    s_old = jnp.concatenate(s_parts, axis=1)
    mk = jnp.concatenate(m_parts, axis=1)
    s_old = jnp.where(mk, s_old, NEG)
    s_all = lax.dot_general(q, knew_ref[...].astype(BF16), _NT, preferred_element_type=F32) * SCALE
    lane = lax.broadcasted_iota(jnp.int32, s_all.shape, 1)
    s_new = jnp.sum(jnp.where(lane == b, s_all, 0.0), axis=-1, keepdims=True)
    mx = jnp.maximum(jnp.max(s_old, axis=-1, keepdims=True), s_new)
    p_old = jnp.where(mk, jnp.exp(s_old - mx), 0.0)
    p_new = jnp.exp(s_new - mx)
    den = jnp.maximum(jnp.sum(p_old, axis=-1, keepdims=True) + p_new, 1e-30)
    v_new = vnew_ref[pl.ds(b, 1), :].astype(BF16).astype(F32)
    o = p_new.astype(BF16).astype(F32) * v_new
    for s in range(k_past):
        vt = _pad_dt(v_refs[s][...]).astype(BF16)
        ps = p_old[:, s * PAGE_SIZE:(s + 1) * PAGE_SIZE].astype(BF16)
        o = o + lax.dot_general(ps, vt, _NT, preferred_element_type=F32)
    o_ref[...] = o / den


def _sample_sel(pages5, page_table, idx, layer_base, q4, kvs_new, *, bd, past):
    k_past = min(TOP_K - 1, past // SEL_BLOCK)
    bpp_shift = (PAGE_SIZE // SEL_BLOCK).bit_length() - 1

    def blk_map(s):
        def f(b, g, pt, ix):
            return (layer_base + pt[b, lax.shift_right_logical(ix[b, g, s], bpp_shift)], 0, g, 0, 0)
        return f

    tile = lambda s: pl.BlockSpec((None, 2, None, HEAD_DIM, PAGE_SIZE), blk_map(s))
    grid_spec = pltpu.PrefetchScalarGridSpec(
        num_scalar_prefetch=2, grid=(bd, KV_HEADS),
        in_specs=[tile(s) for s in range(k_past)]
        + [pl.BlockSpec((None, None, Q_PAD_ROWS, HSLOT), lambda b, g, pt, ix: (b, g, 0, 0)),
           pl.BlockSpec((SAMPLE_ROWS, HSLOT), lambda b, g, pt, ix: (0, g)),
           pl.BlockSpec((SAMPLE_ROWS, HSLOT), lambda b, g, pt, ix: (0, KV_HEADS + g))],
        out_specs=pl.BlockSpec((None, None, Q_PAD_ROWS, HSLOT), lambda b, g, pt, ix: (b, g, 0, 0)))
    return pl.pallas_call(
        functools.partial(_sample_sel_body, k_past=k_past), grid_spec=grid_spec,
        out_shape=jax.ShapeDtypeStruct((bd, KV_HEADS, Q_PAD_ROWS, HSLOT), F32),
        compiler_params=_cparams("parallel", "arbitrary"), name="sample_sel_attention",
    )(page_table, idx, *([pages5] * k_past), q4, kvs_new, kvs_new)


def _sample_win_body(q_ref, win_ref, new_ref, ocmp_ref, osel_ref, gate_ref, o_ref, *, qpos, past, wb):
    b = pl.program_id(0)
    q = q_ref[...].astype(BF16)
    qf = q.astype(F32)
    rg = _group_rows(N_HEADS)
    new = new_ref[pl.ds(b, 1), :].astype(BF16).astype(F32)
    kpos = (past - wb) + lax.broadcasted_iota(jnp.int32, (1, wb), 1)
    mk = jnp.broadcast_to((kpos <= qpos) & (kpos >= qpos - WINDOW), (N_HEADS, wb))
    s_old = jnp.zeros((N_HEADS, wb), F32)
    s_new = jnp.zeros((N_HEADS, 1), F32)
    for g in range(KV_HEADS):
        kt = _pad_dt(win_ref[g]).astype(BF16)
        s_old = jnp.where(rg == g, jnp.dot(q, kt, preferred_element_type=F32) * SCALE, s_old)
        sn = jnp.sum(qf * new[:, g * HSLOT:(g + 1) * HSLOT], axis=-1, keepdims=True) * SCALE
        s_new = jnp.where(rg == g, sn, s_new)
    s_old = jnp.where(mk, s_old, NEG)
    mx = jnp.maximum(jnp.max(s_old, axis=-1, keepdims=True), s_new)
    p_old = jnp.where(mk, jnp.exp(s_old - mx), 0.0)
    p_new = jnp.exp(s_new - mx)
    den = jnp.maximum(jnp.sum(p_old, axis=-1, keepdims=True) + p_new, 1e-30)
    o_win = jnp.zeros((N_HEADS, HSLOT), F32)
    for g in range(KV_HEADS):
        vt = _pad_dt(win_ref[KV_HEADS + g]).astype(BF16)
        og = lax.dot_general(p_old.astype(BF16), vt, _NT, preferred_element_type=F32)
        og = og + p_new.astype(BF16).astype(F32) * new[:, (KV_HEADS + g) * HSLOT:(KV_HEADS + g + 1) * HSLOT]
        o_win = jnp.where(rg == g, og, o_win)
    o_win = o_win / den
    gts = gate_ref[...]
    o_ref[...] = gts[:, 0:1] * ocmp_ref[...] + gts[:, 1:2] * osel_ref[...] + gts[:, 2:3] * o_win


def _sample_win(q3, win4, layer, kvw_new, o_cmp, o_sel, gates3, *, bd, past, qpos):
    wb = win4.shape[-1]
    head3 = pl.BlockSpec((None, N_HEADS, HSLOT), lambda b: (b, 0, 0))
    return pl.pallas_call(
        functools.partial(_sample_win_body, qpos=qpos, past=past, wb=wb), grid=(bd,),
        in_specs=[head3, pl.BlockSpec((None, 2 * KV_HEADS, HEAD_DIM, wb), lambda b: (layer * bd + b, 0, 0, 0)),
                  pl.BlockSpec((SAMPLE_ROWS, KVP), lambda b: (0, 0)), head3, head3, head3],
        out_specs=head3,
        out_shape=jax.ShapeDtypeStruct((bd, N_HEADS, HSLOT), F32),
        compiler_params=_cparams("parallel"), name="sample_win_attention",
    )(q3, win4, kvw_new, o_cmp, o_sel, gates3)


def _window_shift_body(win_ref, new_ref, *refs, wb):
    o_ref = refs[-1]
    b = pl.program_id(0)
    new = new_ref[...]
    pick = lax.broadcasted_iota(jnp.int32, new.shape, 1) == b
    col = jnp.sum(jnp.where(pick, new, 0.0), axis=1, keepdims=True)
    lane = lax.broadcasted_iota(jnp.int32, (1, wb), 1)
    o_ref[...] = jnp.where(lane == wb - 1, col, pltpu.roll(win_ref[...], wb - 1, axis=1))


def _window_shift(win3, new_t, layer, n_layers, prev, *, bd):
    wb = win3.shape[-1]
    in_specs = [pl.BlockSpec((None, KVW, wb), lambda b: (layer * bd + b, 0, 0)),
                pl.BlockSpec((KVW, bd), lambda b: (0, 0))]
    operands = [win3, new_t]
    if prev is not None:
        in_specs.append(pl.BlockSpec(memory_space=pl.ANY))
        operands.append(prev)
    return pl.pallas_call(
        functools.partial(_window_shift_body, wb=wb), grid=(bd,), in_specs=in_specs,
        out_specs=pl.BlockSpec((None, None, KVW, wb), lambda b: (layer, b, 0, 0)),
        out_shape=jax.ShapeDtypeStruct((n_layers, bd, KVW, wb), F32),
        input_output_aliases={} if prev is None else {2: 0},
        compiler_params=_cparams("arbitrary"), name="sample_window_shift",
    )(*operands)


def _sample_conv_body(st_ref, u_ref, w_ref, b_ref, g_ref, bn_ref, o_ref, *, bd):
    w = w_ref[...]
    y = jnp.sum(st_ref[...] * w[None, :CONV_W - 1, :], axis=1) + u_ref[0:bd, :] * w[CONV_W - 1:CONV_W, :] + b_ref[...]
    y = _ln_rows(y, g_ref[...], bn_ref[...])
    o_ref[...] = y * jax.nn.sigmoid(y)


def _sample_conv(state, layer, u, cw, cb, cg, cbn, *, bd):
    vec = lambda a: a.reshape(1, CONV_CH)
    cst = lambda i: (0, 0)
    return pl.pallas_call(
        functools.partial(_sample_conv_body, bd=bd), grid=(1,),
        in_specs=[pl.BlockSpec((None, bd, CONV_W - 1, CONV_CH), lambda i: (layer, 0, 0, 0)),
                  pl.BlockSpec((SAMPLE_ROWS, CONV_CH), cst), pl.BlockSpec((CONV_W, CONV_CH), cst),
                  pl.BlockSpec((1, CONV_CH), cst), pl.BlockSpec((1, CONV_CH), cst), pl.BlockSpec((1, CONV_CH), cst)],
        out_specs=pl.BlockSpec((bd, CONV_CH), cst),
        out_shape=jax.ShapeDtypeStruct((bd, CONV_CH), F32),
        compiler_params=_cparams("arbitrary"), name="sample_conv",
    )(state, u, cw, vec(cb), vec(cg), vec(cbn))


def _sample_odd_body(st_ref, pin_ref, u_ref, vn_ref, pw_ref, ps_ref, w0_ref, b0_ref, o_ref, *, bd, start_pos):
    pin = pin_ref[0:bd, :]
    st = st_ref[...]
    for g, w in enumerate(POOL_WINDOWS):
        sl = slice(g * POOL_GC, (g + 1) * POOL_GC)
        tot = pin[:, sl] + jnp.sum(st[:, POOL_STATE - (w - 1):, sl], axis=1)
        d = tot / float(min(w, start_pos + 1)) - pin[:, sl]
        dp = jnp.concatenate([d, jnp.zeros((SAMPLE_ROWS - bd, POOL_GC), F32)], axis=0).astype(BF16)
        y = jnp.dot(dp, pw_ref[g], preferred_element_type=F32)[0:bd]
        o_ref[:, sl] = y * ps_ref[:, sl]
    mixed = w0_ref[...] * vn_ref[0:bd, :] + b0_ref[...]
    o_ref[:, POOL_CH:] = u_ref[0:bd, :] * mixed


def _sample_odd_mix(state, layer, pin, u, vn, pool_w, pool_scale, sgu_w, sgu_b, *, bd, start_pos):
    w0 = jnp.repeat(sgu_w[:, 0, 0], SGU_GC).reshape(1, SGU_CH)
    b0 = jnp.repeat(sgu_b[:, 0], SGU_GC).reshape(1, SGU_CH)
    cst = lambda i: (0, 0)
    return pl.pallas_call(
        functools.partial(_sample_odd_body, bd=bd, start_pos=start_pos), grid=(1,),
        in_specs=[pl.BlockSpec((None, bd, POOL_STATE, POOL_CH), lambda i: (layer, 0, 0, 0)),
                  pl.BlockSpec((SAMPLE_ROWS, POOL_CH), cst), pl.BlockSpec((SAMPLE_ROWS, SGU_CH), cst),
                  pl.BlockSpec((SAMPLE_ROWS, SGU_CH), cst),
                  pl.BlockSpec((POOL_GROUPS, POOL_GC, POOL_GC), lambda i: (0, 0, 0)),
                  pl.BlockSpec((1, POOL_CH), cst), pl.BlockSpec((1, SGU_CH), cst), pl.BlockSpec((1, SGU_CH), cst)],
        out_specs=pl.BlockSpec((bd, D_MODEL), cst),
        out_shape=jax.ShapeDtypeStruct((bd, D_MODEL), F32),
        compiler_params=_cparams("arbitrary"), name="sample_pool_sgu",
    )(state, pin, u, vn, pool_w.astype(BF16), pool_scale.reshape(1, POOL_CH), w0, b0)


def _pad_rows(x, rows):
    return jnp.pad(x, ((0, rows - x.shape[0]), (0, 0)))


def _split_to_nat(x, nheads):
    y = _unpad_heads(x, nheads)
    return _pad_nat(y.reshape(y.shape[:-1] + (nheads, HEAD_DIM))).reshape(x.shape)


def _nat_to_split(x, nheads):
    xh = x.reshape(x.shape[:-1] + (nheads, HSLOT))[..., :HEAD_DIM]
    return _pad_head(xh).reshape(x.shape)


def _even_in_proj(xb, xsb, w_in, w_gg, rope_p, rope_s, *, tm, bs, stack):
    nrep = rope_p[0].shape[0] // tm
    rs = xsb.shape[0]
    tab = lambda i, j: (i % nrep, 0)
    rope_ex = tuple((t, (tm, LANES), tab) for t in rope_p)
    rope_sx = tuple((t, (rs, LANES), lambda i, j: (0, 0)) for t in rope_s)
    gw = HPG * HSLOT
    qraw, qrot, qraw_s, qrot_s = _mm(
        xb, w_in, tm=tm, tn=2 * gw, n_off=E_Q, n_cols=QW, epilogue=_ep_q, extras=rope_ex,
        outs=((QW, BF16, 2 * gw), (QW, BF16, 2 * gw)), name="even_in_q",
        side=(xsb, _ep_q, rope_sx, ((QW, BF16, 2 * gw), (QW, BF16, 2 * gw))))
    nst = bs[1] // tm
    e, n_even, t_prev = stack
    row = {"f32": (KVP, F32, KVP), "bf16": (KVP, BF16, KVP),
           "t": ((n_even, bs[0], KVW, bs[1]), F32, (None, None, KVW, tm), lambda i, j: (e, i // nst, 0, i % nst))}
    kv_p, kv_s = [], []
    for sec, (off, want) in enumerate(zip((E_KVC, E_KVS, E_KVW), (("f32", "t"), ("bf16", "t"), ("bf16", "t")))):
        res = _mm(xb, w_in, tm=tm, tn=KVP, n_off=off, n_cols=KVP,
                  epilogue=functools.partial(_ep_kv, rope=sec > 0, want=want),
                  extras=rope_ex if sec > 0 else (), outs=tuple(row[k] for k in want),
                  name=("even_in_kvc", "even_in_kvs", "even_in_kvw")[sec],
                  side=(xsb, functools.partial(_ep_kv, rope=sec > 0, want=("f32",)),
                        rope_sx if sec > 0 else (), (row["f32"],)),
                  alias=None if t_prev is None else (t_prev[sec], 1))
        kv_p.append(res[:2])
        kv_s.append(res[2])
    gg_outs = ((CONV_CH, F32, CONV_CH), (GATE_W, F32, GATE_W))
    u, gates, u_s, gates_s = _mm(xb, w_gg, tm=tm, tn=GG_W, n_off=0, n_cols=GG_W, epilogue=_ep_glu_gates,
                                 outs=gg_outs, name="even_in_glu_gates", side=(xsb, _ep_glu_gates, (), gg_outs))
    return (qraw, qrot, kv_p, u, gates), (qraw_s, qrot_s, kv_s, u_s, gates_s)


def _mlp_up_cast(xb, xsb, w1, w2, layer, *, tm, tn):
    m, k = xb.shape
    rs = xsb.shape[0]
    dff = w1.shape[2]
    d_out = w2.shape[2]
    ni, nj = m // tm, dff // tn
    slab = dff // (ni * nj)
    assert m % tm == 0 and dff % tn == 0 and dff % (ni * nj) == 0 and slab % SAMPLE_ROWS == 0
    rsub = min(tm, ROW_SUB)

    def body(x_ref, xs_ref, w1_ref, w2_ref, h_ref, hs_ref, w2b_ref, w1b_ref):
        def act(rows_ref, rows):
            a = jnp.maximum(jnp.dot(rows_ref[rows, :], w1b_ref[...], preferred_element_type=F32), 0.0)
            return (a * a).astype(BF16)

        @pl.when(pl.program_id(1) == 0)
        def _():
            w1b_ref[...] = w1_ref[...].astype(BF16)
            hs_ref[...] = act(xs_ref, slice(0, rs))

        w2b_ref[...] = w2_ref[...].astype(BF16)
        for r in range(tm // rsub):
            rows = slice(r * rsub, (r + 1) * rsub)
            h_ref[rows, :] = act(x_ref, rows)

    return pl.pallas_call(
        body, grid=(nj, ni),
        in_specs=[pl.BlockSpec((tm, k), lambda j, i: (i, 0)), pl.BlockSpec((rs, k), lambda j, i: (0, 0)),
                  pl.BlockSpec((None, k, tn), lambda j, i: (layer, 0, j)),
                  pl.BlockSpec((None, slab, d_out), lambda j, i: (layer, j * ni + i, 0))],
        out_specs=[pl.BlockSpec((tm, tn), lambda j, i: (i, j)), pl.BlockSpec((rs, tn), lambda j, i: (0, j)),
                   pl.BlockSpec((slab, d_out), lambda j, i: (j * ni + i, 0))],
        out_shape=[jax.ShapeDtypeStruct((m, dff), BF16), jax.ShapeDtypeStruct((rs, dff), BF16),
                   jax.ShapeDtypeStruct((dff, d_out), BF16)],
        scratch_shapes=[pltpu.VMEM((k, tn), BF16)],
        compiler_params=_cparams("arbitrary", "arbitrary"), name="mlp_up_cast",
    )(xb, xsb, w1, w2)


def kernel(x_prompt, x_sample, cache_cmp_kv, cache_sel_kv, cache_win_kv, state_conv, state_pool, page_table,
           w_in_even, w_out_even, cmp_pe_k, cmp_pe_v, cmp_w_k, cmp_w_v, conv_w, conv_b, conv_ln_g, conv_ln_b,
           w_in_odd, w_out_odd, pool_w, pool_scale, sgu_ln_g, sgu_ln_b, sgu_w, sgu_b,
           mlp_w1, mlp_w2, ln_mix_g, ln_mix_b, ln_ffn_g, ln_ffn_b):
    B, S, D = x_prompt.shape
    Bd, Sd, _ = x_sample.shape
    n_pages = page_table.shape[1]
    past = n_pages * PAGE_SIZE
    n_even, n_pool = cache_cmp_kv.shape[:2]
    wb = cache_win_kv.shape[2]
    assert D == D_MODEL and Sd == 1 and Bd <= SAMPLE_ROWS
    assert S % 1024 == 0 and past % SEL_BLOCK == 0 and S >= WINDOW
    M = B * S
    Ms = SAMPLE_ROWS
    tm_p = 1024

    rope_p = _rope_tables(jnp.arange(S, dtype=jnp.int32))
    rope_s = _rope_tables(jnp.full((Ms,), past, jnp.int32))
    pps = min(32, n_pages)
    assert n_pages % pps == 0
    bsum = _block_sum_matrices(pps)
    cmp_t = cache_cmp_kv.transpose(0, 1, 3, 4, 5, 2).reshape(n_even * n_pool, KVW, PAGE_SIZE)
    sel_t = cache_sel_kv.transpose(0, 1, 3, 4, 5, 2).reshape(n_even * n_pool, 2, KV_HEADS, HEAD_DIM, PAGE_SIZE)
    win_t = cache_win_kv.transpose(0, 1, 3, 4, 5, 2).reshape(n_even * Bd, 2 * KV_HEADS, HEAD_DIM, wb)

    xp = x_prompt.reshape(M, D)
    xs = _pad_rows(x_sample.reshape(Bd, D), Ms)
    xpb, xsb = xp.astype(BF16), xs.astype(BF16)

    outs = {k: [] for k in ("cmp_s", "sel_s", "win_s", "conv_p", "conv_s", "pool_p", "pool_s", "sgu_p", "sgu_s")}
    kv6 = lambda a, lead: a.reshape(lead + (2, KV_HEADS, HEAD_DIM))
    kv_t = None
    win_s_t = None

    for layer in range(DEPTH):
        if layer % 2 == 0:
            e = layer // 2
            wts = _prep_even_weights(w_in_even[e], w_out_even[e], cmp_pe_k[e], cmp_pe_v[e], cmp_w_k[e], cmp_w_v[e])
            prj_p, prj_s = _even_in_proj(xpb, xsb, wts["w_qkv"], wts["w_gg"], rope_p, rope_s, tm=tm_p, bs=(B, S),
                                         stack=(e, n_even, kv_t))
            qraw, qrot, ((kvc, kvc_t), (kvs_b, kvs_t), (kvw_b, kvw_t)), u, gates = prj_p
            kv_t = (kvc_t, kvs_t, kvw_t)
            summ = _compress_rows(kvc, wts["pe"], rows=512, name="prompt_compress")
            (ckv,) = _mm(summ.astype(BF16), wts["big_p"], tm=min(summ.shape[0], 512), tn=KVP,
                         n_off=0, n_cols=KVP, epilogue=_ep_plain, outs=((KVP, F32, KVP),), name="prompt_compress_map")
            o_att = _prompt_attention(qraw, qrot, gates, ckv, kvs_b, kvw_b, batch=B, seq=S, tq=256, tk=256, ngrp=4)
            c = _prompt_conv(u, conv_w[e], conv_b[e], conv_ln_g[e], conv_ln_b[e], batch=B, seq=S, ts=256)
            outs["conv_p"].append(u.reshape(B, S, CONV_CH)[:, S - (CONV_W - 1):])
            qraw_s, qrot_s, (kvc_s, kvs_s, kvw_s), u_s, gates_s = prj_s
            ckv_t = _compress_pages(cmp_t, page_table, wts["pe_t"], wts["big_t"], bsum, e * n_pool, pps=pps,
                                    name="sample_compress")
            q3 = qraw_s.astype(F32)[:Bd].reshape(Bd, N_HEADS, HSLOT)
            o_cmp, idx = _sample_cmp(q3, ckv_t, bd=Bd, past=past, qpos=past, nbl=pps * (PAGE_SIZE // CMP_BLOCK))
            idx = idx[:, :, 0].reshape(Bd, KV_HEADS, TOP_K)
            qr3 = _split_to_nat(qrot_s.astype(F32)[:Bd], N_HEADS).reshape(Bd, N_HEADS, HSLOT)
            q4 = jnp.pad(qr3.reshape(Bd, KV_HEADS, HPG, HSLOT), ((0, 0), (0, 0), (0, Q_PAD_ROWS - HPG), (0, 0)))
            kvs_nat = _split_to_nat(kvs_s, 2 * KV_HEADS)
            kvw_nat = _split_to_nat(kvw_s, 2 * KV_HEADS)
            o_sel = _sample_sel(sel_t, page_table, idx, e * n_pool, q4, kvs_nat, bd=Bd, past=past)
            o_sel = o_sel[:, :, :HPG].reshape(Bd, N_HEADS, HSLOT)
            g3 = gates_s[:Bd, :3 * N_HEADS].reshape(Bd, KV_HEADS, 3, HPG)
            g3 = g3.transpose(0, 1, 3, 2).reshape(Bd, N_HEADS, 3)
            g3 = jnp.pad(g3, ((0, 0), (0, 0), (0, LANES - 3)))
            o_s = _sample_win(qr3, win_t, e, kvw_nat, o_cmp, o_sel, g3, bd=Bd, past=past, qpos=past)
            c_s = _sample_conv(state_conv, e, u_s, conv_w[e], conv_b[e], conv_ln_g[e], conv_ln_b[e], bd=Bd)
            o_sb = _pad_rows(_nat_to_split(o_s.reshape(Bd, QW), N_HEADS), Ms).astype(BF16)
            c_sb = _pad_rows(c_s, Ms).astype(BF16)
            xp, xpb, xs, xsb = _proj_ln([o_att, c], [wts["wo_att"], wts["wo_conv"]], xp, ln_mix_g[layer],
                                        ln_mix_b[layer], tm=512, name="even_out_ln", side=([o_sb, c_sb], xs))
            kvc_c = _unpad_heads(kvc_s[:Bd], 2 * KV_HEADS)
            kvs_c = _unpad_heads(kvs_s[:Bd], 2 * KV_HEADS)
            kvw_c = _unpad_heads(kvw_s[:Bd], 2 * KV_HEADS)
            outs["cmp_s"].append(kv6(kvc_c, (Bd, 1)))
            outs["sel_s"].append(kv6(kvs_c, (Bd, 1)))
            if wb == WINDOW:
                win_s_t = _window_shift(win_t.reshape(n_even * Bd, KVW, wb), kvw_c.T, e, n_even, win_s_t, bd=Bd)
            else:
                wkv = jnp.concatenate([cache_win_kv[e], kv6(kvw_c, (Bd, 1))], axis=1)
                outs["win_s"].append(wkv[:, wkv.shape[1] - min(WINDOW, wkv.shape[1]):])
            outs["conv_s"].append(jnp.concatenate([state_conv[e], u_s[:Bd, None, :]], axis=1)[:, 1:])
        else:
            o = layer // 2
            w_in = w_in_odd[o]
            wb_in = w_in.astype(BF16)
            w_pu, w_v = wb_in[:, :POOL_CH + SGU_CH], wb_in[:, POOL_CH + SGU_CH:]
            w_out_p = w_out_odd[o].astype(BF16)
            lg, lb = sgu_ln_g[o].reshape(1, SGU_CH), sgu_ln_b[o].reshape(1, SGU_CH)

            gl_ex = ((lg, (1, SGU_CH), lambda i, j: (0, j)), (lb, (1, SGU_CH), lambda i, j: (0, j)))
            o_v, o_pu = ((SGU_CH, F32, SGU_CH),), ((POOL_CH, F32, POOL_CH), (SGU_CH, F32, SGU_CH))
            vn, vn_s = _mm(xpb, w_v, tm=tm_p, tn=SGU_CH, n_off=0, n_cols=SGU_CH, epilogue=_ep_gelu_gln,
                           extras=gl_ex, outs=o_v, name="odd_in_v", side=(xsb, _ep_gelu_gln, gl_ex, o_v))
            pin, uu, pin_s, uu_s = _mm(xpb, w_pu, tm=tm_p, tn=POOL_CH + SGU_CH, n_off=0, n_cols=POOL_CH + SGU_CH,
                                       epilogue=_ep_pool_gelu, outs=o_pu, name="odd_in_pool_u",
                                       side=(xsb, _ep_pool_gelu, (), o_pu))
            cat = _prompt_odd_mix(pin, uu, vn, pool_w[o], pool_scale[o], sgu_w[o], sgu_b[o], batch=B, seq=S, ts=512)
            outs["pool_p"].append(pin.reshape(B, S, POOL_CH)[:, S - POOL_STATE:])
            outs["sgu_p"].append(vn.reshape(B, S, SGU_CH)[:, ((S - 1) // CHUNK) * CHUNK:])
            cat_s = _sample_odd_mix(state_pool, o, pin_s, uu_s, vn_s, pool_w[o], pool_scale[o], sgu_w[o], sgu_b[o],
                                    bd=Bd, start_pos=past)
            xp, xpb, xs, xsb = _proj_ln([cat], [w_out_p], xp, ln_mix_g[layer], ln_mix_b[layer], tm=512,
                                        name="odd_out_ln", side=([_pad_rows(cat_s, Ms).astype(BF16)], xs))
            outs["pool_s"].append(jnp.concatenate([state_pool[o], pin_s[:Bd, None, :]], axis=1)[:, 1:])
            outs["sgu_s"].append(vn_s[:Bd, None, :])
        h, h_s, w2b = _mlp_up_cast(xpb, xsb, mlp_w1, mlp_w2, layer, tm=2 * tm_p, tn=1024)
        xp, xpb, xs, xsb = _matmul_res_ln([h], [w2b], xp, ln_ffn_g[layer], ln_ffn_b[layer], tm=512, tn=512,
                                          name="mlp_down_ln", side=([h_s], xs))

    st = lambda k: jnp.stack(outs[k])
    rows_last = lambda a, nb=B: a.reshape(n_even, nb, 2, KV_HEADS, HEAD_DIM, a.shape[-1]).transpose(0, 1, 5, 2, 3, 4)
    kvc_t, kvs_t, kvw_t = kv_t
    return (xp.reshape(B, S, D), xs[:Bd].reshape(Bd, Sd, D),
            rows_last(kvc_t), st("cmp_s"), rows_last(kvs_t), st("sel_s"),
            rows_last(kvw_t[:, :, :, S - WINDOW:]),
            st("win_s") if win_s_t is None else rows_last(win_s_t, Bd),
            st("conv_p"), st("conv_s"), st("pool_p"), st("pool_s"), st("sgu_p"), st("sgu_s"))
```

```python
import functools

import jax
import jax.numpy as jnp
from jax import lax
from jax.experimental import pallas as pl
from jax.experimental.pallas import tpu as pltpu

F32 = jnp.float32
BF16 = jnp.bfloat16

D_MODEL = 2048
DEPTH = 4
PAGE_SIZE = 128
N_HEADS = 16
HEAD_DIM = 96
KV_HEADS = 4
HPG = N_HEADS // KV_HEADS
ATT_W = N_HEADS * HEAD_DIM
KVW = 2 * KV_HEADS * HEAD_DIM
CMP_BLOCK = 32
SEL_BLOCK = 64
TOP_K = 16
WINDOW = 512
ROPE_THETA = 10000.0
SCALE = HEAD_DIM ** -0.5
LOG2E = 1.4426950408889634
FORCE = 1e9
NEG = -1e30
CONV_CH = D_MODEL // 4
CONV_W = 31
POOL_CH = D_MODEL // 4
POOL_WINDOWS = (2, 4, 8, 16)
POOL_GROUPS = len(POOL_WINDOWS)
POOL_GC = POOL_CH // POOL_GROUPS
POOL_STATE = max(POOL_WINDOWS) - 1
SGU_CH = D_MODEL - POOL_CH
SGU_GROUPS = 4
SGU_GC = SGU_CH // SGU_GROUPS
CHUNK = 128
ALPHA = (2 * DEPTH) ** 0.25
LN_EPS = 1e-5

LANES = 128
SUBLANES = 8
HALF = HEAD_DIM // 2
HSLOT = LANES
HALF_OFF = LANES // 2
QW = N_HEADS * HSLOT
KVP = 2 * KV_HEADS * HSLOT
GATE_W = LANES
SAMPLE_ROWS = 16
ROW_SUB = 256
LN_ROW_SUB = 128

TM_PROJ = 1024
TM_MLP_UP, TN_MLP_UP = 2048, 1024
TM_RES_LN = 512
TN_MLP_DOWN = 512
ATT_TQ = ATT_TK = 256
CONV_TS = 256
POOL_TS = 512
CMP_ROWS = 512
PAGES_PER_STEP = 32
VMEM_LIMIT = 52 * 1024 * 1024

E_Q, E_KVC, E_KVS, E_KVW = 0, QW, QW + KVP, QW + 2 * KVP
GG_W = 2 * CONV_CH + GATE_W


def _cparams(*sem):
    return pltpu.CompilerParams(dimension_semantics=sem, vmem_limit_bytes=VMEM_LIMIT)


def _pad_head(x):
    halves = x.reshape(x.shape[:-1] + (2, HALF))
    halves = jnp.pad(halves, [(0, 0)] * (halves.ndim - 1) + [(0, HALF_OFF - HALF)])
    return halves.reshape(x.shape[:-1] + (HSLOT,))


def _pad_nat(x):
    return jnp.concatenate([x, jnp.zeros(x.shape[:-1] + (HSLOT - HEAD_DIM,), x.dtype)], axis=-1)


def _unpad_heads(x, nheads):
    xh = x.reshape(x.shape[:-1] + (nheads, HSLOT))
    y = jnp.concatenate([xh[..., :HALF], xh[..., HALF_OFF:HALF_OFF + HALF]], axis=-1)
    return y.reshape(x.shape[:-1] + (nheads * HEAD_DIM,))


def _rope_tables(pos):
    inv = jnp.power(ROPE_THETA, -jnp.arange(HALF, dtype=F32) / HALF)
    ang = pos.astype(F32)[:, None] * inv[None, :]
    cos, sin = jnp.cos(ang), jnp.sin(ang)
    z = jnp.zeros((pos.shape[0], HALF_OFF - HALF), F32)
    return (jnp.concatenate([cos, z, cos, z], axis=1),
            jnp.concatenate([-sin, z, sin, z], axis=1))


def _block_diag2(a, b):
    za = jnp.zeros((a.shape[0], b.shape[1]), a.dtype)
    zb = jnp.zeros((b.shape[0], a.shape[1]), a.dtype)
    return jnp.concatenate([jnp.concatenate([a, za], axis=1), jnp.concatenate([zb, b], axis=1)], axis=0)


def _prep_even_weights(w_in, w_out, pe_k, pe_v, w_ck, w_cv):
    d = w_in.shape[0]
    wb = w_in.astype(BF16)
    g0 = ATT_W + 3 * KVW
    n_heads_all = g0 // HEAD_DIM
    qkv = _pad_head(wb[:, :g0].reshape(d, n_heads_all, HEAD_DIM)).reshape(d, n_heads_all * HSLOT)
    gates = wb[:, g0:g0 + 3 * N_HEADS].reshape(d, 3, KV_HEADS, HPG).transpose(0, 2, 1, 3)
    gates = jnp.pad(gates.reshape(d, 3 * N_HEADS), ((0, 0), (0, GATE_W - 3 * N_HEADS)))
    w_gg = jnp.concatenate([wb[:, g0 + 3 * N_HEADS:], gates], axis=1)
    wo = w_out.astype(BF16)
    wo_att = jnp.pad(wo[:ATT_W].reshape(2 * N_HEADS, HALF, d), ((0, 0), (0, HALF_OFF - HALF), (0, 0)))
    wo_att = wo_att.reshape(QW, d)
    wo_conv = wo[ATT_W:]
    eye = jnp.eye(KV_HEADS, dtype=F32)
    pe = jnp.concatenate([jnp.tile(_pad_head(pe_k), (1, KV_HEADS)), jnp.tile(_pad_head(pe_v), (1, KV_HEADS))], axis=1)
    wk_full = _pad_head(_pad_head(w_ck).T).T
    wv_full = _pad_head(_pad_head(w_cv).T).T
    big_p = _block_diag2(jnp.kron(eye, wk_full), jnp.kron(eye, wv_full))
    pe_t = jnp.concatenate([jnp.tile(jnp.tile(pe_k.T, (1, PAGE_SIZE // CMP_BLOCK)), (KV_HEADS, 1)),
                            jnp.tile(jnp.tile(pe_v.T, (1, PAGE_SIZE // CMP_BLOCK)), (KV_HEADS, 1))], axis=0)
    big_t = _block_diag2(jnp.kron(eye, _pad_head(w_ck).T), jnp.kron(eye, _pad_nat(w_cv).T))
    return dict(w_qkv=qkv, w_gg=w_gg, wo_att=wo_att, wo_conv=wo_conv, pe=pe, pe_t=pe_t,
                big_p=big_p.astype(BF16), big_t=big_t.astype(BF16))


def _block_sum_matrices(pps):
    bpp = PAGE_SIZE // CMP_BLOCK
    nbl = bpp * pps
    p = jnp.arange(pps)[:, None, None]
    i = (jnp.arange(PAGE_SIZE) // CMP_BLOCK)[None, :, None]
    c = jnp.arange(nbl)[None, None, :]
    col = (i % 2) * (nbl // 2) + (bpp // 2) * p + i // 2
    return jnp.where(c == col, 1.0 / CMP_BLOCK, 0.0).astype(BF16)


def _ln_rows(y, g, b):
    mu = jnp.mean(y, axis=-1, keepdims=True)
    yc = y - mu
    var = jnp.mean(yc * yc, axis=-1, keepdims=True)
    return yc * lax.rsqrt(var + LN_EPS) * g + b


def _rope_slot(x, cos, sin):
    return x * cos + pltpu.roll(x, HALF_OFF, axis=1) * sin


_NT = (((1,), (1,)), ((), ()))


def _pad_rows_to(x, rows):
    return jnp.concatenate([x, jnp.zeros((rows - x.shape[0],) + x.shape[1:], x.dtype)], axis=0)


def _mm(x, w, *, tm, tn, n_off, n_cols, epilogue, extras=(), outs, name, side=None, alias=None):
    m, k = x.shape
    assert m % tm == 0 and n_cols % tn == 0 and n_off % tn == 0
    joff = n_off // tn
    ji = lambda im: (lambda j, i: im(i, j))
    in_specs = [pl.BlockSpec((tm, k), lambda j, i: (i, 0)),
                pl.BlockSpec((k, tn), lambda j, i: (0, joff + j))]
    in_specs += [pl.BlockSpec(bs, ji(im)) for _, bs, im in extras]
    operands = [x, w] + [a for a, _, _ in extras]
    out_shape, out_specs = [], []
    for o in outs:
        if len(o) == 3:
            out_shape.append(jax.ShapeDtypeStruct((m, o[0]), o[1]))
            out_specs.append(pl.BlockSpec((tm, o[2]), lambda j, i: (i, j)))
        else:
            out_shape.append(jax.ShapeDtypeStruct(o[0], o[1]))
            out_specs.append(pl.BlockSpec(o[2], ji(o[3])))
    ne, no = len(extras), len(outs)
    nse = 0
    if side is not None:
        xs, s_epilogue, s_extras, s_outs = side
        rs = xs.shape[0]
        nse = len(s_extras)
        in_specs += [pl.BlockSpec((rs, k), lambda j, i: (0, 0))]
        in_specs += [pl.BlockSpec(bs, ji(im)) for _, bs, im in s_extras]
        operands += [xs] + [a for a, _, _ in s_extras]
        for cols, dt, bc in s_outs:
            out_shape.append(jax.ShapeDtypeStruct((rs, cols), dt))
            out_specs.append(pl.BlockSpec((rs, bc), lambda j, i: (0, j)))
    rsub = min(tm, ROW_SUB)
    n_in = len(operands)
    io_alias = {}
    if alias is not None:
        io_alias = {n_in: alias[1]}
        in_specs.append(pl.BlockSpec(memory_space=pl.ANY))
        operands.append(alias[0])

    def body(*refs):
        x_ref, w_ref = refs[:2]
        ex = refs[2:2 + ne]
        o_refs = refs[len(operands):len(operands) + no]
        for r in range(tm // rsub):
            rows = slice(r * rsub, (r + 1) * rsub)
            acc = jnp.dot(x_ref[rows, :], w_ref[...], preferred_element_type=F32)
            epilogue(acc, ex, o_refs, rows)
        if side is not None:
            @pl.when(pl.program_id(1) == 0)
            def _():
                acc = jnp.dot(refs[2 + ne][...], w_ref[...], preferred_element_type=F32)
                s_epilogue(acc, refs[3 + ne:n_in], refs[len(operands) + no:], slice(0, rs))

    return pl.pallas_call(
        body, grid=(n_cols // tn, m // tm), in_specs=in_specs, out_specs=out_specs, out_shape=out_shape,
        input_output_aliases=io_alias, compiler_params=_cparams("arbitrary", "arbitrary"), name=name,
    )(*operands)


def _ep_q(acc, ex, outs, rows):
    cos, sin = ex[0][rows, :], ex[1][rows, :]
    outs[0][rows, :] = acc.astype(BF16)
    for j in range(acc.shape[1] // HSLOT):
        sl = slice(j * HSLOT, (j + 1) * HSLOT)
        outs[1][rows, sl] = _rope_slot(acc[:, sl], cos, sin).astype(BF16)


def _ep_kv(acc, ex, outs, rows, *, rope, want):
    o = dict(zip(want, outs))
    if rope:
        cos, sin = ex[0][rows, :], ex[1][rows, :]
    for j in range(2 * KV_HEADS):
        sl = slice(j * HSLOT, (j + 1) * HSLOT)
        x = acc[:, sl]
        if rope and j < KV_HEADS:
            x = _rope_slot(x, cos, sin)
        if "f32" in o:
            o["f32"][rows, sl] = x
        if "bf16" in o:
            o["bf16"][rows, sl] = x.astype(BF16)
        if "t" in o:
            xt = x.T
            o["t"][j * HEAD_DIM:j * HEAD_DIM + HALF, rows] = xt[0:HALF]
            o["t"][j * HEAD_DIM + HALF:(j + 1) * HEAD_DIM, rows] = xt[HALF_OFF:HALF_OFF + HALF]


def _ep_glu_gates(acc, ex, outs, rows):
    outs[0][rows, :] = acc[:, :CONV_CH] * jax.nn.sigmoid(acc[:, CONV_CH:2 * CONV_CH])
    outs[1][rows, :] = jax.nn.sigmoid(acc[:, 2 * CONV_CH:])


def _ep_plain(acc, ex, outs, rows):
    outs[0][rows, :] = acc.astype(outs[0].dtype)


def _ep_pool_gelu(acc, ex, outs, rows):
    outs[0][rows, :] = acc[:, :POOL_CH]
    outs[1][rows, :] = jax.nn.gelu(acc[:, POOL_CH:])


def _ep_gelu_gln(acc, ex, outs, rows):
    g, b = ex[0][...], ex[1][...]
    v = jax.nn.gelu(acc)
    for j in range(acc.shape[1] // SGU_GC):
        sl = slice(j * SGU_GC, (j + 1) * SGU_GC)
        outs[0][rows, sl] = _ln_rows(v[:, sl], g[:, sl], b[:, sl])


def _proj_ln(a_list, w_list, resid, g, b, *, tm, name, side):
    m, n = resid.shape
    as_list, resid_s = side
    rs = resid_s.shape[0]
    npair = len(a_list)
    in_specs = []
    for a in a_list:
        in_specs.append(pl.BlockSpec((tm, a.shape[1]), lambda i: (i, 0)))
    for w in w_list:
        in_specs.append(pl.BlockSpec(w.shape, lambda i: (0, 0), pipeline_mode=pl.Buffered(1)))
    in_specs += [pl.BlockSpec((tm, n), lambda i: (i, 0)),
                 pl.BlockSpec((1, n), lambda i: (0, 0)), pl.BlockSpec((1, n), lambda i: (0, 0))]
    for a in as_list:
        in_specs.append(pl.BlockSpec((rs, a.shape[1]), lambda i: (0, 0)))
    in_specs.append(pl.BlockSpec((rs, n), lambda i: (0, 0)))

    def body(*refs):
        a_refs, w_refs = refs[:npair], refs[npair:2 * npair]
        r_ref, g_ref, b_ref = refs[2 * npair:2 * npair + 3]
        as_refs = refs[2 * npair + 3:3 * npair + 3]
        rs_ref, o_ref, ob_ref, os_ref, osb_ref = refs[3 * npair + 3:]

        def rows_out(a_rs, res, rows):
            acc = ALPHA * res[rows, :]
            for a_ref, w_ref in zip(a_rs, w_refs):
                acc = acc + jnp.dot(a_ref[rows, :], w_ref[...], preferred_element_type=F32)
            return _ln_rows(acc, g_ref[...], b_ref[...])

        rsub = min(tm, LN_ROW_SUB)
        for r in range(tm // rsub):
            rows = slice(r * rsub, (r + 1) * rsub)
            y = rows_out(a_refs, r_ref, rows)
            o_ref[rows, :] = y
            ob_ref[rows, :] = y.astype(BF16)

        @pl.when(pl.program_id(0) == 0)
        def _():
            ys = rows_out(as_refs, rs_ref, slice(0, rs))
            os_ref[...] = ys
            osb_ref[...] = ys.astype(BF16)

    row = pl.BlockSpec((tm, n), lambda i: (i, 0))
    srow = pl.BlockSpec((rs, n), lambda i: (0, 0))
    return pl.pallas_call(
        body, grid=(m // tm,), in_specs=in_specs, out_specs=[row, row, srow, srow],
        out_shape=[jax.ShapeDtypeStruct((m, n), F32), jax.ShapeDtypeStruct((m, n), BF16),
                   jax.ShapeDtypeStruct((rs, n), F32), jax.ShapeDtypeStruct((rs, n), BF16)],
        compiler_params=_cparams("arbitrary"), name=name,
    )(*a_list, *w_list, resid, g.reshape(1, n), b.reshape(1, n), *as_list, resid_s)


def _matmul_res_ln(a_list, w_list, resid, g, b, *, tm, tn, name, side):
    m, n = resid.shape
    nj = n // tn
    as_list, resid_s = side
    rs = resid_s.shape[0]
    npair = len(a_list)
    rsub, rsub_ln = min(tm, ROW_SUB), min(tm, LN_ROW_SUB)

    def body(*refs):
        h_ref, w_refs = refs[:npair], refs[npair:2 * npair]
        r_ref, g_ref, b_ref = refs[2 * npair:2 * npair + 3]
        hs_ref = refs[2 * npair + 3:3 * npair + 3]
        rs_ref, o_ref, ob_ref, os_ref, osb_ref = refs[3 * npair + 3:]
        j = pl.program_id(1)
        first = pl.program_id(0) == 0
        srows = slice(0, rs)

        def tile(hr, rr, rows):
            acc = ALPHA * rr[rows, :]
            for a_ref, w_ref in zip(hr, w_refs):
                acc = acc + jnp.dot(a_ref[rows, :], w_ref[...], preferred_element_type=F32)
            return acc

        def finish(o, ob, y, rows):
            y = _ln_rows(y, g_ref[...], b_ref[...])
            o[rows, :] = y
            ob[rows, :] = y.astype(BF16)

        for jj in range(nj - 1):
            @pl.when(j == jj)
            def _(jj=jj):
                cols = slice(jj * tn, (jj + 1) * tn)
                for r in range(tm // rsub):
                    rows = slice(r * rsub, (r + 1) * rsub)
                    o_ref[rows, cols] = tile(h_ref, r_ref, rows)

                @pl.when(first)
                def _():
                    os_ref[:, cols] = tile(hs_ref, rs_ref, srows)

        @pl.when(j == nj - 1)
        def _():
            done = slice(0, (nj - 1) * tn)
            for r in range(tm // rsub_ln):
                rows = slice(r * rsub_ln, (r + 1) * rsub_ln)
                finish(o_ref, ob_ref, jnp.concatenate([o_ref[rows, done], tile(h_ref, r_ref, rows)], axis=1), rows)

            @pl.when(first)
            def _():
                finish(os_ref, osb_ref, jnp.concatenate([os_ref[:, done], tile(hs_ref, rs_ref, srows)], axis=1), srows)

    assert nj > 1
    row = pl.BlockSpec((tm, n), lambda i, j: (i, 0))
    srow = pl.BlockSpec((rs, n), lambda i, j: (0, 0))
    cst = lambda i, j: (0, 0)
    in_specs = [pl.BlockSpec((tm, a.shape[1]), lambda i, j: (i, 0)) for a in a_list]
    in_specs += [pl.BlockSpec((w.shape[0], tn), lambda i, j: (0, j)) for w in w_list]
    in_specs += [pl.BlockSpec((tm, tn), lambda i, j: (i, j)), pl.BlockSpec((1, n), cst), pl.BlockSpec((1, n), cst)]
    in_specs += [pl.BlockSpec((rs, a.shape[1]), cst) for a in as_list]
    in_specs += [pl.BlockSpec((rs, tn), lambda i, j: (0, j))]
    return pl.pallas_call(
        body, grid=(m // tm, nj), in_specs=in_specs, out_specs=[row, row, srow, srow],
        out_shape=[jax.ShapeDtypeStruct((m, n), F32), jax.ShapeDtypeStruct((m, n), BF16),
                   jax.ShapeDtypeStruct((rs, n), F32), jax.ShapeDtypeStruct((rs, n), BF16)],
        compiler_params=_cparams("arbitrary", "arbitrary"), name=name,
    )(*a_list, *w_list, resid, g.reshape(1, n), b.reshape(1, n), *as_list, resid_s)


def _compress_rows(kvc, pe, *, rows, name):
    m, c = kvc.shape
    nb = rows // CMP_BLOCK

    def body(x_ref, pe_ref, o_ref):
        x = x_ref[...].reshape(nb, CMP_BLOCK, c) * pe_ref[...][None]
        o_ref[...] = jnp.sum(x, axis=1) * (1.0 / CMP_BLOCK)

    return pl.pallas_call(
        body, grid=(m // rows,),
        in_specs=[pl.BlockSpec((rows, c), lambda i: (i, 0)), pl.BlockSpec((CMP_BLOCK, c), lambda i: (0, 0))],
        out_specs=pl.BlockSpec((nb, c), lambda i: (i, 0)),
        out_shape=jax.ShapeDtypeStruct((m // CMP_BLOCK, c), F32),
        compiler_params=_cparams("parallel"), name=name,
    )(kvc, pe)


def _compress_pages(pages_t, page_table, pe_t, big_t, bsum, layer_base, *, pps, name):
    bd, n_pages = page_table.shape
    bpp = PAGE_SIZE // CMP_BLOCK
    nbl = bpp * pps
    nsteps = n_pages // pps

    def body(pt_ref, *refs):
        page_refs = refs[:pps]
        pe_ref, big_ref, bsum_ref, o_ref = refs[pps:]
        acc = jnp.zeros((KVW, nbl), F32)
        for p in range(pps):
            x = page_refs[p][...] * pe_ref[...]
            acc = acc + jnp.dot(x.astype(BF16), bsum_ref[p], preferred_element_type=F32)
        o_ref[...] = jnp.dot(big_ref[...], acc.astype(BF16), preferred_element_type=F32)

    def page_map(p):
        return lambda b, j, pt: (layer_base + pt[b, j * pps + p], 0, 0)

    cst2 = lambda b, j, pt: (0, 0)
    grid_spec = pltpu.PrefetchScalarGridSpec(
        num_scalar_prefetch=1, grid=(bd, nsteps),
        in_specs=[pl.BlockSpec((None, KVW, PAGE_SIZE), page_map(p)) for p in range(pps)]
        + [pl.BlockSpec((KVW, PAGE_SIZE), cst2), pl.BlockSpec((KVP, KVW), cst2),
           pl.BlockSpec((pps, PAGE_SIZE, nbl), lambda b, j, pt: (0, 0, 0))],
        out_specs=pl.BlockSpec((None, KVP, nbl), lambda b, j, pt: (b, 0, j)))
    return pl.pallas_call(
        body, grid_spec=grid_spec,
        out_shape=jax.ShapeDtypeStruct((bd, KVP, n_pages * bpp), F32),
        compiler_params=_cparams("parallel", "arbitrary"), name=name,
    )(page_table, *([pages_t] * pps), pe_t, big_t, bsum)


def _flash_step_t(q, k, vt, valid, m_ref, l_ref, acc_ref):
    bias = jnp.where(valid, 0.0, NEG)
    s = jnp.dot(k, q, preferred_element_type=F32) + jnp.concatenate([bias] * HPG, axis=1)
    m_prev = m_ref[...]
    m_new = jnp.maximum(m_prev, jnp.max(s, axis=0, keepdims=True))
    alpha = jnp.exp2(m_prev - m_new)
    p = jnp.exp2(s - m_new)
    l_ref[...] = alpha * l_ref[...] + jnp.sum(p, axis=0, keepdims=True)
    acc_ref[...] = alpha * acc_ref[...] + jnp.dot(vt, p.astype(BF16), preferred_element_type=F32)
    m_ref[...] = m_new


def _attn_body(*refs, tq, tk, seq, k_top, ngrp):
    n_in = 9
    ins = [refs[g * n_in:(g + 1) * n_in] for g in range(ngrp)]
    o_ref = refs[ngrp * n_in]
    m_ref, l_ref, acc_ref, sel_ref, vts_ref, vtw_ref = refs[ngrp * n_in + 1:]
    i = pl.program_id(2)
    cols = HPG * tq
    nsb = seq // SEL_BLOCK
    nt = seq // tk
    bpt = tk // SEL_BLOCK
    q0 = i * tq
    lane_q = lax.broadcasted_iota(jnp.int32, (1, cols), 1)
    qpos = q0 + (lane_q & (tq - 1))
    qp1 = q0 + lax.broadcasted_iota(jnp.int32, (1, tq), 1)

    @pl.when(i == 0)
    def _():
        def tr(t, c):
            for g in range(ngrp):
                vs_ref, vw_ref = ins[g][6], ins[g][8]
                for h in range(tk // LANES):
                    k0 = pl.multiple_of(t * tk + h * LANES, LANES)
                    hs = slice(h * LANES, (h + 1) * LANES)
                    vts_ref[g, t, :, hs] = vs_ref[pl.ds(k0, LANES), :].astype(F32).T.astype(BF16)
                    vtw_ref[g, t, :, hs] = vw_ref[pl.ds(k0, LANES), :].astype(F32).T.astype(BF16)
            return c
        lax.fori_loop(0, nt, tr, 0)

    def heads_t(ref, scale=1.0):
        parts = [ref[:, j * HSLOT:(j + 1) * HSLOT].astype(F32).T * scale for j in range(HPG)]
        return jnp.concatenate(parts, axis=1).astype(BF16)

    r = lax.broadcasted_iota(jnp.int32, (2 * nsb, 1), 0)
    n_of = jnp.where(r < nsb, 2 * r, 2 * (r - nsb) + 1)
    mk = ((n_of + 1) * CMP_BLOCK - 1) <= qpos
    sb = lax.broadcasted_iota(jnp.int32, (nsb, 1), 0)
    vis = (sb * SEL_BLOCK) <= qp1
    cur = sb == (qp1 >> (SEL_BLOCK.bit_length() - 1))

    def compressed_branch(g):
        qraw_ref, ck_ref, cv_ref = ins[g][0], ins[g][3], ins[g][4]
        qr = heads_t(qraw_ref)
        ck = jnp.concatenate([ck_ref[pl.ds(0, nsb, stride=2), :], ck_ref[pl.ds(1, nsb, stride=2), :]], axis=0)
        s = jnp.dot(ck.astype(BF16), qr, preferred_element_type=F32) * SCALE
        s = jnp.where(mk, s, NEG)
        mx = jnp.max(s, axis=0, keepdims=True)
        p = jnp.where(mk, jnp.exp(s - mx), 0.0)
        pn = p * (1.0 / jnp.maximum(jnp.sum(p, axis=0, keepdims=True), 1e-30))
        cv = jnp.concatenate([cv_ref[pl.ds(0, nsb, stride=2), :], cv_ref[pl.ds(1, nsb, stride=2), :]], axis=0)
        cvt = _pad_rows_to(cv, LANES).T.astype(BF16)
        o_cmp = jnp.dot(cvt, _pad_rows_to(pn, LANES).astype(BF16), preferred_element_type=F32)
        pp = pn[0:nsb] + pn[nsb:2 * nsb]
        imp = pp[:, 0:tq]
        for j in range(1, HPG):
            imp = imp + pp[:, j * tq:(j + 1) * tq]
        imp = jnp.where(cur, FORCE, jnp.where(vis, imp, -FORCE))
        cnt = jnp.zeros((nsb, tq), F32)
        for j in range(nsb):
            rowj = imp[j:j + 1, :]
            beats = (rowj > imp) | ((rowj == imp) & (j < sb))
            cnt = cnt + jnp.where(beats, 1.0, 0.0)
        sel = jnp.where(cnt < k_top, 1.0, 0.0)
        for t in range(nt):
            sel_ref[g, t, 0:bpt, :] = sel[bpt * t:bpt * (t + 1), :]
        return o_cmp

    o_cmp = [compressed_branch(g) for g in range(ngrp)]
    qt = [heads_t(ins[g][1], SCALE * LOG2E) for g in range(ngrp)]
    rowk = lax.broadcasted_iota(jnp.int32, (tk, 1), 0)

    def reset():
        m_ref[...] = jnp.full((ngrp, 1, cols), NEG, F32)
        l_ref[...] = jnp.zeros((ngrp, 1, cols), F32)
        acc_ref[...] = jnp.zeros((ngrp, HSLOT, cols), F32)

    def result(g):
        return acc_ref[g] * (1.0 / jnp.maximum(l_ref[g], 1e-30))

    reset()

    def sel_step(t, c):
        k0 = pl.multiple_of(t * tk, tk)
        kp = k0 + rowk
        for g in range(ngrp):
            sm = sel_ref[g, t, bpt - 1:bpt, :]
            for j in range(bpt - 2, -1, -1):
                sm = jnp.where(rowk < (j + 1) * SEL_BLOCK, sel_ref[g, t, j:j + 1, :], sm)
            valid = (kp <= qp1) & (sm > 0.5)
            _flash_step_t(qt[g], ins[g][5][pl.ds(k0, tk), :], vts_ref[g, t], valid,
                          m_ref.at[g], l_ref.at[g], acc_ref.at[g])
        return c

    t_end = lax.div(q0 + tq + tk - 1, tk)
    lax.fori_loop(0, t_end, sel_step, 0)
    o_sel = [result(g) for g in range(ngrp)]

    reset()

    def win_step(t, c):
        k0 = pl.multiple_of(t * tk, tk)
        kp = k0 + rowk
        valid = (kp <= qp1) & (kp >= qp1 - WINDOW)
        for g in range(ngrp):
            _flash_step_t(qt[g], ins[g][7][pl.ds(k0, tk), :], vtw_ref[g, t], valid,
                          m_ref.at[g], l_ref.at[g], acc_ref.at[g])
        return c

    lax.fori_loop(lax.div(jnp.maximum(q0 - WINDOW, 0), tk), t_end, win_step, 0)

    gw = HPG * HSLOT
    for g in range(ngrp):
        gt = ins[g][2][...].T
        o = jnp.zeros((HSLOT, cols), F32)
        for br, o_br in enumerate((o_cmp[g], o_sel[g], result(g))):
            r0 = (g * 3 + br) * HPG
            grow = jnp.concatenate([gt[r0 + j:r0 + j + 1, :] for j in range(HPG)], axis=1)
            o = o + grow * o_br
        for j in range(HPG):
            o_ref[:, g * gw + j * HSLOT:g * gw + (j + 1) * HSLOT] = o[:, j * tq:(j + 1) * tq].T.astype(BF16)


def _prompt_attention(qraw, qrot, gates, ckv, kvs_b, kvw_b, *, batch, seq, tq, tk, ngrp):
    nq = seq // tq
    ncb = seq // CMP_BLOCK
    nt = seq // tk
    cols = HPG * tq
    k_top = min(TOP_K, seq // SEL_BLOCK)
    assert tq & (tq - 1) == 0 and seq % tk == 0 and tk % LANES == 0 and tk // SEL_BLOCK <= SUBLANES
    assert ngrp == KV_HEADS
    body = functools.partial(_attn_body, tq=tq, tk=tk, seq=seq, k_top=k_top, ngrp=ngrp)
    gw = HPG * HSLOT
    in_specs, operands = [], []
    for g in range(ngrp):
        qmap = functools.partial(lambda b, p, i, g: (b * nq + i, p * ngrp + g), g=g)
        kmap = functools.partial(lambda b, p, i, g: (b, p * ngrp + g), g=g)
        vmap_ = functools.partial(lambda b, p, i, g: (b, KV_HEADS + p * ngrp + g), g=g)
        in_specs += [pl.BlockSpec((tq, gw), qmap), pl.BlockSpec((tq, gw), qmap),
                     pl.BlockSpec((tq, GATE_W), lambda b, p, i: (b * nq + i, 0)),
                     pl.BlockSpec((ncb, HSLOT), kmap), pl.BlockSpec((ncb, HSLOT), vmap_),
                     pl.BlockSpec((seq, HSLOT), kmap), pl.BlockSpec((seq, HSLOT), vmap_),
                     pl.BlockSpec((seq, HSLOT), kmap), pl.BlockSpec((seq, HSLOT), vmap_)]
        operands += [qraw, qrot, gates, ckv, ckv, kvs_b, kvs_b, kvw_b, kvw_b]
    return pl.pallas_call(
        body, grid=(batch, KV_HEADS // ngrp, nq), in_specs=in_specs,
        out_specs=pl.BlockSpec((tq, ngrp * gw), lambda b, p, i: (b * nq + i, p)),
        out_shape=jax.ShapeDtypeStruct((batch * seq, QW), BF16),
        scratch_shapes=[pltpu.VMEM((ngrp, 1, cols), F32), pltpu.VMEM((ngrp, 1, cols), F32),
                        pltpu.VMEM((ngrp, HSLOT, cols), F32), pltpu.VMEM((ngrp, nt, SUBLANES, tq), F32),
                        pltpu.VMEM((ngrp, nt, HSLOT, tk), BF16), pltpu.VMEM((ngrp, nt, HSLOT, tk), BF16)],
        compiler_params=_cparams("parallel", "parallel", "arbitrary"), name="prompt_attention",
    )(*operands)


CONV_HALO = 32


def _conv_body(cur_ref, prev_ref, w_ref, b_ref, g_ref, bn_ref, o_ref, ext_ref, *, ts):
    c = pl.program_id(1)
    span = CONV_HALO + ts - SUBLANES
    ext_ref[0, 0:CONV_HALO, :] = jnp.where(c > 0, prev_ref[...], 0.0)
    ext_ref[0, CONV_HALO:CONV_HALO + ts, :] = cur_ref[...]
    for s in range(1, SUBLANES):
        ext_ref[s, 0:span, :] = ext_ref[0, pl.ds(s, span), :]
    acc = jnp.zeros((ts, CONV_CH), F32) + b_ref[...]
    off = CONV_HALO - (CONV_W - 1)
    for k in range(CONV_W):
        s = (off + k) % SUBLANES
        acc = acc + ext_ref[s, pl.ds(off + k - s, ts), :] * w_ref[k:k + 1, :]
    y = _ln_rows(acc, g_ref[...], bn_ref[...])
    o_ref[...] = (y * jax.nn.sigmoid(y)).astype(BF16)


def _prompt_conv(u, cw, cb, cg, cbn, *, batch, seq, ts):
    nt = seq // ts
    r = ts // CONV_HALO
    cwp = jnp.pad(cw, ((0, CONV_HALO - CONV_W), (0, 0)))
    vec = lambda a: a.reshape(1, CONV_CH)
    cst = lambda b, c: (0, 0)
    return pl.pallas_call(
        functools.partial(_conv_body, ts=ts), grid=(batch, nt),
        in_specs=[pl.BlockSpec((ts, CONV_CH), lambda b, c: (b * nt + c, 0)),
                  pl.BlockSpec((CONV_HALO, CONV_CH), lambda b, c: (jnp.maximum((b * nt + c) * r - 1, 0), 0)),
                  pl.BlockSpec((CONV_HALO, CONV_CH), cst),
                  pl.BlockSpec((1, CONV_CH), cst), pl.BlockSpec((1, CONV_CH), cst), pl.BlockSpec((1, CONV_CH), cst)],
        out_specs=pl.BlockSpec((ts, CONV_CH), lambda b, c: (b * nt + c, 0)),
        out_shape=jax.ShapeDtypeStruct((batch * seq, CONV_CH), BF16),
        scratch_shapes=[pltpu.VMEM((SUBLANES, CONV_HALO + ts, CONV_CH), F32)],
        compiler_params=_cparams("parallel", "arbitrary"), name="prompt_conv",
    )(u, u, cwp, vec(cb), vec(cg), vec(cbn))


POOL_HALO = 16


def _odd_mix_body(pin_ref, prev_ref, u_ref, vn_ref, pw_ref, ps_ref, sw_ref, sb_ref, o_ref, ext_ref, *, ts):
    c = pl.program_id(1)
    ext_ref[0:POOL_HALO, :] = jnp.where(c > 0, prev_ref[...], 0.0)
    ext_ref[POOL_HALO:POOL_HALO + ts, :] = pin_ref[...]
    t = c * ts + lax.broadcasted_iota(jnp.int32, (ts, 1), 0)
    for g, w in enumerate(POOL_WINDOWS):
        sl = slice(g * POOL_GC, (g + 1) * POOL_GC)
        tot = ext_ref[pl.ds(POOL_HALO, ts), sl]
        for j in range(1, w):
            tot = tot + ext_ref[pl.ds(POOL_HALO - j, ts), sl]
        cnt = jnp.minimum(w, t + 1).astype(F32)
        d = tot / cnt - pin_ref[:, sl]
        y = jnp.dot(d.astype(BF16), pw_ref[g], preferred_element_type=F32)
        o_ref[:, sl] = (y * ps_ref[:, sl]).astype(BF16)
    ri = lax.broadcasted_iota(jnp.int32, (CHUNK, CHUNK), 0)
    ci = lax.broadcasted_iota(jnp.int32, (CHUNK, CHUNK), 1)
    for g in range(SGU_GROUPS):
        sl = slice(g * SGU_GC, (g + 1) * SGU_GC)
        ws = jnp.where(ci <= ri, sw_ref[g], 0.0).astype(BF16)
        for q in range(ts // CHUNK):
            rows = slice(q * CHUNK, (q + 1) * CHUNK)
            mixed = jnp.dot(ws, vn_ref[rows, sl].astype(BF16), preferred_element_type=F32) + sb_ref[:, g:g + 1]
            o_ref[rows, POOL_CH + g * SGU_GC:POOL_CH + (g + 1) * SGU_GC] = (u_ref[rows, sl] * mixed).astype(BF16)


def _prompt_odd_mix(pin, u, vn, pool_w, pool_scale, sgu_w, sgu_b, *, batch, seq, ts):
    assert seq % ts == 0 and ts % CHUNK == 0
    nt = seq // ts
    r = ts // POOL_HALO
    cst2 = lambda b, c: (0, 0)
    cst3 = lambda b, c: (0, 0, 0)
    row = lambda b, c: (b * nt + c, 0)
    return pl.pallas_call(
        functools.partial(_odd_mix_body, ts=ts), grid=(batch, nt),
        in_specs=[pl.BlockSpec((ts, POOL_CH), row),
                  pl.BlockSpec((POOL_HALO, POOL_CH), lambda b, c: (jnp.maximum((b * nt + c) * r - 1, 0), 0)),
                  pl.BlockSpec((ts, SGU_CH), row), pl.BlockSpec((ts, SGU_CH), row),
                  pl.BlockSpec((POOL_GROUPS, POOL_GC, POOL_GC), cst3), pl.BlockSpec((1, POOL_CH), cst2),
                  pl.BlockSpec((SGU_GROUPS, CHUNK, CHUNK), cst3), pl.BlockSpec((CHUNK, SGU_GROUPS), cst2)],
        out_specs=pl.BlockSpec((ts, D_MODEL), row),
        out_shape=jax.ShapeDtypeStruct((batch * seq, D_MODEL), BF16),
        scratch_shapes=[pltpu.VMEM((POOL_HALO + ts, POOL_CH), F32)],
        compiler_params=_cparams("parallel", "arbitrary"), name="prompt_pool_sgu",
    )(pin, pin, u, vn, pool_w.astype(BF16), pool_scale.reshape(1, POOL_CH), sgu_w, sgu_b.T)


def _group_rows(nrows):
    return lax.broadcasted_iota(jnp.int32, (nrows, 1), 0) >> (HPG.bit_length() - 1)


def _sample_cmp_body(q_ref, ckv_ref, o_ref, idx_ref, *, qpos, ncb, nbl, k_past):
    q = q_ref[...].astype(BF16)
    rg = _group_rows(N_HEADS)
    half = nbl // 2
    assert nbl & (nbl - 1) == 0
    sh = nbl.bit_length() - 1
    lane = lax.broadcasted_iota(jnp.int32, (1, ncb), 1)
    grp, w = lane >> sh, lane & (nbl - 1)
    n_cmp = grp * nbl + 2 * (w & (half - 1)) + (w >> (sh - 1))
    mk = jnp.broadcast_to(((n_cmp + 1) * CMP_BLOCK - 1) <= qpos, (N_HEADS, ncb))
    s = jnp.zeros((N_HEADS, ncb), F32)
    for g in range(KV_HEADS):
        ck = ckv_ref[g * HSLOT:(g + 1) * HSLOT, :].astype(BF16)
        s = jnp.where(rg == g, jnp.dot(q, ck, preferred_element_type=F32) * SCALE, s)
    s = jnp.where(mk, s, NEG)
    mx = jnp.max(s, axis=-1, keepdims=True)
    p = jnp.where(mk, jnp.exp(s - mx), 0.0)
    pn = p / jnp.maximum(jnp.sum(p, axis=-1, keepdims=True), 1e-30)
    o = jnp.zeros((N_HEADS, HSLOT), F32)
    for g in range(KV_HEADS):
        cv = ckv_ref[(KV_HEADS + g) * HSLOT:(KV_HEADS + g + 1) * HSLOT, :].astype(BF16)
        o = jnp.where(rg == g, lax.dot_general(pn.astype(BF16), cv, _NT, preferred_element_type=F32), o)
    o_ref[...] = o
    pair = pn + pltpu.roll(pn, ncb - half, axis=1)
    valid = w < half
    sb = grp * half + w
    vis = (sb * SEL_BLOCK) <= qpos
    ri = lax.broadcasted_iota(jnp.int32, (ncb, ncb), 0)
    sb_r = (ri >> sh) * half + (ri & (nbl - 1))
    sb_c = jnp.broadcast_to(sb, (ncb, ncb))
    slot = lax.broadcasted_iota(jnp.int32, (TOP_K, 1), 0)
    for g in range(KV_HEADS):
        imp = jnp.sum(jnp.where(rg == g, pair, 0.0), axis=0, keepdims=True)
        imp = jnp.where(valid, jnp.where(vis, imp, -FORCE), -2.0 * FORCE)
        a = jnp.broadcast_to(imp, (ncb, ncb))
        bt = a.T
        beats = (bt > a) | ((bt == a) & (sb_r < sb_c))
        rank = jnp.sum(jnp.where(beats, 1.0, 0.0), axis=0, keepdims=True)
        onehot = jnp.where((rank == slot.astype(F32)) & valid, 1.0, 0.0)
        idx = jnp.sum(onehot * sb.astype(F32), axis=-1, keepdims=True)
        idx = jnp.where(slot < k_past, idx, 0.0)
        idx_ref[g * TOP_K:(g + 1) * TOP_K, :] = jnp.broadcast_to(idx, (TOP_K, LANES)).astype(jnp.int32)


def _sample_cmp(q3, ckv_t, *, bd, past, qpos, nbl):
    ncb = past // CMP_BLOCK
    k_past = min(TOP_K - 1, past // SEL_BLOCK)
    body = functools.partial(_sample_cmp_body, qpos=qpos, ncb=ncb, nbl=nbl, k_past=k_past)
    head3 = pl.BlockSpec((None, N_HEADS, HSLOT), lambda b: (b, 0, 0))
    return pl.pallas_call(
        body, grid=(bd,),
        in_specs=[head3, pl.BlockSpec((None, KVP, ncb), lambda b: (b, 0, 0))],
        out_specs=[head3, pl.BlockSpec((None, KV_HEADS * TOP_K, LANES), lambda b: (b, 0, 0))],
        out_shape=[jax.ShapeDtypeStruct((bd, N_HEADS, HSLOT), F32),
                   jax.ShapeDtypeStruct((bd, KV_HEADS * TOP_K, LANES), jnp.int32)],
        compiler_params=_cparams("parallel"), name="sample_cmp_attention",
    )(q3, ckv_t)


Q_PAD_ROWS = 8


def _pad_dt(x):
    return _pad_rows_to(x, HSLOT)


def _sample_sel_body(pt_ref, idx_ref, *refs, k_past):
    kv_refs = refs[:k_past]
    k_refs = [r.at[0] for r in kv_refs]
    v_refs = [r.at[1] for r in kv_refs]
    q_ref, knew_ref, vnew_ref, o_ref = refs[k_past:]
    b, g = pl.program_id(0), pl.program_id(1)
    bpp = PAGE_SIZE // SEL_BLOCK
    q = q_ref[...].astype(BF16)
    half_of_lane = lax.broadcasted_iota(jnp.int32, (1, PAGE_SIZE), 1) // SEL_BLOCK
    s_parts, m_parts = [], []
    for s in range(k_past):
        kt = _pad_dt(k_refs[s][...]).astype(BF16)
        s_parts.append(jnp.dot(q, kt, preferred_element_type=F32) * SCALE)
        m_parts.append(jnp.broadcast_to(half_of_lane == (idx_ref[b, g, s] & (bpp - 1)), (Q_PAD_ROWS, PAGE_SIZE)))
    s_old = jnp.concatenate(s_parts, axis=1)
    mk = jnp.concatenate(m_parts, axis=1)
    s_old = jnp.where(mk, s_old, NEG)
    s_all = lax.dot_general(q, knew_ref[...].astype(BF16), _NT, preferred_element_type=F32) * SCALE
    lane = lax.broadcasted_iota(jnp.int32, s_all.shape, 1)
    s_new = jnp.sum(jnp.where(lane == b, s_all, 0.0), axis=-1, keepdims=True)
    mx = jnp.maximum(jnp.max(s_old, axis=-1, keepdims=True), s_new)
    p_old = jnp.where(mk, jnp.exp(s_old - mx), 0.0)
    p_new = jnp.exp(s_new - mx)
    den = jnp.maximum(jnp.sum(p_old, axis=-1, keepdims=True) + p_new, 1e-30)
    v_new = vnew_ref[pl.ds(b, 1), :].astype(BF16).astype(F32)
    o = p_new.astype(BF16).astype(F32) * v_new
    for s in range(k_past):
        vt = _pad_dt(v_refs[s][...]).astype(BF16)
        ps = p_old[:, s * PAGE_SIZE:(s + 1) * PAGE_SIZE].astype(BF16)
        o = o + lax.dot_general(ps, vt, _NT, preferred_element_type=F32)
    o_ref[...] = o / den


def _sample_sel(pages5, page_table, idx, layer_base, q4, kvs_new, *, bd, past):
    k_past = min(TOP_K - 1, past // SEL_BLOCK)
    bpp_shift = (PAGE_SIZE // SEL_BLOCK).bit_length() - 1

    def blk_map(s):
        def f(b, g, pt, ix):
            return (layer_base + pt[b, lax.shift_right_logical(ix[b, g, s], bpp_shift)], 0, g, 0, 0)
        return f

    tile = lambda s: pl.BlockSpec((None, 2, None, HEAD_DIM, PAGE_SIZE), blk_map(s))
    grid_spec = pltpu.PrefetchScalarGridSpec(
        num_scalar_prefetch=2, grid=(bd, KV_HEADS),
        in_specs=[tile(s) for s in range(k_past)]
        + [pl.BlockSpec((None, None, Q_PAD_ROWS, HSLOT), lambda b, g, pt, ix: (b, g, 0, 0)),
           pl.BlockSpec((SAMPLE_ROWS, HSLOT), lambda b, g, pt, ix: (0, g)),
           pl.BlockSpec((SAMPLE_ROWS, HSLOT), lambda b, g, pt, ix: (0, KV_HEADS + g))],
        out_specs=pl.BlockSpec((None, None, Q_PAD_ROWS, HSLOT), lambda b, g, pt, ix: (b, g, 0, 0)))
    return pl.pallas_call(
        functools.partial(_sample_sel_body, k_past=k_past), grid_spec=grid_spec,
        out_shape=jax.ShapeDtypeStruct((bd, KV_HEADS, Q_PAD_ROWS, HSLOT), F32),
        compiler_params=_cparams("parallel", "arbitrary"), name="sample_sel_attention",
    )(page_table, idx, *([pages5] * k_past), q4, kvs_new, kvs_new)


def _sample_win_body(q_ref, win_ref, new_ref, ocmp_ref, osel_ref, gate_ref, o_ref, *, qpos, past, wb):
    b = pl.program_id(0)
    q = q_ref[...].astype(BF16)
    qf = q.astype(F32)
    rg = _group_rows(N_HEADS)
    new = new_ref[pl.ds(b, 1), :].astype(BF16).astype(F32)
    kpos = (past - wb) + lax.broadcasted_iota(jnp.int32, (1, wb), 1)
    mk = jnp.broadcast_to((kpos <= qpos) & (kpos >= qpos - WINDOW), (N_HEADS, wb))
    s_old = jnp.zeros((N_HEADS, wb), F32)
    s_new = jnp.zeros((N_HEADS, 1), F32)
    for g in range(KV_HEADS):
        kt = _pad_dt(win_ref[g]).astype(BF16)
        s_old = jnp.where(rg == g, jnp.dot(q, kt, preferred_element_type=F32) * SCALE, s_old)
        sn = jnp.sum(qf * new[:, g * HSLOT:(g + 1) * HSLOT], axis=-1, keepdims=True) * SCALE
        s_new = jnp.where(rg == g, sn, s_new)
    s_old = jnp.where(mk, s_old, NEG)
    mx = jnp.maximum(jnp.max(s_old, axis=-1, keepdims=True), s_new)
    p_old = jnp.where(mk, jnp.exp(s_old - mx), 0.0)
    p_new = jnp.exp(s_new - mx)
    den = jnp.maximum(jnp.sum(p_old, axis=-1, keepdims=True) + p_new, 1e-30)
    o_win = jnp.zeros((N_HEADS, HSLOT), F32)
    for g in range(KV_HEADS):
        vt = _pad_dt(win_ref[KV_HEADS + g]).astype(BF16)
        og = lax.dot_general(p_old.astype(BF16), vt, _NT, preferred_element_type=F32)
        og = og + p_new.astype(BF16).astype(F32) * new[:, (KV_HEADS + g) * HSLOT:(KV_HEADS + g + 1) * HSLOT]
        o_win = jnp.where(rg == g, og, o_win)
    o_win = o_win / den
    gts = gate_ref[...]
    o_ref[...] = gts[:, 0:1] * ocmp_ref[...] + gts[:, 1:2] * osel_ref[...] + gts[:, 2:3] * o_win


def _sample_win(q3, win4, layer, kvw_new, o_cmp, o_sel, gates3, *, bd, past, qpos):
    wb = win4.shape[-1]
    head3 = pl.BlockSpec((None, N_HEADS, HSLOT), lambda b: (b, 0, 0))
    return pl.pallas_call(
        functools.partial(_sample_win_body, qpos=qpos, past=past, wb=wb), grid=(bd,),
        in_specs=[head3, pl.BlockSpec((None, 2 * KV_HEADS, HEAD_DIM, wb), lambda b: (layer * bd + b, 0, 0, 0)),
                  pl.BlockSpec((SAMPLE_ROWS, KVP), lambda b: (0, 0)), head3, head3, head3],
        out_specs=head3,
        out_shape=jax.ShapeDtypeStruct((bd, N_HEADS, HSLOT), F32),
        compiler_params=_cparams("parallel"), name="sample_win_attention",
    )(q3, win4, kvw_new, o_cmp, o_sel, gates3)


def _window_shift_body(win_ref, new_ref, *refs, wb):
    o_ref = refs[-1]
    b = pl.program_id(0)
    new = new_ref[...]
    pick = lax.broadcasted_iota(jnp.int32, new.shape, 1) == b
    col = jnp.sum(jnp.where(pick, new, 0.0), axis=1, keepdims=True)
    lane = lax.broadcasted_iota(jnp.int32, (1, wb), 1)
    o_ref[...] = jnp.where(lane == wb - 1, col, pltpu.roll(win_ref[...], wb - 1, axis=1))


def _window_shift(win3, new_t, layer, n_layers, prev, *, bd):
    wb = win3.shape[-1]
    in_specs = [pl.BlockSpec((None, KVW, wb), lambda b: (layer * bd + b, 0, 0)),
                pl.BlockSpec((KVW, bd), lambda b: (0, 0))]
    operands = [win3, new_t]
    if prev is not None:
        in_specs.append(pl.BlockSpec(memory_space=pl.ANY))
        operands.append(prev)
    return pl.pallas_call(
        functools.partial(_window_shift_body, wb=wb), grid=(bd,), in_specs=in_specs,
        out_specs=pl.BlockSpec((None, None, KVW, wb), lambda b: (layer, b, 0, 0)),
        out_shape=jax.ShapeDtypeStruct((n_layers, bd, KVW, wb), F32),
        input_output_aliases={} if prev is None else {2: 0},
        compiler_params=_cparams("arbitrary"), name="sample_window_shift",
    )(*operands)


def _sample_conv_body(st_ref, u_ref, w_ref, b_ref, g_ref, bn_ref, o_ref, *, bd):
    w = w_ref[...]
    y = jnp.sum(st_ref[...] * w[None, :CONV_W - 1, :], axis=1) + u_ref[0:bd, :] * w[CONV_W - 1:CONV_W, :] + b_ref[...]
    y = _ln_rows(y, g_ref[...], bn_ref[...])
    o_ref[...] = y * jax.nn.sigmoid(y)


def _sample_conv(state, layer, u, cw, cb, cg, cbn, *, bd):
    vec = lambda a: a.reshape(1, CONV_CH)
    cst = lambda i: (0, 0)
    return pl.pallas_call(
        functools.partial(_sample_conv_body, bd=bd), grid=(1,),
        in_specs=[pl.BlockSpec((None, bd, CONV_W - 1, CONV_CH), lambda i: (layer, 0, 0, 0)),
                  pl.BlockSpec((SAMPLE_ROWS, CONV_CH), cst), pl.BlockSpec((CONV_W, CONV_CH), cst),
                  pl.BlockSpec((1, CONV_CH), cst), pl.BlockSpec((1, CONV_CH), cst), pl.BlockSpec((1, CONV_CH), cst)],
        out_specs=pl.BlockSpec((bd, CONV_CH), cst),
        out_shape=jax.ShapeDtypeStruct((bd, CONV_CH), F32),
        compiler_params=_cparams("arbitrary"), name="sample_conv",
    )(state, u, cw, vec(cb), vec(cg), vec(cbn))


def _sample_odd_body(st_ref, pin_ref, u_ref, vn_ref, pw_ref, ps_ref, w0_ref, b0_ref, o_ref, *, bd, start_pos):
    pin = pin_ref[0:bd, :]
    st = st_ref[...]
    for g, w in enumerate(POOL_WINDOWS):
        sl = slice(g * POOL_GC, (g + 1) * POOL_GC)
        tot = pin[:, sl] + jnp.sum(st[:, POOL_STATE - (w - 1):, sl], axis=1)
        d = tot / float(min(w, start_pos + 1)) - pin[:, sl]
        dp = jnp.concatenate([d, jnp.zeros((SAMPLE_ROWS - bd, POOL_GC), F32)], axis=0).astype(BF16)
        y = jnp.dot(dp, pw_ref[g], preferred_element_type=F32)[0:bd]
        o_ref[:, sl] = y * ps_ref[:, sl]
    mixed = w0_ref[...] * vn_ref[0:bd, :] + b0_ref[...]
    o_ref[:, POOL_CH:] = u_ref[0:bd, :] * mixed


def _sample_odd_mix(state, layer, pin, u, vn, pool_w, pool_scale, sgu_w, sgu_b, *, bd, start_pos):
    w0 = jnp.repeat(sgu_w[:, 0, 0], SGU_GC).reshape(1, SGU_CH)
    b0 = jnp.repeat(sgu_b[:, 0], SGU_GC).reshape(1, SGU_CH)
    cst = lambda i: (0, 0)
    return pl.pallas_call(
        functools.partial(_sample_odd_body, bd=bd, start_pos=start_pos), grid=(1,),
        in_specs=[pl.BlockSpec((None, bd, POOL_STATE, POOL_CH), lambda i: (layer, 0, 0, 0)),
                  pl.BlockSpec((SAMPLE_ROWS, POOL_CH), cst), pl.BlockSpec((SAMPLE_ROWS, SGU_CH), cst),
                  pl.BlockSpec((SAMPLE_ROWS, SGU_CH), cst),
                  pl.BlockSpec((POOL_GROUPS, POOL_GC, POOL_GC), lambda i: (0, 0, 0)),
                  pl.BlockSpec((1, POOL_CH), cst), pl.BlockSpec((1, SGU_CH), cst), pl.BlockSpec((1, SGU_CH), cst)],
        out_specs=pl.BlockSpec((bd, D_MODEL), cst),
        out_shape=jax.ShapeDtypeStruct((bd, D_MODEL), F32),
        compiler_params=_cparams("arbitrary"), name="sample_pool_sgu",
    )(state, pin, u, vn, pool_w.astype(BF16), pool_scale.reshape(1, POOL_CH), w0, b0)


def _pad_rows(x, rows):
    return jnp.pad(x, ((0, rows - x.shape[0]), (0, 0)))


def _split_to_nat(x, nheads):
    y = _unpad_heads(x, nheads)
    return _pad_nat(y.reshape(y.shape[:-1] + (nheads, HEAD_DIM))).reshape(x.shape)


def _nat_to_split(x, nheads):
    xh = x.reshape(x.shape[:-1] + (nheads, HSLOT))[..., :HEAD_DIM]
    return _pad_head(xh).reshape(x.shape)


def _even_in_proj(xb, xsb, w_in, w_gg, rope_p, rope_s, *, tm, bs, stack):
    nrep = rope_p[0].shape[0] // tm
    rs = xsb.shape[0]
    tab = lambda i, j: (i % nrep, 0)
    rope_ex = tuple((t, (tm, LANES), tab) for t in rope_p)
    rope_sx = tuple((t, (rs, LANES), lambda i, j: (0, 0)) for t in rope_s)
    gw = HPG * HSLOT
    qraw, qrot, qraw_s, qrot_s = _mm(
        xb, w_in, tm=tm, tn=2 * gw, n_off=E_Q, n_cols=QW, epilogue=_ep_q, extras=rope_ex,
        outs=((QW, BF16, 2 * gw), (QW, BF16, 2 * gw)), name="even_in_q",
        side=(xsb, _ep_q, rope_sx, ((QW, BF16, 2 * gw), (QW, BF16, 2 * gw))))
    nst = bs[1] // tm
    e, n_even, t_prev = stack
    row = {"f32": (KVP, F32, KVP), "bf16": (KVP, BF16, KVP),
           "t": ((n_even, bs[0], KVW, bs[1]), F32, (None, None, KVW, tm), lambda i, j: (e, i // nst, 0, i % nst))}
    kv_p, kv_s = [], []
    for sec, (off, want) in enumerate(zip((E_KVC, E_KVS, E_KVW), (("f32", "t"), ("bf16", "t"), ("bf16", "t")))):
        res = _mm(xb, w_in, tm=tm, tn=KVP, n_off=off, n_cols=KVP,
                  epilogue=functools.partial(_ep_kv, rope=sec > 0, want=want),
                  extras=rope_ex if sec > 0 else (), outs=tuple(row[k] for k in want),
                  name=("even_in_kvc", "even_in_kvs", "even_in_kvw")[sec],
                  side=(xsb, functools.partial(_ep_kv, rope=sec > 0, want=("f32",)),
                        rope_sx if sec > 0 else (), (row["f32"],)),
                  alias=None if t_prev is None else (t_prev[sec], 1))
        kv_p.append(res[:2])
        kv_s.append(res[2])
    gg_outs = ((CONV_CH, F32, CONV_CH), (GATE_W, F32, GATE_W))
    u, gates, u_s, gates_s = _mm(xb, w_gg, tm=tm, tn=GG_W, n_off=0, n_cols=GG_W, epilogue=_ep_glu_gates,
                                 outs=gg_outs, name="even_in_glu_gates", side=(xsb, _ep_glu_gates, (), gg_outs))
    return (qraw, qrot, kv_p, u, gates), (qraw_s, qrot_s, kv_s, u_s, gates_s)


def _mlp_up_cast(xb, xsb, w1, w2, layer, *, tm, tn):
    m, k = xb.shape
    rs = xsb.shape[0]
    dff = w1.shape[2]
    d_out = w2.shape[2]
    ni, nj = m // tm, dff // tn
    slab = dff // (ni * nj)
    assert m % tm == 0 and dff % tn == 0 and dff % (ni * nj) == 0 and slab % SAMPLE_ROWS == 0
    rsub = min(tm, ROW_SUB)

    def body(x_ref, xs_ref, w1_ref, w2_ref, h_ref, hs_ref, w2b_ref, w1b_ref):
        def act(rows_ref, rows):
            a = jnp.maximum(jnp.dot(rows_ref[rows, :], w1b_ref[...], preferred_element_type=F32), 0.0)
            return (a * a).astype(BF16)

        @pl.when(pl.program_id(1) == 0)
        def _():
            w1b_ref[...] = w1_ref[...].astype(BF16)
            hs_ref[...] = act(xs_ref, slice(0, rs))

        w2b_ref[...] = w2_ref[...].astype(BF16)
        for r in range(tm // rsub):
            rows = slice(r * rsub, (r + 1) * rsub)
            h_ref[rows, :] = act(x_ref, rows)

    return pl.pallas_call(
        body, grid=(nj, ni),
        in_specs=[pl.BlockSpec((tm, k), lambda j, i: (i, 0)), pl.BlockSpec((rs, k), lambda j, i: (0, 0)),
                  pl.BlockSpec((None, k, tn), lambda j, i: (layer, 0, j)),
                  pl.BlockSpec((None, slab, d_out), lambda j, i: (layer, j * ni + i, 0))],
        out_specs=[pl.BlockSpec((tm, tn), lambda j, i: (i, j)), pl.BlockSpec((rs, tn), lambda j, i: (0, j)),
                   pl.BlockSpec((slab, d_out), lambda j, i: (j * ni + i, 0))],
        out_shape=[jax.ShapeDtypeStruct((m, dff), BF16), jax.ShapeDtypeStruct((rs, dff), BF16),
                   jax.ShapeDtypeStruct((dff, d_out), BF16)],
        scratch_shapes=[pltpu.VMEM((k, tn), BF16)],
        compiler_params=_cparams("arbitrary", "arbitrary"), name="mlp_up_cast",
    )(xb, xsb, w1, w2)


def kernel(x_prompt, x_sample, cache_cmp_kv, cache_sel_kv, cache_win_kv, state_conv, state_pool, page_table,
           w_in_even, w_out_even, cmp_pe_k, cmp_pe_v, cmp_w_k, cmp_w_v, conv_w, conv_b, conv_ln_g, conv_ln_b,
           w_in_odd, w_out_odd, pool_w, pool_scale, sgu_ln_g, sgu_ln_b, sgu_w, sgu_b,
           mlp_w1, mlp_w2, ln_mix_g, ln_mix_b, ln_ffn_g, ln_ffn_b):
    B, S, D = x_prompt.shape
    Bd, Sd, _ = x_sample.shape
    n_pages = page_table.shape[1]
    past = n_pages * PAGE_SIZE
    n_even, n_pool = cache_cmp_kv.shape[:2]
    wb = cache_win_kv.shape[2]
    assert D == D_MODEL and Sd == 1 and Bd <= SAMPLE_ROWS
    assert S % 1024 == 0 and past % SEL_BLOCK == 0 and S >= WINDOW
    M = B * S
    Ms = SAMPLE_ROWS
    tm_p = TM_PROJ

    rope_p = _rope_tables(jnp.arange(S, dtype=jnp.int32))
    rope_s = _rope_tables(jnp.full((Ms,), past, jnp.int32))
    pps = min(PAGES_PER_STEP, n_pages)
    assert n_pages % pps == 0
    bsum = _block_sum_matrices(pps)
    cmp_t = cache_cmp_kv.transpose(0, 1, 3, 4, 5, 2).reshape(n_even * n_pool, KVW, PAGE_SIZE)
    sel_t = cache_sel_kv.transpose(0, 1, 3, 4, 5, 2).reshape(n_even * n_pool, 2, KV_HEADS, HEAD_DIM, PAGE_SIZE)
    win_t = cache_win_kv.transpose(0, 1, 3, 4, 5, 2).reshape(n_even * Bd, 2 * KV_HEADS, HEAD_DIM, wb)

    xp = x_prompt.reshape(M, D)
    xs = _pad_rows(x_sample.reshape(Bd, D), Ms)
    xpb, xsb = xp.astype(BF16), xs.astype(BF16)

    outs = {k: [] for k in ("cmp_s", "sel_s", "win_s", "conv_p", "conv_s", "pool_p", "pool_s", "sgu_p", "sgu_s")}
    kv6 = lambda a, lead: a.reshape(lead + (2, KV_HEADS, HEAD_DIM))
    kv_t = None
    win_s_t = None

    for layer in range(DEPTH):
        if layer % 2 == 0:
            e = layer // 2
            wts = _prep_even_weights(w_in_even[e], w_out_even[e], cmp_pe_k[e], cmp_pe_v[e], cmp_w_k[e], cmp_w_v[e])
            prj_p, prj_s = _even_in_proj(xpb, xsb, wts["w_qkv"], wts["w_gg"], rope_p, rope_s, tm=tm_p, bs=(B, S),
                                         stack=(e, n_even, kv_t))
            qraw, qrot, ((kvc, kvc_t), (kvs_b, kvs_t), (kvw_b, kvw_t)), u, gates = prj_p
            kv_t = (kvc_t, kvs_t, kvw_t)
            summ = _compress_rows(kvc, wts["pe"], rows=CMP_ROWS, name="prompt_compress")
            (ckv,) = _mm(summ.astype(BF16), wts["big_p"], tm=min(summ.shape[0], CMP_ROWS), tn=KVP,
                         n_off=0, n_cols=KVP, epilogue=_ep_plain, outs=((KVP, F32, KVP),), name="prompt_compress_map")
            o_att = _prompt_attention(qraw, qrot, gates, ckv, kvs_b, kvw_b, batch=B, seq=S, tq=ATT_TQ, tk=ATT_TK,
                                      ngrp=KV_HEADS)
            c = _prompt_conv(u, conv_w[e], conv_b[e], conv_ln_g[e], conv_ln_b[e], batch=B, seq=S, ts=CONV_TS)
            outs["conv_p"].append(u.reshape(B, S, CONV_CH)[:, S - (CONV_W - 1):])
            qraw_s, qrot_s, (kvc_s, kvs_s, kvw_s), u_s, gates_s = prj_s
            ckv_t = _compress_pages(cmp_t, page_table, wts["pe_t"], wts["big_t"], bsum, e * n_pool, pps=pps,
                                    name="sample_compress")
            q3 = qraw_s.astype(F32)[:Bd].reshape(Bd, N_HEADS, HSLOT)
            o_cmp, idx = _sample_cmp(q3, ckv_t, bd=Bd, past=past, qpos=past, nbl=pps * (PAGE_SIZE // CMP_BLOCK))
            idx = idx[:, :, 0].reshape(Bd, KV_HEADS, TOP_K)
            qr3 = _split_to_nat(qrot_s.astype(F32)[:Bd], N_HEADS).reshape(Bd, N_HEADS, HSLOT)
            q4 = jnp.pad(qr3.reshape(Bd, KV_HEADS, HPG, HSLOT), ((0, 0), (0, 0), (0, Q_PAD_ROWS - HPG), (0, 0)))
            kvs_nat = _split_to_nat(kvs_s, 2 * KV_HEADS)
            kvw_nat = _split_to_nat(kvw_s, 2 * KV_HEADS)
            o_sel = _sample_sel(sel_t, page_table, idx, e * n_pool, q4, kvs_nat, bd=Bd, past=past)
            o_sel = o_sel[:, :, :HPG].reshape(Bd, N_HEADS, HSLOT)
            g3 = gates_s[:Bd, :3 * N_HEADS].reshape(Bd, KV_HEADS, 3, HPG)
            g3 = g3.transpose(0, 1, 3, 2).reshape(Bd, N_HEADS, 3)
            g3 = jnp.pad(g3, ((0, 0), (0, 0), (0, LANES - 3)))
            o_s = _sample_win(qr3, win_t, e, kvw_nat, o_cmp, o_sel, g3, bd=Bd, past=past, qpos=past)
            c_s = _sample_conv(state_conv, e, u_s, conv_w[e], conv_b[e], conv_ln_g[e], conv_ln_b[e], bd=Bd)
            o_sb = _pad_rows(_nat_to_split(o_s.reshape(Bd, QW), N_HEADS), Ms).astype(BF16)
            c_sb = _pad_rows(c_s, Ms).astype(BF16)
            xp, xpb, xs, xsb = _proj_ln([o_att, c], [wts["wo_att"], wts["wo_conv"]], xp, ln_mix_g[layer],
                                        ln_mix_b[layer], tm=TM_RES_LN, name="even_out_ln", side=([o_sb, c_sb], xs))
            kvc_c = _unpad_heads(kvc_s[:Bd], 2 * KV_HEADS)
            kvs_c = _unpad_heads(kvs_s[:Bd], 2 * KV_HEADS)
            kvw_c = _unpad_heads(kvw_s[:Bd], 2 * KV_HEADS)
            outs["cmp_s"].append(kv6(kvc_c, (Bd, 1)))
            outs["sel_s"].append(kv6(kvs_c, (Bd, 1)))
            if wb == WINDOW:
                win_s_t = _window_shift(win_t.reshape(n_even * Bd, KVW, wb), kvw_c.T, e, n_even, win_s_t, bd=Bd)
            else:
                wkv = jnp.concatenate([cache_win_kv[e], kv6(kvw_c, (Bd, 1))], axis=1)
                outs["win_s"].append(wkv[:, wkv.shape[1] - min(WINDOW, wkv.shape[1]):])
            outs["conv_s"].append(jnp.concatenate([state_conv[e], u_s[:Bd, None, :]], axis=1)[:, 1:])
        else:
            o = layer // 2
            w_in = w_in_odd[o]
            wb_in = w_in.astype(BF16)
            w_pu, w_v = wb_in[:, :POOL_CH + SGU_CH], wb_in[:, POOL_CH + SGU_CH:]
            w_out_p = w_out_odd[o].astype(BF16)
            lg, lb = sgu_ln_g[o].reshape(1, SGU_CH), sgu_ln_b[o].reshape(1, SGU_CH)

            gl_ex = ((lg, (1, SGU_CH), lambda i, j: (0, j)), (lb, (1, SGU_CH), lambda i, j: (0, j)))
            o_v, o_pu = ((SGU_CH, F32, SGU_CH),), ((POOL_CH, F32, POOL_CH), (SGU_CH, F32, SGU_CH))
            vn, vn_s = _mm(xpb, w_v, tm=tm_p, tn=SGU_CH, n_off=0, n_cols=SGU_CH, epilogue=_ep_gelu_gln,
                           extras=gl_ex, outs=o_v, name="odd_in_v", side=(xsb, _ep_gelu_gln, gl_ex, o_v))
            pin, uu, pin_s, uu_s = _mm(xpb, w_pu, tm=tm_p, tn=POOL_CH + SGU_CH, n_off=0, n_cols=POOL_CH + SGU_CH,
                                       epilogue=_ep_pool_gelu, outs=o_pu, name="odd_in_pool_u",
                                       side=(xsb, _ep_pool_gelu, (), o_pu))
            cat = _prompt_odd_mix(pin, uu, vn, pool_w[o], pool_scale[o], sgu_w[o], sgu_b[o], batch=B, seq=S,
                                  ts=POOL_TS)
            outs["pool_p"].append(pin.reshape(B, S, POOL_CH)[:, S - POOL_STATE:])
            outs["sgu_p"].append(vn.reshape(B, S, SGU_CH)[:, ((S - 1) // CHUNK) * CHUNK:])
            cat_s = _sample_odd_mix(state_pool, o, pin_s, uu_s, vn_s, pool_w[o], pool_scale[o], sgu_w[o], sgu_b[o],
                                    bd=Bd, start_pos=past)
            xp, xpb, xs, xsb = _proj_ln([cat], [w_out_p], xp, ln_mix_g[layer], ln_mix_b[layer], tm=TM_RES_LN,
                                        name="odd_out_ln", side=([_pad_rows(cat_s, Ms).astype(BF16)], xs))
            outs["pool_s"].append(jnp.concatenate([state_pool[o], pin_s[:Bd, None, :]], axis=1)[:, 1:])
            outs["sgu_s"].append(vn_s[:Bd, None, :])
        h, h_s, w2b = _mlp_up_cast(xpb, xsb, mlp_w1, mlp_w2, layer, tm=min(TM_MLP_UP, M), tn=TN_MLP_UP)
        xp, xpb, xs, xsb = _matmul_res_ln([h], [w2b], xp, ln_ffn_g[layer], ln_ffn_b[layer], tm=TM_RES_LN,
                                          tn=TN_MLP_DOWN,
                                          name="mlp_down_ln", side=([h_s], xs))

    st = lambda k: jnp.stack(outs[k])
    rows_last = lambda a, nb=B: a.reshape(n_even, nb, 2, KV_HEADS, HEAD_DIM, a.shape[-1]).transpose(0, 1, 5, 2, 3, 4)
    kvc_t, kvs_t, kvw_t = kv_t
    return (xp.reshape(B, S, D), xs[:Bd].reshape(Bd, Sd, D),
            rows_last(kvc_t), st("cmp_s"), rows_last(kvs_t), st("sel_s"),
            rows_last(kvw_t[:, :, :, S - WINDOW:]),
            st("win_s") if win_s_t is None else rows_last(win_s_t, Bd),
            st("conv_p"), st("conv_s"), st("pool_p"), st("pool_s"), st("sgu_p"), st("sgu_s"))
```
